```python
import jax
import jax.numpy as jnp
from jax import lax
import numpy as np

D_MODEL = 1024
BATCH = 8
SEQ = 4096
DEPTH = 1

N_META = 16
HG_HEADS = 8
HG_DK = 128
HG_DV = D_MODEL // HG_HEADS
HG_KWIDTH = HG_HEADS * HG_DK
HG_VWIDTH = HG_HEADS * HG_DV
CHUNK = 64
SUB = 16
POOL_WINDOWS = (2, 4, 8, 16)
POOL_GROUPS = len(POOL_WINDOWS)
POOL_WIDTH = D_MODEL
POOL_GDIM = POOL_WIDTH // POOL_GROUPS
EPS = 1e-6
IN_SIZES = (HG_KWIDTH, HG_KWIDTH, HG_VWIDTH, HG_VWIDTH, POOL_WIDTH, POOL_WIDTH, D_MODEL, D_MODEL)
IN_COLS = sum(IN_SIZES)

kernel_name = 'hybrid_hgrn2_pool_block'


def rms_norm(x, w):
    xf = x.astype(jnp.float32)
    y = xf * lax.rsqrt(jnp.mean(xf * xf, axis=-1, keepdims=True) + EPS)
    return (y * w.astype(jnp.float32)).astype(x.dtype)


def to_chunks(t, dh):
    B = t.shape[0]
    t = jnp.pad(t, ((0, 0), (CHUNK - N_META, 0), (0, 0)))
    n = t.shape[1] // CHUNK
    return t.reshape(B, n, CHUNK, HG_HEADS, dh).transpose(1, 0, 3, 2, 4)


def hgrn2_chunkwise(q, k, v, log_f):
    _, B, H, C, DK = q.shape
    DV = v.shape[-1]
    ns = C // SUB
    causal = jnp.tril(jnp.ones((SUB, SUB), dtype=bool))
    earlier = jnp.tril(jnp.ones((ns, ns), dtype=bool), -1)
    diag = jnp.eye(ns, dtype=jnp.float32)

    def step(S, inp):
        qc, kc, vc, gc = inp
        b = jnp.cumsum(gc, axis=2)
        o_inter = jnp.einsum('bhtk,bhkv->bhtv', qc * jnp.exp(b), S)
        qs = qc.reshape(B, H, ns, SUB, DK)
        ks = kc.reshape(B, H, ns, SUB, DK)
        bs = b.reshape(B, H, ns, SUB, DK)
        ref = bs[:, :, :, 0]
        q_ref = qs * jnp.exp(bs - ref[:, :, :, None])
        expo = jnp.where(earlier[None, None, :, :, None, None],
                         ref[:, :, :, None, None, :] - bs[:, :, None, :, :, :], -jnp.inf)
        k_ref = ks[:, :, None] * jnp.exp(expo)
        a_off = jnp.einsum('bhitk,bhijsk->bhitjs', q_ref, k_ref)
        dexp = jnp.where(causal[:, :, None],
                         bs[:, :, :, :, None, :] - bs[:, :, :, None, :, :], -jnp.inf)
        a_diag = jnp.einsum('bhitk,bhitsk,bhisk->bhits', qs, jnp.exp(dexp), ks)
        a = (a_off + a_diag[:, :, :, :, None, :] * diag[:, None, :, None]).reshape(B, H, C, C)
        o = o_inter + jnp.einsum('bhts,bhsv->bhtv', a, vc)
        b_last = b[:, :, -1]
        S_new = jnp.exp(b_last)[..., None] * S + jnp.einsum(
            'bhsk,bhsv->bhkv', kc * jnp.exp(b_last[:, :, None] - b), vc)
        return S_new, o

    S0 = jnp.zeros((B, H, DK, DV), jnp.float32)
    _, o = lax.scan(step, S0, (q, k, v, log_f))
    return o


def causal_multiscale_pool(u):
    B, L, _ = u.shape
    ug = u.astype(jnp.float32).reshape(B, L, POOL_GROUPS, POOL_GDIM)
    cs = jnp.cumsum(ug, axis=1)
    pos = jnp.arange(L)
    outs = []
    for g, w in enumerate(POOL_WINDOWS):
        c = cs[:, :, g]
        lagged = jnp.pad(c, ((0, 0), (w, 0), (0, 0)))[:, :L]
        cnt = jnp.minimum(pos + 1, w).astype(jnp.float32)[None, :, None]
        outs.append((c - lagged) / cnt - ug[:, :, g])
    return jnp.stack(outs, axis=2)


def _fwd_setup_inputs(seed: int = 0) -> dict:
    key = jax.random.key(seed)
    ks = jax.random.split(key, 13)
    nrm = jax.random.normal
    d = D_MODEL
    f32 = jnp.float32
    return {
        'x': nrm(ks[0], (BATCH, SEQ, d), f32),
        'meta_tokens': nrm(ks[1], (N_META, d), f32),
        'norm_w': 1.0 + 0.02 * nrm(ks[2], (DEPTH, d), f32),
        'w_in': nrm(ks[3], (DEPTH, d, IN_COLS), f32) * d ** -0.5,
        'b_in': 0.01 * nrm(ks[4], (DEPTH, IN_COLS), f32),
        'lb_logits': 0.1 * nrm(ks[5], (DEPTH + 1, HG_KWIDTH), f32),
        'hg_norm_w': 1.0 + 0.02 * nrm(ks[6], (DEPTH, HG_VWIDTH), f32),
        'pool_w': nrm(ks[7], (DEPTH, POOL_GROUPS, POOL_GDIM, POOL_GDIM), f32) * POOL_GDIM ** -0.5,
        'pool_scale': 1.0 + 0.02 * nrm(ks[8], (DEPTH, POOL_WIDTH), f32),
        'w_down_hg': nrm(ks[9], (DEPTH, HG_VWIDTH, d), f32) * HG_VWIDTH ** -0.5,
        'w_down_pool': nrm(ks[10], (DEPTH, POOL_WIDTH, d), f32) * POOL_WIDTH ** -0.5,
        'w_out': nrm(ks[11], (DEPTH, d, d), f32) * d ** -0.5,
        'final_norm_w': 1.0 + 0.02 * nrm(ks[12], (d,), f32),
    }


def _fwd_reference(x, meta_tokens, norm_w, w_in, b_in, lb_logits, hg_norm_w, pool_w, pool_scale,
              w_down_hg, w_down_pool, w_out, final_norm_w):
    f32 = jnp.float32
    B = x.shape[0]
    meta = jnp.broadcast_to(meta_tokens.astype(x.dtype)[None], (B, N_META, D_MODEL))
    z = jnp.concatenate([meta, x], axis=1)
    L = z.shape[1]
    lower_bounds = jnp.cumsum(jax.nn.softmax(lb_logits.astype(f32), axis=0), axis=0)
    split_at = np.cumsum(IN_SIZES)[:-1].tolist()
    for l in range(DEPTH):
        h = rms_norm(z, norm_w[l])
        p = (h @ w_in[l] + b_in[l]).astype(f32)
        q, fz, iv, g_hg, u, g_pool, m_hg, m_pool = jnp.split(p, split_at, axis=-1)
        lb = lower_bounds[l]
        log_f = jnp.log(lb + (1.0 - lb) * jax.nn.sigmoid(fz))
        k = (1.0 - lb) * jax.nn.sigmoid(-fz)
        o = hgrn2_chunkwise(to_chunks(q, HG_DK), to_chunks(k, HG_DK),
                            to_chunks(iv, HG_DV), to_chunks(log_f, HG_DK))
        o = o.transpose(1, 0, 3, 2, 4).reshape(B, -1, HG_HEADS, HG_DV)[:, CHUNK - N_META:]
        o = rms_norm(o, hg_norm_w[l].reshape(HG_HEADS, HG_DV)).reshape(B, L, HG_VWIDTH)
        y_hg = (o * jax.nn.silu(g_hg)) @ w_down_hg[l].astype(f32)
        pooled = causal_multiscale_pool(u)
        mixed = jnp.einsum('blgc,gcd->blgd', pooled, pool_w[l].astype(f32)).reshape(B, L, POOL_WIDTH)
        y_pool = (mixed * pool_scale[l].astype(f32) * jax.nn.silu(g_pool)) @ w_down_pool[l].astype(f32)
        merged = jax.nn.sigmoid(m_hg) * y_hg + jax.nn.sigmoid(m_pool) * y_pool
        z = z + (merged @ w_out[l].astype(f32)).astype(z.dtype)
    return rms_norm(z, final_norm_w)[:, N_META:]


import jax as _jax
import jax.numpy as _jnp

TWIN_FORMAT = 'train_step'
FWD_PARAMS = ['x', 'meta_tokens', 'norm_w', 'w_in', 'b_in', 'lb_logits', 'hg_norm_w', 'pool_w', 'pool_scale', 'w_down_hg', 'w_down_pool', 'w_out', 'final_norm_w']
TWIN_WEIGHTS = ['meta_tokens', 'norm_w', 'w_in', 'b_in', 'lb_logits', 'hg_norm_w', 'pool_w', 'pool_scale', 'w_down_hg', 'w_down_pool', 'w_out', 'final_norm_w']
TWIN_DIFF_INPUT = 'x'
TWIN_INPUTS = ['x', 'meta_tokens', 'norm_w', 'w_in', 'b_in', 'lb_logits', 'hg_norm_w', 'pool_w', 'pool_scale', 'w_down_hg', 'w_down_pool', 'w_out', 'final_norm_w', 'loss_target', 'm_meta_tokens', 'm_norm_w', 'm_w_in', 'm_b_in', 'm_lb_logits', 'm_hg_norm_w', 'm_pool_w', 'm_pool_scale', 'm_w_down_hg', 'm_w_down_pool', 'm_w_out', 'm_final_norm_w', 'v_meta_tokens', 'v_norm_w', 'v_w_in', 'v_b_in', 'v_lb_logits', 'v_hg_norm_w', 'v_pool_w', 'v_pool_scale', 'v_w_down_hg', 'v_w_down_pool', 'v_w_out', 'v_final_norm_w']
TWIN_OUTPUTS = ['loss', 'grad_x', 'grad_meta_tokens', 'grad_norm_w', 'grad_w_in', 'grad_b_in', 'grad_lb_logits', 'grad_hg_norm_w', 'grad_pool_w', 'grad_pool_scale', 'grad_w_down_hg', 'grad_w_down_pool', 'grad_w_out', 'grad_final_norm_w', 'delta_meta_tokens', 'delta_norm_w', 'delta_w_in', 'delta_b_in', 'delta_lb_logits', 'delta_hg_norm_w', 'delta_pool_w', 'delta_pool_scale', 'delta_w_down_hg', 'delta_w_down_pool', 'delta_w_out', 'delta_final_norm_w', 'new_m_meta_tokens', 'new_m_norm_w', 'new_m_w_in', 'new_m_b_in', 'new_m_lb_logits', 'new_m_hg_norm_w', 'new_m_pool_w', 'new_m_pool_scale', 'new_m_w_down_hg', 'new_m_w_down_pool', 'new_m_w_out', 'new_m_final_norm_w', 'new_v_meta_tokens', 'new_v_norm_w', 'new_v_w_in', 'new_v_b_in', 'new_v_lb_logits', 'new_v_hg_norm_w', 'new_v_pool_w', 'new_v_pool_scale', 'new_v_w_down_hg', 'new_v_w_down_pool', 'new_v_w_out', 'new_v_final_norm_w']
TWIN_LEAF_KINDS = {'loss': 'loss', 'grad_x': 'grad_x', 'grad_meta_tokens': 'grad_w', 'grad_norm_w': 'grad_w', 'grad_w_in': 'grad_w', 'grad_b_in': 'grad_w', 'grad_lb_logits': 'grad_w', 'grad_hg_norm_w': 'grad_w', 'grad_pool_w': 'grad_w', 'grad_pool_scale': 'grad_w', 'grad_w_down_hg': 'grad_w', 'grad_w_down_pool': 'grad_w', 'grad_w_out': 'grad_w', 'grad_final_norm_w': 'grad_w', 'delta_meta_tokens': 'delta_w', 'delta_norm_w': 'delta_w', 'delta_w_in': 'delta_w', 'delta_b_in': 'delta_w', 'delta_lb_logits': 'delta_w', 'delta_hg_norm_w': 'delta_w', 'delta_pool_w': 'delta_w', 'delta_pool_scale': 'delta_w', 'delta_w_down_hg': 'delta_w', 'delta_w_down_pool': 'delta_w', 'delta_w_out': 'delta_w', 'delta_final_norm_w': 'delta_w', 'new_m_meta_tokens': 'new_m', 'new_m_norm_w': 'new_m', 'new_m_w_in': 'new_m', 'new_m_b_in': 'new_m', 'new_m_lb_logits': 'new_m', 'new_m_hg_norm_w': 'new_m', 'new_m_pool_w': 'new_m', 'new_m_pool_scale': 'new_m', 'new_m_w_down_hg': 'new_m', 'new_m_w_down_pool': 'new_m', 'new_m_w_out': 'new_m', 'new_m_final_norm_w': 'new_m', 'new_v_meta_tokens': 'new_v', 'new_v_norm_w': 'new_v', 'new_v_w_in': 'new_v', 'new_v_b_in': 'new_v', 'new_v_lb_logits': 'new_v', 'new_v_hg_norm_w': 'new_v', 'new_v_pool_w': 'new_v', 'new_v_pool_scale': 'new_v', 'new_v_w_down_hg': 'new_v', 'new_v_w_down_pool': 'new_v', 'new_v_w_out': 'new_v', 'new_v_final_norm_w': 'new_v'}


def _forward(args):
    return _fwd_reference(*[args[k] for k in FWD_PARAMS])


def _output_shape():
    def fwd():
        inp = _fwd_setup_inputs(0)
        return _fwd_reference(*[inp[k] for k in FWD_PARAMS])
    out = _jax.eval_shape(fwd)
    return out.shape, out.dtype

N_MICROBATCH = 1
ADAM_LR = 0.001
ADAM_B1 = 0.9
ADAM_B2 = 0.999
ADAM_EPS = 1e-08
ADAM_WD = 0.01
ADAM_STEP = 10
PER_EXAMPLE_BATCH_AXIS = {'x': 0, 'loss_target': 0}
SHARED_INPUTS = []
_WEIGHT_DTYPES = {'meta_tokens': _jnp.float32, 'norm_w': _jnp.float32, 'w_in': _jnp.float32, 'b_in': _jnp.float32, 'lb_logits': _jnp.float32, 'hg_norm_w': _jnp.float32, 'pool_w': _jnp.float32, 'pool_scale': _jnp.float32, 'w_down_hg': _jnp.float32, 'w_down_pool': _jnp.float32, 'w_out': _jnp.float32, 'final_norm_w': _jnp.float32}
MOMENT_SCALE = {'meta_tokens': 1.317583e-03, 'norm_w': 1.386660e-01, 'w_in': 4.789629e-02, 'b_in': 5.580438e-02, 'lb_logits': 3.505395e-02, 'hg_norm_w': 5.258033e-02, 'pool_w': 4.617099e-02, 'pool_scale': 4.695144e-02, 'w_down_hg': 5.250222e-02, 'w_down_pool': 4.647481e-02, 'w_out': 7.024454e-02, 'final_norm_w': 3.204761e+01}


def _to_microbatches(a, axis):
    t = _jnp.moveaxis(a, axis, 0)
    t = t.reshape((N_MICROBATCH, t.shape[0] // N_MICROBATCH) + t.shape[1:])
    return _jnp.moveaxis(t, 1, axis + 1)


def setup_inputs(seed: int = 0) -> dict:
    inp = _fwd_setup_inputs(seed)
    key = _jax.random.fold_in(_jax.random.key(seed), 7919)
    shape, _ = _output_shape()
    out = dict(inp)
    out["loss_target"] = _jax.random.normal(_jax.random.fold_in(key, 0), shape, _jnp.float32)
    for i, name in enumerate(TWIN_WEIGHTS):
        w = inp[name].astype(_jnp.float32)
        if MOMENT_SCALE is None:
            s = _jnp.sqrt(_jnp.mean(_jnp.square(w)) + 1e-30)
        else:
            s = MOMENT_SCALE[name]
        km, kv = _jax.random.split(_jax.random.fold_in(key, i + 1))
        out[name] = w
        out["m_" + name] = s * _jax.random.normal(km, w.shape, _jnp.float32)
        out["v_" + name] = (s * s) * _jax.random.uniform(kv, w.shape, _jnp.float32, 0.5, 1.5)
    if N_MICROBATCH > 1:
        for name, axis in PER_EXAMPLE_BATCH_AXIS.items():
            out[name] = _to_microbatches(out[name], axis)
    return {'x': out['x'], 'meta_tokens': out['meta_tokens'], 'norm_w': out['norm_w'], 'w_in': out['w_in'], 'b_in': out['b_in'], 'lb_logits': out['lb_logits'], 'hg_norm_w': out['hg_norm_w'], 'pool_w': out['pool_w'], 'pool_scale': out['pool_scale'], 'w_down_hg': out['w_down_hg'], 'w_down_pool': out['w_down_pool'], 'w_out': out['w_out'], 'final_norm_w': out['final_norm_w'], 'loss_target': out['loss_target'], 'm_meta_tokens': out['m_meta_tokens'], 'm_norm_w': out['m_norm_w'], 'm_w_in': out['m_w_in'], 'm_b_in': out['m_b_in'], 'm_lb_logits': out['m_lb_logits'], 'm_hg_norm_w': out['m_hg_norm_w'], 'm_pool_w': out['m_pool_w'], 'm_pool_scale': out['m_pool_scale'], 'm_w_down_hg': out['m_w_down_hg'], 'm_w_down_pool': out['m_w_down_pool'], 'm_w_out': out['m_w_out'], 'm_final_norm_w': out['m_final_norm_w'], 'v_meta_tokens': out['v_meta_tokens'], 'v_norm_w': out['v_norm_w'], 'v_w_in': out['v_w_in'], 'v_b_in': out['v_b_in'], 'v_lb_logits': out['v_lb_logits'], 'v_hg_norm_w': out['v_hg_norm_w'], 'v_pool_w': out['v_pool_w'], 'v_pool_scale': out['v_pool_scale'], 'v_w_down_hg': out['v_w_down_hg'], 'v_w_down_pool': out['v_w_down_pool'], 'v_w_out': out['v_w_out'], 'v_final_norm_w': out['v_final_norm_w']}


def _loss(weights, diff, rest, loss_target):
    with _jax.named_scope("forward"):
        args = {**rest, TWIN_DIFF_INPUT: diff, **{k: w.astype(_WEIGHT_DTYPES[k]) for k, w in weights.items()}}
        y = _forward(args)
    with _jax.named_scope("loss_head"):
        err = _jnp.square(y.astype(_jnp.float32) - loss_target)
        return 0.5 * _jnp.sum(_jnp.mean(err, axis=-1)) if err.ndim else 0.5 * err


def _adamw(w, g, m, v):
    m = ADAM_B1 * m + (1.0 - ADAM_B1) * g
    v = ADAM_B2 * v + (1.0 - ADAM_B2) * _jnp.square(g)
    m_hat = m / (1.0 - ADAM_B1 ** ADAM_STEP)
    v_hat = v / (1.0 - ADAM_B2 ** ADAM_STEP)
    delta = -ADAM_LR * (m_hat / (_jnp.sqrt(v_hat) + ADAM_EPS) + ADAM_WD * w)
    return delta, m, v


def reference(x, meta_tokens, norm_w, w_in, b_in, lb_logits, hg_norm_w, pool_w, pool_scale, w_down_hg, w_down_pool, w_out, final_norm_w, loss_target, m_meta_tokens, m_norm_w, m_w_in, m_b_in, m_lb_logits, m_hg_norm_w, m_pool_w, m_pool_scale, m_w_down_hg, m_w_down_pool, m_w_out, m_final_norm_w, v_meta_tokens, v_norm_w, v_w_in, v_b_in, v_lb_logits, v_hg_norm_w, v_pool_w, v_pool_scale, v_w_down_hg, v_w_down_pool, v_w_out, v_final_norm_w):
    given = dict(x=x, meta_tokens=meta_tokens, norm_w=norm_w, w_in=w_in, b_in=b_in, lb_logits=lb_logits, hg_norm_w=hg_norm_w, pool_w=pool_w, pool_scale=pool_scale, w_down_hg=w_down_hg, w_down_pool=w_down_pool, w_out=w_out, final_norm_w=final_norm_w, loss_target=loss_target, m_meta_tokens=m_meta_tokens, m_norm_w=m_norm_w, m_w_in=m_w_in, m_b_in=m_b_in, m_lb_logits=m_lb_logits, m_hg_norm_w=m_hg_norm_w, m_pool_w=m_pool_w, m_pool_scale=m_pool_scale, m_w_down_hg=m_w_down_hg, m_w_down_pool=m_w_down_pool, m_w_out=m_w_out, m_final_norm_w=m_final_norm_w, v_meta_tokens=v_meta_tokens, v_norm_w=v_norm_w, v_w_in=v_w_in, v_b_in=v_b_in, v_lb_logits=v_lb_logits, v_hg_norm_w=v_hg_norm_w, v_pool_w=v_pool_w, v_pool_scale=v_pool_scale, v_w_down_hg=v_w_down_hg, v_w_down_pool=v_w_down_pool, v_w_out=v_w_out, v_final_norm_w=v_final_norm_w)
    weights = {n: given[n] for n in TWIN_WEIGHTS}
    shared = {n: given[n] for n in SHARED_INPUTS}
    per_example = {n: given[n] for n in ['x']}
    grad_fn = _jax.value_and_grad(_loss, argnums=(0, 1))

    def one_microbatch(ex, loss_target):
        ex = dict(ex)
        diff = ex.pop(TWIN_DIFF_INPUT)
        return grad_fn(weights, diff, {**shared, **ex}, loss_target)

    if N_MICROBATCH == 1:
        loss, (grad_w, grad_x) = one_microbatch(per_example, given["loss_target"])
    else:
        def body(carry, xs):
            loss_sum, grad_sum = carry
            l_k, (gw_k, gx_k) = one_microbatch(xs[0], xs[1])
            with _jax.named_scope("update"):
                return (loss_sum + l_k, _jax.tree.map(_jnp.add, grad_sum, gw_k)), gx_k

        init = (_jnp.zeros((), _jnp.float32), _jax.tree.map(_jnp.zeros_like, weights))
        (loss, grad_w), grad_x = _jax.lax.scan(body, init, (per_example, given["loss_target"]))
    with _jax.named_scope("update"):
        delta_w, new_m, new_v = {}, {}, {}
        for n in TWIN_WEIGHTS:
            delta_w[n], new_m[n], new_v[n] = _adamw(weights[n], grad_w[n], given["m_" + n], given["v_" + n])
    return (loss, grad_x, *[grad_w[n] for n in TWIN_WEIGHTS], *[delta_w[n] for n in TWIN_WEIGHTS],
            *[new_m[n] for n in TWIN_WEIGHTS], *[new_v[n] for n in TWIN_WEIGHTS])
```

```python
import functools

import jax
import jax.numpy as jnp
from jax import lax
from jax.experimental import pallas as pl
from jax.experimental.pallas import tpu as pltpu

F32 = jnp.float32
BF16 = jnp.bfloat16

D_MODEL = 1024
N_META = 16
HEADS = 8
HEAD_DIM = 128
CHUNK = 64
SUB = 16
N_SUB = CHUNK // SUB
PAD_ROWS = CHUNK - N_META
POOL_WINDOWS = (2, 4, 8, 16)
POOL_GDIM = D_MODEL // len(POOL_WINDOWS)
HALO = 16
EPS = 1e-6
N_DEV = 8
N_COLBLK = 8
ADAM_LR, ADAM_B1, ADAM_B2, ADAM_EPS, ADAM_WD, ADAM_STEP = 0.001, 0.9, 0.999, 1e-08, 0.01, 10

VMEM_LIMIT = 56 * 1024 * 1024
MESH = pl.DeviceIdType.MESH
ANY = pl.BlockSpec(memory_space=pl.ANY)
HIGHEST = lax.Precision.HIGHEST
NEG_INF = float("-inf")

ROW_W_IN = 0
ROW_W_DOWN_HG = 1024
ROW_W_DOWN_POOL = 1152
ROW_W_OUT = 1280
ROW_POOL_W = 1408
ROW_META = 1440
ROW_SMALL = 1456
N_SMALL = 14
ROWS_PACK = 1472


def _params(sem=None):
    return pltpu.CompilerParams(dimension_semantics=sem, vmem_limit_bytes=VMEM_LIMIT)


def _row_tile(n_rows, prefer):
    best = 16
    for t in range(16, prefer + 1, 16):
        if n_rows % t == 0:
            best = t
    return best


def _sigmoid_pair(x):
    e = jnp.exp(-jnp.abs(x))
    r = 1.0 / (1.0 + e)
    er = e * r
    pos = x >= 0
    return jnp.where(pos, r, er), jnp.where(pos, er, r)


def _dot(a, b):
    return jnp.dot(a.astype(BF16), b.astype(BF16), preferred_element_type=F32)


def _dot_nt(a, b):
    return lax.dot_general(a.astype(BF16), b.astype(BF16), (((1,), (1,)), ((), ())), preferred_element_type=F32)


def _dot_tn(a, b):
    return lax.dot_general(a.astype(BF16), b.astype(BF16), (((0,), (0,)), ((), ())), preferred_element_type=F32)


def _all_gather(block):
    def body(x_ref, out_ref, send_sems, recv_sems, local_sem):
        x, y, c = lax.axis_index("x"), lax.axis_index("y"), lax.axis_index("c")
        me, sibling = (x, y, c), (x, y, 1 - c)
        chips = [(1 - x, y), (x, 1 - y), (1 - x, 1 - y)]

        def slot(px, py, pc):
            return out_ref.at[4 * px + 2 * py + pc]

        def copy(k, blk, to, src=None):
            return pltpu.make_async_remote_copy(
                src_ref=slot(*blk) if src is None else src, dst_ref=slot(*blk),
                send_sem=send_sems.at[k], recv_sem=recv_sems.at[k], device_id=to, device_id_type=MESH)

        mine = pltpu.make_async_copy(x_ref, slot(*me), local_sem)
        mine.start()
        first = [copy(0, me, sibling, src=x_ref)]
        first += [copy(1 + j, me, (*chip, c), src=x_ref) for j, chip in enumerate(chips)]
        for cp in first:
            cp.start()
        passed = [copy(4 + j, (*chip, c), sibling) for j, chip in enumerate(chips)]
        for j, chip in enumerate(chips):
            copy(1 + j, (*chip, c), me).wait_recv()
            passed[j].start()
        copy(0, sibling, me).wait_recv()
        for j, chip in enumerate(chips):
            copy(4 + j, (*chip, 1 - c), me).wait_recv()
        for cp in first + passed:
            cp.wait_send()
        mine.wait()

    return pl.pallas_call(
        body, name="all_gather_weights",
        out_shape=jax.ShapeDtypeStruct((N_DEV,) + block.shape, block.dtype),
        in_specs=[ANY], out_specs=ANY,
        scratch_shapes=[pltpu.SemaphoreType.DMA((7,)), pltpu.SemaphoreType.DMA((7,)), pltpu.SemaphoreType.DMA(())],
    )(block)


def _exchange_sibling(g):
    def body(g_ref, out_ref, send_sems, recv_sems):
        x, y, c = lax.axis_index("x"), lax.axis_index("y"), lax.axis_index("c")
        sibling = (x, y, 1 - c)
        copies = []
        for q in range(4):
            qx, qy = q // 2, q % 2
            copies.append(pltpu.make_async_remote_copy(
                src_ref=g_ref.at[4 * qx + 2 * qy + (1 - c)], dst_ref=out_ref.at[q],
                send_sem=send_sems.at[q], recv_sem=recv_sems.at[q], device_id=sibling, device_id_type=MESH))
        for cp in copies:
            cp.start()
        for cp in copies:
            cp.wait_recv()
        for cp in copies:
            cp.wait_send()

    return pl.pallas_call(
        body, name="reduce_exchange_sibling",
        out_shape=jax.ShapeDtypeStruct((4,) + g.shape[1:], g.dtype),
        in_specs=[ANY], out_specs=ANY,
        scratch_shapes=[pltpu.SemaphoreType.DMA((4,)), pltpu.SemaphoreType.DMA((4,))],
    )(g)


def _exchange_chips(part):
    def body(p_ref, out_ref, send_sems, recv_sems, local_sem):
        x, y, c = lax.axis_index("x"), lax.axis_index("y"), lax.axis_index("c")
        chips = [(1 - x, y), (x, 1 - y), (1 - x, 1 - y)]
        mine = pltpu.make_async_copy(p_ref.at[2 * x + y], out_ref.at[2 * x + y], local_sem)
        mine.start()
        copies = []
        for k, (qx, qy) in enumerate(chips):
            copies.append(pltpu.make_async_remote_copy(
                src_ref=p_ref.at[2 * qx + qy], dst_ref=out_ref.at[2 * x + y],
                send_sem=send_sems.at[k], recv_sem=recv_sems.at[k], device_id=(qx, qy, c), device_id_type=MESH))
        for cp in copies:
            cp.start()
        for k, (qx, qy) in enumerate(chips):
            pltpu.make_async_remote_copy(
                src_ref=p_ref.at[2 * qx + qy], dst_ref=out_ref.at[2 * qx + qy],
                send_sem=send_sems.at[k], recv_sem=recv_sems.at[k], device_id=(qx, qy, c), device_id_type=MESH).wait_recv()
        for cp in copies:
            cp.wait_send()
        mine.wait()

    return pl.pallas_call(
        body, name="reduce_exchange_chips",
        out_shape=jax.ShapeDtypeStruct(part.shape, part.dtype),
        in_specs=[ANY], out_specs=ANY,
        scratch_shapes=[pltpu.SemaphoreType.DMA((3,)), pltpu.SemaphoreType.DMA((3,)), pltpu.SemaphoreType.DMA(())],
    )(part)


def _pair_sum(g, got, own_slots):
    rows, cols = g.shape[1:]
    tr = _row_tile(rows, 368)

    def body(slots_ref, a_ref, b_ref, o_ref):
        o_ref[...] = a_ref[...] + b_ref[...]

    return pl.pallas_call(
        body, name="reduce_pair_sum",
        out_shape=jax.ShapeDtypeStruct(got.shape, F32),
        grid_spec=pltpu.PrefetchScalarGridSpec(
            num_scalar_prefetch=1, grid=(4, rows // tr),
            in_specs=[pl.BlockSpec((1, tr, cols), lambda q, i, s: (s[q], i, 0)),
                      pl.BlockSpec((1, tr, cols), lambda q, i, s: (q, i, 0))],
            out_specs=pl.BlockSpec((1, tr, cols), lambda q, i, s: (q, i, 0))),
        compiler_params=_params(("arbitrary", "arbitrary")),
    )(own_slots, g, got)


def _adamw(parts, w, m, v):
    rows, cols = w.shape
    tr = _row_tile(rows, 368)

    def body(p_ref, w_ref, m_ref, v_ref, go_ref, d_ref, mo_ref, vo_ref):
        acc = (p_ref[0] + p_ref[1]) + (p_ref[2] + p_ref[3])
        go_ref[...] = acc
        mn = ADAM_B1 * m_ref[...] + (1.0 - ADAM_B1) * acc
        vn = ADAM_B2 * v_ref[...] + (1.0 - ADAM_B2) * (acc * acc)
        m_hat = mn / (1.0 - ADAM_B1 ** ADAM_STEP)
        v_hat = vn / (1.0 - ADAM_B2 ** ADAM_STEP)
        d_ref[...] = -ADAM_LR * (m_hat / (jnp.sqrt(v_hat) + ADAM_EPS) + ADAM_WD * w_ref[...])
        mo_ref[...] = mn
        vo_ref[...] = vn

    blk4 = pl.BlockSpec((4, tr, cols), lambda i: (0, i, 0))
    blk = pl.BlockSpec((tr, cols), lambda i: (i, 0))
    sds = jax.ShapeDtypeStruct((rows, cols), F32)
    return pl.pallas_call(
        body, name="reduce_sum_adamw",
        out_shape=(sds, sds, sds, sds),
        grid=(rows // tr,),
        in_specs=[blk4, blk, blk, blk], out_specs=(blk, blk, blk, blk),
        compiler_params=_params(("arbitrary",)),
    )(parts, w, m, v)


def _in_projection(zp, norm_w, w_blocks, b_blocks):
    n_rows = zp.shape[0]
    tr = _row_tile(n_rows, 320)

    def body(z_ref, nw_ref, w_hbm, b_ref, p_ref, h_ref, w_vmem, h_vmem, sem):
        i, j = pl.program_id(0), pl.program_id(1)

        @pl.when((i == 0) & (j == 0))
        def _():
            cp = pltpu.make_async_copy(w_hbm, w_vmem, sem)
            cp.start()
            cp.wait()

        @pl.when(j == 0)
        def _():
            z = z_ref[...]
            r = lax.rsqrt(jnp.mean(z * z, axis=-1, keepdims=True) + EPS)
            h = (z * r * nw_ref[...]).astype(BF16)
            h_vmem[...] = h
            h_ref[...] = h

        p_ref[0] = jnp.dot(h_vmem[...], w_vmem[j], preferred_element_type=F32) + b_ref[0]

    return pl.pallas_call(
        body, name="in_projection",
        out_shape=(jax.ShapeDtypeStruct((N_COLBLK, n_rows, 1024), F32), jax.ShapeDtypeStruct((n_rows, D_MODEL), BF16)),
        grid=(n_rows // tr, N_COLBLK),
        in_specs=[pl.BlockSpec((tr, D_MODEL), lambda i, j: (i, 0)),
                  pl.BlockSpec((1, D_MODEL), lambda i, j: (0, 0)),
                  ANY,
                  pl.BlockSpec((1, 1, 1024), lambda i, j: (j, 0, 0))],
        out_specs=(pl.BlockSpec((1, tr, 1024), lambda i, j: (j, i, 0)),
                   pl.BlockSpec((tr, D_MODEL), lambda i, j: (i, 0))),
        scratch_shapes=[pltpu.VMEM((N_COLBLK, D_MODEL, 1024), BF16), pltpu.VMEM((tr, D_MODEL), BF16),
                        pltpu.SemaphoreType.DMA(())],
        compiler_params=_params(("arbitrary", "arbitrary")),
    )(zp, norm_w, w_blocks, b_blocks)


def _lower_bound(lb_ref):
    l0, l1 = lb_ref[0:1, :], lb_ref[1:2, :]
    _, lb = _sigmoid_pair(l1 - l0)
    return lb


def _chunk_gates(fz, lb, valid):
    sig, nsig = _sigmoid_pair(fz)
    f = lb + (1.0 - lb) * sig
    g = jnp.where(valid, jnp.log(f), 0.0)
    k = jnp.where(valid, (1.0 - lb) * nsig, 0.0)
    return sig, nsig, f, g, k


def _tri(n, upper=False):
    r = lax.broadcasted_iota(jnp.int32, (n, n), 0)
    c = lax.broadcasted_iota(jnp.int32, (n, n), 1)
    return jnp.where((r <= c) if upper else (r >= c), 1.0, 0.0).astype(F32)


def _intra_scores(q_ref, k_ref, b_ref, a_ref, col0):
    cols = pl.ds(col0, HEAD_DIM)
    rows_c = lax.broadcasted_iota(jnp.int32, (CHUNK, 1), 0)
    rows_s = lax.broadcasted_iota(jnp.int32, (SUB, 1), 0)
    lanes_c = lax.broadcasted_iota(jnp.int32, (1, CHUNK), 1)
    b_all = b_ref[:, cols]
    k_all = k_ref[:, cols]
    for i in range(N_SUB):
        lo = i * SUB
        qi = q_ref[lo:lo + SUB, cols]
        bi = b_ref[lo:lo + SUB, cols]
        if i == 0:
            acc = jnp.zeros((SUB, CHUNK), F32)
        else:
            ref_i = b_ref[lo:lo + 1, cols]
            qt = qi * jnp.exp(bi - ref_i)
            kt = k_all * jnp.exp(jnp.where(rows_c < lo, ref_i - b_all, NEG_INF))
            acc = _dot_nt(qt, kt)
        for s in range(SUB):
            b_s = b_ref[lo + s:lo + s + 1, cols]
            k_s = k_ref[lo + s:lo + s + 1, cols]
            w = jnp.exp(jnp.where(rows_s >= s, bi - b_s, NEG_INF))
            col = jnp.sum(qi * k_s * w, axis=-1, keepdims=True)
            acc = acc + jnp.where(lanes_c == lo + s, col, 0.0)
        a_ref[lo:lo + SUB, :] = acc


def _hgrn_forward(p, lb_logits):
    n_rows = p.shape[1]
    n_chunks = n_rows // CHUNK
    width = HEADS * HEAD_DIM

    def body(q_ref, fz_ref, v_ref, lb_ref, o_ref, st_out_ref, state, k_vmem, b_vmem, a_vmem):
        n = pl.program_id(0)

        @pl.when(n == 0)
        def _():
            state[...] = jnp.zeros_like(state)

        rows = n * CHUNK + lax.broadcasted_iota(jnp.int32, (CHUNK, 1), 0)
        valid = rows >= PAD_ROWS
        lb = _lower_bound(lb_ref)
        _, _, _, g, k = _chunk_gates(fz_ref[0], lb, valid)
        b = jnp.dot(_tri(CHUNK), g, precision=HIGHEST, preferred_element_type=F32)
        k_vmem[...] = k
        b_vmem[...] = b
        q_view = q_ref.at[0]
        for h in range(HEADS):
            cols = pl.ds(h * HEAD_DIM, HEAD_DIM)
            st = state[h]
            st_out_ref[0, h] = st
            bh = b_vmem[:, cols]
            kh = k_vmem[:, cols]
            vh = jnp.where(valid, v_ref[0, :, cols], 0.0)
            qe = q_ref[0, :, cols] * jnp.exp(bh)
            _intra_scores(q_view, k_vmem, b_vmem, a_vmem, h * HEAD_DIM)
            o_ref[:, cols] = _dot_nt(qe, st) + _dot(a_vmem[...], vh)
            b_last = b_vmem[CHUNK - 1:CHUNK, cols]
            kd = kh * jnp.exp(b_last - bh)
            state[h] = st * jnp.exp(b_last) + _dot_tn(vh, kd)

    blk = lambda c: pl.BlockSpec((1, CHUNK, width), lambda n, c=c: (c, n, 0))
    return pl.pallas_call(
        body, name="hgrn_forward",
        out_shape=(jax.ShapeDtypeStruct((n_rows, width), F32),
                   jax.ShapeDtypeStruct((n_chunks, HEADS, HEAD_DIM, HEAD_DIM), F32)),
        grid=(n_chunks,),
        in_specs=[blk(0), blk(1), blk(2), pl.BlockSpec((2, width), lambda n: (0, 0))],
        out_specs=(pl.BlockSpec((CHUNK, width), lambda n: (n, 0)),
                   pl.BlockSpec((1, HEADS, HEAD_DIM, HEAD_DIM), lambda n: (n, 0, 0, 0))),
        scratch_shapes=[pltpu.VMEM((HEADS, HEAD_DIM, HEAD_DIM), F32), pltpu.VMEM((CHUNK, width), F32),
                        pltpu.VMEM((CHUNK, width), F32), pltpu.VMEM((CHUNK, CHUNK), F32)],
        compiler_params=_params(("arbitrary",)),
    )(p, p, p, lb_logits)


def _hgrn_backward(p, lb_logits, states, d_o):
    n_rows = p.shape[1]
    n_chunks = n_rows // CHUNK
    width = HEADS * HEAD_DIM

    def body(q_ref, fz_ref, v_ref, lb_ref, st_ref, do_ref, dp_ref, dbias_ref, dlb_ref,
             dstate, k_vmem, b_vmem, a_vmem, dqi_vmem, dki_vmem):
        step = pl.program_id(0)
        n = n_chunks - 1 - step

        @pl.when(step == 0)
        def _():
            dstate[...] = jnp.zeros_like(dstate)
            dbias_ref[...] = jnp.zeros_like(dbias_ref)
            dlb_ref[...] = jnp.zeros_like(dlb_ref)

        rows = n * CHUNK + lax.broadcasted_iota(jnp.int32, (CHUNK, 1), 0)
        valid = rows >= PAD_ROWS
        lb = _lower_bound(lb_ref)
        sig, nsig, f, g, k = _chunk_gates(fz_ref[0], lb, valid)
        b = jnp.dot(_tri(CHUNK), g, precision=HIGHEST, preferred_element_type=F32)
        k_vmem[...] = k
        b_vmem[...] = b
        q_view = q_ref.at[0]
        rows_c = lax.broadcasted_iota(jnp.int32, (CHUNK, 1), 0)
        rows_s = lax.broadcasted_iota(jnp.int32, (SUB, 1), 0)
        lanes_c = lax.broadcasted_iota(jnp.int32, (1, CHUNK), 1)
        causal = lax.broadcasted_iota(jnp.int32, (CHUNK, CHUNK), 0) >= lax.broadcasted_iota(jnp.int32, (CHUNK, CHUNK), 1)
        tri_up = _tri(CHUNK, upper=True)
        for h in range(HEADS):
            cols = pl.ds(h * HEAD_DIM, HEAD_DIM)
            st = st_ref[0, h]
            dst = dstate[h]
            qh = q_ref[0, :, cols]
            bh = b_vmem[:, cols]
            kh = k_vmem[:, cols]
            vh = jnp.where(valid, v_ref[0, :, cols], 0.0)
            doh = do_ref[:, cols]
            eb = jnp.exp(bh)
            qe = qh * eb
            b_last = b_vmem[CHUNK - 1:CHUNK, cols]
            e_last = jnp.exp(b_last)
            decay_k = jnp.exp(b_last - bh)
            kd = kh * decay_k
            _intra_scores(q_view, k_vmem, b_vmem, a_vmem, h * HEAD_DIM)
            a = a_vmem[...]
            dqe = _dot(doh, st)
            da = jnp.where(causal, _dot_nt(doh, vh), 0.0)
            dv = _dot_tn(a, doh) + _dot_nt(kd, dst)
            dkd = _dot(vh, dst)
            dstate[h] = dst * e_last + _dot_tn(doh, qe)
            db_last = (jnp.sum(dst * st, axis=0, keepdims=True) * e_last
                       + jnp.sum(dkd * kd, axis=0, keepdims=True))
            dki_vmem[...] = jnp.zeros_like(dki_vmem)
            for i in range(N_SUB):
                lo = i * SUB
                qi = q_ref[0, lo:lo + SUB, cols]
                bi = b_vmem[lo:lo + SUB, cols]
                da_i = da[lo:lo + SUB, :]
                if i == 0:
                    dq_i = jnp.zeros((SUB, HEAD_DIM), F32)
                else:
                    ref_i = b_vmem[lo:lo + 1, cols]
                    eq = jnp.exp(bi - ref_i)
                    ek = jnp.exp(jnp.where(rows_c < lo, ref_i - bh, NEG_INF))
                    qt = qi * eq
                    kt = kh * ek
                    dq_i = _dot(da_i, kt) * eq
                    dki_vmem[...] += _dot_tn(da_i, qt) * ek
                for s in range(SUB):
                    b_s = b_vmem[lo + s:lo + s + 1, cols]
                    k_s = k_vmem[lo + s:lo + s + 1, cols]
                    w = jnp.exp(jnp.where(rows_s >= s, bi - b_s, NEG_INF))
                    da_col = jnp.sum(jnp.where(lanes_c == lo + s, da_i, 0.0), axis=-1, keepdims=True)
                    gw = da_col * w
                    dq_i = dq_i + gw * k_s
                    dki_vmem[lo + s:lo + s + 1, :] += jnp.sum(gw * qi, axis=0, keepdims=True)
                dqi_vmem[lo:lo + SUB, :] = dq_i
            dq_intra = dqi_vmem[...]
            dk_intra = dki_vmem[...]
            dq = dqe * eb + dq_intra
            dk = dkd * decay_k + dk_intra
            db = dqe * qe - dkd * kd + qh * dq_intra - kh * dk_intra
            db = db + jnp.where(rows_c == CHUNK - 1, db_last, 0.0)
            dg = jnp.dot(tri_up, db, precision=HIGHEST, preferred_element_type=F32)
            fh = f[:, h * HEAD_DIM:(h + 1) * HEAD_DIM]
            sh = sig[:, h * HEAD_DIM:(h + 1) * HEAD_DIM]
            nh = nsig[:, h * HEAD_DIM:(h + 1) * HEAD_DIM]
            lbh = lb[:, h * HEAD_DIM:(h + 1) * HEAD_DIM]
            df = jnp.where(valid, dg / fh - dk, 0.0)
            dfz = df * (1.0 - lbh) * sh * nh
            dq = jnp.where(valid, dq, 0.0)
            dv = jnp.where(valid, dv, 0.0)
            dlb_ref[:, cols] += jnp.sum(df * nh, axis=0, keepdims=True)
            dp_ref[0, :, cols] = dq.astype(BF16)
            dp_ref[1, :, cols] = dfz.astype(BF16)
            dp_ref[2, :, cols] = dv.astype(BF16)
            dbias_ref[0, :, cols] += jnp.sum(dq, axis=0, keepdims=True)
            dbias_ref[1, :, cols] += jnp.sum(dfz, axis=0, keepdims=True)
            dbias_ref[2, :, cols] += jnp.sum(dv, axis=0, keepdims=True)

    rev = lambda s: n_chunks - 1 - s
    blk = lambda c: pl.BlockSpec((1, CHUNK, width), lambda s, c=c: (c, rev(s), 0))
    return pl.pallas_call(
        body, name="hgrn_backward",
        out_shape=(jax.ShapeDtypeStruct((3, n_rows, width), BF16),
                   jax.ShapeDtypeStruct((3, 1, width), F32),
                   jax.ShapeDtypeStruct((1, width), F32)),
        grid=(n_chunks,),
        in_specs=[blk(0), blk(1), blk(2), pl.BlockSpec((2, width), lambda s: (0, 0)),
                  pl.BlockSpec((1, HEADS, HEAD_DIM, HEAD_DIM), lambda s: (rev(s), 0, 0, 0)),
                  pl.BlockSpec((CHUNK, width), lambda s: (rev(s), 0))],
        out_specs=(pl.BlockSpec((3, CHUNK, width), lambda s: (0, rev(s), 0)),
                   pl.BlockSpec((3, 1, width), lambda s: (0, 0, 0)),
                   pl.BlockSpec((1, width), lambda s: (0, 0))),
        scratch_shapes=[pltpu.VMEM((HEADS, HEAD_DIM, HEAD_DIM), F32), pltpu.VMEM((CHUNK, width), F32),
                        pltpu.VMEM((CHUNK, width), F32), pltpu.VMEM((CHUNK, CHUNK), F32),
                        pltpu.VMEM((CHUNK, HEAD_DIM), F32), pltpu.VMEM((CHUNK, HEAD_DIM), F32)],
        compiler_params=_params(("arbitrary",)),
    )(p, p, p, lb_logits, states, d_o)


def _silu_and_grad(x):
    s, ns = _sigmoid_pair(x)
    return x * s, s * (1.0 + x * ns)


def _tail(p, o, zp, tgt, hg_norm_w, pool_w, pool_scale, w_down_hg, w_down_pool, w_out, final_norm_w):
    n_rows = zp.shape[0]
    tr = _row_tile(n_rows, 160)
    nt = n_rows // tr
    ext = tr + HALO
    n_groups = len(POOL_WINDOWS)

    def body(o_ref, ghg_ref, u_ref, gpool_ref, mhg_ref, mpool_ref, uhalo_ref, z_ref, tgt_ref,
             hgw_ref, pw_ref, ps_ref, wdh_ref, wdp_ref, wout_ref, fnw_ref,
             do_ref, dp_ref, dz2_ref, merged_ref, dz2b_ref, ahg_ref, dyhg_ref, apool_ref, dypool_ref,
             dbias_ref, dhgw_ref, dpw_ref, dps_ref, dfnw_ref, loss_ref, halo_vmem):
        step = pl.program_id(0)
        ti = nt - 1 - step

        @pl.when(step == 0)
        def _():
            halo_vmem[...] = jnp.zeros_like(halo_vmem)
            for r in (dbias_ref, dhgw_ref, dpw_ref, dps_ref, dfnw_ref, loss_ref):
                r[...] = jnp.zeros_like(r)

        rows = ti * tr + lax.broadcasted_iota(jnp.int32, (tr, 1), 0)
        valid = rows >= PAD_ROWS
        in_loss = rows >= CHUNK
        count_pos = jnp.maximum(rows - PAD_ROWS + 1, 1).astype(F32)

        o = o_ref[...]
        hgw = hgw_ref[...]
        inv_o, on_parts = [], []
        for h in range(HEADS):
            oh = o[:, h * HEAD_DIM:(h + 1) * HEAD_DIM]
            r = lax.rsqrt(jnp.mean(oh * oh, axis=-1, keepdims=True) + EPS)
            inv_o.append(r)
            on_parts.append(oh * r)
        o_hat = jnp.concatenate(on_parts, axis=1)
        o_n = o_hat * hgw
        g_hg = ghg_ref[0]
        silu_hg, dsilu_hg = _silu_and_grad(g_hg)
        a_hg = o_n * silu_hg
        y_hg = _dot(a_hg, wdh_ref[...])

        u = jnp.where(valid, u_ref[0], 0.0)
        u_prev = jnp.where(ti > 0, uhalo_ref[0], 0.0)
        u_ext = jnp.concatenate([u_prev, u], axis=0)
        pooled_parts, mixed_parts, inv_cnt = [], [], []
        for gi, win in enumerate(POOL_WINDOWS):
            lanes = slice(gi * POOL_GDIM, (gi + 1) * POOL_GDIM)
            s = u_ext[:, lanes]
            shift = 1
            while shift < win:
                s = s + pltpu.roll(s, shift, 0)
                shift *= 2
            ic = 1.0 / jnp.minimum(count_pos, float(win))
            inv_cnt.append(ic)
            pooled = s[HALO:] * ic - u[:, lanes]
            pooled_parts.append(pooled)
            mixed_parts.append(_dot(pooled, pw_ref[gi]))
        mixed = jnp.concatenate(mixed_parts, axis=1)
        ps = ps_ref[...]
        g_pool = gpool_ref[0]
        silu_pool, dsilu_pool = _silu_and_grad(g_pool)
        a_pool = mixed * ps * silu_pool
        y_pool = _dot(a_pool, wdp_ref[...])

        m_hg, m_pool = mhg_ref[0], mpool_ref[0]
        s_hg, ns_hg = _sigmoid_pair(m_hg)
        s_pool, ns_pool = _sigmoid_pair(m_pool)
        merged = s_hg * y_hg + s_pool * y_pool
        z2 = z_ref[...] + _dot(merged, wout_ref[...])
        r2 = lax.rsqrt(jnp.mean(z2 * z2, axis=-1, keepdims=True) + EPS)
        n2 = z2 * r2
        fnw = fnw_ref[...]
        err = jnp.where(in_loss, n2 * fnw - tgt_ref[...], 0.0)
        loss_ref[...] += jnp.sum(jnp.sum(err * err, axis=0, keepdims=True), axis=1, keepdims=True) * (0.5 / D_MODEL)
        dy = err * (1.0 / D_MODEL)

        dfnw_ref[...] += jnp.sum(dy * n2, axis=0, keepdims=True)
        gy = dy * fnw
        dz2 = r2 * (gy - n2 * jnp.mean(gy * n2, axis=-1, keepdims=True))
        dmerged = _dot_nt(dz2, wout_ref[...])
        dy_hg = s_hg * dmerged
        dy_pool = s_pool * dmerged
        dm_hg = dmerged * y_hg * s_hg * ns_hg
        dm_pool = dmerged * y_pool * s_pool * ns_pool
        da_hg = _dot_nt(dy_hg, wdh_ref[...])
        da_pool = _dot_nt(dy_pool, wdp_ref[...])

        d_on = da_hg * silu_hg
        dg_hg = da_hg * o_n * dsilu_hg
        dhgw_ref[...] += jnp.sum(d_on * o_hat, axis=0, keepdims=True)
        gyo = d_on * hgw
        do_parts = []
        for h in range(HEADS):
            lanes = slice(h * HEAD_DIM, (h + 1) * HEAD_DIM)
            gh, nh = gyo[:, lanes], o_hat[:, lanes]
            do_parts.append(inv_o[h] * (gh - nh * jnp.mean(gh * nh, axis=-1, keepdims=True)))
        do_ref[...] = jnp.concatenate(do_parts, axis=1)

        dmixed = da_pool * ps * silu_pool
        dps_ref[...] += jnp.sum(da_pool * mixed * silu_pool, axis=0, keepdims=True)
        dg_pool = da_pool * mixed * ps * dsilu_pool
        du_parts = []
        for gi, win in enumerate(POOL_WINDOWS):
            lanes = slice(gi * POOL_GDIM, (gi + 1) * POOL_GDIM)
            dmx = dmixed[:, lanes]
            dpooled = _dot_nt(dmx, pw_ref[gi])
            dpw_ref[gi] += _dot_tn(pooled_parts[gi], dmx)
            dpt = dpooled * inv_cnt[gi]
            s = jnp.concatenate([dpt, halo_vmem[:, lanes]], axis=0)
            shift = 1
            while shift < win:
                s = s + pltpu.roll(s, ext - shift, 0)
                shift *= 2
            du_parts.append(s[:tr] - dpooled)
            halo_vmem[:, lanes] = dpt[:HALO]
        du = jnp.where(valid, jnp.concatenate(du_parts, axis=1), 0.0)

        for c, val in enumerate((dg_hg, du, dg_pool, dm_hg, dm_pool)):
            dp_ref[c] = val.astype(BF16)
            dbias_ref[c] += jnp.sum(val, axis=0, keepdims=True)
        dz2_ref[...] = dz2
        merged_ref[...] = merged.astype(BF16)
        dz2b_ref[...] = dz2.astype(BF16)
        ahg_ref[...] = a_hg.astype(BF16)
        dyhg_ref[...] = dy_hg.astype(BF16)
        apool_ref[...] = a_pool.astype(BF16)
        dypool_ref[...] = dy_pool.astype(BF16)

    rev = lambda s: nt - 1 - s
    rowblk = pl.BlockSpec((tr, D_MODEL), lambda s: (rev(s), 0))
    pblk = lambda c: pl.BlockSpec((1, tr, 1024), lambda s, c=c: (c, rev(s), 0))
    halo_blk = pl.BlockSpec((1, HALO, 1024), lambda s: (4, jnp.maximum(rev(s) * (tr // HALO) - 1, 0), 0))
    full = lambda shape: pl.BlockSpec(shape, lambda s: (0,) * len(shape))
    vec = full((1, D_MODEL))
    mat = full((D_MODEL, D_MODEL))
    act = jax.ShapeDtypeStruct((n_rows, D_MODEL), BF16)
    return pl.pallas_call(
        body, name="tail_forward_backward",
        out_shape=(jax.ShapeDtypeStruct((n_rows, D_MODEL), F32),
                   jax.ShapeDtypeStruct((5, n_rows, 1024), BF16),
                   jax.ShapeDtypeStruct((n_rows, D_MODEL), F32),
                   act, act, act, act, act, act,
                   jax.ShapeDtypeStruct((5, 1, 1024), F32),
                   jax.ShapeDtypeStruct((1, D_MODEL), F32),
                   jax.ShapeDtypeStruct((n_groups, POOL_GDIM, POOL_GDIM), F32),
                   jax.ShapeDtypeStruct((1, D_MODEL), F32),
                   jax.ShapeDtypeStruct((1, D_MODEL), F32),
                   jax.ShapeDtypeStruct((1, 1), F32)),
        grid=(nt,),
        in_specs=[rowblk, pblk(3), pblk(4), pblk(5), pblk(6), pblk(7), halo_blk, rowblk, rowblk,
                  vec, full((n_groups, POOL_GDIM, POOL_GDIM)), vec, mat, mat, mat, vec],
        out_specs=(rowblk, pl.BlockSpec((5, tr, 1024), lambda s: (0, rev(s), 0)), rowblk,
                   rowblk, rowblk, rowblk, rowblk, rowblk, rowblk,
                   full((5, 1, 1024)), vec, full((n_groups, POOL_GDIM, POOL_GDIM)), vec, vec, full((1, 1))),
        scratch_shapes=[pltpu.VMEM((HALO, D_MODEL), F32)],
        compiler_params=_params(("arbitrary",)),
    )(o, p, p, p, p, p, p, zp, tgt, hg_norm_w, pool_w, pool_scale, w_down_hg, w_down_pool, w_out, final_norm_w)


def _in_projection_backward(dp_a, dp_b, w_blocks, zp, dz2, norm_w):
    n_rows = zp.shape[0]
    tr = _row_tile(n_rows, 320)
    na, nb = dp_a.shape[0], dp_b.shape[0]

    def body(dpa_ref, dpb_ref, w_hbm, z_ref, dz2_ref, nw_ref, dz_ref, dnw_ref, w_vmem, sem):
        i = pl.program_id(0)

        @pl.when(i == 0)
        def _():
            cp = pltpu.make_async_copy(w_hbm, w_vmem, sem)
            cp.start()
            cp.wait()
            dnw_ref[...] = jnp.zeros_like(dnw_ref)

        dh = jnp.zeros((tr, D_MODEL), F32)
        for j in range(na):
            dh = dh + _dot_nt(dpa_ref[j], w_vmem[j])
        for j in range(nb):
            dh = dh + _dot_nt(dpb_ref[j], w_vmem[na + j])
        z = z_ref[...]
        r = lax.rsqrt(jnp.mean(z * z, axis=-1, keepdims=True) + EPS)
        n1 = z * r
        dnw_ref[...] += jnp.sum(dh * n1, axis=0, keepdims=True)
        gh = dh * nw_ref[...]
        dz_ref[...] = dz2_ref[...] + r * (gh - n1 * jnp.mean(gh * n1, axis=-1, keepdims=True))

    rowblk = pl.BlockSpec((tr, D_MODEL), lambda i: (i, 0))
    vec = pl.BlockSpec((1, D_MODEL), lambda i: (0, 0))
    return pl.pallas_call(
        body, name="in_projection_backward",
        out_shape=(jax.ShapeDtypeStruct((n_rows, D_MODEL), F32), jax.ShapeDtypeStruct((1, D_MODEL), F32)),
        grid=(n_rows // tr,),
        in_specs=[pl.BlockSpec((na, tr, 1024), lambda i: (0, i, 0)), pl.BlockSpec((nb, tr, 1024), lambda i: (0, i, 0)),
                  ANY, rowblk, rowblk, vec],
        out_specs=(rowblk, vec),
        scratch_shapes=[pltpu.VMEM((N_COLBLK, D_MODEL, 1024), BF16), pltpu.SemaphoreType.DMA(())],
        compiler_params=_params(("arbitrary",)),
    )(dp_a, dp_b, w_blocks, zp, dz2, norm_w)


def _weight_grad(xs, ys):
    n_rows, m = xs.shape
    nb, _, n = ys.shape
    tk = _row_tile(n_rows, 832)

    def body(x_ref, y_ref, o_ref):
        @pl.when(pl.program_id(1) == 0)
        def _():
            o_ref[...] = jnp.zeros_like(o_ref)

        o_ref[0] += _dot_tn(x_ref[...], y_ref[0])

    return pl.pallas_call(
        body, name="weight_grad",
        out_shape=jax.ShapeDtypeStruct((nb, m, n), F32),
        grid=(nb, n_rows // tk),
        in_specs=[pl.BlockSpec((tk, m), lambda j, k: (k, 0)), pl.BlockSpec((1, tk, n), lambda j, k: (j, k, 0))],
        out_specs=pl.BlockSpec((1, m, n), lambda j, k: (j, 0, 0)),
        compiler_params=_params(("arbitrary", "arbitrary")),
    )(xs, ys)


def _pack_rows(w_in, w_down_hg, w_down_pool, w_out, pool_w, meta, small):
    meta_rows = jnp.pad(meta, ((0, 0), (0, 1024 - meta.shape[1])))
    small_rows = jnp.pad(small, ((0, ROWS_PACK - ROW_SMALL - N_SMALL), (0, 0)))
    return jnp.concatenate([w_in, w_down_hg, w_down_pool, w_out, pool_w.reshape(32, 1024), meta_rows, small_rows], axis=0)


def _small_rows(norm_w, b_in, lb_logits, hg_norm_w, pool_scale, final_norm_w):
    return jnp.concatenate([norm_w.reshape(1, 1024), b_in.reshape(8, 1024), lb_logits.reshape(2, 1024),
                            hg_norm_w.reshape(1, 1024), pool_scale.reshape(1, 1024), final_norm_w.reshape(1, 1024)], axis=0)


def _unpack_rows(blk):
    small = blk[ROW_SMALL:ROW_SMALL + N_SMALL]
    return (blk[ROW_META:ROW_META + N_META, :128],
            small[0:1],
            blk[ROW_W_IN:ROW_W_IN + 1024].reshape(1, 1024, 1024),
            small[1:9].reshape(1, 8192),
            small[9:11],
            small[11:12],
            blk[ROW_POOL_W:ROW_POOL_W + 32].reshape(1, 4, 32, 256),
            small[12:13],
            blk[ROW_W_DOWN_HG:ROW_W_DOWN_HG + 128].reshape(1, 128, 1024),
            blk[ROW_W_DOWN_POOL:ROW_W_DOWN_POOL + 128].reshape(1, 128, 1024),
            blk[ROW_W_OUT:ROW_W_OUT + 128].reshape(1, 128, 1024),
            small[13])


def kernel(x, meta_tokens, norm_w, w_in, b_in, lb_logits, hg_norm_w, pool_w, pool_scale, w_down_hg, w_down_pool, w_out, final_norm_w, loss_target, m_meta_tokens, m_norm_w, m_w_in, m_b_in, m_lb_logits, m_hg_norm_w, m_pool_w, m_pool_scale, m_w_down_hg, m_w_down_pool, m_w_out, m_final_norm_w, v_meta_tokens, v_norm_w, v_w_in, v_b_in, v_lb_logits, v_hg_norm_w, v_pool_w, v_pool_scale, v_w_down_hg, v_w_down_pool, v_w_out, v_final_norm_w):
    seq = x.shape[1]
    ac = lax.axis_index("c")

    meta_bits = lax.bitcast_convert_type(meta_tokens, BF16).reshape(4, 1024)
    wrows = jnp.concatenate([w_in[0].astype(BF16), w_down_hg[0].astype(BF16), w_down_pool[0].astype(BF16),
                             w_out[0].astype(BF16), pool_w[0].astype(BF16).reshape(32, 1024),
                             jnp.pad(meta_bits, ((0, 12), (0, 0)))], axis=0)
    gathered = _all_gather(wrows)
    w_blocks = gathered[:, ROW_W_IN:ROW_W_IN + 1024]
    wdh = gathered[:, ROW_W_DOWN_HG:ROW_W_DOWN_HG + 128].reshape(1024, 1024)
    wdp = gathered[:, ROW_W_DOWN_POOL:ROW_W_DOWN_POOL + 128].reshape(1024, 1024)
    wout = gathered[:, ROW_W_OUT:ROW_W_OUT + 128].reshape(1024, 1024)
    pw = gathered[:, ROW_POOL_W:ROW_POOL_W + 32].reshape(N_DEV, 4, 32, 256).transpose(1, 0, 2, 3).reshape(4, 256, 256)
    meta_full = lax.bitcast_convert_type(gathered[:, ROW_META:ROW_META + 4].reshape(N_DEV, N_META, 128, 2), F32)
    meta_full = meta_full.transpose(1, 0, 2).reshape(N_META, D_MODEL)

    zp = jnp.concatenate([jnp.zeros((PAD_ROWS, D_MODEL), F32), meta_full, x[0]], axis=0)
    tgt = jnp.concatenate([jnp.zeros((CHUNK, D_MODEL), F32), loss_target[0]], axis=0)
    p, h = _in_projection(zp, norm_w, w_blocks, b_in.reshape(N_COLBLK, 1, 1024))
    o, states = _hgrn_forward(p, lb_logits)
    (d_o, dp_b, dz2, merged, dz2_b, a_hg, dy_hg, a_pool, dy_pool,
     dbias_b, d_hgw, d_pw, d_ps, d_fnw, loss_part) = _tail(
        p, o, zp, tgt, hg_norm_w, pw, pool_scale, wdh, wdp, wout, final_norm_w.reshape(1, D_MODEL))
    dp_a, dbias_a, d_lb = _hgrn_backward(p, lb_logits, states, d_o)
    dz, d_nw = _in_projection_backward(dp_a, dp_b, w_blocks, zp, dz2, norm_w)
    dw_in = jnp.concatenate([_weight_grad(h, dp_a), _weight_grad(h, dp_b)], axis=0)
    dw_out = _weight_grad(merged, dz2_b[None])[0]
    dw_dh = _weight_grad(a_hg, dy_hg[None])[0]
    dw_dp = _weight_grad(a_pool, dy_pool[None])[0]

    lb = jax.nn.sigmoid(lb_logits[0:1] - lb_logits[1:2])
    d_l0 = d_lb * lb * (1.0 - lb)
    small = _small_rows(d_nw, jnp.concatenate([dbias_a, dbias_b], axis=0), jnp.concatenate([d_l0, -d_l0], axis=0),
                        d_hgw, d_ps, d_fnw)
    d_meta = dz[PAD_ROWS:CHUNK].reshape(N_META, N_DEV, 128).transpose(1, 0, 2)
    d_pw_blocks = d_pw.reshape(4, N_DEV, 32, 256).transpose(1, 0, 2, 3)
    grads = jnp.stack([
        _pack_rows(dw_in[j], dw_dh[j * 128:(j + 1) * 128], dw_dp[j * 128:(j + 1) * 128], dw_out[j * 128:(j + 1) * 128],
                   d_pw_blocks[j], d_meta[j], small) for j in range(N_DEV)], axis=0)

    own_slots = (4 * (jnp.arange(4, dtype=jnp.int32) // 2) + 2 * (jnp.arange(4, dtype=jnp.int32) % 2) + ac).astype(jnp.int32)
    from_sibling = _exchange_sibling(grads)
    chip_part = _pair_sum(grads, from_sibling, own_slots)
    from_chips = _exchange_chips(chip_part)

    w_pack = _pack_rows(w_in[0], w_down_hg[0], w_down_pool[0], w_out[0], pool_w[0], meta_tokens,
                        _small_rows(norm_w, b_in, lb_logits, hg_norm_w, pool_scale, final_norm_w))
    m_pack = _pack_rows(m_w_in[0], m_w_down_hg[0], m_w_down_pool[0], m_w_out[0], m_pool_w[0], m_meta_tokens,
                        _small_rows(m_norm_w, m_b_in, m_lb_logits, m_hg_norm_w, m_pool_scale, m_final_norm_w))
    v_pack = _pack_rows(v_w_in[0], v_w_down_hg[0], v_w_down_pool[0], v_w_out[0], v_pool_w[0], v_meta_tokens,
                        _small_rows(v_norm_w, v_b_in, v_lb_logits, v_hg_norm_w, v_pool_scale, v_final_norm_w))
    g_pack, d_pack, nm_pack, nv_pack = _adamw(from_chips, w_pack, m_pack, v_pack)

    loss = lax.psum(loss_part[0, 0], ("x", "y", "c"))
    grad_x = dz[CHUNK:].reshape(1, seq, D_MODEL)
    return (loss, grad_x, *_unpack_rows(g_pack), *_unpack_rows(d_pack), *_unpack_rows(nm_pack), *_unpack_rows(nv_pack))
```

```python
import functools

import jax
import jax.numpy as jnp
from jax import lax
from jax.experimental import pallas as pl
from jax.experimental.pallas import tpu as pltpu

F32 = jnp.float32
BF16 = jnp.bfloat16

D_MODEL = 1024
N_META = 16
HEADS = 8
HEAD_DIM = 128
CHUNK = 64
SUB = 16
N_SUB = CHUNK // SUB
PAD_ROWS = CHUNK - N_META
POOL_WINDOWS = (2, 4, 8, 16)
POOL_GDIM = D_MODEL // len(POOL_WINDOWS)
HALO = 16
EPS = 1e-6
N_DEV = 8
N_COLBLK = 8
ADAM_LR, ADAM_B1, ADAM_B2, ADAM_EPS, ADAM_WD, ADAM_STEP = 0.001, 0.9, 0.999, 1e-08, 0.01, 10

VMEM_LIMIT = 56 * 1024 * 1024
MESH = pl.DeviceIdType.MESH
ANY = pl.BlockSpec(memory_space=pl.ANY)
HIGHEST = lax.Precision.HIGHEST
NEG_INF = float("-inf")

ROW_W_IN = 0
ROW_W_DOWN_HG = 1024
ROW_W_DOWN_POOL = 1152
ROW_W_OUT = 1280
ROW_POOL_W = 1408
ROW_META = 1440
ROWS_GATHER = 1456
MISC_POOL_W = 0
MISC_META = 32
MISC_B_IN = 48
MISC_NORM_W = 56
MISC_LB = 57
MISC_HG_NORM_W = 59
MISC_POOL_SCALE = 60
MISC_FINAL_NORM_W = 61
MISC_ROWS = 64


def _params(sem=None):
    return pltpu.CompilerParams(dimension_semantics=sem, vmem_limit_bytes=VMEM_LIMIT)


def _row_tile(n_rows, prefer):
    best = 16
    for t in range(16, prefer + 1, 16):
        if n_rows % t == 0:
            best = t
    return best


def _sigmoid_pair(x):
    e = jnp.exp(-jnp.abs(x))
    r = 1.0 / (1.0 + e)
    er = e * r
    pos = x >= 0
    return jnp.where(pos, r, er), jnp.where(pos, er, r)


def _dot(a, b):
    return jnp.dot(a.astype(BF16), b.astype(BF16), preferred_element_type=F32)


def _dot_nt(a, b):
    return lax.dot_general(a.astype(BF16), b.astype(BF16), (((1,), (1,)), ((), ())), preferred_element_type=F32)


def _dot_tn(a, b):
    return lax.dot_general(a.astype(BF16), b.astype(BF16), (((0,), (0,)), ((), ())), preferred_element_type=F32)


def _all_gather(block):
    def body(x_ref, out_ref, send_sems, recv_sems, local_sem):
        x, y, c = lax.axis_index("x"), lax.axis_index("y"), lax.axis_index("c")
        me, sibling = (x, y, c), (x, y, 1 - c)
        chips = [(1 - x, y), (x, 1 - y), (1 - x, 1 - y)]

        def slot(px, py, pc):
            return out_ref.at[4 * px + 2 * py + pc]

        def copy(k, blk, to, src=None):
            return pltpu.make_async_remote_copy(
                src_ref=slot(*blk) if src is None else src, dst_ref=slot(*blk),
                send_sem=send_sems.at[k], recv_sem=recv_sems.at[k], device_id=to, device_id_type=MESH)

        mine = pltpu.make_async_copy(x_ref, slot(*me), local_sem)
        mine.start()
        first = [copy(0, me, sibling, src=x_ref)]
        first += [copy(1 + j, me, (*chip, c), src=x_ref) for j, chip in enumerate(chips)]
        for cp in first:
            cp.start()
        passed = [copy(4 + j, (*chip, c), sibling) for j, chip in enumerate(chips)]
        for j, chip in enumerate(chips):
            copy(1 + j, (*chip, c), me).wait_recv()
            passed[j].start()
        copy(0, sibling, me).wait_recv()
        for j, chip in enumerate(chips):
            copy(4 + j, (*chip, 1 - c), me).wait_recv()
        for cp in first + passed:
            cp.wait_send()
        mine.wait()

    return pl.pallas_call(
        body, name="all_gather_weights",
        out_shape=jax.ShapeDtypeStruct((N_DEV,) + block.shape, block.dtype),
        in_specs=[ANY], out_specs=ANY,
        scratch_shapes=[pltpu.SemaphoreType.DMA((7,)), pltpu.SemaphoreType.DMA((7,)), pltpu.SemaphoreType.DMA(())],
    )(block)


def _owner_rows(ref, owner):
    return ref.at[:, pl.ds(owner * 128, 128), :]


def _exchange_sibling(a, b, m):
    def body(a_ref, b_ref, m_ref, a_out, b_out, m_out, send_sems, recv_sems):
        x, y, c = lax.axis_index("x"), lax.axis_index("y"), lax.axis_index("c")
        sibling = (x, y, 1 - c)
        copies = []
        for q in range(4):
            owner = 4 * (q // 2) + 2 * (q % 2) + (1 - c)
            pairs = ((a_ref.at[owner], a_out.at[q]), (_owner_rows(b_ref, owner), b_out.at[q]), (m_ref.at[owner], m_out.at[q]))
            for k, (src, dst) in enumerate(pairs):
                copies.append(pltpu.make_async_remote_copy(
                    src_ref=src, dst_ref=dst, send_sem=send_sems.at[3 * q + k], recv_sem=recv_sems.at[3 * q + k],
                    device_id=sibling, device_id_type=MESH))
        for cp in copies:
            cp.start()
        for cp in copies:
            cp.wait_recv()
        for cp in copies:
            cp.wait_send()

    return pl.pallas_call(
        body, name="reduce_exchange_sibling",
        out_shape=(jax.ShapeDtypeStruct((4, 1024, 1024), a.dtype), jax.ShapeDtypeStruct((4, 3, 128, 1024), b.dtype),
                   jax.ShapeDtypeStruct((4, MISC_ROWS, 1024), m.dtype)),
        in_specs=[ANY, ANY, ANY], out_specs=(ANY, ANY, ANY),
        scratch_shapes=[pltpu.SemaphoreType.DMA((12,)), pltpu.SemaphoreType.DMA((12,))],
    )(a, b, m)


def _pair_sum(a, b, m, a1, b1, m1, own_slots):
    def body(slots_ref, a_ref, b_ref, m_ref, a1_ref, b1_ref, m1_ref, a2_ref, b2_ref, m2_ref):
        a2_ref[0] = (a_ref[0].astype(F32) + a1_ref[0].astype(F32)).astype(a2_ref.dtype)
        b2_ref[0] = (b_ref[...].astype(F32) + b1_ref[0].astype(F32)).astype(b2_ref.dtype)
        m2_ref[0] = m_ref[0] + m1_ref[0]

    return pl.pallas_call(
        body, name="reduce_pair_sum",
        out_shape=(jax.ShapeDtypeStruct(a1.shape, a1.dtype), jax.ShapeDtypeStruct(b1.shape, b1.dtype),
                   jax.ShapeDtypeStruct(m1.shape, m1.dtype)),
        grid_spec=pltpu.PrefetchScalarGridSpec(
            num_scalar_prefetch=1, grid=(4,),
            in_specs=[pl.BlockSpec((1, 1024, 1024), lambda q, s: (s[q], 0, 0)),
                      pl.BlockSpec((3, 128, 1024), lambda q, s: (0, s[q], 0)),
                      pl.BlockSpec((1, MISC_ROWS, 1024), lambda q, s: (s[q], 0, 0)),
                      pl.BlockSpec((1, 1024, 1024), lambda q, s: (q, 0, 0)),
                      pl.BlockSpec((1, 3, 128, 1024), lambda q, s: (q, 0, 0, 0)),
                      pl.BlockSpec((1, MISC_ROWS, 1024), lambda q, s: (q, 0, 0))],
            out_specs=(pl.BlockSpec((1, 1024, 1024), lambda q, s: (q, 0, 0)),
                       pl.BlockSpec((1, 3, 128, 1024), lambda q, s: (q, 0, 0, 0)),
                       pl.BlockSpec((1, MISC_ROWS, 1024), lambda q, s: (q, 0, 0)))),
        compiler_params=_params(("arbitrary",)),
    )(own_slots, a, b, m, a1, b1, m1)


def _exchange_chips(a2, b2, m2):
    def body(a_ref, b_ref, m_ref, a_out, b_out, m_out, send_sems, recv_sems, local_sems):
        x, y, c = lax.axis_index("x"), lax.axis_index("y"), lax.axis_index("c")
        chips = [(1 - x, y), (x, 1 - y), (1 - x, 1 - y)]
        arrays = ((a_ref, a_out), (b_ref, b_out), (m_ref, m_out))
        me = 2 * x + y
        local = [pltpu.make_async_copy(src.at[me], dst.at[me], local_sems.at[k]) for k, (src, dst) in enumerate(arrays)]
        for cp in local:
            cp.start()

        def copy(j, k, dst_slot):
            qx, qy = chips[j]
            src, dst = arrays[k]
            return pltpu.make_async_remote_copy(
                src_ref=src.at[2 * qx + qy], dst_ref=dst.at[dst_slot], send_sem=send_sems.at[3 * j + k],
                recv_sem=recv_sems.at[3 * j + k], device_id=(qx, qy, c), device_id_type=MESH)

        sends = [copy(j, k, me) for j in range(3) for k in range(3)]
        for cp in sends:
            cp.start()
        for j, (qx, qy) in enumerate(chips):
            for k in range(3):
                copy(j, k, 2 * qx + qy).wait_recv()
        for cp in sends:
            cp.wait_send()
        for cp in local:
            cp.wait()

    return pl.pallas_call(
        body, name="reduce_exchange_chips",
        out_shape=tuple(jax.ShapeDtypeStruct(t.shape, t.dtype) for t in (a2, b2, m2)),
        in_specs=[ANY, ANY, ANY], out_specs=(ANY, ANY, ANY),
        scratch_shapes=[pltpu.SemaphoreType.DMA((9,)), pltpu.SemaphoreType.DMA((9,)), pltpu.SemaphoreType.DMA((3,))],
    )(a2, b2, m2)


def _adam_update(g, w, m, v):
    mn = ADAM_B1 * m + (1.0 - ADAM_B1) * g
    vn = ADAM_B2 * v + (1.0 - ADAM_B2) * (g * g)
    m_hat = mn / (1.0 - ADAM_B1 ** ADAM_STEP)
    v_hat = vn / (1.0 - ADAM_B2 ** ADAM_STEP)
    return -ADAM_LR * (m_hat / (jnp.sqrt(v_hat) + ADAM_EPS) + ADAM_WD * w), mn, vn


def _chip_sum(ref4):
    t = [ref4[s].astype(F32) for s in range(4)]
    return (t[0] + t[1]) + (t[2] + t[3])


def _finish(a3, b3, m3, big, rows3, small):
    n_steps = 4
    tb, tr3 = 1024 // n_steps, 128 // n_steps

    def body(*refs):
        it = iter(refs)
        a_ref, b_ref, m_ref = next(it), next(it), next(it)
        big_in = [next(it) for _ in range(3)]
        rows_in = [[next(it) for _ in range(3)] for _ in rows3]
        small_in = [[next(it) for _ in range(3)] for _ in small]
        big_out = [next(it) for _ in range(4)]
        rows_out = [[next(it) for _ in range(4)] for _ in rows3]
        small_out = [[next(it) for _ in range(4)] for _ in small]

        def apply(g, ins, outs):
            d, mn, vn = _adam_update(g, ins[0][...], ins[1][...], ins[2][...])
            for r, val in zip(outs, (g, d, mn, vn)):
                r[...] = val

        apply(_chip_sum(a_ref)[None], big_in, big_out)
        for k in range(len(rows3)):
            apply(_chip_sum(b_ref.at[:, k])[None], rows_in[k], rows_out[k])

        @pl.when(pl.program_id(0) == 0)
        def _():
            for (row0, lanes, ins), r_in, r_out in zip(small, small_in, small_out):
                n = ins[0].shape[0]
                g = (m_ref[0, row0:row0 + n, :lanes] + m_ref[1, row0:row0 + n, :lanes]) + (
                    m_ref[2, row0:row0 + n, :lanes] + m_ref[3, row0:row0 + n, :lanes])
                apply(g, r_in, r_out)

    whole = lambda shape: pl.BlockSpec(shape, lambda i: (0,) * len(shape))
    big_blk = pl.BlockSpec((1, tb, 1024), lambda i: (0, i, 0))
    rows_blk = pl.BlockSpec((1, tr3, 1024), lambda i: (0, i, 0))
    in_specs = [pl.BlockSpec((4, tb, 1024), lambda i: (0, i, 0)), pl.BlockSpec((4, 3, tr3, 1024), lambda i: (0, 0, i, 0)),
                whole(m3.shape)]
    in_specs += [big_blk] * 3 + [rows_blk] * (3 * len(rows3))
    out_specs = [big_blk] * 4 + [rows_blk] * (4 * len(rows3))
    out_shape = [jax.ShapeDtypeStruct(big[0].shape, F32)] * 4
    for w, _, _ in rows3:
        out_shape += [jax.ShapeDtypeStruct(w.shape, F32)] * 4
    args = [a3, b3, m3, *big]
    for t in rows3:
        args += list(t)
    for _, _, t in small:
        in_specs += [whole(t[0].shape)] * 3
        out_specs += [whole(t[0].shape)] * 4
        out_shape += [jax.ShapeDtypeStruct(t[0].shape, F32)] * 4
        args += list(t)
    outs = pl.pallas_call(
        body, name="reduce_sum_adamw", out_shape=tuple(out_shape), grid=(n_steps,),
        in_specs=in_specs, out_specs=tuple(out_specs), compiler_params=_params(("arbitrary",)),
    )(*args)
    return [tuple(outs[4 * k:4 * k + 4]) for k in range(len(outs) // 4)]


def _in_projection(zp, norm_w, w_blocks, b_blocks):
    n_rows = zp.shape[0]
    tr = _row_tile(n_rows, 320)

    def body(z_ref, nw_ref, w_hbm, b_ref, p_ref, h_ref, w_vmem, h_vmem, sem):
        i, j = pl.program_id(0), pl.program_id(1)

        @pl.when((i == 0) & (j == 0))
        def _():
            cp = pltpu.make_async_copy(w_hbm, w_vmem, sem)
            cp.start()
            cp.wait()

        @pl.when(j == 0)
        def _():
            z = z_ref[...]
            r = lax.rsqrt(jnp.mean(z * z, axis=-1, keepdims=True) + EPS)
            h = (z * r * nw_ref[...]).astype(BF16)
            h_vmem[...] = h
            h_ref[...] = h

        p_ref[0] = jnp.dot(h_vmem[...], w_vmem[j], preferred_element_type=F32) + b_ref[0]

    return pl.pallas_call(
        body, name="in_projection",
        out_shape=(jax.ShapeDtypeStruct((N_COLBLK, n_rows, 1024), F32), jax.ShapeDtypeStruct((n_rows, D_MODEL), BF16)),
        grid=(n_rows // tr, N_COLBLK),
        in_specs=[pl.BlockSpec((tr, D_MODEL), lambda i, j: (i, 0)),
                  pl.BlockSpec((1, D_MODEL), lambda i, j: (0, 0)),
                  ANY,
                  pl.BlockSpec((1, 1, 1024), lambda i, j: (j, 0, 0))],
        out_specs=(pl.BlockSpec((1, tr, 1024), lambda i, j: (j, i, 0)),
                   pl.BlockSpec((tr, D_MODEL), lambda i, j: (i, 0))),
        scratch_shapes=[pltpu.VMEM((N_COLBLK, D_MODEL, 1024), BF16), pltpu.VMEM((tr, D_MODEL), BF16),
                        pltpu.SemaphoreType.DMA(())],
        compiler_params=_params(("arbitrary", "arbitrary")),
    )(zp, norm_w, w_blocks, b_blocks)


def _lower_bound(lb_ref):
    l0, l1 = lb_ref[0:1, :], lb_ref[1:2, :]
    _, lb = _sigmoid_pair(l1 - l0)
    return lb


def _chunk_gates(fz, lb, valid):
    sig, nsig = _sigmoid_pair(fz)
    f = lb + (1.0 - lb) * sig
    g = jnp.where(valid, jnp.log(f), 0.0)
    k = jnp.where(valid, (1.0 - lb) * nsig, 0.0)
    return sig, nsig, f, g, k


def _tri(n, upper=False):
    r = lax.broadcasted_iota(jnp.int32, (n, n), 0)
    c = lax.broadcasted_iota(jnp.int32, (n, n), 1)
    return jnp.where((r <= c) if upper else (r >= c), 1.0, 0.0).astype(F32)


def _intra_scores(q_ref, k_ref, b_ref, a_ref, col0):
    cols = pl.ds(col0, HEAD_DIM)
    rows_c = lax.broadcasted_iota(jnp.int32, (CHUNK, 1), 0)
    rows_s = lax.broadcasted_iota(jnp.int32, (SUB, 1), 0)
    lanes_c = lax.broadcasted_iota(jnp.int32, (1, CHUNK), 1)
    b_all = b_ref[:, cols]
    k_all = k_ref[:, cols]
    for i in range(N_SUB):
        lo = i * SUB
        qi = q_ref[lo:lo + SUB, cols]
        bi = b_ref[lo:lo + SUB, cols]
        if i == 0:
            acc = jnp.zeros((SUB, CHUNK), F32)
        else:
            ref_i = b_ref[lo:lo + 1, cols]
            qt = qi * jnp.exp(bi - ref_i)
            kt = k_all * jnp.exp(jnp.where(rows_c < lo, ref_i - b_all, NEG_INF))
            acc = _dot_nt(qt, kt)
        for s in range(SUB):
            b_s = b_ref[lo + s:lo + s + 1, cols]
            k_s = k_ref[lo + s:lo + s + 1, cols]
            w = jnp.exp(jnp.where(rows_s >= s, bi - b_s, NEG_INF))
            col = jnp.sum(qi * k_s * w, axis=-1, keepdims=True)
            acc = acc + jnp.where(lanes_c == lo + s, col, 0.0)
        a_ref[lo:lo + SUB, :] = acc


def _hgrn_forward(p, lb_logits):
    n_rows = p.shape[1]
    n_chunks = n_rows // CHUNK
    width = HEADS * HEAD_DIM

    def body(q_ref, fz_ref, v_ref, lb_ref, o_ref, st_out_ref, state, k_vmem, b_vmem, a_vmem):
        n = pl.program_id(0)

        @pl.when(n == 0)
        def _():
            state[...] = jnp.zeros_like(state)

        rows = n * CHUNK + lax.broadcasted_iota(jnp.int32, (CHUNK, 1), 0)
        valid = rows >= PAD_ROWS
        lb = _lower_bound(lb_ref)
        _, _, _, g, k = _chunk_gates(fz_ref[0], lb, valid)
        b = jnp.dot(_tri(CHUNK), g, precision=HIGHEST, preferred_element_type=F32)
        k_vmem[...] = k
        b_vmem[...] = b
        q_view = q_ref.at[0]
        for h in range(HEADS):
            cols = pl.ds(h * HEAD_DIM, HEAD_DIM)
            st = state[h]
            st_out_ref[0, h] = st
            bh = b_vmem[:, cols]
            kh = k_vmem[:, cols]
            vh = jnp.where(valid, v_ref[0, :, cols], 0.0)
            qe = q_ref[0, :, cols] * jnp.exp(bh)
            _intra_scores(q_view, k_vmem, b_vmem, a_vmem, h * HEAD_DIM)
            o_ref[:, cols] = _dot_nt(qe, st) + _dot(a_vmem[...], vh)
            b_last = b_vmem[CHUNK - 1:CHUNK, cols]
            kd = kh * jnp.exp(b_last - bh)
            state[h] = st * jnp.exp(b_last) + _dot_tn(vh, kd)

    blk = lambda c: pl.BlockSpec((1, CHUNK, width), lambda n, c=c: (c, n, 0))
    return pl.pallas_call(
        body, name="hgrn_forward",
        out_shape=(jax.ShapeDtypeStruct((n_rows, width), F32),
                   jax.ShapeDtypeStruct((n_chunks, HEADS, HEAD_DIM, HEAD_DIM), F32)),
        grid=(n_chunks,),
        in_specs=[blk(0), blk(1), blk(2), pl.BlockSpec((2, width), lambda n: (0, 0))],
        out_specs=(pl.BlockSpec((CHUNK, width), lambda n: (n, 0)),
                   pl.BlockSpec((1, HEADS, HEAD_DIM, HEAD_DIM), lambda n: (n, 0, 0, 0))),
        scratch_shapes=[pltpu.VMEM((HEADS, HEAD_DIM, HEAD_DIM), F32), pltpu.VMEM((CHUNK, width), F32),
                        pltpu.VMEM((CHUNK, width), F32), pltpu.VMEM((CHUNK, CHUNK), F32)],
        compiler_params=_params(("arbitrary",)),
    )(p, p, p, lb_logits)


def _hgrn_backward(p, lb_logits, states, d_o):
    n_rows = p.shape[1]
    n_chunks = n_rows // CHUNK
    width = HEADS * HEAD_DIM

    def body(q_ref, fz_ref, v_ref, lb_ref, st_ref, do_ref, dp_ref, dbias_ref, dlb_ref,
             dstate, k_vmem, b_vmem, a_vmem, dqi_vmem, dki_vmem):
        step = pl.program_id(0)
        n = n_chunks - 1 - step

        @pl.when(step == 0)
        def _():
            dstate[...] = jnp.zeros_like(dstate)
            dbias_ref[...] = jnp.zeros_like(dbias_ref)
            dlb_ref[...] = jnp.zeros_like(dlb_ref)

        rows = n * CHUNK + lax.broadcasted_iota(jnp.int32, (CHUNK, 1), 0)
        valid = rows >= PAD_ROWS
        lb = _lower_bound(lb_ref)
        sig, nsig, f, g, k = _chunk_gates(fz_ref[0], lb, valid)
        b = jnp.dot(_tri(CHUNK), g, precision=HIGHEST, preferred_element_type=F32)
        k_vmem[...] = k
        b_vmem[...] = b
        q_view = q_ref.at[0]
        rows_c = lax.broadcasted_iota(jnp.int32, (CHUNK, 1), 0)
        rows_s = lax.broadcasted_iota(jnp.int32, (SUB, 1), 0)
        lanes_c = lax.broadcasted_iota(jnp.int32, (1, CHUNK), 1)
        causal = lax.broadcasted_iota(jnp.int32, (CHUNK, CHUNK), 0) >= lax.broadcasted_iota(jnp.int32, (CHUNK, CHUNK), 1)
        tri_up = _tri(CHUNK, upper=True)
        for h in range(HEADS):
            cols = pl.ds(h * HEAD_DIM, HEAD_DIM)
            st = st_ref[0, h]
            dst = dstate[h]
            qh = q_ref[0, :, cols]
            bh = b_vmem[:, cols]
            kh = k_vmem[:, cols]
            vh = jnp.where(valid, v_ref[0, :, cols], 0.0)
            doh = do_ref[:, cols]
            eb = jnp.exp(bh)
            qe = qh * eb
            b_last = b_vmem[CHUNK - 1:CHUNK, cols]
            e_last = jnp.exp(b_last)
            decay_k = jnp.exp(b_last - bh)
            kd = kh * decay_k
            _intra_scores(q_view, k_vmem, b_vmem, a_vmem, h * HEAD_DIM)
            a = a_vmem[...]
            dqe = _dot(doh, st)
            da = jnp.where(causal, _dot_nt(doh, vh), 0.0)
            dv = _dot_tn(a, doh) + _dot_nt(kd, dst)
            dkd = _dot(vh, dst)
            dstate[h] = dst * e_last + _dot_tn(doh, qe)
            db_last = (jnp.sum(dst * st, axis=0, keepdims=True) * e_last
                       + jnp.sum(dkd * kd, axis=0, keepdims=True))
            dki_vmem[...] = jnp.zeros_like(dki_vmem)
            for i in range(N_SUB):
                lo = i * SUB
                qi = q_ref[0, lo:lo + SUB, cols]
                bi = b_vmem[lo:lo + SUB, cols]
                da_i = da[lo:lo + SUB, :]
                if i == 0:
                    dq_i = jnp.zeros((SUB, HEAD_DIM), F32)
                else:
                    ref_i = b_vmem[lo:lo + 1, cols]
                    eq = jnp.exp(bi - ref_i)
                    ek = jnp.exp(jnp.where(rows_c < lo, ref_i - bh, NEG_INF))
                    qt = qi * eq
                    kt = kh * ek
                    dq_i = _dot(da_i, kt) * eq
                    dki_vmem[...] += _dot_tn(da_i, qt) * ek
                for s in range(SUB):
                    b_s = b_vmem[lo + s:lo + s + 1, cols]
                    k_s = k_vmem[lo + s:lo + s + 1, cols]
                    w = jnp.exp(jnp.where(rows_s >= s, bi - b_s, NEG_INF))
                    da_col = jnp.sum(jnp.where(lanes_c == lo + s, da_i, 0.0), axis=-1, keepdims=True)
                    gw = da_col * w
                    dq_i = dq_i + gw * k_s
                    dki_vmem[lo + s:lo + s + 1, :] += jnp.sum(gw * qi, axis=0, keepdims=True)
                dqi_vmem[lo:lo + SUB, :] = dq_i
            dq_intra = dqi_vmem[...]
            dk_intra = dki_vmem[...]
            dq = dqe * eb + dq_intra
            dk = dkd * decay_k + dk_intra
            db = dqe * qe - dkd * kd + qh * dq_intra - kh * dk_intra
            db = db + jnp.where(rows_c == CHUNK - 1, db_last, 0.0)
            dg = jnp.dot(tri_up, db, precision=HIGHEST, preferred_element_type=F32)
            fh = f[:, h * HEAD_DIM:(h + 1) * HEAD_DIM]
            sh = sig[:, h * HEAD_DIM:(h + 1) * HEAD_DIM]
            nh = nsig[:, h * HEAD_DIM:(h + 1) * HEAD_DIM]
            lbh = lb[:, h * HEAD_DIM:(h + 1) * HEAD_DIM]
            df = jnp.where(valid, dg / fh - dk, 0.0)
            dfz = df * (1.0 - lbh) * sh * nh
            dq = jnp.where(valid, dq, 0.0)
            dv = jnp.where(valid, dv, 0.0)
            dlb_ref[:, cols] += jnp.sum(df * nh, axis=0, keepdims=True)
            dp_ref[0, :, cols] = dq.astype(BF16)
            dp_ref[1, :, cols] = dfz.astype(BF16)
            dp_ref[2, :, cols] = dv.astype(BF16)
            dbias_ref[0, :, cols] += jnp.sum(dq, axis=0, keepdims=True)
            dbias_ref[1, :, cols] += jnp.sum(dfz, axis=0, keepdims=True)
            dbias_ref[2, :, cols] += jnp.sum(dv, axis=0, keepdims=True)

    rev = lambda s: n_chunks - 1 - s
    blk = lambda c: pl.BlockSpec((1, CHUNK, width), lambda s, c=c: (c, rev(s), 0))
    return pl.pallas_call(
        body, name="hgrn_backward",
        out_shape=(jax.ShapeDtypeStruct((3, n_rows, width), BF16),
                   jax.ShapeDtypeStruct((3, 1, width), F32),
                   jax.ShapeDtypeStruct((1, width), F32)),
        grid=(n_chunks,),
        in_specs=[blk(0), blk(1), blk(2), pl.BlockSpec((2, width), lambda s: (0, 0)),
                  pl.BlockSpec((1, HEADS, HEAD_DIM, HEAD_DIM), lambda s: (rev(s), 0, 0, 0)),
                  pl.BlockSpec((CHUNK, width), lambda s: (rev(s), 0))],
        out_specs=(pl.BlockSpec((3, CHUNK, width), lambda s: (0, rev(s), 0)),
                   pl.BlockSpec((3, 1, width), lambda s: (0, 0, 0)),
                   pl.BlockSpec((1, width), lambda s: (0, 0))),
        scratch_shapes=[pltpu.VMEM((HEADS, HEAD_DIM, HEAD_DIM), F32), pltpu.VMEM((CHUNK, width), F32),
                        pltpu.VMEM((CHUNK, width), F32), pltpu.VMEM((CHUNK, CHUNK), F32),
                        pltpu.VMEM((CHUNK, HEAD_DIM), F32), pltpu.VMEM((CHUNK, HEAD_DIM), F32)],
        compiler_params=_params(("arbitrary",)),
    )(p, p, p, lb_logits, states, d_o)


def _silu_and_grad(x):
    s, ns = _sigmoid_pair(x)
    return x * s, s * (1.0 + x * ns)


def _tail(p, o, zp, tgt, hg_norm_w, pool_w, pool_scale, w_down_hg, w_down_pool, w_out, final_norm_w):
    n_rows = zp.shape[0]
    tr = _row_tile(n_rows, 160)
    nt = n_rows // tr
    ext = tr + HALO
    n_groups = len(POOL_WINDOWS)

    def body(o_ref, ghg_ref, u_ref, gpool_ref, mhg_ref, mpool_ref, uhalo_ref, z_ref, tgt_ref,
             hgw_ref, pw_ref, ps_ref, wdh_ref, wdp_ref, wout_ref, fnw_ref,
             do_ref, dp_ref, dz2_ref, lhs_ref, rhs_ref,
             dbias_ref, dhgw_ref, dpw_ref, dps_ref, dfnw_ref, loss_ref, halo_vmem):
        step = pl.program_id(0)
        ti = nt - 1 - step

        @pl.when(step == 0)
        def _():
            halo_vmem[...] = jnp.zeros_like(halo_vmem)
            for r in (dbias_ref, dhgw_ref, dpw_ref, dps_ref, dfnw_ref, loss_ref):
                r[...] = jnp.zeros_like(r)

        rows = ti * tr + lax.broadcasted_iota(jnp.int32, (tr, 1), 0)
        valid = rows >= PAD_ROWS
        in_loss = rows >= CHUNK
        count_pos = jnp.maximum(rows - PAD_ROWS + 1, 1).astype(F32)

        o = o_ref[...]
        hgw = hgw_ref[...]
        inv_o, on_parts = [], []
        for h in range(HEADS):
            oh = o[:, h * HEAD_DIM:(h + 1) * HEAD_DIM]
            r = lax.rsqrt(jnp.mean(oh * oh, axis=-1, keepdims=True) + EPS)
            inv_o.append(r)
            on_parts.append(oh * r)
        o_hat = jnp.concatenate(on_parts, axis=1)
        o_n = o_hat * hgw
        g_hg = ghg_ref[0]
        silu_hg, dsilu_hg = _silu_and_grad(g_hg)
        a_hg = o_n * silu_hg
        y_hg = _dot(a_hg, wdh_ref[...])

        u = jnp.where(valid, u_ref[0], 0.0)
        u_prev = jnp.where(ti > 0, uhalo_ref[0], 0.0)
        u_ext = jnp.concatenate([u_prev, u], axis=0)
        pooled_parts, mixed_parts, inv_cnt = [], [], []
        for gi, win in enumerate(POOL_WINDOWS):
            lanes = slice(gi * POOL_GDIM, (gi + 1) * POOL_GDIM)
            s = u_ext[:, lanes]
            shift = 1
            while shift < win:
                s = s + pltpu.roll(s, shift, 0)
                shift *= 2
            ic = 1.0 / jnp.minimum(count_pos, float(win))
            inv_cnt.append(ic)
            pooled = s[HALO:] * ic - u[:, lanes]
            pooled_parts.append(pooled)
            mixed_parts.append(_dot(pooled, pw_ref[gi]))
        mixed = jnp.concatenate(mixed_parts, axis=1)
        ps = ps_ref[...]
        g_pool = gpool_ref[0]
        silu_pool, dsilu_pool = _silu_and_grad(g_pool)
        a_pool = mixed * ps * silu_pool
        y_pool = _dot(a_pool, wdp_ref[...])

        m_hg, m_pool = mhg_ref[0], mpool_ref[0]
        s_hg, ns_hg = _sigmoid_pair(m_hg)
        s_pool, ns_pool = _sigmoid_pair(m_pool)
        merged = s_hg * y_hg + s_pool * y_pool
        z2 = z_ref[...] + _dot(merged, wout_ref[...])
        r2 = lax.rsqrt(jnp.mean(z2 * z2, axis=-1, keepdims=True) + EPS)
        n2 = z2 * r2
        fnw = fnw_ref[...]
        err = jnp.where(in_loss, n2 * fnw - tgt_ref[...], 0.0)
        loss_ref[...] += jnp.sum(jnp.sum(err * err, axis=0, keepdims=True), axis=1, keepdims=True) * (0.5 / D_MODEL)
        dy = err * (1.0 / D_MODEL)

        dfnw_ref[...] += jnp.sum(dy * n2, axis=0, keepdims=True)
        gy = dy * fnw
        dz2 = r2 * (gy - n2 * jnp.mean(gy * n2, axis=-1, keepdims=True))
        dmerged = _dot_nt(dz2, wout_ref[...])
        dy_hg = s_hg * dmerged
        dy_pool = s_pool * dmerged
        dm_hg = dmerged * y_hg * s_hg * ns_hg
        dm_pool = dmerged * y_pool * s_pool * ns_pool
        da_hg = _dot_nt(dy_hg, wdh_ref[...])
        da_pool = _dot_nt(dy_pool, wdp_ref[...])

        d_on = da_hg * silu_hg
        dg_hg = da_hg * o_n * dsilu_hg
        dhgw_ref[...] += jnp.sum(d_on * o_hat, axis=0, keepdims=True)
        gyo = d_on * hgw
        do_parts = []
        for h in range(HEADS):
            lanes = slice(h * HEAD_DIM, (h + 1) * HEAD_DIM)
            gh, nh = gyo[:, lanes], o_hat[:, lanes]
            do_parts.append(inv_o[h] * (gh - nh * jnp.mean(gh * nh, axis=-1, keepdims=True)))
        do_ref[...] = jnp.concatenate(do_parts, axis=1)

        dmixed = da_pool * ps * silu_pool
        dps_ref[...] += jnp.sum(da_pool * mixed * silu_pool, axis=0, keepdims=True)
        dg_pool = da_pool * mixed * ps * dsilu_pool
        du_parts = []
        for gi, win in enumerate(POOL_WINDOWS):
            lanes = slice(gi * POOL_GDIM, (gi + 1) * POOL_GDIM)
            dmx = dmixed[:, lanes]
            dpooled = _dot_nt(dmx, pw_ref[gi])
            dpw_ref[gi] += _dot_tn(pooled_parts[gi], dmx)
            dpt = dpooled * inv_cnt[gi]
            s = jnp.concatenate([dpt, halo_vmem[:, lanes]], axis=0)
            shift = 1
            while shift < win:
                s = s + pltpu.roll(s, ext - shift, 0)
                shift *= 2
            du_parts.append(s[:tr] - dpooled)
            halo_vmem[:, lanes] = dpt[:HALO]
        du = jnp.where(valid, jnp.concatenate(du_parts, axis=1), 0.0)

        for c, val in enumerate((dg_hg, du, dg_pool, dm_hg, dm_pool)):
            dp_ref[c] = val.astype(BF16)
            dbias_ref[c] += jnp.sum(val, axis=0, keepdims=True)
        dz2_ref[...] = dz2
        for c, (lhs, rhs) in enumerate(((merged, dz2), (a_hg, dy_hg), (a_pool, dy_pool))):
            lhs_ref[c] = lhs.astype(BF16)
            rhs_ref[c] = rhs.astype(BF16)

    rev = lambda s: nt - 1 - s
    rowblk = pl.BlockSpec((tr, D_MODEL), lambda s: (rev(s), 0))
    pblk = lambda c: pl.BlockSpec((1, tr, 1024), lambda s, c=c: (c, rev(s), 0))
    halo_blk = pl.BlockSpec((1, HALO, 1024), lambda s: (4, jnp.maximum(rev(s) * (tr // HALO) - 1, 0), 0))
    full = lambda shape: pl.BlockSpec(shape, lambda s: (0,) * len(shape))
    vec = full((1, D_MODEL))
    mat = full((D_MODEL, D_MODEL))
    act3 = jax.ShapeDtypeStruct((3, n_rows, D_MODEL), BF16)
    act3_blk = pl.BlockSpec((3, tr, D_MODEL), lambda s: (0, rev(s), 0))
    return pl.pallas_call(
        body, name="tail_forward_backward",
        out_shape=(jax.ShapeDtypeStruct((n_rows, D_MODEL), F32),
                   jax.ShapeDtypeStruct((5, n_rows, 1024), BF16),
                   jax.ShapeDtypeStruct((n_rows, D_MODEL), F32),
                   act3, act3,
                   jax.ShapeDtypeStruct((5, 1, 1024), F32),
                   jax.ShapeDtypeStruct((1, D_MODEL), F32),
                   jax.ShapeDtypeStruct((n_groups, POOL_GDIM, POOL_GDIM), F32),
                   jax.ShapeDtypeStruct((1, D_MODEL), F32),
                   jax.ShapeDtypeStruct((1, D_MODEL), F32),
                   jax.ShapeDtypeStruct((1, 1), F32)),
        grid=(nt,),
        in_specs=[rowblk, pblk(3), pblk(4), pblk(5), pblk(6), pblk(7), halo_blk, rowblk, rowblk,
                  vec, full((n_groups, POOL_GDIM, POOL_GDIM)), vec, mat, mat, mat, vec],
        out_specs=(rowblk, pl.BlockSpec((5, tr, 1024), lambda s: (0, rev(s), 0)), rowblk,
                   act3_blk, act3_blk,
                   full((5, 1, 1024)), vec, full((n_groups, POOL_GDIM, POOL_GDIM)), vec, vec, full((1, 1))),
        scratch_shapes=[pltpu.VMEM((HALO, D_MODEL), F32)],
        compiler_params=_params(("arbitrary",)),
    )(o, p, p, p, p, p, p, zp, tgt, hg_norm_w, pool_w, pool_scale, w_down_hg, w_down_pool, w_out, final_norm_w)


def _in_projection_backward(dp_a, dp_b, w_blocks, zp, dz2, norm_w):
    n_rows = zp.shape[0]
    tr = _row_tile(n_rows, 320)
    na, nb = dp_a.shape[0], dp_b.shape[0]

    def body(dpa_ref, dpb_ref, w_hbm, z_ref, dz2_ref, nw_ref, dz_ref, dnw_ref, w_vmem, sem):
        i = pl.program_id(0)

        @pl.when(i == 0)
        def _():
            cp = pltpu.make_async_copy(w_hbm, w_vmem, sem)
            cp.start()
            cp.wait()
            dnw_ref[...] = jnp.zeros_like(dnw_ref)

        dh = jnp.zeros((tr, D_MODEL), F32)
        for j in range(na):
            dh = dh + _dot_nt(dpa_ref[j], w_vmem[j])
        for j in range(nb):
            dh = dh + _dot_nt(dpb_ref[j], w_vmem[na + j])
        z = z_ref[...]
        r = lax.rsqrt(jnp.mean(z * z, axis=-1, keepdims=True) + EPS)
        n1 = z * r
        dnw_ref[...] += jnp.sum(dh * n1, axis=0, keepdims=True)
        gh = dh * nw_ref[...]
        dz_ref[...] = dz2_ref[...] + r * (gh - n1 * jnp.mean(gh * n1, axis=-1, keepdims=True))

    rowblk = pl.BlockSpec((tr, D_MODEL), lambda i: (i, 0))
    vec = pl.BlockSpec((1, D_MODEL), lambda i: (0, 0))
    return pl.pallas_call(
        body, name="in_projection_backward",
        out_shape=(jax.ShapeDtypeStruct((n_rows, D_MODEL), F32), jax.ShapeDtypeStruct((1, D_MODEL), F32)),
        grid=(n_rows // tr,),
        in_specs=[pl.BlockSpec((na, tr, 1024), lambda i: (0, i, 0)), pl.BlockSpec((nb, tr, 1024), lambda i: (0, i, 0)),
                  ANY, rowblk, rowblk, vec],
        out_specs=(rowblk, vec),
        scratch_shapes=[pltpu.VMEM((N_COLBLK, D_MODEL, 1024), BF16), pltpu.SemaphoreType.DMA(())],
        compiler_params=_params(("arbitrary",)),
    )(dp_a, dp_b, w_blocks, zp, dz2, norm_w)


def _weight_grad(xs, ys_parts, name):
    shared = xs.ndim == 2
    n_rows, m = xs.shape[-2:]
    n = ys_parts[0].shape[-1]
    offsets = [sum(y.shape[0] for y in ys_parts[:i]) for i in range(len(ys_parts))]
    nb = offsets[-1] + ys_parts[-1].shape[0]
    tk = _row_tile(n_rows, 832)
    n_k = n_rows // tk

    def body(x_ref, *rest):
        y_refs, o_ref, acc = rest[:len(ys_parts)], rest[-2], rest[-1]
        j, k = pl.program_id(0), pl.program_id(1)

        @pl.when(k == 0)
        def _():
            acc[...] = jnp.zeros_like(acc)

        x = x_ref[...] if shared else x_ref[0]
        for off, y, y_ref in zip(offsets, ys_parts, y_refs):
            @pl.when((j >= off) & (j < off + y.shape[0]))
            def _(y_ref=y_ref):
                acc[...] += _dot_tn(x, y_ref[0])

        @pl.when(k == n_k - 1)
        def _():
            o_ref[0] = acc[...].astype(o_ref.dtype)

    x_spec = pl.BlockSpec((tk, m), lambda j, k: (k, 0)) if shared else pl.BlockSpec((1, tk, m), lambda j, k: (j, k, 0))
    y_specs = [pl.BlockSpec((1, tk, n), lambda j, k, off=off, cnt=y.shape[0]: (jnp.clip(j - off, 0, cnt - 1), k, 0))
               for off, y in zip(offsets, ys_parts)]
    return pl.pallas_call(
        body, name=name,
        out_shape=jax.ShapeDtypeStruct((nb, m, n), BF16),
        grid=(nb, n_k),
        in_specs=[x_spec] + y_specs,
        out_specs=pl.BlockSpec((1, m, n), lambda j, k: (j, 0, 0)),
        scratch_shapes=[pltpu.VMEM((m, n), F32)],
        compiler_params=_params(("arbitrary", "arbitrary")),
    )(xs, *ys_parts)


def kernel(x, meta_tokens, norm_w, w_in, b_in, lb_logits, hg_norm_w, pool_w, pool_scale, w_down_hg, w_down_pool, w_out, final_norm_w, loss_target, m_meta_tokens, m_norm_w, m_w_in, m_b_in, m_lb_logits, m_hg_norm_w, m_pool_w, m_pool_scale, m_w_down_hg, m_w_down_pool, m_w_out, m_final_norm_w, v_meta_tokens, v_norm_w, v_w_in, v_b_in, v_lb_logits, v_hg_norm_w, v_pool_w, v_pool_scale, v_w_down_hg, v_w_down_pool, v_w_out, v_final_norm_w):
    seq = x.shape[1]
    ac = lax.axis_index("c")

    meta_bits = lax.bitcast_convert_type(meta_tokens, BF16).reshape(4, 1024)
    wrows = jnp.concatenate([w_in[0].astype(BF16), w_down_hg[0].astype(BF16), w_down_pool[0].astype(BF16),
                             w_out[0].astype(BF16), pool_w[0].astype(BF16).reshape(32, 1024),
                             jnp.pad(meta_bits, ((0, 12), (0, 0)))], axis=0)
    gathered = _all_gather(wrows)
    w_blocks = gathered[:, ROW_W_IN:ROW_W_IN + 1024]
    wdh = gathered[:, ROW_W_DOWN_HG:ROW_W_DOWN_HG + 128].reshape(1024, 1024)
    wdp = gathered[:, ROW_W_DOWN_POOL:ROW_W_DOWN_POOL + 128].reshape(1024, 1024)
    wout = gathered[:, ROW_W_OUT:ROW_W_OUT + 128].reshape(1024, 1024)
    pw = gathered[:, ROW_POOL_W:ROW_POOL_W + 32].reshape(N_DEV, 4, 32, 256).transpose(1, 0, 2, 3).reshape(4, 256, 256)
    meta_full = lax.bitcast_convert_type(gathered[:, ROW_META:ROW_META + 4].reshape(N_DEV, N_META, 128, 2), F32)
    meta_full = meta_full.transpose(1, 0, 2).reshape(N_META, D_MODEL)

    zp = jnp.concatenate([jnp.zeros((PAD_ROWS, D_MODEL), F32), meta_full, x[0]], axis=0)
    tgt = jnp.concatenate([jnp.zeros((CHUNK, D_MODEL), F32), loss_target[0]], axis=0)
    p, h = _in_projection(zp, norm_w, w_blocks, b_in.reshape(N_COLBLK, 1, 1024))
    o, states = _hgrn_forward(p, lb_logits)
    (d_o, dp_b, dz2, grad_lhs, grad_rhs, dbias_b, d_hgw, d_pw, d_ps, d_fnw, loss_part) = _tail(
        p, o, zp, tgt, hg_norm_w, pw, pool_scale, wdh, wdp, wout, final_norm_w.reshape(1, D_MODEL))
    dp_a, dbias_a, d_lb = _hgrn_backward(p, lb_logits, states, d_o)
    dz, d_nw = _in_projection_backward(dp_a, dp_b, w_blocks, zp, dz2, norm_w)
    g_in = _weight_grad(h, [dp_a, dp_b], "weight_grad_in")
    g_rows = _weight_grad(grad_lhs, [grad_rhs], "weight_grad_rows")

    lb = jax.nn.sigmoid(lb_logits[0:1] - lb_logits[1:2])
    d_l0 = d_lb * lb * (1.0 - lb)
    replicated = jnp.concatenate([dbias_a.reshape(3, 1024), dbias_b.reshape(5, 1024), d_nw, d_l0, -d_l0, d_hgw, d_ps, d_fnw,
                                  jnp.zeros((MISC_ROWS - MISC_FINAL_NORM_W - 1, 1024), F32)], axis=0)
    d_meta = dz[PAD_ROWS:CHUNK].reshape(N_META, N_DEV, 128).transpose(1, 0, 2)
    d_pw_blocks = d_pw.reshape(4, N_DEV, 32, 256).transpose(1, 0, 2, 3).reshape(N_DEV, 32, 1024)
    g_misc = jnp.concatenate([d_pw_blocks, jnp.pad(d_meta, ((0, 0), (0, 0), (0, 1024 - 128))),
                              jnp.broadcast_to(replicated[None], (N_DEV, 16, 1024))], axis=1)

    own_slots = (4 * (jnp.arange(4, dtype=jnp.int32) // 2) + 2 * (jnp.arange(4, dtype=jnp.int32) % 2) + ac).astype(jnp.int32)
    from_sibling = _exchange_sibling(g_in, g_rows, g_misc)
    chip_part = _pair_sum(g_in, g_rows, g_misc, *from_sibling, own_slots)
    from_chips = _exchange_chips(*chip_part)

    as_rows = lambda t, n: t.reshape(n, 1024)
    small = [(MISC_POOL_W, 1024, tuple(as_rows(t, 32) for t in (pool_w, m_pool_w, v_pool_w))),
             (MISC_META, 128, (meta_tokens, m_meta_tokens, v_meta_tokens)),
             (MISC_B_IN, 1024, tuple(as_rows(t, 8) for t in (b_in, m_b_in, v_b_in))),
             (MISC_NORM_W, 1024, (norm_w, m_norm_w, v_norm_w)),
             (MISC_LB, 1024, (lb_logits, m_lb_logits, v_lb_logits)),
             (MISC_HG_NORM_W, 1024, (hg_norm_w, m_hg_norm_w, v_hg_norm_w)),
             (MISC_POOL_SCALE, 1024, (pool_scale, m_pool_scale, v_pool_scale)),
             (MISC_FINAL_NORM_W, 1024, tuple(as_rows(t, 1) for t in (final_norm_w, m_final_norm_w, v_final_norm_w)))]
    res = _finish(*from_chips, (w_in, m_w_in, v_w_in),
                  [(w_out, m_w_out, v_w_out), (w_down_hg, m_w_down_hg, v_w_down_hg), (w_down_pool, m_w_down_pool, v_w_down_pool)],
                  small)
    r_w_in, r_w_out, r_wdh, r_wdp, r_pw, r_meta, r_b_in, r_nw, r_lb, r_hgw, r_ps, r_fnw = res
    loss = lax.psum(loss_part[0, 0], ("x", "y", "c"))
    grad_x = dz[CHUNK:].reshape(1, seq, D_MODEL)
    per_kind = [(r_meta[k], r_nw[k], r_w_in[k], r_b_in[k].reshape(1, 8192), r_lb[k], r_hgw[k], r_pw[k].reshape(1, 4, 32, 256),
                 r_ps[k], r_wdh[k], r_wdp[k], r_w_out[k], r_fnw[k].reshape(1024)) for k in range(4)]
    return (loss, grad_x, *per_kind[0], *per_kind[1], *per_kind[2], *per_kind[3])
```

```python
import functools

import jax
import jax.numpy as jnp
from jax import lax
from jax.experimental import pallas as pl
from jax.experimental.pallas import tpu as pltpu

F32 = jnp.float32
BF16 = jnp.bfloat16

D_MODEL = 1024
N_META = 16
HEADS = 8
HEAD_DIM = 128
CHUNK = 64
SUB = 16
N_SUB = CHUNK // SUB
PAD_ROWS = CHUNK - N_META
POOL_WINDOWS = (2, 4, 8, 16)
POOL_GDIM = D_MODEL // len(POOL_WINDOWS)
HALO = 16
EPS = 1e-6
N_DEV = 8
N_COLBLK = 8
ADAM_LR, ADAM_B1, ADAM_B2, ADAM_EPS, ADAM_WD, ADAM_STEP = 0.001, 0.9, 0.999, 1e-08, 0.01, 10

VMEM_LIMIT = 56 * 1024 * 1024
MESH = pl.DeviceIdType.MESH
ANY = pl.BlockSpec(memory_space=pl.ANY)
HIGHEST = lax.Precision.HIGHEST
LOG2_E = 1.4426950408889634

REST_W_DOWN_HG = 0
REST_W_DOWN_POOL = 128
REST_W_OUT = 256
REST_POOL_W = 384
MISC_POOL_W = 0
MISC_META = 32
MISC_B_IN = 48
MISC_NORM_W = 56
MISC_LB = 57
MISC_HG_NORM_W = 59
MISC_POOL_SCALE = 60
MISC_FINAL_NORM_W = 61
MISC_ROWS = 64


def _params(sem=None):
    return pltpu.CompilerParams(dimension_semantics=sem, vmem_limit_bytes=VMEM_LIMIT)


def _row_tile(n_rows, prefer):
    best = 16
    for t in range(16, prefer + 1, 16):
        if n_rows % t == 0:
            best = t
    return best


def _sigmoid_pair(x):
    e = jnp.exp(-jnp.abs(x))
    r = 1.0 / (1.0 + e)
    er = e * r
    pos = x >= 0
    return jnp.where(pos, r, er), jnp.where(pos, er, r)


def _dot(a, b):
    return jnp.dot(a.astype(BF16), b.astype(BF16), preferred_element_type=F32)


def _dot_nt(a, b):
    return lax.dot_general(a.astype(BF16), b.astype(BF16), (((1,), (1,)), ((), ())), preferred_element_type=F32)


def _dot_tn(a, b):
    return lax.dot_general(a.astype(BF16), b.astype(BF16), (((0,), (0,)), ((), ())), preferred_element_type=F32)


def _device_index(px, py, pc):
    return 4 * px + 2 * py + pc


def _direct_gather(src_ref, dst_ref, send_sems, recv_sems, local_sem):
    x, y, c = lax.axis_index("x"), lax.axis_index("y"), lax.axis_index("c")
    own = pltpu.make_async_copy(src_ref, dst_ref.at[_device_index(x, y, c)], local_sem)
    sends, arrivals = [], []
    for k in range(1, N_DEV):
        peer = (1 - x if k & 4 else x, 1 - y if k & 2 else y, 1 - c if k & 1 else c)
        for slot, out in ((_device_index(x, y, c), sends), (_device_index(*peer), arrivals)):
            out.append(pltpu.make_async_remote_copy(
                src_ref=src_ref, dst_ref=dst_ref.at[slot], send_sem=send_sems.at[k - 1], recv_sem=recv_sems.at[k - 1],
                device_id=peer, device_id_type=MESH))
    return own, sends, arrivals


GATHER_SEMS = [pltpu.SemaphoreType.DMA((N_DEV - 1,)), pltpu.SemaphoreType.DMA((N_DEV - 1,)), pltpu.SemaphoreType.DMA(())]


def _all_gather_small(block):
    def body(x_ref, out_ref, send_sems, recv_sems, local_sem):
        own, sends, arrivals = _direct_gather(x_ref, out_ref, send_sems, recv_sems, local_sem)
        own.start()
        for cp in sends:
            cp.start()
        for cp in arrivals:
            cp.wait_recv()
        for cp in sends:
            cp.wait_send()
        own.wait()

    return pl.pallas_call(
        body, name="all_gather_meta",
        out_shape=jax.ShapeDtypeStruct((N_DEV,) + block.shape, block.dtype),
        in_specs=[ANY], out_specs=ANY, scratch_shapes=list(GATHER_SEMS),
    )(block)


def _owner_rows(ref, owner):
    return ref.at[:, pl.ds(owner * 128, 128), :]


def _exchange_sibling(a, b, m):
    def body(a_ref, b_ref, m_ref, a_out, b_out, m_out, send_sems, recv_sems):
        x, y, c = lax.axis_index("x"), lax.axis_index("y"), lax.axis_index("c")
        sibling = (x, y, 1 - c)
        copies = []
        for q in range(4):
            owner = 4 * (q // 2) + 2 * (q % 2) + (1 - c)
            pairs = ((a_ref.at[owner], a_out.at[q]), (_owner_rows(b_ref, owner), b_out.at[q]), (m_ref.at[owner], m_out.at[q]))
            for k, (src, dst) in enumerate(pairs):
                copies.append(pltpu.make_async_remote_copy(
                    src_ref=src, dst_ref=dst, send_sem=send_sems.at[3 * q + k], recv_sem=recv_sems.at[3 * q + k],
                    device_id=sibling, device_id_type=MESH))
        for cp in copies:
            cp.start()
        for cp in copies:
            cp.wait_recv()
        for cp in copies:
            cp.wait_send()

    return pl.pallas_call(
        body, name="reduce_exchange_sibling",
        out_shape=(jax.ShapeDtypeStruct((4, 1024, 1024), a.dtype), jax.ShapeDtypeStruct((4, 3, 128, 1024), b.dtype),
                   jax.ShapeDtypeStruct((4, MISC_ROWS, 1024), m.dtype)),
        in_specs=[ANY, ANY, ANY], out_specs=(ANY, ANY, ANY),
        scratch_shapes=[pltpu.SemaphoreType.DMA((12,)), pltpu.SemaphoreType.DMA((12,))],
    )(a, b, m)


def _pair_sum(a, b, m, a1, b1, m1, own_slots):
    def body(slots_ref, a_ref, b_ref, m_ref, a1_ref, b1_ref, m1_ref, a2_ref, b2_ref, m2_ref):
        a2_ref[0] = (a_ref[0].astype(F32) + a1_ref[0].astype(F32)).astype(a2_ref.dtype)
        b2_ref[0] = (b_ref[...].astype(F32) + b1_ref[0].astype(F32)).astype(b2_ref.dtype)
        m2_ref[0] = m_ref[0] + m1_ref[0]

    return pl.pallas_call(
        body, name="reduce_pair_sum",
        out_shape=(jax.ShapeDtypeStruct(a1.shape, a1.dtype), jax.ShapeDtypeStruct(b1.shape, b1.dtype),
                   jax.ShapeDtypeStruct(m1.shape, m1.dtype)),
        grid_spec=pltpu.PrefetchScalarGridSpec(
            num_scalar_prefetch=1, grid=(4,),
            in_specs=[pl.BlockSpec((1, 1024, 1024), lambda q, s: (s[q], 0, 0)),
                      pl.BlockSpec((3, 128, 1024), lambda q, s: (0, s[q], 0)),
                      pl.BlockSpec((1, MISC_ROWS, 1024), lambda q, s: (s[q], 0, 0)),
                      pl.BlockSpec((1, 1024, 1024), lambda q, s: (q, 0, 0)),
                      pl.BlockSpec((1, 3, 128, 1024), lambda q, s: (q, 0, 0, 0)),
                      pl.BlockSpec((1, MISC_ROWS, 1024), lambda q, s: (q, 0, 0))],
            out_specs=(pl.BlockSpec((1, 1024, 1024), lambda q, s: (q, 0, 0)),
                       pl.BlockSpec((1, 3, 128, 1024), lambda q, s: (q, 0, 0, 0)),
                       pl.BlockSpec((1, MISC_ROWS, 1024), lambda q, s: (q, 0, 0)))),
        compiler_params=_params(("arbitrary",)),
    )(own_slots, a, b, m, a1, b1, m1)


def _exchange_chips(a2, b2, m2):
    def body(a_ref, b_ref, m_ref, a_out, b_out, m_out, send_sems, recv_sems, local_sems):
        x, y, c = lax.axis_index("x"), lax.axis_index("y"), lax.axis_index("c")
        chips = [(1 - x, y), (x, 1 - y), (1 - x, 1 - y)]
        arrays = ((a_ref, a_out), (b_ref, b_out), (m_ref, m_out))
        me = 2 * x + y
        local = [pltpu.make_async_copy(src.at[me], dst.at[me], local_sems.at[k]) for k, (src, dst) in enumerate(arrays)]
        for cp in local:
            cp.start()

        def copy(j, k, dst_slot):
            qx, qy = chips[j]
            src, dst = arrays[k]
            return pltpu.make_async_remote_copy(
                src_ref=src.at[2 * qx + qy], dst_ref=dst.at[dst_slot], send_sem=send_sems.at[3 * j + k],
                recv_sem=recv_sems.at[3 * j + k], device_id=(qx, qy, c), device_id_type=MESH)

        sends = [copy(j, k, me) for j in range(3) for k in range(3)]
        for cp in sends:
            cp.start()
        for j, (qx, qy) in enumerate(chips):
            for k in range(3):
                copy(j, k, 2 * qx + qy).wait_recv()
        for cp in sends:
            cp.wait_send()
        for cp in local:
            cp.wait()

    return pl.pallas_call(
        body, name="reduce_exchange_chips",
        out_shape=tuple(jax.ShapeDtypeStruct(t.shape, t.dtype) for t in (a2, b2, m2)),
        in_specs=[ANY, ANY, ANY], out_specs=(ANY, ANY, ANY),
        scratch_shapes=[pltpu.SemaphoreType.DMA((9,)), pltpu.SemaphoreType.DMA((9,)), pltpu.SemaphoreType.DMA((3,))],
    )(a2, b2, m2)


def _adam_update(g, w, m, v):
    mn = ADAM_B1 * m + (1.0 - ADAM_B1) * g
    vn = ADAM_B2 * v + (1.0 - ADAM_B2) * (g * g)
    m_hat = mn / (1.0 - ADAM_B1 ** ADAM_STEP)
    v_hat = vn / (1.0 - ADAM_B2 ** ADAM_STEP)
    return -ADAM_LR * (m_hat / (jnp.sqrt(v_hat) + ADAM_EPS) + ADAM_WD * w), mn, vn


def _chip_sum(ref4):
    t = [ref4[s].astype(F32) for s in range(4)]
    return (t[0] + t[1]) + (t[2] + t[3])


def _finish(a3, b3, m3, big, rows3, small):
    n_steps = 4
    tb, tr3 = 1024 // n_steps, 128 // n_steps

    def body(*refs):
        it = iter(refs)
        a_ref, b_ref, m_ref = next(it), next(it), next(it)
        big_in = [next(it) for _ in range(3)]
        rows_in = [[next(it) for _ in range(3)] for _ in rows3]
        small_in = [[next(it) for _ in range(3)] for _ in small]
        big_out = [next(it) for _ in range(4)]
        rows_out = [[next(it) for _ in range(4)] for _ in rows3]
        small_out = [[next(it) for _ in range(4)] for _ in small]

        def apply(g, ins, outs):
            d, mn, vn = _adam_update(g, ins[0][...], ins[1][...], ins[2][...])
            for r, val in zip(outs, (g, d, mn, vn)):
                r[...] = val

        apply(_chip_sum(a_ref)[None], big_in, big_out)
        for k in range(len(rows3)):
            apply(_chip_sum(b_ref.at[:, k])[None], rows_in[k], rows_out[k])

        @pl.when(pl.program_id(0) == 0)
        def _():
            for (row0, lanes, ins), r_in, r_out in zip(small, small_in, small_out):
                n = ins[0].shape[0]
                g = (m_ref[0, row0:row0 + n, :lanes] + m_ref[1, row0:row0 + n, :lanes]) + (
                    m_ref[2, row0:row0 + n, :lanes] + m_ref[3, row0:row0 + n, :lanes])
                apply(g, r_in, r_out)

    whole = lambda shape: pl.BlockSpec(shape, lambda i: (0,) * len(shape))
    big_blk = pl.BlockSpec((1, tb, 1024), lambda i: (0, i, 0))
    rows_blk = pl.BlockSpec((1, tr3, 1024), lambda i: (0, i, 0))
    in_specs = [pl.BlockSpec((4, tb, 1024), lambda i: (0, i, 0)), pl.BlockSpec((4, 3, tr3, 1024), lambda i: (0, 0, i, 0)),
                whole(m3.shape)]
    in_specs += [big_blk] * 3 + [rows_blk] * (3 * len(rows3))
    out_specs = [big_blk] * 4 + [rows_blk] * (4 * len(rows3))
    out_shape = [jax.ShapeDtypeStruct(big[0].shape, F32)] * 4
    for w, _, _ in rows3:
        out_shape += [jax.ShapeDtypeStruct(w.shape, F32)] * 4
    args = [a3, b3, m3, *big]
    for t in rows3:
        args += list(t)
    for _, _, t in small:
        in_specs += [whole(t[0].shape)] * 3
        out_specs += [whole(t[0].shape)] * 4
        out_shape += [jax.ShapeDtypeStruct(t[0].shape, F32)] * 4
        args += list(t)
    outs = pl.pallas_call(
        body, name="reduce_sum_adamw", out_shape=tuple(out_shape), grid=(n_steps,),
        in_specs=in_specs, out_specs=tuple(out_specs), compiler_params=_params(("arbitrary",)),
    )(*args)
    return [tuple(outs[4 * k:4 * k + 4]) for k in range(len(outs) // 4)]


def _first_norm(zp, norm_w):
    n_rows = zp.shape[0]
    tr = _row_tile(n_rows, 832)

    def body(z_ref, nw_ref, h_ref):
        z = z_ref[...]
        r = lax.rsqrt(jnp.mean(z * z, axis=-1, keepdims=True) + EPS)
        h_ref[...] = (z * r * nw_ref[...]).astype(BF16)

    return pl.pallas_call(
        body, name="first_norm",
        out_shape=jax.ShapeDtypeStruct((n_rows, D_MODEL), BF16),
        grid=(n_rows // tr,),
        in_specs=[pl.BlockSpec((tr, D_MODEL), lambda i: (i, 0)), pl.BlockSpec((1, D_MODEL), lambda i: (0, 0))],
        out_specs=pl.BlockSpec((tr, D_MODEL), lambda i: (i, 0)),
        compiler_params=_params(("arbitrary",)),
    )(zp, norm_w)


def _gather_order():
    x, y, c = lax.axis_index("x"), lax.axis_index("y"), lax.axis_index("c")
    chips = [(1 - x, y), (x, 1 - y), (1 - x, 1 - y)]
    order = [_device_index(x, y, c), _device_index(x, y, 1 - c)]
    order += [_device_index(*q, c) for q in chips] + [_device_index(*q, 1 - c) for q in chips]
    return jnp.stack(order).astype(jnp.int32)


def _in_projection(h, w_shard, b_blocks, order):
    n_rows = h.shape[0]
    tr = _row_tile(n_rows, 320)
    nt = n_rows // tr

    def body(order_ref, h_ref, w_hbm, b_ref, p_ref, w_out, w_vmem, send_sems, recv_sems, local_sem, out_sems):
        s, i = pl.program_id(0), pl.program_id(1)
        x, y, c = lax.axis_index("x"), lax.axis_index("y"), lax.axis_index("c")
        me, sibling = (x, y, c), (x, y, 1 - c)
        chips = [(1 - x, y), (x, 1 - y), (1 - x, 1 - y)]

        def slot(px, py, pc):
            return w_vmem.at[_device_index(px, py, pc)]

        def copy(k, blk, to, src=None):
            return pltpu.make_async_remote_copy(
                src_ref=slot(*blk) if src is None else src, dst_ref=slot(*blk),
                send_sem=send_sems.at[k], recv_sem=recv_sems.at[k], device_id=to, device_id_type=MESH)

        own = pltpu.make_async_copy(w_hbm, slot(*me), local_sem)
        first = [copy(0, me, sibling, src=w_hbm)] + [copy(1 + j, me, (*q, c), src=w_hbm) for j, q in enumerate(chips)]
        passed = [copy(4 + j, (*q, c), sibling) for j, q in enumerate(chips)]
        arrivals = [None, copy(0, sibling, me)]
        arrivals += [copy(1 + j, (*q, c), me) for j, q in enumerate(chips)]
        arrivals += [copy(4 + j, (*q, 1 - c), me) for j, q in enumerate(chips)]

        def keep(step):
            return pltpu.make_async_copy(w_vmem.at[order_ref[step]], w_out.at[order_ref[step]], out_sems.at[step])

        for step in range(N_DEV):
            @pl.when((i == 0) & (s == step))
            def _(step=step):
                if step == 0:
                    own.start()
                    for cp in first:
                        cp.start()
                    own.wait()
                else:
                    arrivals[step].wait_recv()
                    if 2 <= step <= 4:
                        passed[step - 2].start()
                keep(step).start()

        p_ref[0] = jnp.dot(h_ref[...], w_vmem[order_ref[s]], preferred_element_type=F32) + b_ref[0]

        @pl.when((s == N_DEV - 1) & (i == nt - 1))
        def _():
            for cp in first + passed:
                cp.wait_send()
            for step in range(N_DEV):
                keep(step).wait()

    return pl.pallas_call(
        body, name="in_projection_gather",
        out_shape=(jax.ShapeDtypeStruct((N_COLBLK, n_rows, 1024), F32),
                   jax.ShapeDtypeStruct((N_DEV, D_MODEL, 1024), BF16)),
        grid_spec=pltpu.PrefetchScalarGridSpec(
            num_scalar_prefetch=1, grid=(N_DEV, nt),
            in_specs=[pl.BlockSpec((tr, D_MODEL), lambda s, i, o: (i, 0)), ANY,
                      pl.BlockSpec((1, 1, 1024), lambda s, i, o: (o[s], 0, 0))],
            out_specs=(pl.BlockSpec((1, tr, 1024), lambda s, i, o: (o[s], i, 0)), ANY),
            scratch_shapes=[pltpu.VMEM((N_DEV, D_MODEL, 1024), BF16), pltpu.SemaphoreType.DMA((7,)),
                            pltpu.SemaphoreType.DMA((7,)), pltpu.SemaphoreType.DMA(()), pltpu.SemaphoreType.DMA((N_DEV,))]),
        compiler_params=_params(("arbitrary", "arbitrary")),
    )(order, h, w_shard, b_blocks)


def _lower_bound(lb_ref):
    l0, l1 = lb_ref[0:1, :], lb_ref[1:2, :]
    _, lb = _sigmoid_pair(l1 - l0)
    return lb


def _chunk_gates(fz, lb, valid):
    sig, nsig = _sigmoid_pair(fz)
    f = lb + (1.0 - lb) * sig
    g = jnp.where(valid, jnp.log(f), 0.0)
    k = jnp.where(valid, (1.0 - lb) * nsig, 0.0)
    return sig, nsig, f, g, k


def _tri(n, upper=False):
    r = lax.broadcasted_iota(jnp.int32, (n, n), 0)
    c = lax.broadcasted_iota(jnp.int32, (n, n), 1)
    return jnp.where((r <= c) if upper else (r >= c), 1.0, 0.0).astype(F32)


def _intra_scores(q_ref, k_ref, b2_ref, a_ref, kt_ref, col0):
    cols = pl.ds(col0, HEAD_DIM)
    rows_s = lax.broadcasted_iota(jnp.int32, (SUB, 1), 0)
    lanes_c = lax.broadcasted_iota(jnp.int32, (1, CHUNK), 1)
    kt_ref[...] = jnp.zeros_like(kt_ref)
    for i in range(N_SUB):
        lo = i * SUB
        qi = q_ref[lo:lo + SUB, cols]
        bi = b2_ref[lo:lo + SUB, cols]
        if i == 0:
            acc = jnp.zeros((SUB, CHUNK), F32)
        else:
            ref_i = b2_ref[lo:lo + 1, cols]
            qt = qi * jnp.exp2(bi - ref_i)
            kt_ref[0:lo, :] = k_ref[0:lo, cols] * jnp.exp2(ref_i - b2_ref[0:lo, cols])
            acc = _dot_nt(qt, kt_ref[...])
        for s in range(SUB):
            b_s = b2_ref[lo + s:lo + s + 1, cols]
            k_s = k_ref[lo + s:lo + s + 1, cols]
            w = jnp.exp2(jnp.minimum(bi - b_s, 0.0))
            col = jnp.sum((qi * w) * k_s, axis=-1, keepdims=True)
            acc = jnp.where(lanes_c == lo + s, col, acc)
        a_ref[lo:lo + SUB, :] = jnp.where(lanes_c <= lo + rows_s, acc, 0.0)


def _hgrn_forward(p, lb_logits, w_rest):
    n_rows = p.shape[1]
    n_chunks = n_rows // CHUNK
    width = HEADS * HEAD_DIM

    def body(q_ref, fz_ref, v_ref, lb_ref, rest_ref, o_ref, st_out_ref, a_out_ref, rest_out,
             state, k_vmem, b2_vmem, a_vmem, kt_vmem, send_sems, recv_sems, local_sem):
        n = pl.program_id(0)
        own, sends, arrivals = _direct_gather(rest_ref, rest_out, send_sems, recv_sems, local_sem)

        @pl.when(n == 0)
        def _():
            state[...] = jnp.zeros_like(state)
            own.start()
            for cp in sends:
                cp.start()

        rows = n * CHUNK + lax.broadcasted_iota(jnp.int32, (CHUNK, 1), 0)
        valid = rows >= PAD_ROWS
        lb = _lower_bound(lb_ref)
        _, _, _, g, k = _chunk_gates(fz_ref[0], lb, valid)
        k_vmem[...] = k
        b2_vmem[...] = jnp.dot(_tri(CHUNK), g, precision=HIGHEST, preferred_element_type=F32) * LOG2_E
        q_view = q_ref.at[0]
        for h in range(HEADS):
            cols = pl.ds(h * HEAD_DIM, HEAD_DIM)
            st = state[h]
            st_out_ref[0, h] = st
            bh = b2_vmem[:, cols]
            kh = k_vmem[:, cols]
            vh = jnp.where(valid, v_ref[0, :, cols], 0.0)
            qe = q_ref[0, :, cols] * jnp.exp2(bh)
            _intra_scores(q_view, k_vmem, b2_vmem, a_vmem, kt_vmem, h * HEAD_DIM)
            a = a_vmem[...].astype(BF16)
            a_out_ref[0, h] = a
            o_ref[:, cols] = _dot_nt(qe, st) + _dot(a, vh)
            b_last = b2_vmem[CHUNK - 1:CHUNK, cols]
            kd = kh * jnp.exp2(b_last - bh)
            state[h] = st * jnp.exp2(b_last) + _dot_tn(vh, kd)

        @pl.when(n == n_chunks - 1)
        def _():
            for cp in arrivals:
                cp.wait_recv()
            for cp in sends:
                cp.wait_send()
            own.wait()

    blk = lambda c: pl.BlockSpec((1, CHUNK, width), lambda n, c=c: (c, n, 0))
    return pl.pallas_call(
        body, name="hgrn_forward",
        out_shape=(jax.ShapeDtypeStruct((n_rows, width), F32),
                   jax.ShapeDtypeStruct((n_chunks, HEADS, HEAD_DIM, HEAD_DIM), F32),
                   jax.ShapeDtypeStruct((n_chunks, HEADS, CHUNK, CHUNK), BF16),
                   jax.ShapeDtypeStruct((N_DEV,) + w_rest.shape, w_rest.dtype)),
        grid=(n_chunks,),
        in_specs=[blk(0), blk(1), blk(2), pl.BlockSpec((2, width), lambda n: (0, 0)), ANY],
        out_specs=(pl.BlockSpec((CHUNK, width), lambda n: (n, 0)),
                   pl.BlockSpec((1, HEADS, HEAD_DIM, HEAD_DIM), lambda n: (n, 0, 0, 0)),
                   pl.BlockSpec((1, HEADS, CHUNK, CHUNK), lambda n: (n, 0, 0, 0)), ANY),
        scratch_shapes=[pltpu.VMEM((HEADS, HEAD_DIM, HEAD_DIM), F32), pltpu.VMEM((CHUNK, width), F32),
                        pltpu.VMEM((CHUNK, width), F32), pltpu.VMEM((CHUNK, CHUNK), F32),
                        pltpu.VMEM((CHUNK, HEAD_DIM), F32)] + list(GATHER_SEMS),
        compiler_params=_params(("arbitrary",)),
    )(p, p, p, lb_logits, w_rest)


def _hgrn_backward(p, lb_logits, states, scores, d_o):
    n_rows = p.shape[1]
    n_chunks = n_rows // CHUNK
    width = HEADS * HEAD_DIM

    def body(q_ref, fz_ref, v_ref, lb_ref, st_ref, a_ref, do_ref, dp_ref, dbias_ref, dlb_ref,
             dstate, k_vmem, b2_vmem, kt_vmem, dqi_vmem, dki_vmem):
        step = pl.program_id(0)
        n = n_chunks - 1 - step

        @pl.when(step == 0)
        def _():
            dstate[...] = jnp.zeros_like(dstate)
            dbias_ref[...] = jnp.zeros_like(dbias_ref)
            dlb_ref[...] = jnp.zeros_like(dlb_ref)

        rows = n * CHUNK + lax.broadcasted_iota(jnp.int32, (CHUNK, 1), 0)
        valid = rows >= PAD_ROWS
        lb = _lower_bound(lb_ref)
        sig, nsig, f, g, k = _chunk_gates(fz_ref[0], lb, valid)
        k_vmem[...] = k
        b2_vmem[...] = jnp.dot(_tri(CHUNK), g, precision=HIGHEST, preferred_element_type=F32) * LOG2_E
        rows_c = lax.broadcasted_iota(jnp.int32, (CHUNK, 1), 0)
        lanes_c = lax.broadcasted_iota(jnp.int32, (1, CHUNK), 1)
        causal = lax.broadcasted_iota(jnp.int32, (CHUNK, CHUNK), 0) >= lax.broadcasted_iota(jnp.int32, (CHUNK, CHUNK), 1)
        tri_up = _tri(CHUNK, upper=True)
        for h in range(HEADS):
            cols = pl.ds(h * HEAD_DIM, HEAD_DIM)
            st = st_ref[0, h]
            dst = dstate[h]
            qh = q_ref[0, :, cols]
            bh = b2_vmem[:, cols]
            kh = k_vmem[:, cols]
            vh = jnp.where(valid, v_ref[0, :, cols], 0.0)
            doh = do_ref[:, cols]
            eb = jnp.exp2(bh)
            qe = qh * eb
            b_last = b2_vmem[CHUNK - 1:CHUNK, cols]
            e_last = jnp.exp2(b_last)
            decay_k = jnp.exp2(b_last - bh)
            kd = kh * decay_k
            dqe = _dot(doh, st)
            da = jnp.where(causal, _dot_nt(doh, vh), 0.0)
            dv = _dot_tn(a_ref[0, h], doh) + _dot_nt(kd, dst)
            dkd = _dot(vh, dst)
            dstate[h] = dst * e_last + _dot_tn(doh, qe)
            db_last = (jnp.sum(dst * st, axis=0, keepdims=True) * e_last
                       + jnp.sum(dkd * kd, axis=0, keepdims=True))
            dki_vmem[...] = jnp.zeros_like(dki_vmem)
            kt_vmem[...] = jnp.zeros_like(kt_vmem)
            for i in range(N_SUB):
                lo = i * SUB
                qi = q_ref[0, lo:lo + SUB, cols]
                bi = b2_vmem[lo:lo + SUB, cols]
                da_i = da[lo:lo + SUB, :]
                if i == 0:
                    dq_i = jnp.zeros((SUB, HEAD_DIM), F32)
                else:
                    ref_i = b2_vmem[lo:lo + 1, cols]
                    eq = jnp.exp2(bi - ref_i)
                    ek = jnp.exp2(ref_i - b2_vmem[0:lo, cols])
                    kt_vmem[0:lo, :] = k_vmem[0:lo, cols] * ek
                    dq_i = _dot(da_i, kt_vmem[...]) * eq
                    dki_vmem[0:lo, :] += _dot_tn(da_i, qi * eq)[0:lo] * ek
                for s in range(SUB):
                    b_s = b2_vmem[lo + s:lo + s + 1, cols]
                    k_s = k_vmem[lo + s:lo + s + 1, cols]
                    w = jnp.exp2(jnp.minimum(bi - b_s, 0.0))
                    da_col = jnp.sum(jnp.where(lanes_c == lo + s, da_i, 0.0), axis=-1, keepdims=True)
                    gw = da_col * w
                    dq_i = dq_i + gw * k_s
                    dki_vmem[lo + s:lo + s + 1, :] += jnp.sum(gw * qi, axis=0, keepdims=True)
                dqi_vmem[lo:lo + SUB, :] = dq_i
            dq_intra = dqi_vmem[...]
            dk_intra = dki_vmem[...]
            dq = dqe * eb + dq_intra
            dk = dkd * decay_k + dk_intra
            db = dqe * qe - dkd * kd + qh * dq_intra - kh * dk_intra
            db = db + jnp.where(rows_c == CHUNK - 1, db_last, 0.0)
            dg = jnp.dot(tri_up, db, precision=HIGHEST, preferred_element_type=F32)
            fh = f[:, h * HEAD_DIM:(h + 1) * HEAD_DIM]
            sh = sig[:, h * HEAD_DIM:(h + 1) * HEAD_DIM]
            nh = nsig[:, h * HEAD_DIM:(h + 1) * HEAD_DIM]
            lbh = lb[:, h * HEAD_DIM:(h + 1) * HEAD_DIM]
            df = jnp.where(valid, dg / fh - dk, 0.0)
            dfz = df * (1.0 - lbh) * sh * nh
            dq = jnp.where(valid, dq, 0.0)
            dv = jnp.where(valid, dv, 0.0)
            dlb_ref[:, cols] += jnp.sum(df * nh, axis=0, keepdims=True)
            dp_ref[0, :, cols] = dq.astype(BF16)
            dp_ref[1, :, cols] = dfz.astype(BF16)
            dp_ref[2, :, cols] = dv.astype(BF16)
            dbias_ref[0, :, cols] += jnp.sum(dq, axis=0, keepdims=True)
            dbias_ref[1, :, cols] += jnp.sum(dfz, axis=0, keepdims=True)
            dbias_ref[2, :, cols] += jnp.sum(dv, axis=0, keepdims=True)

    rev = lambda s: n_chunks - 1 - s
    blk = lambda c: pl.BlockSpec((1, CHUNK, width), lambda s, c=c: (c, rev(s), 0))
    return pl.pallas_call(
        body, name="hgrn_backward",
        out_shape=(jax.ShapeDtypeStruct((3, n_rows, width), BF16),
                   jax.ShapeDtypeStruct((3, 1, width), F32),
                   jax.ShapeDtypeStruct((1, width), F32)),
        grid=(n_chunks,),
        in_specs=[blk(0), blk(1), blk(2), pl.BlockSpec((2, width), lambda s: (0, 0)),
                  pl.BlockSpec((1, HEADS, HEAD_DIM, HEAD_DIM), lambda s: (rev(s), 0, 0, 0)),
                  pl.BlockSpec((1, HEADS, CHUNK, CHUNK), lambda s: (rev(s), 0, 0, 0)),
                  pl.BlockSpec((CHUNK, width), lambda s: (rev(s), 0))],
        out_specs=(pl.BlockSpec((3, CHUNK, width), lambda s: (0, rev(s), 0)),
                   pl.BlockSpec((3, 1, width), lambda s: (0, 0, 0)),
                   pl.BlockSpec((1, width), lambda s: (0, 0))),
        scratch_shapes=[pltpu.VMEM((HEADS, HEAD_DIM, HEAD_DIM), F32), pltpu.VMEM((CHUNK, width), F32),
                        pltpu.VMEM((CHUNK, width), F32), pltpu.VMEM((CHUNK, HEAD_DIM), F32),
                        pltpu.VMEM((CHUNK, HEAD_DIM), F32), pltpu.VMEM((CHUNK, HEAD_DIM), F32)],
        compiler_params=_params(("arbitrary",)),
    )(p, p, p, lb_logits, states, scores, d_o)


def _silu_and_grad(x):
    s, ns = _sigmoid_pair(x)
    return x * s, s * (1.0 + x * ns)


def _tail(p, o, zp, tgt, hg_norm_w, pool_w, pool_scale, w_down_hg, w_down_pool, w_out, final_norm_w):
    n_rows = zp.shape[0]
    tr = _row_tile(n_rows, 160)
    nt = n_rows // tr
    ext = tr + HALO
    n_groups = len(POOL_WINDOWS)

    def body(o_ref, ghg_ref, u_ref, gpool_ref, mhg_ref, mpool_ref, uhalo_ref, z_ref, tgt_ref,
             hgw_ref, pw_ref, ps_ref, wdh_ref, wdp_ref, wout_ref, fnw_ref,
             do_ref, dp_ref, dz2_ref, lhs_ref, rhs_ref,
             dbias_ref, dhgw_ref, dpw_ref, dps_ref, dfnw_ref, loss_ref, halo_vmem):
        step = pl.program_id(0)
        ti = nt - 1 - step

        @pl.when(step == 0)
        def _():
            halo_vmem[...] = jnp.zeros_like(halo_vmem)
            for r in (dbias_ref, dhgw_ref, dpw_ref, dps_ref, dfnw_ref, loss_ref):
                r[...] = jnp.zeros_like(r)

        rows = ti * tr + lax.broadcasted_iota(jnp.int32, (tr, 1), 0)
        valid = rows >= PAD_ROWS
        in_loss = rows >= CHUNK
        count_pos = jnp.maximum(rows - PAD_ROWS + 1, 1).astype(F32)

        o = o_ref[...]
        hgw = hgw_ref[...]
        inv_o, on_parts = [], []
        for h in range(HEADS):
            oh = o[:, h * HEAD_DIM:(h + 1) * HEAD_DIM]
            r = lax.rsqrt(jnp.mean(oh * oh, axis=-1, keepdims=True) + EPS)
            inv_o.append(r)
            on_parts.append(oh * r)
        o_hat = jnp.concatenate(on_parts, axis=1)
        o_n = o_hat * hgw
        g_hg = ghg_ref[0]
        silu_hg, dsilu_hg = _silu_and_grad(g_hg)
        a_hg = o_n * silu_hg
        y_hg = _dot(a_hg, wdh_ref[...])

        u = jnp.where(valid, u_ref[0], 0.0)
        u_prev = jnp.where(ti > 0, uhalo_ref[0], 0.0)
        u_ext = jnp.concatenate([u_prev, u], axis=0)
        pooled_parts, mixed_parts, inv_cnt = [], [], []
        for gi, win in enumerate(POOL_WINDOWS):
            lanes = slice(gi * POOL_GDIM, (gi + 1) * POOL_GDIM)
            s = u_ext[:, lanes]
            shift = 1
            while shift < win:
                s = s + pltpu.roll(s, shift, 0)
                shift *= 2
            ic = 1.0 / jnp.minimum(count_pos, float(win))
            inv_cnt.append(ic)
            pooled = s[HALO:] * ic - u[:, lanes]
            pooled_parts.append(pooled)
            mixed_parts.append(_dot(pooled, pw_ref[gi]))
        mixed = jnp.concatenate(mixed_parts, axis=1)
        ps = ps_ref[...]
        g_pool = gpool_ref[0]
        silu_pool, dsilu_pool = _silu_and_grad(g_pool)
        a_pool = mixed * ps * silu_pool
        y_pool = _dot(a_pool, wdp_ref[...])

        m_hg, m_pool = mhg_ref[0], mpool_ref[0]
        s_hg, ns_hg = _sigmoid_pair(m_hg)
        s_pool, ns_pool = _sigmoid_pair(m_pool)
        merged = s_hg * y_hg + s_pool * y_pool
        z2 = z_ref[...] + _dot(merged, wout_ref[...])
        r2 = lax.rsqrt(jnp.mean(z2 * z2, axis=-1, keepdims=True) + EPS)
        n2 = z2 * r2
        fnw = fnw_ref[...]
        err = jnp.where(in_loss, n2 * fnw - tgt_ref[...], 0.0)
        loss_ref[...] += jnp.sum(jnp.sum(err * err, axis=0, keepdims=True), axis=1, keepdims=True) * (0.5 / D_MODEL)
        dy = err * (1.0 / D_MODEL)

        dfnw_ref[...] += jnp.sum(dy * n2, axis=0, keepdims=True)
        gy = dy * fnw
        dz2 = r2 * (gy - n2 * jnp.mean(gy * n2, axis=-1, keepdims=True))
        dmerged = _dot_nt(dz2, wout_ref[...])
        dy_hg = s_hg * dmerged
        dy_pool = s_pool * dmerged
        dm_hg = dmerged * y_hg * s_hg * ns_hg
        dm_pool = dmerged * y_pool * s_pool * ns_pool
        da_hg = _dot_nt(dy_hg, wdh_ref[...])
        da_pool = _dot_nt(dy_pool, wdp_ref[...])

        d_on = da_hg * silu_hg
        dg_hg = da_hg * o_n * dsilu_hg
        dhgw_ref[...] += jnp.sum(d_on * o_hat, axis=0, keepdims=True)
        gyo = d_on * hgw
        do_parts = []
        for h in range(HEADS):
            lanes = slice(h * HEAD_DIM, (h + 1) * HEAD_DIM)
            gh, nh = gyo[:, lanes], o_hat[:, lanes]
            do_parts.append(inv_o[h] * (gh - nh * jnp.mean(gh * nh, axis=-1, keepdims=True)))
        do_ref[...] = jnp.concatenate(do_parts, axis=1)

        dmixed = da_pool * ps * silu_pool
        dps_ref[...] += jnp.sum(da_pool * mixed * silu_pool, axis=0, keepdims=True)
        dg_pool = da_pool * mixed * ps * dsilu_pool
        du_parts = []
        for gi, win in enumerate(POOL_WINDOWS):
            lanes = slice(gi * POOL_GDIM, (gi + 1) * POOL_GDIM)
            dmx = dmixed[:, lanes]
            dpooled = _dot_nt(dmx, pw_ref[gi])
            dpw_ref[gi] += _dot_tn(pooled_parts[gi], dmx)
            dpt = dpooled * inv_cnt[gi]
            s = jnp.concatenate([dpt, halo_vmem[:, lanes]], axis=0)
            shift = 1
            while shift < win:
                s = s + pltpu.roll(s, ext - shift, 0)
                shift *= 2
            du_parts.append(s[:tr] - dpooled)
            halo_vmem[:, lanes] = dpt[:HALO]
        du = jnp.where(valid, jnp.concatenate(du_parts, axis=1), 0.0)

        for c, val in enumerate((dg_hg, du, dg_pool, dm_hg, dm_pool)):
            dp_ref[c] = val.astype(BF16)
            dbias_ref[c] += jnp.sum(val, axis=0, keepdims=True)
        dz2_ref[...] = dz2
        for c, (lhs, rhs) in enumerate(((merged, dz2), (a_hg, dy_hg), (a_pool, dy_pool))):
            lhs_ref[c] = lhs.astype(BF16)
            rhs_ref[c] = rhs.astype(BF16)

    rev = lambda s: nt - 1 - s
    rowblk = pl.BlockSpec((tr, D_MODEL), lambda s: (rev(s), 0))
    pblk = lambda c: pl.BlockSpec((1, tr, 1024), lambda s, c=c: (c, rev(s), 0))
    halo_blk = pl.BlockSpec((1, HALO, 1024), lambda s: (4, jnp.maximum(rev(s) * (tr // HALO) - 1, 0), 0))
    full = lambda shape: pl.BlockSpec(shape, lambda s: (0,) * len(shape))
    vec = full((1, D_MODEL))
    mat = full((D_MODEL, D_MODEL))
    act3 = jax.ShapeDtypeStruct((3, n_rows, D_MODEL), BF16)
    act3_blk = pl.BlockSpec((3, tr, D_MODEL), lambda s: (0, rev(s), 0))
    return pl.pallas_call(
        body, name="tail_forward_backward",
        out_shape=(jax.ShapeDtypeStruct((n_rows, D_MODEL), F32),
                   jax.ShapeDtypeStruct((5, n_rows, 1024), BF16),
                   jax.ShapeDtypeStruct((n_rows, D_MODEL), F32),
                   act3, act3,
                   jax.ShapeDtypeStruct((5, 1, 1024), F32),
                   jax.ShapeDtypeStruct((1, D_MODEL), F32),
                   jax.ShapeDtypeStruct((n_groups, POOL_GDIM, POOL_GDIM), F32),
                   jax.ShapeDtypeStruct((1, D_MODEL), F32),
                   jax.ShapeDtypeStruct((1, D_MODEL), F32),
                   jax.ShapeDtypeStruct((1, 1), F32)),
        grid=(nt,),
        in_specs=[rowblk, pblk(3), pblk(4), pblk(5), pblk(6), pblk(7), halo_blk, rowblk, rowblk,
                  vec, full((n_groups, POOL_GDIM, POOL_GDIM)), vec, mat, mat, mat, vec],
        out_specs=(rowblk, pl.BlockSpec((5, tr, 1024), lambda s: (0, rev(s), 0)), rowblk,
                   act3_blk, act3_blk,
                   full((5, 1, 1024)), vec, full((n_groups, POOL_GDIM, POOL_GDIM)), vec, vec, full((1, 1))),
        scratch_shapes=[pltpu.VMEM((HALO, D_MODEL), F32)],
        compiler_params=_params(("arbitrary",)),
    )(o, p, p, p, p, p, p, zp, tgt, hg_norm_w, pool_w, pool_scale, w_down_hg, w_down_pool, w_out, final_norm_w)


def _in_projection_backward(dp_a, dp_b, w_blocks, zp, dz2, norm_w):
    n_rows = zp.shape[0]
    tr = _row_tile(n_rows, 320)
    na, nb = dp_a.shape[0], dp_b.shape[0]

    def body(dpa_ref, dpb_ref, w_hbm, z_ref, dz2_ref, nw_ref, dz_ref, dnw_ref, w_vmem, sem):
        i = pl.program_id(0)

        @pl.when(i == 0)
        def _():
            cp = pltpu.make_async_copy(w_hbm, w_vmem, sem)
            cp.start()
            cp.wait()
            dnw_ref[...] = jnp.zeros_like(dnw_ref)

        dh = jnp.zeros((tr, D_MODEL), F32)
        for j in range(na):
            dh = dh + _dot_nt(dpa_ref[j], w_vmem[j])
        for j in range(nb):
            dh = dh + _dot_nt(dpb_ref[j], w_vmem[na + j])
        z = z_ref[...]
        r = lax.rsqrt(jnp.mean(z * z, axis=-1, keepdims=True) + EPS)
        n1 = z * r
        dnw_ref[...] += jnp.sum(dh * n1, axis=0, keepdims=True)
        gh = dh * nw_ref[...]
        dz_ref[...] = dz2_ref[...] + r * (gh - n1 * jnp.mean(gh * n1, axis=-1, keepdims=True))

    rowblk = pl.BlockSpec((tr, D_MODEL), lambda i: (i, 0))
    vec = pl.BlockSpec((1, D_MODEL), lambda i: (0, 0))
    return pl.pallas_call(
        body, name="in_projection_backward",
        out_shape=(jax.ShapeDtypeStruct((n_rows, D_MODEL), F32), jax.ShapeDtypeStruct((1, D_MODEL), F32)),
        grid=(n_rows // tr,),
        in_specs=[pl.BlockSpec((na, tr, 1024), lambda i: (0, i, 0)), pl.BlockSpec((nb, tr, 1024), lambda i: (0, i, 0)),
                  ANY, rowblk, rowblk, vec],
        out_specs=(rowblk, vec),
        scratch_shapes=[pltpu.VMEM((N_COLBLK, D_MODEL, 1024), BF16), pltpu.SemaphoreType.DMA(())],
        compiler_params=_params(("arbitrary",)),
    )(dp_a, dp_b, w_blocks, zp, dz2, norm_w)


def _weight_grad(xs, ys_parts, name):
    shared = xs.ndim == 2
    n_rows, m = xs.shape[-2:]
    n = ys_parts[0].shape[-1]
    offsets = [sum(y.shape[0] for y in ys_parts[:i]) for i in range(len(ys_parts))]
    nb = offsets[-1] + ys_parts[-1].shape[0]
    tk = _row_tile(n_rows, 832)
    n_k = n_rows // tk

    def body(x_ref, *rest):
        y_refs, o_ref, acc = rest[:len(ys_parts)], rest[-2], rest[-1]
        j, k = pl.program_id(0), pl.program_id(1)

        @pl.when(k == 0)
        def _():
            acc[...] = jnp.zeros_like(acc)

        x = x_ref[...] if shared else x_ref[0]
        for off, y, y_ref in zip(offsets, ys_parts, y_refs):
            @pl.when((j >= off) & (j < off + y.shape[0]))
            def _(y_ref=y_ref):
                acc[...] += _dot_tn(x, y_ref[0])

        @pl.when(k == n_k - 1)
        def _():
            o_ref[0] = acc[...].astype(o_ref.dtype)

    x_spec = pl.BlockSpec((tk, m), lambda j, k: (k, 0)) if shared else pl.BlockSpec((1, tk, m), lambda j, k: (j, k, 0))
    y_specs = [pl.BlockSpec((1, tk, n), lambda j, k, off=off, cnt=y.shape[0]: (jnp.clip(j - off, 0, cnt - 1), k, 0))
               for off, y in zip(offsets, ys_parts)]
    return pl.pallas_call(
        body, name=name,
        out_shape=jax.ShapeDtypeStruct((nb, m, n), BF16),
        grid=(nb, n_k),
        in_specs=[x_spec] + y_specs,
        out_specs=pl.BlockSpec((1, m, n), lambda j, k: (j, 0, 0)),
        scratch_shapes=[pltpu.VMEM((m, n), F32)],
        compiler_params=_params(("arbitrary", "arbitrary")),
    )(xs, *ys_parts)


def kernel(x, meta_tokens, norm_w, w_in, b_in, lb_logits, hg_norm_w, pool_w, pool_scale, w_down_hg, w_down_pool, w_out, final_norm_w, loss_target, m_meta_tokens, m_norm_w, m_w_in, m_b_in, m_lb_logits, m_hg_norm_w, m_pool_w, m_pool_scale, m_w_down_hg, m_w_down_pool, m_w_out, m_final_norm_w, v_meta_tokens, v_norm_w, v_w_in, v_b_in, v_lb_logits, v_hg_norm_w, v_pool_w, v_pool_scale, v_w_down_hg, v_w_down_pool, v_w_out, v_final_norm_w):
    seq = x.shape[1]
    ac = lax.axis_index("c")

    meta_full = _all_gather_small(meta_tokens).transpose(1, 0, 2).reshape(N_META, D_MODEL)
    w_rest = jnp.concatenate([w_down_hg[0].astype(BF16), w_down_pool[0].astype(BF16), w_out[0].astype(BF16),
                              pool_w[0].astype(BF16).reshape(32, 1024)], axis=0)

    zp = jnp.concatenate([jnp.zeros((PAD_ROWS, D_MODEL), F32), meta_full, x[0]], axis=0)
    tgt = jnp.concatenate([jnp.zeros((CHUNK, D_MODEL), F32), loss_target[0]], axis=0)
    h = _first_norm(zp, norm_w)
    p, w_blocks = _in_projection(h, w_in[0].astype(BF16), b_in.reshape(N_COLBLK, 1, 1024), _gather_order())
    o, states, scores, rest = _hgrn_forward(p, lb_logits, w_rest)
    wdh = rest[:, REST_W_DOWN_HG:REST_W_DOWN_HG + 128].reshape(1024, 1024)
    wdp = rest[:, REST_W_DOWN_POOL:REST_W_DOWN_POOL + 128].reshape(1024, 1024)
    wout = rest[:, REST_W_OUT:REST_W_OUT + 128].reshape(1024, 1024)
    pw = rest[:, REST_POOL_W:REST_POOL_W + 32].reshape(N_DEV, 4, 32, 256).transpose(1, 0, 2, 3).reshape(4, 256, 256)
    (d_o, dp_b, dz2, grad_lhs, grad_rhs, dbias_b, d_hgw, d_pw, d_ps, d_fnw, loss_part) = _tail(
        p, o, zp, tgt, hg_norm_w, pw, pool_scale, wdh, wdp, wout, final_norm_w.reshape(1, D_MODEL))
    dp_a, dbias_a, d_lb = _hgrn_backward(p, lb_logits, states, scores, d_o)
    dz, d_nw = _in_projection_backward(dp_a, dp_b, w_blocks, zp, dz2, norm_w)
    g_in = _weight_grad(h, [dp_a, dp_b], "weight_grad_in")
    g_rows = _weight_grad(grad_lhs, [grad_rhs], "weight_grad_rows")

    lb = jax.nn.sigmoid(lb_logits[0:1] - lb_logits[1:2])
    d_l0 = d_lb * lb * (1.0 - lb)
    replicated = jnp.concatenate([dbias_a.reshape(3, 1024), dbias_b.reshape(5, 1024), d_nw, d_l0, -d_l0, d_hgw, d_ps, d_fnw,
                                  jnp.zeros((MISC_ROWS - MISC_FINAL_NORM_W - 1, 1024), F32)], axis=0)
    d_meta = dz[PAD_ROWS:CHUNK].reshape(N_META, N_DEV, 128).transpose(1, 0, 2)
    d_pw_blocks = d_pw.reshape(4, N_DEV, 32, 256).transpose(1, 0, 2, 3).reshape(N_DEV, 32, 1024)
    g_misc = jnp.concatenate([d_pw_blocks, jnp.pad(d_meta, ((0, 0), (0, 0), (0, 1024 - 128))),
                              jnp.broadcast_to(replicated[None], (N_DEV, 16, 1024))], axis=1)

    own_slots = (4 * (jnp.arange(4, dtype=jnp.int32) // 2) + 2 * (jnp.arange(4, dtype=jnp.int32) % 2) + ac).astype(jnp.int32)
    from_sibling = _exchange_sibling(g_in, g_rows, g_misc)
    chip_part = _pair_sum(g_in, g_rows, g_misc, *from_sibling, own_slots)
    from_chips = _exchange_chips(*chip_part)

    as_rows = lambda t, n: t.reshape(n, 1024)
    small = [(MISC_POOL_W, 1024, tuple(as_rows(t, 32) for t in (pool_w, m_pool_w, v_pool_w))),
             (MISC_META, 128, (meta_tokens, m_meta_tokens, v_meta_tokens)),
             (MISC_B_IN, 1024, tuple(as_rows(t, 8) for t in (b_in, m_b_in, v_b_in))),
             (MISC_NORM_W, 1024, (norm_w, m_norm_w, v_norm_w)),
             (MISC_LB, 1024, (lb_logits, m_lb_logits, v_lb_logits)),
             (MISC_HG_NORM_W, 1024, (hg_norm_w, m_hg_norm_w, v_hg_norm_w)),
             (MISC_POOL_SCALE, 1024, (pool_scale, m_pool_scale, v_pool_scale)),
             (MISC_FINAL_NORM_W, 1024, tuple(as_rows(t, 1) for t in (final_norm_w, m_final_norm_w, v_final_norm_w)))]
    res = _finish(*from_chips, (w_in, m_w_in, v_w_in),
                  [(w_out, m_w_out, v_w_out), (w_down_hg, m_w_down_hg, v_w_down_hg), (w_down_pool, m_w_down_pool, v_w_down_pool)],
                  small)
    r_w_in, r_w_out, r_wdh, r_wdp, r_pw, r_meta, r_b_in, r_nw, r_lb, r_hgw, r_ps, r_fnw = res
    loss = lax.psum(loss_part[0, 0], ("x", "y", "c"))
    grad_x = dz[CHUNK:].reshape(1, seq, D_MODEL)
    per_kind = [(r_meta[k], r_nw[k], r_w_in[k], r_b_in[k].reshape(1, 8192), r_lb[k], r_hgw[k], r_pw[k].reshape(1, 4, 32, 256),
                 r_ps[k], r_wdh[k], r_wdp[k], r_w_out[k], r_fnw[k].reshape(1024)) for k in range(4)]
    return (loss, grad_x, *per_kind[0], *per_kind[1], *per_kind[2], *per_kind[3])
```

```python
import functools

import jax
import jax.numpy as jnp
from jax import lax
from jax.experimental import pallas as pl
from jax.experimental.pallas import tpu as pltpu

F32 = jnp.float32
BF16 = jnp.bfloat16

D_MODEL = 1024
N_META = 16
HEADS = 8
HEAD_DIM = 128
CHUNK = 64
SUB = 16
N_SUB = CHUNK // SUB
PAD_ROWS = CHUNK - N_META
POOL_WINDOWS = (2, 4, 8, 16)
POOL_GDIM = D_MODEL // len(POOL_WINDOWS)
HALO = 16
EPS = 1e-6
N_DEV = 8
N_COLBLK = 8
ADAM_LR, ADAM_B1, ADAM_B2, ADAM_EPS, ADAM_WD, ADAM_STEP = 0.001, 0.9, 0.999, 1e-08, 0.01, 10

VMEM_LIMIT = 56 * 1024 * 1024
MESH = pl.DeviceIdType.MESH
ANY = pl.BlockSpec(memory_space=pl.ANY)
HIGHEST = lax.Precision.HIGHEST
LOG2_E = 1.4426950408889634

REST_W_DOWN_HG = 0
REST_W_DOWN_POOL = 128
REST_W_OUT = 256
REST_POOL_W = 384
MISC_POOL_W = 0
MISC_META = 32
MISC_B_IN = 48
MISC_NORM_W = 56
MISC_LB = 57
MISC_HG_NORM_W = 59
MISC_POOL_SCALE = 60
MISC_FINAL_NORM_W = 61
MISC_ROWS = 64


def _params(sem=None):
    return pltpu.CompilerParams(dimension_semantics=sem, vmem_limit_bytes=VMEM_LIMIT)


def _row_tile(n_rows, prefer):
    best = 16
    for t in range(16, prefer + 1, 16):
        if n_rows % t == 0:
            best = t
    return best


def _sigmoid_pair(x):
    e = jnp.exp(-jnp.abs(x))
    r = 1.0 / (1.0 + e)
    er = e * r
    pos = x >= 0
    return jnp.where(pos, r, er), jnp.where(pos, er, r)


def _dot(a, b):
    return jnp.dot(a.astype(BF16), b.astype(BF16), preferred_element_type=F32)


def _dot_nt(a, b):
    return lax.dot_general(a.astype(BF16), b.astype(BF16), (((1,), (1,)), ((), ())), preferred_element_type=F32)


def _dot_tn(a, b):
    return lax.dot_general(a.astype(BF16), b.astype(BF16), (((0,), (0,)), ((), ())), preferred_element_type=F32)


def _device_index(px, py, pc):
    return 4 * px + 2 * py + pc


def _direct_gather(src_ref, dst_ref, send_sems, recv_sems, local_sem):
    x, y, c = lax.axis_index("x"), lax.axis_index("y"), lax.axis_index("c")
    own = pltpu.make_async_copy(src_ref, dst_ref.at[_device_index(x, y, c)], local_sem)
    sends, arrivals = [], []
    for k in range(1, N_DEV):
        peer = (1 - x if k & 4 else x, 1 - y if k & 2 else y, 1 - c if k & 1 else c)
        for slot, out in ((_device_index(x, y, c), sends), (_device_index(*peer), arrivals)):
            out.append(pltpu.make_async_remote_copy(
                src_ref=src_ref, dst_ref=dst_ref.at[slot], send_sem=send_sems.at[k - 1], recv_sem=recv_sems.at[k - 1],
                device_id=peer, device_id_type=MESH))
    return own, sends, arrivals


GATHER_SEMS = [pltpu.SemaphoreType.DMA((N_DEV - 1,)), pltpu.SemaphoreType.DMA((N_DEV - 1,)), pltpu.SemaphoreType.DMA(())]


def _all_gather_small(block):
    def body(x_ref, out_ref, send_sems, recv_sems, local_sem):
        own, sends, arrivals = _direct_gather(x_ref, out_ref, send_sems, recv_sems, local_sem)
        own.start()
        for cp in sends:
            cp.start()
        for cp in arrivals:
            cp.wait_recv()
        for cp in sends:
            cp.wait_send()
        own.wait()

    return pl.pallas_call(
        body, name="all_gather_meta",
        out_shape=jax.ShapeDtypeStruct((N_DEV,) + block.shape, block.dtype),
        in_specs=[ANY], out_specs=ANY, scratch_shapes=list(GATHER_SEMS),
    )(block)


def _peer(k):
    x, y, c = lax.axis_index("x"), lax.axis_index("y"), lax.axis_index("c")
    return (1 - x if k & 4 else x, 1 - y if k & 2 else y, 1 - c if k & 1 else c)


def _adam_update(g, w, m, v):
    mn = ADAM_B1 * m + (1.0 - ADAM_B1) * g
    vn = ADAM_B2 * v + (1.0 - ADAM_B2) * (g * g)
    m_hat = mn / (1.0 - ADAM_B1 ** ADAM_STEP)
    v_hat = vn / (1.0 - ADAM_B2 ** ADAM_STEP)
    return -ADAM_LR * (m_hat / (jnp.sqrt(v_hat) + ADAM_EPS) + ADAM_WD * w), mn, vn


def _device_sum(parts):
    t = [p.astype(F32) for p in parts]
    return ((t[0] + t[1]) + (t[2] + t[3])) + ((t[4] + t[5]) + (t[6] + t[7]))


def _finish(a3, b3, m3, big, rows3, small):
    n_steps = 4
    tb, tr3 = 1024 // n_steps, 128 // n_steps

    def body(*refs):
        it = iter(refs)
        a_ref, b_ref, m_ref = next(it), next(it), next(it)
        big_in = [next(it) for _ in range(3)]
        rows_in = [[next(it) for _ in range(3)] for _ in rows3]
        small_in = [[next(it) for _ in range(3)] for _ in small]
        big_out = [next(it) for _ in range(4)]
        rows_out = [[next(it) for _ in range(4)] for _ in rows3]
        small_out = [[next(it) for _ in range(4)] for _ in small]

        def apply(g, ins, outs):
            d, mn, vn = _adam_update(g, ins[0][...], ins[1][...], ins[2][...])
            for r, val in zip(outs, (g, d, mn, vn)):
                r[...] = val

        apply(_device_sum([a_ref[s] for s in range(N_DEV)])[None], big_in, big_out)
        for k in range(len(rows3)):
            apply(_device_sum([b_ref[s, k] for s in range(N_DEV)])[None], rows_in[k], rows_out[k])

        @pl.when(pl.program_id(0) == 0)
        def _():
            for (row0, lanes, ins), r_in, r_out in zip(small, small_in, small_out):
                n = ins[0].shape[0]
                apply(_device_sum([m_ref[s, row0:row0 + n, :lanes] for s in range(N_DEV)]), r_in, r_out)

    whole = lambda shape: pl.BlockSpec(shape, lambda i: (0,) * len(shape))
    big_blk = pl.BlockSpec((1, tb, 1024), lambda i: (0, i, 0))
    rows_blk = pl.BlockSpec((1, tr3, 1024), lambda i: (0, i, 0))
    in_specs = [pl.BlockSpec((N_DEV, tb, 1024), lambda i: (0, i, 0)),
                pl.BlockSpec((N_DEV, 3, tr3, 1024), lambda i: (0, 0, i, 0)), whole(m3.shape)]
    in_specs += [big_blk] * 3 + [rows_blk] * (3 * len(rows3))
    out_specs = [big_blk] * 4 + [rows_blk] * (4 * len(rows3))
    out_shape = [jax.ShapeDtypeStruct(big[0].shape, F32)] * 4
    for w, _, _ in rows3:
        out_shape += [jax.ShapeDtypeStruct(w.shape, F32)] * 4
    args = [a3, b3, m3, *big]
    for t in rows3:
        args += list(t)
    for _, _, t in small:
        in_specs += [whole(t[0].shape)] * 3
        out_specs += [whole(t[0].shape)] * 4
        out_shape += [jax.ShapeDtypeStruct(t[0].shape, F32)] * 4
        args += list(t)
    outs = pl.pallas_call(
        body, name="reduce_sum_adamw", out_shape=tuple(out_shape), grid=(n_steps,),
        in_specs=in_specs, out_specs=tuple(out_specs), compiler_params=_params(("arbitrary",)),
    )(*args)
    return [tuple(outs[4 * k:4 * k + 4]) for k in range(len(outs) // 4)]


def _first_norm(zp, norm_w):
    n_rows = zp.shape[0]
    tr = _row_tile(n_rows, 832)

    def body(z_ref, nw_ref, h_ref):
        z = z_ref[...]
        r = lax.rsqrt(jnp.mean(z * z, axis=-1, keepdims=True) + EPS)
        h_ref[...] = (z * r * nw_ref[...]).astype(BF16)

    return pl.pallas_call(
        body, name="first_norm",
        out_shape=jax.ShapeDtypeStruct((n_rows, D_MODEL), BF16),
        grid=(n_rows // tr,),
        in_specs=[pl.BlockSpec((tr, D_MODEL), lambda i: (i, 0)), pl.BlockSpec((1, D_MODEL), lambda i: (0, 0))],
        out_specs=pl.BlockSpec((tr, D_MODEL), lambda i: (i, 0)),
        compiler_params=_params(("arbitrary",)),
    )(zp, norm_w)


def _gather_order():
    x, y, c = lax.axis_index("x"), lax.axis_index("y"), lax.axis_index("c")
    chips = [(1 - x, y), (x, 1 - y), (1 - x, 1 - y)]
    order = [_device_index(x, y, c), _device_index(x, y, 1 - c)]
    order += [_device_index(*q, c) for q in chips] + [_device_index(*q, 1 - c) for q in chips]
    return jnp.stack(order).astype(jnp.int32)


def _in_projection(h, w_shard, b_blocks, order):
    n_rows = h.shape[0]
    tr = _row_tile(n_rows, 320)
    nt = n_rows // tr

    def body(order_ref, h_ref, w_hbm, b_ref, p_ref, w_out, w_vmem, send_sems, recv_sems, local_sem, out_sems):
        s, i = pl.program_id(0), pl.program_id(1)
        x, y, c = lax.axis_index("x"), lax.axis_index("y"), lax.axis_index("c")
        me, sibling = (x, y, c), (x, y, 1 - c)
        chips = [(1 - x, y), (x, 1 - y), (1 - x, 1 - y)]

        def slot(px, py, pc):
            return w_vmem.at[_device_index(px, py, pc)]

        def copy(k, blk, to, src=None):
            return pltpu.make_async_remote_copy(
                src_ref=slot(*blk) if src is None else src, dst_ref=slot(*blk),
                send_sem=send_sems.at[k], recv_sem=recv_sems.at[k], device_id=to, device_id_type=MESH)

        own = pltpu.make_async_copy(w_hbm, slot(*me), local_sem)
        first = [copy(0, me, sibling, src=w_hbm)] + [copy(1 + j, me, (*q, c), src=w_hbm) for j, q in enumerate(chips)]
        passed = [copy(4 + j, (*q, c), sibling) for j, q in enumerate(chips)]
        arrivals = [None, copy(0, sibling, me)]
        arrivals += [copy(1 + j, (*q, c), me) for j, q in enumerate(chips)]
        arrivals += [copy(4 + j, (*q, 1 - c), me) for j, q in enumerate(chips)]

        def keep(step):
            return pltpu.make_async_copy(w_vmem.at[order_ref[step]], w_out.at[order_ref[step]], out_sems.at[step])

        for step in range(N_DEV):
            @pl.when((i == 0) & (s == step))
            def _(step=step):
                if step == 0:
                    own.start()
                    for cp in first:
                        cp.start()
                    own.wait()
                else:
                    arrivals[step].wait_recv()
                    if 2 <= step <= 4:
                        passed[step - 2].start()
                keep(step).start()

        p_ref[0] = jnp.dot(h_ref[...], w_vmem[order_ref[s]], preferred_element_type=F32) + b_ref[0]

        @pl.when((s == N_DEV - 1) & (i == nt - 1))
        def _():
            for cp in first + passed:
                cp.wait_send()
            for step in range(N_DEV):
                keep(step).wait()

    return pl.pallas_call(
        body, name="in_projection_gather",
        out_shape=(jax.ShapeDtypeStruct((N_COLBLK, n_rows, 1024), F32),
                   jax.ShapeDtypeStruct((N_DEV, D_MODEL, 1024), BF16)),
        grid_spec=pltpu.PrefetchScalarGridSpec(
            num_scalar_prefetch=1, grid=(N_DEV, nt),
            in_specs=[pl.BlockSpec((tr, D_MODEL), lambda s, i, o: (i, 0)), ANY,
                      pl.BlockSpec((1, 1, 1024), lambda s, i, o: (o[s], 0, 0))],
            out_specs=(pl.BlockSpec((1, tr, 1024), lambda s, i, o: (o[s], i, 0)), ANY),
            scratch_shapes=[pltpu.VMEM((N_DEV, D_MODEL, 1024), BF16), pltpu.SemaphoreType.DMA((7,)),
                            pltpu.SemaphoreType.DMA((7,)), pltpu.SemaphoreType.DMA(()), pltpu.SemaphoreType.DMA((N_DEV,))]),
        compiler_params=_params(("arbitrary", "arbitrary")),
    )(order, h, w_shard, b_blocks)


def _lower_bound(lb_ref):
    l0, l1 = lb_ref[0:1, :], lb_ref[1:2, :]
    _, lb = _sigmoid_pair(l1 - l0)
    return lb


def _chunk_gates(fz, lb, valid):
    sig, nsig = _sigmoid_pair(fz)
    f = lb + (1.0 - lb) * sig
    g = jnp.where(valid, jnp.log(f), 0.0)
    k = jnp.where(valid, (1.0 - lb) * nsig, 0.0)
    return sig, nsig, f, g, k


def _tri(n, upper=False):
    r = lax.broadcasted_iota(jnp.int32, (n, n), 0)
    c = lax.broadcasted_iota(jnp.int32, (n, n), 1)
    return jnp.where((r <= c) if upper else (r >= c), 1.0, 0.0).astype(F32)


def _intra_scores(q_ref, k_ref, b2_ref, a_ref, kt_ref, col0):
    cols = pl.ds(col0, HEAD_DIM)
    rows_s = lax.broadcasted_iota(jnp.int32, (SUB, 1), 0)
    lanes_c = lax.broadcasted_iota(jnp.int32, (1, CHUNK), 1)
    kt_ref[...] = jnp.zeros_like(kt_ref)
    for i in range(N_SUB):
        lo = i * SUB
        qi = q_ref[lo:lo + SUB, cols]
        bi = b2_ref[lo:lo + SUB, cols]
        if i == 0:
            acc = jnp.zeros((SUB, CHUNK), F32)
        else:
            ref_i = b2_ref[lo:lo + 1, cols]
            qt = qi * jnp.exp2(bi - ref_i)
            kt_ref[0:lo, :] = k_ref[0:lo, cols] * jnp.exp2(ref_i - b2_ref[0:lo, cols])
            acc = _dot_nt(qt, kt_ref[...])
        for s in range(SUB):
            b_s = b2_ref[lo + s:lo + s + 1, cols]
            k_s = k_ref[lo + s:lo + s + 1, cols]
            w = jnp.exp2(jnp.minimum(bi - b_s, 0.0))
            col = jnp.sum((qi * w) * k_s, axis=-1, keepdims=True)
            acc = jnp.where(lanes_c == lo + s, col, acc)
        a_ref[lo:lo + SUB, :] = jnp.where(lanes_c <= lo + rows_s, acc, 0.0)


def _hgrn_forward(p, lb_logits, w_rest):
    n_rows = p.shape[1]
    n_chunks = n_rows // CHUNK
    width = HEADS * HEAD_DIM

    def body(q_ref, fz_ref, v_ref, lb_ref, rest_ref, o_ref, st_out_ref, a_out_ref, rest_out,
             state, k_vmem, b2_vmem, a_vmem, kt_vmem, send_sems, recv_sems, local_sem):
        n = pl.program_id(0)
        own, sends, arrivals = _direct_gather(rest_ref, rest_out, send_sems, recv_sems, local_sem)

        @pl.when(n == 0)
        def _():
            state[...] = jnp.zeros_like(state)
            own.start()
            for cp in sends:
                cp.start()

        rows = n * CHUNK + lax.broadcasted_iota(jnp.int32, (CHUNK, 1), 0)
        valid = rows >= PAD_ROWS
        lb = _lower_bound(lb_ref)
        _, _, _, g, k = _chunk_gates(fz_ref[0], lb, valid)
        k_vmem[...] = k
        b2_vmem[...] = jnp.dot(_tri(CHUNK), g, precision=HIGHEST, preferred_element_type=F32) * LOG2_E
        q_view = q_ref.at[0]
        for h in range(HEADS):
            cols = pl.ds(h * HEAD_DIM, HEAD_DIM)
            st = state[h]
            st_out_ref[0, h] = st
            bh = b2_vmem[:, cols]
            kh = k_vmem[:, cols]
            vh = jnp.where(valid, v_ref[0, :, cols], 0.0)
            qe = q_ref[0, :, cols] * jnp.exp2(bh)
            _intra_scores(q_view, k_vmem, b2_vmem, a_vmem, kt_vmem, h * HEAD_DIM)
            a = a_vmem[...].astype(BF16)
            a_out_ref[0, h] = a
            o_ref[:, cols] = _dot_nt(qe, st) + _dot(a, vh)
            b_last = b2_vmem[CHUNK - 1:CHUNK, cols]
            kd = kh * jnp.exp2(b_last - bh)
            state[h] = st * jnp.exp2(b_last) + _dot_tn(vh, kd)

        @pl.when(n == n_chunks - 1)
        def _():
            for cp in arrivals:
                cp.wait_recv()
            for cp in sends:
                cp.wait_send()
            own.wait()

    blk = lambda c: pl.BlockSpec((1, CHUNK, width), lambda n, c=c: (c, n, 0))
    return pl.pallas_call(
        body, name="hgrn_forward",
        out_shape=(jax.ShapeDtypeStruct((n_rows, width), F32),
                   jax.ShapeDtypeStruct((n_chunks, HEADS, HEAD_DIM, HEAD_DIM), F32),
                   jax.ShapeDtypeStruct((n_chunks, HEADS, CHUNK, CHUNK), BF16),
                   jax.ShapeDtypeStruct((N_DEV,) + w_rest.shape, w_rest.dtype)),
        grid=(n_chunks,),
        in_specs=[blk(0), blk(1), blk(2), pl.BlockSpec((2, width), lambda n: (0, 0)), ANY],
        out_specs=(pl.BlockSpec((CHUNK, width), lambda n: (n, 0)),
                   pl.BlockSpec((1, HEADS, HEAD_DIM, HEAD_DIM), lambda n: (n, 0, 0, 0)),
                   pl.BlockSpec((1, HEADS, CHUNK, CHUNK), lambda n: (n, 0, 0, 0)), ANY),
        scratch_shapes=[pltpu.VMEM((HEADS, HEAD_DIM, HEAD_DIM), F32), pltpu.VMEM((CHUNK, width), F32),
                        pltpu.VMEM((CHUNK, width), F32), pltpu.VMEM((CHUNK, CHUNK), F32),
                        pltpu.VMEM((CHUNK, HEAD_DIM), F32)] + list(GATHER_SEMS),
        compiler_params=_params(("arbitrary",)),
    )(p, p, p, lb_logits, w_rest)


def _hgrn_backward(p, lb_logits, states, scores, d_o):
    n_rows = p.shape[1]
    n_chunks = n_rows // CHUNK
    width = HEADS * HEAD_DIM

    def body(q_ref, fz_ref, v_ref, lb_ref, st_ref, a_ref, do_ref, dp_ref, dbias_ref, dlb_ref,
             dstate, k_vmem, b2_vmem, kt_vmem, dqi_vmem, dki_vmem):
        step = pl.program_id(0)
        n = n_chunks - 1 - step

        @pl.when(step == 0)
        def _():
            dstate[...] = jnp.zeros_like(dstate)
            dbias_ref[...] = jnp.zeros_like(dbias_ref)
            dlb_ref[...] = jnp.zeros_like(dlb_ref)

        rows = n * CHUNK + lax.broadcasted_iota(jnp.int32, (CHUNK, 1), 0)
        valid = rows >= PAD_ROWS
        lb = _lower_bound(lb_ref)
        sig, nsig, f, g, k = _chunk_gates(fz_ref[0], lb, valid)
        k_vmem[...] = k
        b2_vmem[...] = jnp.dot(_tri(CHUNK), g, precision=HIGHEST, preferred_element_type=F32) * LOG2_E
        rows_c = lax.broadcasted_iota(jnp.int32, (CHUNK, 1), 0)
        lanes_c = lax.broadcasted_iota(jnp.int32, (1, CHUNK), 1)
        causal = lax.broadcasted_iota(jnp.int32, (CHUNK, CHUNK), 0) >= lax.broadcasted_iota(jnp.int32, (CHUNK, CHUNK), 1)
        tri_up = _tri(CHUNK, upper=True)
        for h in range(HEADS):
            cols = pl.ds(h * HEAD_DIM, HEAD_DIM)
            st = st_ref[0, h]
            dst = dstate[h]
            qh = q_ref[0, :, cols]
            bh = b2_vmem[:, cols]
            kh = k_vmem[:, cols]
            vh = jnp.where(valid, v_ref[0, :, cols], 0.0)
            doh = do_ref[:, cols]
            eb = jnp.exp2(bh)
            qe = qh * eb
            b_last = b2_vmem[CHUNK - 1:CHUNK, cols]
            e_last = jnp.exp2(b_last)
            decay_k = jnp.exp2(b_last - bh)
            kd = kh * decay_k
            dqe = _dot(doh, st)
            da = jnp.where(causal, _dot_nt(doh, vh), 0.0)
            dv = _dot_tn(a_ref[0, h], doh) + _dot_nt(kd, dst)
            dkd = _dot(vh, dst)
            dstate[h] = dst * e_last + _dot_tn(doh, qe)
            db_last = (jnp.sum(dst * st, axis=0, keepdims=True) * e_last
                       + jnp.sum(dkd * kd, axis=0, keepdims=True))
            dki_vmem[...] = jnp.zeros_like(dki_vmem)
            kt_vmem[...] = jnp.zeros_like(kt_vmem)
            for i in range(N_SUB):
                lo = i * SUB
                qi = q_ref[0, lo:lo + SUB, cols]
                bi = b2_vmem[lo:lo + SUB, cols]
                da_i = da[lo:lo + SUB, :]
                if i == 0:
                    dq_i = jnp.zeros((SUB, HEAD_DIM), F32)
                else:
                    ref_i = b2_vmem[lo:lo + 1, cols]
                    eq = jnp.exp2(bi - ref_i)
                    ek = jnp.exp2(ref_i - b2_vmem[0:lo, cols])
                    kt_vmem[0:lo, :] = k_vmem[0:lo, cols] * ek
                    dq_i = _dot(da_i, kt_vmem[...]) * eq
                    dki_vmem[0:lo, :] += _dot_tn(da_i, qi * eq)[0:lo] * ek
                for s in range(SUB):
                    b_s = b2_vmem[lo + s:lo + s + 1, cols]
                    k_s = k_vmem[lo + s:lo + s + 1, cols]
                    w = jnp.exp2(jnp.minimum(bi - b_s, 0.0))
                    da_col = jnp.sum(jnp.where(lanes_c == lo + s, da_i, 0.0), axis=-1, keepdims=True)
                    gw = da_col * w
                    dq_i = dq_i + gw * k_s
                    dki_vmem[lo + s:lo + s + 1, :] += jnp.sum(gw * qi, axis=0, keepdims=True)
                dqi_vmem[lo:lo + SUB, :] = dq_i
            dq_intra = dqi_vmem[...]
            dk_intra = dki_vmem[...]
            dq = dqe * eb + dq_intra
            dk = dkd * decay_k + dk_intra
            db = dqe * qe - dkd * kd + qh * dq_intra - kh * dk_intra
            db = db + jnp.where(rows_c == CHUNK - 1, db_last, 0.0)
            dg = jnp.dot(tri_up, db, precision=HIGHEST, preferred_element_type=F32)
            fh = f[:, h * HEAD_DIM:(h + 1) * HEAD_DIM]
            sh = sig[:, h * HEAD_DIM:(h + 1) * HEAD_DIM]
            nh = nsig[:, h * HEAD_DIM:(h + 1) * HEAD_DIM]
            lbh = lb[:, h * HEAD_DIM:(h + 1) * HEAD_DIM]
            df = jnp.where(valid, dg / fh - dk, 0.0)
            dfz = df * (1.0 - lbh) * sh * nh
            dq = jnp.where(valid, dq, 0.0)
            dv = jnp.where(valid, dv, 0.0)
            dlb_ref[:, cols] += jnp.sum(df * nh, axis=0, keepdims=True)
            dp_ref[0, :, cols] = dq.astype(BF16)
            dp_ref[1, :, cols] = dfz.astype(BF16)
            dp_ref[2, :, cols] = dv.astype(BF16)
            dbias_ref[0, :, cols] += jnp.sum(dq, axis=0, keepdims=True)
            dbias_ref[1, :, cols] += jnp.sum(dfz, axis=0, keepdims=True)
            dbias_ref[2, :, cols] += jnp.sum(dv, axis=0, keepdims=True)

    rev = lambda s: n_chunks - 1 - s
    blk = lambda c: pl.BlockSpec((1, CHUNK, width), lambda s, c=c: (c, rev(s), 0))
    return pl.pallas_call(
        body, name="hgrn_backward",
        out_shape=(jax.ShapeDtypeStruct((3, n_rows, width), BF16),
                   jax.ShapeDtypeStruct((3, 1, width), F32),
                   jax.ShapeDtypeStruct((1, width), F32)),
        grid=(n_chunks,),
        in_specs=[blk(0), blk(1), blk(2), pl.BlockSpec((2, width), lambda s: (0, 0)),
                  pl.BlockSpec((1, HEADS, HEAD_DIM, HEAD_DIM), lambda s: (rev(s), 0, 0, 0)),
                  pl.BlockSpec((1, HEADS, CHUNK, CHUNK), lambda s: (rev(s), 0, 0, 0)),
                  pl.BlockSpec((CHUNK, width), lambda s: (rev(s), 0))],
        out_specs=(pl.BlockSpec((3, CHUNK, width), lambda s: (0, rev(s), 0)),
                   pl.BlockSpec((3, 1, width), lambda s: (0, 0, 0)),
                   pl.BlockSpec((1, width), lambda s: (0, 0))),
        scratch_shapes=[pltpu.VMEM((HEADS, HEAD_DIM, HEAD_DIM), F32), pltpu.VMEM((CHUNK, width), F32),
                        pltpu.VMEM((CHUNK, width), F32), pltpu.VMEM((CHUNK, HEAD_DIM), F32),
                        pltpu.VMEM((CHUNK, HEAD_DIM), F32), pltpu.VMEM((CHUNK, HEAD_DIM), F32)],
        compiler_params=_params(("arbitrary",)),
    )(p, p, p, lb_logits, states, scores, d_o)


def _silu_and_grad(x):
    s, ns = _sigmoid_pair(x)
    return x * s, s * (1.0 + x * ns)


def _tail(p, o, zp, tgt, hg_norm_w, pool_w, pool_scale, w_down_hg, w_down_pool, w_out, final_norm_w):
    n_rows = zp.shape[0]
    tr = _row_tile(n_rows, 160)
    nt = n_rows // tr
    ext = tr + HALO
    n_groups = len(POOL_WINDOWS)

    def body(o_ref, ghg_ref, u_ref, gpool_ref, mhg_ref, mpool_ref, uhalo_ref, z_ref, tgt_ref,
             hgw_ref, pw_ref, ps_ref, wdh_ref, wdp_ref, wout_ref, fnw_ref,
             do_ref, dp_ref, dz2_ref, lhs_ref, rhs_ref,
             dbias_ref, dhgw_ref, dpw_ref, dps_ref, dfnw_ref, loss_ref, halo_vmem):
        step = pl.program_id(0)
        ti = nt - 1 - step

        @pl.when(step == 0)
        def _():
            halo_vmem[...] = jnp.zeros_like(halo_vmem)
            for r in (dbias_ref, dhgw_ref, dpw_ref, dps_ref, dfnw_ref, loss_ref):
                r[...] = jnp.zeros_like(r)

        rows = ti * tr + lax.broadcasted_iota(jnp.int32, (tr, 1), 0)
        valid = rows >= PAD_ROWS
        in_loss = rows >= CHUNK
        count_pos = jnp.maximum(rows - PAD_ROWS + 1, 1).astype(F32)

        o = o_ref[...]
        hgw = hgw_ref[...]
        inv_o, on_parts = [], []
        for h in range(HEADS):
            oh = o[:, h * HEAD_DIM:(h + 1) * HEAD_DIM]
            r = lax.rsqrt(jnp.mean(oh * oh, axis=-1, keepdims=True) + EPS)
            inv_o.append(r)
            on_parts.append(oh * r)
        o_hat = jnp.concatenate(on_parts, axis=1)
        o_n = o_hat * hgw
        g_hg = ghg_ref[0]
        silu_hg, dsilu_hg = _silu_and_grad(g_hg)
        a_hg = o_n * silu_hg
        y_hg = _dot(a_hg, wdh_ref[...])

        u = jnp.where(valid, u_ref[0], 0.0)
        u_prev = jnp.where(ti > 0, uhalo_ref[0], 0.0)
        u_ext = jnp.concatenate([u_prev, u], axis=0)
        pooled_parts, mixed_parts, inv_cnt = [], [], []
        for gi, win in enumerate(POOL_WINDOWS):
            lanes = slice(gi * POOL_GDIM, (gi + 1) * POOL_GDIM)
            s = u_ext[:, lanes]
            shift = 1
            while shift < win:
                s = s + pltpu.roll(s, shift, 0)
                shift *= 2
            ic = 1.0 / jnp.minimum(count_pos, float(win))
            inv_cnt.append(ic)
            pooled = s[HALO:] * ic - u[:, lanes]
            pooled_parts.append(pooled)
            mixed_parts.append(_dot(pooled, pw_ref[gi]))
        mixed = jnp.concatenate(mixed_parts, axis=1)
        ps = ps_ref[...]
        g_pool = gpool_ref[0]
        silu_pool, dsilu_pool = _silu_and_grad(g_pool)
        a_pool = mixed * ps * silu_pool
        y_pool = _dot(a_pool, wdp_ref[...])

        m_hg, m_pool = mhg_ref[0], mpool_ref[0]
        s_hg, ns_hg = _sigmoid_pair(m_hg)
        s_pool, ns_pool = _sigmoid_pair(m_pool)
        merged = s_hg * y_hg + s_pool * y_pool
        z2 = z_ref[...] + _dot(merged, wout_ref[...])
        r2 = lax.rsqrt(jnp.mean(z2 * z2, axis=-1, keepdims=True) + EPS)
        n2 = z2 * r2
        fnw = fnw_ref[...]
        err = jnp.where(in_loss, n2 * fnw - tgt_ref[...], 0.0)
        loss_ref[...] += jnp.sum(jnp.sum(err * err, axis=0, keepdims=True), axis=1, keepdims=True) * (0.5 / D_MODEL)
        dy = err * (1.0 / D_MODEL)

        dfnw_ref[...] += jnp.sum(dy * n2, axis=0, keepdims=True)
        gy = dy * fnw
        dz2 = r2 * (gy - n2 * jnp.mean(gy * n2, axis=-1, keepdims=True))
        dmerged = _dot_nt(dz2, wout_ref[...])
        dy_hg = s_hg * dmerged
        dy_pool = s_pool * dmerged
        dm_hg = dmerged * y_hg * s_hg * ns_hg
        dm_pool = dmerged * y_pool * s_pool * ns_pool
        da_hg = _dot_nt(dy_hg, wdh_ref[...])
        da_pool = _dot_nt(dy_pool, wdp_ref[...])

        d_on = da_hg * silu_hg
        dg_hg = da_hg * o_n * dsilu_hg
        dhgw_ref[...] += jnp.sum(d_on * o_hat, axis=0, keepdims=True)
        gyo = d_on * hgw
        do_parts = []
        for h in range(HEADS):
            lanes = slice(h * HEAD_DIM, (h + 1) * HEAD_DIM)
            gh, nh = gyo[:, lanes], o_hat[:, lanes]
            do_parts.append(inv_o[h] * (gh - nh * jnp.mean(gh * nh, axis=-1, keepdims=True)))
        do_ref[...] = jnp.concatenate(do_parts, axis=1)

        dmixed = da_pool * ps * silu_pool
        dps_ref[...] += jnp.sum(da_pool * mixed * silu_pool, axis=0, keepdims=True)
        dg_pool = da_pool * mixed * ps * dsilu_pool
        du_parts = []
        for gi, win in enumerate(POOL_WINDOWS):
            lanes = slice(gi * POOL_GDIM, (gi + 1) * POOL_GDIM)
            dmx = dmixed[:, lanes]
            dpooled = _dot_nt(dmx, pw_ref[gi])
            dpw_ref[gi] += _dot_tn(pooled_parts[gi], dmx)
            dpt = dpooled * inv_cnt[gi]
            s = jnp.concatenate([dpt, halo_vmem[:, lanes]], axis=0)
            shift = 1
            while shift < win:
                s = s + pltpu.roll(s, ext - shift, 0)
                shift *= 2
            du_parts.append(s[:tr] - dpooled)
            halo_vmem[:, lanes] = dpt[:HALO]
        du = jnp.where(valid, jnp.concatenate(du_parts, axis=1), 0.0)

        for c, val in enumerate((dg_hg, du, dg_pool, dm_hg, dm_pool)):
            dp_ref[c] = val.astype(BF16)
            dbias_ref[c] += jnp.sum(val, axis=0, keepdims=True)
        dz2_ref[...] = dz2
        for c, (lhs, rhs) in enumerate(((merged, dz2), (a_hg, dy_hg), (a_pool, dy_pool))):
            lhs_ref[c] = lhs.astype(BF16)
            rhs_ref[c] = rhs.astype(BF16)

    rev = lambda s: nt - 1 - s
    rowblk = pl.BlockSpec((tr, D_MODEL), lambda s: (rev(s), 0))
    pblk = lambda c: pl.BlockSpec((1, tr, 1024), lambda s, c=c: (c, rev(s), 0))
    halo_blk = pl.BlockSpec((1, HALO, 1024), lambda s: (4, jnp.maximum(rev(s) * (tr // HALO) - 1, 0), 0))
    full = lambda shape: pl.BlockSpec(shape, lambda s: (0,) * len(shape))
    vec = full((1, D_MODEL))
    mat = full((D_MODEL, D_MODEL))
    act3 = jax.ShapeDtypeStruct((3, n_rows, D_MODEL), BF16)
    act3_blk = pl.BlockSpec((3, tr, D_MODEL), lambda s: (0, rev(s), 0))
    return pl.pallas_call(
        body, name="tail_forward_backward",
        out_shape=(jax.ShapeDtypeStruct((n_rows, D_MODEL), F32),
                   jax.ShapeDtypeStruct((5, n_rows, 1024), BF16),
                   jax.ShapeDtypeStruct((n_rows, D_MODEL), F32),
                   act3, act3,
                   jax.ShapeDtypeStruct((5, 1, 1024), F32),
                   jax.ShapeDtypeStruct((1, D_MODEL), F32),
                   jax.ShapeDtypeStruct((n_groups, POOL_GDIM, POOL_GDIM), F32),
                   jax.ShapeDtypeStruct((1, D_MODEL), F32),
                   jax.ShapeDtypeStruct((1, D_MODEL), F32),
                   jax.ShapeDtypeStruct((1, 1), F32)),
        grid=(nt,),
        in_specs=[rowblk, pblk(3), pblk(4), pblk(5), pblk(6), pblk(7), halo_blk, rowblk, rowblk,
                  vec, full((n_groups, POOL_GDIM, POOL_GDIM)), vec, mat, mat, mat, vec],
        out_specs=(rowblk, pl.BlockSpec((5, tr, 1024), lambda s: (0, rev(s), 0)), rowblk,
                   act3_blk, act3_blk,
                   full((5, 1, 1024)), vec, full((n_groups, POOL_GDIM, POOL_GDIM)), vec, vec, full((1, 1))),
        scratch_shapes=[pltpu.VMEM((HALO, D_MODEL), F32)],
        compiler_params=_params(("arbitrary",)),
    )(o, p, p, p, p, p, p, zp, tgt, hg_norm_w, pool_w, pool_scale, w_down_hg, w_down_pool, w_out, final_norm_w)


def _in_projection_backward(dp_a, dp_b, w_blocks, zp, dz2, norm_w):
    n_rows = zp.shape[0]
    tr = _row_tile(n_rows, 320)
    na, nb = dp_a.shape[0], dp_b.shape[0]

    def body(dpa_ref, dpb_ref, w_hbm, z_ref, dz2_ref, nw_ref, dz_ref, dnw_ref, w_vmem, sem):
        i = pl.program_id(0)

        @pl.when(i == 0)
        def _():
            cp = pltpu.make_async_copy(w_hbm, w_vmem, sem)
            cp.start()
            cp.wait()
            dnw_ref[...] = jnp.zeros_like(dnw_ref)

        dh = jnp.zeros((tr, D_MODEL), F32)
        for j in range(na):
            dh = dh + _dot_nt(dpa_ref[j], w_vmem[j])
        for j in range(nb):
            dh = dh + _dot_nt(dpb_ref[j], w_vmem[na + j])
        z = z_ref[...]
        r = lax.rsqrt(jnp.mean(z * z, axis=-1, keepdims=True) + EPS)
        n1 = z * r
        dnw_ref[...] += jnp.sum(dh * n1, axis=0, keepdims=True)
        gh = dh * nw_ref[...]
        dz_ref[...] = dz2_ref[...] + r * (gh - n1 * jnp.mean(gh * n1, axis=-1, keepdims=True))

    rowblk = pl.BlockSpec((tr, D_MODEL), lambda i: (i, 0))
    vec = pl.BlockSpec((1, D_MODEL), lambda i: (0, 0))
    return pl.pallas_call(
        body, name="in_projection_backward",
        out_shape=(jax.ShapeDtypeStruct((n_rows, D_MODEL), F32), jax.ShapeDtypeStruct((1, D_MODEL), F32)),
        grid=(n_rows // tr,),
        in_specs=[pl.BlockSpec((na, tr, 1024), lambda i: (0, i, 0)), pl.BlockSpec((nb, tr, 1024), lambda i: (0, i, 0)),
                  ANY, rowblk, rowblk, vec],
        out_specs=(rowblk, vec),
        scratch_shapes=[pltpu.VMEM((N_COLBLK, D_MODEL, 1024), BF16), pltpu.SemaphoreType.DMA(())],
        compiler_params=_params(("arbitrary",)),
    )(dp_a, dp_b, w_blocks, zp, dz2, norm_w)


SLAB_PEERS = (4, 2, 6, 5, 3, 7, 1, 0)
N_ROW_GRADS = 3


def _slab_order():
    me = _device_index(lax.axis_index("x"), lax.axis_index("y"), lax.axis_index("c"))
    return jnp.stack([jnp.bitwise_xor(me, k) for k in SLAB_PEERS]).astype(jnp.int32)


def _weight_grads_reduce(h, dp_a, dp_b, lhs3, rhs3, misc, order):
    n_rows = h.shape[0]
    tk = _row_tile(n_rows, 832)
    n_k = n_rows // tk
    n_a = dp_a.shape[0]
    n_slabs = N_DEV + N_ROW_GRADS

    def body(order_ref, h_ref, dpa_ref, dpb_ref, lhs_ref, rhs_ref, misc_hbm, ra, rb, rm,
             acc, send_buf, a_send, a_recv, b_send, b_recv, m_send, m_recv, local_sems):
        t, k = pl.program_id(0), pl.program_id(1)
        me = _device_index(lax.axis_index("x"), lax.axis_index("y"), lax.axis_index("c"))

        def remote(src, dst, send_sem, recv_sem, peer_bits):
            return pltpu.make_async_remote_copy(src_ref=src, dst_ref=dst, send_sem=send_sem, recv_sem=recv_sem,
                                                device_id=_peer(peer_bits), device_id_type=MESH)

        def misc_copies():
            own = pltpu.make_async_copy(misc_hbm.at[me], rm.at[me], local_sems.at[0])
            sends = [remote(misc_hbm.at[jnp.bitwise_xor(me, kk)], rm.at[me], m_send.at[kk - 1], m_recv.at[kk - 1], kk)
                     for kk in range(1, N_DEV)]
            arrivals = [remote(misc_hbm.at[me], rm.at[jnp.bitwise_xor(me, kk)], m_send.at[kk - 1], m_recv.at[kk - 1], kk)
                        for kk in range(1, N_DEV)]
            return own, sends, arrivals

        def slab_copies(step):
            buf = send_buf.at[step % 2]
            if step < N_DEV:
                kk = SLAB_PEERS[step]
                if kk == 0:
                    return [pltpu.make_async_copy(buf, ra.at[me], local_sems.at[1])], []
                return [], [remote(buf, ra.at[me], a_send.at[kk - 1], a_recv.at[kk - 1], kk)]
            m = step - N_DEV
            rows = lambda dev: buf.at[pl.ds(dev * 128, 128), :]
            local = [pltpu.make_async_copy(rows(me), rb.at[me, m], local_sems.at[2 + m])]
            sends = [remote(rows(jnp.bitwise_xor(me, kk)), rb.at[me, m], b_send.at[7 * m + kk - 1], b_recv.at[7 * m + kk - 1], kk)
                     for kk in range(1, N_DEV)]
            return local, sends

        def wait_slab(step):
            local, sends = slab_copies(step)
            for cp in local:
                cp.wait()
            for cp in sends:
                cp.wait_send()

        @pl.when((t == 0) & (k == 0))
        def _():
            own, sends, _ = misc_copies()
            own.start()
            for cp in sends:
                cp.start()

        @pl.when(k == 0)
        def _():
            acc[...] = jnp.zeros_like(acc)

        j = order_ref[jnp.minimum(t, N_DEV - 1)]

        @pl.when((t < N_DEV) & (j < n_a))
        def _():
            acc[...] += _dot_tn(h_ref[...], dpa_ref[0])

        @pl.when((t < N_DEV) & (j >= n_a))
        def _():
            acc[...] += _dot_tn(h_ref[...], dpb_ref[0])

        @pl.when(t >= N_DEV)
        def _():
            acc[...] += _dot_tn(lhs_ref[0], rhs_ref[0])

        for step in range(n_slabs):
            @pl.when((k == n_k - 1) & (t == step))
            def _(step=step):
                if step >= 2:
                    wait_slab(step - 2)
                send_buf[step % 2] = acc[...].astype(BF16)
                local, sends = slab_copies(step)
                for cp in local + sends:
                    cp.start()

        @pl.when((k == n_k - 1) & (t == n_slabs - 1))
        def _():
            wait_slab(n_slabs - 2)
            wait_slab(n_slabs - 1)
            own, sends, arrivals = misc_copies()
            for kk in range(1, N_DEV):
                src = jnp.bitwise_xor(me, kk)
                remote(send_buf.at[0], ra.at[src], a_send.at[kk - 1], a_recv.at[kk - 1], kk).wait_recv()
                for m in range(N_ROW_GRADS):
                    remote(send_buf.at[0, pl.ds(0, 128), :], rb.at[src, m], b_send.at[7 * m + kk - 1],
                           b_recv.at[7 * m + kk - 1], kk).wait_recv()
            for cp in arrivals:
                cp.wait_recv()
            for cp in sends:
                cp.wait_send()
            own.wait()

    is_in = lambda t: t < N_DEV
    slab = lambda t, o: o[jnp.minimum(t, N_DEV - 1)]
    use_a = lambda t, o: is_in(t) & (slab(t, o) < n_a)
    use_b = lambda t, o: is_in(t) & (slab(t, o) >= n_a)
    hold = lambda cond, k: jnp.where(cond, k, 0)
    row = lambda t: jnp.clip(t - N_DEV, 0, N_ROW_GRADS - 1)
    return pl.pallas_call(
        body, name="weight_grads_reduce",
        out_shape=(jax.ShapeDtypeStruct((N_DEV, D_MODEL, 1024), BF16),
                   jax.ShapeDtypeStruct((N_DEV, N_ROW_GRADS, 128, D_MODEL), BF16),
                   jax.ShapeDtypeStruct(misc.shape, misc.dtype)),
        grid_spec=pltpu.PrefetchScalarGridSpec(
            num_scalar_prefetch=1, grid=(n_slabs, n_k),
            in_specs=[pl.BlockSpec((tk, D_MODEL), lambda t, k, o: (hold(is_in(t), k), 0)),
                      pl.BlockSpec((1, tk, 1024), lambda t, k, o: (jnp.clip(slab(t, o), 0, n_a - 1), hold(use_a(t, o), k), 0)),
                      pl.BlockSpec((1, tk, 1024),
                                   lambda t, k, o: (jnp.clip(slab(t, o) - n_a, 0, N_DEV - n_a - 1), hold(use_b(t, o), k), 0)),
                      pl.BlockSpec((1, tk, D_MODEL), lambda t, k, o: (row(t), hold(~is_in(t), k), 0)),
                      pl.BlockSpec((1, tk, D_MODEL), lambda t, k, o: (row(t), hold(~is_in(t), k), 0)),
                      ANY],
            out_specs=(ANY, ANY, ANY),
            scratch_shapes=[pltpu.VMEM((D_MODEL, 1024), F32), pltpu.VMEM((2, D_MODEL, 1024), BF16),
                            pltpu.SemaphoreType.DMA((N_DEV - 1,)), pltpu.SemaphoreType.DMA((N_DEV - 1,)),
                            pltpu.SemaphoreType.DMA((7 * N_ROW_GRADS,)), pltpu.SemaphoreType.DMA((7 * N_ROW_GRADS,)),
                            pltpu.SemaphoreType.DMA((N_DEV - 1,)), pltpu.SemaphoreType.DMA((N_DEV - 1,)),
                            pltpu.SemaphoreType.DMA((2 + N_ROW_GRADS,))]),
        compiler_params=_params(("arbitrary", "arbitrary")),
    )(order, h, dp_a, dp_b, lhs3, rhs3, misc)


def kernel(x, meta_tokens, norm_w, w_in, b_in, lb_logits, hg_norm_w, pool_w, pool_scale, w_down_hg, w_down_pool, w_out, final_norm_w, loss_target, m_meta_tokens, m_norm_w, m_w_in, m_b_in, m_lb_logits, m_hg_norm_w, m_pool_w, m_pool_scale, m_w_down_hg, m_w_down_pool, m_w_out, m_final_norm_w, v_meta_tokens, v_norm_w, v_w_in, v_b_in, v_lb_logits, v_hg_norm_w, v_pool_w, v_pool_scale, v_w_down_hg, v_w_down_pool, v_w_out, v_final_norm_w):
    seq = x.shape[1]

    meta_full = _all_gather_small(meta_tokens).transpose(1, 0, 2).reshape(N_META, D_MODEL)
    w_rest = jnp.concatenate([w_down_hg[0].astype(BF16), w_down_pool[0].astype(BF16), w_out[0].astype(BF16),
                              pool_w[0].astype(BF16).reshape(32, 1024)], axis=0)

    zp = jnp.concatenate([jnp.zeros((PAD_ROWS, D_MODEL), F32), meta_full, x[0]], axis=0)
    tgt = jnp.concatenate([jnp.zeros((CHUNK, D_MODEL), F32), loss_target[0]], axis=0)
    h = _first_norm(zp, norm_w)
    p, w_blocks = _in_projection(h, w_in[0].astype(BF16), b_in.reshape(N_COLBLK, 1, 1024), _gather_order())
    o, states, scores, rest = _hgrn_forward(p, lb_logits, w_rest)
    wdh = rest[:, REST_W_DOWN_HG:REST_W_DOWN_HG + 128].reshape(1024, 1024)
    wdp = rest[:, REST_W_DOWN_POOL:REST_W_DOWN_POOL + 128].reshape(1024, 1024)
    wout = rest[:, REST_W_OUT:REST_W_OUT + 128].reshape(1024, 1024)
    pw = rest[:, REST_POOL_W:REST_POOL_W + 32].reshape(N_DEV, 4, 32, 256).transpose(1, 0, 2, 3).reshape(4, 256, 256)
    (d_o, dp_b, dz2, grad_lhs, grad_rhs, dbias_b, d_hgw, d_pw, d_ps, d_fnw, loss_part) = _tail(
        p, o, zp, tgt, hg_norm_w, pw, pool_scale, wdh, wdp, wout, final_norm_w.reshape(1, D_MODEL))
    dp_a, dbias_a, d_lb = _hgrn_backward(p, lb_logits, states, scores, d_o)
    dz, d_nw = _in_projection_backward(dp_a, dp_b, w_blocks, zp, dz2, norm_w)

    lb = jax.nn.sigmoid(lb_logits[0:1] - lb_logits[1:2])
    d_l0 = d_lb * lb * (1.0 - lb)
    replicated = jnp.concatenate([dbias_a.reshape(3, 1024), dbias_b.reshape(5, 1024), d_nw, d_l0, -d_l0, d_hgw, d_ps, d_fnw,
                                  jnp.zeros((MISC_ROWS - MISC_FINAL_NORM_W - 1, 1024), F32)], axis=0)
    d_meta = dz[PAD_ROWS:CHUNK].reshape(N_META, N_DEV, 128).transpose(1, 0, 2)
    d_pw_blocks = d_pw.reshape(4, N_DEV, 32, 256).transpose(1, 0, 2, 3).reshape(N_DEV, 32, 1024)
    g_misc = jnp.concatenate([d_pw_blocks, jnp.pad(d_meta, ((0, 0), (0, 0), (0, 1024 - 128))),
                              jnp.broadcast_to(replicated[None], (N_DEV, 16, 1024))], axis=1)

    partials = _weight_grads_reduce(h, dp_a, dp_b, grad_lhs, grad_rhs, g_misc, _slab_order())

    as_rows = lambda t, n: t.reshape(n, 1024)
    small = [(MISC_POOL_W, 1024, tuple(as_rows(t, 32) for t in (pool_w, m_pool_w, v_pool_w))),
             (MISC_META, 128, (meta_tokens, m_meta_tokens, v_meta_tokens)),
             (MISC_B_IN, 1024, tuple(as_rows(t, 8) for t in (b_in, m_b_in, v_b_in))),
             (MISC_NORM_W, 1024, (norm_w, m_norm_w, v_norm_w)),
             (MISC_LB, 1024, (lb_logits, m_lb_logits, v_lb_logits)),
             (MISC_HG_NORM_W, 1024, (hg_norm_w, m_hg_norm_w, v_hg_norm_w)),
             (MISC_POOL_SCALE, 1024, (pool_scale, m_pool_scale, v_pool_scale)),
             (MISC_FINAL_NORM_W, 1024, tuple(as_rows(t, 1) for t in (final_norm_w, m_final_norm_w, v_final_norm_w)))]
    res = _finish(*partials, (w_in, m_w_in, v_w_in),
                  [(w_out, m_w_out, v_w_out), (w_down_hg, m_w_down_hg, v_w_down_hg), (w_down_pool, m_w_down_pool, v_w_down_pool)],
                  small)
    r_w_in, r_w_out, r_wdh, r_wdp, r_pw, r_meta, r_b_in, r_nw, r_lb, r_hgw, r_ps, r_fnw = res
    loss = lax.psum(loss_part[0, 0], ("x", "y", "c"))
    grad_x = dz[CHUNK:].reshape(1, seq, D_MODEL)
    per_kind = [(r_meta[k], r_nw[k], r_w_in[k], r_b_in[k].reshape(1, 8192), r_lb[k], r_hgw[k], r_pw[k].reshape(1, 4, 32, 256),
                 r_ps[k], r_wdh[k], r_wdp[k], r_w_out[k], r_fnw[k].reshape(1024)) for k in range(4)]
    return (loss, grad_x, *per_kind[0], *per_kind[1], *per_kind[2], *per_kind[3])
```

```python
import functools

import jax
import jax.numpy as jnp
from jax import lax
from jax.experimental import pallas as pl
from jax.experimental.pallas import tpu as pltpu

F32 = jnp.float32
BF16 = jnp.bfloat16

D_MODEL = 1024
N_META = 16
HEADS = 8
HEAD_DIM = 128
CHUNK = 64
SUB = 16
N_SUB = CHUNK // SUB
PAD_ROWS = CHUNK - N_META
POOL_WINDOWS = (2, 4, 8, 16)
POOL_GDIM = D_MODEL // len(POOL_WINDOWS)
HALO = 16
EPS = 1e-6
N_DEV = 8
N_COLBLK = 8
ADAM_LR, ADAM_B1, ADAM_B2, ADAM_EPS, ADAM_WD, ADAM_STEP = 0.001, 0.9, 0.999, 1e-08, 0.01, 10

VMEM_LIMIT = 56 * 1024 * 1024
MESH = pl.DeviceIdType.MESH
ANY = pl.BlockSpec(memory_space=pl.ANY)
HIGHEST = lax.Precision.HIGHEST
LOG2_E = 1.4426950408889634

REST_W_DOWN_HG = 0
REST_W_DOWN_POOL = 128
REST_W_OUT = 256
REST_POOL_W = 384
MISC_POOL_W = 0
MISC_META = 32
MISC_B_IN = 48
MISC_NORM_W = 56
MISC_LB = 57
MISC_HG_NORM_W = 59
MISC_POOL_SCALE = 60
MISC_FINAL_NORM_W = 61
MISC_ROWS = 64


def _params(sem=None):
    return pltpu.CompilerParams(dimension_semantics=sem, vmem_limit_bytes=VMEM_LIMIT)


def _row_tile(n_rows, prefer):
    best = 16
    for t in range(16, prefer + 1, 16):
        if n_rows % t == 0:
            best = t
    return best


def _sigmoid_pair(x):
    e = jnp.exp(-jnp.abs(x))
    r = 1.0 / (1.0 + e)
    er = e * r
    pos = x >= 0
    return jnp.where(pos, r, er), jnp.where(pos, er, r)


def _dot(a, b):
    return jnp.dot(a.astype(BF16), b.astype(BF16), preferred_element_type=F32)


def _dot_nt(a, b):
    return lax.dot_general(a.astype(BF16), b.astype(BF16), (((1,), (1,)), ((), ())), preferred_element_type=F32)


def _dot_tn(a, b):
    return lax.dot_general(a.astype(BF16), b.astype(BF16), (((0,), (0,)), ((), ())), preferred_element_type=F32)


def _device_index(px, py, pc):
    return 4 * px + 2 * py + pc


def _direct_gather(src_ref, dst_ref, send_sems, recv_sems, local_sem):
    x, y, c = lax.axis_index("x"), lax.axis_index("y"), lax.axis_index("c")
    own = pltpu.make_async_copy(src_ref, dst_ref.at[_device_index(x, y, c)], local_sem)
    sends, arrivals = [], []
    for k in range(1, N_DEV):
        peer = (1 - x if k & 4 else x, 1 - y if k & 2 else y, 1 - c if k & 1 else c)
        for slot, out in ((_device_index(x, y, c), sends), (_device_index(*peer), arrivals)):
            out.append(pltpu.make_async_remote_copy(
                src_ref=src_ref, dst_ref=dst_ref.at[slot], send_sem=send_sems.at[k - 1], recv_sem=recv_sems.at[k - 1],
                device_id=peer, device_id_type=MESH))
    return own, sends, arrivals


GATHER_SEMS = [pltpu.SemaphoreType.DMA((N_DEV - 1,)), pltpu.SemaphoreType.DMA((N_DEV - 1,)), pltpu.SemaphoreType.DMA(())]


def _all_gather_small(block):
    def body(x_ref, out_ref, send_sems, recv_sems, local_sem):
        own, sends, arrivals = _direct_gather(x_ref, out_ref, send_sems, recv_sems, local_sem)
        own.start()
        for cp in sends:
            cp.start()
        for cp in arrivals:
            cp.wait_recv()
        for cp in sends:
            cp.wait_send()
        own.wait()

    return pl.pallas_call(
        body, name="all_gather_meta",
        out_shape=jax.ShapeDtypeStruct((N_DEV,) + block.shape, block.dtype),
        in_specs=[ANY], out_specs=ANY, scratch_shapes=list(GATHER_SEMS),
    )(block)


def _peer(k):
    x, y, c = lax.axis_index("x"), lax.axis_index("y"), lax.axis_index("c")
    return (1 - x if k & 4 else x, 1 - y if k & 2 else y, 1 - c if k & 1 else c)


def _me():
    return _device_index(lax.axis_index("x"), lax.axis_index("y"), lax.axis_index("c"))


def _remote(src, dst, send_sem, recv_sem, peer_bits):
    return pltpu.make_async_remote_copy(src_ref=src, dst_ref=dst, send_sem=send_sem, recv_sem=recv_sem,
                                        device_id=_peer(peer_bits), device_id_type=MESH)


N_ROW_GRADS = 3
SCATTER_SEMS = [pltpu.SemaphoreType.DMA((N_DEV - 1,)), pltpu.SemaphoreType.DMA((N_DEV - 1,)), pltpu.SemaphoreType.DMA(())]
SCATTER_ROWS_SEMS = [pltpu.SemaphoreType.DMA((7 * N_ROW_GRADS,)), pltpu.SemaphoreType.DMA((7 * N_ROW_GRADS,)),
                     pltpu.SemaphoreType.DMA((N_ROW_GRADS,))]


def _scatter_slabs(g_ref, first, recv_ref, send_sems, recv_sems, local_sem):
    n = g_ref.shape[0]
    me = _me()

    def each(on_send, on_local, on_arrival):
        for kk in range(1, N_DEV):
            peer = jnp.bitwise_xor(me, kk)

            @pl.when((peer >= first) & (peer < first + n))
            def _(kk=kk, peer=peer):
                on_send(_remote(g_ref.at[peer - first], recv_ref.at[me], send_sems.at[kk - 1], recv_sems.at[kk - 1], kk))

        @pl.when((me >= first) & (me < first + n))
        def _():
            on_local(pltpu.make_async_copy(g_ref.at[me - first], recv_ref.at[me], local_sem))
            if on_arrival is not None:
                for kk in range(1, N_DEV):
                    on_arrival(_remote(g_ref.at[0], recv_ref.at[jnp.bitwise_xor(me, kk)], send_sems.at[kk - 1],
                                       recv_sems.at[kk - 1], kk))

    start = lambda: each(lambda cp: cp.start(), lambda cp: cp.start(), None)
    finish = lambda: each(lambda cp: cp.wait_send(), lambda cp: cp.wait(), lambda cp: cp.wait_recv())
    return start, finish


def _scatter_rows(g_ref, recv_ref, send_sems, recv_sems, local_sems):
    me = _me()
    rows = lambda m, dev: g_ref.at[m, pl.ds(dev * 128, 128), :]

    def copies():
        local = [pltpu.make_async_copy(rows(m, me), recv_ref.at[me, m], local_sems.at[m]) for m in range(N_ROW_GRADS)]
        sends, arrivals = [], []
        for m in range(N_ROW_GRADS):
            for kk in range(1, N_DEV):
                peer, sems = jnp.bitwise_xor(me, kk), (send_sems.at[7 * m + kk - 1], recv_sems.at[7 * m + kk - 1])
                sends.append(_remote(rows(m, peer), recv_ref.at[me, m], *sems, kk))
                arrivals.append(_remote(rows(m, me), recv_ref.at[peer, m], *sems, kk))
        return local, sends, arrivals

    def start():
        local, sends, _ = copies()
        for cp in local + sends:
            cp.start()

    def finish():
        local, sends, arrivals = copies()
        for cp in arrivals:
            cp.wait_recv()
        for cp in sends:
            cp.wait_send()
        for cp in local:
            cp.wait()

    return start, finish


def _adam_update(g, w, m, v):
    mn = ADAM_B1 * m + (1.0 - ADAM_B1) * g
    vn = ADAM_B2 * v + (1.0 - ADAM_B2) * (g * g)
    m_hat = mn / (1.0 - ADAM_B1 ** ADAM_STEP)
    v_hat = vn / (1.0 - ADAM_B2 ** ADAM_STEP)
    return -ADAM_LR * (m_hat / (jnp.sqrt(v_hat) + ADAM_EPS) + ADAM_WD * w), mn, vn


def _device_sum(parts):
    t = [p.astype(F32) for p in parts]
    return ((t[0] + t[1]) + (t[2] + t[3])) + ((t[4] + t[5]) + (t[6] + t[7]))


def _finish(a_hi, a_lo, n_lo, b3, misc, big, rows3, small):
    n_steps = 4
    tb, tr3 = 1024 // n_steps, 128 // n_steps

    def body(*refs):
        it = iter(refs)
        hi_ref, lo_ref, b_ref, misc_hbm = next(it), next(it), next(it), next(it)
        big_in = [next(it) for _ in range(3)]
        rows_in = [[next(it) for _ in range(3)] for _ in rows3]
        small_in = [[next(it) for _ in range(3)] for _ in small]
        big_out = [next(it) for _ in range(4)]
        rows_out = [[next(it) for _ in range(4)] for _ in rows3]
        small_out = [[next(it) for _ in range(4)] for _ in small]
        m_vmem, send_sems, recv_sems, local_sem = next(it), next(it), next(it), next(it)
        start, finish = _scatter_slabs(misc_hbm, 0, m_vmem, send_sems, recv_sems, local_sem)
        step = pl.program_id(0)

        @pl.when(step == 0)
        def _():
            start()

        def apply(g, ins, outs):
            d, mn, vn = _adam_update(g, ins[0][...], ins[1][...], ins[2][...])
            for r, val in zip(outs, (g, d, mn, vn)):
                r[...] = val

        low = _me() < n_lo
        apply(_device_sum([jnp.where(low, lo_ref[s], hi_ref[s]) for s in range(N_DEV)])[None], big_in, big_out)
        for k in range(len(rows3)):
            apply(_device_sum([b_ref[s, k] for s in range(N_DEV)])[None], rows_in[k], rows_out[k])

        @pl.when(step == n_steps - 1)
        def _():
            finish()
            for (row0, lanes, ins), r_in, r_out in zip(small, small_in, small_out):
                n = ins[0].shape[0]
                apply(_device_sum([m_vmem[s, row0:row0 + n, :lanes] for s in range(N_DEV)]), r_in, r_out)

    whole = lambda shape: pl.BlockSpec(shape, lambda i: (0,) * len(shape))
    big_blk = pl.BlockSpec((1, tb, 1024), lambda i: (0, i, 0))
    rows_blk = pl.BlockSpec((1, tr3, 1024), lambda i: (0, i, 0))
    a_blk = pl.BlockSpec((N_DEV, tb, 1024), lambda i: (0, i, 0))
    in_specs = [a_blk, a_blk, pl.BlockSpec((N_DEV, 3, tr3, 1024), lambda i: (0, 0, i, 0)), ANY]
    in_specs += [big_blk] * 3 + [rows_blk] * (3 * len(rows3))
    out_specs = [big_blk] * 4 + [rows_blk] * (4 * len(rows3))
    out_shape = [jax.ShapeDtypeStruct(big[0].shape, F32)] * 4
    for w, _, _ in rows3:
        out_shape += [jax.ShapeDtypeStruct(w.shape, F32)] * 4
    args = [a_hi, a_lo, b3, misc, *big]
    for t in rows3:
        args += list(t)
    for _, _, t in small:
        in_specs += [whole(t[0].shape)] * 3
        out_specs += [whole(t[0].shape)] * 4
        out_shape += [jax.ShapeDtypeStruct(t[0].shape, F32)] * 4
        args += list(t)
    outs = pl.pallas_call(
        body, name="reduce_sum_adamw", out_shape=tuple(out_shape), grid=(n_steps,),
        in_specs=in_specs, out_specs=tuple(out_specs),
        scratch_shapes=[pltpu.VMEM(misc.shape, misc.dtype)] + list(SCATTER_SEMS),
        compiler_params=_params(("arbitrary",)),
    )(*args)
    return [tuple(outs[4 * k:4 * k + 4]) for k in range(len(outs) // 4)]


def _first_norm(zp, norm_w):
    n_rows = zp.shape[0]
    tr = _row_tile(n_rows, 832)

    def body(z_ref, nw_ref, h_ref):
        z = z_ref[...]
        r = lax.rsqrt(jnp.mean(z * z, axis=-1, keepdims=True) + EPS)
        h_ref[...] = (z * r * nw_ref[...]).astype(BF16)

    return pl.pallas_call(
        body, name="first_norm",
        out_shape=jax.ShapeDtypeStruct((n_rows, D_MODEL), BF16),
        grid=(n_rows // tr,),
        in_specs=[pl.BlockSpec((tr, D_MODEL), lambda i: (i, 0)), pl.BlockSpec((1, D_MODEL), lambda i: (0, 0))],
        out_specs=pl.BlockSpec((tr, D_MODEL), lambda i: (i, 0)),
        compiler_params=_params(("arbitrary",)),
    )(zp, norm_w)


def _gather_order():
    x, y, c = lax.axis_index("x"), lax.axis_index("y"), lax.axis_index("c")
    chips = [(1 - x, y), (x, 1 - y), (1 - x, 1 - y)]
    order = [_device_index(x, y, c), _device_index(x, y, 1 - c)]
    order += [_device_index(*q, c) for q in chips] + [_device_index(*q, 1 - c) for q in chips]
    return jnp.stack(order).astype(jnp.int32)


def _in_projection(h, w_shard, b_blocks, order):
    n_rows = h.shape[0]
    tr = _row_tile(n_rows, 320)
    nt = n_rows // tr

    def body(order_ref, h_ref, w_hbm, b_ref, p_ref, w_out, w_vmem, send_sems, recv_sems, local_sem, out_sems):
        s, i = pl.program_id(0), pl.program_id(1)
        x, y, c = lax.axis_index("x"), lax.axis_index("y"), lax.axis_index("c")
        me, sibling = (x, y, c), (x, y, 1 - c)
        chips = [(1 - x, y), (x, 1 - y), (1 - x, 1 - y)]

        def slot(px, py, pc):
            return w_vmem.at[_device_index(px, py, pc)]

        def copy(k, blk, to, src=None):
            return pltpu.make_async_remote_copy(
                src_ref=slot(*blk) if src is None else src, dst_ref=slot(*blk),
                send_sem=send_sems.at[k], recv_sem=recv_sems.at[k], device_id=to, device_id_type=MESH)

        own = pltpu.make_async_copy(w_hbm, slot(*me), local_sem)
        first = [copy(0, me, sibling, src=w_hbm)] + [copy(1 + j, me, (*q, c), src=w_hbm) for j, q in enumerate(chips)]
        passed = [copy(4 + j, (*q, c), sibling) for j, q in enumerate(chips)]
        arrivals = [None, copy(0, sibling, me)]
        arrivals += [copy(1 + j, (*q, c), me) for j, q in enumerate(chips)]
        arrivals += [copy(4 + j, (*q, 1 - c), me) for j, q in enumerate(chips)]

        def keep(step):
            return pltpu.make_async_copy(w_vmem.at[order_ref[step]], w_out.at[order_ref[step]], out_sems.at[step])

        for step in range(N_DEV):
            @pl.when((i == 0) & (s == step))
            def _(step=step):
                if step == 0:
                    own.start()
                    for cp in first:
                        cp.start()
                    own.wait()
                else:
                    arrivals[step].wait_recv()
                    if 2 <= step <= 4:
                        passed[step - 2].start()
                keep(step).start()

        p_ref[0] = jnp.dot(h_ref[...], w_vmem[order_ref[s]], preferred_element_type=F32) + b_ref[0]

        @pl.when((s == N_DEV - 1) & (i == nt - 1))
        def _():
            for cp in first + passed:
                cp.wait_send()
            for step in range(N_DEV):
                keep(step).wait()

    return pl.pallas_call(
        body, name="in_projection_gather",
        out_shape=(jax.ShapeDtypeStruct((N_COLBLK, n_rows, 1024), F32),
                   jax.ShapeDtypeStruct((N_DEV, D_MODEL, 1024), BF16)),
        grid_spec=pltpu.PrefetchScalarGridSpec(
            num_scalar_prefetch=1, grid=(N_DEV, nt),
            in_specs=[pl.BlockSpec((tr, D_MODEL), lambda s, i, o: (i, 0)), ANY,
                      pl.BlockSpec((1, 1, 1024), lambda s, i, o: (o[s], 0, 0))],
            out_specs=(pl.BlockSpec((1, tr, 1024), lambda s, i, o: (o[s], i, 0)), ANY),
            scratch_shapes=[pltpu.VMEM((N_DEV, D_MODEL, 1024), BF16), pltpu.SemaphoreType.DMA((7,)),
                            pltpu.SemaphoreType.DMA((7,)), pltpu.SemaphoreType.DMA(()), pltpu.SemaphoreType.DMA((N_DEV,))]),
        compiler_params=_params(("arbitrary", "arbitrary")),
    )(order, h, w_shard, b_blocks)


def _lower_bound(lb_ref):
    l0, l1 = lb_ref[0:1, :], lb_ref[1:2, :]
    _, lb = _sigmoid_pair(l1 - l0)
    return lb


def _chunk_gates(fz, lb, valid):
    sig, nsig = _sigmoid_pair(fz)
    f = lb + (1.0 - lb) * sig
    g = jnp.where(valid, jnp.log(f), 0.0)
    k = jnp.where(valid, (1.0 - lb) * nsig, 0.0)
    return sig, nsig, f, g, k


def _tri(n, upper=False):
    r = lax.broadcasted_iota(jnp.int32, (n, n), 0)
    c = lax.broadcasted_iota(jnp.int32, (n, n), 1)
    return jnp.where((r <= c) if upper else (r >= c), 1.0, 0.0).astype(F32)


def _intra_scores(q_ref, k_ref, b2_ref, a_ref, kt_ref, col0):
    cols = pl.ds(col0, HEAD_DIM)
    rows_s = lax.broadcasted_iota(jnp.int32, (SUB, 1), 0)
    lanes_c = lax.broadcasted_iota(jnp.int32, (1, CHUNK), 1)
    kt_ref[...] = jnp.zeros_like(kt_ref)
    for i in range(N_SUB):
        lo = i * SUB
        qi = q_ref[lo:lo + SUB, cols]
        bi = b2_ref[lo:lo + SUB, cols]
        if i == 0:
            acc = jnp.zeros((SUB, CHUNK), F32)
        else:
            ref_i = b2_ref[lo:lo + 1, cols]
            qt = qi * jnp.exp2(bi - ref_i)
            kt_ref[0:lo, :] = k_ref[0:lo, cols] * jnp.exp2(ref_i - b2_ref[0:lo, cols])
            acc = _dot_nt(qt, kt_ref[...])
        for s in range(SUB):
            b_s = b2_ref[lo + s:lo + s + 1, cols]
            k_s = k_ref[lo + s:lo + s + 1, cols]
            w = jnp.exp2(jnp.minimum(bi - b_s, 0.0))
            col = jnp.sum((qi * w) * k_s, axis=-1, keepdims=True)
            acc = jnp.where(lanes_c == lo + s, col, acc)
        a_ref[lo:lo + SUB, :] = jnp.where(lanes_c <= lo + rows_s, acc, 0.0)


def _hgrn_forward(p, lb_logits, w_rest):
    n_rows = p.shape[1]
    n_chunks = n_rows // CHUNK
    width = HEADS * HEAD_DIM

    def body(q_ref, fz_ref, v_ref, lb_ref, rest_ref, o_ref, st_out_ref, a_out_ref, rest_out,
             state, k_vmem, b2_vmem, a_vmem, kt_vmem, send_sems, recv_sems, local_sem):
        n = pl.program_id(0)
        own, sends, arrivals = _direct_gather(rest_ref, rest_out, send_sems, recv_sems, local_sem)

        @pl.when(n == 0)
        def _():
            state[...] = jnp.zeros_like(state)
            own.start()
            for cp in sends:
                cp.start()

        rows = n * CHUNK + lax.broadcasted_iota(jnp.int32, (CHUNK, 1), 0)
        valid = rows >= PAD_ROWS
        lb = _lower_bound(lb_ref)
        _, _, _, g, k = _chunk_gates(fz_ref[0], lb, valid)
        k_vmem[...] = k
        b2_vmem[...] = jnp.dot(_tri(CHUNK), g, precision=HIGHEST, preferred_element_type=F32) * LOG2_E
        q_view = q_ref.at[0]
        for h in range(HEADS):
            cols = pl.ds(h * HEAD_DIM, HEAD_DIM)
            st = state[h]
            st_out_ref[0, h] = st
            bh = b2_vmem[:, cols]
            kh = k_vmem[:, cols]
            vh = jnp.where(valid, v_ref[0, :, cols], 0.0)
            qe = q_ref[0, :, cols] * jnp.exp2(bh)
            _intra_scores(q_view, k_vmem, b2_vmem, a_vmem, kt_vmem, h * HEAD_DIM)
            a = a_vmem[...].astype(BF16)
            a_out_ref[0, h] = a
            o_ref[:, cols] = _dot_nt(qe, st) + _dot(a, vh)
            b_last = b2_vmem[CHUNK - 1:CHUNK, cols]
            kd = kh * jnp.exp2(b_last - bh)
            state[h] = st * jnp.exp2(b_last) + _dot_tn(vh, kd)

        @pl.when(n == n_chunks - 1)
        def _():
            for cp in arrivals:
                cp.wait_recv()
            for cp in sends:
                cp.wait_send()
            own.wait()

    blk = lambda c: pl.BlockSpec((1, CHUNK, width), lambda n, c=c: (c, n, 0))
    return pl.pallas_call(
        body, name="hgrn_forward",
        out_shape=(jax.ShapeDtypeStruct((n_rows, width), F32),
                   jax.ShapeDtypeStruct((n_chunks, HEADS, HEAD_DIM, HEAD_DIM), F32),
                   jax.ShapeDtypeStruct((n_chunks, HEADS, CHUNK, CHUNK), BF16),
                   jax.ShapeDtypeStruct((N_DEV,) + w_rest.shape, w_rest.dtype)),
        grid=(n_chunks,),
        in_specs=[blk(0), blk(1), blk(2), pl.BlockSpec((2, width), lambda n: (0, 0)), ANY],
        out_specs=(pl.BlockSpec((CHUNK, width), lambda n: (n, 0)),
                   pl.BlockSpec((1, HEADS, HEAD_DIM, HEAD_DIM), lambda n: (n, 0, 0, 0)),
                   pl.BlockSpec((1, HEADS, CHUNK, CHUNK), lambda n: (n, 0, 0, 0)), ANY),
        scratch_shapes=[pltpu.VMEM((HEADS, HEAD_DIM, HEAD_DIM), F32), pltpu.VMEM((CHUNK, width), F32),
                        pltpu.VMEM((CHUNK, width), F32), pltpu.VMEM((CHUNK, CHUNK), F32),
                        pltpu.VMEM((CHUNK, HEAD_DIM), F32)] + list(GATHER_SEMS),
        compiler_params=_params(("arbitrary",)),
    )(p, p, p, lb_logits, w_rest)


def _hgrn_backward(p, lb_logits, states, scores, d_o, g_slabs, first_owner, g_rows):
    n_rows = p.shape[1]
    n_chunks = n_rows // CHUNK
    width = HEADS * HEAD_DIM

    def body(q_ref, fz_ref, v_ref, lb_ref, st_ref, a_ref, do_ref, gs_hbm, gr_hbm, dp_ref, dbias_ref, dlb_ref, rs_hbm, rr_hbm,
             dstate, k_vmem, b2_vmem, kt_vmem, dqi_vmem, dki_vmem, *sems):
        step = pl.program_id(0)
        n = n_chunks - 1 - step
        start_slabs, finish_slabs = _scatter_slabs(gs_hbm, first_owner, rs_hbm, *sems[:3])
        start_rows, finish_rows = _scatter_rows(gr_hbm, rr_hbm, *sems[3:])

        @pl.when(step == 0)
        def _():
            dstate[...] = jnp.zeros_like(dstate)
            dbias_ref[...] = jnp.zeros_like(dbias_ref)
            dlb_ref[...] = jnp.zeros_like(dlb_ref)
            start_slabs()
            start_rows()

        @pl.when(step == n_chunks - 1)
        def _():
            finish_slabs()
            finish_rows()

        rows = n * CHUNK + lax.broadcasted_iota(jnp.int32, (CHUNK, 1), 0)
        valid = rows >= PAD_ROWS
        lb = _lower_bound(lb_ref)
        sig, nsig, f, g, k = _chunk_gates(fz_ref[0], lb, valid)
        k_vmem[...] = k
        b2_vmem[...] = jnp.dot(_tri(CHUNK), g, precision=HIGHEST, preferred_element_type=F32) * LOG2_E
        rows_c = lax.broadcasted_iota(jnp.int32, (CHUNK, 1), 0)
        lanes_c = lax.broadcasted_iota(jnp.int32, (1, CHUNK), 1)
        causal = lax.broadcasted_iota(jnp.int32, (CHUNK, CHUNK), 0) >= lax.broadcasted_iota(jnp.int32, (CHUNK, CHUNK), 1)
        tri_up = _tri(CHUNK, upper=True)
        for h in range(HEADS):
            cols = pl.ds(h * HEAD_DIM, HEAD_DIM)
            st = st_ref[0, h]
            dst = dstate[h]
            qh = q_ref[0, :, cols]
            bh = b2_vmem[:, cols]
            kh = k_vmem[:, cols]
            vh = jnp.where(valid, v_ref[0, :, cols], 0.0)
            doh = do_ref[:, cols]
            eb = jnp.exp2(bh)
            qe = qh * eb
            b_last = b2_vmem[CHUNK - 1:CHUNK, cols]
            e_last = jnp.exp2(b_last)
            decay_k = jnp.exp2(b_last - bh)
            kd = kh * decay_k
            dqe = _dot(doh, st)
            da = jnp.where(causal, _dot_nt(doh, vh), 0.0)
            dv = _dot_tn(a_ref[0, h], doh) + _dot_nt(kd, dst)
            dkd = _dot(vh, dst)
            dstate[h] = dst * e_last + _dot_tn(doh, qe)
            db_last = (jnp.sum(dst * st, axis=0, keepdims=True) * e_last
                       + jnp.sum(dkd * kd, axis=0, keepdims=True))
            dki_vmem[...] = jnp.zeros_like(dki_vmem)
            kt_vmem[...] = jnp.zeros_like(kt_vmem)
            for i in range(N_SUB):
                lo = i * SUB
                qi = q_ref[0, lo:lo + SUB, cols]
                bi = b2_vmem[lo:lo + SUB, cols]
                da_i = da[lo:lo + SUB, :]
                if i == 0:
                    dq_i = jnp.zeros((SUB, HEAD_DIM), F32)
                else:
                    ref_i = b2_vmem[lo:lo + 1, cols]
                    eq = jnp.exp2(bi - ref_i)
                    ek = jnp.exp2(ref_i - b2_vmem[0:lo, cols])
                    kt_vmem[0:lo, :] = k_vmem[0:lo, cols] * ek
                    dq_i = _dot(da_i, kt_vmem[...]) * eq
                    dki_vmem[0:lo, :] += _dot_tn(da_i, qi * eq)[0:lo] * ek
                for s in range(SUB):
                    b_s = b2_vmem[lo + s:lo + s + 1, cols]
                    k_s = k_vmem[lo + s:lo + s + 1, cols]
                    w = jnp.exp2(jnp.minimum(bi - b_s, 0.0))
                    da_col = jnp.sum(jnp.where(lanes_c == lo + s, da_i, 0.0), axis=-1, keepdims=True)
                    gw = da_col * w
                    dq_i = dq_i + gw * k_s
                    dki_vmem[lo + s:lo + s + 1, :] += jnp.sum(gw * qi, axis=0, keepdims=True)
                dqi_vmem[lo:lo + SUB, :] = dq_i
            dq_intra = dqi_vmem[...]
            dk_intra = dki_vmem[...]
            dq = dqe * eb + dq_intra
            dk = dkd * decay_k + dk_intra
            db = dqe * qe - dkd * kd + qh * dq_intra - kh * dk_intra
            db = db + jnp.where(rows_c == CHUNK - 1, db_last, 0.0)
            dg = jnp.dot(tri_up, db, precision=HIGHEST, preferred_element_type=F32)
            fh = f[:, h * HEAD_DIM:(h + 1) * HEAD_DIM]
            sh = sig[:, h * HEAD_DIM:(h + 1) * HEAD_DIM]
            nh = nsig[:, h * HEAD_DIM:(h + 1) * HEAD_DIM]
            lbh = lb[:, h * HEAD_DIM:(h + 1) * HEAD_DIM]
            df = jnp.where(valid, dg / fh - dk, 0.0)
            dfz = df * (1.0 - lbh) * sh * nh
            dq = jnp.where(valid, dq, 0.0)
            dv = jnp.where(valid, dv, 0.0)
            dlb_ref[:, cols] += jnp.sum(df * nh, axis=0, keepdims=True)
            dp_ref[0, :, cols] = dq.astype(BF16)
            dp_ref[1, :, cols] = dfz.astype(BF16)
            dp_ref[2, :, cols] = dv.astype(BF16)
            dbias_ref[0, :, cols] += jnp.sum(dq, axis=0, keepdims=True)
            dbias_ref[1, :, cols] += jnp.sum(dfz, axis=0, keepdims=True)
            dbias_ref[2, :, cols] += jnp.sum(dv, axis=0, keepdims=True)

    rev = lambda s: n_chunks - 1 - s
    blk = lambda c: pl.BlockSpec((1, CHUNK, width), lambda s, c=c: (c, rev(s), 0))
    return pl.pallas_call(
        body, name="hgrn_backward",
        out_shape=(jax.ShapeDtypeStruct((3, n_rows, width), BF16),
                   jax.ShapeDtypeStruct((3, 1, width), F32),
                   jax.ShapeDtypeStruct((1, width), F32),
                   jax.ShapeDtypeStruct((N_DEV,) + g_slabs.shape[1:], g_slabs.dtype),
                   jax.ShapeDtypeStruct((N_DEV, N_ROW_GRADS, 128, g_rows.shape[2]), g_rows.dtype)),
        grid=(n_chunks,),
        in_specs=[blk(0), blk(1), blk(2), pl.BlockSpec((2, width), lambda s: (0, 0)),
                  pl.BlockSpec((1, HEADS, HEAD_DIM, HEAD_DIM), lambda s: (rev(s), 0, 0, 0)),
                  pl.BlockSpec((1, HEADS, CHUNK, CHUNK), lambda s: (rev(s), 0, 0, 0)),
                  pl.BlockSpec((CHUNK, width), lambda s: (rev(s), 0)), ANY, ANY],
        out_specs=(pl.BlockSpec((3, CHUNK, width), lambda s: (0, rev(s), 0)),
                   pl.BlockSpec((3, 1, width), lambda s: (0, 0, 0)),
                   pl.BlockSpec((1, width), lambda s: (0, 0)), ANY, ANY),
        scratch_shapes=[pltpu.VMEM((HEADS, HEAD_DIM, HEAD_DIM), F32), pltpu.VMEM((CHUNK, width), F32),
                        pltpu.VMEM((CHUNK, width), F32), pltpu.VMEM((CHUNK, HEAD_DIM), F32),
                        pltpu.VMEM((CHUNK, HEAD_DIM), F32), pltpu.VMEM((CHUNK, HEAD_DIM), F32)]
        + list(SCATTER_SEMS) + list(SCATTER_ROWS_SEMS),
        compiler_params=_params(("arbitrary",)),
    )(p, p, p, lb_logits, states, scores, d_o, g_slabs, g_rows)


def _silu_and_grad(x):
    s, ns = _sigmoid_pair(x)
    return x * s, s * (1.0 + x * ns)


def _tail(p, o, zp, tgt, hg_norm_w, pool_w, pool_scale, w_down_hg, w_down_pool, w_out, final_norm_w):
    n_rows = zp.shape[0]
    tr = _row_tile(n_rows, 160)
    nt = n_rows // tr
    ext = tr + HALO
    n_groups = len(POOL_WINDOWS)

    def body(o_ref, ghg_ref, u_ref, gpool_ref, mhg_ref, mpool_ref, uhalo_ref, z_ref, tgt_ref,
             hgw_ref, pw_ref, ps_ref, wdh_ref, wdp_ref, wout_ref, fnw_ref,
             do_ref, dp_ref, dz2_ref, lhs_ref, rhs_ref,
             dbias_ref, dhgw_ref, dpw_ref, dps_ref, dfnw_ref, loss_ref, halo_vmem):
        step = pl.program_id(0)
        ti = nt - 1 - step

        @pl.when(step == 0)
        def _():
            halo_vmem[...] = jnp.zeros_like(halo_vmem)
            for r in (dbias_ref, dhgw_ref, dpw_ref, dps_ref, dfnw_ref, loss_ref):
                r[...] = jnp.zeros_like(r)

        rows = ti * tr + lax.broadcasted_iota(jnp.int32, (tr, 1), 0)
        valid = rows >= PAD_ROWS
        in_loss = rows >= CHUNK
        count_pos = jnp.maximum(rows - PAD_ROWS + 1, 1).astype(F32)

        o = o_ref[...]
        hgw = hgw_ref[...]
        inv_o, on_parts = [], []
        for h in range(HEADS):
            oh = o[:, h * HEAD_DIM:(h + 1) * HEAD_DIM]
            r = lax.rsqrt(jnp.mean(oh * oh, axis=-1, keepdims=True) + EPS)
            inv_o.append(r)
            on_parts.append(oh * r)
        o_hat = jnp.concatenate(on_parts, axis=1)
        o_n = o_hat * hgw
        g_hg = ghg_ref[0]
        silu_hg, dsilu_hg = _silu_and_grad(g_hg)
        a_hg = o_n * silu_hg
        y_hg = _dot(a_hg, wdh_ref[...])

        u = jnp.where(valid, u_ref[0], 0.0)
        u_prev = jnp.where(ti > 0, uhalo_ref[0], 0.0)
        u_ext = jnp.concatenate([u_prev, u], axis=0)
        pooled_parts, mixed_parts, inv_cnt = [], [], []
        for gi, win in enumerate(POOL_WINDOWS):
            lanes = slice(gi * POOL_GDIM, (gi + 1) * POOL_GDIM)
            s = u_ext[:, lanes]
            shift = 1
            while shift < win:
                s = s + pltpu.roll(s, shift, 0)
                shift *= 2
            ic = 1.0 / jnp.minimum(count_pos, float(win))
            inv_cnt.append(ic)
            pooled = s[HALO:] * ic - u[:, lanes]
            pooled_parts.append(pooled)
            mixed_parts.append(_dot(pooled, pw_ref[gi]))
        mixed = jnp.concatenate(mixed_parts, axis=1)
        ps = ps_ref[...]
        g_pool = gpool_ref[0]
        silu_pool, dsilu_pool = _silu_and_grad(g_pool)
        a_pool = mixed * ps * silu_pool
        y_pool = _dot(a_pool, wdp_ref[...])

        m_hg, m_pool = mhg_ref[0], mpool_ref[0]
        s_hg, ns_hg = _sigmoid_pair(m_hg)
        s_pool, ns_pool = _sigmoid_pair(m_pool)
        merged = s_hg * y_hg + s_pool * y_pool
        z2 = z_ref[...] + _dot(merged, wout_ref[...])
        r2 = lax.rsqrt(jnp.mean(z2 * z2, axis=-1, keepdims=True) + EPS)
        n2 = z2 * r2
        fnw = fnw_ref[...]
        err = jnp.where(in_loss, n2 * fnw - tgt_ref[...], 0.0)
        loss_ref[...] += jnp.sum(jnp.sum(err * err, axis=0, keepdims=True), axis=1, keepdims=True) * (0.5 / D_MODEL)
        dy = err * (1.0 / D_MODEL)

        dfnw_ref[...] += jnp.sum(dy * n2, axis=0, keepdims=True)
        gy = dy * fnw
        dz2 = r2 * (gy - n2 * jnp.mean(gy * n2, axis=-1, keepdims=True))
        dmerged = _dot_nt(dz2, wout_ref[...])
        dy_hg = s_hg * dmerged
        dy_pool = s_pool * dmerged
        dm_hg = dmerged * y_hg * s_hg * ns_hg
        dm_pool = dmerged * y_pool * s_pool * ns_pool
        da_hg = _dot_nt(dy_hg, wdh_ref[...])
        da_pool = _dot_nt(dy_pool, wdp_ref[...])

        d_on = da_hg * silu_hg
        dg_hg = da_hg * o_n * dsilu_hg
        dhgw_ref[...] += jnp.sum(d_on * o_hat, axis=0, keepdims=True)
        gyo = d_on * hgw
        do_parts = []
        for h in range(HEADS):
            lanes = slice(h * HEAD_DIM, (h + 1) * HEAD_DIM)
            gh, nh = gyo[:, lanes], o_hat[:, lanes]
            do_parts.append(inv_o[h] * (gh - nh * jnp.mean(gh * nh, axis=-1, keepdims=True)))
        do_ref[...] = jnp.concatenate(do_parts, axis=1)

        dmixed = da_pool * ps * silu_pool
        dps_ref[...] += jnp.sum(da_pool * mixed * silu_pool, axis=0, keepdims=True)
        dg_pool = da_pool * mixed * ps * dsilu_pool
        du_parts = []
        for gi, win in enumerate(POOL_WINDOWS):
            lanes = slice(gi * POOL_GDIM, (gi + 1) * POOL_GDIM)
            dmx = dmixed[:, lanes]
            dpooled = _dot_nt(dmx, pw_ref[gi])
            dpw_ref[gi] += _dot_tn(pooled_parts[gi], dmx)
            dpt = dpooled * inv_cnt[gi]
            s = jnp.concatenate([dpt, halo_vmem[:, lanes]], axis=0)
            shift = 1
            while shift < win:
                s = s + pltpu.roll(s, ext - shift, 0)
                shift *= 2
            du_parts.append(s[:tr] - dpooled)
            halo_vmem[:, lanes] = dpt[:HALO]
        du = jnp.where(valid, jnp.concatenate(du_parts, axis=1), 0.0)

        for c, val in enumerate((dg_hg, du, dg_pool, dm_hg, dm_pool)):
            dp_ref[c] = val.astype(BF16)
            dbias_ref[c] += jnp.sum(val, axis=0, keepdims=True)
        dz2_ref[...] = dz2
        for c, (lhs, rhs) in enumerate(((merged, dz2), (a_hg, dy_hg), (a_pool, dy_pool))):
            lhs_ref[c] = lhs.astype(BF16)
            rhs_ref[c] = rhs.astype(BF16)

    rev = lambda s: nt - 1 - s
    rowblk = pl.BlockSpec((tr, D_MODEL), lambda s: (rev(s), 0))
    pblk = lambda c: pl.BlockSpec((1, tr, 1024), lambda s, c=c: (c, rev(s), 0))
    halo_blk = pl.BlockSpec((1, HALO, 1024), lambda s: (4, jnp.maximum(rev(s) * (tr // HALO) - 1, 0), 0))
    full = lambda shape: pl.BlockSpec(shape, lambda s: (0,) * len(shape))
    vec = full((1, D_MODEL))
    mat = full((D_MODEL, D_MODEL))
    act3 = jax.ShapeDtypeStruct((3, n_rows, D_MODEL), BF16)
    act3_blk = pl.BlockSpec((3, tr, D_MODEL), lambda s: (0, rev(s), 0))
    return pl.pallas_call(
        body, name="tail_forward_backward",
        out_shape=(jax.ShapeDtypeStruct((n_rows, D_MODEL), F32),
                   jax.ShapeDtypeStruct((5, n_rows, 1024), BF16),
                   jax.ShapeDtypeStruct((n_rows, D_MODEL), F32),
                   act3, act3,
                   jax.ShapeDtypeStruct((5, 1, 1024), F32),
                   jax.ShapeDtypeStruct((1, D_MODEL), F32),
                   jax.ShapeDtypeStruct((n_groups, POOL_GDIM, POOL_GDIM), F32),
                   jax.ShapeDtypeStruct((1, D_MODEL), F32),
                   jax.ShapeDtypeStruct((1, D_MODEL), F32),
                   jax.ShapeDtypeStruct((1, 1), F32)),
        grid=(nt,),
        in_specs=[rowblk, pblk(3), pblk(4), pblk(5), pblk(6), pblk(7), halo_blk, rowblk, rowblk,
                  vec, full((n_groups, POOL_GDIM, POOL_GDIM)), vec, mat, mat, mat, vec],
        out_specs=(rowblk, pl.BlockSpec((5, tr, 1024), lambda s: (0, rev(s), 0)), rowblk,
                   act3_blk, act3_blk,
                   full((5, 1, 1024)), vec, full((n_groups, POOL_GDIM, POOL_GDIM)), vec, vec, full((1, 1))),
        scratch_shapes=[pltpu.VMEM((HALO, D_MODEL), F32)],
        compiler_params=_params(("arbitrary",)),
    )(o, p, p, p, p, p, p, zp, tgt, hg_norm_w, pool_w, pool_scale, w_down_hg, w_down_pool, w_out, final_norm_w)


def _in_projection_backward(dp_a, dp_b, w_blocks, zp, dz2, norm_w, g_slabs):
    n_rows = zp.shape[0]
    tr = _row_tile(n_rows, 320)
    nt = n_rows // tr
    na, nb = dp_a.shape[0], dp_b.shape[0]

    def body(dpa_ref, dpb_ref, w_hbm, z_ref, dz2_ref, nw_ref, gs_hbm, dz_ref, dnw_ref, rs_hbm, w_vmem, sem, *sems):
        i = pl.program_id(0)
        start_slabs, finish_slabs = _scatter_slabs(gs_hbm, 0, rs_hbm, *sems)

        @pl.when(i == 0)
        def _():
            start_slabs()
            cp = pltpu.make_async_copy(w_hbm, w_vmem, sem)
            cp.start()
            cp.wait()
            dnw_ref[...] = jnp.zeros_like(dnw_ref)

        @pl.when(i == nt - 1)
        def _():
            finish_slabs()

        dh = jnp.zeros((tr, D_MODEL), F32)
        for j in range(na):
            dh = dh + _dot_nt(dpa_ref[j], w_vmem[j])
        for j in range(nb):
            dh = dh + _dot_nt(dpb_ref[j], w_vmem[na + j])
        z = z_ref[...]
        r = lax.rsqrt(jnp.mean(z * z, axis=-1, keepdims=True) + EPS)
        n1 = z * r
        dnw_ref[...] += jnp.sum(dh * n1, axis=0, keepdims=True)
        gh = dh * nw_ref[...]
        dz_ref[...] = dz2_ref[...] + r * (gh - n1 * jnp.mean(gh * n1, axis=-1, keepdims=True))

    rowblk = pl.BlockSpec((tr, D_MODEL), lambda i: (i, 0))
    vec = pl.BlockSpec((1, D_MODEL), lambda i: (0, 0))
    return pl.pallas_call(
        body, name="in_projection_backward",
        out_shape=(jax.ShapeDtypeStruct((n_rows, D_MODEL), F32), jax.ShapeDtypeStruct((1, D_MODEL), F32),
                   jax.ShapeDtypeStruct((N_DEV,) + g_slabs.shape[1:], g_slabs.dtype)),
        grid=(nt,),
        in_specs=[pl.BlockSpec((na, tr, 1024), lambda i: (0, i, 0)), pl.BlockSpec((nb, tr, 1024), lambda i: (0, i, 0)),
                  ANY, rowblk, rowblk, vec, ANY],
        out_specs=(rowblk, vec, ANY),
        scratch_shapes=[pltpu.VMEM((N_COLBLK, D_MODEL, 1024), BF16), pltpu.SemaphoreType.DMA(())] + list(SCATTER_SEMS),
        compiler_params=_params(("arbitrary",)),
    )(dp_a, dp_b, w_blocks, zp, dz2, norm_w, g_slabs)


def _weight_grad(xs, ys, name):
    shared = xs.ndim == 2
    n_rows, m = xs.shape[-2:]
    nb, _, n = ys.shape
    tk = _row_tile(n_rows, 832)
    n_k = n_rows // tk

    def body(x_ref, y_ref, o_ref, acc):
        k = pl.program_id(1)

        @pl.when(k == 0)
        def _():
            acc[...] = jnp.zeros_like(acc)

        acc[...] += _dot_tn(x_ref[...] if shared else x_ref[0], y_ref[0])

        @pl.when(k == n_k - 1)
        def _():
            o_ref[0] = acc[...].astype(o_ref.dtype)

    x_spec = pl.BlockSpec((tk, m), lambda j, k: (k, 0)) if shared else pl.BlockSpec((1, tk, m), lambda j, k: (j, k, 0))
    return pl.pallas_call(
        body, name=name,
        out_shape=jax.ShapeDtypeStruct((nb, m, n), BF16),
        grid=(nb, n_k),
        in_specs=[x_spec, pl.BlockSpec((1, tk, n), lambda j, k: (j, k, 0))],
        out_specs=pl.BlockSpec((1, m, n), lambda j, k: (j, 0, 0)),
        scratch_shapes=[pltpu.VMEM((m, n), F32)],
        compiler_params=_params(("arbitrary", "arbitrary")),
    )(xs, ys)


def kernel(x, meta_tokens, norm_w, w_in, b_in, lb_logits, hg_norm_w, pool_w, pool_scale, w_down_hg, w_down_pool, w_out, final_norm_w, loss_target, m_meta_tokens, m_norm_w, m_w_in, m_b_in, m_lb_logits, m_hg_norm_w, m_pool_w, m_pool_scale, m_w_down_hg, m_w_down_pool, m_w_out, m_final_norm_w, v_meta_tokens, v_norm_w, v_w_in, v_b_in, v_lb_logits, v_hg_norm_w, v_pool_w, v_pool_scale, v_w_down_hg, v_w_down_pool, v_w_out, v_final_norm_w):
    seq = x.shape[1]

    meta_full = _all_gather_small(meta_tokens).transpose(1, 0, 2).reshape(N_META, D_MODEL)
    w_rest = jnp.concatenate([w_down_hg[0].astype(BF16), w_down_pool[0].astype(BF16), w_out[0].astype(BF16),
                              pool_w[0].astype(BF16).reshape(32, 1024)], axis=0)

    zp = jnp.concatenate([jnp.zeros((PAD_ROWS, D_MODEL), F32), meta_full, x[0]], axis=0)
    tgt = jnp.concatenate([jnp.zeros((CHUNK, D_MODEL), F32), loss_target[0]], axis=0)
    h = _first_norm(zp, norm_w)
    p, w_blocks = _in_projection(h, w_in[0].astype(BF16), b_in.reshape(N_COLBLK, 1, 1024), _gather_order())
    o, states, scores, rest = _hgrn_forward(p, lb_logits, w_rest)
    wdh = rest[:, REST_W_DOWN_HG:REST_W_DOWN_HG + 128].reshape(1024, 1024)
    wdp = rest[:, REST_W_DOWN_POOL:REST_W_DOWN_POOL + 128].reshape(1024, 1024)
    wout = rest[:, REST_W_OUT:REST_W_OUT + 128].reshape(1024, 1024)
    pw = rest[:, REST_POOL_W:REST_POOL_W + 32].reshape(N_DEV, 4, 32, 256).transpose(1, 0, 2, 3).reshape(4, 256, 256)
    (d_o, dp_b, dz2, grad_lhs, grad_rhs, dbias_b, d_hgw, d_pw, d_ps, d_fnw, loss_part) = _tail(
        p, o, zp, tgt, hg_norm_w, pw, pool_scale, wdh, wdp, wout, final_norm_w.reshape(1, D_MODEL))
    n_a = N_COLBLK - dp_b.shape[0]
    g_hi = _weight_grad(h, dp_b, "weight_grad_in_hi")
    g_rows = _weight_grad(grad_lhs, grad_rhs, "weight_grad_rows")
    dp_a, dbias_a, d_lb, recv_hi, recv_rows = _hgrn_backward(p, lb_logits, states, scores, d_o, g_hi, n_a, g_rows)
    g_lo = _weight_grad(h, dp_a, "weight_grad_in_lo")
    dz, d_nw, recv_lo = _in_projection_backward(dp_a, dp_b, w_blocks, zp, dz2, norm_w, g_lo)

    lb = jax.nn.sigmoid(lb_logits[0:1] - lb_logits[1:2])
    d_l0 = d_lb * lb * (1.0 - lb)
    replicated = jnp.concatenate([dbias_a.reshape(3, 1024), dbias_b.reshape(5, 1024), d_nw, d_l0, -d_l0, d_hgw, d_ps, d_fnw,
                                  jnp.zeros((MISC_ROWS - MISC_FINAL_NORM_W - 1, 1024), F32)], axis=0)
    d_meta = dz[PAD_ROWS:CHUNK].reshape(N_META, N_DEV, 128).transpose(1, 0, 2)
    d_pw_blocks = d_pw.reshape(4, N_DEV, 32, 256).transpose(1, 0, 2, 3).reshape(N_DEV, 32, 1024)
    g_misc = jnp.concatenate([d_pw_blocks, jnp.pad(d_meta, ((0, 0), (0, 0), (0, 1024 - 128))),
                              jnp.broadcast_to(replicated[None], (N_DEV, 16, 1024))], axis=1)

    as_rows = lambda t, n: t.reshape(n, 1024)
    small = [(MISC_POOL_W, 1024, tuple(as_rows(t, 32) for t in (pool_w, m_pool_w, v_pool_w))),
             (MISC_META, 128, (meta_tokens, m_meta_tokens, v_meta_tokens)),
             (MISC_B_IN, 1024, tuple(as_rows(t, 8) for t in (b_in, m_b_in, v_b_in))),
             (MISC_NORM_W, 1024, (norm_w, m_norm_w, v_norm_w)),
             (MISC_LB, 1024, (lb_logits, m_lb_logits, v_lb_logits)),
             (MISC_HG_NORM_W, 1024, (hg_norm_w, m_hg_norm_w, v_hg_norm_w)),
             (MISC_POOL_SCALE, 1024, (pool_scale, m_pool_scale, v_pool_scale)),
             (MISC_FINAL_NORM_W, 1024, tuple(as_rows(t, 1) for t in (final_norm_w, m_final_norm_w, v_final_norm_w)))]
    res = _finish(recv_hi, recv_lo, n_a, recv_rows, g_misc, (w_in, m_w_in, v_w_in),
                  [(w_out, m_w_out, v_w_out), (w_down_hg, m_w_down_hg, v_w_down_hg), (w_down_pool, m_w_down_pool, v_w_down_pool)],
                  small)
    r_w_in, r_w_out, r_wdh, r_wdp, r_pw, r_meta, r_b_in, r_nw, r_lb, r_hgw, r_ps, r_fnw = res
    loss = lax.psum(loss_part[0, 0], ("x", "y", "c"))
    grad_x = dz[CHUNK:].reshape(1, seq, D_MODEL)
    per_kind = [(r_meta[k], r_nw[k], r_w_in[k], r_b_in[k].reshape(1, 8192), r_lb[k], r_hgw[k], r_pw[k].reshape(1, 4, 32, 256),
                 r_ps[k], r_wdh[k], r_wdp[k], r_w_out[k], r_fnw[k].reshape(1024)) for k in range(4)]
    return (loss, grad_x, *per_kind[0], *per_kind[1], *per_kind[2], *per_kind[3])
```

```python
import functools

import jax
import jax.numpy as jnp
from jax import lax
from jax.experimental import pallas as pl
from jax.experimental.pallas import tpu as pltpu

F32 = jnp.float32
BF16 = jnp.bfloat16

D_MODEL = 1024
N_META = 16
HEADS = 8
HEAD_DIM = 128
CHUNK = 64
SUB = 16
N_SUB = CHUNK // SUB
PAD_ROWS = CHUNK - N_META
POOL_WINDOWS = (2, 4, 8, 16)
POOL_GDIM = D_MODEL // len(POOL_WINDOWS)
HALO = 16
EPS = 1e-6
N_DEV = 8
N_COLBLK = 8
ADAM_LR, ADAM_B1, ADAM_B2, ADAM_EPS, ADAM_WD, ADAM_STEP = 0.001, 0.9, 0.999, 1e-08, 0.01, 10

VMEM_LIMIT = 56 * 1024 * 1024
MESH = pl.DeviceIdType.MESH
ANY = pl.BlockSpec(memory_space=pl.ANY)
HIGHEST = lax.Precision.HIGHEST
LOG2_E = 1.4426950408889634

REST_W_DOWN_HG = 0
REST_W_DOWN_POOL = 128
REST_W_OUT = 256
REST_POOL_W = 384
MISC_POOL_W = 0
MISC_META = 32
MISC_B_IN = 48
MISC_NORM_W = 56
MISC_LB = 57
MISC_HG_NORM_W = 59
MISC_POOL_SCALE = 60
MISC_FINAL_NORM_W = 61
MISC_ROWS = 64


def _params(sem=None):
    return pltpu.CompilerParams(dimension_semantics=sem, vmem_limit_bytes=VMEM_LIMIT)


def _row_tile(n_rows, prefer):
    best = 16
    for t in range(16, prefer + 1, 16):
        if n_rows % t == 0:
            best = t
    return best


def _sigmoid_pair(x):
    e = jnp.exp(-jnp.abs(x))
    r = 1.0 / (1.0 + e)
    er = e * r
    pos = x >= 0
    return jnp.where(pos, r, er), jnp.where(pos, er, r)


def _dot(a, b):
    return jnp.dot(a.astype(BF16), b.astype(BF16), preferred_element_type=F32)


def _dot_nt(a, b):
    return lax.dot_general(a.astype(BF16), b.astype(BF16), (((1,), (1,)), ((), ())), preferred_element_type=F32)


def _dot_tn(a, b):
    return lax.dot_general(a.astype(BF16), b.astype(BF16), (((0,), (0,)), ((), ())), preferred_element_type=F32)


def _device_index(px, py, pc):
    return 4 * px + 2 * py + pc


def _direct_gather(src_ref, dst_ref, send_sems, recv_sems, local_sem):
    x, y, c = lax.axis_index("x"), lax.axis_index("y"), lax.axis_index("c")
    own = pltpu.make_async_copy(src_ref, dst_ref.at[_device_index(x, y, c)], local_sem)
    sends, arrivals = [], []
    for k in range(1, N_DEV):
        peer = (1 - x if k & 4 else x, 1 - y if k & 2 else y, 1 - c if k & 1 else c)
        for slot, out in ((_device_index(x, y, c), sends), (_device_index(*peer), arrivals)):
            out.append(pltpu.make_async_remote_copy(
                src_ref=src_ref, dst_ref=dst_ref.at[slot], send_sem=send_sems.at[k - 1], recv_sem=recv_sems.at[k - 1],
                device_id=peer, device_id_type=MESH))
    return own, sends, arrivals


GATHER_SEMS = [pltpu.SemaphoreType.DMA((N_DEV - 1,)), pltpu.SemaphoreType.DMA((N_DEV - 1,)), pltpu.SemaphoreType.DMA(())]


def _all_gather_small(block):
    def body(x_ref, out_ref, send_sems, recv_sems, local_sem):
        own, sends, arrivals = _direct_gather(x_ref, out_ref, send_sems, recv_sems, local_sem)
        own.start()
        for cp in sends:
            cp.start()
        for cp in arrivals:
            cp.wait_recv()
        for cp in sends:
            cp.wait_send()
        own.wait()

    return pl.pallas_call(
        body, name="all_gather_meta",
        out_shape=jax.ShapeDtypeStruct((N_DEV,) + block.shape, block.dtype),
        in_specs=[ANY], out_specs=ANY, scratch_shapes=list(GATHER_SEMS),
    )(block)


def _peer(k):
    x, y, c = lax.axis_index("x"), lax.axis_index("y"), lax.axis_index("c")
    return (1 - x if k & 4 else x, 1 - y if k & 2 else y, 1 - c if k & 1 else c)


def _me():
    return _device_index(lax.axis_index("x"), lax.axis_index("y"), lax.axis_index("c"))


def _remote(src, dst, send_sem, recv_sem, peer_bits):
    return pltpu.make_async_remote_copy(src_ref=src, dst_ref=dst, send_sem=send_sem, recv_sem=recv_sem,
                                        device_id=_peer(peer_bits), device_id_type=MESH)


N_ROW_GRADS = 3
SCATTER_SEMS = [pltpu.SemaphoreType.DMA((N_DEV - 1,)), pltpu.SemaphoreType.DMA((N_DEV - 1,)), pltpu.SemaphoreType.DMA(())]
SCATTER_ROWS_SEMS = [pltpu.SemaphoreType.DMA((7 * N_ROW_GRADS,)), pltpu.SemaphoreType.DMA((7 * N_ROW_GRADS,)),
                     pltpu.SemaphoreType.DMA((N_ROW_GRADS,))]


def _scatter_slabs(g_ref, first, recv_ref, send_sems, recv_sems, local_sem):
    n = g_ref.shape[0]
    me = _me()

    def each(on_send, on_local, on_arrival):
        for kk in range(1, N_DEV):
            peer = jnp.bitwise_xor(me, kk)

            @pl.when((peer >= first) & (peer < first + n))
            def _(kk=kk, peer=peer):
                on_send(_remote(g_ref.at[peer - first], recv_ref.at[me], send_sems.at[kk - 1], recv_sems.at[kk - 1], kk))

        @pl.when((me >= first) & (me < first + n))
        def _():
            on_local(pltpu.make_async_copy(g_ref.at[me - first], recv_ref.at[me], local_sem))
            if on_arrival is not None:
                for kk in range(1, N_DEV):
                    on_arrival(_remote(g_ref.at[0], recv_ref.at[jnp.bitwise_xor(me, kk)], send_sems.at[kk - 1],
                                       recv_sems.at[kk - 1], kk))

    start = lambda: each(lambda cp: cp.start(), lambda cp: cp.start(), None)
    finish = lambda: each(lambda cp: cp.wait_send(), lambda cp: cp.wait(), lambda cp: cp.wait_recv())
    return start, finish


def _scatter_rows(g_ref, recv_ref, send_sems, recv_sems, local_sems):
    me = _me()
    rows = lambda m, dev: g_ref.at[m, pl.ds(dev * 128, 128), :]

    def copies():
        local = [pltpu.make_async_copy(rows(m, me), recv_ref.at[me, m], local_sems.at[m]) for m in range(N_ROW_GRADS)]
        sends, arrivals = [], []
        for m in range(N_ROW_GRADS):
            for kk in range(1, N_DEV):
                peer, sems = jnp.bitwise_xor(me, kk), (send_sems.at[7 * m + kk - 1], recv_sems.at[7 * m + kk - 1])
                sends.append(_remote(rows(m, peer), recv_ref.at[me, m], *sems, kk))
                arrivals.append(_remote(rows(m, me), recv_ref.at[peer, m], *sems, kk))
        return local, sends, arrivals

    def start():
        local, sends, _ = copies()
        for cp in local + sends:
            cp.start()

    def finish():
        local, sends, arrivals = copies()
        for cp in arrivals:
            cp.wait_recv()
        for cp in sends:
            cp.wait_send()
        for cp in local:
            cp.wait()

    return start, finish


LOW_OWNERS = 3


def _pair_reduce_low(g_lo):
    def body(g_ref, out_ref, got, kept, send_sems, recv_sems, local_sems):
        c = lax.axis_index("c")

        def to_sibling(slab, slot):
            return _remote(g_ref.at[slab], got.at[slot], send_sems.at[slot], recv_sems.at[slot], 1)

        def keep(slab, slot):
            return pltpu.make_async_copy(g_ref.at[slab], kept.at[slot], local_sems.at[slot])

        def add(slot):
            out_ref[slot] = (kept[slot].astype(F32) + got[slot].astype(F32)).astype(out_ref.dtype)

        @pl.when(c == 0)
        def _():
            copies = [to_sibling(1, 0), keep(0, 0), keep(2, 1)]
            for cp in copies:
                cp.start()
            to_sibling(0, 0).wait_recv()
            to_sibling(2, 1).wait_recv()
            copies[0].wait_send()
            copies[1].wait()
            copies[2].wait()
            add(0)
            add(1)

        @pl.when(c == 1)
        def _():
            copies = [to_sibling(0, 0), to_sibling(2, 1), keep(1, 0)]
            for cp in copies:
                cp.start()
            to_sibling(1, 0).wait_recv()
            copies[0].wait_send()
            copies[1].wait_send()
            copies[2].wait()
            add(0)
            out_ref[1] = jnp.zeros(out_ref.shape[1:], out_ref.dtype)

    pair = (2,) + g_lo.shape[1:]
    return pl.pallas_call(
        body, name="pair_reduce_low",
        out_shape=jax.ShapeDtypeStruct(pair, g_lo.dtype),
        in_specs=[ANY], out_specs=pl.BlockSpec(memory_space=pltpu.VMEM),
        scratch_shapes=[pltpu.VMEM(pair, g_lo.dtype), pltpu.VMEM(pair, g_lo.dtype), pltpu.SemaphoreType.DMA((2,)),
                        pltpu.SemaphoreType.DMA((2,)), pltpu.SemaphoreType.DMA((2,))],
        compiler_params=_params(),
    )(g_lo)


def _scatter_low(part_ref, recv_ref, send_sems, recv_sems, local_sem):
    x, y, c = lax.axis_index("x"), lax.axis_index("y"), lax.axis_index("c")
    chip = 2 * x + y
    routes = ((0, (0, 0, c), 0, None), (1, (0, 1, 0), 1, 0))

    def each(on_send, on_local, on_arrival):
        for slot, owner, owner_chip, core in routes:
            holds = (c == core) if core is not None else (c >= 0)
            rel = jnp.bitwise_xor(chip, owner_chip)

            @pl.when(holds & (rel != 0))
            def _(slot=slot, owner=owner, rel=rel):
                on_send(pltpu.make_async_remote_copy(
                    src_ref=part_ref.at[slot], dst_ref=recv_ref.at[chip], send_sem=send_sems.at[slot],
                    recv_sem=recv_sems.at[rel - 1], device_id=owner, device_id_type=MESH))

            @pl.when(holds & (rel == 0))
            def _(slot=slot, owner=owner, owner_chip=owner_chip):
                on_local(pltpu.make_async_copy(part_ref.at[slot], recv_ref.at[chip], local_sem))
                if on_arrival is not None:
                    for r in range(1, 4):
                        on_arrival(pltpu.make_async_remote_copy(
                            src_ref=part_ref.at[slot], dst_ref=recv_ref.at[r ^ owner_chip], send_sem=send_sems.at[slot],
                            recv_sem=recv_sems.at[r - 1], device_id=owner, device_id_type=MESH))

    start = lambda: each(lambda cp: cp.start(), lambda cp: cp.start(), None)
    finish = lambda: each(lambda cp: cp.wait_send(), lambda cp: cp.wait(), lambda cp: cp.wait_recv())
    return start, finish


def _adam_update(g, w, m, v):
    mn = ADAM_B1 * m + (1.0 - ADAM_B1) * g
    vn = ADAM_B2 * v + (1.0 - ADAM_B2) * (g * g)
    m_hat = mn / (1.0 - ADAM_B1 ** ADAM_STEP)
    v_hat = vn / (1.0 - ADAM_B2 ** ADAM_STEP)
    return -ADAM_LR * (m_hat / (jnp.sqrt(v_hat) + ADAM_EPS) + ADAM_WD * w), mn, vn


def _device_sum(parts):
    t = [p.astype(F32) for p in parts]
    return ((t[0] + t[1]) + (t[2] + t[3])) + ((t[4] + t[5]) + (t[6] + t[7]))


def _finish(a_hi, a_lo, n_lo, b3, misc, big, rows3, small):
    n_steps = 4
    tb, tr3 = 1024 // n_steps, 128 // n_steps

    def body(*refs):
        it = iter(refs)
        hi_ref, lo_ref, b_ref, misc_hbm = next(it), next(it), next(it), next(it)
        big_in = [next(it) for _ in range(3)]
        rows_in = [[next(it) for _ in range(3)] for _ in rows3]
        small_in = [[next(it) for _ in range(3)] for _ in small]
        big_out = [next(it) for _ in range(4)]
        rows_out = [[next(it) for _ in range(4)] for _ in rows3]
        small_out = [[next(it) for _ in range(4)] for _ in small]
        m_vmem, send_sems, recv_sems, local_sem = next(it), next(it), next(it), next(it)
        start, finish = _scatter_slabs(misc_hbm, 0, m_vmem, send_sems, recv_sems, local_sem)
        step = pl.program_id(0)

        @pl.when(step == 0)
        def _():
            start()

        def apply(g, ins, outs):
            d, mn, vn = _adam_update(g, ins[0][...], ins[1][...], ins[2][...])
            for r, val in zip(outs, (g, d, mn, vn)):
                r[...] = val

        lo = [lo_ref[s].astype(F32) for s in range(4)]
        g_big = jnp.where(_me() < n_lo, (lo[0] + lo[1]) + (lo[2] + lo[3]), _device_sum([hi_ref[s] for s in range(N_DEV)]))
        apply(g_big[None], big_in, big_out)
        for k in range(len(rows3)):
            apply(_device_sum([b_ref[s, k] for s in range(N_DEV)])[None], rows_in[k], rows_out[k])

        @pl.when(step == n_steps - 1)
        def _():
            finish()
            for (row0, lanes, ins), r_in, r_out in zip(small, small_in, small_out):
                n = ins[0].shape[0]
                apply(_device_sum([m_vmem[s, row0:row0 + n, :lanes] for s in range(N_DEV)]), r_in, r_out)

    whole = lambda shape: pl.BlockSpec(shape, lambda i: (0,) * len(shape))
    big_blk = pl.BlockSpec((1, tb, 1024), lambda i: (0, i, 0))
    rows_blk = pl.BlockSpec((1, tr3, 1024), lambda i: (0, i, 0))
    in_specs = [pl.BlockSpec((N_DEV, tb, 1024), lambda i: (0, i, 0)), pl.BlockSpec((4, tb, 1024), lambda i: (0, i, 0)),
                pl.BlockSpec((N_DEV, 3, tr3, 1024), lambda i: (0, 0, i, 0)), ANY]
    in_specs += [big_blk] * 3 + [rows_blk] * (3 * len(rows3))
    out_specs = [big_blk] * 4 + [rows_blk] * (4 * len(rows3))
    out_shape = [jax.ShapeDtypeStruct(big[0].shape, F32)] * 4
    for w, _, _ in rows3:
        out_shape += [jax.ShapeDtypeStruct(w.shape, F32)] * 4
    args = [a_hi, a_lo, b3, misc, *big]
    for t in rows3:
        args += list(t)
    for _, _, t in small:
        in_specs += [whole(t[0].shape)] * 3
        out_specs += [whole(t[0].shape)] * 4
        out_shape += [jax.ShapeDtypeStruct(t[0].shape, F32)] * 4
        args += list(t)
    outs = pl.pallas_call(
        body, name="reduce_sum_adamw", out_shape=tuple(out_shape), grid=(n_steps,),
        in_specs=in_specs, out_specs=tuple(out_specs),
        scratch_shapes=[pltpu.VMEM(misc.shape, misc.dtype)] + list(SCATTER_SEMS),
        compiler_params=_params(("arbitrary",)),
    )(*args)
    return [tuple(outs[4 * k:4 * k + 4]) for k in range(len(outs) // 4)]


def _first_norm(zp, norm_w):
    n_rows = zp.shape[0]
    tr = _row_tile(n_rows, 832)

    def body(z_ref, nw_ref, h_ref):
        z = z_ref[...]
        r = lax.rsqrt(jnp.mean(z * z, axis=-1, keepdims=True) + EPS)
        h_ref[...] = (z * r * nw_ref[...]).astype(BF16)

    return pl.pallas_call(
        body, name="first_norm",
        out_shape=jax.ShapeDtypeStruct((n_rows, D_MODEL), BF16),
        grid=(n_rows // tr,),
        in_specs=[pl.BlockSpec((tr, D_MODEL), lambda i: (i, 0)), pl.BlockSpec((1, D_MODEL), lambda i: (0, 0))],
        out_specs=pl.BlockSpec((tr, D_MODEL), lambda i: (i, 0)),
        compiler_params=_params(("arbitrary",)),
    )(zp, norm_w)


def _gather_order():
    x, y, c = lax.axis_index("x"), lax.axis_index("y"), lax.axis_index("c")
    chips = [(1 - x, y), (x, 1 - y), (1 - x, 1 - y)]
    order = [_device_index(x, y, c), _device_index(x, y, 1 - c)]
    order += [_device_index(*q, c) for q in chips] + [_device_index(*q, 1 - c) for q in chips]
    return jnp.stack(order).astype(jnp.int32)


def _in_projection(h, w_shard, b_blocks, order):
    n_rows = h.shape[0]
    tr = _row_tile(n_rows, 320)
    nt = n_rows // tr

    def body(order_ref, h_ref, w_hbm, b_ref, p_ref, w_out, w_vmem, send_sems, recv_sems, local_sem, out_sems):
        s, i = pl.program_id(0), pl.program_id(1)
        x, y, c = lax.axis_index("x"), lax.axis_index("y"), lax.axis_index("c")
        me, sibling = (x, y, c), (x, y, 1 - c)
        chips = [(1 - x, y), (x, 1 - y), (1 - x, 1 - y)]

        def slot(px, py, pc):
            return w_vmem.at[_device_index(px, py, pc)]

        def copy(k, blk, to, src=None):
            return pltpu.make_async_remote_copy(
                src_ref=slot(*blk) if src is None else src, dst_ref=slot(*blk),
                send_sem=send_sems.at[k], recv_sem=recv_sems.at[k], device_id=to, device_id_type=MESH)

        own = pltpu.make_async_copy(w_hbm, slot(*me), local_sem)
        first = [copy(0, me, sibling, src=w_hbm)] + [copy(1 + j, me, (*q, c), src=w_hbm) for j, q in enumerate(chips)]
        passed = [copy(4 + j, (*q, c), sibling) for j, q in enumerate(chips)]
        arrivals = [None, copy(0, sibling, me)]
        arrivals += [copy(1 + j, (*q, c), me) for j, q in enumerate(chips)]
        arrivals += [copy(4 + j, (*q, 1 - c), me) for j, q in enumerate(chips)]

        def keep(step):
            return pltpu.make_async_copy(w_vmem.at[order_ref[step]], w_out.at[order_ref[step]], out_sems.at[step])

        for step in range(N_DEV):
            @pl.when((i == 0) & (s == step))
            def _(step=step):
                if step == 0:
                    own.start()
                    for cp in first:
                        cp.start()
                    own.wait()
                else:
                    arrivals[step].wait_recv()
                    if 2 <= step <= 4:
                        passed[step - 2].start()
                keep(step).start()

        p_ref[0] = jnp.dot(h_ref[...], w_vmem[order_ref[s]], preferred_element_type=F32) + b_ref[0]

        @pl.when((s == N_DEV - 1) & (i == nt - 1))
        def _():
            for cp in first + passed:
                cp.wait_send()
            for step in range(N_DEV):
                keep(step).wait()

    return pl.pallas_call(
        body, name="in_projection_gather",
        out_shape=(jax.ShapeDtypeStruct((N_COLBLK, n_rows, 1024), F32),
                   jax.ShapeDtypeStruct((N_DEV, D_MODEL, 1024), BF16)),
        grid_spec=pltpu.PrefetchScalarGridSpec(
            num_scalar_prefetch=1, grid=(N_DEV, nt),
            in_specs=[pl.BlockSpec((tr, D_MODEL), lambda s, i, o: (i, 0)), ANY,
                      pl.BlockSpec((1, 1, 1024), lambda s, i, o: (o[s], 0, 0))],
            out_specs=(pl.BlockSpec((1, tr, 1024), lambda s, i, o: (o[s], i, 0)), ANY),
            scratch_shapes=[pltpu.VMEM((N_DEV, D_MODEL, 1024), BF16), pltpu.SemaphoreType.DMA((7,)),
                            pltpu.SemaphoreType.DMA((7,)), pltpu.SemaphoreType.DMA(()), pltpu.SemaphoreType.DMA((N_DEV,))]),
        compiler_params=_params(("arbitrary", "arbitrary")),
    )(order, h, w_shard, b_blocks)


def _lower_bound(lb_ref):
    l0, l1 = lb_ref[0:1, :], lb_ref[1:2, :]
    _, lb = _sigmoid_pair(l1 - l0)
    return lb


def _chunk_gates(fz, lb, valid):
    sig, nsig = _sigmoid_pair(fz)
    f = lb + (1.0 - lb) * sig
    g = jnp.where(valid, jnp.log(f), 0.0)
    k = jnp.where(valid, (1.0 - lb) * nsig, 0.0)
    return sig, nsig, f, g, k


def _tri(n, upper=False):
    r = lax.broadcasted_iota(jnp.int32, (n, n), 0)
    c = lax.broadcasted_iota(jnp.int32, (n, n), 1)
    return jnp.where((r <= c) if upper else (r >= c), 1.0, 0.0).astype(F32)


def _intra_scores(q_ref, k_ref, b2_ref, a_ref, kt_ref, col0):
    cols = pl.ds(col0, HEAD_DIM)
    rows_s = lax.broadcasted_iota(jnp.int32, (SUB, 1), 0)
    lanes_c = lax.broadcasted_iota(jnp.int32, (1, CHUNK), 1)
    kt_ref[...] = jnp.zeros_like(kt_ref)
    for i in range(N_SUB):
        lo = i * SUB
        qi = q_ref[lo:lo + SUB, cols]
        bi = b2_ref[lo:lo + SUB, cols]
        if i == 0:
            acc = jnp.zeros((SUB, CHUNK), F32)
        else:
            ref_i = b2_ref[lo:lo + 1, cols]
            qt = qi * jnp.exp2(bi - ref_i)
            kt_ref[0:lo, :] = k_ref[0:lo, cols] * jnp.exp2(ref_i - b2_ref[0:lo, cols])
            acc = _dot_nt(qt, kt_ref[...])
        for s in range(SUB):
            b_s = b2_ref[lo + s:lo + s + 1, cols]
            k_s = k_ref[lo + s:lo + s + 1, cols]
            w = jnp.exp2(jnp.minimum(bi - b_s, 0.0))
            col = jnp.sum((qi * w) * k_s, axis=-1, keepdims=True)
            acc = jnp.where(lanes_c == lo + s, col, acc)
        a_ref[lo:lo + SUB, :] = jnp.where(lanes_c <= lo + rows_s, acc, 0.0)


def _hgrn_forward(p, lb_logits, w_rest):
    n_rows = p.shape[1]
    n_chunks = n_rows // CHUNK
    width = HEADS * HEAD_DIM

    def body(q_ref, fz_ref, v_ref, lb_ref, rest_ref, o_ref, st_out_ref, a_out_ref, rest_out,
             state, k_vmem, b2_vmem, a_vmem, kt_vmem, send_sems, recv_sems, local_sem):
        n = pl.program_id(0)
        own, sends, arrivals = _direct_gather(rest_ref, rest_out, send_sems, recv_sems, local_sem)

        @pl.when(n == 0)
        def _():
            state[...] = jnp.zeros_like(state)
            own.start()
            for cp in sends:
                cp.start()

        rows = n * CHUNK + lax.broadcasted_iota(jnp.int32, (CHUNK, 1), 0)
        valid = rows >= PAD_ROWS
        lb = _lower_bound(lb_ref)
        _, _, _, g, k = _chunk_gates(fz_ref[0], lb, valid)
        k_vmem[...] = k
        b2_vmem[...] = jnp.dot(_tri(CHUNK), g, precision=HIGHEST, preferred_element_type=F32) * LOG2_E
        q_view = q_ref.at[0]
        for h in range(HEADS):
            cols = pl.ds(h * HEAD_DIM, HEAD_DIM)
            st = state[h]
            st_out_ref[0, h] = st
            bh = b2_vmem[:, cols]
            kh = k_vmem[:, cols]
            vh = jnp.where(valid, v_ref[0, :, cols], 0.0)
            qe = q_ref[0, :, cols] * jnp.exp2(bh)
            _intra_scores(q_view, k_vmem, b2_vmem, a_vmem, kt_vmem, h * HEAD_DIM)
            a = a_vmem[...].astype(BF16)
            a_out_ref[0, h] = a
            o_ref[:, cols] = _dot_nt(qe, st) + _dot(a, vh)
            b_last = b2_vmem[CHUNK - 1:CHUNK, cols]
            kd = kh * jnp.exp2(b_last - bh)
            state[h] = st * jnp.exp2(b_last) + _dot_tn(vh, kd)

        @pl.when(n == n_chunks - 1)
        def _():
            for cp in arrivals:
                cp.wait_recv()
            for cp in sends:
                cp.wait_send()
            own.wait()

    blk = lambda c: pl.BlockSpec((1, CHUNK, width), lambda n, c=c: (c, n, 0))
    return pl.pallas_call(
        body, name="hgrn_forward",
        out_shape=(jax.ShapeDtypeStruct((n_rows, width), F32),
                   jax.ShapeDtypeStruct((n_chunks, HEADS, HEAD_DIM, HEAD_DIM), F32),
                   jax.ShapeDtypeStruct((n_chunks, HEADS, CHUNK, CHUNK), BF16),
                   jax.ShapeDtypeStruct((N_DEV,) + w_rest.shape, w_rest.dtype)),
        grid=(n_chunks,),
        in_specs=[blk(0), blk(1), blk(2), pl.BlockSpec((2, width), lambda n: (0, 0)), ANY],
        out_specs=(pl.BlockSpec((CHUNK, width), lambda n: (n, 0)),
                   pl.BlockSpec((1, HEADS, HEAD_DIM, HEAD_DIM), lambda n: (n, 0, 0, 0)),
                   pl.BlockSpec((1, HEADS, CHUNK, CHUNK), lambda n: (n, 0, 0, 0)), ANY),
        scratch_shapes=[pltpu.VMEM((HEADS, HEAD_DIM, HEAD_DIM), F32), pltpu.VMEM((CHUNK, width), F32),
                        pltpu.VMEM((CHUNK, width), F32), pltpu.VMEM((CHUNK, CHUNK), F32),
                        pltpu.VMEM((CHUNK, HEAD_DIM), F32)] + list(GATHER_SEMS),
        compiler_params=_params(("arbitrary",)),
    )(p, p, p, lb_logits, w_rest)


def _hgrn_backward(p, lb_logits, states, scores, d_o, g_slabs, first_owner, g_rows):
    n_rows = p.shape[1]
    n_chunks = n_rows // CHUNK
    width = HEADS * HEAD_DIM

    def body(q_ref, fz_ref, v_ref, lb_ref, st_ref, a_ref, do_ref, gs_hbm, gr_hbm, dp_ref, dbias_ref, dlb_ref, rs_hbm, rr_hbm,
             dstate, k_vmem, b2_vmem, kt_vmem, dqi_vmem, dki_vmem, *sems):
        step = pl.program_id(0)
        n = n_chunks - 1 - step
        start_slabs, finish_slabs = _scatter_slabs(gs_hbm, first_owner, rs_hbm, *sems[:3])
        start_rows, finish_rows = _scatter_rows(gr_hbm, rr_hbm, *sems[3:])

        @pl.when(step == 0)
        def _():
            dstate[...] = jnp.zeros_like(dstate)
            dbias_ref[...] = jnp.zeros_like(dbias_ref)
            dlb_ref[...] = jnp.zeros_like(dlb_ref)
            start_slabs()
            start_rows()

        @pl.when(step == n_chunks - 1)
        def _():
            finish_slabs()
            finish_rows()

        rows = n * CHUNK + lax.broadcasted_iota(jnp.int32, (CHUNK, 1), 0)
        valid = rows >= PAD_ROWS
        lb = _lower_bound(lb_ref)
        sig, nsig, f, g, k = _chunk_gates(fz_ref[0], lb, valid)
        k_vmem[...] = k
        b2_vmem[...] = jnp.dot(_tri(CHUNK), g, precision=HIGHEST, preferred_element_type=F32) * LOG2_E
        rows_c = lax.broadcasted_iota(jnp.int32, (CHUNK, 1), 0)
        lanes_c = lax.broadcasted_iota(jnp.int32, (1, CHUNK), 1)
        causal = lax.broadcasted_iota(jnp.int32, (CHUNK, CHUNK), 0) >= lax.broadcasted_iota(jnp.int32, (CHUNK, CHUNK), 1)
        tri_up = _tri(CHUNK, upper=True)
        for h in range(HEADS):
            cols = pl.ds(h * HEAD_DIM, HEAD_DIM)
            st = st_ref[0, h]
            dst = dstate[h]
            qh = q_ref[0, :, cols]
            bh = b2_vmem[:, cols]
            kh = k_vmem[:, cols]
            vh = jnp.where(valid, v_ref[0, :, cols], 0.0)
            doh = do_ref[:, cols]
            eb = jnp.exp2(bh)
            qe = qh * eb
            b_last = b2_vmem[CHUNK - 1:CHUNK, cols]
            e_last = jnp.exp2(b_last)
            decay_k = jnp.exp2(b_last - bh)
            kd = kh * decay_k
            dqe = _dot(doh, st)
            da = jnp.where(causal, _dot_nt(doh, vh), 0.0)
            dv = _dot_tn(a_ref[0, h], doh) + _dot_nt(kd, dst)
            dkd = _dot(vh, dst)
            dstate[h] = dst * e_last + _dot_tn(doh, qe)
            db_last = (jnp.sum(dst * st, axis=0, keepdims=True) * e_last
                       + jnp.sum(dkd * kd, axis=0, keepdims=True))
            dki_vmem[...] = jnp.zeros_like(dki_vmem)
            kt_vmem[...] = jnp.zeros_like(kt_vmem)
            for i in range(N_SUB):
                lo = i * SUB
                qi = q_ref[0, lo:lo + SUB, cols]
                bi = b2_vmem[lo:lo + SUB, cols]
                da_i = da[lo:lo + SUB, :]
                if i == 0:
                    dq_i = jnp.zeros((SUB, HEAD_DIM), F32)
                else:
                    ref_i = b2_vmem[lo:lo + 1, cols]
                    eq = jnp.exp2(bi - ref_i)
                    ek = jnp.exp2(ref_i - b2_vmem[0:lo, cols])
                    kt_vmem[0:lo, :] = k_vmem[0:lo, cols] * ek
                    dq_i = _dot(da_i, kt_vmem[...]) * eq
                    dki_vmem[0:lo, :] += _dot_tn(da_i, qi * eq)[0:lo] * ek
                for s in range(SUB):
                    b_s = b2_vmem[lo + s:lo + s + 1, cols]
                    k_s = k_vmem[lo + s:lo + s + 1, cols]
                    w = jnp.exp2(jnp.minimum(bi - b_s, 0.0))
                    da_col = jnp.sum(jnp.where(lanes_c == lo + s, da_i, 0.0), axis=-1, keepdims=True)
                    gw = da_col * w
                    dq_i = dq_i + gw * k_s
                    dki_vmem[lo + s:lo + s + 1, :] += jnp.sum(gw * qi, axis=0, keepdims=True)
                dqi_vmem[lo:lo + SUB, :] = dq_i
            dq_intra = dqi_vmem[...]
            dk_intra = dki_vmem[...]
            dq = dqe * eb + dq_intra
            dk = dkd * decay_k + dk_intra
            db = dqe * qe - dkd * kd + qh * dq_intra - kh * dk_intra
            db = db + jnp.where(rows_c == CHUNK - 1, db_last, 0.0)
            dg = jnp.dot(tri_up, db, precision=HIGHEST, preferred_element_type=F32)
            fh = f[:, h * HEAD_DIM:(h + 1) * HEAD_DIM]
            sh = sig[:, h * HEAD_DIM:(h + 1) * HEAD_DIM]
            nh = nsig[:, h * HEAD_DIM:(h + 1) * HEAD_DIM]
            lbh = lb[:, h * HEAD_DIM:(h + 1) * HEAD_DIM]
            df = jnp.where(valid, dg / fh - dk, 0.0)
            dfz = df * (1.0 - lbh) * sh * nh
            dq = jnp.where(valid, dq, 0.0)
            dv = jnp.where(valid, dv, 0.0)
            dlb_ref[:, cols] += jnp.sum(df * nh, axis=0, keepdims=True)
            dp_ref[0, :, cols] = dq.astype(BF16)
            dp_ref[1, :, cols] = dfz.astype(BF16)
            dp_ref[2, :, cols] = dv.astype(BF16)
            dbias_ref[0, :, cols] += jnp.sum(dq, axis=0, keepdims=True)
            dbias_ref[1, :, cols] += jnp.sum(dfz, axis=0, keepdims=True)
            dbias_ref[2, :, cols] += jnp.sum(dv, axis=0, keepdims=True)

    rev = lambda s: n_chunks - 1 - s
    blk = lambda c: pl.BlockSpec((1, CHUNK, width), lambda s, c=c: (c, rev(s), 0))
    return pl.pallas_call(
        body, name="hgrn_backward",
        out_shape=(jax.ShapeDtypeStruct((3, n_rows, width), BF16),
                   jax.ShapeDtypeStruct((3, 1, width), F32),
                   jax.ShapeDtypeStruct((1, width), F32),
                   jax.ShapeDtypeStruct((N_DEV,) + g_slabs.shape[1:], g_slabs.dtype),
                   jax.ShapeDtypeStruct((N_DEV, N_ROW_GRADS, 128, g_rows.shape[2]), g_rows.dtype)),
        grid=(n_chunks,),
        in_specs=[blk(0), blk(1), blk(2), pl.BlockSpec((2, width), lambda s: (0, 0)),
                  pl.BlockSpec((1, HEADS, HEAD_DIM, HEAD_DIM), lambda s: (rev(s), 0, 0, 0)),
                  pl.BlockSpec((1, HEADS, CHUNK, CHUNK), lambda s: (rev(s), 0, 0, 0)),
                  pl.BlockSpec((CHUNK, width), lambda s: (rev(s), 0)), ANY, ANY],
        out_specs=(pl.BlockSpec((3, CHUNK, width), lambda s: (0, rev(s), 0)),
                   pl.BlockSpec((3, 1, width), lambda s: (0, 0, 0)),
                   pl.BlockSpec((1, width), lambda s: (0, 0)), ANY, ANY),
        scratch_shapes=[pltpu.VMEM((HEADS, HEAD_DIM, HEAD_DIM), F32), pltpu.VMEM((CHUNK, width), F32),
                        pltpu.VMEM((CHUNK, width), F32), pltpu.VMEM((CHUNK, HEAD_DIM), F32),
                        pltpu.VMEM((CHUNK, HEAD_DIM), F32), pltpu.VMEM((CHUNK, HEAD_DIM), F32)]
        + list(SCATTER_SEMS) + list(SCATTER_ROWS_SEMS),
        compiler_params=_params(("arbitrary",)),
    )(p, p, p, lb_logits, states, scores, d_o, g_slabs, g_rows)


def _silu_and_grad(x):
    s, ns = _sigmoid_pair(x)
    return x * s, s * (1.0 + x * ns)


def _tail(p, o, zp, tgt, hg_norm_w, pool_w, pool_scale, w_down_hg, w_down_pool, w_out, final_norm_w):
    n_rows = zp.shape[0]
    tr = _row_tile(n_rows, 160)
    nt = n_rows // tr
    ext = tr + HALO
    n_groups = len(POOL_WINDOWS)

    def body(o_ref, ghg_ref, u_ref, gpool_ref, mhg_ref, mpool_ref, uhalo_ref, z_ref, tgt_ref,
             hgw_ref, pw_ref, ps_ref, wdh_ref, wdp_ref, wout_ref, fnw_ref,
             do_ref, dp_ref, dz2_ref, lhs_ref, rhs_ref,
             dbias_ref, dhgw_ref, dpw_ref, dps_ref, dfnw_ref, loss_ref, halo_vmem):
        step = pl.program_id(0)
        ti = nt - 1 - step

        @pl.when(step == 0)
        def _():
            halo_vmem[...] = jnp.zeros_like(halo_vmem)
            for r in (dbias_ref, dhgw_ref, dpw_ref, dps_ref, dfnw_ref, loss_ref):
                r[...] = jnp.zeros_like(r)

        rows = ti * tr + lax.broadcasted_iota(jnp.int32, (tr, 1), 0)
        valid = rows >= PAD_ROWS
        in_loss = rows >= CHUNK
        count_pos = jnp.maximum(rows - PAD_ROWS + 1, 1).astype(F32)

        o = o_ref[...]
        hgw = hgw_ref[...]
        inv_o, on_parts = [], []
        for h in range(HEADS):
            oh = o[:, h * HEAD_DIM:(h + 1) * HEAD_DIM]
            r = lax.rsqrt(jnp.mean(oh * oh, axis=-1, keepdims=True) + EPS)
            inv_o.append(r)
            on_parts.append(oh * r)
        o_hat = jnp.concatenate(on_parts, axis=1)
        o_n = o_hat * hgw
        g_hg = ghg_ref[0]
        silu_hg, dsilu_hg = _silu_and_grad(g_hg)
        a_hg = o_n * silu_hg
        y_hg = _dot(a_hg, wdh_ref[...])

        u = jnp.where(valid, u_ref[0], 0.0)
        u_prev = jnp.where(ti > 0, uhalo_ref[0], 0.0)
        u_ext = jnp.concatenate([u_prev, u], axis=0)
        pooled_parts, mixed_parts, inv_cnt = [], [], []
        for gi, win in enumerate(POOL_WINDOWS):
            lanes = slice(gi * POOL_GDIM, (gi + 1) * POOL_GDIM)
            s = u_ext[:, lanes]
            shift = 1
            while shift < win:
                s = s + pltpu.roll(s, shift, 0)
                shift *= 2
            ic = 1.0 / jnp.minimum(count_pos, float(win))
            inv_cnt.append(ic)
            pooled = s[HALO:] * ic - u[:, lanes]
            pooled_parts.append(pooled)
            mixed_parts.append(_dot(pooled, pw_ref[gi]))
        mixed = jnp.concatenate(mixed_parts, axis=1)
        ps = ps_ref[...]
        g_pool = gpool_ref[0]
        silu_pool, dsilu_pool = _silu_and_grad(g_pool)
        a_pool = mixed * ps * silu_pool
        y_pool = _dot(a_pool, wdp_ref[...])

        m_hg, m_pool = mhg_ref[0], mpool_ref[0]
        s_hg, ns_hg = _sigmoid_pair(m_hg)
        s_pool, ns_pool = _sigmoid_pair(m_pool)
        merged = s_hg * y_hg + s_pool * y_pool
        z2 = z_ref[...] + _dot(merged, wout_ref[...])
        r2 = lax.rsqrt(jnp.mean(z2 * z2, axis=-1, keepdims=True) + EPS)
        n2 = z2 * r2
        fnw = fnw_ref[...]
        err = jnp.where(in_loss, n2 * fnw - tgt_ref[...], 0.0)
        loss_ref[...] += jnp.sum(jnp.sum(err * err, axis=0, keepdims=True), axis=1, keepdims=True) * (0.5 / D_MODEL)
        dy = err * (1.0 / D_MODEL)

        dfnw_ref[...] += jnp.sum(dy * n2, axis=0, keepdims=True)
        gy = dy * fnw
        dz2 = r2 * (gy - n2 * jnp.mean(gy * n2, axis=-1, keepdims=True))
        dmerged = _dot_nt(dz2, wout_ref[...])
        dy_hg = s_hg * dmerged
        dy_pool = s_pool * dmerged
        dm_hg = dmerged * y_hg * s_hg * ns_hg
        dm_pool = dmerged * y_pool * s_pool * ns_pool
        da_hg = _dot_nt(dy_hg, wdh_ref[...])
        da_pool = _dot_nt(dy_pool, wdp_ref[...])

        d_on = da_hg * silu_hg
        dg_hg = da_hg * o_n * dsilu_hg
        dhgw_ref[...] += jnp.sum(d_on * o_hat, axis=0, keepdims=True)
        gyo = d_on * hgw
        do_parts = []
        for h in range(HEADS):
            lanes = slice(h * HEAD_DIM, (h + 1) * HEAD_DIM)
            gh, nh = gyo[:, lanes], o_hat[:, lanes]
            do_parts.append(inv_o[h] * (gh - nh * jnp.mean(gh * nh, axis=-1, keepdims=True)))
        do_ref[...] = jnp.concatenate(do_parts, axis=1)

        dmixed = da_pool * ps * silu_pool
        dps_ref[...] += jnp.sum(da_pool * mixed * silu_pool, axis=0, keepdims=True)
        dg_pool = da_pool * mixed * ps * dsilu_pool
        du_parts = []
        for gi, win in enumerate(POOL_WINDOWS):
            lanes = slice(gi * POOL_GDIM, (gi + 1) * POOL_GDIM)
            dmx = dmixed[:, lanes]
            dpooled = _dot_nt(dmx, pw_ref[gi])
            dpw_ref[gi] += _dot_tn(pooled_parts[gi], dmx)
            dpt = dpooled * inv_cnt[gi]
            s = jnp.concatenate([dpt, halo_vmem[:, lanes]], axis=0)
            shift = 1
            while shift < win:
                s = s + pltpu.roll(s, ext - shift, 0)
                shift *= 2
            du_parts.append(s[:tr] - dpooled)
            halo_vmem[:, lanes] = dpt[:HALO]
        du = jnp.where(valid, jnp.concatenate(du_parts, axis=1), 0.0)

        for c, val in enumerate((dg_hg, du, dg_pool, dm_hg, dm_pool)):
            dp_ref[c] = val.astype(BF16)
            dbias_ref[c] += jnp.sum(val, axis=0, keepdims=True)
        dz2_ref[...] = dz2
        for c, (lhs, rhs) in enumerate(((merged, dz2), (a_hg, dy_hg), (a_pool, dy_pool))):
            lhs_ref[c] = lhs.astype(BF16)
            rhs_ref[c] = rhs.astype(BF16)

    rev = lambda s: nt - 1 - s
    rowblk = pl.BlockSpec((tr, D_MODEL), lambda s: (rev(s), 0))
    pblk = lambda c: pl.BlockSpec((1, tr, 1024), lambda s, c=c: (c, rev(s), 0))
    halo_blk = pl.BlockSpec((1, HALO, 1024), lambda s: (4, jnp.maximum(rev(s) * (tr // HALO) - 1, 0), 0))
    full = lambda shape: pl.BlockSpec(shape, lambda s: (0,) * len(shape))
    vec = full((1, D_MODEL))
    mat = full((D_MODEL, D_MODEL))
    act3 = jax.ShapeDtypeStruct((3, n_rows, D_MODEL), BF16)
    act3_blk = pl.BlockSpec((3, tr, D_MODEL), lambda s: (0, rev(s), 0))
    return pl.pallas_call(
        body, name="tail_forward_backward",
        out_shape=(jax.ShapeDtypeStruct((n_rows, D_MODEL), F32),
                   jax.ShapeDtypeStruct((5, n_rows, 1024), BF16),
                   jax.ShapeDtypeStruct((n_rows, D_MODEL), F32),
                   act3, act3,
                   jax.ShapeDtypeStruct((5, 1, 1024), F32),
                   jax.ShapeDtypeStruct((1, D_MODEL), F32),
                   jax.ShapeDtypeStruct((n_groups, POOL_GDIM, POOL_GDIM), F32),
                   jax.ShapeDtypeStruct((1, D_MODEL), F32),
                   jax.ShapeDtypeStruct((1, D_MODEL), F32),
                   jax.ShapeDtypeStruct((1, 1), F32)),
        grid=(nt,),
        in_specs=[rowblk, pblk(3), pblk(4), pblk(5), pblk(6), pblk(7), halo_blk, rowblk, rowblk,
                  vec, full((n_groups, POOL_GDIM, POOL_GDIM)), vec, mat, mat, mat, vec],
        out_specs=(rowblk, pl.BlockSpec((5, tr, 1024), lambda s: (0, rev(s), 0)), rowblk,
                   act3_blk, act3_blk,
                   full((5, 1, 1024)), vec, full((n_groups, POOL_GDIM, POOL_GDIM)), vec, vec, full((1, 1))),
        scratch_shapes=[pltpu.VMEM((HALO, D_MODEL), F32)],
        compiler_params=_params(("arbitrary",)),
    )(o, p, p, p, p, p, p, zp, tgt, hg_norm_w, pool_w, pool_scale, w_down_hg, w_down_pool, w_out, final_norm_w)


def _in_projection_backward(dp_a, dp_b, w_blocks, zp, dz2, norm_w, chip_sums):
    n_rows = zp.shape[0]
    tr = _row_tile(n_rows, 320)
    nt = n_rows // tr
    na, nb = dp_a.shape[0], dp_b.shape[0]

    def body(dpa_ref, dpb_ref, w_hbm, z_ref, dz2_ref, nw_ref, gs_hbm, dz_ref, dnw_ref, rs_hbm, w_vmem, sem, *sems):
        i = pl.program_id(0)
        start_slabs, finish_slabs = _scatter_low(gs_hbm, rs_hbm, *sems)

        @pl.when(i == 0)
        def _():
            start_slabs()
            cp = pltpu.make_async_copy(w_hbm, w_vmem, sem)
            cp.start()
            cp.wait()
            dnw_ref[...] = jnp.zeros_like(dnw_ref)

        @pl.when(i == nt - 1)
        def _():
            finish_slabs()

        dh = jnp.zeros((tr, D_MODEL), F32)
        for j in range(na):
            dh = dh + _dot_nt(dpa_ref[j], w_vmem[j])
        for j in range(nb):
            dh = dh + _dot_nt(dpb_ref[j], w_vmem[na + j])
        z = z_ref[...]
        r = lax.rsqrt(jnp.mean(z * z, axis=-1, keepdims=True) + EPS)
        n1 = z * r
        dnw_ref[...] += jnp.sum(dh * n1, axis=0, keepdims=True)
        gh = dh * nw_ref[...]
        dz_ref[...] = dz2_ref[...] + r * (gh - n1 * jnp.mean(gh * n1, axis=-1, keepdims=True))

    rowblk = pl.BlockSpec((tr, D_MODEL), lambda i: (i, 0))
    vec = pl.BlockSpec((1, D_MODEL), lambda i: (0, 0))
    return pl.pallas_call(
        body, name="in_projection_backward",
        out_shape=(jax.ShapeDtypeStruct((n_rows, D_MODEL), F32), jax.ShapeDtypeStruct((1, D_MODEL), F32),
                   jax.ShapeDtypeStruct((4,) + chip_sums.shape[1:], chip_sums.dtype)),
        grid=(nt,),
        in_specs=[pl.BlockSpec((na, tr, 1024), lambda i: (0, i, 0)), pl.BlockSpec((nb, tr, 1024), lambda i: (0, i, 0)),
                  ANY, rowblk, rowblk, vec, ANY],
        out_specs=(rowblk, vec, ANY),
        scratch_shapes=[pltpu.VMEM((N_COLBLK, D_MODEL, 1024), BF16), pltpu.SemaphoreType.DMA(()),
                        pltpu.SemaphoreType.DMA((2,)), pltpu.SemaphoreType.DMA((3,)), pltpu.SemaphoreType.DMA(())],
        compiler_params=_params(("arbitrary",)),
    )(dp_a, dp_b, w_blocks, zp, dz2, norm_w, chip_sums)


def _weight_grad(xs, ys, name):
    shared = xs.ndim == 2
    n_rows, m = xs.shape[-2:]
    nb, _, n = ys.shape
    tk = _row_tile(n_rows, 832)
    n_k = n_rows // tk

    def body(x_ref, y_ref, o_ref, acc):
        k = pl.program_id(1)

        @pl.when(k == 0)
        def _():
            acc[...] = jnp.zeros_like(acc)

        acc[...] += _dot_tn(x_ref[...] if shared else x_ref[0], y_ref[0])

        @pl.when(k == n_k - 1)
        def _():
            o_ref[0] = acc[...].astype(o_ref.dtype)

    x_spec = pl.BlockSpec((tk, m), lambda j, k: (k, 0)) if shared else pl.BlockSpec((1, tk, m), lambda j, k: (j, k, 0))
    return pl.pallas_call(
        body, name=name,
        out_shape=jax.ShapeDtypeStruct((nb, m, n), BF16),
        grid=(nb, n_k),
        in_specs=[x_spec, pl.BlockSpec((1, tk, n), lambda j, k: (j, k, 0))],
        out_specs=pl.BlockSpec((1, m, n), lambda j, k: (j, 0, 0)),
        scratch_shapes=[pltpu.VMEM((m, n), F32)],
        compiler_params=_params(("arbitrary", "arbitrary")),
    )(xs, ys)


def kernel(x, meta_tokens, norm_w, w_in, b_in, lb_logits, hg_norm_w, pool_w, pool_scale, w_down_hg, w_down_pool, w_out, final_norm_w, loss_target, m_meta_tokens, m_norm_w, m_w_in, m_b_in, m_lb_logits, m_hg_norm_w, m_pool_w, m_pool_scale, m_w_down_hg, m_w_down_pool, m_w_out, m_final_norm_w, v_meta_tokens, v_norm_w, v_w_in, v_b_in, v_lb_logits, v_hg_norm_w, v_pool_w, v_pool_scale, v_w_down_hg, v_w_down_pool, v_w_out, v_final_norm_w):
    seq = x.shape[1]

    meta_full = _all_gather_small(meta_tokens).transpose(1, 0, 2).reshape(N_META, D_MODEL)
    w_rest = jnp.concatenate([w_down_hg[0].astype(BF16), w_down_pool[0].astype(BF16), w_out[0].astype(BF16),
                              pool_w[0].astype(BF16).reshape(32, 1024)], axis=0)

    zp = jnp.concatenate([jnp.zeros((PAD_ROWS, D_MODEL), F32), meta_full, x[0]], axis=0)
    tgt = jnp.concatenate([jnp.zeros((CHUNK, D_MODEL), F32), loss_target[0]], axis=0)
    h = _first_norm(zp, norm_w)
    p, w_blocks = _in_projection(h, w_in[0].astype(BF16), b_in.reshape(N_COLBLK, 1, 1024), _gather_order())
    o, states, scores, rest = _hgrn_forward(p, lb_logits, w_rest)
    wdh = rest[:, REST_W_DOWN_HG:REST_W_DOWN_HG + 128].reshape(1024, 1024)
    wdp = rest[:, REST_W_DOWN_POOL:REST_W_DOWN_POOL + 128].reshape(1024, 1024)
    wout = rest[:, REST_W_OUT:REST_W_OUT + 128].reshape(1024, 1024)
    pw = rest[:, REST_POOL_W:REST_POOL_W + 32].reshape(N_DEV, 4, 32, 256).transpose(1, 0, 2, 3).reshape(4, 256, 256)
    (d_o, dp_b, dz2, grad_lhs, grad_rhs, dbias_b, d_hgw, d_pw, d_ps, d_fnw, loss_part) = _tail(
        p, o, zp, tgt, hg_norm_w, pw, pool_scale, wdh, wdp, wout, final_norm_w.reshape(1, D_MODEL))
    n_a = N_COLBLK - dp_b.shape[0]
    g_hi = _weight_grad(h, dp_b, "weight_grad_in_hi")
    g_rows = _weight_grad(grad_lhs, grad_rhs, "weight_grad_rows")
    dp_a, dbias_a, d_lb, recv_hi, recv_rows = _hgrn_backward(p, lb_logits, states, scores, d_o, g_hi, n_a, g_rows)
    assert n_a == LOW_OWNERS
    g_lo = _weight_grad(h, dp_a, "weight_grad_in_lo")
    dz, d_nw, recv_lo = _in_projection_backward(dp_a, dp_b, w_blocks, zp, dz2, norm_w, _pair_reduce_low(g_lo))

    lb = jax.nn.sigmoid(lb_logits[0:1] - lb_logits[1:2])
    d_l0 = d_lb * lb * (1.0 - lb)
    replicated = jnp.concatenate([dbias_a.reshape(3, 1024), dbias_b.reshape(5, 1024), d_nw, d_l0, -d_l0, d_hgw, d_ps, d_fnw,
                                  jnp.zeros((MISC_ROWS - MISC_FINAL_NORM_W - 1, 1024), F32)], axis=0)
    d_meta = dz[PAD_ROWS:CHUNK].reshape(N_META, N_DEV, 128).transpose(1, 0, 2)
    d_pw_blocks = d_pw.reshape(4, N_DEV, 32, 256).transpose(1, 0, 2, 3).reshape(N_DEV, 32, 1024)
    g_misc = jnp.concatenate([d_pw_blocks, jnp.pad(d_meta, ((0, 0), (0, 0), (0, 1024 - 128))),
                              jnp.broadcast_to(replicated[None], (N_DEV, 16, 1024))], axis=1)

    as_rows = lambda t, n: t.reshape(n, 1024)
    small = [(MISC_POOL_W, 1024, tuple(as_rows(t, 32) for t in (pool_w, m_pool_w, v_pool_w))),
             (MISC_META, 128, (meta_tokens, m_meta_tokens, v_meta_tokens)),
             (MISC_B_IN, 1024, tuple(as_rows(t, 8) for t in (b_in, m_b_in, v_b_in))),
             (MISC_NORM_W, 1024, (norm_w, m_norm_w, v_norm_w)),
             (MISC_LB, 1024, (lb_logits, m_lb_logits, v_lb_logits)),
             (MISC_HG_NORM_W, 1024, (hg_norm_w, m_hg_norm_w, v_hg_norm_w)),
             (MISC_POOL_SCALE, 1024, (pool_scale, m_pool_scale, v_pool_scale)),
             (MISC_FINAL_NORM_W, 1024, tuple(as_rows(t, 1) for t in (final_norm_w, m_final_norm_w, v_final_norm_w)))]
    res = _finish(recv_hi, recv_lo, n_a, recv_rows, g_misc, (w_in, m_w_in, v_w_in),
                  [(w_out, m_w_out, v_w_out), (w_down_hg, m_w_down_hg, v_w_down_hg), (w_down_pool, m_w_down_pool, v_w_down_pool)],
                  small)
    r_w_in, r_w_out, r_wdh, r_wdp, r_pw, r_meta, r_b_in, r_nw, r_lb, r_hgw, r_ps, r_fnw = res
    loss = lax.psum(loss_part[0, 0], ("x", "y", "c"))
    grad_x = dz[CHUNK:].reshape(1, seq, D_MODEL)
    per_kind = [(r_meta[k], r_nw[k], r_w_in[k], r_b_in[k].reshape(1, 8192), r_lb[k], r_hgw[k], r_pw[k].reshape(1, 4, 32, 256),
                 r_ps[k], r_wdh[k], r_wdp[k], r_w_out[k], r_fnw[k].reshape(1024)) for k in range(4)]
    return (loss, grad_x, *per_kind[0], *per_kind[1], *per_kind[2], *per_kind[3])
```

```python
import functools

import jax
import jax.numpy as jnp
from jax import lax
from jax.experimental import pallas as pl
from jax.experimental.pallas import tpu as pltpu

F32 = jnp.float32
BF16 = jnp.bfloat16

D_MODEL = 1024
N_META = 16
HEADS = 8
HEAD_DIM = 128
CHUNK = 64
SUB = 8
N_SUB = CHUNK // SUB
PAD_ROWS = CHUNK - N_META
POOL_WINDOWS = (2, 4, 8, 16)
POOL_GDIM = D_MODEL // len(POOL_WINDOWS)
HALO = 16
EPS = 1e-6
N_DEV = 8
N_COLBLK = 8
ADAM_LR, ADAM_B1, ADAM_B2, ADAM_EPS, ADAM_WD, ADAM_STEP = 0.001, 0.9, 0.999, 1e-08, 0.01, 10

VMEM_LIMIT = 56 * 1024 * 1024
MESH = pl.DeviceIdType.MESH
ANY = pl.BlockSpec(memory_space=pl.ANY)
HIGHEST = lax.Precision.HIGHEST
LOG2_E = 1.4426950408889634

REST_W_DOWN_HG = 0
REST_W_DOWN_POOL = 128
REST_W_OUT = 256
REST_POOL_W = 384
MISC_POOL_W = 0
MISC_META = 32
MISC_B_IN = 48
MISC_NORM_W = 56
MISC_LB = 57
MISC_HG_NORM_W = 59
MISC_POOL_SCALE = 60
MISC_FINAL_NORM_W = 61
MISC_ROWS = 64


def _params(sem=None):
    return pltpu.CompilerParams(dimension_semantics=sem, vmem_limit_bytes=VMEM_LIMIT)


def _row_tile(n_rows, prefer):
    best = 16
    for t in range(16, prefer + 1, 16):
        if n_rows % t == 0:
            best = t
    return best


def _sigmoid_pair(x):
    e = jnp.exp(-jnp.abs(x))
    r = 1.0 / (1.0 + e)
    er = e * r
    pos = x >= 0
    return jnp.where(pos, r, er), jnp.where(pos, er, r)


def _dot(a, b):
    return jnp.dot(a.astype(BF16), b.astype(BF16), preferred_element_type=F32)


def _dot_nt(a, b):
    return lax.dot_general(a.astype(BF16), b.astype(BF16), (((1,), (1,)), ((), ())), preferred_element_type=F32)


def _dot_tn(a, b):
    return lax.dot_general(a.astype(BF16), b.astype(BF16), (((0,), (0,)), ((), ())), preferred_element_type=F32)


def _device_index(px, py, pc):
    return 4 * px + 2 * py + pc


def _direct_gather(src_ref, dst_ref, send_sems, recv_sems, local_sem):
    x, y, c = lax.axis_index("x"), lax.axis_index("y"), lax.axis_index("c")
    own = pltpu.make_async_copy(src_ref, dst_ref.at[_device_index(x, y, c)], local_sem)
    sends, arrivals = [], []
    for k in range(1, N_DEV):
        peer = (1 - x if k & 4 else x, 1 - y if k & 2 else y, 1 - c if k & 1 else c)
        for slot, out in ((_device_index(x, y, c), sends), (_device_index(*peer), arrivals)):
            out.append(pltpu.make_async_remote_copy(
                src_ref=src_ref, dst_ref=dst_ref.at[slot], send_sem=send_sems.at[k - 1], recv_sem=recv_sems.at[k - 1],
                device_id=peer, device_id_type=MESH))
    return own, sends, arrivals


GATHER_SEMS = [pltpu.SemaphoreType.DMA((N_DEV - 1,)), pltpu.SemaphoreType.DMA((N_DEV - 1,)), pltpu.SemaphoreType.DMA(())]


def _all_gather_small(block):
    def body(x_ref, out_ref, send_sems, recv_sems, local_sem):
        own, sends, arrivals = _direct_gather(x_ref, out_ref, send_sems, recv_sems, local_sem)
        own.start()
        for cp in sends:
            cp.start()
        for cp in arrivals:
            cp.wait_recv()
        for cp in sends:
            cp.wait_send()
        own.wait()

    return pl.pallas_call(
        body, name="all_gather_meta",
        out_shape=jax.ShapeDtypeStruct((N_DEV,) + block.shape, block.dtype),
        in_specs=[ANY], out_specs=ANY, scratch_shapes=list(GATHER_SEMS),
    )(block)


def _peer(k):
    x, y, c = lax.axis_index("x"), lax.axis_index("y"), lax.axis_index("c")
    return (1 - x if k & 4 else x, 1 - y if k & 2 else y, 1 - c if k & 1 else c)


def _me():
    return _device_index(lax.axis_index("x"), lax.axis_index("y"), lax.axis_index("c"))


def _remote(src, dst, send_sem, recv_sem, peer_bits):
    return pltpu.make_async_remote_copy(src_ref=src, dst_ref=dst, send_sem=send_sem, recv_sem=recv_sem,
                                        device_id=_peer(peer_bits), device_id_type=MESH)


N_ROW_GRADS = 3
SCATTER_SEMS = [pltpu.SemaphoreType.DMA((N_DEV - 1,)), pltpu.SemaphoreType.DMA((N_DEV - 1,)), pltpu.SemaphoreType.DMA(())]
SCATTER_ROWS_SEMS = [pltpu.SemaphoreType.DMA((7 * N_ROW_GRADS,)), pltpu.SemaphoreType.DMA((7 * N_ROW_GRADS,)),
                     pltpu.SemaphoreType.DMA((N_ROW_GRADS,))]


def _scatter_slabs(g_ref, first, recv_ref, send_sems, recv_sems, local_sem):
    n = g_ref.shape[0]
    me = _me()

    def each(on_send, on_local, on_arrival):
        for kk in range(1, N_DEV):
            peer = jnp.bitwise_xor(me, kk)

            @pl.when((peer >= first) & (peer < first + n))
            def _(kk=kk, peer=peer):
                on_send(_remote(g_ref.at[peer - first], recv_ref.at[me], send_sems.at[kk - 1], recv_sems.at[kk - 1], kk))

        @pl.when((me >= first) & (me < first + n))
        def _():
            on_local(pltpu.make_async_copy(g_ref.at[me - first], recv_ref.at[me], local_sem))
            if on_arrival is not None:
                for kk in range(1, N_DEV):
                    on_arrival(_remote(g_ref.at[0], recv_ref.at[jnp.bitwise_xor(me, kk)], send_sems.at[kk - 1],
                                       recv_sems.at[kk - 1], kk))

    start = lambda: each(lambda cp: cp.start(), lambda cp: cp.start(), None)
    finish = lambda: each(lambda cp: cp.wait_send(), lambda cp: cp.wait(), lambda cp: cp.wait_recv())
    return start, finish


def _scatter_rows(g_ref, recv_ref, send_sems, recv_sems, local_sems):
    me = _me()
    rows = lambda m, dev: g_ref.at[m, pl.ds(dev * 128, 128), :]

    def copies():
        local = [pltpu.make_async_copy(rows(m, me), recv_ref.at[me, m], local_sems.at[m]) for m in range(N_ROW_GRADS)]
        sends, arrivals = [], []
        for m in range(N_ROW_GRADS):
            for kk in range(1, N_DEV):
                peer, sems = jnp.bitwise_xor(me, kk), (send_sems.at[7 * m + kk - 1], recv_sems.at[7 * m + kk - 1])
                sends.append(_remote(rows(m, peer), recv_ref.at[me, m], *sems, kk))
                arrivals.append(_remote(rows(m, me), recv_ref.at[peer, m], *sems, kk))
        return local, sends, arrivals

    def start():
        local, sends, _ = copies()
        for cp in local + sends:
            cp.start()

    def finish():
        local, sends, arrivals = copies()
        for cp in arrivals:
            cp.wait_recv()
        for cp in sends:
            cp.wait_send()
        for cp in local:
            cp.wait()

    return start, finish


LOW_OWNERS = 3


def _pair_reduce_low(g_lo):
    def body(g_ref, out_ref, got, kept, send_sems, recv_sems, local_sems):
        c = lax.axis_index("c")

        def to_sibling(slab, slot):
            return _remote(g_ref.at[slab], got.at[slot], send_sems.at[slot], recv_sems.at[slot], 1)

        def keep(slab, slot):
            return pltpu.make_async_copy(g_ref.at[slab], kept.at[slot], local_sems.at[slot])

        def add(slot):
            out_ref[slot] = (kept[slot].astype(F32) + got[slot].astype(F32)).astype(out_ref.dtype)

        @pl.when(c == 0)
        def _():
            copies = [to_sibling(1, 0), keep(0, 0), keep(2, 1)]
            for cp in copies:
                cp.start()
            to_sibling(0, 0).wait_recv()
            to_sibling(2, 1).wait_recv()
            copies[0].wait_send()
            copies[1].wait()
            copies[2].wait()
            add(0)
            add(1)

        @pl.when(c == 1)
        def _():
            copies = [to_sibling(0, 0), to_sibling(2, 1), keep(1, 0)]
            for cp in copies:
                cp.start()
            to_sibling(1, 0).wait_recv()
            copies[0].wait_send()
            copies[1].wait_send()
            copies[2].wait()
            add(0)
            out_ref[1] = jnp.zeros(out_ref.shape[1:], out_ref.dtype)

    pair = (2,) + g_lo.shape[1:]
    return pl.pallas_call(
        body, name="pair_reduce_low",
        out_shape=jax.ShapeDtypeStruct(pair, g_lo.dtype),
        in_specs=[ANY], out_specs=pl.BlockSpec(memory_space=pltpu.VMEM),
        scratch_shapes=[pltpu.VMEM(pair, g_lo.dtype), pltpu.VMEM(pair, g_lo.dtype), pltpu.SemaphoreType.DMA((2,)),
                        pltpu.SemaphoreType.DMA((2,)), pltpu.SemaphoreType.DMA((2,))],
        compiler_params=_params(),
    )(g_lo)


def _scatter_low(part_ref, recv_ref, send_sems, recv_sems, local_sem):
    x, y, c = lax.axis_index("x"), lax.axis_index("y"), lax.axis_index("c")
    chip = 2 * x + y
    routes = ((0, (0, 0, c), 0, None), (1, (0, 1, 0), 1, 0))

    def each(on_send, on_local, on_arrival):
        for slot, owner, owner_chip, core in routes:
            holds = (c == core) if core is not None else (c >= 0)
            rel = jnp.bitwise_xor(chip, owner_chip)

            @pl.when(holds & (rel != 0))
            def _(slot=slot, owner=owner, rel=rel):
                on_send(pltpu.make_async_remote_copy(
                    src_ref=part_ref.at[slot], dst_ref=recv_ref.at[chip], send_sem=send_sems.at[slot],
                    recv_sem=recv_sems.at[rel - 1], device_id=owner, device_id_type=MESH))

            @pl.when(holds & (rel == 0))
            def _(slot=slot, owner=owner, owner_chip=owner_chip):
                on_local(pltpu.make_async_copy(part_ref.at[slot], recv_ref.at[chip], local_sem))
                if on_arrival is not None:
                    for r in range(1, 4):
                        on_arrival(pltpu.make_async_remote_copy(
                            src_ref=part_ref.at[slot], dst_ref=recv_ref.at[r ^ owner_chip], send_sem=send_sems.at[slot],
                            recv_sem=recv_sems.at[r - 1], device_id=owner, device_id_type=MESH))

    start = lambda: each(lambda cp: cp.start(), lambda cp: cp.start(), None)
    finish = lambda: each(lambda cp: cp.wait_send(), lambda cp: cp.wait(), lambda cp: cp.wait_recv())
    return start, finish


def _adam_update(g, w, m, v):
    mn = ADAM_B1 * m + (1.0 - ADAM_B1) * g
    vn = ADAM_B2 * v + (1.0 - ADAM_B2) * (g * g)
    m_hat = mn / (1.0 - ADAM_B1 ** ADAM_STEP)
    v_hat = vn / (1.0 - ADAM_B2 ** ADAM_STEP)
    return -ADAM_LR * (m_hat / (jnp.sqrt(v_hat) + ADAM_EPS) + ADAM_WD * w), mn, vn


def _device_sum(parts):
    t = [p.astype(F32) for p in parts]
    return ((t[0] + t[1]) + (t[2] + t[3])) + ((t[4] + t[5]) + (t[6] + t[7]))


def _finish(a_hi, a_lo, n_lo, b3, misc, big, rows3, small):
    n_steps = 4
    tb, tr3 = 1024 // n_steps, 128 // n_steps

    def body(*refs):
        it = iter(refs)
        hi_ref, lo_ref, b_ref, misc_hbm = next(it), next(it), next(it), next(it)
        big_in = [next(it) for _ in range(3)]
        rows_in = [[next(it) for _ in range(3)] for _ in rows3]
        small_in = [[next(it) for _ in range(3)] for _ in small]
        big_out = [next(it) for _ in range(4)]
        rows_out = [[next(it) for _ in range(4)] for _ in rows3]
        small_out = [[next(it) for _ in range(4)] for _ in small]
        m_vmem, send_sems, recv_sems, local_sem = next(it), next(it), next(it), next(it)
        start, finish = _scatter_slabs(misc_hbm, 0, m_vmem, send_sems, recv_sems, local_sem)
        step = pl.program_id(0)

        @pl.when(step == 0)
        def _():
            start()

        def apply(g, ins, outs):
            d, mn, vn = _adam_update(g, ins[0][...], ins[1][...], ins[2][...])
            for r, val in zip(outs, (g, d, mn, vn)):
                r[...] = val

        lo = [lo_ref[s].astype(F32) for s in range(4)]
        g_big = jnp.where(_me() < n_lo, (lo[0] + lo[1]) + (lo[2] + lo[3]), _device_sum([hi_ref[s] for s in range(N_DEV)]))
        apply(g_big[None], big_in, big_out)
        for k in range(len(rows3)):
            apply(_device_sum([b_ref[s, k] for s in range(N_DEV)])[None], rows_in[k], rows_out[k])

        @pl.when(step == n_steps - 1)
        def _():
            finish()
            for (row0, lanes, ins), r_in, r_out in zip(small, small_in, small_out):
                n = ins[0].shape[0]
                apply(_device_sum([m_vmem[s, row0:row0 + n, :lanes] for s in range(N_DEV)]), r_in, r_out)

    whole = lambda shape: pl.BlockSpec(shape, lambda i: (0,) * len(shape))
    big_blk = pl.BlockSpec((1, tb, 1024), lambda i: (0, i, 0))
    rows_blk = pl.BlockSpec((1, tr3, 1024), lambda i: (0, i, 0))
    in_specs = [pl.BlockSpec((N_DEV, tb, 1024), lambda i: (0, i, 0)), pl.BlockSpec((4, tb, 1024), lambda i: (0, i, 0)),
                pl.BlockSpec((N_DEV, 3, tr3, 1024), lambda i: (0, 0, i, 0)), ANY]
    in_specs += [big_blk] * 3 + [rows_blk] * (3 * len(rows3))
    out_specs = [big_blk] * 4 + [rows_blk] * (4 * len(rows3))
    out_shape = [jax.ShapeDtypeStruct(big[0].shape, F32)] * 4
    for w, _, _ in rows3:
        out_shape += [jax.ShapeDtypeStruct(w.shape, F32)] * 4
    args = [a_hi, a_lo, b3, misc, *big]
    for t in rows3:
        args += list(t)
    for _, _, t in small:
        in_specs += [whole(t[0].shape)] * 3
        out_specs += [whole(t[0].shape)] * 4
        out_shape += [jax.ShapeDtypeStruct(t[0].shape, F32)] * 4
        args += list(t)
    outs = pl.pallas_call(
        body, name="reduce_sum_adamw", out_shape=tuple(out_shape), grid=(n_steps,),
        in_specs=in_specs, out_specs=tuple(out_specs),
        scratch_shapes=[pltpu.VMEM(misc.shape, misc.dtype)] + list(SCATTER_SEMS),
        compiler_params=_params(("arbitrary",)),
    )(*args)
    return [tuple(outs[4 * k:4 * k + 4]) for k in range(len(outs) // 4)]


def _first_norm(zp, norm_w):
    n_rows = zp.shape[0]
    tr = _row_tile(n_rows, 832)

    def body(z_ref, nw_ref, h_ref):
        z = z_ref[...]
        r = lax.rsqrt(jnp.mean(z * z, axis=-1, keepdims=True) + EPS)
        h_ref[...] = (z * r * nw_ref[...]).astype(BF16)

    return pl.pallas_call(
        body, name="first_norm",
        out_shape=jax.ShapeDtypeStruct((n_rows, D_MODEL), BF16),
        grid=(n_rows // tr,),
        in_specs=[pl.BlockSpec((tr, D_MODEL), lambda i: (i, 0)), pl.BlockSpec((1, D_MODEL), lambda i: (0, 0))],
        out_specs=pl.BlockSpec((tr, D_MODEL), lambda i: (i, 0)),
        compiler_params=_params(("arbitrary",)),
    )(zp, norm_w)


def _gather_order():
    x, y, c = lax.axis_index("x"), lax.axis_index("y"), lax.axis_index("c")
    chips = [(1 - x, y), (x, 1 - y), (1 - x, 1 - y)]
    order = [_device_index(x, y, c), _device_index(x, y, 1 - c)]
    order += [_device_index(*q, c) for q in chips] + [_device_index(*q, 1 - c) for q in chips]
    return jnp.stack(order).astype(jnp.int32)


def _in_projection(h, w_shard, b_blocks, order):
    n_rows = h.shape[0]
    tr = _row_tile(n_rows, 832)
    nt = n_rows // tr

    def body(order_ref, h_ref, w_hbm, b_ref, p_ref, w_out, w_vmem, send_sems, recv_sems, local_sem, out_sems):
        s, i = pl.program_id(0), pl.program_id(1)
        x, y, c = lax.axis_index("x"), lax.axis_index("y"), lax.axis_index("c")
        me, sibling = (x, y, c), (x, y, 1 - c)
        chips = [(1 - x, y), (x, 1 - y), (1 - x, 1 - y)]

        def slot(px, py, pc):
            return w_vmem.at[_device_index(px, py, pc)]

        def copy(k, blk, to, src=None):
            return pltpu.make_async_remote_copy(
                src_ref=slot(*blk) if src is None else src, dst_ref=slot(*blk),
                send_sem=send_sems.at[k], recv_sem=recv_sems.at[k], device_id=to, device_id_type=MESH)

        own = pltpu.make_async_copy(w_hbm, slot(*me), local_sem)
        first = [copy(0, me, sibling, src=w_hbm)] + [copy(1 + j, me, (*q, c), src=w_hbm) for j, q in enumerate(chips)]
        passed = [copy(4 + j, (*q, c), sibling) for j, q in enumerate(chips)]
        arrivals = [None, copy(0, sibling, me)]
        arrivals += [copy(1 + j, (*q, c), me) for j, q in enumerate(chips)]
        arrivals += [copy(4 + j, (*q, 1 - c), me) for j, q in enumerate(chips)]

        def keep(step):
            return pltpu.make_async_copy(w_vmem.at[order_ref[step]], w_out.at[order_ref[step]], out_sems.at[step])

        for step in range(N_DEV):
            @pl.when((i == 0) & (s == step))
            def _(step=step):
                if step == 0:
                    own.start()
                    for cp in first:
                        cp.start()
                    own.wait()
                else:
                    arrivals[step].wait_recv()
                    if 2 <= step <= 4:
                        passed[step - 2].start()
                keep(step).start()

        p_ref[0] = jnp.dot(h_ref[...], w_vmem[order_ref[s]], preferred_element_type=F32) + b_ref[0]

        @pl.when((s == N_DEV - 1) & (i == nt - 1))
        def _():
            for cp in first + passed:
                cp.wait_send()
            for step in range(N_DEV):
                keep(step).wait()

    return pl.pallas_call(
        body, name="in_projection_gather",
        out_shape=(jax.ShapeDtypeStruct((N_COLBLK, n_rows, 1024), F32),
                   jax.ShapeDtypeStruct((N_DEV, D_MODEL, 1024), BF16)),
        grid_spec=pltpu.PrefetchScalarGridSpec(
            num_scalar_prefetch=1, grid=(N_DEV, nt),
            in_specs=[pl.BlockSpec((tr, D_MODEL), lambda s, i, o: (i, 0)), ANY,
                      pl.BlockSpec((1, 1, 1024), lambda s, i, o: (o[s], 0, 0))],
            out_specs=(pl.BlockSpec((1, tr, 1024), lambda s, i, o: (o[s], i, 0)), ANY),
            scratch_shapes=[pltpu.VMEM((N_DEV, D_MODEL, 1024), BF16), pltpu.SemaphoreType.DMA((7,)),
                            pltpu.SemaphoreType.DMA((7,)), pltpu.SemaphoreType.DMA(()), pltpu.SemaphoreType.DMA((N_DEV,))]),
        compiler_params=_params(("arbitrary", "arbitrary")),
    )(order, h, w_shard, b_blocks)


def _lower_bound(lb_ref):
    l0, l1 = lb_ref[0:1, :], lb_ref[1:2, :]
    _, lb = _sigmoid_pair(l1 - l0)
    return lb


def _chunk_gates(fz, lb, valid):
    sig, nsig = _sigmoid_pair(fz)
    f = lb + (1.0 - lb) * sig
    g = jnp.where(valid, jnp.log(f), 0.0)
    k = jnp.where(valid, (1.0 - lb) * nsig, 0.0)
    return sig, nsig, f, g, k


def _tri(n, upper=False):
    r = lax.broadcasted_iota(jnp.int32, (n, n), 0)
    c = lax.broadcasted_iota(jnp.int32, (n, n), 1)
    return jnp.where((r <= c) if upper else (r >= c), 1.0, 0.0).astype(F32)


def _intra_scores(q_ref, k_ref, b2_ref, col0):
    cols = pl.ds(col0, HEAD_DIM)
    rows_s = lax.broadcasted_iota(jnp.int32, (SUB, 1), 0)
    lanes_c = lax.broadcasted_iota(jnp.int32, (1, CHUNK), 1)
    blocks = []
    for i in range(N_SUB):
        lo = i * SUB
        qi = q_ref[lo:lo + SUB, cols]
        bi = b2_ref[lo:lo + SUB, cols]
        if i == 0:
            acc = jnp.zeros((SUB, CHUNK), F32)
        else:
            ref_i = b2_ref[lo:lo + 1, cols]
            qt = qi * jnp.exp2(bi - ref_i)
            kt = jnp.concatenate([k_ref[0:lo, cols] * jnp.exp2(ref_i - b2_ref[0:lo, cols]),
                                  jnp.zeros((CHUNK - lo, HEAD_DIM), F32)], axis=0)
            acc = _dot_nt(qt, kt)
        for s in range(SUB):
            b_s = b2_ref[lo + s:lo + s + 1, cols]
            k_s = k_ref[lo + s:lo + s + 1, cols]
            w = jnp.exp2(jnp.minimum(bi - b_s, 0.0))
            col = jnp.sum((qi * w) * k_s, axis=-1, keepdims=True)
            acc = jnp.where(lanes_c == lo + s, col, acc)
        blocks.append(jnp.where(lanes_c <= lo + rows_s, acc, 0.0))
    return jnp.concatenate(blocks, axis=0)


def _hgrn_forward(p, lb_logits, w_rest):
    n_rows = p.shape[1]
    n_chunks = n_rows // CHUNK
    width = HEADS * HEAD_DIM

    def body(q_ref, fz_ref, v_ref, lb_ref, rest_ref, o_ref, st_out_ref, a_out_ref, rest_out,
             state, k_vmem, b2_vmem, send_sems, recv_sems, local_sem):
        n = pl.program_id(0)
        own, sends, arrivals = _direct_gather(rest_ref, rest_out, send_sems, recv_sems, local_sem)

        @pl.when(n == 0)
        def _():
            state[...] = jnp.zeros_like(state)
            own.start()
            for cp in sends:
                cp.start()

        rows = n * CHUNK + lax.broadcasted_iota(jnp.int32, (CHUNK, 1), 0)
        valid = rows >= PAD_ROWS
        lb = _lower_bound(lb_ref)
        _, _, _, g, k = _chunk_gates(fz_ref[0], lb, valid)
        k_vmem[...] = k
        b2_vmem[...] = jnp.dot(_tri(CHUNK), g, precision=HIGHEST, preferred_element_type=F32) * LOG2_E
        q_view = q_ref.at[0]
        for h in range(HEADS):
            cols = pl.ds(h * HEAD_DIM, HEAD_DIM)
            st = state[h]
            st_out_ref[0, h] = st
            bh = b2_vmem[:, cols]
            kh = k_vmem[:, cols]
            vh = jnp.where(valid, v_ref[0, :, cols], 0.0)
            qe = q_ref[0, :, cols] * jnp.exp2(bh)
            a = _intra_scores(q_view, k_vmem, b2_vmem, h * HEAD_DIM).astype(BF16)
            a_out_ref[0, h] = a
            o_ref[:, cols] = _dot_nt(qe, st) + _dot(a, vh)
            b_last = b2_vmem[CHUNK - 1:CHUNK, cols]
            kd = kh * jnp.exp2(b_last - bh)
            state[h] = st * jnp.exp2(b_last) + _dot_tn(vh, kd)

        @pl.when(n == n_chunks - 1)
        def _():
            for cp in arrivals:
                cp.wait_recv()
            for cp in sends:
                cp.wait_send()
            own.wait()

    blk = lambda c: pl.BlockSpec((1, CHUNK, width), lambda n, c=c: (c, n, 0))
    return pl.pallas_call(
        body, name="hgrn_forward",
        out_shape=(jax.ShapeDtypeStruct((n_rows, width), F32),
                   jax.ShapeDtypeStruct((n_chunks, HEADS, HEAD_DIM, HEAD_DIM), F32),
                   jax.ShapeDtypeStruct((n_chunks, HEADS, CHUNK, CHUNK), BF16),
                   jax.ShapeDtypeStruct((N_DEV,) + w_rest.shape, w_rest.dtype)),
        grid=(n_chunks,),
        in_specs=[blk(0), blk(1), blk(2), pl.BlockSpec((2, width), lambda n: (0, 0)), ANY],
        out_specs=(pl.BlockSpec((CHUNK, width), lambda n: (n, 0)),
                   pl.BlockSpec((1, HEADS, HEAD_DIM, HEAD_DIM), lambda n: (n, 0, 0, 0)),
                   pl.BlockSpec((1, HEADS, CHUNK, CHUNK), lambda n: (n, 0, 0, 0)), ANY),
        scratch_shapes=[pltpu.VMEM((HEADS, HEAD_DIM, HEAD_DIM), F32), pltpu.VMEM((CHUNK, width), F32),
                        pltpu.VMEM((CHUNK, width), F32)] + list(GATHER_SEMS),
        compiler_params=_params(("arbitrary",)),
    )(p, p, p, lb_logits, w_rest)


def _hgrn_backward(p, lb_logits, states, scores, d_o, g_slabs, first_owner, g_rows):
    n_rows = p.shape[1]
    n_chunks = n_rows // CHUNK
    width = HEADS * HEAD_DIM

    def body(q_ref, fz_ref, v_ref, lb_ref, st_ref, a_ref, do_ref, gs_hbm, gr_hbm, dp_ref, dbias_ref, dlb_ref, rs_hbm, rr_hbm,
             dstate, k_vmem, b2_vmem, *sems):
        step = pl.program_id(0)
        n = n_chunks - 1 - step
        start_slabs, finish_slabs = _scatter_slabs(gs_hbm, first_owner, rs_hbm, *sems[:3])
        start_rows, finish_rows = _scatter_rows(gr_hbm, rr_hbm, *sems[3:])

        @pl.when(step == 0)
        def _():
            dstate[...] = jnp.zeros_like(dstate)
            dbias_ref[...] = jnp.zeros_like(dbias_ref)
            dlb_ref[...] = jnp.zeros_like(dlb_ref)
            start_slabs()
            start_rows()

        @pl.when(step == n_chunks - 1)
        def _():
            finish_slabs()
            finish_rows()

        rows = n * CHUNK + lax.broadcasted_iota(jnp.int32, (CHUNK, 1), 0)
        valid = rows >= PAD_ROWS
        lb = _lower_bound(lb_ref)
        sig, nsig, f, g, k = _chunk_gates(fz_ref[0], lb, valid)
        k_vmem[...] = k
        b2_vmem[...] = jnp.dot(_tri(CHUNK), g, precision=HIGHEST, preferred_element_type=F32) * LOG2_E
        rows_c = lax.broadcasted_iota(jnp.int32, (CHUNK, 1), 0)
        rows_s = lax.broadcasted_iota(jnp.int32, (SUB, 1), 0)
        lanes_c = lax.broadcasted_iota(jnp.int32, (1, CHUNK), 1)
        causal = lax.broadcasted_iota(jnp.int32, (CHUNK, CHUNK), 0) >= lax.broadcasted_iota(jnp.int32, (CHUNK, CHUNK), 1)
        tri_up = _tri(CHUNK, upper=True)
        for h in range(HEADS):
            cols = pl.ds(h * HEAD_DIM, HEAD_DIM)
            st = st_ref[0, h]
            dst = dstate[h]
            qh = q_ref[0, :, cols]
            bh = b2_vmem[:, cols]
            kh = k_vmem[:, cols]
            vh = jnp.where(valid, v_ref[0, :, cols], 0.0)
            doh = do_ref[:, cols]
            eb = jnp.exp2(bh)
            qe = qh * eb
            b_last = b2_vmem[CHUNK - 1:CHUNK, cols]
            e_last = jnp.exp2(b_last)
            decay_k = jnp.exp2(b_last - bh)
            kd = kh * decay_k
            dqe = _dot(doh, st)
            da = jnp.where(causal, _dot_nt(doh, vh), 0.0)
            dv = _dot_tn(a_ref[0, h], doh) + _dot_nt(kd, dst)
            dkd = _dot(vh, dst)
            dstate[h] = dst * e_last + _dot_tn(doh, qe)
            db_last = (jnp.sum(dst * st, axis=0, keepdims=True) * e_last
                       + jnp.sum(dkd * kd, axis=0, keepdims=True))
            dq_blocks, dk_blocks = [], []
            dk_earlier = jnp.zeros((CHUNK, HEAD_DIM), F32)
            for i in range(N_SUB):
                lo = i * SUB
                qi = q_ref[0, lo:lo + SUB, cols]
                bi = b2_vmem[lo:lo + SUB, cols]
                da_i = da[lo:lo + SUB, :]
                if i == 0:
                    dq_i = jnp.zeros((SUB, HEAD_DIM), F32)
                else:
                    ref_i = b2_vmem[lo:lo + 1, cols]
                    eq = jnp.exp2(bi - ref_i)
                    ek = jnp.exp2(ref_i - b2_vmem[0:lo, cols])
                    later = jnp.zeros((CHUNK - lo, HEAD_DIM), F32)
                    kt = jnp.concatenate([k_vmem[0:lo, cols] * ek, later], axis=0)
                    dq_i = _dot(da_i, kt) * eq
                    dk_earlier = dk_earlier + jnp.concatenate([_dot_tn(da_i, qi * eq)[0:lo] * ek, later], axis=0)
                dk_i = jnp.zeros((SUB, HEAD_DIM), F32)
                for s in range(SUB):
                    b_s = b2_vmem[lo + s:lo + s + 1, cols]
                    k_s = k_vmem[lo + s:lo + s + 1, cols]
                    w = jnp.exp2(jnp.minimum(bi - b_s, 0.0))
                    da_col = jnp.sum(jnp.where(lanes_c == lo + s, da_i, 0.0), axis=-1, keepdims=True)
                    gw = da_col * w
                    dq_i = dq_i + gw * k_s
                    dk_i = jnp.where(rows_s == s, jnp.sum(gw * qi, axis=0, keepdims=True), dk_i)
                dq_blocks.append(dq_i)
                dk_blocks.append(dk_i)
            dq_intra = jnp.concatenate(dq_blocks, axis=0)
            dk_intra = jnp.concatenate(dk_blocks, axis=0) + dk_earlier
            dq = dqe * eb + dq_intra
            dk = dkd * decay_k + dk_intra
            db = dqe * qe - dkd * kd + qh * dq_intra - kh * dk_intra
            db = db + jnp.where(rows_c == CHUNK - 1, db_last, 0.0)
            dg = jnp.dot(tri_up, db, precision=HIGHEST, preferred_element_type=F32)
            fh = f[:, h * HEAD_DIM:(h + 1) * HEAD_DIM]
            sh = sig[:, h * HEAD_DIM:(h + 1) * HEAD_DIM]
            nh = nsig[:, h * HEAD_DIM:(h + 1) * HEAD_DIM]
            lbh = lb[:, h * HEAD_DIM:(h + 1) * HEAD_DIM]
            df = jnp.where(valid, dg / fh - dk, 0.0)
            dfz = df * (1.0 - lbh) * sh * nh
            dq = jnp.where(valid, dq, 0.0)
            dv = jnp.where(valid, dv, 0.0)
            dlb_ref[:, cols] += jnp.sum(df * nh, axis=0, keepdims=True)
            dp_ref[0, :, cols] = dq.astype(BF16)
            dp_ref[1, :, cols] = dfz.astype(BF16)
            dp_ref[2, :, cols] = dv.astype(BF16)
            dbias_ref[0, :, cols] += jnp.sum(dq, axis=0, keepdims=True)
            dbias_ref[1, :, cols] += jnp.sum(dfz, axis=0, keepdims=True)
            dbias_ref[2, :, cols] += jnp.sum(dv, axis=0, keepdims=True)

    rev = lambda s: n_chunks - 1 - s
    blk = lambda c: pl.BlockSpec((1, CHUNK, width), lambda s, c=c: (c, rev(s), 0))
    return pl.pallas_call(
        body, name="hgrn_backward",
        out_shape=(jax.ShapeDtypeStruct((3, n_rows, width), BF16),
                   jax.ShapeDtypeStruct((3, 1, width), F32),
                   jax.ShapeDtypeStruct((1, width), F32),
                   jax.ShapeDtypeStruct((N_DEV,) + g_slabs.shape[1:], g_slabs.dtype),
                   jax.ShapeDtypeStruct((N_DEV, N_ROW_GRADS, 128, g_rows.shape[2]), g_rows.dtype)),
        grid=(n_chunks,),
        in_specs=[blk(0), blk(1), blk(2), pl.BlockSpec((2, width), lambda s: (0, 0)),
                  pl.BlockSpec((1, HEADS, HEAD_DIM, HEAD_DIM), lambda s: (rev(s), 0, 0, 0)),
                  pl.BlockSpec((1, HEADS, CHUNK, CHUNK), lambda s: (rev(s), 0, 0, 0)),
                  pl.BlockSpec((CHUNK, width), lambda s: (rev(s), 0)), ANY, ANY],
        out_specs=(pl.BlockSpec((3, CHUNK, width), lambda s: (0, rev(s), 0)),
                   pl.BlockSpec((3, 1, width), lambda s: (0, 0, 0)),
                   pl.BlockSpec((1, width), lambda s: (0, 0)), ANY, ANY),
        scratch_shapes=[pltpu.VMEM((HEADS, HEAD_DIM, HEAD_DIM), F32), pltpu.VMEM((CHUNK, width), F32),
                        pltpu.VMEM((CHUNK, width), F32)] + list(SCATTER_SEMS) + list(SCATTER_ROWS_SEMS),
        compiler_params=_params(("arbitrary",)),
    )(p, p, p, lb_logits, states, scores, d_o, g_slabs, g_rows)


def _silu_and_grad(x):
    s, ns = _sigmoid_pair(x)
    return x * s, s * (1.0 + x * ns)


def _tail(p, o, zp, tgt, hg_norm_w, pool_w, pool_scale, w_down_hg, w_down_pool, w_out, final_norm_w):
    n_rows = zp.shape[0]
    tr = _row_tile(n_rows, 208)
    nt = n_rows // tr
    ext = tr + HALO
    n_groups = len(POOL_WINDOWS)

    def body(o_ref, ghg_ref, u_ref, gpool_ref, mhg_ref, mpool_ref, uhalo_ref, z_ref, tgt_ref,
             hgw_ref, pw_ref, ps_ref, wdh_ref, wdp_ref, wout_ref, fnw_ref,
             do_ref, dp_ref, dz2_ref, lhs_ref, rhs_ref,
             dbias_ref, dhgw_ref, dpw_ref, dps_ref, dfnw_ref, loss_ref, halo_vmem):
        step = pl.program_id(0)
        ti = nt - 1 - step

        @pl.when(step == 0)
        def _():
            halo_vmem[...] = jnp.zeros_like(halo_vmem)
            for r in (dbias_ref, dhgw_ref, dpw_ref, dps_ref, dfnw_ref, loss_ref):
                r[...] = jnp.zeros_like(r)

        rows = ti * tr + lax.broadcasted_iota(jnp.int32, (tr, 1), 0)
        valid = rows >= PAD_ROWS
        in_loss = rows >= CHUNK
        count_pos = jnp.maximum(rows - PAD_ROWS + 1, 1).astype(F32)

        o = o_ref[...]
        hgw = hgw_ref[...]
        inv_o, on_parts = [], []
        for h in range(HEADS):
            oh = o[:, h * HEAD_DIM:(h + 1) * HEAD_DIM]
            r = lax.rsqrt(jnp.mean(oh * oh, axis=-1, keepdims=True) + EPS)
            inv_o.append(r)
            on_parts.append(oh * r)
        o_hat = jnp.concatenate(on_parts, axis=1)
        o_n = o_hat * hgw
        g_hg = ghg_ref[0]
        silu_hg, dsilu_hg = _silu_and_grad(g_hg)
        a_hg = o_n * silu_hg
        y_hg = _dot(a_hg, wdh_ref[...])

        u = jnp.where(valid, u_ref[0], 0.0)
        u_prev = jnp.where(ti > 0, uhalo_ref[0], 0.0)
        u_ext = jnp.concatenate([u_prev, u], axis=0)
        pooled_parts, mixed_parts, inv_cnt = [], [], []
        for gi, win in enumerate(POOL_WINDOWS):
            lanes = slice(gi * POOL_GDIM, (gi + 1) * POOL_GDIM)
            s = u_ext[:, lanes]
            shift = 1
            while shift < win:
                s = s + pltpu.roll(s, shift, 0)
                shift *= 2
            ic = 1.0 / jnp.minimum(count_pos, float(win))
            inv_cnt.append(ic)
            pooled = s[HALO:] * ic - u[:, lanes]
            pooled_parts.append(pooled)
            mixed_parts.append(_dot(pooled, pw_ref[gi]))
        mixed = jnp.concatenate(mixed_parts, axis=1)
        ps = ps_ref[...]
        g_pool = gpool_ref[0]
        silu_pool, dsilu_pool = _silu_and_grad(g_pool)
        a_pool = mixed * ps * silu_pool
        y_pool = _dot(a_pool, wdp_ref[...])

        m_hg, m_pool = mhg_ref[0], mpool_ref[0]
        s_hg, ns_hg = _sigmoid_pair(m_hg)
        s_pool, ns_pool = _sigmoid_pair(m_pool)
        merged = s_hg * y_hg + s_pool * y_pool
        z2 = z_ref[...] + _dot(merged, wout_ref[...])
        r2 = lax.rsqrt(jnp.mean(z2 * z2, axis=-1, keepdims=True) + EPS)
        n2 = z2 * r2
        fnw = fnw_ref[...]
        err = jnp.where(in_loss, n2 * fnw - tgt_ref[...], 0.0)
        loss_ref[...] += jnp.sum(jnp.sum(err * err, axis=0, keepdims=True), axis=1, keepdims=True) * (0.5 / D_MODEL)
        dy = err * (1.0 / D_MODEL)

        dfnw_ref[...] += jnp.sum(dy * n2, axis=0, keepdims=True)
        gy = dy * fnw
        dz2 = r2 * (gy - n2 * jnp.mean(gy * n2, axis=-1, keepdims=True))
        dmerged = _dot_nt(dz2, wout_ref[...])
        dy_hg = s_hg * dmerged
        dy_pool = s_pool * dmerged
        dm_hg = dmerged * y_hg * s_hg * ns_hg
        dm_pool = dmerged * y_pool * s_pool * ns_pool
        da_hg = _dot_nt(dy_hg, wdh_ref[...])
        da_pool = _dot_nt(dy_pool, wdp_ref[...])

        d_on = da_hg * silu_hg
        dg_hg = da_hg * o_n * dsilu_hg
        dhgw_ref[...] += jnp.sum(d_on * o_hat, axis=0, keepdims=True)
        gyo = d_on * hgw
        do_parts = []
        for h in range(HEADS):
            lanes = slice(h * HEAD_DIM, (h + 1) * HEAD_DIM)
            gh, nh = gyo[:, lanes], o_hat[:, lanes]
            do_parts.append(inv_o[h] * (gh - nh * jnp.mean(gh * nh, axis=-1, keepdims=True)))
        do_ref[...] = jnp.concatenate(do_parts, axis=1)

        dmixed = da_pool * ps * silu_pool
        dps_ref[...] += jnp.sum(da_pool * mixed * silu_pool, axis=0, keepdims=True)
        dg_pool = da_pool * mixed * ps * dsilu_pool
        du_parts = []
        for gi, win in enumerate(POOL_WINDOWS):
            lanes = slice(gi * POOL_GDIM, (gi + 1) * POOL_GDIM)
            dmx = dmixed[:, lanes]
            dpooled = _dot_nt(dmx, pw_ref[gi])
            dpw_ref[gi] += _dot_tn(pooled_parts[gi], dmx)
            dpt = dpooled * inv_cnt[gi]
            s = jnp.concatenate([dpt, halo_vmem[:, lanes]], axis=0)
            shift = 1
            while shift < win:
                s = s + pltpu.roll(s, ext - shift, 0)
                shift *= 2
            du_parts.append(s[:tr] - dpooled)
            halo_vmem[:, lanes] = dpt[:HALO]
        du = jnp.where(valid, jnp.concatenate(du_parts, axis=1), 0.0)

        for c, val in enumerate((dg_hg, du, dg_pool, dm_hg, dm_pool)):
            dp_ref[c] = val.astype(BF16)
            dbias_ref[c] += jnp.sum(val, axis=0, keepdims=True)
        dz2_ref[...] = dz2
        for c, (lhs, rhs) in enumerate(((merged, dz2), (a_hg, dy_hg), (a_pool, dy_pool))):
            lhs_ref[c] = lhs.astype(BF16)
            rhs_ref[c] = rhs.astype(BF16)

    rev = lambda s: nt - 1 - s
    rowblk = pl.BlockSpec((tr, D_MODEL), lambda s: (rev(s), 0))
    pblk = lambda c: pl.BlockSpec((1, tr, 1024), lambda s, c=c: (c, rev(s), 0))
    halo_blk = pl.BlockSpec((1, HALO, 1024), lambda s: (4, jnp.maximum(rev(s) * (tr // HALO) - 1, 0), 0))
    full = lambda shape: pl.BlockSpec(shape, lambda s: (0,) * len(shape))
    vec = full((1, D_MODEL))
    mat = full((D_MODEL, D_MODEL))
    act3 = jax.ShapeDtypeStruct((3, n_rows, D_MODEL), BF16)
    act3_blk = pl.BlockSpec((3, tr, D_MODEL), lambda s: (0, rev(s), 0))
    return pl.pallas_call(
        body, name="tail_forward_backward",
        out_shape=(jax.ShapeDtypeStruct((n_rows, D_MODEL), F32),
                   jax.ShapeDtypeStruct((5, n_rows, 1024), BF16),
                   jax.ShapeDtypeStruct((n_rows, D_MODEL), F32),
                   act3, act3,
                   jax.ShapeDtypeStruct((5, 1, 1024), F32),
                   jax.ShapeDtypeStruct((1, D_MODEL), F32),
                   jax.ShapeDtypeStruct((n_groups, POOL_GDIM, POOL_GDIM), F32),
                   jax.ShapeDtypeStruct((1, D_MODEL), F32),
                   jax.ShapeDtypeStruct((1, D_MODEL), F32),
                   jax.ShapeDtypeStruct((1, 1), F32)),
        grid=(nt,),
        in_specs=[rowblk, pblk(3), pblk(4), pblk(5), pblk(6), pblk(7), halo_blk, rowblk, rowblk,
                  vec, full((n_groups, POOL_GDIM, POOL_GDIM)), vec, mat, mat, mat, vec],
        out_specs=(rowblk, pl.BlockSpec((5, tr, 1024), lambda s: (0, rev(s), 0)), rowblk,
                   act3_blk, act3_blk,
                   full((5, 1, 1024)), vec, full((n_groups, POOL_GDIM, POOL_GDIM)), vec, vec, full((1, 1))),
        scratch_shapes=[pltpu.VMEM((HALO, D_MODEL), F32)],
        compiler_params=_params(("arbitrary",)),
    )(o, p, p, p, p, p, p, zp, tgt, hg_norm_w, pool_w, pool_scale, w_down_hg, w_down_pool, w_out, final_norm_w)


def _in_projection_backward(dp_a, dp_b, w_blocks, zp, dz2, norm_w, chip_sums):
    n_rows = zp.shape[0]
    tr = _row_tile(n_rows, 416)
    nt = n_rows // tr
    na, nb = dp_a.shape[0], dp_b.shape[0]

    def body(dpa_ref, dpb_ref, w_hbm, z_ref, dz2_ref, nw_ref, gs_hbm, dz_ref, dnw_ref, rs_hbm, w_vmem, sem, *sems):
        i = pl.program_id(0)
        start_slabs, finish_slabs = _scatter_low(gs_hbm, rs_hbm, *sems)

        @pl.when(i == 0)
        def _():
            start_slabs()
            cp = pltpu.make_async_copy(w_hbm, w_vmem, sem)
            cp.start()
            cp.wait()
            dnw_ref[...] = jnp.zeros_like(dnw_ref)

        @pl.when(i == nt - 1)
        def _():
            finish_slabs()

        dh = jnp.zeros((tr, D_MODEL), F32)
        for j in range(na):
            dh = dh + _dot_nt(dpa_ref[j], w_vmem[j])
        for j in range(nb):
            dh = dh + _dot_nt(dpb_ref[j], w_vmem[na + j])
        z = z_ref[...]
        r = lax.rsqrt(jnp.mean(z * z, axis=-1, keepdims=True) + EPS)
        n1 = z * r
        dnw_ref[...] += jnp.sum(dh * n1, axis=0, keepdims=True)
        gh = dh * nw_ref[...]
        dz_ref[...] = dz2_ref[...] + r * (gh - n1 * jnp.mean(gh * n1, axis=-1, keepdims=True))

    rowblk = pl.BlockSpec((tr, D_MODEL), lambda i: (i, 0))
    vec = pl.BlockSpec((1, D_MODEL), lambda i: (0, 0))
    return pl.pallas_call(
        body, name="in_projection_backward",
        out_shape=(jax.ShapeDtypeStruct((n_rows, D_MODEL), F32), jax.ShapeDtypeStruct((1, D_MODEL), F32),
                   jax.ShapeDtypeStruct((4,) + chip_sums.shape[1:], chip_sums.dtype)),
        grid=(nt,),
        in_specs=[pl.BlockSpec((na, tr, 1024), lambda i: (0, i, 0)), pl.BlockSpec((nb, tr, 1024), lambda i: (0, i, 0)),
                  ANY, rowblk, rowblk, vec, ANY],
        out_specs=(rowblk, vec, ANY),
        scratch_shapes=[pltpu.VMEM((N_COLBLK, D_MODEL, 1024), BF16), pltpu.SemaphoreType.DMA(()),
                        pltpu.SemaphoreType.DMA((2,)), pltpu.SemaphoreType.DMA((3,)), pltpu.SemaphoreType.DMA(())],
        compiler_params=_params(("arbitrary",)),
    )(dp_a, dp_b, w_blocks, zp, dz2, norm_w, chip_sums)


def _weight_grad(xs, ys, name):
    shared = xs.ndim == 2
    n_rows, m = xs.shape[-2:]
    nb, _, n = ys.shape
    tk = _row_tile(n_rows, 2080)
    n_k = n_rows // tk

    def body(x_ref, y_ref, o_ref, acc):
        k = pl.program_id(1)

        @pl.when(k == 0)
        def _():
            acc[...] = jnp.zeros_like(acc)

        acc[...] += _dot_tn(x_ref[...] if shared else x_ref[0], y_ref[0])

        @pl.when(k == n_k - 1)
        def _():
            o_ref[0] = acc[...].astype(o_ref.dtype)

    x_spec = pl.BlockSpec((tk, m), lambda j, k: (k, 0)) if shared else pl.BlockSpec((1, tk, m), lambda j, k: (j, k, 0))
    return pl.pallas_call(
        body, name=name,
        out_shape=jax.ShapeDtypeStruct((nb, m, n), BF16),
        grid=(nb, n_k),
        in_specs=[x_spec, pl.BlockSpec((1, tk, n), lambda j, k: (j, k, 0))],
        out_specs=pl.BlockSpec((1, m, n), lambda j, k: (j, 0, 0)),
        scratch_shapes=[pltpu.VMEM((m, n), F32)],
        compiler_params=_params(("arbitrary", "arbitrary")),
    )(xs, ys)


def kernel(x, meta_tokens, norm_w, w_in, b_in, lb_logits, hg_norm_w, pool_w, pool_scale, w_down_hg, w_down_pool, w_out, final_norm_w, loss_target, m_meta_tokens, m_norm_w, m_w_in, m_b_in, m_lb_logits, m_hg_norm_w, m_pool_w, m_pool_scale, m_w_down_hg, m_w_down_pool, m_w_out, m_final_norm_w, v_meta_tokens, v_norm_w, v_w_in, v_b_in, v_lb_logits, v_hg_norm_w, v_pool_w, v_pool_scale, v_w_down_hg, v_w_down_pool, v_w_out, v_final_norm_w):
    seq = x.shape[1]

    meta_full = _all_gather_small(meta_tokens).transpose(1, 0, 2).reshape(N_META, D_MODEL)
    w_rest = jnp.concatenate([w_down_hg[0].astype(BF16), w_down_pool[0].astype(BF16), w_out[0].astype(BF16),
                              pool_w[0].astype(BF16).reshape(32, 1024)], axis=0)

    zp = jnp.concatenate([jnp.zeros((PAD_ROWS, D_MODEL), F32), meta_full, x[0]], axis=0)
    tgt = jnp.concatenate([jnp.zeros((CHUNK, D_MODEL), F32), loss_target[0]], axis=0)
    h = _first_norm(zp, norm_w)
    p, w_blocks = _in_projection(h, w_in[0].astype(BF16), b_in.reshape(N_COLBLK, 1, 1024), _gather_order())
    o, states, scores, rest = _hgrn_forward(p, lb_logits, w_rest)
    wdh = rest[:, REST_W_DOWN_HG:REST_W_DOWN_HG + 128].reshape(1024, 1024)
    wdp = rest[:, REST_W_DOWN_POOL:REST_W_DOWN_POOL + 128].reshape(1024, 1024)
    wout = rest[:, REST_W_OUT:REST_W_OUT + 128].reshape(1024, 1024)
    pw = rest[:, REST_POOL_W:REST_POOL_W + 32].reshape(N_DEV, 4, 32, 256).transpose(1, 0, 2, 3).reshape(4, 256, 256)
    (d_o, dp_b, dz2, grad_lhs, grad_rhs, dbias_b, d_hgw, d_pw, d_ps, d_fnw, loss_part) = _tail(
        p, o, zp, tgt, hg_norm_w, pw, pool_scale, wdh, wdp, wout, final_norm_w.reshape(1, D_MODEL))
    n_a = N_COLBLK - dp_b.shape[0]
    g_hi = _weight_grad(h, dp_b, "weight_grad_in_hi")
    g_rows = _weight_grad(grad_lhs, grad_rhs, "weight_grad_rows")
    dp_a, dbias_a, d_lb, recv_hi, recv_rows = _hgrn_backward(p, lb_logits, states, scores, d_o, g_hi, n_a, g_rows)
    assert n_a == LOW_OWNERS
    g_lo = _weight_grad(h, dp_a, "weight_grad_in_lo")
    dz, d_nw, recv_lo = _in_projection_backward(dp_a, dp_b, w_blocks, zp, dz2, norm_w, _pair_reduce_low(g_lo))

    lb = jax.nn.sigmoid(lb_logits[0:1] - lb_logits[1:2])
    d_l0 = d_lb * lb * (1.0 - lb)
    replicated = jnp.concatenate([dbias_a.reshape(3, 1024), dbias_b.reshape(5, 1024), d_nw, d_l0, -d_l0, d_hgw, d_ps, d_fnw,
                                  jnp.zeros((MISC_ROWS - MISC_FINAL_NORM_W - 1, 1024), F32)], axis=0)
    d_meta = dz[PAD_ROWS:CHUNK].reshape(N_META, N_DEV, 128).transpose(1, 0, 2)
    d_pw_blocks = d_pw.reshape(4, N_DEV, 32, 256).transpose(1, 0, 2, 3).reshape(N_DEV, 32, 1024)
    g_misc = jnp.concatenate([d_pw_blocks, jnp.pad(d_meta, ((0, 0), (0, 0), (0, 1024 - 128))),
                              jnp.broadcast_to(replicated[None], (N_DEV, 16, 1024))], axis=1)

    as_rows = lambda t, n: t.reshape(n, 1024)
    small = [(MISC_POOL_W, 1024, tuple(as_rows(t, 32) for t in (pool_w, m_pool_w, v_pool_w))),
             (MISC_META, 128, (meta_tokens, m_meta_tokens, v_meta_tokens)),
             (MISC_B_IN, 1024, tuple(as_rows(t, 8) for t in (b_in, m_b_in, v_b_in))),
             (MISC_NORM_W, 1024, (norm_w, m_norm_w, v_norm_w)),
             (MISC_LB, 1024, (lb_logits, m_lb_logits, v_lb_logits)),
             (MISC_HG_NORM_W, 1024, (hg_norm_w, m_hg_norm_w, v_hg_norm_w)),
             (MISC_POOL_SCALE, 1024, (pool_scale, m_pool_scale, v_pool_scale)),
             (MISC_FINAL_NORM_W, 1024, tuple(as_rows(t, 1) for t in (final_norm_w, m_final_norm_w, v_final_norm_w)))]
    res = _finish(recv_hi, recv_lo, n_a, recv_rows, g_misc, (w_in, m_w_in, v_w_in),
                  [(w_out, m_w_out, v_w_out), (w_down_hg, m_w_down_hg, v_w_down_hg), (w_down_pool, m_w_down_pool, v_w_down_pool)],
                  small)
    r_w_in, r_w_out, r_wdh, r_wdp, r_pw, r_meta, r_b_in, r_nw, r_lb, r_hgw, r_ps, r_fnw = res
    loss = lax.psum(loss_part[0, 0], ("x", "y", "c"))
    grad_x = dz[CHUNK:].reshape(1, seq, D_MODEL)
    per_kind = [(r_meta[k], r_nw[k], r_w_in[k], r_b_in[k].reshape(1, 8192), r_lb[k], r_hgw[k], r_pw[k].reshape(1, 4, 32, 256),
                 r_ps[k], r_wdh[k], r_wdp[k], r_w_out[k], r_fnw[k].reshape(1024)) for k in range(4)]
    return (loss, grad_x, *per_kind[0], *per_kind[1], *per_kind[2], *per_kind[3])
```

```python
import functools

import jax
import jax.numpy as jnp
from jax import lax
from jax.experimental import pallas as pl
from jax.experimental.pallas import tpu as pltpu

F32 = jnp.float32
BF16 = jnp.bfloat16

D_MODEL = 1024
N_META = 16
HEADS = 8
HEAD_DIM = 128
CHUNK = 64
SUB = 8
N_SUB = CHUNK // SUB
PAD_ROWS = CHUNK - N_META
POOL_WINDOWS = (2, 4, 8, 16)
POOL_GDIM = D_MODEL // len(POOL_WINDOWS)
HALO = 16
EPS = 1e-6
N_DEV = 8
N_COLBLK = 8
ADAM_LR, ADAM_B1, ADAM_B2, ADAM_EPS, ADAM_WD, ADAM_STEP = 0.001, 0.9, 0.999, 1e-08, 0.01, 10

VMEM_LIMIT = 56 * 1024 * 1024
MESH = pl.DeviceIdType.MESH
ANY = pl.BlockSpec(memory_space=pl.ANY)

REST_W_DOWN_HG = 0
REST_W_DOWN_POOL = 128
REST_W_OUT = 256
REST_POOL_W = 384
MISC_POOL_W = 0
MISC_META = 32
MISC_B_IN = 48
MISC_NORM_W = 56
MISC_LB = 57
MISC_HG_NORM_W = 59
MISC_POOL_SCALE = 60
MISC_FINAL_NORM_W = 61
MISC_ROWS = 64


def _params(sem=None):
    return pltpu.CompilerParams(dimension_semantics=sem, vmem_limit_bytes=VMEM_LIMIT)


def _row_tile(n_rows, prefer):
    best = 16
    for t in range(16, prefer + 1, 16):
        if n_rows % t == 0:
            best = t
    return best


def _sigmoid_pair(x):
    e = jnp.exp(-jnp.abs(x))
    r = 1.0 / (1.0 + e)
    er = e * r
    pos = x >= 0
    return jnp.where(pos, r, er), jnp.where(pos, er, r)


def _dot(a, b):
    return jnp.dot(a.astype(BF16), b.astype(BF16), preferred_element_type=F32)


def _dot_nt(a, b):
    return lax.dot_general(a.astype(BF16), b.astype(BF16), (((1,), (1,)), ((), ())), preferred_element_type=F32)


def _dot_tn(a, b):
    return lax.dot_general(a.astype(BF16), b.astype(BF16), (((0,), (0,)), ((), ())), preferred_element_type=F32)


def _device_index(px, py, pc):
    return 4 * px + 2 * py + pc


def _direct_gather(src_ref, dst_ref, send_sems, recv_sems, local_sem):
    x, y, c = lax.axis_index("x"), lax.axis_index("y"), lax.axis_index("c")
    own = pltpu.make_async_copy(src_ref, dst_ref.at[_device_index(x, y, c)], local_sem)
    sends, arrivals = [], []
    for k in range(1, N_DEV):
        peer = (1 - x if k & 4 else x, 1 - y if k & 2 else y, 1 - c if k & 1 else c)
        for slot, out in ((_device_index(x, y, c), sends), (_device_index(*peer), arrivals)):
            out.append(pltpu.make_async_remote_copy(
                src_ref=src_ref, dst_ref=dst_ref.at[slot], send_sem=send_sems.at[k - 1], recv_sem=recv_sems.at[k - 1],
                device_id=peer, device_id_type=MESH))
    return own, sends, arrivals


GATHER_SEMS = [pltpu.SemaphoreType.DMA((N_DEV - 1,)), pltpu.SemaphoreType.DMA((N_DEV - 1,)), pltpu.SemaphoreType.DMA(())]


def _all_gather_small(block):
    def body(x_ref, out_ref, send_sems, recv_sems, local_sem):
        own, sends, arrivals = _direct_gather(x_ref, out_ref, send_sems, recv_sems, local_sem)
        own.start()
        for cp in sends:
            cp.start()
        for cp in arrivals:
            cp.wait_recv()
        for cp in sends:
            cp.wait_send()
        own.wait()

    return pl.pallas_call(
        body, name="all_gather_meta",
        out_shape=jax.ShapeDtypeStruct((N_DEV,) + block.shape, block.dtype),
        in_specs=[ANY], out_specs=ANY, scratch_shapes=list(GATHER_SEMS),
    )(block)


def _peer(k):
    x, y, c = lax.axis_index("x"), lax.axis_index("y"), lax.axis_index("c")
    return (1 - x if k & 4 else x, 1 - y if k & 2 else y, 1 - c if k & 1 else c)


def _me():
    return _device_index(lax.axis_index("x"), lax.axis_index("y"), lax.axis_index("c"))


def _remote(src, dst, send_sem, recv_sem, peer_bits):
    return pltpu.make_async_remote_copy(src_ref=src, dst_ref=dst, send_sem=send_sem, recv_sem=recv_sem,
                                        device_id=_peer(peer_bits), device_id_type=MESH)


N_ROW_GRADS = 3
SCATTER_SEMS = [pltpu.SemaphoreType.DMA((N_DEV - 1,)), pltpu.SemaphoreType.DMA((N_DEV - 1,)), pltpu.SemaphoreType.DMA(())]
SCATTER_ROWS_SEMS = [pltpu.SemaphoreType.DMA((7 * N_ROW_GRADS,)), pltpu.SemaphoreType.DMA((7 * N_ROW_GRADS,)),
                     pltpu.SemaphoreType.DMA((N_ROW_GRADS,))]


def _scatter_slabs(g_ref, first, recv_ref, send_sems, recv_sems, local_sem):
    n = g_ref.shape[0]
    me = _me()

    def each(on_send, on_local, on_arrival):
        for kk in range(1, N_DEV):
            peer = jnp.bitwise_xor(me, kk)

            @pl.when((peer >= first) & (peer < first + n))
            def _(kk=kk, peer=peer):
                on_send(_remote(g_ref.at[peer - first], recv_ref.at[me], send_sems.at[kk - 1], recv_sems.at[kk - 1], kk))

        @pl.when((me >= first) & (me < first + n))
        def _():
            on_local(pltpu.make_async_copy(g_ref.at[me - first], recv_ref.at[me], local_sem))
            if on_arrival is not None:
                for kk in range(1, N_DEV):
                    on_arrival(_remote(g_ref.at[0], recv_ref.at[jnp.bitwise_xor(me, kk)], send_sems.at[kk - 1],
                                       recv_sems.at[kk - 1], kk))

    start = lambda: each(lambda cp: cp.start(), lambda cp: cp.start(), None)
    finish = lambda: each(lambda cp: cp.wait_send(), lambda cp: cp.wait(), lambda cp: cp.wait_recv())
    return start, finish


def _scatter_rows(g_ref, recv_ref, send_sems, recv_sems, local_sems):
    me = _me()
    rows = lambda m, dev: g_ref.at[m, pl.ds(dev * 128, 128), :]

    def copies():
        local = [pltpu.make_async_copy(rows(m, me), recv_ref.at[me, m], local_sems.at[m]) for m in range(N_ROW_GRADS)]
        sends, arrivals = [], []
        for m in range(N_ROW_GRADS):
            for kk in range(1, N_DEV):
                peer, sems = jnp.bitwise_xor(me, kk), (send_sems.at[7 * m + kk - 1], recv_sems.at[7 * m + kk - 1])
                sends.append(_remote(rows(m, peer), recv_ref.at[me, m], *sems, kk))
                arrivals.append(_remote(rows(m, me), recv_ref.at[peer, m], *sems, kk))
        return local, sends, arrivals

    def start():
        local, sends, _ = copies()
        for cp in local + sends:
            cp.start()

    def finish():
        local, sends, arrivals = copies()
        for cp in arrivals:
            cp.wait_recv()
        for cp in sends:
            cp.wait_send()
        for cp in local:
            cp.wait()

    return start, finish


LOW_OWNERS = 3


def _pair_reduce_low(g_lo):
    def body(g_ref, out_ref, got, kept, send_sems, recv_sems, local_sems):
        c = lax.axis_index("c")

        def to_sibling(slab, slot):
            return _remote(g_ref.at[slab], got.at[slot], send_sems.at[slot], recv_sems.at[slot], 1)

        def keep(slab, slot):
            return pltpu.make_async_copy(g_ref.at[slab], kept.at[slot], local_sems.at[slot])

        def add(slot):
            out_ref[slot] = (kept[slot].astype(F32) + got[slot].astype(F32)).astype(out_ref.dtype)

        @pl.when(c == 0)
        def _():
            copies = [to_sibling(1, 0), keep(0, 0), keep(2, 1)]
            for cp in copies:
                cp.start()
            to_sibling(0, 0).wait_recv()
            to_sibling(2, 1).wait_recv()
            copies[0].wait_send()
            copies[1].wait()
            copies[2].wait()
            add(0)
            add(1)

        @pl.when(c == 1)
        def _():
            copies = [to_sibling(0, 0), to_sibling(2, 1), keep(1, 0)]
            for cp in copies:
                cp.start()
            to_sibling(1, 0).wait_recv()
            copies[0].wait_send()
            copies[1].wait_send()
            copies[2].wait()
            add(0)
            out_ref[1] = jnp.zeros(out_ref.shape[1:], out_ref.dtype)

    pair = (2,) + g_lo.shape[1:]
    return pl.pallas_call(
        body, name="pair_reduce_low",
        out_shape=jax.ShapeDtypeStruct(pair, g_lo.dtype),
        in_specs=[ANY], out_specs=pl.BlockSpec(memory_space=pltpu.VMEM),
        scratch_shapes=[pltpu.VMEM(pair, g_lo.dtype), pltpu.VMEM(pair, g_lo.dtype), pltpu.SemaphoreType.DMA((2,)),
                        pltpu.SemaphoreType.DMA((2,)), pltpu.SemaphoreType.DMA((2,))],
        compiler_params=_params(),
    )(g_lo)


def _scatter_low(part_ref, recv_ref, send_sems, recv_sems, local_sem):
    x, y, c = lax.axis_index("x"), lax.axis_index("y"), lax.axis_index("c")
    chip = 2 * x + y
    routes = ((0, (0, 0, c), 0, None), (1, (0, 1, 0), 1, 0))

    def each(on_send, on_local, on_arrival):
        for slot, owner, owner_chip, core in routes:
            holds = (c == core) if core is not None else (c >= 0)
            rel = jnp.bitwise_xor(chip, owner_chip)

            @pl.when(holds & (rel != 0))
            def _(slot=slot, owner=owner, rel=rel):
                on_send(pltpu.make_async_remote_copy(
                    src_ref=part_ref.at[slot], dst_ref=recv_ref.at[chip], send_sem=send_sems.at[slot],
                    recv_sem=recv_sems.at[rel - 1], device_id=owner, device_id_type=MESH))

            @pl.when(holds & (rel == 0))
            def _(slot=slot, owner=owner, owner_chip=owner_chip):
                on_local(pltpu.make_async_copy(part_ref.at[slot], recv_ref.at[chip], local_sem))
                if on_arrival is not None:
                    for r in range(1, 4):
                        on_arrival(pltpu.make_async_remote_copy(
                            src_ref=part_ref.at[slot], dst_ref=recv_ref.at[r ^ owner_chip], send_sem=send_sems.at[slot],
                            recv_sem=recv_sems.at[r - 1], device_id=owner, device_id_type=MESH))

    start = lambda: each(lambda cp: cp.start(), lambda cp: cp.start(), None)
    finish = lambda: each(lambda cp: cp.wait_send(), lambda cp: cp.wait(), lambda cp: cp.wait_recv())
    return start, finish


def _adam_update(g, w, m, v):
    mn = ADAM_B1 * m + (1.0 - ADAM_B1) * g
    vn = ADAM_B2 * v + (1.0 - ADAM_B2) * (g * g)
    m_hat = mn / (1.0 - ADAM_B1 ** ADAM_STEP)
    v_hat = vn / (1.0 - ADAM_B2 ** ADAM_STEP)
    return -ADAM_LR * (m_hat / (jnp.sqrt(v_hat) + ADAM_EPS) + ADAM_WD * w), mn, vn


def _device_sum(parts):
    t = [p.astype(F32) for p in parts]
    return ((t[0] + t[1]) + (t[2] + t[3])) + ((t[4] + t[5]) + (t[6] + t[7]))


def _finish(a_hi, a_lo, n_lo, b3, misc, big, rows3, small):
    n_steps = 4
    tb, tr3 = 1024 // n_steps, 128 // n_steps

    def body(*refs):
        it = iter(refs)
        hi_ref, lo_ref, b_ref, misc_hbm = next(it), next(it), next(it), next(it)
        big_in = [next(it) for _ in range(3)]
        rows_in = [[next(it) for _ in range(3)] for _ in rows3]
        small_in = [[next(it) for _ in range(3)] for _ in small]
        big_out = [next(it) for _ in range(4)]
        rows_out = [[next(it) for _ in range(4)] for _ in rows3]
        small_out = [[next(it) for _ in range(4)] for _ in small]
        m_vmem, send_sems, recv_sems, local_sem = next(it), next(it), next(it), next(it)
        start, finish = _scatter_slabs(misc_hbm, 0, m_vmem, send_sems, recv_sems, local_sem)
        step = pl.program_id(0)

        @pl.when(step == 0)
        def _():
            start()

        def apply(g, ins, outs):
            d, mn, vn = _adam_update(g, ins[0][...], ins[1][...], ins[2][...])
            for r, val in zip(outs, (g, d, mn, vn)):
                r[...] = val

        lo = [lo_ref[s].astype(F32) for s in range(4)]
        g_big = jnp.where(_me() < n_lo, (lo[0] + lo[1]) + (lo[2] + lo[3]), _device_sum([hi_ref[s] for s in range(N_DEV)]))
        apply(g_big[None], big_in, big_out)
        for k in range(len(rows3)):
            apply(_device_sum([b_ref[s, k] for s in range(N_DEV)])[None], rows_in[k], rows_out[k])

        @pl.when(step == n_steps - 1)
        def _():
            finish()
            for (row0, lanes, ins), r_in, r_out in zip(small, small_in, small_out):
                n = ins[0].shape[0]
                apply(_device_sum([m_vmem[s, row0:row0 + n, :lanes] for s in range(N_DEV)]), r_in, r_out)

    whole = lambda shape: pl.BlockSpec(shape, lambda i: (0,) * len(shape))
    big_blk = pl.BlockSpec((1, tb, 1024), lambda i: (0, i, 0))
    rows_blk = pl.BlockSpec((1, tr3, 1024), lambda i: (0, i, 0))
    in_specs = [pl.BlockSpec((N_DEV, tb, 1024), lambda i: (0, i, 0)), pl.BlockSpec((4, tb, 1024), lambda i: (0, i, 0)),
                pl.BlockSpec((N_DEV, 3, tr3, 1024), lambda i: (0, 0, i, 0)), ANY]
    in_specs += [big_blk] * 3 + [rows_blk] * (3 * len(rows3))
    out_specs = [big_blk] * 4 + [rows_blk] * (4 * len(rows3))
    out_shape = [jax.ShapeDtypeStruct(big[0].shape, F32)] * 4
    for w, _, _ in rows3:
        out_shape += [jax.ShapeDtypeStruct(w.shape, F32)] * 4
    args = [a_hi, a_lo, b3, misc, *big]
    for t in rows3:
        args += list(t)
    for _, _, t in small:
        in_specs += [whole(t[0].shape)] * 3
        out_specs += [whole(t[0].shape)] * 4
        out_shape += [jax.ShapeDtypeStruct(t[0].shape, F32)] * 4
        args += list(t)
    outs = pl.pallas_call(
        body, name="reduce_sum_adamw", out_shape=tuple(out_shape), grid=(n_steps,),
        in_specs=in_specs, out_specs=tuple(out_specs),
        scratch_shapes=[pltpu.VMEM(misc.shape, misc.dtype)] + list(SCATTER_SEMS),
        compiler_params=_params(("arbitrary",)),
    )(*args)
    return [tuple(outs[4 * k:4 * k + 4]) for k in range(len(outs) // 4)]


def _first_norm(zp, norm_w):
    n_rows = zp.shape[0]
    tr = _row_tile(n_rows, 832)

    def body(z_ref, nw_ref, h_ref):
        z = z_ref[...]
        r = lax.rsqrt(jnp.mean(z * z, axis=-1, keepdims=True) + EPS)
        h_ref[...] = (z * r * nw_ref[...]).astype(BF16)

    return pl.pallas_call(
        body, name="first_norm",
        out_shape=jax.ShapeDtypeStruct((n_rows, D_MODEL), BF16),
        grid=(n_rows // tr,),
        in_specs=[pl.BlockSpec((tr, D_MODEL), lambda i: (i, 0)), pl.BlockSpec((1, D_MODEL), lambda i: (0, 0))],
        out_specs=pl.BlockSpec((tr, D_MODEL), lambda i: (i, 0)),
        compiler_params=_params(("arbitrary",)),
    )(zp, norm_w)


def _gather_order():
    x, y, c = lax.axis_index("x"), lax.axis_index("y"), lax.axis_index("c")
    chips = [(1 - x, y), (x, 1 - y), (1 - x, 1 - y)]
    order = [_device_index(x, y, c), _device_index(x, y, 1 - c)]
    order += [_device_index(*q, c) for q in chips] + [_device_index(*q, 1 - c) for q in chips]
    return jnp.stack(order).astype(jnp.int32)


def _in_projection(h, w_shard, b_blocks, order):
    n_rows = h.shape[0]
    tr = _row_tile(n_rows, 832)
    nt = n_rows // tr

    def body(order_ref, h_ref, w_hbm, b_ref, p_ref, w_out, w_vmem, send_sems, recv_sems, local_sem, out_sems):
        s, i = pl.program_id(0), pl.program_id(1)
        x, y, c = lax.axis_index("x"), lax.axis_index("y"), lax.axis_index("c")
        me, sibling = (x, y, c), (x, y, 1 - c)
        chips = [(1 - x, y), (x, 1 - y), (1 - x, 1 - y)]

        def slot(px, py, pc):
            return w_vmem.at[_device_index(px, py, pc)]

        def copy(k, blk, to, src=None):
            return pltpu.make_async_remote_copy(
                src_ref=slot(*blk) if src is None else src, dst_ref=slot(*blk),
                send_sem=send_sems.at[k], recv_sem=recv_sems.at[k], device_id=to, device_id_type=MESH)

        own = pltpu.make_async_copy(w_hbm, slot(*me), local_sem)
        first = [copy(0, me, sibling, src=w_hbm)] + [copy(1 + j, me, (*q, c), src=w_hbm) for j, q in enumerate(chips)]
        passed = [copy(4 + j, (*q, c), sibling) for j, q in enumerate(chips)]
        arrivals = [None, copy(0, sibling, me)]
        arrivals += [copy(1 + j, (*q, c), me) for j, q in enumerate(chips)]
        arrivals += [copy(4 + j, (*q, 1 - c), me) for j, q in enumerate(chips)]

        def keep(step):
            return pltpu.make_async_copy(w_vmem.at[order_ref[step]], w_out.at[order_ref[step]], out_sems.at[step])

        for step in range(N_DEV):
            @pl.when((i == 0) & (s == step))
            def _(step=step):
                if step == 0:
                    own.start()
                    for cp in first:
                        cp.start()
                    own.wait()
                else:
                    arrivals[step].wait_recv()
                    if 2 <= step <= 4:
                        passed[step - 2].start()
                keep(step).start()

        p_ref[0] = jnp.dot(h_ref[...], w_vmem[order_ref[s]], preferred_element_type=F32) + b_ref[0]

        @pl.when((s == N_DEV - 1) & (i == nt - 1))
        def _():
            for cp in first + passed:
                cp.wait_send()
            for step in range(N_DEV):
                keep(step).wait()

    return pl.pallas_call(
        body, name="in_projection_gather",
        out_shape=(jax.ShapeDtypeStruct((N_COLBLK, n_rows, 1024), F32),
                   jax.ShapeDtypeStruct((N_DEV, D_MODEL, 1024), BF16)),
        grid_spec=pltpu.PrefetchScalarGridSpec(
            num_scalar_prefetch=1, grid=(N_DEV, nt),
            in_specs=[pl.BlockSpec((tr, D_MODEL), lambda s, i, o: (i, 0)), ANY,
                      pl.BlockSpec((1, 1, 1024), lambda s, i, o: (o[s], 0, 0))],
            out_specs=(pl.BlockSpec((1, tr, 1024), lambda s, i, o: (o[s], i, 0)), ANY),
            scratch_shapes=[pltpu.VMEM((N_DEV, D_MODEL, 1024), BF16), pltpu.SemaphoreType.DMA((7,)),
                            pltpu.SemaphoreType.DMA((7,)), pltpu.SemaphoreType.DMA(()), pltpu.SemaphoreType.DMA((N_DEV,))]),
        compiler_params=_params(("arbitrary", "arbitrary")),
    )(order, h, w_shard, b_blocks)


def _lower_bound(lb_ref):
    l0, l1 = lb_ref[0:1, :], lb_ref[1:2, :]
    _, lb = _sigmoid_pair(l1 - l0)
    return lb


def _chunk_gates(fz, lb, valid):
    sig, nsig = _sigmoid_pair(fz)
    f = lb + (1.0 - lb) * sig
    g2 = jnp.where(valid, jnp.log2(f), 0.0)
    k = jnp.where(valid, (1.0 - lb) * nsig, 0.0)
    return sig, nsig, f, g2, k


def _tri(n, upper=False):
    r = lax.broadcasted_iota(jnp.int32, (n, n), 0)
    c = lax.broadcasted_iota(jnp.int32, (n, n), 1)
    return jnp.where((r <= c) if upper else (r >= c), 1.0, 0.0).astype(BF16)


def _tri_dot(tri, x):
    hi = x.astype(BF16)
    rest = x - hi.astype(F32)
    mid = rest.astype(BF16)
    low = (rest - mid.astype(F32)).astype(BF16)
    return (jnp.dot(tri, hi, preferred_element_type=F32) + jnp.dot(tri, mid, preferred_element_type=F32)
            + jnp.dot(tri, low, preferred_element_type=F32))


def _intra_scores(q_ref, k_ref, b2_ref, col0):
    cols = pl.ds(col0, HEAD_DIM)
    rows_s = lax.broadcasted_iota(jnp.int32, (SUB, 1), 0)
    lanes_c = lax.broadcasted_iota(jnp.int32, (1, CHUNK), 1)
    blocks = []
    for i in range(N_SUB):
        lo = i * SUB
        qi = q_ref[lo:lo + SUB, cols]
        bi = b2_ref[lo:lo + SUB, cols]
        if i == 0:
            acc = jnp.zeros((SUB, CHUNK), F32)
        else:
            ref_i = b2_ref[lo:lo + 1, cols]
            qt = qi * jnp.exp2(bi - ref_i)
            kt = jnp.concatenate([k_ref[0:lo, cols] * jnp.exp2(ref_i - b2_ref[0:lo, cols]),
                                  jnp.zeros((CHUNK - lo, HEAD_DIM), F32)], axis=0)
            acc = _dot_nt(qt, kt)
        for s in range(SUB):
            b_s = b2_ref[lo + s:lo + s + 1, cols]
            k_s = k_ref[lo + s:lo + s + 1, cols]
            w = jnp.exp2(jnp.minimum(bi - b_s, 0.0))
            col = jnp.sum((qi * w) * k_s, axis=-1, keepdims=True)
            acc = jnp.where(lanes_c == lo + s, col, acc)
        blocks.append(jnp.where(lanes_c <= lo + rows_s, acc, 0.0))
    return jnp.concatenate(blocks, axis=0)


def _hgrn_forward(p, lb_logits, w_rest):
    n_rows = p.shape[1]
    n_chunks = n_rows // CHUNK
    width = HEADS * HEAD_DIM

    def body(q_ref, fz_ref, v_ref, lb_ref, rest_ref, o_ref, st_out_ref, a_out_ref, rest_out,
             state, k_vmem, b2_vmem, send_sems, recv_sems, local_sem):
        n = pl.program_id(0)
        own, sends, arrivals = _direct_gather(rest_ref, rest_out, send_sems, recv_sems, local_sem)

        @pl.when(n == 0)
        def _():
            state[...] = jnp.zeros_like(state)
            own.start()
            for cp in sends:
                cp.start()

        rows = n * CHUNK + lax.broadcasted_iota(jnp.int32, (CHUNK, 1), 0)
        valid = rows >= PAD_ROWS
        lb = _lower_bound(lb_ref)
        _, _, _, g2, k = _chunk_gates(fz_ref[0], lb, valid)
        k_vmem[...] = k
        b2_vmem[...] = _tri_dot(_tri(CHUNK), g2)
        q_view = q_ref.at[0]
        for h in range(HEADS):
            cols = pl.ds(h * HEAD_DIM, HEAD_DIM)
            st = state[h]
            st_out_ref[0, h] = st
            bh = b2_vmem[:, cols]
            kh = k_vmem[:, cols]
            vh = jnp.where(valid, v_ref[0, :, cols], 0.0)
            qe = q_ref[0, :, cols] * jnp.exp2(bh)
            a = _intra_scores(q_view, k_vmem, b2_vmem, h * HEAD_DIM).astype(BF16)
            a_out_ref[0, h] = a
            o_ref[:, cols] = _dot_nt(qe, st) + _dot(a, vh)
            b_last = b2_vmem[CHUNK - 1:CHUNK, cols]
            kd = kh * jnp.exp2(b_last - bh)
            state[h] = st * jnp.exp2(b_last) + _dot_tn(vh, kd)

        @pl.when(n == n_chunks - 1)
        def _():
            for cp in arrivals:
                cp.wait_recv()
            for cp in sends:
                cp.wait_send()
            own.wait()

    blk = lambda c: pl.BlockSpec((1, CHUNK, width), lambda n, c=c: (c, n, 0))
    return pl.pallas_call(
        body, name="hgrn_forward",
        out_shape=(jax.ShapeDtypeStruct((n_rows, width), F32),
                   jax.ShapeDtypeStruct((n_chunks, HEADS, HEAD_DIM, HEAD_DIM), F32),
                   jax.ShapeDtypeStruct((n_chunks, HEADS, CHUNK, CHUNK), BF16),
                   jax.ShapeDtypeStruct((N_DEV,) + w_rest.shape, w_rest.dtype)),
        grid=(n_chunks,),
        in_specs=[blk(0), blk(1), blk(2), pl.BlockSpec((2, width), lambda n: (0, 0)), ANY],
        out_specs=(pl.BlockSpec((CHUNK, width), lambda n: (n, 0)),
                   pl.BlockSpec((1, HEADS, HEAD_DIM, HEAD_DIM), lambda n: (n, 0, 0, 0)),
                   pl.BlockSpec((1, HEADS, CHUNK, CHUNK), lambda n: (n, 0, 0, 0)), ANY),
        scratch_shapes=[pltpu.VMEM((HEADS, HEAD_DIM, HEAD_DIM), F32), pltpu.VMEM((CHUNK, width), F32),
                        pltpu.VMEM((CHUNK, width), F32)] + list(GATHER_SEMS),
        compiler_params=_params(("arbitrary",)),
    )(p, p, p, lb_logits, w_rest)


def _hgrn_backward(p, lb_logits, states, scores, d_o, g_slabs, first_owner, g_rows):
    n_rows = p.shape[1]
    n_chunks = n_rows // CHUNK
    width = HEADS * HEAD_DIM

    def body(q_ref, fz_ref, v_ref, lb_ref, st_ref, a_ref, do_ref, gs_hbm, gr_hbm, dp_ref, dbias_ref, dlb_ref, rs_hbm, rr_hbm,
             dstate, k_vmem, b2_vmem, *sems):
        step = pl.program_id(0)
        n = n_chunks - 1 - step
        start_slabs, finish_slabs = _scatter_slabs(gs_hbm, first_owner, rs_hbm, *sems[:3])
        start_rows, finish_rows = _scatter_rows(gr_hbm, rr_hbm, *sems[3:])

        @pl.when(step == 0)
        def _():
            dstate[...] = jnp.zeros_like(dstate)
            dbias_ref[...] = jnp.zeros_like(dbias_ref)
            dlb_ref[...] = jnp.zeros_like(dlb_ref)
            start_slabs()
            start_rows()

        rows = n * CHUNK + lax.broadcasted_iota(jnp.int32, (CHUNK, 1), 0)
        valid = rows >= PAD_ROWS
        lb = _lower_bound(lb_ref)
        sig, nsig, f, g2, k = _chunk_gates(fz_ref[0], lb, valid)
        k_vmem[...] = k
        b2_vmem[...] = _tri_dot(_tri(CHUNK), g2)
        rows_c = lax.broadcasted_iota(jnp.int32, (CHUNK, 1), 0)
        rows_s = lax.broadcasted_iota(jnp.int32, (SUB, 1), 0)
        lanes_c = lax.broadcasted_iota(jnp.int32, (1, CHUNK), 1)
        causal = lax.broadcasted_iota(jnp.int32, (CHUNK, CHUNK), 0) >= lax.broadcasted_iota(jnp.int32, (CHUNK, CHUNK), 1)
        tri_up = _tri(CHUNK, upper=True)
        for h in range(HEADS):
            cols = pl.ds(h * HEAD_DIM, HEAD_DIM)
            st = st_ref[0, h]
            dst = dstate[h]
            qh = q_ref[0, :, cols]
            bh = b2_vmem[:, cols]
            kh = k_vmem[:, cols]
            vh = jnp.where(valid, v_ref[0, :, cols], 0.0)
            doh = do_ref[:, cols]
            eb = jnp.exp2(bh)
            qe = qh * eb
            b_last = b2_vmem[CHUNK - 1:CHUNK, cols]
            e_last = jnp.exp2(b_last)
            decay_k = jnp.exp2(b_last - bh)
            kd = kh * decay_k
            dqe = _dot(doh, st)
            da = jnp.where(causal, _dot_nt(doh, vh), 0.0)
            dv = _dot_tn(a_ref[0, h], doh) + _dot_nt(kd, dst)
            dkd = _dot(vh, dst)
            dstate[h] = dst * e_last + _dot_tn(doh, qe)
            db_last = (jnp.sum(dst * st, axis=0, keepdims=True) * e_last
                       + jnp.sum(dkd * kd, axis=0, keepdims=True))
            dq_blocks, dk_blocks = [], []
            dk_earlier = jnp.zeros((CHUNK, HEAD_DIM), F32)
            for i in range(N_SUB):
                lo = i * SUB
                qi = q_ref[0, lo:lo + SUB, cols]
                bi = b2_vmem[lo:lo + SUB, cols]
                da_i = da[lo:lo + SUB, :]
                if i == 0:
                    dq_i = jnp.zeros((SUB, HEAD_DIM), F32)
                else:
                    ref_i = b2_vmem[lo:lo + 1, cols]
                    eq = jnp.exp2(bi - ref_i)
                    ek = jnp.exp2(ref_i - b2_vmem[0:lo, cols])
                    later = jnp.zeros((CHUNK - lo, HEAD_DIM), F32)
                    kt = jnp.concatenate([k_vmem[0:lo, cols] * ek, later], axis=0)
                    dq_i = _dot(da_i, kt) * eq
                    dk_earlier = dk_earlier + jnp.concatenate([_dot_tn(da_i, qi * eq)[0:lo] * ek, later], axis=0)
                dk_i = jnp.zeros((SUB, HEAD_DIM), F32)
                for s in range(SUB):
                    b_s = b2_vmem[lo + s:lo + s + 1, cols]
                    k_s = k_vmem[lo + s:lo + s + 1, cols]
                    w = jnp.exp2(jnp.minimum(bi - b_s, 0.0))
                    da_col = jnp.sum(jnp.where(lanes_c == lo + s, da_i, 0.0), axis=-1, keepdims=True)
                    gw = da_col * w
                    dq_i = dq_i + gw * k_s
                    dk_i = jnp.where(rows_s == s, jnp.sum(gw * qi, axis=0, keepdims=True), dk_i)
                dq_blocks.append(dq_i)
                dk_blocks.append(dk_i)
            dq_intra = jnp.concatenate(dq_blocks, axis=0)
            dk_intra = jnp.concatenate(dk_blocks, axis=0) + dk_earlier
            dq = dqe * eb + dq_intra
            dk = dkd * decay_k + dk_intra
            db = dqe * qe - dkd * kd + qh * dq_intra - kh * dk_intra
            db = db + jnp.where(rows_c == CHUNK - 1, db_last, 0.0)
            dg = _tri_dot(tri_up, db)
            fh = f[:, h * HEAD_DIM:(h + 1) * HEAD_DIM]
            sh = sig[:, h * HEAD_DIM:(h + 1) * HEAD_DIM]
            nh = nsig[:, h * HEAD_DIM:(h + 1) * HEAD_DIM]
            lbh = lb[:, h * HEAD_DIM:(h + 1) * HEAD_DIM]
            df = jnp.where(valid, dg / fh - dk, 0.0)
            dfz = df * (1.0 - lbh) * sh * nh
            dq = jnp.where(valid, dq, 0.0)
            dv = jnp.where(valid, dv, 0.0)
            dlb_ref[:, cols] += jnp.sum(df * nh, axis=0, keepdims=True)
            dp_ref[0, :, cols] = dq.astype(BF16)
            dp_ref[1, :, cols] = dfz.astype(BF16)
            dp_ref[2, :, cols] = dv.astype(BF16)
            dbias_ref[0, :, cols] += jnp.sum(dq, axis=0, keepdims=True)
            dbias_ref[1, :, cols] += jnp.sum(dfz, axis=0, keepdims=True)
            dbias_ref[2, :, cols] += jnp.sum(dv, axis=0, keepdims=True)

        @pl.when(step == n_chunks - 1)
        def _():
            finish_slabs()
            finish_rows()

    rev = lambda s: n_chunks - 1 - s
    blk = lambda c: pl.BlockSpec((1, CHUNK, width), lambda s, c=c: (c, rev(s), 0))
    return pl.pallas_call(
        body, name="hgrn_backward",
        out_shape=(jax.ShapeDtypeStruct((3, n_rows, width), BF16),
                   jax.ShapeDtypeStruct((3, 1, width), F32),
                   jax.ShapeDtypeStruct((1, width), F32),
                   jax.ShapeDtypeStruct((N_DEV,) + g_slabs.shape[1:], g_slabs.dtype),
                   jax.ShapeDtypeStruct((N_DEV, N_ROW_GRADS, 128, g_rows.shape[2]), g_rows.dtype)),
        grid=(n_chunks,),
        in_specs=[blk(0), blk(1), blk(2), pl.BlockSpec((2, width), lambda s: (0, 0)),
                  pl.BlockSpec((1, HEADS, HEAD_DIM, HEAD_DIM), lambda s: (rev(s), 0, 0, 0)),
                  pl.BlockSpec((1, HEADS, CHUNK, CHUNK), lambda s: (rev(s), 0, 0, 0)),
                  pl.BlockSpec((CHUNK, width), lambda s: (rev(s), 0)), ANY, ANY],
        out_specs=(pl.BlockSpec((3, CHUNK, width), lambda s: (0, rev(s), 0)),
                   pl.BlockSpec((3, 1, width), lambda s: (0, 0, 0)),
                   pl.BlockSpec((1, width), lambda s: (0, 0)), ANY, ANY),
        scratch_shapes=[pltpu.VMEM((HEADS, HEAD_DIM, HEAD_DIM), F32), pltpu.VMEM((CHUNK, width), F32),
                        pltpu.VMEM((CHUNK, width), F32)] + list(SCATTER_SEMS) + list(SCATTER_ROWS_SEMS),
        compiler_params=_params(("arbitrary",)),
    )(p, p, p, lb_logits, states, scores, d_o, g_slabs, g_rows)


def _sigmoid_and_complement(x):
    s = 0.5 * jnp.tanh(0.5 * x) + 0.5
    return s, 1.0 - s


def _silu_and_grad(x):
    s, ns = _sigmoid_and_complement(x)
    return x * s, s * (1.0 + x * ns)


def _tail(p, o, zp, tgt, hg_norm_w, pool_w, pool_scale, w_down_hg, w_down_pool, w_out, final_norm_w):
    n_rows = zp.shape[0]
    tr = _row_tile(n_rows, 208)
    nt = n_rows // tr
    ext = tr + HALO
    n_groups = len(POOL_WINDOWS)

    def body(o_ref, ghg_ref, u_ref, gpool_ref, mhg_ref, mpool_ref, uhalo_ref, z_ref, tgt_ref,
             hgw_ref, pw_ref, ps_ref, wdh_ref, wdp_ref, wout_ref, fnw_ref,
             do_ref, dp_ref, dz2_ref, lhs_ref, rhs_ref,
             dbias_ref, dhgw_ref, dpw_ref, dps_ref, dfnw_ref, loss_ref, halo_vmem):
        step = pl.program_id(0)
        ti = nt - 1 - step

        @pl.when(step == 0)
        def _():
            halo_vmem[...] = jnp.zeros_like(halo_vmem)
            for r in (dbias_ref, dhgw_ref, dpw_ref, dps_ref, dfnw_ref, loss_ref):
                r[...] = jnp.zeros_like(r)

        rows = ti * tr + lax.broadcasted_iota(jnp.int32, (tr, 1), 0)
        valid = rows >= PAD_ROWS
        in_loss = rows >= CHUNK
        count_pos = jnp.maximum(rows - PAD_ROWS + 1, 1).astype(F32)

        o = o_ref[...]
        hgw = hgw_ref[...]
        inv_o, on_parts = [], []
        for h in range(HEADS):
            oh = o[:, h * HEAD_DIM:(h + 1) * HEAD_DIM]
            r = lax.rsqrt(jnp.mean(oh * oh, axis=-1, keepdims=True) + EPS)
            inv_o.append(r)
            on_parts.append(oh * r)
        o_hat = jnp.concatenate(on_parts, axis=1)
        o_n = o_hat * hgw
        g_hg = ghg_ref[0]
        silu_hg, dsilu_hg = _silu_and_grad(g_hg)
        a_hg = o_n * silu_hg
        y_hg = _dot(a_hg, wdh_ref[...])

        u = jnp.where(valid, u_ref[0], 0.0)
        u_prev = jnp.where(ti > 0, uhalo_ref[0], 0.0)
        u_ext = jnp.concatenate([u_prev, u], axis=0)
        pooled_parts, mixed_parts, inv_cnt = [], [], []
        for gi, win in enumerate(POOL_WINDOWS):
            lanes = slice(gi * POOL_GDIM, (gi + 1) * POOL_GDIM)
            s = u_ext[:, lanes]
            shift = 1
            while shift < win:
                s = s + pltpu.roll(s, shift, 0)
                shift *= 2
            ic = 1.0 / jnp.minimum(count_pos, float(win))
            inv_cnt.append(ic)
            pooled = s[HALO:] * ic - u[:, lanes]
            pooled_parts.append(pooled)
            mixed_parts.append(_dot(pooled, pw_ref[gi]))
        mixed = jnp.concatenate(mixed_parts, axis=1)
        ps = ps_ref[...]
        g_pool = gpool_ref[0]
        silu_pool, dsilu_pool = _silu_and_grad(g_pool)
        a_pool = mixed * ps * silu_pool
        y_pool = _dot(a_pool, wdp_ref[...])

        m_hg, m_pool = mhg_ref[0], mpool_ref[0]
        s_hg, ns_hg = _sigmoid_and_complement(m_hg)
        s_pool, ns_pool = _sigmoid_and_complement(m_pool)
        merged = s_hg * y_hg + s_pool * y_pool
        z2 = z_ref[...] + _dot(merged, wout_ref[...])
        r2 = lax.rsqrt(jnp.mean(z2 * z2, axis=-1, keepdims=True) + EPS)
        n2 = z2 * r2
        fnw = fnw_ref[...]
        err = jnp.where(in_loss, n2 * fnw - tgt_ref[...], 0.0)
        loss_ref[...] += jnp.sum(jnp.sum(err * err, axis=0, keepdims=True), axis=1, keepdims=True) * (0.5 / D_MODEL)
        dy = err * (1.0 / D_MODEL)

        dfnw_ref[...] += jnp.sum(dy * n2, axis=0, keepdims=True)
        gy = dy * fnw
        dz2 = r2 * (gy - n2 * jnp.mean(gy * n2, axis=-1, keepdims=True))
        dmerged = _dot_nt(dz2, wout_ref[...])
        dy_hg = s_hg * dmerged
        dy_pool = s_pool * dmerged
        dm_hg = dmerged * y_hg * s_hg * ns_hg
        dm_pool = dmerged * y_pool * s_pool * ns_pool
        da_hg = _dot_nt(dy_hg, wdh_ref[...])
        da_pool = _dot_nt(dy_pool, wdp_ref[...])

        d_on = da_hg * silu_hg
        dg_hg = da_hg * o_n * dsilu_hg
        dhgw_ref[...] += jnp.sum(d_on * o_hat, axis=0, keepdims=True)
        gyo = d_on * hgw
        do_parts = []
        for h in range(HEADS):
            lanes = slice(h * HEAD_DIM, (h + 1) * HEAD_DIM)
            gh, nh = gyo[:, lanes], o_hat[:, lanes]
            do_parts.append(inv_o[h] * (gh - nh * jnp.mean(gh * nh, axis=-1, keepdims=True)))
        do_ref[...] = jnp.concatenate(do_parts, axis=1)

        dmixed = da_pool * ps * silu_pool
        dps_ref[...] += jnp.sum(da_pool * mixed * silu_pool, axis=0, keepdims=True)
        dg_pool = da_pool * mixed * ps * dsilu_pool
        du_parts = []
        for gi, win in enumerate(POOL_WINDOWS):
            lanes = slice(gi * POOL_GDIM, (gi + 1) * POOL_GDIM)
            dmx = dmixed[:, lanes]
            dpooled = _dot_nt(dmx, pw_ref[gi])
            dpw_ref[gi] += _dot_tn(pooled_parts[gi], dmx)
            dpt = dpooled * inv_cnt[gi]
            s = jnp.concatenate([dpt, halo_vmem[:, lanes]], axis=0)
            shift = 1
            while shift < win:
                s = s + pltpu.roll(s, ext - shift, 0)
                shift *= 2
            du_parts.append(s[:tr] - dpooled)
            halo_vmem[:, lanes] = dpt[:HALO]
        du = jnp.where(valid, jnp.concatenate(du_parts, axis=1), 0.0)

        for c, val in enumerate((dg_hg, du, dg_pool, dm_hg, dm_pool)):
            dp_ref[c] = val.astype(BF16)
            dbias_ref[c] += jnp.sum(val, axis=0, keepdims=True)
        dz2_ref[...] = dz2
        for c, (lhs, rhs) in enumerate(((merged, dz2), (a_hg, dy_hg), (a_pool, dy_pool))):
            lhs_ref[c] = lhs.astype(BF16)
            rhs_ref[c] = rhs.astype(BF16)

    rev = lambda s: nt - 1 - s
    rowblk = pl.BlockSpec((tr, D_MODEL), lambda s: (rev(s), 0))
    pblk = lambda c: pl.BlockSpec((1, tr, 1024), lambda s, c=c: (c, rev(s), 0))
    halo_blk = pl.BlockSpec((1, HALO, 1024), lambda s: (4, jnp.maximum(rev(s) * (tr // HALO) - 1, 0), 0))
    full = lambda shape: pl.BlockSpec(shape, lambda s: (0,) * len(shape))
    vec = full((1, D_MODEL))
    mat = full((D_MODEL, D_MODEL))
    act3 = jax.ShapeDtypeStruct((3, n_rows, D_MODEL), BF16)
    act3_blk = pl.BlockSpec((3, tr, D_MODEL), lambda s: (0, rev(s), 0))
    return pl.pallas_call(
        body, name="tail_forward_backward",
        out_shape=(jax.ShapeDtypeStruct((n_rows, D_MODEL), F32),
                   jax.ShapeDtypeStruct((5, n_rows, 1024), BF16),
                   jax.ShapeDtypeStruct((n_rows, D_MODEL), F32),
                   act3, act3,
                   jax.ShapeDtypeStruct((5, 1, 1024), F32),
                   jax.ShapeDtypeStruct((1, D_MODEL), F32),
                   jax.ShapeDtypeStruct((n_groups, POOL_GDIM, POOL_GDIM), F32),
                   jax.ShapeDtypeStruct((1, D_MODEL), F32),
                   jax.ShapeDtypeStruct((1, D_MODEL), F32),
                   jax.ShapeDtypeStruct((1, 1), F32)),
        grid=(nt,),
        in_specs=[rowblk, pblk(3), pblk(4), pblk(5), pblk(6), pblk(7), halo_blk, rowblk, rowblk,
                  vec, full((n_groups, POOL_GDIM, POOL_GDIM)), vec, mat, mat, mat, vec],
        out_specs=(rowblk, pl.BlockSpec((5, tr, 1024), lambda s: (0, rev(s), 0)), rowblk,
                   act3_blk, act3_blk,
                   full((5, 1, 1024)), vec, full((n_groups, POOL_GDIM, POOL_GDIM)), vec, vec, full((1, 1))),
        scratch_shapes=[pltpu.VMEM((HALO, D_MODEL), F32)],
        compiler_params=_params(("arbitrary",)),
    )(o, p, p, p, p, p, p, zp, tgt, hg_norm_w, pool_w, pool_scale, w_down_hg, w_down_pool, w_out, final_norm_w)


def _in_projection_backward(dp_a, dp_b, w_blocks, zp, dz2, norm_w, chip_sums):
    n_rows = zp.shape[0]
    tr = _row_tile(n_rows, 416)
    nt = n_rows // tr
    na, nb = dp_a.shape[0], dp_b.shape[0]

    def body(dpa_ref, dpb_ref, w_hbm, z_ref, dz2_ref, nw_ref, gs_hbm, dz_ref, dnw_ref, rs_hbm, w_vmem, sem, *sems):
        i = pl.program_id(0)
        start_slabs, finish_slabs = _scatter_low(gs_hbm, rs_hbm, *sems)

        @pl.when(i == 0)
        def _():
            start_slabs()
            cp = pltpu.make_async_copy(w_hbm, w_vmem, sem)
            cp.start()
            cp.wait()
            dnw_ref[...] = jnp.zeros_like(dnw_ref)

        dh = jnp.zeros((tr, D_MODEL), F32)
        for j in range(na):
            dh = dh + _dot_nt(dpa_ref[j], w_vmem[j])
        for j in range(nb):
            dh = dh + _dot_nt(dpb_ref[j], w_vmem[na + j])
        z = z_ref[...]
        r = lax.rsqrt(jnp.mean(z * z, axis=-1, keepdims=True) + EPS)
        n1 = z * r
        dnw_ref[...] += jnp.sum(dh * n1, axis=0, keepdims=True)
        gh = dh * nw_ref[...]
        dz_ref[...] = dz2_ref[...] + r * (gh - n1 * jnp.mean(gh * n1, axis=-1, keepdims=True))

        @pl.when(i == nt - 1)
        def _():
            finish_slabs()

    rowblk = pl.BlockSpec((tr, D_MODEL), lambda i: (i, 0))
    vec = pl.BlockSpec((1, D_MODEL), lambda i: (0, 0))
    return pl.pallas_call(
        body, name="in_projection_backward",
        out_shape=(jax.ShapeDtypeStruct((n_rows, D_MODEL), F32), jax.ShapeDtypeStruct((1, D_MODEL), F32),
                   jax.ShapeDtypeStruct((4,) + chip_sums.shape[1:], chip_sums.dtype)),
        grid=(nt,),
        in_specs=[pl.BlockSpec((na, tr, 1024), lambda i: (0, i, 0)), pl.BlockSpec((nb, tr, 1024), lambda i: (0, i, 0)),
                  ANY, rowblk, rowblk, vec, ANY],
        out_specs=(rowblk, vec, ANY),
        scratch_shapes=[pltpu.VMEM((N_COLBLK, D_MODEL, 1024), BF16), pltpu.SemaphoreType.DMA(()),
                        pltpu.SemaphoreType.DMA((2,)), pltpu.SemaphoreType.DMA((3,)), pltpu.SemaphoreType.DMA(())],
        compiler_params=_params(("arbitrary",)),
    )(dp_a, dp_b, w_blocks, zp, dz2, norm_w, chip_sums)


def _weight_grad(xs, ys, name):
    shared = xs.ndim == 2
    n_rows, m = xs.shape[-2:]
    nb, _, n = ys.shape
    tk = _row_tile(n_rows, 2080)
    n_k = n_rows // tk

    def body(x_ref, y_ref, o_ref, acc):
        k = pl.program_id(1)

        @pl.when(k == 0)
        def _():
            acc[...] = jnp.zeros_like(acc)

        acc[...] += _dot_tn(x_ref[...] if shared else x_ref[0], y_ref[0])

        @pl.when(k == n_k - 1)
        def _():
            o_ref[0] = acc[...].astype(o_ref.dtype)

    x_spec = pl.BlockSpec((tk, m), lambda j, k: (k, 0)) if shared else pl.BlockSpec((1, tk, m), lambda j, k: (j, k, 0))
    return pl.pallas_call(
        body, name=name,
        out_shape=jax.ShapeDtypeStruct((nb, m, n), BF16),
        grid=(nb, n_k),
        in_specs=[x_spec, pl.BlockSpec((1, tk, n), lambda j, k: (j, k, 0))],
        out_specs=pl.BlockSpec((1, m, n), lambda j, k: (j, 0, 0)),
        scratch_shapes=[pltpu.VMEM((m, n), F32)],
        compiler_params=_params(("arbitrary", "arbitrary")),
    )(xs, ys)


def kernel(x, meta_tokens, norm_w, w_in, b_in, lb_logits, hg_norm_w, pool_w, pool_scale, w_down_hg, w_down_pool, w_out, final_norm_w, loss_target, m_meta_tokens, m_norm_w, m_w_in, m_b_in, m_lb_logits, m_hg_norm_w, m_pool_w, m_pool_scale, m_w_down_hg, m_w_down_pool, m_w_out, m_final_norm_w, v_meta_tokens, v_norm_w, v_w_in, v_b_in, v_lb_logits, v_hg_norm_w, v_pool_w, v_pool_scale, v_w_down_hg, v_w_down_pool, v_w_out, v_final_norm_w):
    seq = x.shape[1]

    meta_full = _all_gather_small(meta_tokens).transpose(1, 0, 2).reshape(N_META, D_MODEL)
    w_rest = jnp.concatenate([w_down_hg[0].astype(BF16), w_down_pool[0].astype(BF16), w_out[0].astype(BF16),
                              pool_w[0].astype(BF16).reshape(32, 1024)], axis=0)

    zp = jnp.concatenate([jnp.zeros((PAD_ROWS, D_MODEL), F32), meta_full, x[0]], axis=0)
    tgt = jnp.concatenate([jnp.zeros((CHUNK, D_MODEL), F32), loss_target[0]], axis=0)
    h = _first_norm(zp, norm_w)
    p, w_blocks = _in_projection(h, w_in[0].astype(BF16), b_in.reshape(N_COLBLK, 1, 1024), _gather_order())
    o, states, scores, rest = _hgrn_forward(p, lb_logits, w_rest)
    wdh = rest[:, REST_W_DOWN_HG:REST_W_DOWN_HG + 128].reshape(1024, 1024)
    wdp = rest[:, REST_W_DOWN_POOL:REST_W_DOWN_POOL + 128].reshape(1024, 1024)
    wout = rest[:, REST_W_OUT:REST_W_OUT + 128].reshape(1024, 1024)
    pw = rest[:, REST_POOL_W:REST_POOL_W + 32].reshape(N_DEV, 4, 32, 256).transpose(1, 0, 2, 3).reshape(4, 256, 256)
    (d_o, dp_b, dz2, grad_lhs, grad_rhs, dbias_b, d_hgw, d_pw, d_ps, d_fnw, loss_part) = _tail(
        p, o, zp, tgt, hg_norm_w, pw, pool_scale, wdh, wdp, wout, final_norm_w.reshape(1, D_MODEL))
    n_a = N_COLBLK - dp_b.shape[0]
    g_hi = _weight_grad(h, dp_b, "weight_grad_in_hi")
    g_rows = _weight_grad(grad_lhs, grad_rhs, "weight_grad_rows")
    dp_a, dbias_a, d_lb, recv_hi, recv_rows = _hgrn_backward(p, lb_logits, states, scores, d_o, g_hi, n_a, g_rows)
    assert n_a == LOW_OWNERS
    g_lo = _weight_grad(h, dp_a, "weight_grad_in_lo")
    dz, d_nw, recv_lo = _in_projection_backward(dp_a, dp_b, w_blocks, zp, dz2, norm_w, _pair_reduce_low(g_lo))

    lb = jax.nn.sigmoid(lb_logits[0:1] - lb_logits[1:2])
    d_l0 = d_lb * lb * (1.0 - lb)
    replicated = jnp.concatenate([dbias_a.reshape(3, 1024), dbias_b.reshape(5, 1024), d_nw, d_l0, -d_l0, d_hgw, d_ps, d_fnw,
                                  jnp.zeros((MISC_ROWS - MISC_FINAL_NORM_W - 1, 1024), F32)], axis=0)
    d_meta = dz[PAD_ROWS:CHUNK].reshape(N_META, N_DEV, 128).transpose(1, 0, 2)
    d_pw_blocks = d_pw.reshape(4, N_DEV, 32, 256).transpose(1, 0, 2, 3).reshape(N_DEV, 32, 1024)
    g_misc = jnp.concatenate([d_pw_blocks, jnp.pad(d_meta, ((0, 0), (0, 0), (0, 1024 - 128))),
                              jnp.broadcast_to(replicated[None], (N_DEV, 16, 1024))], axis=1)

    as_rows = lambda t, n: t.reshape(n, 1024)
    small = [(MISC_POOL_W, 1024, tuple(as_rows(t, 32) for t in (pool_w, m_pool_w, v_pool_w))),
             (MISC_META, 128, (meta_tokens, m_meta_tokens, v_meta_tokens)),
             (MISC_B_IN, 1024, tuple(as_rows(t, 8) for t in (b_in, m_b_in, v_b_in))),
             (MISC_NORM_W, 1024, (norm_w, m_norm_w, v_norm_w)),
             (MISC_LB, 1024, (lb_logits, m_lb_logits, v_lb_logits)),
             (MISC_HG_NORM_W, 1024, (hg_norm_w, m_hg_norm_w, v_hg_norm_w)),
             (MISC_POOL_SCALE, 1024, (pool_scale, m_pool_scale, v_pool_scale)),
             (MISC_FINAL_NORM_W, 1024, tuple(as_rows(t, 1) for t in (final_norm_w, m_final_norm_w, v_final_norm_w)))]
    res = _finish(recv_hi, recv_lo, n_a, recv_rows, g_misc, (w_in, m_w_in, v_w_in),
                  [(w_out, m_w_out, v_w_out), (w_down_hg, m_w_down_hg, v_w_down_hg), (w_down_pool, m_w_down_pool, v_w_down_pool)],
                  small)
    r_w_in, r_w_out, r_wdh, r_wdp, r_pw, r_meta, r_b_in, r_nw, r_lb, r_hgw, r_ps, r_fnw = res
    loss = lax.psum(loss_part[0, 0], ("x", "y", "c"))
    grad_x = dz[CHUNK:].reshape(1, seq, D_MODEL)
    per_kind = [(r_meta[k], r_nw[k], r_w_in[k], r_b_in[k].reshape(1, 8192), r_lb[k], r_hgw[k], r_pw[k].reshape(1, 4, 32, 256),
                 r_ps[k], r_wdh[k], r_wdp[k], r_w_out[k], r_fnw[k].reshape(1024)) for k in range(4)]
    return (loss, grad_x, *per_kind[0], *per_kind[1], *per_kind[2], *per_kind[3])
```

```python
import functools

import jax
import jax.numpy as jnp
from jax import lax
from jax.experimental import pallas as pl
from jax.experimental.pallas import tpu as pltpu

F32 = jnp.float32
BF16 = jnp.bfloat16

D_MODEL = 1024
N_META = 16
HEADS = 8
HEAD_DIM = 128
CHUNK = 64
SUB = 8
N_SUB = CHUNK // SUB
PAD_ROWS = CHUNK - N_META
POOL_WINDOWS = (2, 4, 8, 16)
POOL_GDIM = D_MODEL // len(POOL_WINDOWS)
HALO = 16
EPS = 1e-6
N_DEV = 8
N_COLBLK = 8
ADAM_LR, ADAM_B1, ADAM_B2, ADAM_EPS, ADAM_WD, ADAM_STEP = 0.001, 0.9, 0.999, 1e-08, 0.01, 10

VMEM_LIMIT = 56 * 1024 * 1024
MESH = pl.DeviceIdType.MESH
ANY = pl.BlockSpec(memory_space=pl.ANY)

REST_W_DOWN_HG = 0
REST_W_DOWN_POOL = 128
REST_W_OUT = 256
REST_POOL_W = 384
MISC_POOL_W = 0
MISC_META = 32
MISC_B_IN = 48
MISC_NORM_W = 56
MISC_LB = 57
MISC_HG_NORM_W = 59
MISC_POOL_SCALE = 60
MISC_FINAL_NORM_W = 61
MISC_LOSS = 62
MISC_ROWS = 64


def _params(sem=None):
    return pltpu.CompilerParams(dimension_semantics=sem, vmem_limit_bytes=VMEM_LIMIT)


def _row_tile(n_rows, prefer):
    best = 16
    for t in range(16, prefer + 1, 16):
        if n_rows % t == 0:
            best = t
    return best


def _sigmoid_pair(x):
    e = jnp.exp(-jnp.abs(x))
    r = 1.0 / (1.0 + e)
    er = e * r
    pos = x >= 0
    return jnp.where(pos, r, er), jnp.where(pos, er, r)


def _dot(a, b):
    return jnp.dot(a.astype(BF16), b.astype(BF16), preferred_element_type=F32)


def _dot_nt(a, b):
    return lax.dot_general(a.astype(BF16), b.astype(BF16), (((1,), (1,)), ((), ())), preferred_element_type=F32)


def _dot_tn(a, b):
    return lax.dot_general(a.astype(BF16), b.astype(BF16), (((0,), (0,)), ((), ())), preferred_element_type=F32)


def _device_index(px, py, pc):
    return 4 * px + 2 * py + pc


def _direct_gather(src_ref, dst_ref, send_sems, recv_sems, local_sem):
    x, y, c = lax.axis_index("x"), lax.axis_index("y"), lax.axis_index("c")
    own = pltpu.make_async_copy(src_ref, dst_ref.at[_device_index(x, y, c)], local_sem)
    sends, arrivals = [], []
    for k in range(1, N_DEV):
        peer = (1 - x if k & 4 else x, 1 - y if k & 2 else y, 1 - c if k & 1 else c)
        for slot, out in ((_device_index(x, y, c), sends), (_device_index(*peer), arrivals)):
            out.append(pltpu.make_async_remote_copy(
                src_ref=src_ref, dst_ref=dst_ref.at[slot], send_sem=send_sems.at[k - 1], recv_sem=recv_sems.at[k - 1],
                device_id=peer, device_id_type=MESH))
    return own, sends, arrivals


GATHER_SEMS = [pltpu.SemaphoreType.DMA((N_DEV - 1,)), pltpu.SemaphoreType.DMA((N_DEV - 1,)), pltpu.SemaphoreType.DMA(())]


def _all_gather_small(block):
    def body(x_ref, out_ref, send_sems, recv_sems, local_sem):
        own, sends, arrivals = _direct_gather(x_ref, out_ref, send_sems, recv_sems, local_sem)
        own.start()
        for cp in sends:
            cp.start()
        for cp in arrivals:
            cp.wait_recv()
        for cp in sends:
            cp.wait_send()
        own.wait()

    return pl.pallas_call(
        body, name="all_gather_meta",
        out_shape=jax.ShapeDtypeStruct((N_DEV,) + block.shape, block.dtype),
        in_specs=[ANY], out_specs=ANY, scratch_shapes=list(GATHER_SEMS),
    )(block)


def _peer(k):
    x, y, c = lax.axis_index("x"), lax.axis_index("y"), lax.axis_index("c")
    return (1 - x if k & 4 else x, 1 - y if k & 2 else y, 1 - c if k & 1 else c)


def _me():
    return _device_index(lax.axis_index("x"), lax.axis_index("y"), lax.axis_index("c"))


def _remote(src, dst, send_sem, recv_sem, peer_bits):
    return pltpu.make_async_remote_copy(src_ref=src, dst_ref=dst, send_sem=send_sem, recv_sem=recv_sem,
                                        device_id=_peer(peer_bits), device_id_type=MESH)


N_ROW_GRADS = 3
SCATTER_SEMS = [pltpu.SemaphoreType.DMA((N_DEV - 1,)), pltpu.SemaphoreType.DMA((N_DEV - 1,)), pltpu.SemaphoreType.DMA(())]
SCATTER_ROWS_SEMS = [pltpu.SemaphoreType.DMA((7 * N_ROW_GRADS,)), pltpu.SemaphoreType.DMA((7 * N_ROW_GRADS,)),
                     pltpu.SemaphoreType.DMA((N_ROW_GRADS,))]


def _scatter_slabs(g_ref, first, recv_ref, send_sems, recv_sems, local_sem):
    n = g_ref.shape[0]
    me = _me()

    def each(on_send, on_local, on_arrival):
        for kk in range(1, N_DEV):
            peer = jnp.bitwise_xor(me, kk)

            @pl.when((peer >= first) & (peer < first + n))
            def _(kk=kk, peer=peer):
                on_send(_remote(g_ref.at[peer - first], recv_ref.at[me], send_sems.at[kk - 1], recv_sems.at[kk - 1], kk))

        @pl.when((me >= first) & (me < first + n))
        def _():
            on_local(pltpu.make_async_copy(g_ref.at[me - first], recv_ref.at[me], local_sem))
            if on_arrival is not None:
                for kk in range(1, N_DEV):
                    on_arrival(_remote(g_ref.at[0], recv_ref.at[jnp.bitwise_xor(me, kk)], send_sems.at[kk - 1],
                                       recv_sems.at[kk - 1], kk))

    start = lambda: each(lambda cp: cp.start(), lambda cp: cp.start(), None)
    finish = lambda: each(lambda cp: cp.wait_send(), lambda cp: cp.wait(), lambda cp: cp.wait_recv())
    return start, finish


def _scatter_rows(g_ref, recv_ref, send_sems, recv_sems, local_sems):
    me = _me()
    rows = lambda m, dev: g_ref.at[m, pl.ds(dev * 128, 128), :]

    def copies():
        local = [pltpu.make_async_copy(rows(m, me), recv_ref.at[me, m], local_sems.at[m]) for m in range(N_ROW_GRADS)]
        sends, arrivals = [], []
        for m in range(N_ROW_GRADS):
            for kk in range(1, N_DEV):
                peer, sems = jnp.bitwise_xor(me, kk), (send_sems.at[7 * m + kk - 1], recv_sems.at[7 * m + kk - 1])
                sends.append(_remote(rows(m, peer), recv_ref.at[me, m], *sems, kk))
                arrivals.append(_remote(rows(m, me), recv_ref.at[peer, m], *sems, kk))
        return local, sends, arrivals

    def start():
        local, sends, _ = copies()
        for cp in local + sends:
            cp.start()

    def finish():
        local, sends, arrivals = copies()
        for cp in arrivals:
            cp.wait_recv()
        for cp in sends:
            cp.wait_send()
        for cp in local:
            cp.wait()

    return start, finish


LOW_OWNERS = 3


def _pair_reduce_low(g_lo):
    def body(g_ref, out_ref, got, kept, send_sems, recv_sems, local_sems):
        c = lax.axis_index("c")

        def to_sibling(slab, slot):
            return _remote(g_ref.at[slab], got.at[slot], send_sems.at[slot], recv_sems.at[slot], 1)

        def keep(slab, slot):
            return pltpu.make_async_copy(g_ref.at[slab], kept.at[slot], local_sems.at[slot])

        def add(slot):
            out_ref[slot] = (kept[slot].astype(F32) + got[slot].astype(F32)).astype(out_ref.dtype)

        @pl.when(c == 0)
        def _():
            copies = [to_sibling(1, 0), keep(0, 0), keep(2, 1)]
            for cp in copies:
                cp.start()
            to_sibling(0, 0).wait_recv()
            to_sibling(2, 1).wait_recv()
            copies[0].wait_send()
            copies[1].wait()
            copies[2].wait()
            add(0)
            add(1)

        @pl.when(c == 1)
        def _():
            copies = [to_sibling(0, 0), to_sibling(2, 1), keep(1, 0)]
            for cp in copies:
                cp.start()
            to_sibling(1, 0).wait_recv()
            copies[0].wait_send()
            copies[1].wait_send()
            copies[2].wait()
            add(0)
            out_ref[1] = jnp.zeros(out_ref.shape[1:], out_ref.dtype)

    pair = (2,) + g_lo.shape[1:]
    return pl.pallas_call(
        body, name="pair_reduce_low",
        out_shape=jax.ShapeDtypeStruct(pair, g_lo.dtype),
        in_specs=[ANY], out_specs=pl.BlockSpec(memory_space=pltpu.VMEM),
        scratch_shapes=[pltpu.VMEM(pair, g_lo.dtype), pltpu.VMEM(pair, g_lo.dtype), pltpu.SemaphoreType.DMA((2,)),
                        pltpu.SemaphoreType.DMA((2,)), pltpu.SemaphoreType.DMA((2,))],
        compiler_params=_params(),
    )(g_lo)


def _scatter_low(part_ref, recv_ref, send_sems, recv_sems, local_sem):
    x, y, c = lax.axis_index("x"), lax.axis_index("y"), lax.axis_index("c")
    chip = 2 * x + y
    routes = ((0, (0, 0, c), 0, None), (1, (0, 1, 0), 1, 0))

    def each(on_send, on_local, on_arrival):
        for slot, owner, owner_chip, core in routes:
            holds = (c == core) if core is not None else (c >= 0)
            rel = jnp.bitwise_xor(chip, owner_chip)

            @pl.when(holds & (rel != 0))
            def _(slot=slot, owner=owner, rel=rel):
                on_send(pltpu.make_async_remote_copy(
                    src_ref=part_ref.at[slot], dst_ref=recv_ref.at[chip], send_sem=send_sems.at[slot],
                    recv_sem=recv_sems.at[rel - 1], device_id=owner, device_id_type=MESH))

            @pl.when(holds & (rel == 0))
            def _(slot=slot, owner=owner, owner_chip=owner_chip):
                on_local(pltpu.make_async_copy(part_ref.at[slot], recv_ref.at[chip], local_sem))
                if on_arrival is not None:
                    for r in range(1, 4):
                        on_arrival(pltpu.make_async_remote_copy(
                            src_ref=part_ref.at[slot], dst_ref=recv_ref.at[r ^ owner_chip], send_sem=send_sems.at[slot],
                            recv_sem=recv_sems.at[r - 1], device_id=owner, device_id_type=MESH))

    start = lambda: each(lambda cp: cp.start(), lambda cp: cp.start(), None)
    finish = lambda: each(lambda cp: cp.wait_send(), lambda cp: cp.wait(), lambda cp: cp.wait_recv())
    return start, finish


def _adam_update(g, w, m, v):
    mn = ADAM_B1 * m + (1.0 - ADAM_B1) * g
    vn = ADAM_B2 * v + (1.0 - ADAM_B2) * (g * g)
    m_hat = mn / (1.0 - ADAM_B1 ** ADAM_STEP)
    v_hat = vn / (1.0 - ADAM_B2 ** ADAM_STEP)
    return -ADAM_LR * (m_hat / (jnp.sqrt(v_hat) + ADAM_EPS) + ADAM_WD * w), mn, vn


def _device_sum(parts):
    t = [p.astype(F32) for p in parts]
    return ((t[0] + t[1]) + (t[2] + t[3])) + ((t[4] + t[5]) + (t[6] + t[7]))


def _finish(a_hi, a_lo, n_lo, b3, misc, big, rows3, small):
    n_steps = 4
    tb, tr3 = 1024 // n_steps, 128 // n_steps

    def body(*refs):
        it = iter(refs)
        hi_ref, lo_ref, b_ref, misc_hbm = next(it), next(it), next(it), next(it)
        big_in = [next(it) for _ in range(3)]
        rows_in = [[next(it) for _ in range(3)] for _ in rows3]
        small_in = [[next(it) for _ in range(3)] for _ in small]
        big_out = [next(it) for _ in range(4)]
        rows_out = [[next(it) for _ in range(4)] for _ in rows3]
        small_out = [[next(it) for _ in range(4)] for _ in small]
        loss_ref = next(it)
        m_vmem, send_sems, recv_sems, local_sem = next(it), next(it), next(it), next(it)
        start, finish = _scatter_slabs(misc_hbm, 0, m_vmem, send_sems, recv_sems, local_sem)
        step = pl.program_id(0)

        @pl.when(step == 0)
        def _():
            start()

        def apply(g, ins, outs):
            d, mn, vn = _adam_update(g, ins[0][...], ins[1][...], ins[2][...])
            for r, val in zip(outs, (g, d, mn, vn)):
                r[...] = val

        lo = [lo_ref[s].astype(F32) for s in range(4)]
        g_big = jnp.where(_me() < n_lo, (lo[0] + lo[1]) + (lo[2] + lo[3]), _device_sum([hi_ref[s] for s in range(N_DEV)]))
        apply(g_big[None], big_in, big_out)
        for k in range(len(rows3)):
            apply(_device_sum([b_ref[s, k] for s in range(N_DEV)])[None], rows_in[k], rows_out[k])

        @pl.when(step == n_steps - 1)
        def _():
            finish()
            loss_ref[...] = _device_sum([m_vmem[s, MISC_LOSS:MISC_LOSS + 1, :] for s in range(N_DEV)])
            for (row0, lanes, ins), r_in, r_out in zip(small, small_in, small_out):
                n = ins[0].shape[0]
                apply(_device_sum([m_vmem[s, row0:row0 + n, :lanes] for s in range(N_DEV)]), r_in, r_out)

    whole = lambda shape: pl.BlockSpec(shape, lambda i: (0,) * len(shape))
    big_blk = pl.BlockSpec((1, tb, 1024), lambda i: (0, i, 0))
    rows_blk = pl.BlockSpec((1, tr3, 1024), lambda i: (0, i, 0))
    in_specs = [pl.BlockSpec((N_DEV, tb, 1024), lambda i: (0, i, 0)), pl.BlockSpec((4, tb, 1024), lambda i: (0, i, 0)),
                pl.BlockSpec((N_DEV, 3, tr3, 1024), lambda i: (0, 0, i, 0)), ANY]
    in_specs += [big_blk] * 3 + [rows_blk] * (3 * len(rows3))
    out_specs = [big_blk] * 4 + [rows_blk] * (4 * len(rows3))
    out_shape = [jax.ShapeDtypeStruct(big[0].shape, F32)] * 4
    for w, _, _ in rows3:
        out_shape += [jax.ShapeDtypeStruct(w.shape, F32)] * 4
    args = [a_hi, a_lo, b3, misc, *big]
    for t in rows3:
        args += list(t)
    for _, _, t in small:
        in_specs += [whole(t[0].shape)] * 3
        out_specs += [whole(t[0].shape)] * 4
        out_shape += [jax.ShapeDtypeStruct(t[0].shape, F32)] * 4
        args += list(t)
    out_specs.append(whole((1, 1024)))
    out_shape.append(jax.ShapeDtypeStruct((1, 1024), F32))
    outs = pl.pallas_call(
        body, name="reduce_sum_adamw", out_shape=tuple(out_shape), grid=(n_steps,),
        in_specs=in_specs, out_specs=tuple(out_specs),
        scratch_shapes=[pltpu.VMEM(misc.shape, misc.dtype)] + list(SCATTER_SEMS),
        compiler_params=_params(("arbitrary",)),
    )(*args)
    return [tuple(outs[4 * k:4 * k + 4]) for k in range(len(outs) // 4)], outs[-1][0, 0]


def _gather_order():
    x, y, c = lax.axis_index("x"), lax.axis_index("y"), lax.axis_index("c")
    chips = [(1 - x, y), (x, 1 - y), (1 - x, 1 - y)]
    order = [_device_index(x, y, c), _device_index(x, y, 1 - c)]
    order += [_device_index(*q, c) for q in chips] + [_device_index(*q, 1 - c) for q in chips]
    return jnp.stack(order).astype(jnp.int32)


def _in_projection(zp, norm_w, w_shard, b_blocks, order):
    n_rows = zp.shape[0]
    tr = _row_tile(n_rows, 832)
    nt = n_rows // tr

    def body(order_ref, z_ref, nw_ref, w_hbm, b_ref, p_ref, h_ref, w_out, w_vmem, h_all, send_sems, recv_sems, local_sem,
             out_sems):
        s, i = pl.program_id(0), pl.program_id(1)

        x, y, c = lax.axis_index("x"), lax.axis_index("y"), lax.axis_index("c")
        me, sibling = (x, y, c), (x, y, 1 - c)
        chips = [(1 - x, y), (x, 1 - y), (1 - x, 1 - y)]

        def slot(px, py, pc):
            return w_vmem.at[_device_index(px, py, pc)]

        def copy(k, blk, to, src=None):
            return pltpu.make_async_remote_copy(
                src_ref=slot(*blk) if src is None else src, dst_ref=slot(*blk),
                send_sem=send_sems.at[k], recv_sem=recv_sems.at[k], device_id=to, device_id_type=MESH)

        own = pltpu.make_async_copy(w_hbm, slot(*me), local_sem)
        first = [copy(0, me, sibling, src=w_hbm)] + [copy(1 + j, me, (*q, c), src=w_hbm) for j, q in enumerate(chips)]
        passed = [copy(4 + j, (*q, c), sibling) for j, q in enumerate(chips)]
        arrivals = [None, copy(0, sibling, me)]
        arrivals += [copy(1 + j, (*q, c), me) for j, q in enumerate(chips)]
        arrivals += [copy(4 + j, (*q, 1 - c), me) for j, q in enumerate(chips)]

        def keep(step):
            return pltpu.make_async_copy(w_vmem.at[order_ref[step]], w_out.at[order_ref[step]], out_sems.at[step])

        for step in range(N_DEV):
            @pl.when((i == 0) & (s == step))
            def _(step=step):
                if step == 0:
                    own.start()
                    for cp in first:
                        cp.start()
                    own.wait()
                else:
                    arrivals[step].wait_recv()
                    if 2 <= step <= 4:
                        passed[step - 2].start()
                keep(step).start()

        @pl.when(s == 0)
        def _():
            z = z_ref[...]
            r = lax.rsqrt(jnp.mean(z * z, axis=-1, keepdims=True) + EPS)
            h = (z * r * nw_ref[...]).astype(BF16)
            h_all[i] = h
            h_ref[...] = h

        p_ref[0] = jnp.dot(h_all[i], w_vmem[order_ref[s]], preferred_element_type=F32) + b_ref[0]

        @pl.when((s == N_DEV - 1) & (i == nt - 1))
        def _():
            for cp in first + passed:
                cp.wait_send()
            for step in range(N_DEV):
                keep(step).wait()

    first_pass = lambda s, i, rest: jnp.where(s == 0, i, rest)
    return pl.pallas_call(
        body, name="in_projection_gather",
        out_shape=(jax.ShapeDtypeStruct((N_COLBLK, n_rows, 1024), F32),
                   jax.ShapeDtypeStruct((n_rows, D_MODEL), BF16),
                   jax.ShapeDtypeStruct((N_DEV, D_MODEL, 1024), BF16)),
        grid_spec=pltpu.PrefetchScalarGridSpec(
            num_scalar_prefetch=1, grid=(N_DEV, nt),
            in_specs=[pl.BlockSpec((tr, D_MODEL), lambda s, i, o: (first_pass(s, i, 0), 0)),
                      pl.BlockSpec((1, D_MODEL), lambda s, i, o: (0, 0)), ANY,
                      pl.BlockSpec((1, 1, 1024), lambda s, i, o: (o[s], 0, 0))],
            out_specs=(pl.BlockSpec((1, tr, 1024), lambda s, i, o: (o[s], i, 0)),
                       pl.BlockSpec((tr, D_MODEL), lambda s, i, o: (first_pass(s, i, nt - 1), 0)), ANY),
            scratch_shapes=[pltpu.VMEM((N_DEV, D_MODEL, 1024), BF16), pltpu.VMEM((nt, tr, D_MODEL), BF16),
                            pltpu.SemaphoreType.DMA((7,)), pltpu.SemaphoreType.DMA((7,)), pltpu.SemaphoreType.DMA(()),
                            pltpu.SemaphoreType.DMA((N_DEV,))]),
        compiler_params=_params(("arbitrary", "arbitrary")),
    )(order, zp, norm_w, w_shard, b_blocks)


def _lower_bound(lb_ref):
    l0, l1 = lb_ref[0:1, :], lb_ref[1:2, :]
    _, lb = _sigmoid_pair(l1 - l0)
    return lb


def _chunk_gates(fz, lb, valid):
    sig, nsig = _sigmoid_pair(fz)
    f = lb + (1.0 - lb) * sig
    g2 = jnp.where(valid, jnp.log2(f), 0.0)
    k = jnp.where(valid, (1.0 - lb) * nsig, 0.0)
    return sig, nsig, f, g2, k


def _tri(n, upper=False):
    r = lax.broadcasted_iota(jnp.int32, (n, n), 0)
    c = lax.broadcasted_iota(jnp.int32, (n, n), 1)
    return jnp.where((r <= c) if upper else (r >= c), 1.0, 0.0).astype(BF16)


def _tri_dot(tri, x):
    hi = x.astype(BF16)
    rest = x - hi.astype(F32)
    mid = rest.astype(BF16)
    low = (rest - mid.astype(F32)).astype(BF16)
    return (jnp.dot(tri, hi, preferred_element_type=F32) + jnp.dot(tri, mid, preferred_element_type=F32)
            + jnp.dot(tri, low, preferred_element_type=F32))


def _intra_scores(q_ref, k_ref, b2_ref, col0):
    cols = pl.ds(col0, HEAD_DIM)
    rows_s = lax.broadcasted_iota(jnp.int32, (SUB, 1), 0)
    lanes_c = lax.broadcasted_iota(jnp.int32, (1, CHUNK), 1)
    blocks = []
    for i in range(N_SUB):
        lo = i * SUB
        qi = q_ref[lo:lo + SUB, cols]
        bi = b2_ref[lo:lo + SUB, cols]
        if i == 0:
            acc = jnp.zeros((SUB, CHUNK), F32)
        else:
            ref_i = b2_ref[lo:lo + 1, cols]
            qt = qi * jnp.exp2(bi - ref_i)
            kt = jnp.concatenate([k_ref[0:lo, cols] * jnp.exp2(ref_i - b2_ref[0:lo, cols]),
                                  jnp.zeros((CHUNK - lo, HEAD_DIM), F32)], axis=0)
            acc = _dot_nt(qt, kt)
        for s in range(SUB):
            b_s = b2_ref[lo + s:lo + s + 1, cols]
            k_s = k_ref[lo + s:lo + s + 1, cols]
            w = jnp.exp2(jnp.minimum(bi - b_s, 0.0))
            col = jnp.sum((qi * w) * k_s, axis=-1, keepdims=True)
            acc = jnp.where(lanes_c == lo + s, col, acc)
        blocks.append(jnp.where(lanes_c <= lo + rows_s, acc, 0.0))
    return jnp.concatenate(blocks, axis=0)


def _hgrn_forward(p, lb_logits, w_rest):
    n_rows = p.shape[1]
    n_chunks = n_rows // CHUNK
    width = HEADS * HEAD_DIM

    def body(q_ref, fz_ref, v_ref, lb_ref, rest_ref, o_ref, st_out_ref, a_out_ref, rest_out,
             state, k_vmem, b2_vmem, send_sems, recv_sems, local_sem):
        n = pl.program_id(0)
        own, sends, arrivals = _direct_gather(rest_ref, rest_out, send_sems, recv_sems, local_sem)

        @pl.when(n == 0)
        def _():
            state[...] = jnp.zeros_like(state)
            own.start()
            for cp in sends:
                cp.start()

        rows = n * CHUNK + lax.broadcasted_iota(jnp.int32, (CHUNK, 1), 0)
        valid = rows >= PAD_ROWS
        lb = _lower_bound(lb_ref)
        _, _, _, g2, k = _chunk_gates(fz_ref[0], lb, valid)
        k_vmem[...] = k
        b2_vmem[...] = _tri_dot(_tri(CHUNK), g2)
        q_view = q_ref.at[0]
        for h in range(HEADS):
            cols = pl.ds(h * HEAD_DIM, HEAD_DIM)
            st = state[h]
            st_out_ref[0, h] = st
            bh = b2_vmem[:, cols]
            kh = k_vmem[:, cols]
            vh = jnp.where(valid, v_ref[0, :, cols], 0.0)
            qe = q_ref[0, :, cols] * jnp.exp2(bh)
            a = _intra_scores(q_view, k_vmem, b2_vmem, h * HEAD_DIM).astype(BF16)
            a_out_ref[0, h] = a
            o_ref[:, cols] = _dot_nt(qe, st) + _dot(a, vh)
            b_last = b2_vmem[CHUNK - 1:CHUNK, cols]
            kd = kh * jnp.exp2(b_last - bh)
            state[h] = st * jnp.exp2(b_last) + _dot_tn(vh, kd)

        @pl.when(n == n_chunks - 1)
        def _():
            for cp in arrivals:
                cp.wait_recv()
            for cp in sends:
                cp.wait_send()
            own.wait()

    blk = lambda c: pl.BlockSpec((1, CHUNK, width), lambda n, c=c: (c, n, 0))
    return pl.pallas_call(
        body, name="hgrn_forward",
        out_shape=(jax.ShapeDtypeStruct((n_rows, width), F32),
                   jax.ShapeDtypeStruct((n_chunks, HEADS, HEAD_DIM, HEAD_DIM), F32),
                   jax.ShapeDtypeStruct((n_chunks, HEADS, CHUNK, CHUNK), BF16),
                   jax.ShapeDtypeStruct((N_DEV,) + w_rest.shape, w_rest.dtype)),
        grid=(n_chunks,),
        in_specs=[blk(0), blk(1), blk(2), pl.BlockSpec((2, width), lambda n: (0, 0)), ANY],
        out_specs=(pl.BlockSpec((CHUNK, width), lambda n: (n, 0)),
                   pl.BlockSpec((1, HEADS, HEAD_DIM, HEAD_DIM), lambda n: (n, 0, 0, 0)),
                   pl.BlockSpec((1, HEADS, CHUNK, CHUNK), lambda n: (n, 0, 0, 0)), ANY),
        scratch_shapes=[pltpu.VMEM((HEADS, HEAD_DIM, HEAD_DIM), F32), pltpu.VMEM((CHUNK, width), F32),
                        pltpu.VMEM((CHUNK, width), F32)] + list(GATHER_SEMS),
        compiler_params=_params(("arbitrary",)),
    )(p, p, p, lb_logits, w_rest)


def _hgrn_backward(p, lb_logits, states, scores, d_o, g_slabs, first_owner, g_rows):
    n_rows = p.shape[1]
    n_chunks = n_rows // CHUNK
    width = HEADS * HEAD_DIM

    def body(q_ref, fz_ref, v_ref, lb_ref, st_ref, a_ref, do_ref, gs_hbm, gr_hbm, dp_ref, dbias_ref, dlb_ref, rs_hbm, rr_hbm,
             dstate, k_vmem, b2_vmem, *sems):
        step = pl.program_id(0)
        n = n_chunks - 1 - step
        start_slabs, finish_slabs = _scatter_slabs(gs_hbm, first_owner, rs_hbm, *sems[:3])
        start_rows, finish_rows = _scatter_rows(gr_hbm, rr_hbm, *sems[3:])

        @pl.when(step == 0)
        def _():
            dstate[...] = jnp.zeros_like(dstate)
            dbias_ref[...] = jnp.zeros_like(dbias_ref)
            dlb_ref[...] = jnp.zeros_like(dlb_ref)
            start_slabs()
            start_rows()

        rows = n * CHUNK + lax.broadcasted_iota(jnp.int32, (CHUNK, 1), 0)
        valid = rows >= PAD_ROWS
        lb = _lower_bound(lb_ref)
        sig, nsig, f, g2, k = _chunk_gates(fz_ref[0], lb, valid)
        k_vmem[...] = k
        b2_vmem[...] = _tri_dot(_tri(CHUNK), g2)
        rows_c = lax.broadcasted_iota(jnp.int32, (CHUNK, 1), 0)
        rows_s = lax.broadcasted_iota(jnp.int32, (SUB, 1), 0)
        lanes_c = lax.broadcasted_iota(jnp.int32, (1, CHUNK), 1)
        causal = lax.broadcasted_iota(jnp.int32, (CHUNK, CHUNK), 0) >= lax.broadcasted_iota(jnp.int32, (CHUNK, CHUNK), 1)
        tri_up = _tri(CHUNK, upper=True)
        for h in range(HEADS):
            cols = pl.ds(h * HEAD_DIM, HEAD_DIM)
            st = st_ref[0, h]
            dst = dstate[h]
            qh = q_ref[0, :, cols]
            bh = b2_vmem[:, cols]
            kh = k_vmem[:, cols]
            vh = jnp.where(valid, v_ref[0, :, cols], 0.0)
            doh = do_ref[:, cols]
            eb = jnp.exp2(bh)
            qe = qh * eb
            b_last = b2_vmem[CHUNK - 1:CHUNK, cols]
            e_last = jnp.exp2(b_last)
            decay_k = jnp.exp2(b_last - bh)
            kd = kh * decay_k
            dqe = _dot(doh, st)
            da = jnp.where(causal, _dot_nt(doh, vh), 0.0)
            dv = _dot_tn(a_ref[0, h], doh) + _dot_nt(kd, dst)
            dkd = _dot(vh, dst)
            dstate[h] = dst * e_last + _dot_tn(doh, qe)
            db_last = (jnp.sum(dst * st, axis=0, keepdims=True) * e_last
                       + jnp.sum(dkd * kd, axis=0, keepdims=True))
            dq_blocks, dk_blocks = [], []
            dk_earlier = jnp.zeros((CHUNK, HEAD_DIM), F32)
            for i in range(N_SUB):
                lo = i * SUB
                qi = q_ref[0, lo:lo + SUB, cols]
                bi = b2_vmem[lo:lo + SUB, cols]
                da_i = da[lo:lo + SUB, :]
                if i == 0:
                    dq_i = jnp.zeros((SUB, HEAD_DIM), F32)
                else:
                    ref_i = b2_vmem[lo:lo + 1, cols]
                    eq = jnp.exp2(bi - ref_i)
                    ek = jnp.exp2(ref_i - b2_vmem[0:lo, cols])
                    later = jnp.zeros((CHUNK - lo, HEAD_DIM), F32)
                    kt = jnp.concatenate([k_vmem[0:lo, cols] * ek, later], axis=0)
                    dq_i = _dot(da_i, kt) * eq
                    dk_earlier = dk_earlier + jnp.concatenate([_dot_tn(da_i, qi * eq)[0:lo] * ek, later], axis=0)
                dk_i = jnp.zeros((SUB, HEAD_DIM), F32)
                for s in range(SUB):
                    b_s = b2_vmem[lo + s:lo + s + 1, cols]
                    k_s = k_vmem[lo + s:lo + s + 1, cols]
                    w = jnp.exp2(jnp.minimum(bi - b_s, 0.0))
                    da_col = jnp.sum(jnp.where(lanes_c == lo + s, da_i, 0.0), axis=-1, keepdims=True)
                    gw = da_col * w
                    dq_i = dq_i + gw * k_s
                    dk_i = jnp.where(rows_s == s, jnp.sum(gw * qi, axis=0, keepdims=True), dk_i)
                dq_blocks.append(dq_i)
                dk_blocks.append(dk_i)
            dq_intra = jnp.concatenate(dq_blocks, axis=0)
            dk_intra = jnp.concatenate(dk_blocks, axis=0) + dk_earlier
            dq = dqe * eb + dq_intra
            dk = dkd * decay_k + dk_intra
            db = dqe * qe - dkd * kd + qh * dq_intra - kh * dk_intra
            db = db + jnp.where(rows_c == CHUNK - 1, db_last, 0.0)
            dg = _tri_dot(tri_up, db)
            fh = f[:, h * HEAD_DIM:(h + 1) * HEAD_DIM]
            sh = sig[:, h * HEAD_DIM:(h + 1) * HEAD_DIM]
            nh = nsig[:, h * HEAD_DIM:(h + 1) * HEAD_DIM]
            lbh = lb[:, h * HEAD_DIM:(h + 1) * HEAD_DIM]
            df = jnp.where(valid, dg / fh - dk, 0.0)
            dfz = df * (1.0 - lbh) * sh * nh
            dq = jnp.where(valid, dq, 0.0)
            dv = jnp.where(valid, dv, 0.0)
            dlb_ref[:, cols] += jnp.sum(df * nh, axis=0, keepdims=True)
            dp_ref[0, :, cols] = dq.astype(BF16)
            dp_ref[1, :, cols] = dfz.astype(BF16)
            dp_ref[2, :, cols] = dv.astype(BF16)
            dbias_ref[0, :, cols] += jnp.sum(dq, axis=0, keepdims=True)
            dbias_ref[1, :, cols] += jnp.sum(dfz, axis=0, keepdims=True)
            dbias_ref[2, :, cols] += jnp.sum(dv, axis=0, keepdims=True)

        @pl.when(step == n_chunks - 1)
        def _():
            finish_slabs()
            finish_rows()

    rev = lambda s: n_chunks - 1 - s
    blk = lambda c: pl.BlockSpec((1, CHUNK, width), lambda s, c=c: (c, rev(s), 0))
    return pl.pallas_call(
        body, name="hgrn_backward",
        out_shape=(jax.ShapeDtypeStruct((3, n_rows, width), BF16),
                   jax.ShapeDtypeStruct((3, 1, width), F32),
                   jax.ShapeDtypeStruct((1, width), F32),
                   jax.ShapeDtypeStruct((N_DEV,) + g_slabs.shape[1:], g_slabs.dtype),
                   jax.ShapeDtypeStruct((N_DEV, N_ROW_GRADS, 128, g_rows.shape[2]), g_rows.dtype)),
        grid=(n_chunks,),
        in_specs=[blk(0), blk(1), blk(2), pl.BlockSpec((2, width), lambda s: (0, 0)),
                  pl.BlockSpec((1, HEADS, HEAD_DIM, HEAD_DIM), lambda s: (rev(s), 0, 0, 0)),
                  pl.BlockSpec((1, HEADS, CHUNK, CHUNK), lambda s: (rev(s), 0, 0, 0)),
                  pl.BlockSpec((CHUNK, width), lambda s: (rev(s), 0)), ANY, ANY],
        out_specs=(pl.BlockSpec((3, CHUNK, width), lambda s: (0, rev(s), 0)),
                   pl.BlockSpec((3, 1, width), lambda s: (0, 0, 0)),
                   pl.BlockSpec((1, width), lambda s: (0, 0)), ANY, ANY),
        scratch_shapes=[pltpu.VMEM((HEADS, HEAD_DIM, HEAD_DIM), F32), pltpu.VMEM((CHUNK, width), F32),
                        pltpu.VMEM((CHUNK, width), F32)] + list(SCATTER_SEMS) + list(SCATTER_ROWS_SEMS),
        compiler_params=_params(("arbitrary",)),
    )(p, p, p, lb_logits, states, scores, d_o, g_slabs, g_rows)


def _sigmoid_and_complement(x):
    s = 0.5 * jnp.tanh(0.5 * x) + 0.5
    return s, 1.0 - s


def _silu_and_grad(x):
    s, ns = _sigmoid_and_complement(x)
    return x * s, s * (1.0 + x * ns)


def _tail(p, o, zp, tgt, hg_norm_w, pool_w, pool_scale, w_down_hg, w_down_pool, w_out, final_norm_w):
    n_rows = zp.shape[0]
    tr = _row_tile(n_rows, 208)
    nt = n_rows // tr
    ext = tr + HALO
    n_groups = len(POOL_WINDOWS)

    def body(o_ref, ghg_ref, u_ref, gpool_ref, mhg_ref, mpool_ref, uhalo_ref, z_ref, tgt_ref,
             hgw_ref, pw_ref, ps_ref, wdh_ref, wdp_ref, wout_ref, fnw_ref,
             do_ref, dp_ref, dz2_ref, lhs_ref, rhs_ref,
             dbias_ref, dhgw_ref, dpw_ref, dps_ref, dfnw_ref, loss_ref, halo_vmem):
        step = pl.program_id(0)
        ti = nt - 1 - step

        @pl.when(step == 0)
        def _():
            halo_vmem[...] = jnp.zeros_like(halo_vmem)
            for r in (dbias_ref, dhgw_ref, dpw_ref, dps_ref, dfnw_ref, loss_ref):
                r[...] = jnp.zeros_like(r)

        rows = ti * tr + lax.broadcasted_iota(jnp.int32, (tr, 1), 0)
        valid = rows >= PAD_ROWS
        in_loss = rows >= CHUNK
        count_pos = jnp.maximum(rows - PAD_ROWS + 1, 1).astype(F32)

        o = o_ref[...]
        hgw = hgw_ref[...]
        inv_o, on_parts = [], []
        for h in range(HEADS):
            oh = o[:, h * HEAD_DIM:(h + 1) * HEAD_DIM]
            r = lax.rsqrt(jnp.mean(oh * oh, axis=-1, keepdims=True) + EPS)
            inv_o.append(r)
            on_parts.append(oh * r)
        o_hat = jnp.concatenate(on_parts, axis=1)
        o_n = o_hat * hgw
        g_hg = ghg_ref[0]
        silu_hg, dsilu_hg = _silu_and_grad(g_hg)
        a_hg = o_n * silu_hg
        y_hg = _dot(a_hg, wdh_ref[...])

        u = jnp.where(valid, u_ref[0], 0.0)
        u_prev = jnp.where(ti > 0, uhalo_ref[0], 0.0)
        u_ext = jnp.concatenate([u_prev, u], axis=0)
        pooled_parts, mixed_parts, inv_cnt = [], [], []
        for gi, win in enumerate(POOL_WINDOWS):
            lanes = slice(gi * POOL_GDIM, (gi + 1) * POOL_GDIM)
            s = u_ext[:, lanes]
            shift = 1
            while shift < win:
                s = s + pltpu.roll(s, shift, 0)
                shift *= 2
            ic = 1.0 / jnp.minimum(count_pos, float(win))
            inv_cnt.append(ic)
            pooled = s[HALO:] * ic - u[:, lanes]
            pooled_parts.append(pooled)
            mixed_parts.append(_dot(pooled, pw_ref[gi]))
        mixed = jnp.concatenate(mixed_parts, axis=1)
        ps = ps_ref[...]
        g_pool = gpool_ref[0]
        silu_pool, dsilu_pool = _silu_and_grad(g_pool)
        a_pool = mixed * ps * silu_pool
        y_pool = _dot(a_pool, wdp_ref[...])

        m_hg, m_pool = mhg_ref[0], mpool_ref[0]
        s_hg, ns_hg = _sigmoid_and_complement(m_hg)
        s_pool, ns_pool = _sigmoid_and_complement(m_pool)
        merged = s_hg * y_hg + s_pool * y_pool
        z2 = z_ref[...] + _dot(merged, wout_ref[...])
        r2 = lax.rsqrt(jnp.mean(z2 * z2, axis=-1, keepdims=True) + EPS)
        n2 = z2 * r2
        fnw = fnw_ref[...]
        err = jnp.where(in_loss, n2 * fnw - tgt_ref[...], 0.0)
        loss_ref[...] += jnp.sum(jnp.sum(err * err, axis=0, keepdims=True), axis=1, keepdims=True) * (0.5 / D_MODEL)
        dy = err * (1.0 / D_MODEL)

        dfnw_ref[...] += jnp.sum(dy * n2, axis=0, keepdims=True)
        gy = dy * fnw
        dz2 = r2 * (gy - n2 * jnp.mean(gy * n2, axis=-1, keepdims=True))
        dmerged = _dot_nt(dz2, wout_ref[...])
        dy_hg = s_hg * dmerged
        dy_pool = s_pool * dmerged
        dm_hg = dmerged * y_hg * s_hg * ns_hg
        dm_pool = dmerged * y_pool * s_pool * ns_pool
        da_hg = _dot_nt(dy_hg, wdh_ref[...])
        da_pool = _dot_nt(dy_pool, wdp_ref[...])

        d_on = da_hg * silu_hg
        dg_hg = da_hg * o_n * dsilu_hg
        dhgw_ref[...] += jnp.sum(d_on * o_hat, axis=0, keepdims=True)
        gyo = d_on * hgw
        do_parts = []
        for h in range(HEADS):
            lanes = slice(h * HEAD_DIM, (h + 1) * HEAD_DIM)
            gh, nh = gyo[:, lanes], o_hat[:, lanes]
            do_parts.append(inv_o[h] * (gh - nh * jnp.mean(gh * nh, axis=-1, keepdims=True)))
        do_ref[...] = jnp.concatenate(do_parts, axis=1)

        dmixed = da_pool * ps * silu_pool
        dps_ref[...] += jnp.sum(da_pool * mixed * silu_pool, axis=0, keepdims=True)
        dg_pool = da_pool * mixed * ps * dsilu_pool
        du_parts = []
        for gi, win in enumerate(POOL_WINDOWS):
            lanes = slice(gi * POOL_GDIM, (gi + 1) * POOL_GDIM)
            dmx = dmixed[:, lanes]
            dpooled = _dot_nt(dmx, pw_ref[gi])
            dpw_ref[gi] += _dot_tn(pooled_parts[gi], dmx)
            dpt = dpooled * inv_cnt[gi]
            s = jnp.concatenate([dpt, halo_vmem[:, lanes]], axis=0)
            shift = 1
            while shift < win:
                s = s + pltpu.roll(s, ext - shift, 0)
                shift *= 2
            du_parts.append(s[:tr] - dpooled)
            halo_vmem[:, lanes] = dpt[:HALO]
        du = jnp.where(valid, jnp.concatenate(du_parts, axis=1), 0.0)

        for c, val in enumerate((dg_hg, du, dg_pool, dm_hg, dm_pool)):
            dp_ref[c] = val.astype(BF16)
            dbias_ref[c] += jnp.sum(val, axis=0, keepdims=True)
        dz2_ref[...] = dz2
        for c, (lhs, rhs) in enumerate(((merged, dz2), (a_hg, dy_hg), (a_pool, dy_pool))):
            lhs_ref[c] = lhs.astype(BF16)
            rhs_ref[c] = rhs.astype(BF16)

    rev = lambda s: nt - 1 - s
    rowblk = pl.BlockSpec((tr, D_MODEL), lambda s: (rev(s), 0))
    pblk = lambda c: pl.BlockSpec((1, tr, 1024), lambda s, c=c: (c, rev(s), 0))
    halo_blk = pl.BlockSpec((1, HALO, 1024), lambda s: (4, jnp.maximum(rev(s) * (tr // HALO) - 1, 0), 0))
    full = lambda shape: pl.BlockSpec(shape, lambda s: (0,) * len(shape))
    vec = full((1, D_MODEL))
    mat = full((D_MODEL, D_MODEL))
    act3 = jax.ShapeDtypeStruct((3, n_rows, D_MODEL), BF16)
    act3_blk = pl.BlockSpec((3, tr, D_MODEL), lambda s: (0, rev(s), 0))
    return pl.pallas_call(
        body, name="tail_forward_backward",
        out_shape=(jax.ShapeDtypeStruct((n_rows, D_MODEL), F32),
                   jax.ShapeDtypeStruct((5, n_rows, 1024), BF16),
                   jax.ShapeDtypeStruct((n_rows, D_MODEL), F32),
                   act3, act3,
                   jax.ShapeDtypeStruct((5, 1, 1024), F32),
                   jax.ShapeDtypeStruct((1, D_MODEL), F32),
                   jax.ShapeDtypeStruct((n_groups, POOL_GDIM, POOL_GDIM), F32),
                   jax.ShapeDtypeStruct((1, D_MODEL), F32),
                   jax.ShapeDtypeStruct((1, D_MODEL), F32),
                   jax.ShapeDtypeStruct((1, 1), F32)),
        grid=(nt,),
        in_specs=[rowblk, pblk(3), pblk(4), pblk(5), pblk(6), pblk(7), halo_blk, rowblk, rowblk,
                  vec, full((n_groups, POOL_GDIM, POOL_GDIM)), vec, mat, mat, mat, vec],
        out_specs=(rowblk, pl.BlockSpec((5, tr, 1024), lambda s: (0, rev(s), 0)), rowblk,
                   act3_blk, act3_blk,
                   full((5, 1, 1024)), vec, full((n_groups, POOL_GDIM, POOL_GDIM)), vec, vec, full((1, 1))),
        scratch_shapes=[pltpu.VMEM((HALO, D_MODEL), F32)],
        compiler_params=_params(("arbitrary",)),
    )(o, p, p, p, p, p, p, zp, tgt, hg_norm_w, pool_w, pool_scale, w_down_hg, w_down_pool, w_out, final_norm_w)


def _in_projection_backward(dp_a, dp_b, w_blocks, zp, dz2, norm_w, chip_sums):
    n_rows = zp.shape[0]
    tr = _row_tile(n_rows, 416)
    nt = n_rows // tr
    na, nb = dp_a.shape[0], dp_b.shape[0]

    def body(dpa_ref, dpb_ref, w_hbm, z_ref, dz2_ref, nw_ref, gs_hbm, dz_ref, dnw_ref, rs_hbm, w_vmem, sem, *sems):
        i = pl.program_id(0)
        start_slabs, finish_slabs = _scatter_low(gs_hbm, rs_hbm, *sems)

        @pl.when(i == 0)
        def _():
            start_slabs()
            cp = pltpu.make_async_copy(w_hbm, w_vmem, sem)
            cp.start()
            cp.wait()
            dnw_ref[...] = jnp.zeros_like(dnw_ref)

        dh = jnp.zeros((tr, D_MODEL), F32)
        for j in range(na):
            dh = dh + _dot_nt(dpa_ref[j], w_vmem[j])
        for j in range(nb):
            dh = dh + _dot_nt(dpb_ref[j], w_vmem[na + j])
        z = z_ref[...]
        r = lax.rsqrt(jnp.mean(z * z, axis=-1, keepdims=True) + EPS)
        n1 = z * r
        dnw_ref[...] += jnp.sum(dh * n1, axis=0, keepdims=True)
        gh = dh * nw_ref[...]
        dz_ref[...] = dz2_ref[...] + r * (gh - n1 * jnp.mean(gh * n1, axis=-1, keepdims=True))

        @pl.when(i == nt - 1)
        def _():
            finish_slabs()

    rowblk = pl.BlockSpec((tr, D_MODEL), lambda i: (i, 0))
    vec = pl.BlockSpec((1, D_MODEL), lambda i: (0, 0))
    return pl.pallas_call(
        body, name="in_projection_backward",
        out_shape=(jax.ShapeDtypeStruct((n_rows, D_MODEL), F32), jax.ShapeDtypeStruct((1, D_MODEL), F32),
                   jax.ShapeDtypeStruct((4,) + chip_sums.shape[1:], chip_sums.dtype)),
        grid=(nt,),
        in_specs=[pl.BlockSpec((na, tr, 1024), lambda i: (0, i, 0)), pl.BlockSpec((nb, tr, 1024), lambda i: (0, i, 0)),
                  ANY, rowblk, rowblk, vec, ANY],
        out_specs=(rowblk, vec, ANY),
        scratch_shapes=[pltpu.VMEM((N_COLBLK, D_MODEL, 1024), BF16), pltpu.SemaphoreType.DMA(()),
                        pltpu.SemaphoreType.DMA((2,)), pltpu.SemaphoreType.DMA((3,)), pltpu.SemaphoreType.DMA(())],
        compiler_params=_params(("arbitrary",)),
    )(dp_a, dp_b, w_blocks, zp, dz2, norm_w, chip_sums)


def _weight_grad(xs, ys, name):
    shared = xs.ndim == 2
    n_rows, m = xs.shape[-2:]
    nb, _, n = ys.shape
    tk = _row_tile(n_rows, 4160)
    n_k = n_rows // tk

    def body(x_ref, y_ref, o_ref, acc):
        k = pl.program_id(1)

        @pl.when(k == 0)
        def _():
            acc[...] = jnp.zeros_like(acc)

        acc[...] += _dot_tn(x_ref[...] if shared else x_ref[0], y_ref[0])

        @pl.when(k == n_k - 1)
        def _():
            o_ref[0] = acc[...].astype(o_ref.dtype)

    x_spec = pl.BlockSpec((tk, m), lambda j, k: (k, 0)) if shared else pl.BlockSpec((1, tk, m), lambda j, k: (j, k, 0))
    return pl.pallas_call(
        body, name=name,
        out_shape=jax.ShapeDtypeStruct((nb, m, n), BF16),
        grid=(nb, n_k),
        in_specs=[x_spec, pl.BlockSpec((1, tk, n), lambda j, k: (j, k, 0))],
        out_specs=pl.BlockSpec((1, m, n), lambda j, k: (j, 0, 0)),
        scratch_shapes=[pltpu.VMEM((m, n), F32)],
        compiler_params=_params(("arbitrary", "arbitrary")),
    )(xs, ys)


def kernel(x, meta_tokens, norm_w, w_in, b_in, lb_logits, hg_norm_w, pool_w, pool_scale, w_down_hg, w_down_pool, w_out, final_norm_w, loss_target, m_meta_tokens, m_norm_w, m_w_in, m_b_in, m_lb_logits, m_hg_norm_w, m_pool_w, m_pool_scale, m_w_down_hg, m_w_down_pool, m_w_out, m_final_norm_w, v_meta_tokens, v_norm_w, v_w_in, v_b_in, v_lb_logits, v_hg_norm_w, v_pool_w, v_pool_scale, v_w_down_hg, v_w_down_pool, v_w_out, v_final_norm_w):
    seq = x.shape[1]

    meta_full = _all_gather_small(meta_tokens).transpose(1, 0, 2).reshape(N_META, D_MODEL)
    w_rest = jnp.concatenate([w_down_hg[0].astype(BF16), w_down_pool[0].astype(BF16), w_out[0].astype(BF16),
                              pool_w[0].astype(BF16).reshape(32, 1024)], axis=0)

    zp = jnp.concatenate([jnp.zeros((PAD_ROWS, D_MODEL), F32), meta_full, x[0]], axis=0)
    tgt = jnp.concatenate([jnp.zeros((CHUNK, D_MODEL), F32), loss_target[0]], axis=0)
    p, h, w_blocks = _in_projection(zp, norm_w, w_in[0].astype(BF16), b_in.reshape(N_COLBLK, 1, 1024), _gather_order())
    o, states, scores, rest = _hgrn_forward(p, lb_logits, w_rest)
    wdh = rest[:, REST_W_DOWN_HG:REST_W_DOWN_HG + 128].reshape(1024, 1024)
    wdp = rest[:, REST_W_DOWN_POOL:REST_W_DOWN_POOL + 128].reshape(1024, 1024)
    wout = rest[:, REST_W_OUT:REST_W_OUT + 128].reshape(1024, 1024)
    pw = rest[:, REST_POOL_W:REST_POOL_W + 32].reshape(N_DEV, 4, 32, 256).transpose(1, 0, 2, 3).reshape(4, 256, 256)
    (d_o, dp_b, dz2, grad_lhs, grad_rhs, dbias_b, d_hgw, d_pw, d_ps, d_fnw, loss_part) = _tail(
        p, o, zp, tgt, hg_norm_w, pw, pool_scale, wdh, wdp, wout, final_norm_w.reshape(1, D_MODEL))
    n_a = N_COLBLK - dp_b.shape[0]
    g_hi = _weight_grad(h, dp_b, "weight_grad_in_hi")
    g_rows = _weight_grad(grad_lhs, grad_rhs, "weight_grad_rows")
    dp_a, dbias_a, d_lb, recv_hi, recv_rows = _hgrn_backward(p, lb_logits, states, scores, d_o, g_hi, n_a, g_rows)
    assert n_a == LOW_OWNERS
    g_lo = _weight_grad(h, dp_a, "weight_grad_in_lo")
    dz, d_nw, recv_lo = _in_projection_backward(dp_a, dp_b, w_blocks, zp, dz2, norm_w, _pair_reduce_low(g_lo))

    lb = jax.nn.sigmoid(lb_logits[0:1] - lb_logits[1:2])
    d_l0 = d_lb * lb * (1.0 - lb)
    replicated = jnp.concatenate([dbias_a.reshape(3, 1024), dbias_b.reshape(5, 1024), d_nw, d_l0, -d_l0, d_hgw, d_ps, d_fnw,
                                  jnp.pad(loss_part, ((0, MISC_ROWS - MISC_LOSS - 1), (0, 1023)))], axis=0)
    d_meta = dz[PAD_ROWS:CHUNK].reshape(N_META, N_DEV, 128).transpose(1, 0, 2)
    d_pw_blocks = d_pw.reshape(4, N_DEV, 32, 256).transpose(1, 0, 2, 3).reshape(N_DEV, 32, 1024)
    g_misc = jnp.concatenate([d_pw_blocks, jnp.pad(d_meta, ((0, 0), (0, 0), (0, 1024 - 128))),
                              jnp.broadcast_to(replicated[None], (N_DEV, 16, 1024))], axis=1)

    as_rows = lambda t, n: t.reshape(n, 1024)
    small = [(MISC_POOL_W, 1024, tuple(as_rows(t, 32) for t in (pool_w, m_pool_w, v_pool_w))),
             (MISC_META, 128, (meta_tokens, m_meta_tokens, v_meta_tokens)),
             (MISC_B_IN, 1024, tuple(as_rows(t, 8) for t in (b_in, m_b_in, v_b_in))),
             (MISC_NORM_W, 1024, (norm_w, m_norm_w, v_norm_w)),
             (MISC_LB, 1024, (lb_logits, m_lb_logits, v_lb_logits)),
             (MISC_HG_NORM_W, 1024, (hg_norm_w, m_hg_norm_w, v_hg_norm_w)),
             (MISC_POOL_SCALE, 1024, (pool_scale, m_pool_scale, v_pool_scale)),
             (MISC_FINAL_NORM_W, 1024, tuple(as_rows(t, 1) for t in (final_norm_w, m_final_norm_w, v_final_norm_w)))]
    res, loss = _finish(recv_hi, recv_lo, n_a, recv_rows, g_misc, (w_in, m_w_in, v_w_in),
                  [(w_out, m_w_out, v_w_out), (w_down_hg, m_w_down_hg, v_w_down_hg), (w_down_pool, m_w_down_pool, v_w_down_pool)],
                  small)
    r_w_in, r_w_out, r_wdh, r_wdp, r_pw, r_meta, r_b_in, r_nw, r_lb, r_hgw, r_ps, r_fnw = res
    grad_x = dz[CHUNK:].reshape(1, seq, D_MODEL)
    per_kind = [(r_meta[k], r_nw[k], r_w_in[k], r_b_in[k].reshape(1, 8192), r_lb[k], r_hgw[k], r_pw[k].reshape(1, 4, 32, 256),
                 r_ps[k], r_wdh[k], r_wdp[k], r_w_out[k], r_fnw[k].reshape(1024)) for k in range(4)]
    return (loss, grad_x, *per_kind[0], *per_kind[1], *per_kind[2], *per_kind[3])
```

```python
import functools

import jax
import jax.numpy as jnp
from jax import lax
from jax.experimental import pallas as pl
from jax.experimental.pallas import tpu as pltpu

F32 = jnp.float32
BF16 = jnp.bfloat16

D_MODEL = 1024
N_META = 16
HEADS = 8
HEAD_DIM = 128
CHUNK = 64
SUB = 8
N_SUB = CHUNK // SUB
PAD_ROWS = CHUNK - N_META
POOL_WINDOWS = (2, 4, 8, 16)
POOL_GDIM = D_MODEL // len(POOL_WINDOWS)
HALO = 16
EPS = 1e-6
N_DEV = 8
N_COLBLK = 8
ADAM_LR, ADAM_B1, ADAM_B2, ADAM_EPS, ADAM_WD, ADAM_STEP = 0.001, 0.9, 0.999, 1e-08, 0.01, 10

VMEM_LIMIT = 56 * 1024 * 1024
MESH = pl.DeviceIdType.MESH
ANY = pl.BlockSpec(memory_space=pl.ANY)

REST_W_DOWN_HG = 0
REST_W_DOWN_POOL = 128
REST_W_OUT = 256
REST_POOL_W = 384
MISC_POOL_W = 0
MISC_META = 32
MISC_B_IN = 48
MISC_NORM_W = 56
MISC_LB = 57
MISC_HG_NORM_W = 59
MISC_POOL_SCALE = 60
MISC_FINAL_NORM_W = 61
MISC_LOSS = 62
MISC_ROWS = 64


def _params(sem=None):
    return pltpu.CompilerParams(dimension_semantics=sem, vmem_limit_bytes=VMEM_LIMIT)


def _row_tile(n_rows, prefer):
    best = 16
    for t in range(16, prefer + 1, 16):
        if n_rows % t == 0:
            best = t
    return best


def _sigmoid_pair(x):
    e = jnp.exp(-jnp.abs(x))
    r = 1.0 / (1.0 + e)
    er = e * r
    pos = x >= 0
    return jnp.where(pos, r, er), jnp.where(pos, er, r)


def _dot(a, b):
    return jnp.dot(a.astype(BF16), b.astype(BF16), preferred_element_type=F32)


def _dot_nt(a, b):
    return lax.dot_general(a.astype(BF16), b.astype(BF16), (((1,), (1,)), ((), ())), preferred_element_type=F32)


def _dot_tn(a, b):
    return lax.dot_general(a.astype(BF16), b.astype(BF16), (((0,), (0,)), ((), ())), preferred_element_type=F32)


def _device_index(px, py, pc):
    return 4 * px + 2 * py + pc


def _direct_gather(src_ref, dst_ref, send_sems, recv_sems, local_sem):
    x, y, c = lax.axis_index("x"), lax.axis_index("y"), lax.axis_index("c")
    own = pltpu.make_async_copy(src_ref, dst_ref.at[_device_index(x, y, c)], local_sem)
    sends, arrivals = [], []
    for k in range(1, N_DEV):
        peer = (1 - x if k & 4 else x, 1 - y if k & 2 else y, 1 - c if k & 1 else c)
        for slot, out in ((_device_index(x, y, c), sends), (_device_index(*peer), arrivals)):
            out.append(pltpu.make_async_remote_copy(
                src_ref=src_ref, dst_ref=dst_ref.at[slot], send_sem=send_sems.at[k - 1], recv_sem=recv_sems.at[k - 1],
                device_id=peer, device_id_type=MESH))
    return own, sends, arrivals


GATHER_SEMS = [pltpu.SemaphoreType.DMA((N_DEV - 1,)), pltpu.SemaphoreType.DMA((N_DEV - 1,)), pltpu.SemaphoreType.DMA(())]


def _all_gather_small(block):
    def body(x_ref, out_ref, send_sems, recv_sems, local_sem):
        own, sends, arrivals = _direct_gather(x_ref, out_ref, send_sems, recv_sems, local_sem)
        own.start()
        for cp in sends:
            cp.start()
        for cp in arrivals:
            cp.wait_recv()
        for cp in sends:
            cp.wait_send()
        own.wait()

    return pl.pallas_call(
        body, name="all_gather_meta",
        out_shape=jax.ShapeDtypeStruct((N_DEV,) + block.shape, block.dtype),
        in_specs=[ANY], out_specs=ANY, scratch_shapes=list(GATHER_SEMS),
    )(block)


def _peer(k):
    x, y, c = lax.axis_index("x"), lax.axis_index("y"), lax.axis_index("c")
    return (1 - x if k & 4 else x, 1 - y if k & 2 else y, 1 - c if k & 1 else c)


def _me():
    return _device_index(lax.axis_index("x"), lax.axis_index("y"), lax.axis_index("c"))


def _remote(src, dst, send_sem, recv_sem, peer_bits):
    return pltpu.make_async_remote_copy(src_ref=src, dst_ref=dst, send_sem=send_sem, recv_sem=recv_sem,
                                        device_id=_peer(peer_bits), device_id_type=MESH)


N_ROW_GRADS = 3
SCATTER_SEMS = [pltpu.SemaphoreType.DMA((N_DEV - 1,)), pltpu.SemaphoreType.DMA((N_DEV - 1,)), pltpu.SemaphoreType.DMA(())]
SCATTER_ROWS_SEMS = [pltpu.SemaphoreType.DMA((7 * N_ROW_GRADS,)), pltpu.SemaphoreType.DMA((7 * N_ROW_GRADS,)),
                     pltpu.SemaphoreType.DMA((N_ROW_GRADS,))]


def _scatter_slabs(g_ref, first, recv_ref, send_sems, recv_sems, local_sem):
    n = g_ref.shape[0]
    me = _me()

    def each(on_send, on_local, on_arrival):
        for kk in range(1, N_DEV):
            peer = jnp.bitwise_xor(me, kk)

            @pl.when((peer >= first) & (peer < first + n))
            def _(kk=kk, peer=peer):
                on_send(_remote(g_ref.at[peer - first], recv_ref.at[me], send_sems.at[kk - 1], recv_sems.at[kk - 1], kk))

        @pl.when((me >= first) & (me < first + n))
        def _():
            on_local(pltpu.make_async_copy(g_ref.at[me - first], recv_ref.at[me], local_sem))
            if on_arrival is not None:
                for kk in range(1, N_DEV):
                    on_arrival(_remote(g_ref.at[0], recv_ref.at[jnp.bitwise_xor(me, kk)], send_sems.at[kk - 1],
                                       recv_sems.at[kk - 1], kk))

    start = lambda: each(lambda cp: cp.start(), lambda cp: cp.start(), None)
    finish = lambda: each(lambda cp: cp.wait_send(), lambda cp: cp.wait(), lambda cp: cp.wait_recv())
    return start, finish


def _scatter_rows(g_ref, recv_ref, send_sems, recv_sems, local_sems):
    me = _me()
    rows = lambda m, dev: g_ref.at[m, pl.ds(dev * 128, 128), :]

    def copies():
        local = [pltpu.make_async_copy(rows(m, me), recv_ref.at[me, m], local_sems.at[m]) for m in range(N_ROW_GRADS)]
        sends, arrivals = [], []
        for m in range(N_ROW_GRADS):
            for kk in range(1, N_DEV):
                peer, sems = jnp.bitwise_xor(me, kk), (send_sems.at[7 * m + kk - 1], recv_sems.at[7 * m + kk - 1])
                sends.append(_remote(rows(m, peer), recv_ref.at[me, m], *sems, kk))
                arrivals.append(_remote(rows(m, me), recv_ref.at[peer, m], *sems, kk))
        return local, sends, arrivals

    def start():
        local, sends, _ = copies()
        for cp in local + sends:
            cp.start()

    def finish():
        local, sends, arrivals = copies()
        for cp in arrivals:
            cp.wait_recv()
        for cp in sends:
            cp.wait_send()
        for cp in local:
            cp.wait()

    return start, finish


LOW_OWNERS = 3


def _pair_reduce_low(g_lo):
    def body(g_ref, out_ref, got, kept, send_sems, recv_sems, local_sems):
        c = lax.axis_index("c")

        def to_sibling(slab, slot):
            return _remote(g_ref.at[slab], got.at[slot], send_sems.at[slot], recv_sems.at[slot], 1)

        def keep(slab, slot):
            return pltpu.make_async_copy(g_ref.at[slab], kept.at[slot], local_sems.at[slot])

        def add(slot):
            out_ref[slot] = (kept[slot].astype(F32) + got[slot].astype(F32)).astype(out_ref.dtype)

        @pl.when(c == 0)
        def _():
            copies = [to_sibling(1, 0), keep(0, 0), keep(2, 1)]
            for cp in copies:
                cp.start()
            to_sibling(0, 0).wait_recv()
            to_sibling(2, 1).wait_recv()
            copies[0].wait_send()
            copies[1].wait()
            copies[2].wait()
            add(0)
            add(1)

        @pl.when(c == 1)
        def _():
            copies = [to_sibling(0, 0), to_sibling(2, 1), keep(1, 0)]
            for cp in copies:
                cp.start()
            to_sibling(1, 0).wait_recv()
            copies[0].wait_send()
            copies[1].wait_send()
            copies[2].wait()
            add(0)
            out_ref[1] = jnp.zeros(out_ref.shape[1:], out_ref.dtype)

    pair = (2,) + g_lo.shape[1:]
    return pl.pallas_call(
        body, name="pair_reduce_low",
        out_shape=jax.ShapeDtypeStruct(pair, g_lo.dtype),
        in_specs=[ANY], out_specs=pl.BlockSpec(memory_space=pltpu.VMEM),
        scratch_shapes=[pltpu.VMEM(pair, g_lo.dtype), pltpu.VMEM(pair, g_lo.dtype), pltpu.SemaphoreType.DMA((2,)),
                        pltpu.SemaphoreType.DMA((2,)), pltpu.SemaphoreType.DMA((2,))],
        compiler_params=_params(),
    )(g_lo)


def _scatter_low(part_ref, recv_ref, send_sems, recv_sems, local_sem):
    x, y, c = lax.axis_index("x"), lax.axis_index("y"), lax.axis_index("c")
    chip = 2 * x + y
    routes = ((0, (0, 0, c), 0, None), (1, (0, 1, 0), 1, 0))

    def each(on_send, on_local, on_arrival):
        for slot, owner, owner_chip, core in routes:
            holds = (c == core) if core is not None else (c >= 0)
            rel = jnp.bitwise_xor(chip, owner_chip)

            @pl.when(holds & (rel != 0))
            def _(slot=slot, owner=owner, rel=rel):
                on_send(pltpu.make_async_remote_copy(
                    src_ref=part_ref.at[slot], dst_ref=recv_ref.at[chip], send_sem=send_sems.at[slot],
                    recv_sem=recv_sems.at[rel - 1], device_id=owner, device_id_type=MESH))

            @pl.when(holds & (rel == 0))
            def _(slot=slot, owner=owner, owner_chip=owner_chip):
                on_local(pltpu.make_async_copy(part_ref.at[slot], recv_ref.at[chip], local_sem))
                if on_arrival is not None:
                    for r in range(1, 4):
                        on_arrival(pltpu.make_async_remote_copy(
                            src_ref=part_ref.at[slot], dst_ref=recv_ref.at[r ^ owner_chip], send_sem=send_sems.at[slot],
                            recv_sem=recv_sems.at[r - 1], device_id=owner, device_id_type=MESH))

    start = lambda: each(lambda cp: cp.start(), lambda cp: cp.start(), None)
    finish = lambda: each(lambda cp: cp.wait_send(), lambda cp: cp.wait(), lambda cp: cp.wait_recv())
    return start, finish


def _adam_update(g, w, m, v):
    mn = ADAM_B1 * m + (1.0 - ADAM_B1) * g
    vn = ADAM_B2 * v + (1.0 - ADAM_B2) * (g * g)
    m_hat = mn / (1.0 - ADAM_B1 ** ADAM_STEP)
    v_hat = vn / (1.0 - ADAM_B2 ** ADAM_STEP)
    return -ADAM_LR * (m_hat / (jnp.sqrt(v_hat) + ADAM_EPS) + ADAM_WD * w), mn, vn


def _device_sum(parts):
    t = [p.astype(F32) for p in parts]
    return ((t[0] + t[1]) + (t[2] + t[3])) + ((t[4] + t[5]) + (t[6] + t[7]))


def _finish(a_hi, a_lo, n_lo, b3, misc, big, rows3, small):
    n_steps = 4
    tb, tr3 = 1024 // n_steps, 128 // n_steps

    def body(*refs):
        it = iter(refs)
        hi_ref, lo_ref, b_ref, misc_hbm = next(it), next(it), next(it), next(it)
        big_in = [next(it) for _ in range(3)]
        rows_in = [[next(it) for _ in range(3)] for _ in rows3]
        small_in = [[next(it) for _ in range(3)] for _ in small]
        big_out = [next(it) for _ in range(4)]
        rows_out = [[next(it) for _ in range(4)] for _ in rows3]
        small_out = [[next(it) for _ in range(4)] for _ in small]
        loss_ref = next(it)
        m_vmem, send_sems, recv_sems, local_sem = next(it), next(it), next(it), next(it)
        start, finish = _scatter_slabs(misc_hbm, 0, m_vmem, send_sems, recv_sems, local_sem)
        step = pl.program_id(0)

        @pl.when(step == 0)
        def _():
            start()

        def apply(g, ins, outs):
            d, mn, vn = _adam_update(g, ins[0][...], ins[1][...], ins[2][...])
            for r, val in zip(outs, (g, d, mn, vn)):
                r[...] = val

        lo = [lo_ref[s].astype(F32) for s in range(4)]
        g_big = jnp.where(_me() < n_lo, (lo[0] + lo[1]) + (lo[2] + lo[3]), _device_sum([hi_ref[s] for s in range(N_DEV)]))
        apply(g_big[None], big_in, big_out)
        for k in range(len(rows3)):
            apply(_device_sum([b_ref[s, k] for s in range(N_DEV)])[None], rows_in[k], rows_out[k])

        @pl.when(step == n_steps - 1)
        def _():
            finish()
            loss_ref[...] = _device_sum([m_vmem[s, MISC_LOSS:MISC_LOSS + 1, :] for s in range(N_DEV)])
            for (row0, lanes, ins), r_in, r_out in zip(small, small_in, small_out):
                n = ins[0].shape[0]
                apply(_device_sum([m_vmem[s, row0:row0 + n, :lanes] for s in range(N_DEV)]), r_in, r_out)

    whole = lambda shape: pl.BlockSpec(shape, lambda i: (0,) * len(shape))
    big_blk = pl.BlockSpec((1, tb, 1024), lambda i: (0, i, 0))
    rows_blk = pl.BlockSpec((1, tr3, 1024), lambda i: (0, i, 0))
    in_specs = [pl.BlockSpec((N_DEV, tb, 1024), lambda i: (0, i, 0)), pl.BlockSpec((4, tb, 1024), lambda i: (0, i, 0)),
                pl.BlockSpec((N_DEV, 3, tr3, 1024), lambda i: (0, 0, i, 0)), ANY]
    in_specs += [big_blk] * 3 + [rows_blk] * (3 * len(rows3))
    out_specs = [big_blk] * 4 + [rows_blk] * (4 * len(rows3))
    out_shape = [jax.ShapeDtypeStruct(big[0].shape, F32)] * 4
    for w, _, _ in rows3:
        out_shape += [jax.ShapeDtypeStruct(w.shape, F32)] * 4
    args = [a_hi, a_lo, b3, misc, *big]
    for t in rows3:
        args += list(t)
    for _, _, t in small:
        in_specs += [whole(t[0].shape)] * 3
        out_specs += [whole(t[0].shape)] * 4
        out_shape += [jax.ShapeDtypeStruct(t[0].shape, F32)] * 4
        args += list(t)
    out_specs.append(whole((1, 1024)))
    out_shape.append(jax.ShapeDtypeStruct((1, 1024), F32))
    outs = pl.pallas_call(
        body, name="reduce_sum_adamw", out_shape=tuple(out_shape), grid=(n_steps,),
        in_specs=in_specs, out_specs=tuple(out_specs),
        scratch_shapes=[pltpu.VMEM(misc.shape, misc.dtype)] + list(SCATTER_SEMS),
        compiler_params=_params(("arbitrary",)),
    )(*args)
    return [tuple(outs[4 * k:4 * k + 4]) for k in range(len(outs) // 4)], outs[-1][0, 0]


def _gather_order():
    x, y, c = lax.axis_index("x"), lax.axis_index("y"), lax.axis_index("c")
    chips = [(1 - x, y), (x, 1 - y), (1 - x, 1 - y)]
    order = [_device_index(x, y, c), _device_index(x, y, 1 - c)]
    order += [_device_index(*q, c) for q in chips] + [_device_index(*q, 1 - c) for q in chips]
    return jnp.stack(order).astype(jnp.int32)


def _in_projection(zp, norm_w, w_shard, b_blocks, order):
    n_rows = zp.shape[0]
    tr = _row_tile(n_rows, 832)
    nt = n_rows // tr

    def body(order_ref, z_ref, nw_ref, w_hbm, b_ref, p_ref, h_ref, w_out, w_vmem, h_all, send_sems, recv_sems, local_sem,
             out_sems):
        s, i = pl.program_id(0), pl.program_id(1)

        x, y, c = lax.axis_index("x"), lax.axis_index("y"), lax.axis_index("c")
        me, sibling = (x, y, c), (x, y, 1 - c)
        chips = [(1 - x, y), (x, 1 - y), (1 - x, 1 - y)]

        def slot(px, py, pc):
            return w_vmem.at[_device_index(px, py, pc)]

        def copy(k, blk, to, src=None):
            return pltpu.make_async_remote_copy(
                src_ref=slot(*blk) if src is None else src, dst_ref=slot(*blk),
                send_sem=send_sems.at[k], recv_sem=recv_sems.at[k], device_id=to, device_id_type=MESH)

        own = pltpu.make_async_copy(w_hbm, slot(*me), local_sem)
        first = [copy(0, me, sibling, src=w_hbm)] + [copy(1 + j, me, (*q, c), src=w_hbm) for j, q in enumerate(chips)]
        passed = [copy(4 + j, (*q, c), sibling) for j, q in enumerate(chips)]
        arrivals = [None, copy(0, sibling, me)]
        arrivals += [copy(1 + j, (*q, c), me) for j, q in enumerate(chips)]
        arrivals += [copy(4 + j, (*q, 1 - c), me) for j, q in enumerate(chips)]

        def keep(step):
            return pltpu.make_async_copy(w_vmem.at[order_ref[step]], w_out.at[order_ref[step]], out_sems.at[step])

        for step in range(N_DEV):
            @pl.when((i == 0) & (s == step))
            def _(step=step):
                if step == 0:
                    own.start()
                    for cp in first:
                        cp.start()
                    own.wait()
                else:
                    arrivals[step].wait_recv()
                    if 2 <= step <= 4:
                        passed[step - 2].start()
                keep(step).start()

        @pl.when(s == 0)
        def _():
            z = z_ref[...]
            r = lax.rsqrt(jnp.mean(z * z, axis=-1, keepdims=True) + EPS)
            h = (z * r * nw_ref[...]).astype(BF16)
            h_all[i] = h
            h_ref[...] = h

        p_ref[0] = jnp.dot(h_all[i], w_vmem[order_ref[s]], preferred_element_type=F32) + b_ref[0]

        @pl.when((s == N_DEV - 1) & (i == nt - 1))
        def _():
            for cp in first + passed:
                cp.wait_send()
            for step in range(N_DEV):
                keep(step).wait()

    first_pass = lambda s, i, rest: jnp.where(s == 0, i, rest)
    return pl.pallas_call(
        body, name="in_projection_gather",
        out_shape=(jax.ShapeDtypeStruct((N_COLBLK, n_rows, 1024), F32),
                   jax.ShapeDtypeStruct((n_rows, D_MODEL), BF16),
                   jax.ShapeDtypeStruct((N_DEV, D_MODEL, 1024), BF16)),
        grid_spec=pltpu.PrefetchScalarGridSpec(
            num_scalar_prefetch=1, grid=(N_DEV, nt),
            in_specs=[pl.BlockSpec((tr, D_MODEL), lambda s, i, o: (first_pass(s, i, 0), 0)),
                      pl.BlockSpec((1, D_MODEL), lambda s, i, o: (0, 0)), ANY,
                      pl.BlockSpec((1, 1, 1024), lambda s, i, o: (o[s], 0, 0))],
            out_specs=(pl.BlockSpec((1, tr, 1024), lambda s, i, o: (o[s], i, 0)),
                       pl.BlockSpec((tr, D_MODEL), lambda s, i, o: (first_pass(s, i, nt - 1), 0)), ANY),
            scratch_shapes=[pltpu.VMEM((N_DEV, D_MODEL, 1024), BF16), pltpu.VMEM((nt, tr, D_MODEL), BF16),
                            pltpu.SemaphoreType.DMA((7,)), pltpu.SemaphoreType.DMA((7,)), pltpu.SemaphoreType.DMA(()),
                            pltpu.SemaphoreType.DMA((N_DEV,))]),
        compiler_params=_params(("arbitrary", "arbitrary")),
    )(order, zp, norm_w, w_shard, b_blocks)


def _lower_bound(lb_ref):
    l0, l1 = lb_ref[0:1, :], lb_ref[1:2, :]
    _, lb = _sigmoid_pair(l1 - l0)
    return lb


def _chunk_gates(fz, lb, valid):
    sig, nsig = _sigmoid_pair(fz)
    f = lb + (1.0 - lb) * sig
    g2 = jnp.where(valid, jnp.log2(f), 0.0)
    k = jnp.where(valid, (1.0 - lb) * nsig, 0.0)
    return sig, nsig, f, g2, k


def _tri(n, upper=False):
    r = lax.broadcasted_iota(jnp.int32, (n, n), 0)
    c = lax.broadcasted_iota(jnp.int32, (n, n), 1)
    return jnp.where((r <= c) if upper else (r >= c), 1.0, 0.0).astype(BF16)


def _tri_dot(tri, x):
    hi = x.astype(BF16)
    rest = x - hi.astype(F32)
    mid = rest.astype(BF16)
    low = (rest - mid.astype(F32)).astype(BF16)
    return (jnp.dot(tri, hi, preferred_element_type=F32) + jnp.dot(tri, mid, preferred_element_type=F32)
            + jnp.dot(tri, low, preferred_element_type=F32))


def _intra_scores(q_ref, k_ref, b2_ref, col0):
    cols = pl.ds(col0, HEAD_DIM)
    rows_s = lax.broadcasted_iota(jnp.int32, (SUB, 1), 0)
    lanes_c = lax.broadcasted_iota(jnp.int32, (1, CHUNK), 1)
    blocks = []
    for i in range(N_SUB):
        lo = i * SUB
        qi = q_ref[lo:lo + SUB, cols]
        bi = b2_ref[lo:lo + SUB, cols]
        if i == 0:
            acc = jnp.zeros((SUB, CHUNK), F32)
        else:
            ref_i = b2_ref[lo:lo + 1, cols]
            qt = qi * jnp.exp2(bi - ref_i)
            kt = jnp.concatenate([k_ref[0:lo, cols] * jnp.exp2(ref_i - b2_ref[0:lo, cols]),
                                  jnp.zeros((CHUNK - lo, HEAD_DIM), F32)], axis=0)
            acc = _dot_nt(qt, kt)
        for s in range(SUB):
            b_s = b2_ref[lo + s:lo + s + 1, cols]
            k_s = k_ref[lo + s:lo + s + 1, cols]
            w = jnp.exp2(jnp.minimum(bi - b_s, 0.0))
            col = jnp.sum((qi * w) * k_s, axis=-1, keepdims=True)
            acc = jnp.where(lanes_c == lo + s, col, acc)
        blocks.append(jnp.where(lanes_c <= lo + rows_s, acc, 0.0))
    return jnp.concatenate(blocks, axis=0)


def _hgrn_forward(p, lb_logits, w_rest):
    n_rows = p.shape[1]
    n_chunks = n_rows // CHUNK
    width = HEADS * HEAD_DIM

    def body(q_ref, fz_ref, v_ref, lb_ref, rest_ref, o_ref, st_out_ref, a_out_ref, rest_out,
             state, k_vmem, b2_vmem, send_sems, recv_sems, local_sem):
        n = pl.program_id(0)
        own, sends, arrivals = _direct_gather(rest_ref, rest_out, send_sems, recv_sems, local_sem)

        @pl.when(n == 0)
        def _():
            state[...] = jnp.zeros_like(state)
            own.start()
            for cp in sends:
                cp.start()

        rows = n * CHUNK + lax.broadcasted_iota(jnp.int32, (CHUNK, 1), 0)
        valid = rows >= PAD_ROWS
        lb = _lower_bound(lb_ref)
        _, _, _, g2, k = _chunk_gates(fz_ref[0], lb, valid)
        k_vmem[...] = k
        b2_vmem[...] = _tri_dot(_tri(CHUNK), g2)
        q_view = q_ref.at[0]
        for h in range(HEADS):
            cols = pl.ds(h * HEAD_DIM, HEAD_DIM)
            st = state[h]
            st_out_ref[0, h] = st
            bh = b2_vmem[:, cols]
            kh = k_vmem[:, cols]
            vh = jnp.where(valid, v_ref[0, :, cols], 0.0)
            qe = q_ref[0, :, cols] * jnp.exp2(bh)
            a = _intra_scores(q_view, k_vmem, b2_vmem, h * HEAD_DIM).astype(BF16)
            a_out_ref[0, h] = a
            o_ref[:, cols] = _dot_nt(qe, st) + _dot(a, vh)
            b_last = b2_vmem[CHUNK - 1:CHUNK, cols]
            kd = kh * jnp.exp2(b_last - bh)
            state[h] = st * jnp.exp2(b_last) + _dot_tn(vh, kd)

        @pl.when(n == n_chunks - 1)
        def _():
            for cp in arrivals:
                cp.wait_recv()
            for cp in sends:
                cp.wait_send()
            own.wait()

    blk = lambda c: pl.BlockSpec((1, CHUNK, width), lambda n, c=c: (c, n, 0))
    return pl.pallas_call(
        body, name="hgrn_forward",
        out_shape=(jax.ShapeDtypeStruct((n_rows, width), F32),
                   jax.ShapeDtypeStruct((n_chunks, HEADS, HEAD_DIM, HEAD_DIM), F32),
                   jax.ShapeDtypeStruct((n_chunks, HEADS, CHUNK, CHUNK), BF16),
                   jax.ShapeDtypeStruct((N_DEV,) + w_rest.shape, w_rest.dtype)),
        grid=(n_chunks,),
        in_specs=[blk(0), blk(1), blk(2), pl.BlockSpec((2, width), lambda n: (0, 0)), ANY],
        out_specs=(pl.BlockSpec((CHUNK, width), lambda n: (n, 0)),
                   pl.BlockSpec((1, HEADS, HEAD_DIM, HEAD_DIM), lambda n: (n, 0, 0, 0)),
                   pl.BlockSpec((1, HEADS, CHUNK, CHUNK), lambda n: (n, 0, 0, 0)), ANY),
        scratch_shapes=[pltpu.VMEM((HEADS, HEAD_DIM, HEAD_DIM), F32), pltpu.VMEM((CHUNK, width), F32),
                        pltpu.VMEM((CHUNK, width), F32)] + list(GATHER_SEMS),
        compiler_params=_params(("arbitrary",)),
    )(p, p, p, lb_logits, w_rest)


def _hgrn_backward(p, lb_logits, states, scores, d_o, g_slabs, first_owner, g_rows):
    n_rows = p.shape[1]
    n_chunks = n_rows // CHUNK
    width = HEADS * HEAD_DIM

    def body(q_ref, fz_ref, v_ref, lb_ref, st_ref, a_ref, do_ref, gs_hbm, gr_hbm, dp_ref, dbias_ref, dlb_ref, rs_hbm, rr_hbm,
             dstate, k_vmem, b2_vmem, *sems):
        step = pl.program_id(0)
        n = n_chunks - 1 - step
        start_slabs, finish_slabs = _scatter_slabs(gs_hbm, first_owner, rs_hbm, *sems[:3])
        start_rows, finish_rows = _scatter_rows(gr_hbm, rr_hbm, *sems[3:])

        @pl.when(step == 0)
        def _():
            dstate[...] = jnp.zeros_like(dstate)
            dbias_ref[...] = jnp.zeros_like(dbias_ref)
            dlb_ref[...] = jnp.zeros_like(dlb_ref)
            start_slabs()
            start_rows()

        rows = n * CHUNK + lax.broadcasted_iota(jnp.int32, (CHUNK, 1), 0)
        valid = rows >= PAD_ROWS
        lb = _lower_bound(lb_ref)
        sig, nsig, f, g2, k = _chunk_gates(fz_ref[0], lb, valid)
        k_vmem[...] = k
        b2_vmem[...] = _tri_dot(_tri(CHUNK), g2)
        rows_c = lax.broadcasted_iota(jnp.int32, (CHUNK, 1), 0)
        rows_s = lax.broadcasted_iota(jnp.int32, (SUB, 1), 0)
        lanes_c = lax.broadcasted_iota(jnp.int32, (1, CHUNK), 1)
        causal = lax.broadcasted_iota(jnp.int32, (CHUNK, CHUNK), 0) >= lax.broadcasted_iota(jnp.int32, (CHUNK, CHUNK), 1)
        tri_up = _tri(CHUNK, upper=True)
        for h in range(HEADS):
            cols = pl.ds(h * HEAD_DIM, HEAD_DIM)
            st = st_ref[0, h]
            dst = dstate[h]
            qh = q_ref[0, :, cols]
            bh = b2_vmem[:, cols]
            kh = k_vmem[:, cols]
            vh = jnp.where(valid, v_ref[0, :, cols], 0.0)
            doh = do_ref[:, cols]
            eb = jnp.exp2(bh)
            qe = qh * eb
            b_last = b2_vmem[CHUNK - 1:CHUNK, cols]
            e_last = jnp.exp2(b_last)
            decay_k = jnp.exp2(b_last - bh)
            kd = kh * decay_k
            dqe = _dot(doh, st)
            da = jnp.where(causal, _dot_nt(doh, vh), 0.0)
            dv = _dot_tn(a_ref[0, h], doh) + _dot_nt(kd, dst)
            dkd = _dot(vh, dst)
            dstate[h] = dst * e_last + _dot_tn(doh, qe)
            db_last = (jnp.sum(dst * st, axis=0, keepdims=True) * e_last
                       + jnp.sum(dkd * kd, axis=0, keepdims=True))
            dq_blocks, dk_blocks = [], []
            dk_earlier = jnp.zeros((CHUNK, HEAD_DIM), F32)
            for i in range(N_SUB):
                lo = i * SUB
                qi = q_ref[0, lo:lo + SUB, cols]
                bi = b2_vmem[lo:lo + SUB, cols]
                da_i = da[lo:lo + SUB, :]
                if i == 0:
                    dq_i = jnp.zeros((SUB, HEAD_DIM), F32)
                else:
                    ref_i = b2_vmem[lo:lo + 1, cols]
                    eq = jnp.exp2(bi - ref_i)
                    ek = jnp.exp2(ref_i - b2_vmem[0:lo, cols])
                    later = jnp.zeros((CHUNK - lo, HEAD_DIM), F32)
                    kt = jnp.concatenate([k_vmem[0:lo, cols] * ek, later], axis=0)
                    dq_i = _dot(da_i, kt) * eq
                    dk_earlier = dk_earlier + jnp.concatenate([_dot_tn(da_i, qi * eq)[0:lo] * ek, later], axis=0)
                dk_i = jnp.zeros((SUB, HEAD_DIM), F32)
                for s in range(SUB):
                    b_s = b2_vmem[lo + s:lo + s + 1, cols]
                    k_s = k_vmem[lo + s:lo + s + 1, cols]
                    w = jnp.exp2(jnp.minimum(bi - b_s, 0.0))
                    da_col = jnp.sum(jnp.where(lanes_c == lo + s, da_i, 0.0), axis=-1, keepdims=True)
                    gw = da_col * w
                    dq_i = dq_i + gw * k_s
                    dk_i = jnp.where(rows_s == s, jnp.sum(gw * qi, axis=0, keepdims=True), dk_i)
                dq_blocks.append(dq_i)
                dk_blocks.append(dk_i)
            dq_intra = jnp.concatenate(dq_blocks, axis=0)
            dk_intra = jnp.concatenate(dk_blocks, axis=0) + dk_earlier
            dq = dqe * eb + dq_intra
            dk = dkd * decay_k + dk_intra
            db = dqe * qe - dkd * kd + qh * dq_intra - kh * dk_intra
            db = db + jnp.where(rows_c == CHUNK - 1, db_last, 0.0)
            dg = _tri_dot(tri_up, db)
            fh = f[:, h * HEAD_DIM:(h + 1) * HEAD_DIM]
            sh = sig[:, h * HEAD_DIM:(h + 1) * HEAD_DIM]
            nh = nsig[:, h * HEAD_DIM:(h + 1) * HEAD_DIM]
            lbh = lb[:, h * HEAD_DIM:(h + 1) * HEAD_DIM]
            df = jnp.where(valid, dg / fh - dk, 0.0)
            dfz = df * (1.0 - lbh) * sh * nh
            dq = jnp.where(valid, dq, 0.0)
            dv = jnp.where(valid, dv, 0.0)
            dlb_ref[:, cols] += jnp.sum(df * nh, axis=0, keepdims=True)
            dp_ref[0, :, cols] = dq.astype(BF16)
            dp_ref[1, :, cols] = dfz.astype(BF16)
            dp_ref[2, :, cols] = dv.astype(BF16)
            dbias_ref[0, :, cols] += jnp.sum(dq, axis=0, keepdims=True)
            dbias_ref[1, :, cols] += jnp.sum(dfz, axis=0, keepdims=True)
            dbias_ref[2, :, cols] += jnp.sum(dv, axis=0, keepdims=True)

        @pl.when(step == n_chunks - 1)
        def _():
            finish_slabs()
            finish_rows()

    rev = lambda s: n_chunks - 1 - s
    blk = lambda c: pl.BlockSpec((1, CHUNK, width), lambda s, c=c: (c, rev(s), 0))
    return pl.pallas_call(
        body, name="hgrn_backward",
        out_shape=(jax.ShapeDtypeStruct((3, n_rows, width), BF16),
                   jax.ShapeDtypeStruct((3, 1, width), F32),
                   jax.ShapeDtypeStruct((1, width), F32),
                   jax.ShapeDtypeStruct((N_DEV,) + g_slabs.shape[1:], g_slabs.dtype),
                   jax.ShapeDtypeStruct((N_DEV, N_ROW_GRADS, 128, g_rows.shape[2]), g_rows.dtype)),
        grid=(n_chunks,),
        in_specs=[blk(0), blk(1), blk(2), pl.BlockSpec((2, width), lambda s: (0, 0)),
                  pl.BlockSpec((1, HEADS, HEAD_DIM, HEAD_DIM), lambda s: (rev(s), 0, 0, 0)),
                  pl.BlockSpec((1, HEADS, CHUNK, CHUNK), lambda s: (rev(s), 0, 0, 0)),
                  pl.BlockSpec((CHUNK, width), lambda s: (rev(s), 0)), ANY, ANY],
        out_specs=(pl.BlockSpec((3, CHUNK, width), lambda s: (0, rev(s), 0)),
                   pl.BlockSpec((3, 1, width), lambda s: (0, 0, 0)),
                   pl.BlockSpec((1, width), lambda s: (0, 0)), ANY, ANY),
        scratch_shapes=[pltpu.VMEM((HEADS, HEAD_DIM, HEAD_DIM), F32), pltpu.VMEM((CHUNK, width), F32),
                        pltpu.VMEM((CHUNK, width), F32)] + list(SCATTER_SEMS) + list(SCATTER_ROWS_SEMS),
        compiler_params=_params(("arbitrary",)),
    )(p, p, p, lb_logits, states, scores, d_o, g_slabs, g_rows)


def _sigmoid_and_complement(x):
    s = 0.5 * jnp.tanh(0.5 * x) + 0.5
    return s, 1.0 - s


def _silu_and_grad(x):
    s, ns = _sigmoid_and_complement(x)
    return x * s, s * (1.0 + x * ns)


def _tail(p, o, zp, tgt, hg_norm_w, pool_w, pool_scale, w_down_hg, w_down_pool, w_out, final_norm_w):
    n_rows = zp.shape[0]
    tr = _row_tile(n_rows, 208)
    nt = n_rows // tr
    ext = tr + HALO
    n_groups = len(POOL_WINDOWS)

    def body(o_ref, ghg_ref, u_ref, gpool_ref, mhg_ref, mpool_ref, uhalo_ref, z_ref, tgt_hbm,
             hgw_ref, pw_ref, ps_ref, wdh_ref, wdp_ref, wout_ref, fnw_ref,
             do_ref, dp_ref, dz2_ref, lhs_ref, rhs_ref,
             dbias_ref, dhgw_ref, dpw_ref, dps_ref, dfnw_ref, loss_ref, halo_vmem, tgt_buf, tgt_sems):
        step = pl.program_id(0)
        ti = nt - 1 - step

        def target_rows(tile, slot, act):
            @pl.when(tile == 0)
            def _():
                cp = pltpu.make_async_copy(tgt_hbm.at[pl.ds(0, tr - CHUNK), :], tgt_buf.at[slot, pl.ds(CHUNK, tr - CHUNK), :],
                                           tgt_sems.at[slot])
                getattr(cp, act)()

            @pl.when(tile > 0)
            def _():
                cp = pltpu.make_async_copy(tgt_hbm.at[pl.ds(tile * tr - CHUNK, tr), :], tgt_buf.at[slot], tgt_sems.at[slot])
                getattr(cp, act)()

        @pl.when(step == 0)
        def _():
            halo_vmem[...] = jnp.zeros_like(halo_vmem)
            for r in (dbias_ref, dhgw_ref, dpw_ref, dps_ref, dfnw_ref, loss_ref):
                r[...] = jnp.zeros_like(r)
            if nt <= 2:
                tgt_buf[(nt - 1) % 2, 0:CHUNK, :] = jnp.zeros((CHUNK, D_MODEL), F32)
            target_rows(ti, 0, "start")

        @pl.when(ti > 0)
        def _():
            target_rows(ti - 1, (step + 1) % 2, "start")

        rows = ti * tr + lax.broadcasted_iota(jnp.int32, (tr, 1), 0)
        valid = rows >= PAD_ROWS
        in_loss = rows >= CHUNK
        count_pos = jnp.maximum(rows - PAD_ROWS + 1, 1).astype(F32)

        o = o_ref[...]
        hgw = hgw_ref[...]
        inv_o, on_parts = [], []
        for h in range(HEADS):
            oh = o[:, h * HEAD_DIM:(h + 1) * HEAD_DIM]
            r = lax.rsqrt(jnp.mean(oh * oh, axis=-1, keepdims=True) + EPS)
            inv_o.append(r)
            on_parts.append(oh * r)
        o_hat = jnp.concatenate(on_parts, axis=1)
        o_n = o_hat * hgw
        g_hg = ghg_ref[0]
        silu_hg, dsilu_hg = _silu_and_grad(g_hg)
        a_hg = o_n * silu_hg
        y_hg = _dot(a_hg, wdh_ref[...])

        u = jnp.where(valid, u_ref[0], 0.0)
        u_prev = jnp.where(ti > 0, uhalo_ref[0], 0.0)
        u_ext = jnp.concatenate([u_prev, u], axis=0)
        pooled_parts, mixed_parts, inv_cnt = [], [], []
        for gi, win in enumerate(POOL_WINDOWS):
            lanes = slice(gi * POOL_GDIM, (gi + 1) * POOL_GDIM)
            s = u_ext[:, lanes]
            shift = 1
            while shift < win:
                s = s + pltpu.roll(s, shift, 0)
                shift *= 2
            ic = 1.0 / jnp.minimum(count_pos, float(win))
            inv_cnt.append(ic)
            pooled = s[HALO:] * ic - u[:, lanes]
            pooled_parts.append(pooled)
            mixed_parts.append(_dot(pooled, pw_ref[gi]))
        mixed = jnp.concatenate(mixed_parts, axis=1)
        ps = ps_ref[...]
        g_pool = gpool_ref[0]
        silu_pool, dsilu_pool = _silu_and_grad(g_pool)
        a_pool = mixed * ps * silu_pool
        y_pool = _dot(a_pool, wdp_ref[...])

        m_hg, m_pool = mhg_ref[0], mpool_ref[0]
        s_hg, ns_hg = _sigmoid_and_complement(m_hg)
        s_pool, ns_pool = _sigmoid_and_complement(m_pool)
        merged = s_hg * y_hg + s_pool * y_pool
        z2 = z_ref[...] + _dot(merged, wout_ref[...])
        r2 = lax.rsqrt(jnp.mean(z2 * z2, axis=-1, keepdims=True) + EPS)
        n2 = z2 * r2
        fnw = fnw_ref[...]
        target_rows(ti, step % 2, "wait")
        err = jnp.where(in_loss, n2 * fnw - tgt_buf[step % 2], 0.0)
        loss_ref[...] += jnp.sum(jnp.sum(err * err, axis=0, keepdims=True), axis=1, keepdims=True) * (0.5 / D_MODEL)
        dy = err * (1.0 / D_MODEL)

        dfnw_ref[...] += jnp.sum(dy * n2, axis=0, keepdims=True)
        gy = dy * fnw
        dz2 = r2 * (gy - n2 * jnp.mean(gy * n2, axis=-1, keepdims=True))
        dmerged = _dot_nt(dz2, wout_ref[...])
        dy_hg = s_hg * dmerged
        dy_pool = s_pool * dmerged
        dm_hg = dmerged * y_hg * s_hg * ns_hg
        dm_pool = dmerged * y_pool * s_pool * ns_pool
        da_hg = _dot_nt(dy_hg, wdh_ref[...])
        da_pool = _dot_nt(dy_pool, wdp_ref[...])

        d_on = da_hg * silu_hg
        dg_hg = da_hg * o_n * dsilu_hg
        dhgw_ref[...] += jnp.sum(d_on * o_hat, axis=0, keepdims=True)
        gyo = d_on * hgw
        do_parts = []
        for h in range(HEADS):
            lanes = slice(h * HEAD_DIM, (h + 1) * HEAD_DIM)
            gh, nh = gyo[:, lanes], o_hat[:, lanes]
            do_parts.append(inv_o[h] * (gh - nh * jnp.mean(gh * nh, axis=-1, keepdims=True)))
        do_ref[...] = jnp.concatenate(do_parts, axis=1)

        dmixed = da_pool * ps * silu_pool
        dps_ref[...] += jnp.sum(da_pool * mixed * silu_pool, axis=0, keepdims=True)
        dg_pool = da_pool * mixed * ps * dsilu_pool
        du_parts = []
        for gi, win in enumerate(POOL_WINDOWS):
            lanes = slice(gi * POOL_GDIM, (gi + 1) * POOL_GDIM)
            dmx = dmixed[:, lanes]
            dpooled = _dot_nt(dmx, pw_ref[gi])
            dpw_ref[gi] += _dot_tn(pooled_parts[gi], dmx)
            dpt = dpooled * inv_cnt[gi]
            s = jnp.concatenate([dpt, halo_vmem[:, lanes]], axis=0)
            shift = 1
            while shift < win:
                s = s + pltpu.roll(s, ext - shift, 0)
                shift *= 2
            du_parts.append(s[:tr] - dpooled)
            halo_vmem[:, lanes] = dpt[:HALO]
        du = jnp.where(valid, jnp.concatenate(du_parts, axis=1), 0.0)

        for c, val in enumerate((dg_hg, du, dg_pool, dm_hg, dm_pool)):
            dp_ref[c] = val.astype(BF16)
            dbias_ref[c] += jnp.sum(val, axis=0, keepdims=True)
        dz2_ref[...] = dz2
        for c, (lhs, rhs) in enumerate(((merged, dz2), (a_hg, dy_hg), (a_pool, dy_pool))):
            lhs_ref[c] = lhs.astype(BF16)
            rhs_ref[c] = rhs.astype(BF16)

    rev = lambda s: nt - 1 - s
    rowblk = pl.BlockSpec((tr, D_MODEL), lambda s: (rev(s), 0))
    pblk = lambda c: pl.BlockSpec((1, tr, 1024), lambda s, c=c: (c, rev(s), 0))
    halo_blk = pl.BlockSpec((1, HALO, 1024), lambda s: (4, jnp.maximum(rev(s) * (tr // HALO) - 1, 0), 0))
    full = lambda shape: pl.BlockSpec(shape, lambda s: (0,) * len(shape))
    vec = full((1, D_MODEL))
    mat = full((D_MODEL, D_MODEL))
    act3 = jax.ShapeDtypeStruct((3, n_rows, D_MODEL), BF16)
    act3_blk = pl.BlockSpec((3, tr, D_MODEL), lambda s: (0, rev(s), 0))
    return pl.pallas_call(
        body, name="tail_forward_backward",
        out_shape=(jax.ShapeDtypeStruct((n_rows, D_MODEL), F32),
                   jax.ShapeDtypeStruct((5, n_rows, 1024), BF16),
                   jax.ShapeDtypeStruct((n_rows, D_MODEL), F32),
                   act3, act3,
                   jax.ShapeDtypeStruct((5, 1, 1024), F32),
                   jax.ShapeDtypeStruct((1, D_MODEL), F32),
                   jax.ShapeDtypeStruct((n_groups, POOL_GDIM, POOL_GDIM), F32),
                   jax.ShapeDtypeStruct((1, D_MODEL), F32),
                   jax.ShapeDtypeStruct((1, D_MODEL), F32),
                   jax.ShapeDtypeStruct((1, 1), F32)),
        grid=(nt,),
        in_specs=[rowblk, pblk(3), pblk(4), pblk(5), pblk(6), pblk(7), halo_blk, rowblk, ANY,
                  vec, full((n_groups, POOL_GDIM, POOL_GDIM)), vec, mat, mat, mat, vec],
        out_specs=(rowblk, pl.BlockSpec((5, tr, 1024), lambda s: (0, rev(s), 0)), rowblk,
                   act3_blk, act3_blk,
                   full((5, 1, 1024)), vec, full((n_groups, POOL_GDIM, POOL_GDIM)), vec, vec, full((1, 1))),
        scratch_shapes=[pltpu.VMEM((HALO, D_MODEL), F32), pltpu.VMEM((2, tr, D_MODEL), F32), pltpu.SemaphoreType.DMA((2,))],
        compiler_params=_params(("arbitrary",)),
    )(o, p, p, p, p, p, p, zp, tgt, hg_norm_w, pool_w, pool_scale, w_down_hg, w_down_pool, w_out, final_norm_w)


def _in_projection_backward(dp_a, dp_b, w_blocks, zp, dz2, norm_w, chip_sums):
    n_rows = zp.shape[0]
    tr = _row_tile(n_rows, 416)
    nt = n_rows // tr
    na, nb = dp_a.shape[0], dp_b.shape[0]

    def body(dpa_ref, dpb_ref, w_hbm, z_ref, dz2_ref, nw_ref, gs_hbm, gx_hbm, head_ref, dnw_ref, rs_hbm,
             w_vmem, sem, dz_buf, gx_sems, *sems):
        i = pl.program_id(0)
        start_slabs, finish_slabs = _scatter_low(gs_hbm, rs_hbm, *sems)

        def wait_rows_out(tile):
            @pl.when(tile == 0)
            def _():
                pltpu.make_async_copy(dz_buf.at[0, pl.ds(CHUNK, tr - CHUNK), :], gx_hbm.at[pl.ds(0, tr - CHUNK), :],
                                      gx_sems.at[0]).wait()

            @pl.when(tile > 0)
            def _():
                pltpu.make_async_copy(dz_buf.at[tile % 2], gx_hbm.at[pl.ds(tile * tr - CHUNK, tr), :],
                                      gx_sems.at[tile % 2]).wait()

        @pl.when(i == 0)
        def _():
            start_slabs()
            cp = pltpu.make_async_copy(w_hbm, w_vmem, sem)
            cp.start()
            cp.wait()
            dnw_ref[...] = jnp.zeros_like(dnw_ref)

        dh = jnp.zeros((tr, D_MODEL), F32)
        for j in range(na):
            dh = dh + _dot_nt(dpa_ref[j], w_vmem[j])
        for j in range(nb):
            dh = dh + _dot_nt(dpb_ref[j], w_vmem[na + j])
        z = z_ref[...]
        r = lax.rsqrt(jnp.mean(z * z, axis=-1, keepdims=True) + EPS)
        n1 = z * r
        dnw_ref[...] += jnp.sum(dh * n1, axis=0, keepdims=True)
        gh = dh * nw_ref[...]
        dz = dz2_ref[...] + r * (gh - n1 * jnp.mean(gh * n1, axis=-1, keepdims=True))

        @pl.when(i >= 2)
        def _():
            wait_rows_out(i - 2)

        dz_buf[i % 2] = dz

        @pl.when(i == 0)
        def _():
            head_ref[...] = dz[0:CHUNK]
            pltpu.make_async_copy(dz_buf.at[0, pl.ds(CHUNK, tr - CHUNK), :], gx_hbm.at[pl.ds(0, tr - CHUNK), :],
                                  gx_sems.at[0]).start()

        @pl.when(i > 0)
        def _():
            pltpu.make_async_copy(dz_buf.at[i % 2], gx_hbm.at[pl.ds(i * tr - CHUNK, tr), :], gx_sems.at[i % 2]).start()

        @pl.when(i == nt - 1)
        def _():
            if nt >= 2:
                wait_rows_out(i - 1)
            wait_rows_out(i)
            finish_slabs()

    rowblk = pl.BlockSpec((tr, D_MODEL), lambda i: (i, 0))
    vec = pl.BlockSpec((1, D_MODEL), lambda i: (0, 0))
    return pl.pallas_call(
        body, name="in_projection_backward",
        out_shape=(jax.ShapeDtypeStruct((n_rows - CHUNK, D_MODEL), F32), jax.ShapeDtypeStruct((CHUNK, D_MODEL), F32),
                   jax.ShapeDtypeStruct((1, D_MODEL), F32),
                   jax.ShapeDtypeStruct((4,) + chip_sums.shape[1:], chip_sums.dtype)),
        grid=(nt,),
        in_specs=[pl.BlockSpec((na, tr, 1024), lambda i: (0, i, 0)), pl.BlockSpec((nb, tr, 1024), lambda i: (0, i, 0)),
                  ANY, rowblk, rowblk, vec, ANY],
        out_specs=(ANY, pl.BlockSpec((CHUNK, D_MODEL), lambda i: (0, 0)), vec, ANY),
        scratch_shapes=[pltpu.VMEM((N_COLBLK, D_MODEL, 1024), BF16), pltpu.SemaphoreType.DMA(()),
                        pltpu.VMEM((2, tr, D_MODEL), F32), pltpu.SemaphoreType.DMA((2,)),
                        pltpu.SemaphoreType.DMA((2,)), pltpu.SemaphoreType.DMA((3,)), pltpu.SemaphoreType.DMA(())],
        compiler_params=_params(("arbitrary",)),
    )(dp_a, dp_b, w_blocks, zp, dz2, norm_w, chip_sums)


def _weight_grad(xs, ys, name):
    shared = xs.ndim == 2
    n_rows, m = xs.shape[-2:]
    nb, _, n = ys.shape
    tk = _row_tile(n_rows, 4160)
    n_k = n_rows // tk

    def body(x_ref, y_ref, o_ref, acc):
        k = pl.program_id(1)

        @pl.when(k == 0)
        def _():
            acc[...] = jnp.zeros_like(acc)

        acc[...] += _dot_tn(x_ref[...] if shared else x_ref[0], y_ref[0])

        @pl.when(k == n_k - 1)
        def _():
            o_ref[0] = acc[...].astype(o_ref.dtype)

    x_spec = pl.BlockSpec((tk, m), lambda j, k: (k, 0)) if shared else pl.BlockSpec((1, tk, m), lambda j, k: (j, k, 0))
    return pl.pallas_call(
        body, name=name,
        out_shape=jax.ShapeDtypeStruct((nb, m, n), BF16),
        grid=(nb, n_k),
        in_specs=[x_spec, pl.BlockSpec((1, tk, n), lambda j, k: (j, k, 0))],
        out_specs=pl.BlockSpec((1, m, n), lambda j, k: (j, 0, 0)),
        scratch_shapes=[pltpu.VMEM((m, n), F32)],
        compiler_params=_params(("arbitrary", "arbitrary")),
    )(xs, ys)


def kernel(x, meta_tokens, norm_w, w_in, b_in, lb_logits, hg_norm_w, pool_w, pool_scale, w_down_hg, w_down_pool, w_out, final_norm_w, loss_target, m_meta_tokens, m_norm_w, m_w_in, m_b_in, m_lb_logits, m_hg_norm_w, m_pool_w, m_pool_scale, m_w_down_hg, m_w_down_pool, m_w_out, m_final_norm_w, v_meta_tokens, v_norm_w, v_w_in, v_b_in, v_lb_logits, v_hg_norm_w, v_pool_w, v_pool_scale, v_w_down_hg, v_w_down_pool, v_w_out, v_final_norm_w):
    seq = x.shape[1]

    meta_full = _all_gather_small(meta_tokens).transpose(1, 0, 2).reshape(N_META, D_MODEL)
    w_rest = jnp.concatenate([w_down_hg[0].astype(BF16), w_down_pool[0].astype(BF16), w_out[0].astype(BF16),
                              pool_w[0].astype(BF16).reshape(32, 1024)], axis=0)

    zp = jnp.concatenate([jnp.zeros((PAD_ROWS, D_MODEL), F32), meta_full, x[0]], axis=0)
    p, h, w_blocks = _in_projection(zp, norm_w, w_in[0].astype(BF16), b_in.reshape(N_COLBLK, 1, 1024), _gather_order())
    o, states, scores, rest = _hgrn_forward(p, lb_logits, w_rest)
    wdh = rest[:, REST_W_DOWN_HG:REST_W_DOWN_HG + 128].reshape(1024, 1024)
    wdp = rest[:, REST_W_DOWN_POOL:REST_W_DOWN_POOL + 128].reshape(1024, 1024)
    wout = rest[:, REST_W_OUT:REST_W_OUT + 128].reshape(1024, 1024)
    pw = rest[:, REST_POOL_W:REST_POOL_W + 32].reshape(N_DEV, 4, 32, 256).transpose(1, 0, 2, 3).reshape(4, 256, 256)
    (d_o, dp_b, dz2, grad_lhs, grad_rhs, dbias_b, d_hgw, d_pw, d_ps, d_fnw, loss_part) = _tail(
        p, o, zp, loss_target[0], hg_norm_w, pw, pool_scale, wdh, wdp, wout, final_norm_w.reshape(1, D_MODEL))
    n_a = N_COLBLK - dp_b.shape[0]
    g_hi = _weight_grad(h, dp_b, "weight_grad_in_hi")
    g_rows = _weight_grad(grad_lhs, grad_rhs, "weight_grad_rows")
    dp_a, dbias_a, d_lb, recv_hi, recv_rows = _hgrn_backward(p, lb_logits, states, scores, d_o, g_hi, n_a, g_rows)
    assert n_a == LOW_OWNERS
    g_lo = _weight_grad(h, dp_a, "weight_grad_in_lo")
    dz_seq, dz_head, d_nw, recv_lo = _in_projection_backward(dp_a, dp_b, w_blocks, zp, dz2, norm_w, _pair_reduce_low(g_lo))

    lb = jax.nn.sigmoid(lb_logits[0:1] - lb_logits[1:2])
    d_l0 = d_lb * lb * (1.0 - lb)
    replicated = jnp.concatenate([dbias_a.reshape(3, 1024), dbias_b.reshape(5, 1024), d_nw, d_l0, -d_l0, d_hgw, d_ps, d_fnw,
                                  jnp.pad(loss_part, ((0, MISC_ROWS - MISC_LOSS - 1), (0, 1023)))], axis=0)
    d_meta = dz_head[PAD_ROWS:CHUNK].reshape(N_META, N_DEV, 128).transpose(1, 0, 2)
    d_pw_blocks = d_pw.reshape(4, N_DEV, 32, 256).transpose(1, 0, 2, 3).reshape(N_DEV, 32, 1024)
    g_misc = jnp.concatenate([d_pw_blocks, jnp.pad(d_meta, ((0, 0), (0, 0), (0, 1024 - 128))),
                              jnp.broadcast_to(replicated[None], (N_DEV, 16, 1024))], axis=1)

    as_rows = lambda t, n: t.reshape(n, 1024)
    small = [(MISC_POOL_W, 1024, tuple(as_rows(t, 32) for t in (pool_w, m_pool_w, v_pool_w))),
             (MISC_META, 128, (meta_tokens, m_meta_tokens, v_meta_tokens)),
             (MISC_B_IN, 1024, tuple(as_rows(t, 8) for t in (b_in, m_b_in, v_b_in))),
             (MISC_NORM_W, 1024, (norm_w, m_norm_w, v_norm_w)),
             (MISC_LB, 1024, (lb_logits, m_lb_logits, v_lb_logits)),
             (MISC_HG_NORM_W, 1024, (hg_norm_w, m_hg_norm_w, v_hg_norm_w)),
             (MISC_POOL_SCALE, 1024, (pool_scale, m_pool_scale, v_pool_scale)),
             (MISC_FINAL_NORM_W, 1024, tuple(as_rows(t, 1) for t in (final_norm_w, m_final_norm_w, v_final_norm_w)))]
    res, loss = _finish(recv_hi, recv_lo, n_a, recv_rows, g_misc, (w_in, m_w_in, v_w_in),
                  [(w_out, m_w_out, v_w_out), (w_down_hg, m_w_down_hg, v_w_down_hg), (w_down_pool, m_w_down_pool, v_w_down_pool)],
                  small)
    r_w_in, r_w_out, r_wdh, r_wdp, r_pw, r_meta, r_b_in, r_nw, r_lb, r_hgw, r_ps, r_fnw = res
    grad_x = dz_seq.reshape(1, seq, D_MODEL)
    per_kind = [(r_meta[k], r_nw[k], r_w_in[k], r_b_in[k].reshape(1, 8192), r_lb[k], r_hgw[k], r_pw[k].reshape(1, 4, 32, 256),
                 r_ps[k], r_wdh[k], r_wdp[k], r_w_out[k], r_fnw[k].reshape(1024)) for k in range(4)]
    return (loss, grad_x, *per_kind[0], *per_kind[1], *per_kind[2], *per_kind[3])
```

```python
import functools

import jax
import jax.numpy as jnp
from jax import lax
from jax.experimental import pallas as pl
from jax.experimental.pallas import tpu as pltpu

F32 = jnp.float32
BF16 = jnp.bfloat16

D_MODEL = 1024
N_META = 16
HEADS = 8
HEAD_DIM = 128
CHUNK = 64
SUB = 8
N_SUB = CHUNK // SUB
PAD_ROWS = CHUNK - N_META
POOL_WINDOWS = (2, 4, 8, 16)
POOL_GDIM = D_MODEL // len(POOL_WINDOWS)
HALO = 16
EPS = 1e-6
N_DEV = 8
N_COLBLK = 8
ADAM_LR, ADAM_B1, ADAM_B2, ADAM_EPS, ADAM_WD, ADAM_STEP = 0.001, 0.9, 0.999, 1e-08, 0.01, 10

VMEM_LIMIT = 56 * 1024 * 1024
MESH = pl.DeviceIdType.MESH
ANY = pl.BlockSpec(memory_space=pl.ANY)

REST_W_DOWN_HG = 0
REST_W_DOWN_POOL = 128
REST_W_OUT = 256
REST_POOL_W = 384
MISC_POOL_W = 0
MISC_META = 32
MISC_B_IN = 48
MISC_NORM_W = 56
MISC_LB = 57
MISC_HG_NORM_W = 59
MISC_POOL_SCALE = 60
MISC_FINAL_NORM_W = 61
MISC_LOSS = 62
MISC_ROWS = 64


def _params(sem=None):
    return pltpu.CompilerParams(dimension_semantics=sem, vmem_limit_bytes=VMEM_LIMIT)


def _row_tile(n_rows, prefer):
    best = 16
    for t in range(16, prefer + 1, 16):
        if n_rows % t == 0:
            best = t
    return best


def _sigmoid_pair(x):
    e = jnp.exp(-jnp.abs(x))
    r = 1.0 / (1.0 + e)
    er = e * r
    pos = x >= 0
    return jnp.where(pos, r, er), jnp.where(pos, er, r)


def _dot(a, b):
    return jnp.dot(a.astype(BF16), b.astype(BF16), preferred_element_type=F32)


def _dot_nt(a, b):
    return lax.dot_general(a.astype(BF16), b.astype(BF16), (((1,), (1,)), ((), ())), preferred_element_type=F32)


def _dot_tn(a, b):
    return lax.dot_general(a.astype(BF16), b.astype(BF16), (((0,), (0,)), ((), ())), preferred_element_type=F32)


def _device_index(px, py, pc):
    return 4 * px + 2 * py + pc


def _direct_gather(src_ref, dst_ref, send_sems, recv_sems, local_sem):
    x, y, c = lax.axis_index("x"), lax.axis_index("y"), lax.axis_index("c")
    own = pltpu.make_async_copy(src_ref, dst_ref.at[_device_index(x, y, c)], local_sem)
    sends, arrivals = [], []
    for k in range(1, N_DEV):
        peer = (1 - x if k & 4 else x, 1 - y if k & 2 else y, 1 - c if k & 1 else c)
        for slot, out in ((_device_index(x, y, c), sends), (_device_index(*peer), arrivals)):
            out.append(pltpu.make_async_remote_copy(
                src_ref=src_ref, dst_ref=dst_ref.at[slot], send_sem=send_sems.at[k - 1], recv_sem=recv_sems.at[k - 1],
                device_id=peer, device_id_type=MESH))
    return own, sends, arrivals


GATHER_SEMS = [pltpu.SemaphoreType.DMA((N_DEV - 1,)), pltpu.SemaphoreType.DMA((N_DEV - 1,)), pltpu.SemaphoreType.DMA(())]


def _all_gather_small(block):
    def body(x_ref, out_ref, send_sems, recv_sems, local_sem):
        own, sends, arrivals = _direct_gather(x_ref, out_ref, send_sems, recv_sems, local_sem)
        own.start()
        for cp in sends:
            cp.start()
        for cp in arrivals:
            cp.wait_recv()
        for cp in sends:
            cp.wait_send()
        own.wait()

    return pl.pallas_call(
        body, name="all_gather_meta",
        out_shape=jax.ShapeDtypeStruct((N_DEV,) + block.shape, block.dtype),
        in_specs=[ANY], out_specs=ANY, scratch_shapes=list(GATHER_SEMS),
    )(block)


def _peer(k):
    x, y, c = lax.axis_index("x"), lax.axis_index("y"), lax.axis_index("c")
    return (1 - x if k & 4 else x, 1 - y if k & 2 else y, 1 - c if k & 1 else c)


def _me():
    return _device_index(lax.axis_index("x"), lax.axis_index("y"), lax.axis_index("c"))


def _remote(src, dst, send_sem, recv_sem, peer_bits):
    return pltpu.make_async_remote_copy(src_ref=src, dst_ref=dst, send_sem=send_sem, recv_sem=recv_sem,
                                        device_id=_peer(peer_bits), device_id_type=MESH)


N_ROW_GRADS = 3
SCATTER_SEMS = [pltpu.SemaphoreType.DMA((N_DEV - 1,)), pltpu.SemaphoreType.DMA((N_DEV - 1,)), pltpu.SemaphoreType.DMA(())]
SCATTER_ROWS_SEMS = [pltpu.SemaphoreType.DMA((7 * N_ROW_GRADS,)), pltpu.SemaphoreType.DMA((7 * N_ROW_GRADS,)),
                     pltpu.SemaphoreType.DMA((N_ROW_GRADS,))]


def _scatter_slabs(g_ref, first, recv_ref, send_sems, recv_sems, local_sem):
    n = g_ref.shape[0]
    me = _me()

    def each(on_send, on_local, on_arrival):
        for kk in range(1, N_DEV):
            peer = jnp.bitwise_xor(me, kk)

            @pl.when((peer >= first) & (peer < first + n))
            def _(kk=kk, peer=peer):
                on_send(_remote(g_ref.at[peer - first], recv_ref.at[me], send_sems.at[kk - 1], recv_sems.at[kk - 1], kk))

        @pl.when((me >= first) & (me < first + n))
        def _():
            on_local(pltpu.make_async_copy(g_ref.at[me - first], recv_ref.at[me], local_sem))
            if on_arrival is not None:
                for kk in range(1, N_DEV):
                    on_arrival(_remote(g_ref.at[0], recv_ref.at[jnp.bitwise_xor(me, kk)], send_sems.at[kk - 1],
                                       recv_sems.at[kk - 1], kk))

    start = lambda: each(lambda cp: cp.start(), lambda cp: cp.start(), None)
    finish = lambda: each(lambda cp: cp.wait_send(), lambda cp: cp.wait(), lambda cp: cp.wait_recv())
    return start, finish


def _scatter_rows(g_ref, recv_ref, send_sems, recv_sems, local_sems):
    me = _me()
    rows = lambda m, dev: g_ref.at[m, pl.ds(dev * 128, 128), :]

    def copies():
        local = [pltpu.make_async_copy(rows(m, me), recv_ref.at[me, m], local_sems.at[m]) for m in range(N_ROW_GRADS)]
        sends, arrivals = [], []
        for m in range(N_ROW_GRADS):
            for kk in range(1, N_DEV):
                peer, sems = jnp.bitwise_xor(me, kk), (send_sems.at[7 * m + kk - 1], recv_sems.at[7 * m + kk - 1])
                sends.append(_remote(rows(m, peer), recv_ref.at[me, m], *sems, kk))
                arrivals.append(_remote(rows(m, me), recv_ref.at[peer, m], *sems, kk))
        return local, sends, arrivals

    def start():
        local, sends, _ = copies()
        for cp in local + sends:
            cp.start()

    def finish():
        local, sends, arrivals = copies()
        for cp in arrivals:
            cp.wait_recv()
        for cp in sends:
            cp.wait_send()
        for cp in local:
            cp.wait()

    return start, finish


LOW_OWNERS = 3


def _pair_reduce_low(g_lo):
    def body(g_ref, out_ref, got, kept, send_sems, recv_sems, local_sems):
        c = lax.axis_index("c")

        def to_sibling(slab, slot):
            return _remote(g_ref.at[slab], got.at[slot], send_sems.at[slot], recv_sems.at[slot], 1)

        def keep(slab, slot):
            return pltpu.make_async_copy(g_ref.at[slab], kept.at[slot], local_sems.at[slot])

        def add(slot):
            out_ref[slot] = (kept[slot].astype(F32) + got[slot].astype(F32)).astype(out_ref.dtype)

        @pl.when(c == 0)
        def _():
            copies = [to_sibling(1, 0), keep(0, 0), keep(2, 1)]
            for cp in copies:
                cp.start()
            to_sibling(0, 0).wait_recv()
            to_sibling(2, 1).wait_recv()
            copies[0].wait_send()
            copies[1].wait()
            copies[2].wait()
            add(0)
            add(1)

        @pl.when(c == 1)
        def _():
            copies = [to_sibling(0, 0), to_sibling(2, 1), keep(1, 0)]
            for cp in copies:
                cp.start()
            to_sibling(1, 0).wait_recv()
            copies[0].wait_send()
            copies[1].wait_send()
            copies[2].wait()
            add(0)
            out_ref[1] = jnp.zeros(out_ref.shape[1:], out_ref.dtype)

    pair = (2,) + g_lo.shape[1:]
    return pl.pallas_call(
        body, name="pair_reduce_low",
        out_shape=jax.ShapeDtypeStruct(pair, g_lo.dtype),
        in_specs=[ANY], out_specs=pl.BlockSpec(memory_space=pltpu.VMEM),
        scratch_shapes=[pltpu.VMEM(pair, g_lo.dtype), pltpu.VMEM(pair, g_lo.dtype), pltpu.SemaphoreType.DMA((2,)),
                        pltpu.SemaphoreType.DMA((2,)), pltpu.SemaphoreType.DMA((2,))],
        compiler_params=_params(),
    )(g_lo)


def _scatter_low(part_ref, recv_ref, send_sems, recv_sems, local_sem):
    x, y, c = lax.axis_index("x"), lax.axis_index("y"), lax.axis_index("c")
    chip = 2 * x + y
    routes = ((0, (0, 0, c), 0, None), (1, (0, 1, 0), 1, 0))

    def each(on_send, on_local, on_arrival):
        for slot, owner, owner_chip, core in routes:
            holds = (c == core) if core is not None else (c >= 0)
            rel = jnp.bitwise_xor(chip, owner_chip)

            @pl.when(holds & (rel != 0))
            def _(slot=slot, owner=owner, rel=rel):
                on_send(pltpu.make_async_remote_copy(
                    src_ref=part_ref.at[slot], dst_ref=recv_ref.at[chip], send_sem=send_sems.at[slot],
                    recv_sem=recv_sems.at[rel - 1], device_id=owner, device_id_type=MESH))

            @pl.when(holds & (rel == 0))
            def _(slot=slot, owner=owner, owner_chip=owner_chip):
                on_local(pltpu.make_async_copy(part_ref.at[slot], recv_ref.at[chip], local_sem))
                if on_arrival is not None:
                    for r in range(1, 4):
                        on_arrival(pltpu.make_async_remote_copy(
                            src_ref=part_ref.at[slot], dst_ref=recv_ref.at[r ^ owner_chip], send_sem=send_sems.at[slot],
                            recv_sem=recv_sems.at[r - 1], device_id=owner, device_id_type=MESH))

    start = lambda: each(lambda cp: cp.start(), lambda cp: cp.start(), None)
    finish = lambda: each(lambda cp: cp.wait_send(), lambda cp: cp.wait(), lambda cp: cp.wait_recv())
    return start, finish


def _adam_update(g, w, m, v):
    mn = ADAM_B1 * m + (1.0 - ADAM_B1) * g
    vn = ADAM_B2 * v + (1.0 - ADAM_B2) * (g * g)
    m_hat = mn / (1.0 - ADAM_B1 ** ADAM_STEP)
    v_hat = vn / (1.0 - ADAM_B2 ** ADAM_STEP)
    return -ADAM_LR * (m_hat / (jnp.sqrt(v_hat) + ADAM_EPS) + ADAM_WD * w), mn, vn


def _device_sum(parts):
    t = [p.astype(F32) for p in parts]
    return ((t[0] + t[1]) + (t[2] + t[3])) + ((t[4] + t[5]) + (t[6] + t[7]))


def _finish(a_hi, a_lo, n_lo, b3, misc, big, rows3, small):
    n_steps = 4
    tb, tr3 = 1024 // n_steps, 128 // n_steps

    def body(*refs):
        it = iter(refs)
        hi_ref, lo_ref, b_ref, misc_hbm = next(it), next(it), next(it), next(it)
        big_in = [next(it) for _ in range(3)]
        rows_in = [[next(it) for _ in range(3)] for _ in rows3]
        small_in = [[next(it) for _ in range(3)] for _ in small]
        big_out = [next(it) for _ in range(4)]
        rows_out = [[next(it) for _ in range(4)] for _ in rows3]
        small_out = [[next(it) for _ in range(4)] for _ in small]
        loss_ref = next(it)
        m_vmem, send_sems, recv_sems, local_sem = next(it), next(it), next(it), next(it)
        start, finish = _scatter_slabs(misc_hbm, 0, m_vmem, send_sems, recv_sems, local_sem)
        step = pl.program_id(0)

        @pl.when(step == 0)
        def _():
            start()

        def apply(g, ins, outs):
            d, mn, vn = _adam_update(g, ins[0][...], ins[1][...], ins[2][...])
            for r, val in zip(outs, (g, d, mn, vn)):
                r[...] = val

        lo = [lo_ref[s].astype(F32) for s in range(4)]
        g_big = jnp.where(_me() < n_lo, (lo[0] + lo[1]) + (lo[2] + lo[3]), _device_sum([hi_ref[s] for s in range(N_DEV)]))
        apply(g_big[None], big_in, big_out)
        for k in range(len(rows3)):
            apply(_device_sum([b_ref[s, k] for s in range(N_DEV)])[None], rows_in[k], rows_out[k])

        @pl.when(step == n_steps - 1)
        def _():
            finish()
            loss_ref[...] = _device_sum([m_vmem[s, MISC_LOSS:MISC_LOSS + 1, :] for s in range(N_DEV)])
            for (row0, lanes, ins), r_in, r_out in zip(small, small_in, small_out):
                n = ins[0].shape[0]
                apply(_device_sum([m_vmem[s, row0:row0 + n, :lanes] for s in range(N_DEV)]), r_in, r_out)

    whole = lambda shape: pl.BlockSpec(shape, lambda i: (0,) * len(shape))
    big_blk = pl.BlockSpec((1, tb, 1024), lambda i: (0, i, 0))
    rows_blk = pl.BlockSpec((1, tr3, 1024), lambda i: (0, i, 0))
    in_specs = [pl.BlockSpec((N_DEV, tb, 1024), lambda i: (0, i, 0)), pl.BlockSpec((4, tb, 1024), lambda i: (0, i, 0)),
                pl.BlockSpec((N_DEV, 3, tr3, 1024), lambda i: (0, 0, i, 0)), ANY]
    in_specs += [big_blk] * 3 + [rows_blk] * (3 * len(rows3))
    out_specs = [big_blk] * 4 + [rows_blk] * (4 * len(rows3))
    out_shape = [jax.ShapeDtypeStruct(big[0].shape, F32)] * 4
    for w, _, _ in rows3:
        out_shape += [jax.ShapeDtypeStruct(w.shape, F32)] * 4
    args = [a_hi, a_lo, b3, misc, *big]
    for t in rows3:
        args += list(t)
    for _, _, t in small:
        in_specs += [whole(t[0].shape)] * 3
        out_specs += [whole(t[0].shape)] * 4
        out_shape += [jax.ShapeDtypeStruct(t[0].shape, F32)] * 4
        args += list(t)
    out_specs.append(whole((1, 1024)))
    out_shape.append(jax.ShapeDtypeStruct((1, 1024), F32))
    outs = pl.pallas_call(
        body, name="reduce_sum_adamw", out_shape=tuple(out_shape), grid=(n_steps,),
        in_specs=in_specs, out_specs=tuple(out_specs),
        scratch_shapes=[pltpu.VMEM(misc.shape, misc.dtype)] + list(SCATTER_SEMS),
        compiler_params=_params(("arbitrary",)),
    )(*args)
    return [tuple(outs[4 * k:4 * k + 4]) for k in range(len(outs) // 4)], outs[-1][0, 0]


GATHER_UNITS = ((0, 0), (0, 1), (1, 0), (1, 1)) + tuple((place, 0) for place in range(2, 8)) + tuple(
    (place, 1) for place in range(2, 8))


def _gather_order():
    x, y, c = lax.axis_index("x"), lax.axis_index("y"), lax.axis_index("c")
    chips = [(1 - x, y), (x, 1 - y), (1 - x, 1 - y)]
    order = [_device_index(x, y, c), _device_index(x, y, 1 - c)]
    order += [_device_index(*q, c) for q in chips] + [_device_index(*q, 1 - c) for q in chips]
    return order


def _gather_units():
    order = _gather_order()
    blocks = jnp.stack([order[place] for place, _ in GATHER_UNITS]).astype(jnp.int32)
    return blocks, jnp.array([half for _, half in GATHER_UNITS], jnp.int32)


def _in_projection(zp, norm_w, w_shard, b_blocks, units):
    n_rows = zp.shape[0]
    tr = _row_tile(n_rows, 832)
    nt = n_rows // tr

    half_cols = 1024 // 2
    n_units = len(GATHER_UNITS)

    def body(blocks_ref, halves_ref, z_ref, nw_ref, w_hbm, b_ref, p_ref, h_ref, w_out, w_vmem, h_all,
             send_sems, recv_sems, local_sem, out_sems):
        s, i = pl.program_id(0), pl.program_id(1)
        x, y, c = lax.axis_index("x"), lax.axis_index("y"), lax.axis_index("c")
        me, sibling = (x, y, c), (x, y, 1 - c)
        chips = [(1 - x, y), (x, 1 - y), (1 - x, 1 - y)]

        def slot(blk, half):
            return w_vmem.at[_device_index(*blk), half]

        def mine(half):
            return w_hbm.at[:, pl.ds(half * half_cols, half_cols)]

        def copy(k, half, blk, to, own_block=False):
            return pltpu.make_async_remote_copy(
                src_ref=mine(half) if own_block else slot(blk, half), dst_ref=slot(blk, half),
                send_sem=send_sems.at[2 * k + half], recv_sem=recv_sems.at[2 * k + half], device_id=to, device_id_type=MESH)

        own = [pltpu.make_async_copy(mine(half), slot(me, half), local_sem.at[half]) for half in range(2)]
        first = [copy(k, half, me, to, own_block=True) for half in range(2)
                 for k, to in enumerate([sibling] + [(*q, c) for q in chips])]
        passed = [[copy(4 + j, half, (*q, c), sibling) for j, q in enumerate(chips)] for half in range(2)]
        sources = [None, sibling] + [(*q, c) for q in chips] + [(*q, 1 - c) for q in chips]
        arrival = lambda place, half: copy(place - 1, half, sources[place], me)

        def keep(unit):
            blk, half = blocks_ref[unit], halves_ref[unit]
            return pltpu.make_async_copy(w_vmem.at[blk, half], w_out.at[blk, half], out_sems.at[unit])

        for unit, (place, half) in enumerate(GATHER_UNITS):
            @pl.when((i == 0) & (s == unit))
            def _(unit=unit, place=place, half=half):
                if unit == 0:
                    for cp in own + first:
                        cp.start()
                if place == 0:
                    own[half].wait()
                else:
                    arrival(place, half).wait_recv()
                    if 2 <= place <= 4:
                        passed[half][place - 2].start()
                keep(unit).start()

        @pl.when(s == 0)
        def _():
            z = z_ref[...]
            r = lax.rsqrt(jnp.mean(z * z, axis=-1, keepdims=True) + EPS)
            h = (z * r * nw_ref[...]).astype(BF16)
            h_all[i] = h
            h_ref[...] = h

        p_ref[0] = jnp.dot(h_all[i], w_vmem[blocks_ref[s], halves_ref[s]], preferred_element_type=F32) + b_ref[0]

        @pl.when((s == n_units - 1) & (i == nt - 1))
        def _():
            for cp in first + passed[0] + passed[1]:
                cp.wait_send()
            for unit in range(n_units):
                keep(unit).wait()

    first_pass = lambda s, i, rest: jnp.where(s == 0, i, rest)
    return pl.pallas_call(
        body, name="in_projection_gather",
        out_shape=(jax.ShapeDtypeStruct((N_COLBLK, n_rows, 1024), F32),
                   jax.ShapeDtypeStruct((n_rows, D_MODEL), BF16),
                   jax.ShapeDtypeStruct((N_DEV, 2, D_MODEL, half_cols), BF16)),
        grid_spec=pltpu.PrefetchScalarGridSpec(
            num_scalar_prefetch=2, grid=(n_units, nt),
            in_specs=[pl.BlockSpec((tr, D_MODEL), lambda s, i, blk, hf: (first_pass(s, i, 0), 0)),
                      pl.BlockSpec((1, D_MODEL), lambda s, i, blk, hf: (0, 0)), ANY,
                      pl.BlockSpec((1, 1, half_cols), lambda s, i, blk, hf: (blk[s], 0, hf[s]))],
            out_specs=(pl.BlockSpec((1, tr, half_cols), lambda s, i, blk, hf: (blk[s], i, hf[s])),
                       pl.BlockSpec((tr, D_MODEL), lambda s, i, blk, hf: (first_pass(s, i, nt - 1), 0)), ANY),
            scratch_shapes=[pltpu.VMEM((N_DEV, 2, D_MODEL, half_cols), BF16), pltpu.VMEM((nt, tr, D_MODEL), BF16),
                            pltpu.SemaphoreType.DMA((14,)), pltpu.SemaphoreType.DMA((14,)), pltpu.SemaphoreType.DMA((2,)),
                            pltpu.SemaphoreType.DMA((n_units,))]),
        compiler_params=_params(("arbitrary", "arbitrary")),
    )(*units, zp, norm_w, w_shard, b_blocks)


def _lower_bound(lb_ref):
    l0, l1 = lb_ref[0:1, :], lb_ref[1:2, :]
    _, lb = _sigmoid_pair(l1 - l0)
    return lb


def _chunk_gates(fz, lb, valid):
    sig, nsig = _sigmoid_pair(fz)
    f = lb + (1.0 - lb) * sig
    g2 = jnp.where(valid, jnp.log2(f), 0.0)
    k = jnp.where(valid, (1.0 - lb) * nsig, 0.0)
    return sig, nsig, f, g2, k


def _tri(n, upper=False):
    r = lax.broadcasted_iota(jnp.int32, (n, n), 0)
    c = lax.broadcasted_iota(jnp.int32, (n, n), 1)
    return jnp.where((r <= c) if upper else (r >= c), 1.0, 0.0).astype(BF16)


def _tri_dot(tri, x):
    hi = x.astype(BF16)
    rest = x - hi.astype(F32)
    mid = rest.astype(BF16)
    low = (rest - mid.astype(F32)).astype(BF16)
    return (jnp.dot(tri, hi, preferred_element_type=F32) + jnp.dot(tri, mid, preferred_element_type=F32)
            + jnp.dot(tri, low, preferred_element_type=F32))


def _intra_scores(q_ref, k_ref, b2_ref, col0):
    cols = pl.ds(col0, HEAD_DIM)
    rows_s = lax.broadcasted_iota(jnp.int32, (SUB, 1), 0)
    lanes_c = lax.broadcasted_iota(jnp.int32, (1, CHUNK), 1)
    blocks = []
    for i in range(N_SUB):
        lo = i * SUB
        qi = q_ref[lo:lo + SUB, cols]
        bi = b2_ref[lo:lo + SUB, cols]
        if i == 0:
            acc = jnp.zeros((SUB, CHUNK), F32)
        else:
            ref_i = b2_ref[lo:lo + 1, cols]
            qt = qi * jnp.exp2(bi - ref_i)
            kt = jnp.concatenate([k_ref[0:lo, cols] * jnp.exp2(ref_i - b2_ref[0:lo, cols]),
                                  jnp.zeros((CHUNK - lo, HEAD_DIM), F32)], axis=0)
            acc = _dot_nt(qt, kt)
        for s in range(SUB):
            b_s = b2_ref[lo + s:lo + s + 1, cols]
            k_s = k_ref[lo + s:lo + s + 1, cols]
            w = jnp.exp2(jnp.minimum(bi - b_s, 0.0))
            col = jnp.sum((qi * w) * k_s, axis=-1, keepdims=True)
            acc = jnp.where(lanes_c == lo + s, col, acc)
        blocks.append(jnp.where(lanes_c <= lo + rows_s, acc, 0.0))
    return jnp.concatenate(blocks, axis=0)


def _hgrn_forward(p, lb_logits, w_rest):
    n_rows = p.shape[1]
    n_chunks = n_rows // CHUNK
    width = HEADS * HEAD_DIM

    def body(q_ref, fz_ref, v_ref, lb_ref, rest_ref, o_ref, st_out_ref, a_out_ref, rest_out,
             state, k_vmem, b2_vmem, send_sems, recv_sems, local_sem):
        n = pl.program_id(0)
        own, sends, arrivals = _direct_gather(rest_ref, rest_out, send_sems, recv_sems, local_sem)

        @pl.when(n == 0)
        def _():
            state[...] = jnp.zeros_like(state)
            own.start()
            for cp in sends:
                cp.start()

        rows = n * CHUNK + lax.broadcasted_iota(jnp.int32, (CHUNK, 1), 0)
        valid = rows >= PAD_ROWS
        lb = _lower_bound(lb_ref)
        _, _, _, g2, k = _chunk_gates(fz_ref[0], lb, valid)
        k_vmem[...] = k
        b2_vmem[...] = _tri_dot(_tri(CHUNK), g2)
        q_view = q_ref.at[0]
        for h in range(HEADS):
            cols = pl.ds(h * HEAD_DIM, HEAD_DIM)
            st = state[h]
            st_out_ref[0, h] = st
            bh = b2_vmem[:, cols]
            kh = k_vmem[:, cols]
            vh = jnp.where(valid, v_ref[0, :, cols], 0.0)
            qe = q_ref[0, :, cols] * jnp.exp2(bh)
            a = _intra_scores(q_view, k_vmem, b2_vmem, h * HEAD_DIM).astype(BF16)
            a_out_ref[0, h] = a
            o_ref[:, cols] = _dot_nt(qe, st) + _dot(a, vh)
            b_last = b2_vmem[CHUNK - 1:CHUNK, cols]
            kd = kh * jnp.exp2(b_last - bh)
            state[h] = st * jnp.exp2(b_last) + _dot_tn(vh, kd)

        @pl.when(n == n_chunks - 1)
        def _():
            for cp in arrivals:
                cp.wait_recv()
            for cp in sends:
                cp.wait_send()
            own.wait()

    blk = lambda c: pl.BlockSpec((1, CHUNK, width), lambda n, c=c: (c, n, 0))
    return pl.pallas_call(
        body, name="hgrn_forward",
        out_shape=(jax.ShapeDtypeStruct((n_rows, width), F32),
                   jax.ShapeDtypeStruct((n_chunks, HEADS, HEAD_DIM, HEAD_DIM), F32),
                   jax.ShapeDtypeStruct((n_chunks, HEADS, CHUNK, CHUNK), BF16),
                   jax.ShapeDtypeStruct((N_DEV,) + w_rest.shape, w_rest.dtype)),
        grid=(n_chunks,),
        in_specs=[blk(0), blk(1), blk(2), pl.BlockSpec((2, width), lambda n: (0, 0)), ANY],
        out_specs=(pl.BlockSpec((CHUNK, width), lambda n: (n, 0)),
                   pl.BlockSpec((1, HEADS, HEAD_DIM, HEAD_DIM), lambda n: (n, 0, 0, 0)),
                   pl.BlockSpec((1, HEADS, CHUNK, CHUNK), lambda n: (n, 0, 0, 0)), ANY),
        scratch_shapes=[pltpu.VMEM((HEADS, HEAD_DIM, HEAD_DIM), F32), pltpu.VMEM((CHUNK, width), F32),
                        pltpu.VMEM((CHUNK, width), F32)] + list(GATHER_SEMS),
        compiler_params=_params(("arbitrary",)),
    )(p, p, p, lb_logits, w_rest)


def _hgrn_backward(p, lb_logits, states, scores, d_o, g_slabs, first_owner, g_rows):
    n_rows = p.shape[1]
    n_chunks = n_rows // CHUNK
    width = HEADS * HEAD_DIM

    def body(q_ref, fz_ref, v_ref, lb_ref, st_ref, a_ref, do_ref, gs_hbm, gr_hbm, dp_ref, dbias_ref, dlb_ref, rs_hbm, rr_hbm,
             dstate, k_vmem, b2_vmem, *sems):
        step = pl.program_id(0)
        n = n_chunks - 1 - step
        start_slabs, finish_slabs = _scatter_slabs(gs_hbm, first_owner, rs_hbm, *sems[:3])
        start_rows, finish_rows = _scatter_rows(gr_hbm, rr_hbm, *sems[3:])

        @pl.when(step == 0)
        def _():
            dstate[...] = jnp.zeros_like(dstate)
            dbias_ref[...] = jnp.zeros_like(dbias_ref)
            dlb_ref[...] = jnp.zeros_like(dlb_ref)
            start_slabs()
            start_rows()

        rows = n * CHUNK + lax.broadcasted_iota(jnp.int32, (CHUNK, 1), 0)
        valid = rows >= PAD_ROWS
        lb = _lower_bound(lb_ref)
        sig, nsig, f, g2, k = _chunk_gates(fz_ref[0], lb, valid)
        k_vmem[...] = k
        b2_vmem[...] = _tri_dot(_tri(CHUNK), g2)
        rows_c = lax.broadcasted_iota(jnp.int32, (CHUNK, 1), 0)
        rows_s = lax.broadcasted_iota(jnp.int32, (SUB, 1), 0)
        lanes_c = lax.broadcasted_iota(jnp.int32, (1, CHUNK), 1)
        causal = lax.broadcasted_iota(jnp.int32, (CHUNK, CHUNK), 0) >= lax.broadcasted_iota(jnp.int32, (CHUNK, CHUNK), 1)
        tri_up = _tri(CHUNK, upper=True)
        for h in range(HEADS):
            cols = pl.ds(h * HEAD_DIM, HEAD_DIM)
            st = st_ref[0, h]
            dst = dstate[h]
            qh = q_ref[0, :, cols]
            bh = b2_vmem[:, cols]
            kh = k_vmem[:, cols]
            vh = jnp.where(valid, v_ref[0, :, cols], 0.0)
            doh = do_ref[:, cols]
            eb = jnp.exp2(bh)
            qe = qh * eb
            b_last = b2_vmem[CHUNK - 1:CHUNK, cols]
            e_last = jnp.exp2(b_last)
            decay_k = jnp.exp2(b_last - bh)
            kd = kh * decay_k
            dqe = _dot(doh, st)
            da = jnp.where(causal, _dot_nt(doh, vh), 0.0)
            dv = _dot_tn(a_ref[0, h], doh) + _dot_nt(kd, dst)
            dkd = _dot(vh, dst)
            dstate[h] = dst * e_last + _dot_tn(doh, qe)
            db_last = (jnp.sum(dst * st, axis=0, keepdims=True) * e_last
                       + jnp.sum(dkd * kd, axis=0, keepdims=True))
            dq_blocks, dk_blocks = [], []
            dk_earlier = jnp.zeros((CHUNK, HEAD_DIM), F32)
            for i in range(N_SUB):
                lo = i * SUB
                qi = q_ref[0, lo:lo + SUB, cols]
                bi = b2_vmem[lo:lo + SUB, cols]
                da_i = da[lo:lo + SUB, :]
                if i == 0:
                    dq_i = jnp.zeros((SUB, HEAD_DIM), F32)
                else:
                    ref_i = b2_vmem[lo:lo + 1, cols]
                    eq = jnp.exp2(bi - ref_i)
                    ek = jnp.exp2(ref_i - b2_vmem[0:lo, cols])
                    later = jnp.zeros((CHUNK - lo, HEAD_DIM), F32)
                    kt = jnp.concatenate([k_vmem[0:lo, cols] * ek, later], axis=0)
                    dq_i = _dot(da_i, kt) * eq
                    dk_earlier = dk_earlier + jnp.concatenate([_dot_tn(da_i, qi * eq)[0:lo] * ek, later], axis=0)
                dk_i = jnp.zeros((SUB, HEAD_DIM), F32)
                for s in range(SUB):
                    b_s = b2_vmem[lo + s:lo + s + 1, cols]
                    k_s = k_vmem[lo + s:lo + s + 1, cols]
                    w = jnp.exp2(jnp.minimum(bi - b_s, 0.0))
                    da_col = jnp.sum(jnp.where(lanes_c == lo + s, da_i, 0.0), axis=-1, keepdims=True)
                    gw = da_col * w
                    dq_i = dq_i + gw * k_s
                    dk_i = jnp.where(rows_s == s, jnp.sum(gw * qi, axis=0, keepdims=True), dk_i)
                dq_blocks.append(dq_i)
                dk_blocks.append(dk_i)
            dq_intra = jnp.concatenate(dq_blocks, axis=0)
            dk_intra = jnp.concatenate(dk_blocks, axis=0) + dk_earlier
            dq = dqe * eb + dq_intra
            dk = dkd * decay_k + dk_intra
            db = dqe * qe - dkd * kd + qh * dq_intra - kh * dk_intra
            db = db + jnp.where(rows_c == CHUNK - 1, db_last, 0.0)
            dg = _tri_dot(tri_up, db)
            fh = f[:, h * HEAD_DIM:(h + 1) * HEAD_DIM]
            sh = sig[:, h * HEAD_DIM:(h + 1) * HEAD_DIM]
            nh = nsig[:, h * HEAD_DIM:(h + 1) * HEAD_DIM]
            lbh = lb[:, h * HEAD_DIM:(h + 1) * HEAD_DIM]
            df = jnp.where(valid, dg / fh - dk, 0.0)
            dfz = df * (1.0 - lbh) * sh * nh
            dq = jnp.where(valid, dq, 0.0)
            dv = jnp.where(valid, dv, 0.0)
            dlb_ref[:, cols] += jnp.sum(df * nh, axis=0, keepdims=True)
            dp_ref[0, :, cols] = dq.astype(BF16)
            dp_ref[1, :, cols] = dfz.astype(BF16)
            dp_ref[2, :, cols] = dv.astype(BF16)
            dbias_ref[0, :, cols] += jnp.sum(dq, axis=0, keepdims=True)
            dbias_ref[1, :, cols] += jnp.sum(dfz, axis=0, keepdims=True)
            dbias_ref[2, :, cols] += jnp.sum(dv, axis=0, keepdims=True)

        @pl.when(step == n_chunks - 1)
        def _():
            finish_slabs()
            finish_rows()

    rev = lambda s: n_chunks - 1 - s
    blk = lambda c: pl.BlockSpec((1, CHUNK, width), lambda s, c=c: (c, rev(s), 0))
    return pl.pallas_call(
        body, name="hgrn_backward",
        out_shape=(jax.ShapeDtypeStruct((3, n_rows, width), BF16),
                   jax.ShapeDtypeStruct((3, 1, width), F32),
                   jax.ShapeDtypeStruct((1, width), F32),
                   jax.ShapeDtypeStruct((N_DEV,) + g_slabs.shape[1:], g_slabs.dtype),
                   jax.ShapeDtypeStruct((N_DEV, N_ROW_GRADS, 128, g_rows.shape[2]), g_rows.dtype)),
        grid=(n_chunks,),
        in_specs=[blk(0), blk(1), blk(2), pl.BlockSpec((2, width), lambda s: (0, 0)),
                  pl.BlockSpec((1, HEADS, HEAD_DIM, HEAD_DIM), lambda s: (rev(s), 0, 0, 0)),
                  pl.BlockSpec((1, HEADS, CHUNK, CHUNK), lambda s: (rev(s), 0, 0, 0)),
                  pl.BlockSpec((CHUNK, width), lambda s: (rev(s), 0)), ANY, ANY],
        out_specs=(pl.BlockSpec((3, CHUNK, width), lambda s: (0, rev(s), 0)),
                   pl.BlockSpec((3, 1, width), lambda s: (0, 0, 0)),
                   pl.BlockSpec((1, width), lambda s: (0, 0)), ANY, ANY),
        scratch_shapes=[pltpu.VMEM((HEADS, HEAD_DIM, HEAD_DIM), F32), pltpu.VMEM((CHUNK, width), F32),
                        pltpu.VMEM((CHUNK, width), F32)] + list(SCATTER_SEMS) + list(SCATTER_ROWS_SEMS),
        compiler_params=_params(("arbitrary",)),
    )(p, p, p, lb_logits, states, scores, d_o, g_slabs, g_rows)


def _sigmoid_and_complement(x):
    s = 0.5 * jnp.tanh(0.5 * x) + 0.5
    return s, 1.0 - s


def _silu_and_grad(x):
    s, ns = _sigmoid_and_complement(x)
    return x * s, s * (1.0 + x * ns)


def _tail(p, o, zp, tgt, hg_norm_w, pool_w, pool_scale, w_down_hg, w_down_pool, w_out, final_norm_w):
    n_rows = zp.shape[0]
    tr = _row_tile(n_rows, 208)
    nt = n_rows // tr
    ext = tr + HALO
    n_groups = len(POOL_WINDOWS)

    def body(o_ref, ghg_ref, u_ref, gpool_ref, mhg_ref, mpool_ref, uhalo_ref, z_ref, tgt_hbm,
             hgw_ref, pw_ref, ps_ref, wdh_ref, wdp_ref, wout_ref, fnw_ref,
             do_ref, dp_ref, dz2_ref, lhs_ref, rhs_ref,
             dbias_ref, dhgw_ref, dpw_ref, dps_ref, dfnw_ref, loss_ref, halo_vmem, tgt_buf, tgt_sems):
        step = pl.program_id(0)
        ti = nt - 1 - step

        def target_rows(tile, slot, act):
            @pl.when(tile == 0)
            def _():
                cp = pltpu.make_async_copy(tgt_hbm.at[pl.ds(0, tr - CHUNK), :], tgt_buf.at[slot, pl.ds(CHUNK, tr - CHUNK), :],
                                           tgt_sems.at[slot])
                getattr(cp, act)()

            @pl.when(tile > 0)
            def _():
                cp = pltpu.make_async_copy(tgt_hbm.at[pl.ds(tile * tr - CHUNK, tr), :], tgt_buf.at[slot], tgt_sems.at[slot])
                getattr(cp, act)()

        @pl.when(step == 0)
        def _():
            halo_vmem[...] = jnp.zeros_like(halo_vmem)
            for r in (dbias_ref, dhgw_ref, dpw_ref, dps_ref, dfnw_ref, loss_ref):
                r[...] = jnp.zeros_like(r)
            if nt <= 2:
                tgt_buf[(nt - 1) % 2, 0:CHUNK, :] = jnp.zeros((CHUNK, D_MODEL), F32)
            target_rows(ti, 0, "start")

        @pl.when(ti > 0)
        def _():
            target_rows(ti - 1, (step + 1) % 2, "start")

        rows = ti * tr + lax.broadcasted_iota(jnp.int32, (tr, 1), 0)
        valid = rows >= PAD_ROWS
        in_loss = rows >= CHUNK
        count_pos = jnp.maximum(rows - PAD_ROWS + 1, 1).astype(F32)

        o = o_ref[...]
        hgw = hgw_ref[...]
        inv_o, on_parts = [], []
        for h in range(HEADS):
            oh = o[:, h * HEAD_DIM:(h + 1) * HEAD_DIM]
            r = lax.rsqrt(jnp.mean(oh * oh, axis=-1, keepdims=True) + EPS)
            inv_o.append(r)
            on_parts.append(oh * r)
        o_hat = jnp.concatenate(on_parts, axis=1)
        o_n = o_hat * hgw
        g_hg = ghg_ref[0]
        silu_hg, dsilu_hg = _silu_and_grad(g_hg)
        a_hg = o_n * silu_hg
        y_hg = _dot(a_hg, wdh_ref[...])

        u = jnp.where(valid, u_ref[0], 0.0)
        u_prev = jnp.where(ti > 0, uhalo_ref[0], 0.0)
        u_ext = jnp.concatenate([u_prev, u], axis=0)
        pooled_parts, mixed_parts, inv_cnt = [], [], []
        for gi, win in enumerate(POOL_WINDOWS):
            lanes = slice(gi * POOL_GDIM, (gi + 1) * POOL_GDIM)
            s = u_ext[:, lanes]
            shift = 1
            while shift < win:
                s = s + pltpu.roll(s, shift, 0)
                shift *= 2
            ic = 1.0 / jnp.minimum(count_pos, float(win))
            inv_cnt.append(ic)
            pooled = s[HALO:] * ic - u[:, lanes]
            pooled_parts.append(pooled)
            mixed_parts.append(_dot(pooled, pw_ref[gi]))
        mixed = jnp.concatenate(mixed_parts, axis=1)
        ps = ps_ref[...]
        g_pool = gpool_ref[0]
        silu_pool, dsilu_pool = _silu_and_grad(g_pool)
        a_pool = mixed * ps * silu_pool
        y_pool = _dot(a_pool, wdp_ref[...])

        m_hg, m_pool = mhg_ref[0], mpool_ref[0]
        s_hg, ns_hg = _sigmoid_and_complement(m_hg)
        s_pool, ns_pool = _sigmoid_and_complement(m_pool)
        merged = s_hg * y_hg + s_pool * y_pool
        z2 = z_ref[...] + _dot(merged, wout_ref[...])
        r2 = lax.rsqrt(jnp.mean(z2 * z2, axis=-1, keepdims=True) + EPS)
        n2 = z2 * r2
        fnw = fnw_ref[...]
        target_rows(ti, step % 2, "wait")
        err = jnp.where(in_loss, n2 * fnw - tgt_buf[step % 2], 0.0)
        loss_ref[...] += jnp.sum(jnp.sum(err * err, axis=0, keepdims=True), axis=1, keepdims=True) * (0.5 / D_MODEL)
        dy = err * (1.0 / D_MODEL)

        dfnw_ref[...] += jnp.sum(dy * n2, axis=0, keepdims=True)
        gy = dy * fnw
        dz2 = r2 * (gy - n2 * jnp.mean(gy * n2, axis=-1, keepdims=True))
        dmerged = _dot_nt(dz2, wout_ref[...])
        dy_hg = s_hg * dmerged
        dy_pool = s_pool * dmerged
        dm_hg = dmerged * y_hg * s_hg * ns_hg
        dm_pool = dmerged * y_pool * s_pool * ns_pool
        da_hg = _dot_nt(dy_hg, wdh_ref[...])
        da_pool = _dot_nt(dy_pool, wdp_ref[...])

        d_on = da_hg * silu_hg
        dg_hg = da_hg * o_n * dsilu_hg
        dhgw_ref[...] += jnp.sum(d_on * o_hat, axis=0, keepdims=True)
        gyo = d_on * hgw
        do_parts = []
        for h in range(HEADS):
            lanes = slice(h * HEAD_DIM, (h + 1) * HEAD_DIM)
            gh, nh = gyo[:, lanes], o_hat[:, lanes]
            do_parts.append(inv_o[h] * (gh - nh * jnp.mean(gh * nh, axis=-1, keepdims=True)))
        do_ref[...] = jnp.concatenate(do_parts, axis=1)

        dmixed = da_pool * ps * silu_pool
        dps_ref[...] += jnp.sum(da_pool * mixed * silu_pool, axis=0, keepdims=True)
        dg_pool = da_pool * mixed * ps * dsilu_pool
        du_parts = []
        for gi, win in enumerate(POOL_WINDOWS):
            lanes = slice(gi * POOL_GDIM, (gi + 1) * POOL_GDIM)
            dmx = dmixed[:, lanes]
            dpooled = _dot_nt(dmx, pw_ref[gi])
            dpw_ref[gi] += _dot_tn(pooled_parts[gi], dmx)
            dpt = dpooled * inv_cnt[gi]
            s = jnp.concatenate([dpt, halo_vmem[:, lanes]], axis=0)
            shift = 1
            while shift < win:
                s = s + pltpu.roll(s, ext - shift, 0)
                shift *= 2
            du_parts.append(s[:tr] - dpooled)
            halo_vmem[:, lanes] = dpt[:HALO]
        du = jnp.where(valid, jnp.concatenate(du_parts, axis=1), 0.0)

        for c, val in enumerate((dg_hg, du, dg_pool, dm_hg, dm_pool)):
            dp_ref[c] = val.astype(BF16)
            dbias_ref[c] += jnp.sum(val, axis=0, keepdims=True)
        dz2_ref[...] = dz2
        for c, (lhs, rhs) in enumerate(((merged, dz2), (a_hg, dy_hg), (a_pool, dy_pool))):
            lhs_ref[c] = lhs.astype(BF16)
            rhs_ref[c] = rhs.astype(BF16)

    rev = lambda s: nt - 1 - s
    rowblk = pl.BlockSpec((tr, D_MODEL), lambda s: (rev(s), 0))
    pblk = lambda c: pl.BlockSpec((1, tr, 1024), lambda s, c=c: (c, rev(s), 0))
    halo_blk = pl.BlockSpec((1, HALO, 1024), lambda s: (4, jnp.maximum(rev(s) * (tr // HALO) - 1, 0), 0))
    full = lambda shape: pl.BlockSpec(shape, lambda s: (0,) * len(shape))
    vec = full((1, D_MODEL))
    mat = full((D_MODEL, D_MODEL))
    act3 = jax.ShapeDtypeStruct((3, n_rows, D_MODEL), BF16)
    act3_blk = pl.BlockSpec((3, tr, D_MODEL), lambda s: (0, rev(s), 0))
    return pl.pallas_call(
        body, name="tail_forward_backward",
        out_shape=(jax.ShapeDtypeStruct((n_rows, D_MODEL), F32),
                   jax.ShapeDtypeStruct((5, n_rows, 1024), BF16),
                   jax.ShapeDtypeStruct((n_rows, D_MODEL), F32),
                   act3, act3,
                   jax.ShapeDtypeStruct((5, 1, 1024), F32),
                   jax.ShapeDtypeStruct((1, D_MODEL), F32),
                   jax.ShapeDtypeStruct((n_groups, POOL_GDIM, POOL_GDIM), F32),
                   jax.ShapeDtypeStruct((1, D_MODEL), F32),
                   jax.ShapeDtypeStruct((1, D_MODEL), F32),
                   jax.ShapeDtypeStruct((1, 1), F32)),
        grid=(nt,),
        in_specs=[rowblk, pblk(3), pblk(4), pblk(5), pblk(6), pblk(7), halo_blk, rowblk, ANY,
                  vec, full((n_groups, POOL_GDIM, POOL_GDIM)), vec, mat, mat, mat, vec],
        out_specs=(rowblk, pl.BlockSpec((5, tr, 1024), lambda s: (0, rev(s), 0)), rowblk,
                   act3_blk, act3_blk,
                   full((5, 1, 1024)), vec, full((n_groups, POOL_GDIM, POOL_GDIM)), vec, vec, full((1, 1))),
        scratch_shapes=[pltpu.VMEM((HALO, D_MODEL), F32), pltpu.VMEM((2, tr, D_MODEL), F32), pltpu.SemaphoreType.DMA((2,))],
        compiler_params=_params(("arbitrary",)),
    )(o, p, p, p, p, p, p, zp, tgt, hg_norm_w, pool_w, pool_scale, w_down_hg, w_down_pool, w_out, final_norm_w)


def _in_projection_backward(dp_a, dp_b, w_blocks, zp, dz2, norm_w, chip_sums):
    n_rows = zp.shape[0]
    tr = _row_tile(n_rows, 416)
    nt = n_rows // tr
    na, nb = dp_a.shape[0], dp_b.shape[0]

    def body(dpa_ref, dpb_ref, w_hbm, z_ref, dz2_ref, nw_ref, gs_hbm, gx_hbm, head_ref, dnw_ref, rs_hbm,
             w_vmem, sem, dz_buf, gx_sems, *sems):
        i = pl.program_id(0)
        start_slabs, finish_slabs = _scatter_low(gs_hbm, rs_hbm, *sems)

        def wait_rows_out(tile):
            @pl.when(tile == 0)
            def _():
                pltpu.make_async_copy(dz_buf.at[0, pl.ds(CHUNK, tr - CHUNK), :], gx_hbm.at[pl.ds(0, tr - CHUNK), :],
                                      gx_sems.at[0]).wait()

            @pl.when(tile > 0)
            def _():
                pltpu.make_async_copy(dz_buf.at[tile % 2], gx_hbm.at[pl.ds(tile * tr - CHUNK, tr), :],
                                      gx_sems.at[tile % 2]).wait()

        @pl.when(i == 0)
        def _():
            start_slabs()
            cp = pltpu.make_async_copy(w_hbm, w_vmem, sem)
            cp.start()
            cp.wait()
            dnw_ref[...] = jnp.zeros_like(dnw_ref)

        dh = jnp.zeros((tr, D_MODEL), F32)
        half_cols = w_vmem.shape[-1]
        for j in range(na + nb):
            dp_ref, jj = (dpa_ref, j) if j < na else (dpb_ref, j - na)
            for half in range(2):
                dh = dh + _dot_nt(dp_ref[jj, :, half * half_cols:(half + 1) * half_cols], w_vmem[j, half])
        z = z_ref[...]
        r = lax.rsqrt(jnp.mean(z * z, axis=-1, keepdims=True) + EPS)
        n1 = z * r
        dnw_ref[...] += jnp.sum(dh * n1, axis=0, keepdims=True)
        gh = dh * nw_ref[...]
        dz = dz2_ref[...] + r * (gh - n1 * jnp.mean(gh * n1, axis=-1, keepdims=True))

        @pl.when(i >= 2)
        def _():
            wait_rows_out(i - 2)

        dz_buf[i % 2] = dz

        @pl.when(i == 0)
        def _():
            head_ref[...] = dz[0:CHUNK]
            pltpu.make_async_copy(dz_buf.at[0, pl.ds(CHUNK, tr - CHUNK), :], gx_hbm.at[pl.ds(0, tr - CHUNK), :],
                                  gx_sems.at[0]).start()

        @pl.when(i > 0)
        def _():
            pltpu.make_async_copy(dz_buf.at[i % 2], gx_hbm.at[pl.ds(i * tr - CHUNK, tr), :], gx_sems.at[i % 2]).start()

        @pl.when(i == nt - 1)
        def _():
            if nt >= 2:
                wait_rows_out(i - 1)
            wait_rows_out(i)
            finish_slabs()

    rowblk = pl.BlockSpec((tr, D_MODEL), lambda i: (i, 0))
    vec = pl.BlockSpec((1, D_MODEL), lambda i: (0, 0))
    return pl.pallas_call(
        body, name="in_projection_backward",
        out_shape=(jax.ShapeDtypeStruct((n_rows - CHUNK, D_MODEL), F32), jax.ShapeDtypeStruct((CHUNK, D_MODEL), F32),
                   jax.ShapeDtypeStruct((1, D_MODEL), F32),
                   jax.ShapeDtypeStruct((4,) + chip_sums.shape[1:], chip_sums.dtype)),
        grid=(nt,),
        in_specs=[pl.BlockSpec((na, tr, 1024), lambda i: (0, i, 0)), pl.BlockSpec((nb, tr, 1024), lambda i: (0, i, 0)),
                  ANY, rowblk, rowblk, vec, ANY],
        out_specs=(ANY, pl.BlockSpec((CHUNK, D_MODEL), lambda i: (0, 0)), vec, ANY),
        scratch_shapes=[pltpu.VMEM(w_blocks.shape, w_blocks.dtype), pltpu.SemaphoreType.DMA(()),
                        pltpu.VMEM((2, tr, D_MODEL), F32), pltpu.SemaphoreType.DMA((2,)),
                        pltpu.SemaphoreType.DMA((2,)), pltpu.SemaphoreType.DMA((3,)), pltpu.SemaphoreType.DMA(())],
        compiler_params=_params(("arbitrary",)),
    )(dp_a, dp_b, w_blocks, zp, dz2, norm_w, chip_sums)


def _weight_grad(xs, ys, name):
    shared = xs.ndim == 2
    n_rows, m = xs.shape[-2:]
    nb, _, n = ys.shape
    tk = _row_tile(n_rows, 4160)
    n_k = n_rows // tk

    def body(x_ref, y_ref, o_ref, acc):
        k = pl.program_id(1)

        @pl.when(k == 0)
        def _():
            acc[...] = jnp.zeros_like(acc)

        acc[...] += _dot_tn(x_ref[...] if shared else x_ref[0], y_ref[0])

        @pl.when(k == n_k - 1)
        def _():
            o_ref[0] = acc[...].astype(o_ref.dtype)

    x_spec = pl.BlockSpec((tk, m), lambda j, k: (k, 0)) if shared else pl.BlockSpec((1, tk, m), lambda j, k: (j, k, 0))
    return pl.pallas_call(
        body, name=name,
        out_shape=jax.ShapeDtypeStruct((nb, m, n), BF16),
        grid=(nb, n_k),
        in_specs=[x_spec, pl.BlockSpec((1, tk, n), lambda j, k: (j, k, 0))],
        out_specs=pl.BlockSpec((1, m, n), lambda j, k: (j, 0, 0)),
        scratch_shapes=[pltpu.VMEM((m, n), F32)],
        compiler_params=_params(("arbitrary", "arbitrary")),
    )(xs, ys)


def kernel(x, meta_tokens, norm_w, w_in, b_in, lb_logits, hg_norm_w, pool_w, pool_scale, w_down_hg, w_down_pool, w_out, final_norm_w, loss_target, m_meta_tokens, m_norm_w, m_w_in, m_b_in, m_lb_logits, m_hg_norm_w, m_pool_w, m_pool_scale, m_w_down_hg, m_w_down_pool, m_w_out, m_final_norm_w, v_meta_tokens, v_norm_w, v_w_in, v_b_in, v_lb_logits, v_hg_norm_w, v_pool_w, v_pool_scale, v_w_down_hg, v_w_down_pool, v_w_out, v_final_norm_w):
    seq = x.shape[1]

    meta_full = _all_gather_small(meta_tokens).transpose(1, 0, 2).reshape(N_META, D_MODEL)
    w_rest = jnp.concatenate([w_down_hg[0].astype(BF16), w_down_pool[0].astype(BF16), w_out[0].astype(BF16),
                              pool_w[0].astype(BF16).reshape(32, 1024)], axis=0)

    zp = jnp.concatenate([jnp.zeros((PAD_ROWS, D_MODEL), F32), meta_full, x[0]], axis=0)
    p, h, w_blocks = _in_projection(zp, norm_w, w_in[0].astype(BF16), b_in.reshape(N_COLBLK, 1, 1024), _gather_units())
    o, states, scores, rest = _hgrn_forward(p, lb_logits, w_rest)
    wdh = rest[:, REST_W_DOWN_HG:REST_W_DOWN_HG + 128].reshape(1024, 1024)
    wdp = rest[:, REST_W_DOWN_POOL:REST_W_DOWN_POOL + 128].reshape(1024, 1024)
    wout = rest[:, REST_W_OUT:REST_W_OUT + 128].reshape(1024, 1024)
    pw = rest[:, REST_POOL_W:REST_POOL_W + 32].reshape(N_DEV, 4, 32, 256).transpose(1, 0, 2, 3).reshape(4, 256, 256)
    (d_o, dp_b, dz2, grad_lhs, grad_rhs, dbias_b, d_hgw, d_pw, d_ps, d_fnw, loss_part) = _tail(
        p, o, zp, loss_target[0], hg_norm_w, pw, pool_scale, wdh, wdp, wout, final_norm_w.reshape(1, D_MODEL))
    n_a = N_COLBLK - dp_b.shape[0]
    g_hi = _weight_grad(h, dp_b, "weight_grad_in_hi")
    g_rows = _weight_grad(grad_lhs, grad_rhs, "weight_grad_rows")
    dp_a, dbias_a, d_lb, recv_hi, recv_rows = _hgrn_backward(p, lb_logits, states, scores, d_o, g_hi, n_a, g_rows)
    assert n_a == LOW_OWNERS
    g_lo = _weight_grad(h, dp_a, "weight_grad_in_lo")
    dz_seq, dz_head, d_nw, recv_lo = _in_projection_backward(dp_a, dp_b, w_blocks, zp, dz2, norm_w, _pair_reduce_low(g_lo))

    lb = jax.nn.sigmoid(lb_logits[0:1] - lb_logits[1:2])
    d_l0 = d_lb * lb * (1.0 - lb)
    replicated = jnp.concatenate([dbias_a.reshape(3, 1024), dbias_b.reshape(5, 1024), d_nw, d_l0, -d_l0, d_hgw, d_ps, d_fnw,
                                  jnp.pad(loss_part, ((0, MISC_ROWS - MISC_LOSS - 1), (0, 1023)))], axis=0)
    d_meta = dz_head[PAD_ROWS:CHUNK].reshape(N_META, N_DEV, 128).transpose(1, 0, 2)
    d_pw_blocks = d_pw.reshape(4, N_DEV, 32, 256).transpose(1, 0, 2, 3).reshape(N_DEV, 32, 1024)
    g_misc = jnp.concatenate([d_pw_blocks, jnp.pad(d_meta, ((0, 0), (0, 0), (0, 1024 - 128))),
                              jnp.broadcast_to(replicated[None], (N_DEV, 16, 1024))], axis=1)

    as_rows = lambda t, n: t.reshape(n, 1024)
    small = [(MISC_POOL_W, 1024, tuple(as_rows(t, 32) for t in (pool_w, m_pool_w, v_pool_w))),
             (MISC_META, 128, (meta_tokens, m_meta_tokens, v_meta_tokens)),
             (MISC_B_IN, 1024, tuple(as_rows(t, 8) for t in (b_in, m_b_in, v_b_in))),
             (MISC_NORM_W, 1024, (norm_w, m_norm_w, v_norm_w)),
             (MISC_LB, 1024, (lb_logits, m_lb_logits, v_lb_logits)),
             (MISC_HG_NORM_W, 1024, (hg_norm_w, m_hg_norm_w, v_hg_norm_w)),
             (MISC_POOL_SCALE, 1024, (pool_scale, m_pool_scale, v_pool_scale)),
             (MISC_FINAL_NORM_W, 1024, tuple(as_rows(t, 1) for t in (final_norm_w, m_final_norm_w, v_final_norm_w)))]
    res, loss = _finish(recv_hi, recv_lo, n_a, recv_rows, g_misc, (w_in, m_w_in, v_w_in),
                  [(w_out, m_w_out, v_w_out), (w_down_hg, m_w_down_hg, v_w_down_hg), (w_down_pool, m_w_down_pool, v_w_down_pool)],
                  small)
    r_w_in, r_w_out, r_wdh, r_wdp, r_pw, r_meta, r_b_in, r_nw, r_lb, r_hgw, r_ps, r_fnw = res
    grad_x = dz_seq.reshape(1, seq, D_MODEL)
    per_kind = [(r_meta[k], r_nw[k], r_w_in[k], r_b_in[k].reshape(1, 8192), r_lb[k], r_hgw[k], r_pw[k].reshape(1, 4, 32, 256),
                 r_ps[k], r_wdh[k], r_wdp[k], r_w_out[k], r_fnw[k].reshape(1024)) for k in range(4)]
    return (loss, grad_x, *per_kind[0], *per_kind[1], *per_kind[2], *per_kind[3])
```

```python
import functools

import jax
import jax.numpy as jnp
from jax import lax
from jax.experimental import pallas as pl
from jax.experimental.pallas import tpu as pltpu

F32 = jnp.float32
BF16 = jnp.bfloat16

D_MODEL = 1024
N_META = 16
HEADS = 8
HEAD_DIM = 128
CHUNK = 64
SUB = 8
N_SUB = CHUNK // SUB
PAD_ROWS = CHUNK - N_META
POOL_WINDOWS = (2, 4, 8, 16)
POOL_GDIM = D_MODEL // len(POOL_WINDOWS)
HALO = 16
EPS = 1e-6
N_DEV = 8
N_COLBLK = 8
ADAM_LR, ADAM_B1, ADAM_B2, ADAM_EPS, ADAM_WD, ADAM_STEP = 0.001, 0.9, 0.999, 1e-08, 0.01, 10

VMEM_LIMIT = 56 * 1024 * 1024
MESH = pl.DeviceIdType.MESH
ANY = pl.BlockSpec(memory_space=pl.ANY)

REST_W_DOWN_HG = 0
REST_W_DOWN_POOL = 128
REST_W_OUT = 256
REST_POOL_W = 384
MISC_POOL_W = 0
MISC_META = 32
MISC_B_IN = 48
MISC_NORM_W = 56
MISC_LB = 57
MISC_HG_NORM_W = 59
MISC_POOL_SCALE = 60
MISC_FINAL_NORM_W = 61
MISC_LOSS = 62
MISC_ROWS = 64


def _params(sem=None):
    return pltpu.CompilerParams(dimension_semantics=sem, vmem_limit_bytes=VMEM_LIMIT)


def _row_tile(n_rows, prefer):
    best = 16
    for t in range(16, prefer + 1, 16):
        if n_rows % t == 0:
            best = t
    return best


def _sigmoid_pair(x):
    e = jnp.exp(-jnp.abs(x))
    r = 1.0 / (1.0 + e)
    er = e * r
    pos = x >= 0
    return jnp.where(pos, r, er), jnp.where(pos, er, r)


def _dot(a, b):
    return jnp.dot(a.astype(BF16), b.astype(BF16), preferred_element_type=F32)


def _dot_nt(a, b):
    return lax.dot_general(a.astype(BF16), b.astype(BF16), (((1,), (1,)), ((), ())), preferred_element_type=F32)


def _dot_tn(a, b):
    return lax.dot_general(a.astype(BF16), b.astype(BF16), (((0,), (0,)), ((), ())), preferred_element_type=F32)


def _device_index(px, py, pc):
    return 4 * px + 2 * py + pc


def _direct_gather(src_ref, dst_ref, send_sems, recv_sems, local_sem):
    x, y, c = lax.axis_index("x"), lax.axis_index("y"), lax.axis_index("c")
    own = pltpu.make_async_copy(src_ref, dst_ref.at[_device_index(x, y, c)], local_sem)
    sends, arrivals = [], []
    for k in range(1, N_DEV):
        peer = (1 - x if k & 4 else x, 1 - y if k & 2 else y, 1 - c if k & 1 else c)
        for slot, out in ((_device_index(x, y, c), sends), (_device_index(*peer), arrivals)):
            out.append(pltpu.make_async_remote_copy(
                src_ref=src_ref, dst_ref=dst_ref.at[slot], send_sem=send_sems.at[k - 1], recv_sem=recv_sems.at[k - 1],
                device_id=peer, device_id_type=MESH))
    return own, sends, arrivals


GATHER_SEMS = [pltpu.SemaphoreType.DMA((N_DEV - 1,)), pltpu.SemaphoreType.DMA((N_DEV - 1,)), pltpu.SemaphoreType.DMA(())]


def _all_gather_small(block):
    def body(x_ref, out_ref, send_sems, recv_sems, local_sem):
        own, sends, arrivals = _direct_gather(x_ref, out_ref, send_sems, recv_sems, local_sem)
        own.start()
        for cp in sends:
            cp.start()
        for cp in arrivals:
            cp.wait_recv()
        for cp in sends:
            cp.wait_send()
        own.wait()

    return pl.pallas_call(
        body, name="all_gather_meta",
        out_shape=jax.ShapeDtypeStruct((N_DEV,) + block.shape, block.dtype),
        in_specs=[ANY], out_specs=ANY, scratch_shapes=list(GATHER_SEMS),
    )(block)


def _peer(k):
    x, y, c = lax.axis_index("x"), lax.axis_index("y"), lax.axis_index("c")
    return (1 - x if k & 4 else x, 1 - y if k & 2 else y, 1 - c if k & 1 else c)


def _me():
    return _device_index(lax.axis_index("x"), lax.axis_index("y"), lax.axis_index("c"))


def _remote(src, dst, send_sem, recv_sem, peer_bits):
    return pltpu.make_async_remote_copy(src_ref=src, dst_ref=dst, send_sem=send_sem, recv_sem=recv_sem,
                                        device_id=_peer(peer_bits), device_id_type=MESH)


N_ROW_GRADS = 3
SCATTER_SEMS = [pltpu.SemaphoreType.DMA((N_DEV - 1,)), pltpu.SemaphoreType.DMA((N_DEV - 1,)), pltpu.SemaphoreType.DMA(())]
SCATTER_ROWS_SEMS = [pltpu.SemaphoreType.DMA((7 * N_ROW_GRADS,)), pltpu.SemaphoreType.DMA((7 * N_ROW_GRADS,)),
                     pltpu.SemaphoreType.DMA((N_ROW_GRADS,))]


def _scatter_slabs(g_ref, first, recv_ref, send_sems, recv_sems, local_sem):
    n = g_ref.shape[0]
    me = _me()

    def each(on_send, on_local, on_arrival):
        for kk in range(1, N_DEV):
            peer = jnp.bitwise_xor(me, kk)

            @pl.when((peer >= first) & (peer < first + n))
            def _(kk=kk, peer=peer):
                on_send(_remote(g_ref.at[peer - first], recv_ref.at[me], send_sems.at[kk - 1], recv_sems.at[kk - 1], kk))

        @pl.when((me >= first) & (me < first + n))
        def _():
            on_local(pltpu.make_async_copy(g_ref.at[me - first], recv_ref.at[me], local_sem))
            if on_arrival is not None:
                for kk in range(1, N_DEV):
                    on_arrival(_remote(g_ref.at[0], recv_ref.at[jnp.bitwise_xor(me, kk)], send_sems.at[kk - 1],
                                       recv_sems.at[kk - 1], kk))

    start = lambda: each(lambda cp: cp.start(), lambda cp: cp.start(), None)
    finish = lambda: each(lambda cp: cp.wait_send(), lambda cp: cp.wait(), lambda cp: cp.wait_recv())
    return start, finish


def _scatter_rows(g_ref, recv_ref, send_sems, recv_sems, local_sems):
    me = _me()
    rows = lambda m, dev: g_ref.at[m, pl.ds(dev * 128, 128), :]

    def copies():
        local = [pltpu.make_async_copy(rows(m, me), recv_ref.at[me, m], local_sems.at[m]) for m in range(N_ROW_GRADS)]
        sends, arrivals = [], []
        for m in range(N_ROW_GRADS):
            for kk in range(1, N_DEV):
                peer, sems = jnp.bitwise_xor(me, kk), (send_sems.at[7 * m + kk - 1], recv_sems.at[7 * m + kk - 1])
                sends.append(_remote(rows(m, peer), recv_ref.at[me, m], *sems, kk))
                arrivals.append(_remote(rows(m, me), recv_ref.at[peer, m], *sems, kk))
        return local, sends, arrivals

    def start():
        local, sends, _ = copies()
        for cp in local + sends:
            cp.start()

    def finish():
        local, sends, arrivals = copies()
        for cp in arrivals:
            cp.wait_recv()
        for cp in sends:
            cp.wait_send()
        for cp in local:
            cp.wait()

    return start, finish


LOW_OWNERS = 3


def _weight_grad_low(h, dp_a):
    n_rows = h.shape[0]
    tk = _row_tile(n_rows, 2080)
    n_k = n_rows // tk
    order = jnp.where(lax.axis_index("c") == 0, jnp.array([1, 0, 2], jnp.int32), jnp.array([0, 2, 1], jnp.int32))

    def body(order_ref, h_ref, y_ref, out_ref, acc, send_buf, got, send_sems, recv_sems):
        t, k = pl.program_id(0), pl.program_id(1)
        c = lax.axis_index("c")

        @pl.when(k == 0)
        def _():
            acc[...] = jnp.zeros_like(acc)

        acc[...] += _dot_tn(h_ref[...], y_ref[0])

        def to_sibling(slot):
            return _remote(send_buf.at[slot], got.at[slot], send_sems.at[slot], recv_sems.at[slot], 1)

        def send(slot):
            send_buf[slot] = acc[...].astype(send_buf.dtype)
            to_sibling(slot).start()

        def keep(slot):
            to_sibling(slot).wait_recv()
            out_ref[slot] = (acc[...] + got[slot].astype(F32)).astype(out_ref.dtype)

        done = k == n_k - 1
        for core, step, action, slot in ((0, 0, send, 0), (0, 1, keep, 0), (0, 2, keep, 1),
                                         (1, 0, send, 0), (1, 1, send, 1), (1, 2, keep, 0)):
            @pl.when(done & (c == core) & (t == step))
            def _(action=action, slot=slot):
                action(slot)

        @pl.when(done & (t == LOW_OWNERS - 1))
        def _():
            to_sibling(0).wait_send()

            @pl.when(c == 1)
            def _():
                to_sibling(1).wait_send()
                out_ref[1] = jnp.zeros(out_ref.shape[1:], out_ref.dtype)

    pair = (2, h.shape[1], dp_a.shape[2])
    return pl.pallas_call(
        body, name="weight_grad_in_low",
        out_shape=jax.ShapeDtypeStruct(pair, BF16),
        grid_spec=pltpu.PrefetchScalarGridSpec(
            num_scalar_prefetch=1, grid=(LOW_OWNERS, n_k),
            in_specs=[pl.BlockSpec((tk, h.shape[1]), lambda t, k, o: (k, 0)),
                      pl.BlockSpec((1, tk, dp_a.shape[2]), lambda t, k, o: (o[t], k, 0))],
            out_specs=pl.BlockSpec(pair, lambda t, k, o: (0, 0, 0)),
            scratch_shapes=[pltpu.VMEM(pair[1:], F32), pltpu.VMEM(pair, BF16), pltpu.VMEM(pair, BF16),
                            pltpu.SemaphoreType.DMA((2,)), pltpu.SemaphoreType.DMA((2,))]),
        compiler_params=_params(("arbitrary", "arbitrary")),
    )(order, h, dp_a)


def _exchange_small(misc):
    def body(m_ref, out_ref, send_sems, recv_sems, local_sem):
        start, finish = _scatter_slabs(m_ref, 0, out_ref, send_sems, recv_sems, local_sem)
        start()
        finish()

    return pl.pallas_call(
        body, name="exchange_small", out_shape=jax.ShapeDtypeStruct(misc.shape, misc.dtype),
        in_specs=[ANY], out_specs=ANY, scratch_shapes=list(SCATTER_SEMS),
    )(misc)


def _scatter_low(part_ref, recv_ref, send_sems, recv_sems, local_sem):
    x, y, c = lax.axis_index("x"), lax.axis_index("y"), lax.axis_index("c")
    chip = 2 * x + y
    routes = ((0, (0, 0, c), 0, None), (1, (0, 1, 0), 1, 0))

    def each(on_send, on_local, on_arrival):
        for slot, owner, owner_chip, core in routes:
            holds = (c == core) if core is not None else (c >= 0)
            rel = jnp.bitwise_xor(chip, owner_chip)

            @pl.when(holds & (rel != 0))
            def _(slot=slot, owner=owner, rel=rel):
                on_send(pltpu.make_async_remote_copy(
                    src_ref=part_ref.at[slot], dst_ref=recv_ref.at[chip], send_sem=send_sems.at[slot],
                    recv_sem=recv_sems.at[rel - 1], device_id=owner, device_id_type=MESH))

            @pl.when(holds & (rel == 0))
            def _(slot=slot, owner=owner, owner_chip=owner_chip):
                on_local(pltpu.make_async_copy(part_ref.at[slot], recv_ref.at[chip], local_sem))
                if on_arrival is not None:
                    for r in range(1, 4):
                        on_arrival(pltpu.make_async_remote_copy(
                            src_ref=part_ref.at[slot], dst_ref=recv_ref.at[r ^ owner_chip], send_sem=send_sems.at[slot],
                            recv_sem=recv_sems.at[r - 1], device_id=owner, device_id_type=MESH))

    start = lambda: each(lambda cp: cp.start(), lambda cp: cp.start(), None)
    finish = lambda: each(lambda cp: cp.wait_send(), lambda cp: cp.wait(), lambda cp: cp.wait_recv())
    return start, finish


def _adam_update(g, w, m, v):
    mn = ADAM_B1 * m + (1.0 - ADAM_B1) * g
    vn = ADAM_B2 * v + (1.0 - ADAM_B2) * (g * g)
    m_hat = mn / (1.0 - ADAM_B1 ** ADAM_STEP)
    v_hat = vn / (1.0 - ADAM_B2 ** ADAM_STEP)
    return -ADAM_LR * (m_hat / (jnp.sqrt(v_hat) + ADAM_EPS) + ADAM_WD * w), mn, vn


def _device_sum(parts):
    t = [p.astype(F32) for p in parts]
    return ((t[0] + t[1]) + (t[2] + t[3])) + ((t[4] + t[5]) + (t[6] + t[7]))


def _finish(a_hi, a_lo, n_lo, b3, misc, big, rows3, small):
    n_steps = 4
    tb, tr3 = 1024 // n_steps, 128 // n_steps

    def body(*refs):
        it = iter(refs)
        hi_ref, lo_ref, b_ref, m_ref = next(it), next(it), next(it), next(it)
        big_in = [next(it) for _ in range(3)]
        rows_in = [[next(it) for _ in range(3)] for _ in rows3]
        small_in = [[next(it) for _ in range(3)] for _ in small]
        big_out = [next(it) for _ in range(4)]
        rows_out = [[next(it) for _ in range(4)] for _ in rows3]
        small_out = [[next(it) for _ in range(4)] for _ in small]
        loss_ref = next(it)

        def apply(g, ins, outs):
            d, mn, vn = _adam_update(g, ins[0][...], ins[1][...], ins[2][...])
            for r, val in zip(outs, (g, d, mn, vn)):
                r[...] = val

        lo = [lo_ref[s].astype(F32) for s in range(4)]
        g_big = jnp.where(_me() < n_lo, (lo[0] + lo[1]) + (lo[2] + lo[3]), _device_sum([hi_ref[s] for s in range(N_DEV)]))
        apply(g_big[None], big_in, big_out)
        for k in range(len(rows3)):
            apply(_device_sum([b_ref[s, k] for s in range(N_DEV)])[None], rows_in[k], rows_out[k])

        @pl.when(pl.program_id(0) == 0)
        def _():
            loss_ref[...] = _device_sum([m_ref[s, MISC_LOSS:MISC_LOSS + 1, :] for s in range(N_DEV)])
            for (row0, lanes, ins), r_in, r_out in zip(small, small_in, small_out):
                n = ins[0].shape[0]
                apply(_device_sum([m_ref[s, row0:row0 + n, :lanes] for s in range(N_DEV)]), r_in, r_out)

    whole = lambda shape: pl.BlockSpec(shape, lambda i: (0,) * len(shape))
    big_blk = pl.BlockSpec((1, tb, 1024), lambda i: (0, i, 0))
    rows_blk = pl.BlockSpec((1, tr3, 1024), lambda i: (0, i, 0))
    in_specs = [pl.BlockSpec((N_DEV, tb, 1024), lambda i: (0, i, 0)), pl.BlockSpec((4, tb, 1024), lambda i: (0, i, 0)),
                pl.BlockSpec((N_DEV, 3, tr3, 1024), lambda i: (0, 0, i, 0)), whole(misc.shape)]
    in_specs += [big_blk] * 3 + [rows_blk] * (3 * len(rows3))
    out_specs = [big_blk] * 4 + [rows_blk] * (4 * len(rows3))
    out_shape = [jax.ShapeDtypeStruct(big[0].shape, F32)] * 4
    for w, _, _ in rows3:
        out_shape += [jax.ShapeDtypeStruct(w.shape, F32)] * 4
    args = [a_hi, a_lo, b3, misc, *big]
    for t in rows3:
        args += list(t)
    for _, _, t in small:
        in_specs += [whole(t[0].shape)] * 3
        out_specs += [whole(t[0].shape)] * 4
        out_shape += [jax.ShapeDtypeStruct(t[0].shape, F32)] * 4
        args += list(t)
    out_specs.append(whole((1, 1024)))
    out_shape.append(jax.ShapeDtypeStruct((1, 1024), F32))
    outs = pl.pallas_call(
        body, name="reduce_sum_adamw", out_shape=tuple(out_shape), grid=(n_steps,),
        in_specs=in_specs, out_specs=tuple(out_specs),
        compiler_params=_params(("arbitrary",)),
    )(*args)
    return [tuple(outs[4 * k:4 * k + 4]) for k in range(len(outs) // 4)], outs[-1][0, 0]


GATHER_UNITS = ((0, 0), (0, 1), (1, 0), (1, 1)) + tuple((place, 0) for place in range(2, 8)) + tuple(
    (place, 1) for place in range(2, 8))


def _gather_order():
    x, y, c = lax.axis_index("x"), lax.axis_index("y"), lax.axis_index("c")
    chips = [(1 - x, y), (x, 1 - y), (1 - x, 1 - y)]
    order = [_device_index(x, y, c), _device_index(x, y, 1 - c)]
    order += [_device_index(*q, c) for q in chips] + [_device_index(*q, 1 - c) for q in chips]
    return order


def _gather_units():
    order = _gather_order()
    blocks = jnp.stack([order[place] for place, _ in GATHER_UNITS]).astype(jnp.int32)
    return blocks, jnp.array([half for _, half in GATHER_UNITS], jnp.int32)


def _in_projection(zp, norm_w, w_shard, b_blocks, units):
    n_rows = zp.shape[0]
    tr = _row_tile(n_rows, 832)
    nt = n_rows // tr

    half_cols = 1024 // 2
    n_units = len(GATHER_UNITS)

    def body(blocks_ref, halves_ref, z_ref, nw_ref, w_hbm, b_ref, p_ref, h_ref, w_out, w_vmem, h_all,
             send_sems, recv_sems, local_sem, out_sems):
        s, i = pl.program_id(0), pl.program_id(1)
        x, y, c = lax.axis_index("x"), lax.axis_index("y"), lax.axis_index("c")
        me, sibling = (x, y, c), (x, y, 1 - c)
        chips = [(1 - x, y), (x, 1 - y), (1 - x, 1 - y)]

        def slot(blk, half):
            return w_vmem.at[_device_index(*blk), half]

        def mine(half):
            return w_hbm.at[:, pl.ds(half * half_cols, half_cols)]

        def copy(k, half, blk, to, own_block=False):
            return pltpu.make_async_remote_copy(
                src_ref=mine(half) if own_block else slot(blk, half), dst_ref=slot(blk, half),
                send_sem=send_sems.at[2 * k + half], recv_sem=recv_sems.at[2 * k + half], device_id=to, device_id_type=MESH)

        own = [pltpu.make_async_copy(mine(half), slot(me, half), local_sem.at[half]) for half in range(2)]
        first = [copy(k, half, me, to, own_block=True) for half in range(2)
                 for k, to in enumerate([sibling] + [(*q, c) for q in chips])]
        passed = [[copy(4 + j, half, (*q, c), sibling) for j, q in enumerate(chips)] for half in range(2)]
        sources = [None, sibling] + [(*q, c) for q in chips] + [(*q, 1 - c) for q in chips]
        arrival = lambda place, half: copy(place - 1, half, sources[place], me)

        def keep(unit):
            blk, half = blocks_ref[unit], halves_ref[unit]
            return pltpu.make_async_copy(w_vmem.at[blk, half], w_out.at[blk, half], out_sems.at[unit])

        for unit, (place, half) in enumerate(GATHER_UNITS):
            @pl.when((i == 0) & (s == unit))
            def _(unit=unit, place=place, half=half):
                if unit == 0:
                    for cp in own + first:
                        cp.start()
                if place == 0:
                    own[half].wait()
                else:
                    arrival(place, half).wait_recv()
                    if 2 <= place <= 4:
                        passed[half][place - 2].start()
                keep(unit).start()

        @pl.when(s == 0)
        def _():
            z = z_ref[...]
            r = lax.rsqrt(jnp.mean(z * z, axis=-1, keepdims=True) + EPS)
            h = (z * r * nw_ref[...]).astype(BF16)
            h_all[i] = h
            h_ref[...] = h

        p_ref[0] = jnp.dot(h_all[i], w_vmem[blocks_ref[s], halves_ref[s]], preferred_element_type=F32) + b_ref[0]

        @pl.when((s == n_units - 1) & (i == nt - 1))
        def _():
            for cp in first + passed[0] + passed[1]:
                cp.wait_send()
            for unit in range(n_units):
                keep(unit).wait()

    first_pass = lambda s, i, rest: jnp.where(s == 0, i, rest)
    return pl.pallas_call(
        body, name="in_projection_gather",
        out_shape=(jax.ShapeDtypeStruct((N_COLBLK, n_rows, 1024), F32),
                   jax.ShapeDtypeStruct((n_rows, D_MODEL), BF16),
                   jax.ShapeDtypeStruct((N_DEV, 2, D_MODEL, half_cols), BF16)),
        grid_spec=pltpu.PrefetchScalarGridSpec(
            num_scalar_prefetch=2, grid=(n_units, nt),
            in_specs=[pl.BlockSpec((tr, D_MODEL), lambda s, i, blk, hf: (first_pass(s, i, 0), 0)),
                      pl.BlockSpec((1, D_MODEL), lambda s, i, blk, hf: (0, 0)), ANY,
                      pl.BlockSpec((1, 1, half_cols), lambda s, i, blk, hf: (blk[s], 0, hf[s]))],
            out_specs=(pl.BlockSpec((1, tr, half_cols), lambda s, i, blk, hf: (blk[s], i, hf[s])),
                       pl.BlockSpec((tr, D_MODEL), lambda s, i, blk, hf: (first_pass(s, i, nt - 1), 0)), ANY),
            scratch_shapes=[pltpu.VMEM((N_DEV, 2, D_MODEL, half_cols), BF16), pltpu.VMEM((nt, tr, D_MODEL), BF16),
                            pltpu.SemaphoreType.DMA((14,)), pltpu.SemaphoreType.DMA((14,)), pltpu.SemaphoreType.DMA((2,)),
                            pltpu.SemaphoreType.DMA((n_units,))]),
        compiler_params=_params(("arbitrary", "arbitrary")),
    )(*units, zp, norm_w, w_shard, b_blocks)


def _lower_bound(lb_ref):
    l0, l1 = lb_ref[0:1, :], lb_ref[1:2, :]
    _, lb = _sigmoid_pair(l1 - l0)
    return lb


def _chunk_gates(fz, lb, valid):
    sig, nsig = _sigmoid_pair(fz)
    f = lb + (1.0 - lb) * sig
    g2 = jnp.where(valid, jnp.log2(f), 0.0)
    k = jnp.where(valid, (1.0 - lb) * nsig, 0.0)
    return sig, nsig, f, g2, k


def _tri(n, upper=False):
    r = lax.broadcasted_iota(jnp.int32, (n, n), 0)
    c = lax.broadcasted_iota(jnp.int32, (n, n), 1)
    return jnp.where((r <= c) if upper else (r >= c), 1.0, 0.0).astype(BF16)


def _tri_dot(tri, x):
    hi = x.astype(BF16)
    rest = x - hi.astype(F32)
    mid = rest.astype(BF16)
    low = (rest - mid.astype(F32)).astype(BF16)
    return (jnp.dot(tri, hi, preferred_element_type=F32) + jnp.dot(tri, mid, preferred_element_type=F32)
            + jnp.dot(tri, low, preferred_element_type=F32))


def _intra_scores(q_ref, k_ref, b2_ref, col0):
    cols = pl.ds(col0, HEAD_DIM)
    rows_s = lax.broadcasted_iota(jnp.int32, (SUB, 1), 0)
    lanes_c = lax.broadcasted_iota(jnp.int32, (1, CHUNK), 1)
    blocks = []
    for i in range(N_SUB):
        lo = i * SUB
        qi = q_ref[lo:lo + SUB, cols]
        bi = b2_ref[lo:lo + SUB, cols]
        if i == 0:
            acc = jnp.zeros((SUB, CHUNK), F32)
        else:
            ref_i = b2_ref[lo:lo + 1, cols]
            qt = qi * jnp.exp2(bi - ref_i)
            kt = jnp.concatenate([k_ref[0:lo, cols] * jnp.exp2(ref_i - b2_ref[0:lo, cols]),
                                  jnp.zeros((CHUNK - lo, HEAD_DIM), F32)], axis=0)
            acc = _dot_nt(qt, kt)
        for s in range(SUB):
            b_s = b2_ref[lo + s:lo + s + 1, cols]
            k_s = k_ref[lo + s:lo + s + 1, cols]
            w = jnp.exp2(jnp.minimum(bi - b_s, 0.0))
            col = jnp.sum((qi * w) * k_s, axis=-1, keepdims=True)
            acc = jnp.where(lanes_c == lo + s, col, acc)
        blocks.append(jnp.where(lanes_c <= lo + rows_s, acc, 0.0))
    return jnp.concatenate(blocks, axis=0)


def _hgrn_forward(p, lb_logits, w_rest):
    n_rows = p.shape[1]
    n_chunks = n_rows // CHUNK
    width = HEADS * HEAD_DIM

    def body(q_ref, fz_ref, v_ref, lb_ref, rest_ref, o_ref, st_out_ref, a_out_ref, rest_out,
             state, k_vmem, b2_vmem, send_sems, recv_sems, local_sem):
        n = pl.program_id(0)
        own, sends, arrivals = _direct_gather(rest_ref, rest_out, send_sems, recv_sems, local_sem)

        @pl.when(n == 0)
        def _():
            state[...] = jnp.zeros_like(state)
            own.start()
            for cp in sends:
                cp.start()

        rows = n * CHUNK + lax.broadcasted_iota(jnp.int32, (CHUNK, 1), 0)
        valid = rows >= PAD_ROWS
        lb = _lower_bound(lb_ref)
        _, _, _, g2, k = _chunk_gates(fz_ref[0], lb, valid)
        k_vmem[...] = k
        b2_vmem[...] = _tri_dot(_tri(CHUNK), g2)
        q_view = q_ref.at[0]
        for h in range(HEADS):
            cols = pl.ds(h * HEAD_DIM, HEAD_DIM)
            st = state[h]
            st_out_ref[0, h] = st
            bh = b2_vmem[:, cols]
            kh = k_vmem[:, cols]
            vh = jnp.where(valid, v_ref[0, :, cols], 0.0)
            qe = q_ref[0, :, cols] * jnp.exp2(bh)
            a = _intra_scores(q_view, k_vmem, b2_vmem, h * HEAD_DIM).astype(BF16)
            a_out_ref[0, h] = a
            o_ref[:, cols] = _dot_nt(qe, st) + _dot(a, vh)
            b_last = b2_vmem[CHUNK - 1:CHUNK, cols]
            kd = kh * jnp.exp2(b_last - bh)
            state[h] = st * jnp.exp2(b_last) + _dot_tn(vh, kd)

        @pl.when(n == n_chunks - 1)
        def _():
            for cp in arrivals:
                cp.wait_recv()
            for cp in sends:
                cp.wait_send()
            own.wait()

    blk = lambda c: pl.BlockSpec((1, CHUNK, width), lambda n, c=c: (c, n, 0))
    return pl.pallas_call(
        body, name="hgrn_forward",
        out_shape=(jax.ShapeDtypeStruct((n_rows, width), F32),
                   jax.ShapeDtypeStruct((n_chunks, HEADS, HEAD_DIM, HEAD_DIM), F32),
                   jax.ShapeDtypeStruct((n_chunks, HEADS, CHUNK, CHUNK), BF16),
                   jax.ShapeDtypeStruct((N_DEV,) + w_rest.shape, w_rest.dtype)),
        grid=(n_chunks,),
        in_specs=[blk(0), blk(1), blk(2), pl.BlockSpec((2, width), lambda n: (0, 0)), ANY],
        out_specs=(pl.BlockSpec((CHUNK, width), lambda n: (n, 0)),
                   pl.BlockSpec((1, HEADS, HEAD_DIM, HEAD_DIM), lambda n: (n, 0, 0, 0)),
                   pl.BlockSpec((1, HEADS, CHUNK, CHUNK), lambda n: (n, 0, 0, 0)), ANY),
        scratch_shapes=[pltpu.VMEM((HEADS, HEAD_DIM, HEAD_DIM), F32), pltpu.VMEM((CHUNK, width), F32),
                        pltpu.VMEM((CHUNK, width), F32)] + list(GATHER_SEMS),
        compiler_params=_params(("arbitrary",)),
    )(p, p, p, lb_logits, w_rest)


def _hgrn_backward(p, lb_logits, states, scores, d_o, g_slabs, first_owner, g_rows):
    n_rows = p.shape[1]
    n_chunks = n_rows // CHUNK
    width = HEADS * HEAD_DIM

    def body(q_ref, fz_ref, v_ref, lb_ref, st_ref, a_ref, do_ref, gs_hbm, gr_hbm, dp_ref, dbias_ref, dlb_ref, rs_hbm, rr_hbm,
             dstate, k_vmem, b2_vmem, *sems):
        step = pl.program_id(0)
        n = n_chunks - 1 - step
        start_slabs, finish_slabs = _scatter_slabs(gs_hbm, first_owner, rs_hbm, *sems[:3])
        start_rows, finish_rows = _scatter_rows(gr_hbm, rr_hbm, *sems[3:])

        @pl.when(step == 0)
        def _():
            dstate[...] = jnp.zeros_like(dstate)
            dbias_ref[...] = jnp.zeros_like(dbias_ref)
            dlb_ref[...] = jnp.zeros_like(dlb_ref)
            start_slabs()
            start_rows()

        rows = n * CHUNK + lax.broadcasted_iota(jnp.int32, (CHUNK, 1), 0)
        valid = rows >= PAD_ROWS
        lb = _lower_bound(lb_ref)
        sig, nsig, f, g2, k = _chunk_gates(fz_ref[0], lb, valid)
        k_vmem[...] = k
        b2_vmem[...] = _tri_dot(_tri(CHUNK), g2)
        rows_c = lax.broadcasted_iota(jnp.int32, (CHUNK, 1), 0)
        rows_s = lax.broadcasted_iota(jnp.int32, (SUB, 1), 0)
        lanes_c = lax.broadcasted_iota(jnp.int32, (1, CHUNK), 1)
        causal = lax.broadcasted_iota(jnp.int32, (CHUNK, CHUNK), 0) >= lax.broadcasted_iota(jnp.int32, (CHUNK, CHUNK), 1)
        tri_up = _tri(CHUNK, upper=True)
        for h in range(HEADS):
            cols = pl.ds(h * HEAD_DIM, HEAD_DIM)
            st = st_ref[0, h]
            dst = dstate[h]
            qh = q_ref[0, :, cols]
            bh = b2_vmem[:, cols]
            kh = k_vmem[:, cols]
            vh = jnp.where(valid, v_ref[0, :, cols], 0.0)
            doh = do_ref[:, cols]
            eb = jnp.exp2(bh)
            qe = qh * eb
            b_last = b2_vmem[CHUNK - 1:CHUNK, cols]
            e_last = jnp.exp2(b_last)
            decay_k = jnp.exp2(b_last - bh)
            kd = kh * decay_k
            dqe = _dot(doh, st)
            da = jnp.where(causal, _dot_nt(doh, vh), 0.0)
            dv = _dot_tn(a_ref[0, h], doh) + _dot_nt(kd, dst)
            dkd = _dot(vh, dst)
            dstate[h] = dst * e_last + _dot_tn(doh, qe)
            db_last = (jnp.sum(dst * st, axis=0, keepdims=True) * e_last
                       + jnp.sum(dkd * kd, axis=0, keepdims=True))
            dq_blocks, dk_blocks = [], []
            dk_earlier = jnp.zeros((CHUNK, HEAD_DIM), F32)
            for i in range(N_SUB):
                lo = i * SUB
                qi = q_ref[0, lo:lo + SUB, cols]
                bi = b2_vmem[lo:lo + SUB, cols]
                da_i = da[lo:lo + SUB, :]
                if i == 0:
                    dq_i = jnp.zeros((SUB, HEAD_DIM), F32)
                else:
                    ref_i = b2_vmem[lo:lo + 1, cols]
                    eq = jnp.exp2(bi - ref_i)
                    ek = jnp.exp2(ref_i - b2_vmem[0:lo, cols])
                    later = jnp.zeros((CHUNK - lo, HEAD_DIM), F32)
                    kt = jnp.concatenate([k_vmem[0:lo, cols] * ek, later], axis=0)
                    dq_i = _dot(da_i, kt) * eq
                    dk_earlier = dk_earlier + jnp.concatenate([_dot_tn(da_i, qi * eq)[0:lo] * ek, later], axis=0)
                dk_i = jnp.zeros((SUB, HEAD_DIM), F32)
                for s in range(SUB):
                    b_s = b2_vmem[lo + s:lo + s + 1, cols]
                    k_s = k_vmem[lo + s:lo + s + 1, cols]
                    w = jnp.exp2(jnp.minimum(bi - b_s, 0.0))
                    da_col = jnp.sum(jnp.where(lanes_c == lo + s, da_i, 0.0), axis=-1, keepdims=True)
                    gw = da_col * w
                    dq_i = dq_i + gw * k_s
                    dk_i = jnp.where(rows_s == s, jnp.sum(gw * qi, axis=0, keepdims=True), dk_i)
                dq_blocks.append(dq_i)
                dk_blocks.append(dk_i)
            dq_intra = jnp.concatenate(dq_blocks, axis=0)
            dk_intra = jnp.concatenate(dk_blocks, axis=0) + dk_earlier
            dq = dqe * eb + dq_intra
            dk = dkd * decay_k + dk_intra
            db = dqe * qe - dkd * kd + qh * dq_intra - kh * dk_intra
            db = db + jnp.where(rows_c == CHUNK - 1, db_last, 0.0)
            dg = _tri_dot(tri_up, db)
            fh = f[:, h * HEAD_DIM:(h + 1) * HEAD_DIM]
            sh = sig[:, h * HEAD_DIM:(h + 1) * HEAD_DIM]
            nh = nsig[:, h * HEAD_DIM:(h + 1) * HEAD_DIM]
            lbh = lb[:, h * HEAD_DIM:(h + 1) * HEAD_DIM]
            df = jnp.where(valid, dg / fh - dk, 0.0)
            dfz = df * (1.0 - lbh) * sh * nh
            dq = jnp.where(valid, dq, 0.0)
            dv = jnp.where(valid, dv, 0.0)
            dlb_ref[:, cols] += jnp.sum(df * nh, axis=0, keepdims=True)
            dp_ref[0, :, cols] = dq.astype(BF16)
            dp_ref[1, :, cols] = dfz.astype(BF16)
            dp_ref[2, :, cols] = dv.astype(BF16)
            dbias_ref[0, :, cols] += jnp.sum(dq, axis=0, keepdims=True)
            dbias_ref[1, :, cols] += jnp.sum(dfz, axis=0, keepdims=True)
            dbias_ref[2, :, cols] += jnp.sum(dv, axis=0, keepdims=True)

        @pl.when(step == n_chunks - 1)
        def _():
            finish_slabs()
            finish_rows()

    rev = lambda s: n_chunks - 1 - s
    blk = lambda c: pl.BlockSpec((1, CHUNK, width), lambda s, c=c: (c, rev(s), 0))
    return pl.pallas_call(
        body, name="hgrn_backward",
        out_shape=(jax.ShapeDtypeStruct((3, n_rows, width), BF16),
                   jax.ShapeDtypeStruct((3, 1, width), F32),
                   jax.ShapeDtypeStruct((1, width), F32),
                   jax.ShapeDtypeStruct((N_DEV,) + g_slabs.shape[1:], g_slabs.dtype),
                   jax.ShapeDtypeStruct((N_DEV, N_ROW_GRADS, 128, g_rows.shape[2]), g_rows.dtype)),
        grid=(n_chunks,),
        in_specs=[blk(0), blk(1), blk(2), pl.BlockSpec((2, width), lambda s: (0, 0)),
                  pl.BlockSpec((1, HEADS, HEAD_DIM, HEAD_DIM), lambda s: (rev(s), 0, 0, 0)),
                  pl.BlockSpec((1, HEADS, CHUNK, CHUNK), lambda s: (rev(s), 0, 0, 0)),
                  pl.BlockSpec((CHUNK, width), lambda s: (rev(s), 0)), ANY, ANY],
        out_specs=(pl.BlockSpec((3, CHUNK, width), lambda s: (0, rev(s), 0)),
                   pl.BlockSpec((3, 1, width), lambda s: (0, 0, 0)),
                   pl.BlockSpec((1, width), lambda s: (0, 0)), ANY, ANY),
        scratch_shapes=[pltpu.VMEM((HEADS, HEAD_DIM, HEAD_DIM), F32), pltpu.VMEM((CHUNK, width), F32),
                        pltpu.VMEM((CHUNK, width), F32)] + list(SCATTER_SEMS) + list(SCATTER_ROWS_SEMS),
        compiler_params=_params(("arbitrary",)),
    )(p, p, p, lb_logits, states, scores, d_o, g_slabs, g_rows)


def _sigmoid_and_complement(x):
    s = 0.5 * jnp.tanh(0.5 * x) + 0.5
    return s, 1.0 - s


def _silu_and_grad(x):
    s, ns = _sigmoid_and_complement(x)
    return x * s, s * (1.0 + x * ns)


def _tail(p, o, zp, tgt, hg_norm_w, pool_w, pool_scale, w_down_hg, w_down_pool, w_out, final_norm_w):
    n_rows = zp.shape[0]
    tr = _row_tile(n_rows, 208)
    nt = n_rows // tr
    ext = tr + HALO
    n_groups = len(POOL_WINDOWS)

    def body(o_ref, ghg_ref, u_ref, gpool_ref, mhg_ref, mpool_ref, uhalo_ref, z_ref, tgt_hbm,
             hgw_ref, pw_ref, ps_ref, wdh_ref, wdp_ref, wout_ref, fnw_ref,
             do_ref, dp_ref, dz2_ref, lhs_ref, rhs_ref,
             dbias_ref, dhgw_ref, dpw_ref, dps_ref, dfnw_ref, loss_ref, halo_vmem, tgt_buf, tgt_sems):
        step = pl.program_id(0)
        ti = nt - 1 - step

        def target_rows(tile, slot, act):
            @pl.when(tile == 0)
            def _():
                cp = pltpu.make_async_copy(tgt_hbm.at[pl.ds(0, tr - CHUNK), :], tgt_buf.at[slot, pl.ds(CHUNK, tr - CHUNK), :],
                                           tgt_sems.at[slot])
                getattr(cp, act)()

            @pl.when(tile > 0)
            def _():
                cp = pltpu.make_async_copy(tgt_hbm.at[pl.ds(tile * tr - CHUNK, tr), :], tgt_buf.at[slot], tgt_sems.at[slot])
                getattr(cp, act)()

        @pl.when(step == 0)
        def _():
            halo_vmem[...] = jnp.zeros_like(halo_vmem)
            for r in (dbias_ref, dhgw_ref, dpw_ref, dps_ref, dfnw_ref, loss_ref):
                r[...] = jnp.zeros_like(r)
            if nt <= 2:
                tgt_buf[(nt - 1) % 2, 0:CHUNK, :] = jnp.zeros((CHUNK, D_MODEL), F32)
            target_rows(ti, 0, "start")

        @pl.when(ti > 0)
        def _():
            target_rows(ti - 1, (step + 1) % 2, "start")

        rows = ti * tr + lax.broadcasted_iota(jnp.int32, (tr, 1), 0)
        valid = rows >= PAD_ROWS
        in_loss = rows >= CHUNK
        count_pos = jnp.maximum(rows - PAD_ROWS + 1, 1).astype(F32)

        o = o_ref[...]
        hgw = hgw_ref[...]
        inv_o, on_parts = [], []
        for h in range(HEADS):
            oh = o[:, h * HEAD_DIM:(h + 1) * HEAD_DIM]
            r = lax.rsqrt(jnp.mean(oh * oh, axis=-1, keepdims=True) + EPS)
            inv_o.append(r)
            on_parts.append(oh * r)
        o_hat = jnp.concatenate(on_parts, axis=1)
        o_n = o_hat * hgw
        g_hg = ghg_ref[0]
        silu_hg, dsilu_hg = _silu_and_grad(g_hg)
        a_hg = o_n * silu_hg
        y_hg = _dot(a_hg, wdh_ref[...])

        u = jnp.where(valid, u_ref[0], 0.0)
        u_prev = jnp.where(ti > 0, uhalo_ref[0], 0.0)
        u_ext = jnp.concatenate([u_prev, u], axis=0)
        pooled_parts, mixed_parts, inv_cnt = [], [], []
        for gi, win in enumerate(POOL_WINDOWS):
            lanes = slice(gi * POOL_GDIM, (gi + 1) * POOL_GDIM)
            s = u_ext[:, lanes]
            shift = 1
            while shift < win:
                s = s + pltpu.roll(s, shift, 0)
                shift *= 2
            ic = 1.0 / jnp.minimum(count_pos, float(win))
            inv_cnt.append(ic)
            pooled = s[HALO:] * ic - u[:, lanes]
            pooled_parts.append(pooled)
            mixed_parts.append(_dot(pooled, pw_ref[gi]))
        mixed = jnp.concatenate(mixed_parts, axis=1)
        ps = ps_ref[...]
        g_pool = gpool_ref[0]
        silu_pool, dsilu_pool = _silu_and_grad(g_pool)
        a_pool = mixed * ps * silu_pool
        y_pool = _dot(a_pool, wdp_ref[...])

        m_hg, m_pool = mhg_ref[0], mpool_ref[0]
        s_hg, ns_hg = _sigmoid_and_complement(m_hg)
        s_pool, ns_pool = _sigmoid_and_complement(m_pool)
        merged = s_hg * y_hg + s_pool * y_pool
        z2 = z_ref[...] + _dot(merged, wout_ref[...])
        r2 = lax.rsqrt(jnp.mean(z2 * z2, axis=-1, keepdims=True) + EPS)
        n2 = z2 * r2
        fnw = fnw_ref[...]
        target_rows(ti, step % 2, "wait")
        err = jnp.where(in_loss, n2 * fnw - tgt_buf[step % 2], 0.0)
        loss_ref[...] += jnp.sum(jnp.sum(err * err, axis=0, keepdims=True), axis=1, keepdims=True) * (0.5 / D_MODEL)
        dy = err * (1.0 / D_MODEL)

        dfnw_ref[...] += jnp.sum(dy * n2, axis=0, keepdims=True)
        gy = dy * fnw
        dz2 = r2 * (gy - n2 * jnp.mean(gy * n2, axis=-1, keepdims=True))
        dmerged = _dot_nt(dz2, wout_ref[...])
        dy_hg = s_hg * dmerged
        dy_pool = s_pool * dmerged
        dm_hg = dmerged * y_hg * s_hg * ns_hg
        dm_pool = dmerged * y_pool * s_pool * ns_pool
        da_hg = _dot_nt(dy_hg, wdh_ref[...])
        da_pool = _dot_nt(dy_pool, wdp_ref[...])

        d_on = da_hg * silu_hg
        dg_hg = da_hg * o_n * dsilu_hg
        dhgw_ref[...] += jnp.sum(d_on * o_hat, axis=0, keepdims=True)
        gyo = d_on * hgw
        do_parts = []
        for h in range(HEADS):
            lanes = slice(h * HEAD_DIM, (h + 1) * HEAD_DIM)
            gh, nh = gyo[:, lanes], o_hat[:, lanes]
            do_parts.append(inv_o[h] * (gh - nh * jnp.mean(gh * nh, axis=-1, keepdims=True)))
        do_ref[...] = jnp.concatenate(do_parts, axis=1)

        dmixed = da_pool * ps * silu_pool
        dps_ref[...] += jnp.sum(da_pool * mixed * silu_pool, axis=0, keepdims=True)
        dg_pool = da_pool * mixed * ps * dsilu_pool
        du_parts = []
        for gi, win in enumerate(POOL_WINDOWS):
            lanes = slice(gi * POOL_GDIM, (gi + 1) * POOL_GDIM)
            dmx = dmixed[:, lanes]
            dpooled = _dot_nt(dmx, pw_ref[gi])
            dpw_ref[gi] += _dot_tn(pooled_parts[gi], dmx)
            dpt = dpooled * inv_cnt[gi]
            s = jnp.concatenate([dpt, halo_vmem[:, lanes]], axis=0)
            shift = 1
            while shift < win:
                s = s + pltpu.roll(s, ext - shift, 0)
                shift *= 2
            du_parts.append(s[:tr] - dpooled)
            halo_vmem[:, lanes] = dpt[:HALO]
        du = jnp.where(valid, jnp.concatenate(du_parts, axis=1), 0.0)

        for c, val in enumerate((dg_hg, du, dg_pool, dm_hg, dm_pool)):
            dp_ref[c] = val.astype(BF16)
            dbias_ref[c] += jnp.sum(val, axis=0, keepdims=True)
        dz2_ref[...] = dz2
        for c, (lhs, rhs) in enumerate(((merged, dz2), (a_hg, dy_hg), (a_pool, dy_pool))):
            lhs_ref[c] = lhs.astype(BF16)
            rhs_ref[c] = rhs.astype(BF16)

    rev = lambda s: nt - 1 - s
    rowblk = pl.BlockSpec((tr, D_MODEL), lambda s: (rev(s), 0))
    pblk = lambda c: pl.BlockSpec((1, tr, 1024), lambda s, c=c: (c, rev(s), 0))
    halo_blk = pl.BlockSpec((1, HALO, 1024), lambda s: (4, jnp.maximum(rev(s) * (tr // HALO) - 1, 0), 0))
    full = lambda shape: pl.BlockSpec(shape, lambda s: (0,) * len(shape))
    vec = full((1, D_MODEL))
    mat = full((D_MODEL, D_MODEL))
    act3 = jax.ShapeDtypeStruct((3, n_rows, D_MODEL), BF16)
    act3_blk = pl.BlockSpec((3, tr, D_MODEL), lambda s: (0, rev(s), 0))
    return pl.pallas_call(
        body, name="tail_forward_backward",
        out_shape=(jax.ShapeDtypeStruct((n_rows, D_MODEL), F32),
                   jax.ShapeDtypeStruct((5, n_rows, 1024), BF16),
                   jax.ShapeDtypeStruct((n_rows, D_MODEL), F32),
                   act3, act3,
                   jax.ShapeDtypeStruct((5, 1, 1024), F32),
                   jax.ShapeDtypeStruct((1, D_MODEL), F32),
                   jax.ShapeDtypeStruct((n_groups, POOL_GDIM, POOL_GDIM), F32),
                   jax.ShapeDtypeStruct((1, D_MODEL), F32),
                   jax.ShapeDtypeStruct((1, D_MODEL), F32),
                   jax.ShapeDtypeStruct((1, 1), F32)),
        grid=(nt,),
        in_specs=[rowblk, pblk(3), pblk(4), pblk(5), pblk(6), pblk(7), halo_blk, rowblk, ANY,
                  vec, full((n_groups, POOL_GDIM, POOL_GDIM)), vec, mat, mat, mat, vec],
        out_specs=(rowblk, pl.BlockSpec((5, tr, 1024), lambda s: (0, rev(s), 0)), rowblk,
                   act3_blk, act3_blk,
                   full((5, 1, 1024)), vec, full((n_groups, POOL_GDIM, POOL_GDIM)), vec, vec, full((1, 1))),
        scratch_shapes=[pltpu.VMEM((HALO, D_MODEL), F32), pltpu.VMEM((2, tr, D_MODEL), F32), pltpu.SemaphoreType.DMA((2,))],
        compiler_params=_params(("arbitrary",)),
    )(o, p, p, p, p, p, p, zp, tgt, hg_norm_w, pool_w, pool_scale, w_down_hg, w_down_pool, w_out, final_norm_w)


def _in_projection_backward(dp_a, dp_b, w_blocks, zp, dz2, norm_w, chip_sums):
    n_rows = zp.shape[0]
    tr = _row_tile(n_rows, 416)
    nt = n_rows // tr
    na, nb = dp_a.shape[0], dp_b.shape[0]

    def body(dpa_ref, dpb_ref, w_hbm, z_ref, dz2_ref, nw_ref, gs_hbm, gx_hbm, head_ref, dnw_ref, rs_hbm,
             w_vmem, sem, dz_buf, gx_sems, *sems):
        i = pl.program_id(0)
        start_slabs, finish_slabs = _scatter_low(gs_hbm, rs_hbm, *sems)

        def wait_rows_out(tile):
            @pl.when(tile == 0)
            def _():
                pltpu.make_async_copy(dz_buf.at[0, pl.ds(CHUNK, tr - CHUNK), :], gx_hbm.at[pl.ds(0, tr - CHUNK), :],
                                      gx_sems.at[0]).wait()

            @pl.when(tile > 0)
            def _():
                pltpu.make_async_copy(dz_buf.at[tile % 2], gx_hbm.at[pl.ds(tile * tr - CHUNK, tr), :],
                                      gx_sems.at[tile % 2]).wait()

        @pl.when(i == 0)
        def _():
            start_slabs()
            cp = pltpu.make_async_copy(w_hbm, w_vmem, sem)
            cp.start()
            cp.wait()
            dnw_ref[...] = jnp.zeros_like(dnw_ref)

        dh = jnp.zeros((tr, D_MODEL), F32)
        half_cols = w_vmem.shape[-1]
        for j in range(na + nb):
            dp_ref, jj = (dpa_ref, j) if j < na else (dpb_ref, j - na)
            for half in range(2):
                dh = dh + _dot_nt(dp_ref[jj, :, half * half_cols:(half + 1) * half_cols], w_vmem[j, half])
        z = z_ref[...]
        r = lax.rsqrt(jnp.mean(z * z, axis=-1, keepdims=True) + EPS)
        n1 = z * r
        dnw_ref[...] += jnp.sum(dh * n1, axis=0, keepdims=True)
        gh = dh * nw_ref[...]
        dz = dz2_ref[...] + r * (gh - n1 * jnp.mean(gh * n1, axis=-1, keepdims=True))

        @pl.when(i >= 2)
        def _():
            wait_rows_out(i - 2)

        dz_buf[i % 2] = dz

        @pl.when(i == 0)
        def _():
            head_ref[...] = dz[0:CHUNK]
            pltpu.make_async_copy(dz_buf.at[0, pl.ds(CHUNK, tr - CHUNK), :], gx_hbm.at[pl.ds(0, tr - CHUNK), :],
                                  gx_sems.at[0]).start()

        @pl.when(i > 0)
        def _():
            pltpu.make_async_copy(dz_buf.at[i % 2], gx_hbm.at[pl.ds(i * tr - CHUNK, tr), :], gx_sems.at[i % 2]).start()

        @pl.when(i == nt - 1)
        def _():
            if nt >= 2:
                wait_rows_out(i - 1)
            wait_rows_out(i)
            finish_slabs()

    rowblk = pl.BlockSpec((tr, D_MODEL), lambda i: (i, 0))
    vec = pl.BlockSpec((1, D_MODEL), lambda i: (0, 0))
    return pl.pallas_call(
        body, name="in_projection_backward",
        out_shape=(jax.ShapeDtypeStruct((n_rows - CHUNK, D_MODEL), F32), jax.ShapeDtypeStruct((CHUNK, D_MODEL), F32),
                   jax.ShapeDtypeStruct((1, D_MODEL), F32),
                   jax.ShapeDtypeStruct((4,) + chip_sums.shape[1:], chip_sums.dtype)),
        grid=(nt,),
        in_specs=[pl.BlockSpec((na, tr, 1024), lambda i: (0, i, 0)), pl.BlockSpec((nb, tr, 1024), lambda i: (0, i, 0)),
                  ANY, rowblk, rowblk, vec, ANY],
        out_specs=(ANY, pl.BlockSpec((CHUNK, D_MODEL), lambda i: (0, 0)), vec, ANY),
        scratch_shapes=[pltpu.VMEM(w_blocks.shape, w_blocks.dtype), pltpu.SemaphoreType.DMA(()),
                        pltpu.VMEM((2, tr, D_MODEL), F32), pltpu.SemaphoreType.DMA((2,)),
                        pltpu.SemaphoreType.DMA((2,)), pltpu.SemaphoreType.DMA((3,)), pltpu.SemaphoreType.DMA(())],
        compiler_params=_params(("arbitrary",)),
    )(dp_a, dp_b, w_blocks, zp, dz2, norm_w, chip_sums)


def _weight_grad(xs, ys, name):
    shared = xs.ndim == 2
    n_rows, m = xs.shape[-2:]
    nb, _, n = ys.shape
    tk = _row_tile(n_rows, 4160)
    n_k = n_rows // tk

    def body(x_ref, y_ref, o_ref, acc):
        k = pl.program_id(1)

        @pl.when(k == 0)
        def _():
            acc[...] = jnp.zeros_like(acc)

        acc[...] += _dot_tn(x_ref[...] if shared else x_ref[0], y_ref[0])

        @pl.when(k == n_k - 1)
        def _():
            o_ref[0] = acc[...].astype(o_ref.dtype)

    x_spec = pl.BlockSpec((tk, m), lambda j, k: (k, 0)) if shared else pl.BlockSpec((1, tk, m), lambda j, k: (j, k, 0))
    return pl.pallas_call(
        body, name=name,
        out_shape=jax.ShapeDtypeStruct((nb, m, n), BF16),
        grid=(nb, n_k),
        in_specs=[x_spec, pl.BlockSpec((1, tk, n), lambda j, k: (j, k, 0))],
        out_specs=pl.BlockSpec((1, m, n), lambda j, k: (j, 0, 0)),
        scratch_shapes=[pltpu.VMEM((m, n), F32)],
        compiler_params=_params(("arbitrary", "arbitrary")),
    )(xs, ys)


def kernel(x, meta_tokens, norm_w, w_in, b_in, lb_logits, hg_norm_w, pool_w, pool_scale, w_down_hg, w_down_pool, w_out, final_norm_w, loss_target, m_meta_tokens, m_norm_w, m_w_in, m_b_in, m_lb_logits, m_hg_norm_w, m_pool_w, m_pool_scale, m_w_down_hg, m_w_down_pool, m_w_out, m_final_norm_w, v_meta_tokens, v_norm_w, v_w_in, v_b_in, v_lb_logits, v_hg_norm_w, v_pool_w, v_pool_scale, v_w_down_hg, v_w_down_pool, v_w_out, v_final_norm_w):
    seq = x.shape[1]

    meta_full = _all_gather_small(meta_tokens).transpose(1, 0, 2).reshape(N_META, D_MODEL)
    w_rest = jnp.concatenate([w_down_hg[0].astype(BF16), w_down_pool[0].astype(BF16), w_out[0].astype(BF16),
                              pool_w[0].astype(BF16).reshape(32, 1024)], axis=0)

    zp = jnp.concatenate([jnp.zeros((PAD_ROWS, D_MODEL), F32), meta_full, x[0]], axis=0)
    p, h, w_blocks = _in_projection(zp, norm_w, w_in[0].astype(BF16), b_in.reshape(N_COLBLK, 1, 1024), _gather_units())
    o, states, scores, rest = _hgrn_forward(p, lb_logits, w_rest)
    wdh = rest[:, REST_W_DOWN_HG:REST_W_DOWN_HG + 128].reshape(1024, 1024)
    wdp = rest[:, REST_W_DOWN_POOL:REST_W_DOWN_POOL + 128].reshape(1024, 1024)
    wout = rest[:, REST_W_OUT:REST_W_OUT + 128].reshape(1024, 1024)
    pw = rest[:, REST_POOL_W:REST_POOL_W + 32].reshape(N_DEV, 4, 32, 256).transpose(1, 0, 2, 3).reshape(4, 256, 256)
    (d_o, dp_b, dz2, grad_lhs, grad_rhs, dbias_b, d_hgw, d_pw, d_ps, d_fnw, loss_part) = _tail(
        p, o, zp, loss_target[0], hg_norm_w, pw, pool_scale, wdh, wdp, wout, final_norm_w.reshape(1, D_MODEL))
    n_a = N_COLBLK - dp_b.shape[0]
    g_hi = _weight_grad(h, dp_b, "weight_grad_in_hi")
    g_rows = _weight_grad(grad_lhs, grad_rhs, "weight_grad_rows")
    dp_a, dbias_a, d_lb, recv_hi, recv_rows = _hgrn_backward(p, lb_logits, states, scores, d_o, g_hi, n_a, g_rows)
    assert n_a == LOW_OWNERS
    chip_lo = _weight_grad_low(h, dp_a)
    dz_seq, dz_head, d_nw, recv_lo = _in_projection_backward(dp_a, dp_b, w_blocks, zp, dz2, norm_w, chip_lo)

    lb = jax.nn.sigmoid(lb_logits[0:1] - lb_logits[1:2])
    d_l0 = d_lb * lb * (1.0 - lb)
    replicated = jnp.concatenate([dbias_a.reshape(3, 1024), dbias_b.reshape(5, 1024), d_nw, d_l0, -d_l0, d_hgw, d_ps, d_fnw,
                                  jnp.pad(loss_part, ((0, MISC_ROWS - MISC_LOSS - 1), (0, 1023)))], axis=0)
    d_meta = dz_head[PAD_ROWS:CHUNK].reshape(N_META, N_DEV, 128).transpose(1, 0, 2)
    d_pw_blocks = d_pw.reshape(4, N_DEV, 32, 256).transpose(1, 0, 2, 3).reshape(N_DEV, 32, 1024)
    g_misc = jnp.concatenate([d_pw_blocks, jnp.pad(d_meta, ((0, 0), (0, 0), (0, 1024 - 128))),
                              jnp.broadcast_to(replicated[None], (N_DEV, 16, 1024))], axis=1)

    as_rows = lambda t, n: t.reshape(n, 1024)
    small = [(MISC_POOL_W, 1024, tuple(as_rows(t, 32) for t in (pool_w, m_pool_w, v_pool_w))),
             (MISC_META, 128, (meta_tokens, m_meta_tokens, v_meta_tokens)),
             (MISC_B_IN, 1024, tuple(as_rows(t, 8) for t in (b_in, m_b_in, v_b_in))),
             (MISC_NORM_W, 1024, (norm_w, m_norm_w, v_norm_w)),
             (MISC_LB, 1024, (lb_logits, m_lb_logits, v_lb_logits)),
             (MISC_HG_NORM_W, 1024, (hg_norm_w, m_hg_norm_w, v_hg_norm_w)),
             (MISC_POOL_SCALE, 1024, (pool_scale, m_pool_scale, v_pool_scale)),
             (MISC_FINAL_NORM_W, 1024, tuple(as_rows(t, 1) for t in (final_norm_w, m_final_norm_w, v_final_norm_w)))]
    res, loss = _finish(recv_hi, recv_lo, n_a, recv_rows, _exchange_small(g_misc), (w_in, m_w_in, v_w_in),
                  [(w_out, m_w_out, v_w_out), (w_down_hg, m_w_down_hg, v_w_down_hg), (w_down_pool, m_w_down_pool, v_w_down_pool)],
                  small)
    r_w_in, r_w_out, r_wdh, r_wdp, r_pw, r_meta, r_b_in, r_nw, r_lb, r_hgw, r_ps, r_fnw = res
    grad_x = dz_seq.reshape(1, seq, D_MODEL)
    per_kind = [(r_meta[k], r_nw[k], r_w_in[k], r_b_in[k].reshape(1, 8192), r_lb[k], r_hgw[k], r_pw[k].reshape(1, 4, 32, 256),
                 r_ps[k], r_wdh[k], r_wdp[k], r_w_out[k], r_fnw[k].reshape(1024)) for k in range(4)]
    return (loss, grad_x, *per_kind[0], *per_kind[1], *per_kind[2], *per_kind[3])
```

```python
import functools

import jax
import jax.numpy as jnp
from jax import lax
from jax.experimental import pallas as pl
from jax.experimental.pallas import tpu as pltpu

F32 = jnp.float32
BF16 = jnp.bfloat16

D_MODEL = 1024
N_META = 16
HEADS = 8
HEAD_DIM = 128
CHUNK = 64
SUB = 8
N_SUB = CHUNK // SUB
PAD_ROWS = CHUNK - N_META
POOL_WINDOWS = (2, 4, 8, 16)
POOL_GDIM = D_MODEL // len(POOL_WINDOWS)
HALO = 16
EPS = 1e-6
N_DEV = 8
N_COLBLK = 8
ADAM_LR, ADAM_B1, ADAM_B2, ADAM_EPS, ADAM_WD, ADAM_STEP = 0.001, 0.9, 0.999, 1e-08, 0.01, 10

VMEM_LIMIT = 56 * 1024 * 1024
MESH = pl.DeviceIdType.MESH
ANY = pl.BlockSpec(memory_space=pl.ANY)

REST_W_DOWN_HG = 0
REST_W_DOWN_POOL = 128
REST_W_OUT = 256
REST_POOL_W = 384
MISC_POOL_W = 0
MISC_META = 32
MISC_B_IN = 48
MISC_NORM_W = 56
MISC_LB = 57
MISC_HG_NORM_W = 59
MISC_POOL_SCALE = 60
MISC_FINAL_NORM_W = 61
MISC_LOSS = 62
MISC_ROWS = 64


def _params(sem=None):
    return pltpu.CompilerParams(dimension_semantics=sem, vmem_limit_bytes=VMEM_LIMIT)


def _row_tile(n_rows, prefer):
    best = 16
    for t in range(16, prefer + 1, 16):
        if n_rows % t == 0:
            best = t
    return best


def _sigmoid_pair(x):
    e = jnp.exp(-jnp.abs(x))
    r = 1.0 / (1.0 + e)
    er = e * r
    pos = x >= 0
    return jnp.where(pos, r, er), jnp.where(pos, er, r)


def _dot(a, b):
    return jnp.dot(a.astype(BF16), b.astype(BF16), preferred_element_type=F32)


def _dot_nt(a, b):
    return lax.dot_general(a.astype(BF16), b.astype(BF16), (((1,), (1,)), ((), ())), preferred_element_type=F32)


def _dot_tn(a, b):
    return lax.dot_general(a.astype(BF16), b.astype(BF16), (((0,), (0,)), ((), ())), preferred_element_type=F32)


def _device_index(px, py, pc):
    return 4 * px + 2 * py + pc


def _direct_gather(src_ref, dst_ref, send_sems, recv_sems, local_sem):
    x, y, c = lax.axis_index("x"), lax.axis_index("y"), lax.axis_index("c")
    own = pltpu.make_async_copy(src_ref, dst_ref.at[_device_index(x, y, c)], local_sem)
    sends, arrivals = [], []
    for k in range(1, N_DEV):
        peer = (1 - x if k & 4 else x, 1 - y if k & 2 else y, 1 - c if k & 1 else c)
        for slot, out in ((_device_index(x, y, c), sends), (_device_index(*peer), arrivals)):
            out.append(pltpu.make_async_remote_copy(
                src_ref=src_ref, dst_ref=dst_ref.at[slot], send_sem=send_sems.at[k - 1], recv_sem=recv_sems.at[k - 1],
                device_id=peer, device_id_type=MESH))
    return own, sends, arrivals


GATHER_SEMS = [pltpu.SemaphoreType.DMA((N_DEV - 1,)), pltpu.SemaphoreType.DMA((N_DEV - 1,)), pltpu.SemaphoreType.DMA(())]


def _padded_rows(meta_shard, x_rows):
    seq, d_model = x_rows.shape
    n_meta, shard_cols = meta_shard.shape

    def body(m_ref, x_ref, zeros_ref, zp_ref, gathered, send_sems, recv_sems, local_sem, row_sems):
        own, sends, arrivals = _direct_gather(m_ref, gathered, send_sems, recv_sems, local_sem)
        bulk = [pltpu.make_async_copy(x_ref, zp_ref.at[pl.ds(CHUNK, seq), :], row_sems.at[0]),
                pltpu.make_async_copy(zeros_ref, zp_ref.at[pl.ds(0, PAD_ROWS), :], row_sems.at[1])]
        for cp in [own] + sends + bulk:
            cp.start()
        for cp in arrivals:
            cp.wait_recv()
        own.wait()
        meta = [pltpu.make_async_copy(gathered.at[j], zp_ref.at[pl.ds(PAD_ROWS, n_meta), pl.ds(j * shard_cols, shard_cols)],
                                      row_sems.at[2 + j]) for j in range(N_DEV)]
        for cp in meta:
            cp.start()
        for cp in sends:
            cp.wait_send()
        for cp in bulk + meta:
            cp.wait()

    return pl.pallas_call(
        body, name="padded_rows_gather",
        out_shape=jax.ShapeDtypeStruct((CHUNK + seq, d_model), x_rows.dtype),
        in_specs=[ANY, ANY, ANY], out_specs=ANY,
        scratch_shapes=[pltpu.VMEM((N_DEV,) + meta_shard.shape, meta_shard.dtype)] + list(GATHER_SEMS)
        + [pltpu.SemaphoreType.DMA((2 + N_DEV,))],
    )(meta_shard, x_rows, jnp.zeros((PAD_ROWS, d_model), x_rows.dtype))


def _peer(k):
    x, y, c = lax.axis_index("x"), lax.axis_index("y"), lax.axis_index("c")
    return (1 - x if k & 4 else x, 1 - y if k & 2 else y, 1 - c if k & 1 else c)


def _me():
    return _device_index(lax.axis_index("x"), lax.axis_index("y"), lax.axis_index("c"))


def _remote(src, dst, send_sem, recv_sem, peer_bits):
    return pltpu.make_async_remote_copy(src_ref=src, dst_ref=dst, send_sem=send_sem, recv_sem=recv_sem,
                                        device_id=_peer(peer_bits), device_id_type=MESH)


N_ROW_GRADS = 3
SCATTER_SEMS = [pltpu.SemaphoreType.DMA((N_DEV - 1,)), pltpu.SemaphoreType.DMA((N_DEV - 1,)), pltpu.SemaphoreType.DMA(())]
SCATTER_ROWS_SEMS = [pltpu.SemaphoreType.DMA((7 * N_ROW_GRADS,)), pltpu.SemaphoreType.DMA((7 * N_ROW_GRADS,)),
                     pltpu.SemaphoreType.DMA((N_ROW_GRADS,))]


def _scatter_slabs(g_ref, first, recv_ref, send_sems, recv_sems, local_sem):
    n = g_ref.shape[0]
    me = _me()

    def each(on_send, on_local, on_arrival):
        for kk in range(1, N_DEV):
            peer = jnp.bitwise_xor(me, kk)

            @pl.when((peer >= first) & (peer < first + n))
            def _(kk=kk, peer=peer):
                on_send(_remote(g_ref.at[peer - first], recv_ref.at[me], send_sems.at[kk - 1], recv_sems.at[kk - 1], kk))

        @pl.when((me >= first) & (me < first + n))
        def _():
            on_local(pltpu.make_async_copy(g_ref.at[me - first], recv_ref.at[me], local_sem))
            if on_arrival is not None:
                for kk in range(1, N_DEV):
                    on_arrival(_remote(g_ref.at[0], recv_ref.at[jnp.bitwise_xor(me, kk)], send_sems.at[kk - 1],
                                       recv_sems.at[kk - 1], kk))

    start = lambda: each(lambda cp: cp.start(), lambda cp: cp.start(), None)
    finish = lambda: each(lambda cp: cp.wait_send(), lambda cp: cp.wait(), lambda cp: cp.wait_recv())
    return start, finish


def _scatter_rows(g_ref, recv_ref, send_sems, recv_sems, local_sems):
    me = _me()
    rows = lambda m, dev: g_ref.at[m, pl.ds(dev * 128, 128), :]

    def copies():
        local = [pltpu.make_async_copy(rows(m, me), recv_ref.at[me, m], local_sems.at[m]) for m in range(N_ROW_GRADS)]
        sends, arrivals = [], []
        for m in range(N_ROW_GRADS):
            for kk in range(1, N_DEV):
                peer, sems = jnp.bitwise_xor(me, kk), (send_sems.at[7 * m + kk - 1], recv_sems.at[7 * m + kk - 1])
                sends.append(_remote(rows(m, peer), recv_ref.at[me, m], *sems, kk))
                arrivals.append(_remote(rows(m, me), recv_ref.at[peer, m], *sems, kk))
        return local, sends, arrivals

    def start():
        local, sends, _ = copies()
        for cp in local + sends:
            cp.start()

    def finish():
        local, sends, arrivals = copies()
        for cp in arrivals:
            cp.wait_recv()
        for cp in sends:
            cp.wait_send()
        for cp in local:
            cp.wait()

    return start, finish


LOW_OWNERS = 3


def _weight_grad_low(h, dp_a):
    n_rows = h.shape[0]
    tk = _row_tile(n_rows, 2080)
    n_k = n_rows // tk
    order = jnp.where(lax.axis_index("c") == 0, jnp.array([1, 0, 2], jnp.int32), jnp.array([0, 2, 1], jnp.int32))

    def body(order_ref, h_ref, y_ref, out_ref, acc, send_buf, got, send_sems, recv_sems):
        t, k = pl.program_id(0), pl.program_id(1)
        c = lax.axis_index("c")

        @pl.when(k == 0)
        def _():
            acc[...] = jnp.zeros_like(acc)

        acc[...] += _dot_tn(h_ref[...], y_ref[0])

        def to_sibling(slot):
            return _remote(send_buf.at[slot], got.at[slot], send_sems.at[slot], recv_sems.at[slot], 1)

        def send(slot):
            send_buf[slot] = acc[...].astype(send_buf.dtype)
            to_sibling(slot).start()

        def keep(slot):
            to_sibling(slot).wait_recv()
            out_ref[slot] = (acc[...] + got[slot].astype(F32)).astype(out_ref.dtype)

        done = k == n_k - 1
        for core, step, action, slot in ((0, 0, send, 0), (0, 1, keep, 0), (0, 2, keep, 1),
                                         (1, 0, send, 0), (1, 1, send, 1), (1, 2, keep, 0)):
            @pl.when(done & (c == core) & (t == step))
            def _(action=action, slot=slot):
                action(slot)

        @pl.when(done & (t == LOW_OWNERS - 1))
        def _():
            to_sibling(0).wait_send()

            @pl.when(c == 1)
            def _():
                to_sibling(1).wait_send()
                out_ref[1] = jnp.zeros(out_ref.shape[1:], out_ref.dtype)

    pair = (2, h.shape[1], dp_a.shape[2])
    return pl.pallas_call(
        body, name="weight_grad_in_low",
        out_shape=jax.ShapeDtypeStruct(pair, BF16),
        grid_spec=pltpu.PrefetchScalarGridSpec(
            num_scalar_prefetch=1, grid=(LOW_OWNERS, n_k),
            in_specs=[pl.BlockSpec((tk, h.shape[1]), lambda t, k, o: (k, 0)),
                      pl.BlockSpec((1, tk, dp_a.shape[2]), lambda t, k, o: (o[t], k, 0))],
            out_specs=pl.BlockSpec(pair, lambda t, k, o: (0, 0, 0)),
            scratch_shapes=[pltpu.VMEM(pair[1:], F32), pltpu.VMEM(pair, BF16), pltpu.VMEM(pair, BF16),
                            pltpu.SemaphoreType.DMA((2,)), pltpu.SemaphoreType.DMA((2,))]),
        compiler_params=_params(("arbitrary", "arbitrary")),
    )(order, h, dp_a)


def _exchange_small(misc):
    def body(m_ref, out_ref, send_sems, recv_sems, local_sem):
        start, finish = _scatter_slabs(m_ref, 0, out_ref, send_sems, recv_sems, local_sem)
        start()
        finish()

    return pl.pallas_call(
        body, name="exchange_small", out_shape=jax.ShapeDtypeStruct(misc.shape, misc.dtype),
        in_specs=[ANY], out_specs=ANY, scratch_shapes=list(SCATTER_SEMS),
    )(misc)


def _scatter_low(part_ref, recv_ref, send_sems, recv_sems, local_sem):
    x, y, c = lax.axis_index("x"), lax.axis_index("y"), lax.axis_index("c")
    chip = 2 * x + y
    routes = ((0, (0, 0, c), 0, None), (1, (0, 1, 0), 1, 0))

    def each(on_send, on_local, on_arrival):
        for slot, owner, owner_chip, core in routes:
            holds = (c == core) if core is not None else (c >= 0)
            rel = jnp.bitwise_xor(chip, owner_chip)

            @pl.when(holds & (rel != 0))
            def _(slot=slot, owner=owner, rel=rel):
                on_send(pltpu.make_async_remote_copy(
                    src_ref=part_ref.at[slot], dst_ref=recv_ref.at[chip], send_sem=send_sems.at[slot],
                    recv_sem=recv_sems.at[rel - 1], device_id=owner, device_id_type=MESH))

            @pl.when(holds & (rel == 0))
            def _(slot=slot, owner=owner, owner_chip=owner_chip):
                on_local(pltpu.make_async_copy(part_ref.at[slot], recv_ref.at[chip], local_sem))
                if on_arrival is not None:
                    for r in range(1, 4):
                        on_arrival(pltpu.make_async_remote_copy(
                            src_ref=part_ref.at[slot], dst_ref=recv_ref.at[r ^ owner_chip], send_sem=send_sems.at[slot],
                            recv_sem=recv_sems.at[r - 1], device_id=owner, device_id_type=MESH))

    start = lambda: each(lambda cp: cp.start(), lambda cp: cp.start(), None)
    finish = lambda: each(lambda cp: cp.wait_send(), lambda cp: cp.wait(), lambda cp: cp.wait_recv())
    return start, finish


def _adam_update(g, w, m, v):
    mn = ADAM_B1 * m + (1.0 - ADAM_B1) * g
    vn = ADAM_B2 * v + (1.0 - ADAM_B2) * (g * g)
    m_hat = mn / (1.0 - ADAM_B1 ** ADAM_STEP)
    v_hat = vn / (1.0 - ADAM_B2 ** ADAM_STEP)
    return -ADAM_LR * (m_hat / (jnp.sqrt(v_hat) + ADAM_EPS) + ADAM_WD * w), mn, vn


def _device_sum(parts):
    t = [p.astype(F32) for p in parts]
    return ((t[0] + t[1]) + (t[2] + t[3])) + ((t[4] + t[5]) + (t[6] + t[7]))


def _finish(a_hi, a_lo, n_lo, b3, misc, big, rows3, small):
    n_steps = 4
    tb, tr3 = 1024 // n_steps, 128 // n_steps

    def body(*refs):
        it = iter(refs)
        hi_ref, lo_ref, b_ref, m_ref = next(it), next(it), next(it), next(it)
        big_in = [next(it) for _ in range(3)]
        rows_in = [[next(it) for _ in range(3)] for _ in rows3]
        small_in = [[next(it) for _ in range(3)] for _ in small]
        big_out = [next(it) for _ in range(4)]
        rows_out = [[next(it) for _ in range(4)] for _ in rows3]
        small_out = [[next(it) for _ in range(4)] for _ in small]
        loss_ref = next(it)

        def apply(g, ins, outs):
            d, mn, vn = _adam_update(g, ins[0][...], ins[1][...], ins[2][...])
            for r, val in zip(outs, (g, d, mn, vn)):
                r[...] = val

        lo = [lo_ref[s].astype(F32) for s in range(4)]
        g_big = jnp.where(_me() < n_lo, (lo[0] + lo[1]) + (lo[2] + lo[3]), _device_sum([hi_ref[s] for s in range(N_DEV)]))
        apply(g_big[None], big_in, big_out)
        for k in range(len(rows3)):
            apply(_device_sum([b_ref[s, k] for s in range(N_DEV)])[None], rows_in[k], rows_out[k])

        @pl.when(pl.program_id(0) == 0)
        def _():
            loss_ref[...] = _device_sum([m_ref[s, MISC_LOSS:MISC_LOSS + 1, :] for s in range(N_DEV)])
            for (row0, lanes, ins), r_in, r_out in zip(small, small_in, small_out):
                n = ins[0].shape[0]
                apply(_device_sum([m_ref[s, row0:row0 + n, :lanes] for s in range(N_DEV)]), r_in, r_out)

    whole = lambda shape: pl.BlockSpec(shape, lambda i: (0,) * len(shape))
    big_blk = pl.BlockSpec((1, tb, 1024), lambda i: (0, i, 0))
    rows_blk = pl.BlockSpec((1, tr3, 1024), lambda i: (0, i, 0))
    in_specs = [pl.BlockSpec((N_DEV, tb, 1024), lambda i: (0, i, 0)), pl.BlockSpec((4, tb, 1024), lambda i: (0, i, 0)),
                pl.BlockSpec((N_DEV, 3, tr3, 1024), lambda i: (0, 0, i, 0)), whole(misc.shape)]
    in_specs += [big_blk] * 3 + [rows_blk] * (3 * len(rows3))
    out_specs = [big_blk] * 4 + [rows_blk] * (4 * len(rows3))
    out_shape = [jax.ShapeDtypeStruct(big[0].shape, F32)] * 4
    for w, _, _ in rows3:
        out_shape += [jax.ShapeDtypeStruct(w.shape, F32)] * 4
    args = [a_hi, a_lo, b3, misc, *big]
    for t in rows3:
        args += list(t)
    for _, _, t in small:
        in_specs += [whole(t[0].shape)] * 3
        out_specs += [whole(t[0].shape)] * 4
        out_shape += [jax.ShapeDtypeStruct(t[0].shape, F32)] * 4
        args += list(t)
    out_specs.append(whole((1, 1024)))
    out_shape.append(jax.ShapeDtypeStruct((1, 1024), F32))
    outs = pl.pallas_call(
        body, name="reduce_sum_adamw", out_shape=tuple(out_shape), grid=(n_steps,),
        in_specs=in_specs, out_specs=tuple(out_specs),
        compiler_params=_params(("arbitrary",)),
    )(*args)
    return [tuple(outs[4 * k:4 * k + 4]) for k in range(len(outs) // 4)], outs[-1][0, 0]


GATHER_UNITS = ((0, 0), (0, 1), (1, 0), (1, 1)) + tuple((place, 0) for place in range(2, 8)) + tuple(
    (place, 1) for place in range(2, 8))


def _gather_order():
    x, y, c = lax.axis_index("x"), lax.axis_index("y"), lax.axis_index("c")
    chips = [(1 - x, y), (x, 1 - y), (1 - x, 1 - y)]
    order = [_device_index(x, y, c), _device_index(x, y, 1 - c)]
    order += [_device_index(*q, c) for q in chips] + [_device_index(*q, 1 - c) for q in chips]
    return order


def _gather_units():
    order = _gather_order()
    blocks = jnp.stack([order[place] for place, _ in GATHER_UNITS]).astype(jnp.int32)
    return blocks, jnp.array([half for _, half in GATHER_UNITS], jnp.int32)


def _in_projection(zp, norm_w, w_shard, b_blocks, units):
    n_rows = zp.shape[0]
    tr = _row_tile(n_rows, 832)
    nt = n_rows // tr

    half_cols = 1024 // 2
    n_units = len(GATHER_UNITS)

    def body(blocks_ref, halves_ref, z_ref, nw_ref, w_hbm, b_ref, p_ref, h_ref, w_out, w_vmem, h_all,
             send_sems, recv_sems, local_sem, out_sems):
        s, i = pl.program_id(0), pl.program_id(1)
        x, y, c = lax.axis_index("x"), lax.axis_index("y"), lax.axis_index("c")
        me, sibling = (x, y, c), (x, y, 1 - c)
        chips = [(1 - x, y), (x, 1 - y), (1 - x, 1 - y)]

        def slot(blk, half):
            return w_vmem.at[_device_index(*blk), half]

        def mine(half):
            return w_hbm.at[:, pl.ds(half * half_cols, half_cols)]

        def copy(k, half, blk, to, own_block=False):
            return pltpu.make_async_remote_copy(
                src_ref=mine(half) if own_block else slot(blk, half), dst_ref=slot(blk, half),
                send_sem=send_sems.at[2 * k + half], recv_sem=recv_sems.at[2 * k + half], device_id=to, device_id_type=MESH)

        own = [pltpu.make_async_copy(mine(half), slot(me, half), local_sem.at[half]) for half in range(2)]
        first = [copy(k, half, me, to, own_block=True) for half in range(2)
                 for k, to in enumerate([sibling] + [(*q, c) for q in chips])]
        passed = [[copy(4 + j, half, (*q, c), sibling) for j, q in enumerate(chips)] for half in range(2)]
        sources = [None, sibling] + [(*q, c) for q in chips] + [(*q, 1 - c) for q in chips]
        arrival = lambda place, half: copy(place - 1, half, sources[place], me)

        def keep(unit):
            blk, half = blocks_ref[unit], halves_ref[unit]
            return pltpu.make_async_copy(w_vmem.at[blk, half], w_out.at[blk, half], out_sems.at[unit])

        for unit, (place, half) in enumerate(GATHER_UNITS):
            @pl.when((i == 0) & (s == unit))
            def _(unit=unit, place=place, half=half):
                if unit == 0:
                    for cp in own + first:
                        cp.start()
                if place == 0:
                    own[half].wait()
                else:
                    arrival(place, half).wait_recv()
                    if 2 <= place <= 4:
                        passed[half][place - 2].start()
                keep(unit).start()

        @pl.when(s == 0)
        def _():
            z = z_ref[...]
            r = lax.rsqrt(jnp.mean(z * z, axis=-1, keepdims=True) + EPS)
            h = (z * r * nw_ref[...]).astype(BF16)
            h_all[i] = h
            h_ref[...] = h

        p_ref[0] = jnp.dot(h_all[i], w_vmem[blocks_ref[s], halves_ref[s]], preferred_element_type=F32) + b_ref[0]

        @pl.when((s == n_units - 1) & (i == nt - 1))
        def _():
            for cp in first + passed[0] + passed[1]:
                cp.wait_send()
            for unit in range(n_units):
                keep(unit).wait()

    first_pass = lambda s, i, rest: jnp.where(s == 0, i, rest)
    return pl.pallas_call(
        body, name="in_projection_gather",
        out_shape=(jax.ShapeDtypeStruct((N_COLBLK, n_rows, 1024), F32),
                   jax.ShapeDtypeStruct((n_rows, D_MODEL), BF16),
                   jax.ShapeDtypeStruct((N_DEV, 2, D_MODEL, half_cols), BF16)),
        grid_spec=pltpu.PrefetchScalarGridSpec(
            num_scalar_prefetch=2, grid=(n_units, nt),
            in_specs=[pl.BlockSpec((tr, D_MODEL), lambda s, i, blk, hf: (first_pass(s, i, 0), 0)),
                      pl.BlockSpec((1, D_MODEL), lambda s, i, blk, hf: (0, 0)), ANY,
                      pl.BlockSpec((1, 1, half_cols), lambda s, i, blk, hf: (blk[s], 0, hf[s]))],
            out_specs=(pl.BlockSpec((1, tr, half_cols), lambda s, i, blk, hf: (blk[s], i, hf[s])),
                       pl.BlockSpec((tr, D_MODEL), lambda s, i, blk, hf: (first_pass(s, i, nt - 1), 0)), ANY),
            scratch_shapes=[pltpu.VMEM((N_DEV, 2, D_MODEL, half_cols), BF16), pltpu.VMEM((nt, tr, D_MODEL), BF16),
                            pltpu.SemaphoreType.DMA((14,)), pltpu.SemaphoreType.DMA((14,)), pltpu.SemaphoreType.DMA((2,)),
                            pltpu.SemaphoreType.DMA((n_units,))]),
        compiler_params=_params(("arbitrary", "arbitrary")),
    )(*units, zp, norm_w, w_shard, b_blocks)


def _lower_bound(lb_ref):
    l0, l1 = lb_ref[0:1, :], lb_ref[1:2, :]
    _, lb = _sigmoid_pair(l1 - l0)
    return lb


def _chunk_gates(fz, lb, valid):
    sig, nsig = _sigmoid_pair(fz)
    f = lb + (1.0 - lb) * sig
    g2 = jnp.where(valid, jnp.log2(f), 0.0)
    k = jnp.where(valid, (1.0 - lb) * nsig, 0.0)
    return sig, nsig, f, g2, k


def _tri(n, upper=False):
    r = lax.broadcasted_iota(jnp.int32, (n, n), 0)
    c = lax.broadcasted_iota(jnp.int32, (n, n), 1)
    return jnp.where((r <= c) if upper else (r >= c), 1.0, 0.0).astype(BF16)


def _tri_dot(tri, x):
    hi = x.astype(BF16)
    rest = x - hi.astype(F32)
    mid = rest.astype(BF16)
    low = (rest - mid.astype(F32)).astype(BF16)
    return (jnp.dot(tri, hi, preferred_element_type=F32) + jnp.dot(tri, mid, preferred_element_type=F32)
            + jnp.dot(tri, low, preferred_element_type=F32))


def _intra_scores(q_ref, k_ref, b2_ref, col0):
    cols = pl.ds(col0, HEAD_DIM)
    rows_s = lax.broadcasted_iota(jnp.int32, (SUB, 1), 0)
    lanes_c = lax.broadcasted_iota(jnp.int32, (1, CHUNK), 1)
    blocks = []
    for i in range(N_SUB):
        lo = i * SUB
        qi = q_ref[lo:lo + SUB, cols]
        bi = b2_ref[lo:lo + SUB, cols]
        if i == 0:
            acc = jnp.zeros((SUB, CHUNK), F32)
        else:
            ref_i = b2_ref[lo:lo + 1, cols]
            qt = qi * jnp.exp2(bi - ref_i)
            kt = jnp.concatenate([k_ref[0:lo, cols] * jnp.exp2(ref_i - b2_ref[0:lo, cols]),
                                  jnp.zeros((CHUNK - lo, HEAD_DIM), F32)], axis=0)
            acc = _dot_nt(qt, kt)
        for s in range(SUB):
            b_s = b2_ref[lo + s:lo + s + 1, cols]
            k_s = k_ref[lo + s:lo + s + 1, cols]
            w = jnp.exp2(jnp.minimum(bi - b_s, 0.0))
            col = jnp.sum((qi * w) * k_s, axis=-1, keepdims=True)
            acc = jnp.where(lanes_c == lo + s, col, acc)
        blocks.append(jnp.where(lanes_c <= lo + rows_s, acc, 0.0))
    return jnp.concatenate(blocks, axis=0)


def _hgrn_forward(p, lb_logits, w_rest):
    n_rows = p.shape[1]
    n_chunks = n_rows // CHUNK
    width = HEADS * HEAD_DIM

    def body(q_ref, fz_ref, v_ref, lb_ref, rest_ref, o_ref, st_out_ref, a_out_ref, rest_out,
             state, k_vmem, b2_vmem, send_sems, recv_sems, local_sem):
        n = pl.program_id(0)
        own, sends, arrivals = _direct_gather(rest_ref, rest_out, send_sems, recv_sems, local_sem)

        @pl.when(n == 0)
        def _():
            state[...] = jnp.zeros_like(state)
            own.start()
            for cp in sends:
                cp.start()

        rows = n * CHUNK + lax.broadcasted_iota(jnp.int32, (CHUNK, 1), 0)
        valid = rows >= PAD_ROWS
        lb = _lower_bound(lb_ref)
        _, _, _, g2, k = _chunk_gates(fz_ref[0], lb, valid)
        k_vmem[...] = k
        b2_vmem[...] = _tri_dot(_tri(CHUNK), g2)
        q_view = q_ref.at[0]
        for h in range(HEADS):
            cols = pl.ds(h * HEAD_DIM, HEAD_DIM)
            st = state[h]
            st_out_ref[0, h] = st
            bh = b2_vmem[:, cols]
            kh = k_vmem[:, cols]
            vh = jnp.where(valid, v_ref[0, :, cols], 0.0)
            qe = q_ref[0, :, cols] * jnp.exp2(bh)
            a = _intra_scores(q_view, k_vmem, b2_vmem, h * HEAD_DIM).astype(BF16)
            a_out_ref[0, h] = a
            o_ref[:, cols] = _dot_nt(qe, st) + _dot(a, vh)
            b_last = b2_vmem[CHUNK - 1:CHUNK, cols]
            kd = kh * jnp.exp2(b_last - bh)
            state[h] = st * jnp.exp2(b_last) + _dot_tn(vh, kd)

        @pl.when(n == n_chunks - 1)
        def _():
            for cp in arrivals:
                cp.wait_recv()
            for cp in sends:
                cp.wait_send()
            own.wait()

    blk = lambda c: pl.BlockSpec((1, CHUNK, width), lambda n, c=c: (c, n, 0))
    return pl.pallas_call(
        body, name="hgrn_forward",
        out_shape=(jax.ShapeDtypeStruct((n_rows, width), F32),
                   jax.ShapeDtypeStruct((n_chunks, HEADS, HEAD_DIM, HEAD_DIM), F32),
                   jax.ShapeDtypeStruct((n_chunks, HEADS, CHUNK, CHUNK), BF16),
                   jax.ShapeDtypeStruct((N_DEV,) + w_rest.shape, w_rest.dtype)),
        grid=(n_chunks,),
        in_specs=[blk(0), blk(1), blk(2), pl.BlockSpec((2, width), lambda n: (0, 0)), ANY],
        out_specs=(pl.BlockSpec((CHUNK, width), lambda n: (n, 0)),
                   pl.BlockSpec((1, HEADS, HEAD_DIM, HEAD_DIM), lambda n: (n, 0, 0, 0)),
                   pl.BlockSpec((1, HEADS, CHUNK, CHUNK), lambda n: (n, 0, 0, 0)), ANY),
        scratch_shapes=[pltpu.VMEM((HEADS, HEAD_DIM, HEAD_DIM), F32), pltpu.VMEM((CHUNK, width), F32),
                        pltpu.VMEM((CHUNK, width), F32)] + list(GATHER_SEMS),
        compiler_params=_params(("arbitrary",)),
    )(p, p, p, lb_logits, w_rest)


def _hgrn_backward(p, lb_logits, states, scores, d_o, g_slabs, first_owner, g_rows):
    n_rows = p.shape[1]
    n_chunks = n_rows // CHUNK
    width = HEADS * HEAD_DIM

    def body(q_ref, fz_ref, v_ref, lb_ref, st_ref, a_ref, do_ref, gs_hbm, gr_hbm, dp_ref, dbias_ref, dlb_ref, rs_hbm, rr_hbm,
             dstate, k_vmem, b2_vmem, *sems):
        step = pl.program_id(0)
        n = n_chunks - 1 - step
        start_slabs, finish_slabs = _scatter_slabs(gs_hbm, first_owner, rs_hbm, *sems[:3])
        start_rows, finish_rows = _scatter_rows(gr_hbm, rr_hbm, *sems[3:])

        @pl.when(step == 0)
        def _():
            dstate[...] = jnp.zeros_like(dstate)
            dbias_ref[...] = jnp.zeros_like(dbias_ref)
            dlb_ref[...] = jnp.zeros_like(dlb_ref)
            start_slabs()
            start_rows()

        rows = n * CHUNK + lax.broadcasted_iota(jnp.int32, (CHUNK, 1), 0)
        valid = rows >= PAD_ROWS
        lb = _lower_bound(lb_ref)
        sig, nsig, f, g2, k = _chunk_gates(fz_ref[0], lb, valid)
        k_vmem[...] = k
        b2_vmem[...] = _tri_dot(_tri(CHUNK), g2)
        rows_c = lax.broadcasted_iota(jnp.int32, (CHUNK, 1), 0)
        rows_s = lax.broadcasted_iota(jnp.int32, (SUB, 1), 0)
        lanes_c = lax.broadcasted_iota(jnp.int32, (1, CHUNK), 1)
        causal = lax.broadcasted_iota(jnp.int32, (CHUNK, CHUNK), 0) >= lax.broadcasted_iota(jnp.int32, (CHUNK, CHUNK), 1)
        tri_up = _tri(CHUNK, upper=True)
        for h in range(HEADS):
            cols = pl.ds(h * HEAD_DIM, HEAD_DIM)
            st = st_ref[0, h]
            dst = dstate[h]
            qh = q_ref[0, :, cols]
            bh = b2_vmem[:, cols]
            kh = k_vmem[:, cols]
            vh = jnp.where(valid, v_ref[0, :, cols], 0.0)
            doh = do_ref[:, cols]
            eb = jnp.exp2(bh)
            qe = qh * eb
            b_last = b2_vmem[CHUNK - 1:CHUNK, cols]
            e_last = jnp.exp2(b_last)
            decay_k = jnp.exp2(b_last - bh)
            kd = kh * decay_k
            dqe = _dot(doh, st)
            da = jnp.where(causal, _dot_nt(doh, vh), 0.0)
            dv = _dot_tn(a_ref[0, h], doh) + _dot_nt(kd, dst)
            dkd = _dot(vh, dst)
            dstate[h] = dst * e_last + _dot_tn(doh, qe)
            db_last = (jnp.sum(dst * st, axis=0, keepdims=True) * e_last
                       + jnp.sum(dkd * kd, axis=0, keepdims=True))
            dq_blocks, dk_blocks = [], []
            dk_earlier = jnp.zeros((CHUNK, HEAD_DIM), F32)
            for i in range(N_SUB):
                lo = i * SUB
                qi = q_ref[0, lo:lo + SUB, cols]
                bi = b2_vmem[lo:lo + SUB, cols]
                da_i = da[lo:lo + SUB, :]
                if i == 0:
                    dq_i = jnp.zeros((SUB, HEAD_DIM), F32)
                else:
                    ref_i = b2_vmem[lo:lo + 1, cols]
                    eq = jnp.exp2(bi - ref_i)
                    ek = jnp.exp2(ref_i - b2_vmem[0:lo, cols])
                    later = jnp.zeros((CHUNK - lo, HEAD_DIM), F32)
                    kt = jnp.concatenate([k_vmem[0:lo, cols] * ek, later], axis=0)
                    dq_i = _dot(da_i, kt) * eq
                    dk_earlier = dk_earlier + jnp.concatenate([_dot_tn(da_i, qi * eq)[0:lo] * ek, later], axis=0)
                dk_i = jnp.zeros((SUB, HEAD_DIM), F32)
                for s in range(SUB):
                    b_s = b2_vmem[lo + s:lo + s + 1, cols]
                    k_s = k_vmem[lo + s:lo + s + 1, cols]
                    w = jnp.exp2(jnp.minimum(bi - b_s, 0.0))
                    da_col = jnp.sum(jnp.where(lanes_c == lo + s, da_i, 0.0), axis=-1, keepdims=True)
                    gw = da_col * w
                    dq_i = dq_i + gw * k_s
                    dk_i = jnp.where(rows_s == s, jnp.sum(gw * qi, axis=0, keepdims=True), dk_i)
                dq_blocks.append(dq_i)
                dk_blocks.append(dk_i)
            dq_intra = jnp.concatenate(dq_blocks, axis=0)
            dk_intra = jnp.concatenate(dk_blocks, axis=0) + dk_earlier
            dq = dqe * eb + dq_intra
            dk = dkd * decay_k + dk_intra
            db = dqe * qe - dkd * kd + qh * dq_intra - kh * dk_intra
            db = db + jnp.where(rows_c == CHUNK - 1, db_last, 0.0)
            dg = _tri_dot(tri_up, db)
            fh = f[:, h * HEAD_DIM:(h + 1) * HEAD_DIM]
            sh = sig[:, h * HEAD_DIM:(h + 1) * HEAD_DIM]
            nh = nsig[:, h * HEAD_DIM:(h + 1) * HEAD_DIM]
            lbh = lb[:, h * HEAD_DIM:(h + 1) * HEAD_DIM]
            df = jnp.where(valid, dg / fh - dk, 0.0)
            dfz = df * (1.0 - lbh) * sh * nh
            dq = jnp.where(valid, dq, 0.0)
            dv = jnp.where(valid, dv, 0.0)
            dlb_ref[:, cols] += jnp.sum(df * nh, axis=0, keepdims=True)
            dp_ref[0, :, cols] = dq.astype(BF16)
            dp_ref[1, :, cols] = dfz.astype(BF16)
            dp_ref[2, :, cols] = dv.astype(BF16)
            dbias_ref[0, :, cols] += jnp.sum(dq, axis=0, keepdims=True)
            dbias_ref[1, :, cols] += jnp.sum(dfz, axis=0, keepdims=True)
            dbias_ref[2, :, cols] += jnp.sum(dv, axis=0, keepdims=True)

        @pl.when(step == n_chunks - 1)
        def _():
            finish_slabs()
            finish_rows()

    rev = lambda s: n_chunks - 1 - s
    blk = lambda c: pl.BlockSpec((1, CHUNK, width), lambda s, c=c: (c, rev(s), 0))
    return pl.pallas_call(
        body, name="hgrn_backward",
        out_shape=(jax.ShapeDtypeStruct((3, n_rows, width), BF16),
                   jax.ShapeDtypeStruct((3, 1, width), F32),
                   jax.ShapeDtypeStruct((1, width), F32),
                   jax.ShapeDtypeStruct((N_DEV,) + g_slabs.shape[1:], g_slabs.dtype),
                   jax.ShapeDtypeStruct((N_DEV, N_ROW_GRADS, 128, g_rows.shape[2]), g_rows.dtype)),
        grid=(n_chunks,),
        in_specs=[blk(0), blk(1), blk(2), pl.BlockSpec((2, width), lambda s: (0, 0)),
                  pl.BlockSpec((1, HEADS, HEAD_DIM, HEAD_DIM), lambda s: (rev(s), 0, 0, 0)),
                  pl.BlockSpec((1, HEADS, CHUNK, CHUNK), lambda s: (rev(s), 0, 0, 0)),
                  pl.BlockSpec((CHUNK, width), lambda s: (rev(s), 0)), ANY, ANY],
        out_specs=(pl.BlockSpec((3, CHUNK, width), lambda s: (0, rev(s), 0)),
                   pl.BlockSpec((3, 1, width), lambda s: (0, 0, 0)),
                   pl.BlockSpec((1, width), lambda s: (0, 0)), ANY, ANY),
        scratch_shapes=[pltpu.VMEM((HEADS, HEAD_DIM, HEAD_DIM), F32), pltpu.VMEM((CHUNK, width), F32),
                        pltpu.VMEM((CHUNK, width), F32)] + list(SCATTER_SEMS) + list(SCATTER_ROWS_SEMS),
        compiler_params=_params(("arbitrary",)),
    )(p, p, p, lb_logits, states, scores, d_o, g_slabs, g_rows)


def _sigmoid_and_complement(x):
    s = 0.5 * jnp.tanh(0.5 * x) + 0.5
    return s, 1.0 - s


def _silu_and_grad(x):
    s, ns = _sigmoid_and_complement(x)
    return x * s, s * (1.0 + x * ns)


def _tail(p, o, zp, tgt, hg_norm_w, pool_w, pool_scale, w_down_hg, w_down_pool, w_out, final_norm_w):
    n_rows = zp.shape[0]
    tr = _row_tile(n_rows, 208)
    nt = n_rows // tr
    ext = tr + HALO
    n_groups = len(POOL_WINDOWS)

    def body(o_ref, ghg_ref, u_ref, gpool_ref, mhg_ref, mpool_ref, uhalo_ref, z_ref, tgt_hbm,
             hgw_ref, pw_ref, ps_ref, wdh_ref, wdp_ref, wout_ref, fnw_ref,
             do_ref, dp_ref, dz2_ref, lhs_ref, rhs_ref,
             dbias_ref, dhgw_ref, dpw_ref, dps_ref, dfnw_ref, loss_ref, halo_vmem, tgt_buf, tgt_sems):
        step = pl.program_id(0)
        ti = nt - 1 - step

        def target_rows(tile, slot, act):
            @pl.when(tile == 0)
            def _():
                cp = pltpu.make_async_copy(tgt_hbm.at[pl.ds(0, tr - CHUNK), :], tgt_buf.at[slot, pl.ds(CHUNK, tr - CHUNK), :],
                                           tgt_sems.at[slot])
                getattr(cp, act)()

            @pl.when(tile > 0)
            def _():
                cp = pltpu.make_async_copy(tgt_hbm.at[pl.ds(tile * tr - CHUNK, tr), :], tgt_buf.at[slot], tgt_sems.at[slot])
                getattr(cp, act)()

        @pl.when(step == 0)
        def _():
            halo_vmem[...] = jnp.zeros_like(halo_vmem)
            for r in (dbias_ref, dhgw_ref, dpw_ref, dps_ref, dfnw_ref, loss_ref):
                r[...] = jnp.zeros_like(r)
            if nt <= 2:
                tgt_buf[(nt - 1) % 2, 0:CHUNK, :] = jnp.zeros((CHUNK, D_MODEL), F32)
            target_rows(ti, 0, "start")

        @pl.when(ti > 0)
        def _():
            target_rows(ti - 1, (step + 1) % 2, "start")

        rows = ti * tr + lax.broadcasted_iota(jnp.int32, (tr, 1), 0)
        valid = rows >= PAD_ROWS
        in_loss = rows >= CHUNK
        count_pos = jnp.maximum(rows - PAD_ROWS + 1, 1).astype(F32)

        o = o_ref[...]
        hgw = hgw_ref[...]
        inv_o, on_parts = [], []
        for h in range(HEADS):
            oh = o[:, h * HEAD_DIM:(h + 1) * HEAD_DIM]
            r = lax.rsqrt(jnp.mean(oh * oh, axis=-1, keepdims=True) + EPS)
            inv_o.append(r)
            on_parts.append(oh * r)
        o_hat = jnp.concatenate(on_parts, axis=1)
        o_n = o_hat * hgw
        g_hg = ghg_ref[0]
        silu_hg, dsilu_hg = _silu_and_grad(g_hg)
        a_hg = o_n * silu_hg
        y_hg = _dot(a_hg, wdh_ref[...])

        u = jnp.where(valid, u_ref[0], 0.0)
        u_prev = jnp.where(ti > 0, uhalo_ref[0], 0.0)
        u_ext = jnp.concatenate([u_prev, u], axis=0)
        pooled_parts, mixed_parts, inv_cnt = [], [], []
        for gi, win in enumerate(POOL_WINDOWS):
            lanes = slice(gi * POOL_GDIM, (gi + 1) * POOL_GDIM)
            s = u_ext[:, lanes]
            shift = 1
            while shift < win:
                s = s + pltpu.roll(s, shift, 0)
                shift *= 2
            ic = 1.0 / jnp.minimum(count_pos, float(win))
            inv_cnt.append(ic)
            pooled = s[HALO:] * ic - u[:, lanes]
            pooled_parts.append(pooled)
            mixed_parts.append(_dot(pooled, pw_ref[gi]))
        mixed = jnp.concatenate(mixed_parts, axis=1)
        ps = ps_ref[...]
        g_pool = gpool_ref[0]
        silu_pool, dsilu_pool = _silu_and_grad(g_pool)
        a_pool = mixed * ps * silu_pool
        y_pool = _dot(a_pool, wdp_ref[...])

        m_hg, m_pool = mhg_ref[0], mpool_ref[0]
        s_hg, ns_hg = _sigmoid_and_complement(m_hg)
        s_pool, ns_pool = _sigmoid_and_complement(m_pool)
        merged = s_hg * y_hg + s_pool * y_pool
        z2 = z_ref[...] + _dot(merged, wout_ref[...])
        r2 = lax.rsqrt(jnp.mean(z2 * z2, axis=-1, keepdims=True) + EPS)
        n2 = z2 * r2
        fnw = fnw_ref[...]
        target_rows(ti, step % 2, "wait")
        err = jnp.where(in_loss, n2 * fnw - tgt_buf[step % 2], 0.0)
        loss_ref[...] += jnp.sum(jnp.sum(err * err, axis=0, keepdims=True), axis=1, keepdims=True) * (0.5 / D_MODEL)
        dy = err * (1.0 / D_MODEL)

        dfnw_ref[...] += jnp.sum(dy * n2, axis=0, keepdims=True)
        gy = dy * fnw
        dz2 = r2 * (gy - n2 * jnp.mean(gy * n2, axis=-1, keepdims=True))
        dmerged = _dot_nt(dz2, wout_ref[...])
        dy_hg = s_hg * dmerged
        dy_pool = s_pool * dmerged
        dm_hg = dmerged * y_hg * s_hg * ns_hg
        dm_pool = dmerged * y_pool * s_pool * ns_pool
        da_hg = _dot_nt(dy_hg, wdh_ref[...])
        da_pool = _dot_nt(dy_pool, wdp_ref[...])

        d_on = da_hg * silu_hg
        dg_hg = da_hg * o_n * dsilu_hg
        dhgw_ref[...] += jnp.sum(d_on * o_hat, axis=0, keepdims=True)
        gyo = d_on * hgw
        do_parts = []
        for h in range(HEADS):
            lanes = slice(h * HEAD_DIM, (h + 1) * HEAD_DIM)
            gh, nh = gyo[:, lanes], o_hat[:, lanes]
            do_parts.append(inv_o[h] * (gh - nh * jnp.mean(gh * nh, axis=-1, keepdims=True)))
        do_ref[...] = jnp.concatenate(do_parts, axis=1)

        dmixed = da_pool * ps * silu_pool
        dps_ref[...] += jnp.sum(da_pool * mixed * silu_pool, axis=0, keepdims=True)
        dg_pool = da_pool * mixed * ps * dsilu_pool
        du_parts = []
        for gi, win in enumerate(POOL_WINDOWS):
            lanes = slice(gi * POOL_GDIM, (gi + 1) * POOL_GDIM)
            dmx = dmixed[:, lanes]
            dpooled = _dot_nt(dmx, pw_ref[gi])
            dpw_ref[gi] += _dot_tn(pooled_parts[gi], dmx)
            dpt = dpooled * inv_cnt[gi]
            s = jnp.concatenate([dpt, halo_vmem[:, lanes]], axis=0)
            shift = 1
            while shift < win:
                s = s + pltpu.roll(s, ext - shift, 0)
                shift *= 2
            du_parts.append(s[:tr] - dpooled)
            halo_vmem[:, lanes] = dpt[:HALO]
        du = jnp.where(valid, jnp.concatenate(du_parts, axis=1), 0.0)

        for c, val in enumerate((dg_hg, du, dg_pool, dm_hg, dm_pool)):
            dp_ref[c] = val.astype(BF16)
            dbias_ref[c] += jnp.sum(val, axis=0, keepdims=True)
        dz2_ref[...] = dz2
        for c, (lhs, rhs) in enumerate(((merged, dz2), (a_hg, dy_hg), (a_pool, dy_pool))):
            lhs_ref[c] = lhs.astype(BF16)
            rhs_ref[c] = rhs.astype(BF16)

    rev = lambda s: nt - 1 - s
    rowblk = pl.BlockSpec((tr, D_MODEL), lambda s: (rev(s), 0))
    pblk = lambda c: pl.BlockSpec((1, tr, 1024), lambda s, c=c: (c, rev(s), 0))
    halo_blk = pl.BlockSpec((1, HALO, 1024), lambda s: (4, jnp.maximum(rev(s) * (tr // HALO) - 1, 0), 0))
    full = lambda shape: pl.BlockSpec(shape, lambda s: (0,) * len(shape))
    vec = full((1, D_MODEL))
    mat = full((D_MODEL, D_MODEL))
    act3 = jax.ShapeDtypeStruct((3, n_rows, D_MODEL), BF16)
    act3_blk = pl.BlockSpec((3, tr, D_MODEL), lambda s: (0, rev(s), 0))
    return pl.pallas_call(
        body, name="tail_forward_backward",
        out_shape=(jax.ShapeDtypeStruct((n_rows, D_MODEL), F32),
                   jax.ShapeDtypeStruct((5, n_rows, 1024), BF16),
                   jax.ShapeDtypeStruct((n_rows, D_MODEL), F32),
                   act3, act3,
                   jax.ShapeDtypeStruct((5, 1, 1024), F32),
                   jax.ShapeDtypeStruct((1, D_MODEL), F32),
                   jax.ShapeDtypeStruct((n_groups, POOL_GDIM, POOL_GDIM), F32),
                   jax.ShapeDtypeStruct((1, D_MODEL), F32),
                   jax.ShapeDtypeStruct((1, D_MODEL), F32),
                   jax.ShapeDtypeStruct((1, 1), F32)),
        grid=(nt,),
        in_specs=[rowblk, pblk(3), pblk(4), pblk(5), pblk(6), pblk(7), halo_blk, rowblk, ANY,
                  vec, full((n_groups, POOL_GDIM, POOL_GDIM)), vec, mat, mat, mat, vec],
        out_specs=(rowblk, pl.BlockSpec((5, tr, 1024), lambda s: (0, rev(s), 0)), rowblk,
                   act3_blk, act3_blk,
                   full((5, 1, 1024)), vec, full((n_groups, POOL_GDIM, POOL_GDIM)), vec, vec, full((1, 1))),
        scratch_shapes=[pltpu.VMEM((HALO, D_MODEL), F32), pltpu.VMEM((2, tr, D_MODEL), F32), pltpu.SemaphoreType.DMA((2,))],
        compiler_params=_params(("arbitrary",)),
    )(o, p, p, p, p, p, p, zp, tgt, hg_norm_w, pool_w, pool_scale, w_down_hg, w_down_pool, w_out, final_norm_w)


def _in_projection_backward(dp_a, dp_b, w_blocks, zp, dz2, norm_w, chip_sums):
    n_rows = zp.shape[0]
    tr = _row_tile(n_rows, 416)
    nt = n_rows // tr
    na, nb = dp_a.shape[0], dp_b.shape[0]

    def body(dpa_ref, dpb_ref, w_hbm, z_ref, dz2_ref, nw_ref, gs_hbm, gx_hbm, head_ref, dnw_ref, rs_hbm,
             w_vmem, sem, dz_buf, gx_sems, *sems):
        i = pl.program_id(0)
        start_slabs, finish_slabs = _scatter_low(gs_hbm, rs_hbm, *sems)

        def wait_rows_out(tile):
            @pl.when(tile == 0)
            def _():
                pltpu.make_async_copy(dz_buf.at[0, pl.ds(CHUNK, tr - CHUNK), :], gx_hbm.at[pl.ds(0, tr - CHUNK), :],
                                      gx_sems.at[0]).wait()

            @pl.when(tile > 0)
            def _():
                pltpu.make_async_copy(dz_buf.at[tile % 2], gx_hbm.at[pl.ds(tile * tr - CHUNK, tr), :],
                                      gx_sems.at[tile % 2]).wait()

        @pl.when(i == 0)
        def _():
            start_slabs()
            cp = pltpu.make_async_copy(w_hbm, w_vmem, sem)
            cp.start()
            cp.wait()
            dnw_ref[...] = jnp.zeros_like(dnw_ref)

        dh = jnp.zeros((tr, D_MODEL), F32)
        half_cols = w_vmem.shape[-1]
        for j in range(na + nb):
            dp_ref, jj = (dpa_ref, j) if j < na else (dpb_ref, j - na)
            for half in range(2):
                dh = dh + _dot_nt(dp_ref[jj, :, half * half_cols:(half + 1) * half_cols], w_vmem[j, half])
        z = z_ref[...]
        r = lax.rsqrt(jnp.mean(z * z, axis=-1, keepdims=True) + EPS)
        n1 = z * r
        dnw_ref[...] += jnp.sum(dh * n1, axis=0, keepdims=True)
        gh = dh * nw_ref[...]
        dz = dz2_ref[...] + r * (gh - n1 * jnp.mean(gh * n1, axis=-1, keepdims=True))

        @pl.when(i >= 2)
        def _():
            wait_rows_out(i - 2)

        dz_buf[i % 2] = dz

        @pl.when(i == 0)
        def _():
            head_ref[...] = dz[0:CHUNK]
            pltpu.make_async_copy(dz_buf.at[0, pl.ds(CHUNK, tr - CHUNK), :], gx_hbm.at[pl.ds(0, tr - CHUNK), :],
                                  gx_sems.at[0]).start()

        @pl.when(i > 0)
        def _():
            pltpu.make_async_copy(dz_buf.at[i % 2], gx_hbm.at[pl.ds(i * tr - CHUNK, tr), :], gx_sems.at[i % 2]).start()

        @pl.when(i == nt - 1)
        def _():
            if nt >= 2:
                wait_rows_out(i - 1)
            wait_rows_out(i)
            finish_slabs()

    rowblk = pl.BlockSpec((tr, D_MODEL), lambda i: (i, 0))
    vec = pl.BlockSpec((1, D_MODEL), lambda i: (0, 0))
    return pl.pallas_call(
        body, name="in_projection_backward",
        out_shape=(jax.ShapeDtypeStruct((n_rows - CHUNK, D_MODEL), F32), jax.ShapeDtypeStruct((CHUNK, D_MODEL), F32),
                   jax.ShapeDtypeStruct((1, D_MODEL), F32),
                   jax.ShapeDtypeStruct((4,) + chip_sums.shape[1:], chip_sums.dtype)),
        grid=(nt,),
        in_specs=[pl.BlockSpec((na, tr, 1024), lambda i: (0, i, 0)), pl.BlockSpec((nb, tr, 1024), lambda i: (0, i, 0)),
                  ANY, rowblk, rowblk, vec, ANY],
        out_specs=(ANY, pl.BlockSpec((CHUNK, D_MODEL), lambda i: (0, 0)), vec, ANY),
        scratch_shapes=[pltpu.VMEM(w_blocks.shape, w_blocks.dtype), pltpu.SemaphoreType.DMA(()),
                        pltpu.VMEM((2, tr, D_MODEL), F32), pltpu.SemaphoreType.DMA((2,)),
                        pltpu.SemaphoreType.DMA((2,)), pltpu.SemaphoreType.DMA((3,)), pltpu.SemaphoreType.DMA(())],
        compiler_params=_params(("arbitrary",)),
    )(dp_a, dp_b, w_blocks, zp, dz2, norm_w, chip_sums)


def _weight_grad(xs, ys, name):
    shared = xs.ndim == 2
    n_rows, m = xs.shape[-2:]
    nb, _, n = ys.shape
    tk = _row_tile(n_rows, 4160)
    n_k = n_rows // tk

    def body(x_ref, y_ref, o_ref, acc):
        k = pl.program_id(1)

        @pl.when(k == 0)
        def _():
            acc[...] = jnp.zeros_like(acc)

        acc[...] += _dot_tn(x_ref[...] if shared else x_ref[0], y_ref[0])

        @pl.when(k == n_k - 1)
        def _():
            o_ref[0] = acc[...].astype(o_ref.dtype)

    x_spec = pl.BlockSpec((tk, m), lambda j, k: (k, 0)) if shared else pl.BlockSpec((1, tk, m), lambda j, k: (j, k, 0))
    return pl.pallas_call(
        body, name=name,
        out_shape=jax.ShapeDtypeStruct((nb, m, n), BF16),
        grid=(nb, n_k),
        in_specs=[x_spec, pl.BlockSpec((1, tk, n), lambda j, k: (j, k, 0))],
        out_specs=pl.BlockSpec((1, m, n), lambda j, k: (j, 0, 0)),
        scratch_shapes=[pltpu.VMEM((m, n), F32)],
        compiler_params=_params(("arbitrary", "arbitrary")),
    )(xs, ys)


def kernel(x, meta_tokens, norm_w, w_in, b_in, lb_logits, hg_norm_w, pool_w, pool_scale, w_down_hg, w_down_pool, w_out, final_norm_w, loss_target, m_meta_tokens, m_norm_w, m_w_in, m_b_in, m_lb_logits, m_hg_norm_w, m_pool_w, m_pool_scale, m_w_down_hg, m_w_down_pool, m_w_out, m_final_norm_w, v_meta_tokens, v_norm_w, v_w_in, v_b_in, v_lb_logits, v_hg_norm_w, v_pool_w, v_pool_scale, v_w_down_hg, v_w_down_pool, v_w_out, v_final_norm_w):
    seq = x.shape[1]

    zp = _padded_rows(meta_tokens, x[0])
    w_rest = jnp.concatenate([w_down_hg[0].astype(BF16), w_down_pool[0].astype(BF16), w_out[0].astype(BF16),
                              pool_w[0].astype(BF16).reshape(32, 1024)], axis=0)

    p, h, w_blocks = _in_projection(zp, norm_w, w_in[0].astype(BF16), b_in.reshape(N_COLBLK, 1, 1024), _gather_units())
    o, states, scores, rest = _hgrn_forward(p, lb_logits, w_rest)
    wdh = rest[:, REST_W_DOWN_HG:REST_W_DOWN_HG + 128].reshape(1024, 1024)
    wdp = rest[:, REST_W_DOWN_POOL:REST_W_DOWN_POOL + 128].reshape(1024, 1024)
    wout = rest[:, REST_W_OUT:REST_W_OUT + 128].reshape(1024, 1024)
    pw = rest[:, REST_POOL_W:REST_POOL_W + 32].reshape(N_DEV, 4, 32, 256).transpose(1, 0, 2, 3).reshape(4, 256, 256)
    (d_o, dp_b, dz2, grad_lhs, grad_rhs, dbias_b, d_hgw, d_pw, d_ps, d_fnw, loss_part) = _tail(
        p, o, zp, loss_target[0], hg_norm_w, pw, pool_scale, wdh, wdp, wout, final_norm_w.reshape(1, D_MODEL))
    n_a = N_COLBLK - dp_b.shape[0]
    g_hi = _weight_grad(h, dp_b, "weight_grad_in_hi")
    g_rows = _weight_grad(grad_lhs, grad_rhs, "weight_grad_rows")
    dp_a, dbias_a, d_lb, recv_hi, recv_rows = _hgrn_backward(p, lb_logits, states, scores, d_o, g_hi, n_a, g_rows)
    assert n_a == LOW_OWNERS
    chip_lo = _weight_grad_low(h, dp_a)
    dz_seq, dz_head, d_nw, recv_lo = _in_projection_backward(dp_a, dp_b, w_blocks, zp, dz2, norm_w, chip_lo)

    lb = jax.nn.sigmoid(lb_logits[0:1] - lb_logits[1:2])
    d_l0 = d_lb * lb * (1.0 - lb)
    replicated = jnp.concatenate([dbias_a.reshape(3, 1024), dbias_b.reshape(5, 1024), d_nw, d_l0, -d_l0, d_hgw, d_ps, d_fnw,
                                  jnp.pad(loss_part, ((0, MISC_ROWS - MISC_LOSS - 1), (0, 1023)))], axis=0)
    d_meta = dz_head[PAD_ROWS:CHUNK].reshape(N_META, N_DEV, 128).transpose(1, 0, 2)
    d_pw_blocks = d_pw.reshape(4, N_DEV, 32, 256).transpose(1, 0, 2, 3).reshape(N_DEV, 32, 1024)
    g_misc = jnp.concatenate([d_pw_blocks, jnp.pad(d_meta, ((0, 0), (0, 0), (0, 1024 - 128))),
                              jnp.broadcast_to(replicated[None], (N_DEV, 16, 1024))], axis=1)

    as_rows = lambda t, n: t.reshape(n, 1024)
    small = [(MISC_POOL_W, 1024, tuple(as_rows(t, 32) for t in (pool_w, m_pool_w, v_pool_w))),
             (MISC_META, 128, (meta_tokens, m_meta_tokens, v_meta_tokens)),
             (MISC_B_IN, 1024, tuple(as_rows(t, 8) for t in (b_in, m_b_in, v_b_in))),
             (MISC_NORM_W, 1024, (norm_w, m_norm_w, v_norm_w)),
             (MISC_LB, 1024, (lb_logits, m_lb_logits, v_lb_logits)),
             (MISC_HG_NORM_W, 1024, (hg_norm_w, m_hg_norm_w, v_hg_norm_w)),
             (MISC_POOL_SCALE, 1024, (pool_scale, m_pool_scale, v_pool_scale)),
             (MISC_FINAL_NORM_W, 1024, tuple(as_rows(t, 1) for t in (final_norm_w, m_final_norm_w, v_final_norm_w)))]
    res, loss = _finish(recv_hi, recv_lo, n_a, recv_rows, _exchange_small(g_misc), (w_in, m_w_in, v_w_in),
                  [(w_out, m_w_out, v_w_out), (w_down_hg, m_w_down_hg, v_w_down_hg), (w_down_pool, m_w_down_pool, v_w_down_pool)],
                  small)
    r_w_in, r_w_out, r_wdh, r_wdp, r_pw, r_meta, r_b_in, r_nw, r_lb, r_hgw, r_ps, r_fnw = res
    grad_x = dz_seq.reshape(1, seq, D_MODEL)
    per_kind = [(r_meta[k], r_nw[k], r_w_in[k], r_b_in[k].reshape(1, 8192), r_lb[k], r_hgw[k], r_pw[k].reshape(1, 4, 32, 256),
                 r_ps[k], r_wdh[k], r_wdp[k], r_w_out[k], r_fnw[k].reshape(1024)) for k in range(4)]
    return (loss, grad_x, *per_kind[0], *per_kind[1], *per_kind[2], *per_kind[3])
```

```python
import functools

import jax
import jax.numpy as jnp
from jax import lax
from jax.experimental import pallas as pl
from jax.experimental.pallas import tpu as pltpu

F32 = jnp.float32
BF16 = jnp.bfloat16

D_MODEL = 1024
N_META = 16
HEADS = 8
HEAD_DIM = 128
CHUNK = 64
SUB = 8
N_SUB = CHUNK // SUB
PAD_ROWS = CHUNK - N_META
POOL_WINDOWS = (2, 4, 8, 16)
POOL_GDIM = D_MODEL // len(POOL_WINDOWS)
HALO = 16
EPS = 1e-6
N_DEV = 8
N_COLBLK = 8
ADAM_LR, ADAM_B1, ADAM_B2, ADAM_EPS, ADAM_WD, ADAM_STEP = 0.001, 0.9, 0.999, 1e-08, 0.01, 10

VMEM_LIMIT = 56 * 1024 * 1024
MESH = pl.DeviceIdType.MESH
ANY = pl.BlockSpec(memory_space=pl.ANY)

REST_W_DOWN_HG = 0
REST_W_DOWN_POOL = 128
REST_W_OUT = 256
REST_POOL_W = 384
MISC_POOL_W = 0
MISC_META = 32
MISC_B_IN = 48
MISC_NORM_W = 56
MISC_LB = 57
MISC_HG_NORM_W = 59
MISC_POOL_SCALE = 60
MISC_FINAL_NORM_W = 61
MISC_LOSS = 62
MISC_ROWS = 64


def _params(sem=None):
    return pltpu.CompilerParams(dimension_semantics=sem, vmem_limit_bytes=VMEM_LIMIT)


def _row_tile(n_rows, prefer):
    best = 16
    for t in range(16, prefer + 1, 16):
        if n_rows % t == 0:
            best = t
    return best


def _sigmoid_pair(x):
    e = jnp.exp(-jnp.abs(x))
    r = 1.0 / (1.0 + e)
    er = e * r
    pos = x >= 0
    return jnp.where(pos, r, er), jnp.where(pos, er, r)


def _dot(a, b):
    return jnp.dot(a.astype(BF16), b.astype(BF16), preferred_element_type=F32)


def _dot_nt(a, b):
    return lax.dot_general(a.astype(BF16), b.astype(BF16), (((1,), (1,)), ((), ())), preferred_element_type=F32)


def _dot_tn(a, b):
    return lax.dot_general(a.astype(BF16), b.astype(BF16), (((0,), (0,)), ((), ())), preferred_element_type=F32)


def _device_index(px, py, pc):
    return 4 * px + 2 * py + pc


def _direct_gather(src_ref, dst_ref, send_sems, recv_sems, local_sem):
    x, y, c = lax.axis_index("x"), lax.axis_index("y"), lax.axis_index("c")
    own = pltpu.make_async_copy(src_ref, dst_ref.at[_device_index(x, y, c)], local_sem)
    sends, arrivals = [], []
    for k in range(1, N_DEV):
        peer = (1 - x if k & 4 else x, 1 - y if k & 2 else y, 1 - c if k & 1 else c)
        for slot, out in ((_device_index(x, y, c), sends), (_device_index(*peer), arrivals)):
            out.append(pltpu.make_async_remote_copy(
                src_ref=src_ref, dst_ref=dst_ref.at[slot], send_sem=send_sems.at[k - 1], recv_sem=recv_sems.at[k - 1],
                device_id=peer, device_id_type=MESH))
    return own, sends, arrivals


GATHER_SEMS = [pltpu.SemaphoreType.DMA((N_DEV - 1,)), pltpu.SemaphoreType.DMA((N_DEV - 1,)), pltpu.SemaphoreType.DMA(())]


def _all_gather_small(block):
    def body(x_ref, out_ref, send_sems, recv_sems, local_sem):
        own, sends, arrivals = _direct_gather(x_ref, out_ref, send_sems, recv_sems, local_sem)
        own.start()
        for cp in sends:
            cp.start()
        for cp in arrivals:
            cp.wait_recv()
        for cp in sends:
            cp.wait_send()
        own.wait()

    return pl.pallas_call(
        body, name="all_gather_meta",
        out_shape=jax.ShapeDtypeStruct((N_DEV,) + block.shape, block.dtype),
        in_specs=[ANY], out_specs=ANY, scratch_shapes=list(GATHER_SEMS),
    )(block)


def _peer(k):
    x, y, c = lax.axis_index("x"), lax.axis_index("y"), lax.axis_index("c")
    return (1 - x if k & 4 else x, 1 - y if k & 2 else y, 1 - c if k & 1 else c)


def _me():
    return _device_index(lax.axis_index("x"), lax.axis_index("y"), lax.axis_index("c"))


def _remote(src, dst, send_sem, recv_sem, peer_bits):
    return pltpu.make_async_remote_copy(src_ref=src, dst_ref=dst, send_sem=send_sem, recv_sem=recv_sem,
                                        device_id=_peer(peer_bits), device_id_type=MESH)


N_ROW_GRADS = 3
SCATTER_SEMS = [pltpu.SemaphoreType.DMA((N_DEV - 1,)), pltpu.SemaphoreType.DMA((N_DEV - 1,)), pltpu.SemaphoreType.DMA(())]
SCATTER_ROWS_SEMS = [pltpu.SemaphoreType.DMA((7 * N_ROW_GRADS,)), pltpu.SemaphoreType.DMA((7 * N_ROW_GRADS,)),
                     pltpu.SemaphoreType.DMA((N_ROW_GRADS,))]


def _scatter_slabs(g_ref, first, recv_ref, send_sems, recv_sems, local_sem):
    n = g_ref.shape[0]
    me = _me()

    def each(on_send, on_local, on_arrival):
        for kk in range(1, N_DEV):
            peer = jnp.bitwise_xor(me, kk)

            @pl.when((peer >= first) & (peer < first + n))
            def _(kk=kk, peer=peer):
                on_send(_remote(g_ref.at[peer - first], recv_ref.at[me], send_sems.at[kk - 1], recv_sems.at[kk - 1], kk))

        @pl.when((me >= first) & (me < first + n))
        def _():
            on_local(pltpu.make_async_copy(g_ref.at[me - first], recv_ref.at[me], local_sem))
            if on_arrival is not None:
                for kk in range(1, N_DEV):
                    on_arrival(_remote(g_ref.at[0], recv_ref.at[jnp.bitwise_xor(me, kk)], send_sems.at[kk - 1],
                                       recv_sems.at[kk - 1], kk))

    start = lambda: each(lambda cp: cp.start(), lambda cp: cp.start(), None)
    finish = lambda: each(lambda cp: cp.wait_send(), lambda cp: cp.wait(), lambda cp: cp.wait_recv())
    return start, finish


def _scatter_rows(g_ref, recv_ref, send_sems, recv_sems, local_sems):
    me = _me()
    rows = lambda m, dev: g_ref.at[m, pl.ds(dev * 128, 128), :]

    def copies():
        local = [pltpu.make_async_copy(rows(m, me), recv_ref.at[me, m], local_sems.at[m]) for m in range(N_ROW_GRADS)]
        sends, arrivals = [], []
        for m in range(N_ROW_GRADS):
            for kk in range(1, N_DEV):
                peer, sems = jnp.bitwise_xor(me, kk), (send_sems.at[7 * m + kk - 1], recv_sems.at[7 * m + kk - 1])
                sends.append(_remote(rows(m, peer), recv_ref.at[me, m], *sems, kk))
                arrivals.append(_remote(rows(m, me), recv_ref.at[peer, m], *sems, kk))
        return local, sends, arrivals

    def start():
        local, sends, _ = copies()
        for cp in local + sends:
            cp.start()

    def finish():
        local, sends, arrivals = copies()
        for cp in arrivals:
            cp.wait_recv()
        for cp in sends:
            cp.wait_send()
        for cp in local:
            cp.wait()

    return start, finish


LOW_OWNERS = 3


def _weight_grad_low(h, dp_a):
    n_rows = h.shape[0]
    tk = _row_tile(n_rows, 2080)
    n_k = n_rows // tk
    order = jnp.where(lax.axis_index("c") == 0, jnp.array([1, 0, 2], jnp.int32), jnp.array([0, 2, 1], jnp.int32))

    def body(order_ref, h_ref, y_ref, out_ref, acc, send_buf, got, send_sems, recv_sems):
        t, k = pl.program_id(0), pl.program_id(1)
        c = lax.axis_index("c")

        @pl.when(k == 0)
        def _():
            acc[...] = jnp.zeros_like(acc)

        acc[...] += _dot_tn(h_ref[...], y_ref[0])

        def to_sibling(slot):
            return _remote(send_buf.at[slot], got.at[slot], send_sems.at[slot], recv_sems.at[slot], 1)

        def send(slot):
            send_buf[slot] = acc[...].astype(send_buf.dtype)
            to_sibling(slot).start()

        def keep(slot):
            to_sibling(slot).wait_recv()
            out_ref[slot] = (acc[...] + got[slot].astype(F32)).astype(out_ref.dtype)

        done = k == n_k - 1
        for core, step, action, slot in ((0, 0, send, 0), (0, 1, keep, 0), (0, 2, keep, 1),
                                         (1, 0, send, 0), (1, 1, send, 1), (1, 2, keep, 0)):
            @pl.when(done & (c == core) & (t == step))
            def _(action=action, slot=slot):
                action(slot)

        @pl.when(done & (t == LOW_OWNERS - 1))
        def _():
            to_sibling(0).wait_send()

            @pl.when(c == 1)
            def _():
                to_sibling(1).wait_send()
                out_ref[1] = jnp.zeros(out_ref.shape[1:], out_ref.dtype)

    pair = (2, h.shape[1], dp_a.shape[2])
    return pl.pallas_call(
        body, name="weight_grad_in_low",
        out_shape=jax.ShapeDtypeStruct(pair, BF16),
        grid_spec=pltpu.PrefetchScalarGridSpec(
            num_scalar_prefetch=1, grid=(LOW_OWNERS, n_k),
            in_specs=[pl.BlockSpec((tk, h.shape[1]), lambda t, k, o: (k, 0)),
                      pl.BlockSpec((1, tk, dp_a.shape[2]), lambda t, k, o: (o[t], k, 0))],
            out_specs=pl.BlockSpec(pair, lambda t, k, o: (0, 0, 0)),
            scratch_shapes=[pltpu.VMEM(pair[1:], F32), pltpu.VMEM(pair, BF16), pltpu.VMEM(pair, BF16),
                            pltpu.SemaphoreType.DMA((2,)), pltpu.SemaphoreType.DMA((2,))]),
        compiler_params=_params(("arbitrary", "arbitrary")),
    )(order, h, dp_a)


def _scatter_low(part_ref, recv_ref, send_sems, recv_sems, local_sem):
    x, y, c = lax.axis_index("x"), lax.axis_index("y"), lax.axis_index("c")
    chip = 2 * x + y
    routes = ((0, (0, 0, c), 0, None), (1, (0, 1, 0), 1, 0))

    def each(on_send, on_local, on_arrival):
        for slot, owner, owner_chip, core in routes:
            holds = (c == core) if core is not None else (c >= 0)
            rel = jnp.bitwise_xor(chip, owner_chip)

            @pl.when(holds & (rel != 0))
            def _(slot=slot, owner=owner, rel=rel):
                on_send(pltpu.make_async_remote_copy(
                    src_ref=part_ref.at[slot], dst_ref=recv_ref.at[chip], send_sem=send_sems.at[slot],
                    recv_sem=recv_sems.at[rel - 1], device_id=owner, device_id_type=MESH))

            @pl.when(holds & (rel == 0))
            def _(slot=slot, owner=owner, owner_chip=owner_chip):
                on_local(pltpu.make_async_copy(part_ref.at[slot], recv_ref.at[chip], local_sem))
                if on_arrival is not None:
                    for r in range(1, 4):
                        on_arrival(pltpu.make_async_remote_copy(
                            src_ref=part_ref.at[slot], dst_ref=recv_ref.at[r ^ owner_chip], send_sem=send_sems.at[slot],
                            recv_sem=recv_sems.at[r - 1], device_id=owner, device_id_type=MESH))

    start = lambda: each(lambda cp: cp.start(), lambda cp: cp.start(), None)
    finish = lambda: each(lambda cp: cp.wait_send(), lambda cp: cp.wait(), lambda cp: cp.wait_recv())
    return start, finish


def _adam_update(g, w, m, v):
    mn = ADAM_B1 * m + (1.0 - ADAM_B1) * g
    vn = ADAM_B2 * v + (1.0 - ADAM_B2) * (g * g)
    m_hat = mn / (1.0 - ADAM_B1 ** ADAM_STEP)
    v_hat = vn / (1.0 - ADAM_B2 ** ADAM_STEP)
    return -ADAM_LR * (m_hat / (jnp.sqrt(v_hat) + ADAM_EPS) + ADAM_WD * w), mn, vn


def _device_sum(parts):
    t = [p.astype(F32) for p in parts]
    return ((t[0] + t[1]) + (t[2] + t[3])) + ((t[4] + t[5]) + (t[6] + t[7]))


def _finish(a_hi, a_lo, n_lo, b3, sources, big, rows3, small):
    n_steps = 4
    tb, tr3 = 1024 // n_steps, 128 // n_steps

    def body(*refs):
        it = iter(refs)
        hi_ref, lo_ref, b_ref = next(it), next(it), next(it)
        src_refs = [next(it) for _ in sources]
        big_in = [next(it) for _ in range(3)]
        rows_in = [[next(it) for _ in range(3)] for _ in rows3]
        small_in = [[next(it) for _ in range(3)] for _ in small]
        big_out = [next(it) for _ in range(4)]
        rows_out = [[next(it) for _ in range(4)] for _ in rows3]
        small_out = [[next(it) for _ in range(4)] for _ in small]
        loss_ref = next(it)

        def apply(g, ins, outs):
            d, mn, vn = _adam_update(g, ins[0][...], ins[1][...], ins[2][...])
            for r, val in zip(outs, (g, d, mn, vn)):
                r[...] = val

        lo = [lo_ref[s].astype(F32) for s in range(4)]
        g_big = jnp.where(_me() < n_lo, (lo[0] + lo[1]) + (lo[2] + lo[3]), _device_sum([hi_ref[s] for s in range(N_DEV)]))
        apply(g_big[None], big_in, big_out)
        for k in range(len(rows3)):
            apply(_device_sum([b_ref[s, k] for s in range(N_DEV)])[None], rows_in[k], rows_out[k])

        @pl.when(pl.program_id(0) == 0)
        def _():
            loss_ref[...] = _device_sum([src_refs[0][s, MISC_LOSS:MISC_LOSS + 1, :] for s in range(N_DEV)])
            for (src, row0, lanes, ins), r_in, r_out in zip(small, small_in, small_out):
                n = ins[0].shape[0]
                apply(_device_sum([src_refs[src][s, row0:row0 + n, :lanes] for s in range(N_DEV)]), r_in, r_out)

    whole = lambda shape: pl.BlockSpec(shape, lambda i: (0,) * len(shape))
    big_blk = pl.BlockSpec((1, tb, 1024), lambda i: (0, i, 0))
    rows_blk = pl.BlockSpec((1, tr3, 1024), lambda i: (0, i, 0))
    in_specs = [pl.BlockSpec((N_DEV, tb, 1024), lambda i: (0, i, 0)), pl.BlockSpec((4, tb, 1024), lambda i: (0, i, 0)),
                pl.BlockSpec((N_DEV, 3, tr3, 1024), lambda i: (0, 0, i, 0))] + [whole(t.shape) for t in sources]
    in_specs += [big_blk] * 3 + [rows_blk] * (3 * len(rows3))
    out_specs = [big_blk] * 4 + [rows_blk] * (4 * len(rows3))
    out_shape = [jax.ShapeDtypeStruct(big[0].shape, F32)] * 4
    for w, _, _ in rows3:
        out_shape += [jax.ShapeDtypeStruct(w.shape, F32)] * 4
    args = [a_hi, a_lo, b3, *sources, *big]
    for t in rows3:
        args += list(t)
    for _, _, _, t in small:
        in_specs += [whole(t[0].shape)] * 3
        out_specs += [whole(t[0].shape)] * 4
        out_shape += [jax.ShapeDtypeStruct(t[0].shape, F32)] * 4
        args += list(t)
    out_specs.append(whole((1, 1024)))
    out_shape.append(jax.ShapeDtypeStruct((1, 1024), F32))
    outs = pl.pallas_call(
        body, name="reduce_sum_adamw", out_shape=tuple(out_shape), grid=(n_steps,),
        in_specs=in_specs, out_specs=tuple(out_specs),
        compiler_params=_params(("arbitrary",)),
    )(*args)
    return [tuple(outs[4 * k:4 * k + 4]) for k in range(len(outs) // 4)], outs[-1][0, 0]


GATHER_UNITS = ((0, 0), (0, 1), (1, 0), (1, 1)) + tuple((place, 0) for place in range(2, 8)) + tuple(
    (place, 1) for place in range(2, 8))


def _gather_order():
    x, y, c = lax.axis_index("x"), lax.axis_index("y"), lax.axis_index("c")
    chips = [(1 - x, y), (x, 1 - y), (1 - x, 1 - y)]
    order = [_device_index(x, y, c), _device_index(x, y, 1 - c)]
    order += [_device_index(*q, c) for q in chips] + [_device_index(*q, 1 - c) for q in chips]
    return order


def _gather_units():
    order = _gather_order()
    blocks = jnp.stack([order[place] for place, _ in GATHER_UNITS]).astype(jnp.int32)
    return blocks, jnp.array([half for _, half in GATHER_UNITS], jnp.int32)


def _in_projection(zp, norm_w, w_shard, b_blocks, units):
    n_rows = zp.shape[0]
    tr = _row_tile(n_rows, 832)
    nt = n_rows // tr

    half_cols = 1024 // 2
    n_units = len(GATHER_UNITS)

    def body(blocks_ref, halves_ref, z_ref, nw_ref, w_hbm, b_ref, p_ref, h_ref, w_out, w_vmem, h_all,
             send_sems, recv_sems, local_sem, out_sems):
        s, i = pl.program_id(0), pl.program_id(1)
        x, y, c = lax.axis_index("x"), lax.axis_index("y"), lax.axis_index("c")
        me, sibling = (x, y, c), (x, y, 1 - c)
        chips = [(1 - x, y), (x, 1 - y), (1 - x, 1 - y)]

        def slot(blk, half):
            return w_vmem.at[_device_index(*blk), half]

        def mine(half):
            return w_hbm.at[:, pl.ds(half * half_cols, half_cols)]

        def copy(k, half, blk, to, own_block=False):
            return pltpu.make_async_remote_copy(
                src_ref=mine(half) if own_block else slot(blk, half), dst_ref=slot(blk, half),
                send_sem=send_sems.at[2 * k + half], recv_sem=recv_sems.at[2 * k + half], device_id=to, device_id_type=MESH)

        own = [pltpu.make_async_copy(mine(half), slot(me, half), local_sem.at[half]) for half in range(2)]
        first = [copy(k, half, me, to, own_block=True) for half in range(2)
                 for k, to in enumerate([sibling] + [(*q, c) for q in chips])]
        passed = [[copy(4 + j, half, (*q, c), sibling) for j, q in enumerate(chips)] for half in range(2)]
        sources = [None, sibling] + [(*q, c) for q in chips] + [(*q, 1 - c) for q in chips]
        arrival = lambda place, half: copy(place - 1, half, sources[place], me)

        def keep(unit):
            blk, half = blocks_ref[unit], halves_ref[unit]
            return pltpu.make_async_copy(w_vmem.at[blk, half], w_out.at[blk, half], out_sems.at[unit])

        for unit, (place, half) in enumerate(GATHER_UNITS):
            @pl.when((i == 0) & (s == unit))
            def _(unit=unit, place=place, half=half):
                if unit == 0:
                    for cp in own + first:
                        cp.start()
                if place == 0:
                    own[half].wait()
                else:
                    arrival(place, half).wait_recv()
                    if 2 <= place <= 4:
                        passed[half][place - 2].start()
                keep(unit).start()

        @pl.when(s == 0)
        def _():
            z = z_ref[...]
            r = lax.rsqrt(jnp.mean(z * z, axis=-1, keepdims=True) + EPS)
            h = (z * r * nw_ref[...]).astype(BF16)
            h_all[i] = h
            h_ref[...] = h

        p_ref[0] = jnp.dot(h_all[i], w_vmem[blocks_ref[s], halves_ref[s]], preferred_element_type=F32) + b_ref[0]

        @pl.when((s == n_units - 1) & (i == nt - 1))
        def _():
            for cp in first + passed[0] + passed[1]:
                cp.wait_send()
            for unit in range(n_units):
                keep(unit).wait()

    first_pass = lambda s, i, rest: jnp.where(s == 0, i, rest)
    return pl.pallas_call(
        body, name="in_projection_gather",
        out_shape=(jax.ShapeDtypeStruct((N_COLBLK, n_rows, 1024), F32),
                   jax.ShapeDtypeStruct((n_rows, D_MODEL), BF16),
                   jax.ShapeDtypeStruct((N_DEV, 2, D_MODEL, half_cols), BF16)),
        grid_spec=pltpu.PrefetchScalarGridSpec(
            num_scalar_prefetch=2, grid=(n_units, nt),
            in_specs=[pl.BlockSpec((tr, D_MODEL), lambda s, i, blk, hf: (first_pass(s, i, 0), 0)),
                      pl.BlockSpec((1, D_MODEL), lambda s, i, blk, hf: (0, 0)), ANY,
                      pl.BlockSpec((1, 1, half_cols), lambda s, i, blk, hf: (blk[s], 0, hf[s]))],
            out_specs=(pl.BlockSpec((1, tr, half_cols), lambda s, i, blk, hf: (blk[s], i, hf[s])),
                       pl.BlockSpec((tr, D_MODEL), lambda s, i, blk, hf: (first_pass(s, i, nt - 1), 0)), ANY),
            scratch_shapes=[pltpu.VMEM((N_DEV, 2, D_MODEL, half_cols), BF16), pltpu.VMEM((nt, tr, D_MODEL), BF16),
                            pltpu.SemaphoreType.DMA((14,)), pltpu.SemaphoreType.DMA((14,)), pltpu.SemaphoreType.DMA((2,)),
                            pltpu.SemaphoreType.DMA((n_units,))]),
        compiler_params=_params(("arbitrary", "arbitrary")),
    )(*units, zp, norm_w, w_shard, b_blocks)


def _lower_bound(lb_ref):
    l0, l1 = lb_ref[0:1, :], lb_ref[1:2, :]
    _, lb = _sigmoid_pair(l1 - l0)
    return lb


def _chunk_gates(fz, lb, valid):
    sig, nsig = _sigmoid_pair(fz)
    f = lb + (1.0 - lb) * sig
    g2 = jnp.where(valid, jnp.log2(f), 0.0)
    k = jnp.where(valid, (1.0 - lb) * nsig, 0.0)
    return sig, nsig, f, g2, k


def _tri(n, upper=False):
    r = lax.broadcasted_iota(jnp.int32, (n, n), 0)
    c = lax.broadcasted_iota(jnp.int32, (n, n), 1)
    return jnp.where((r <= c) if upper else (r >= c), 1.0, 0.0).astype(BF16)


def _tri_dot(tri, x):
    hi = x.astype(BF16)
    rest = x - hi.astype(F32)
    mid = rest.astype(BF16)
    low = (rest - mid.astype(F32)).astype(BF16)
    return (jnp.dot(tri, hi, preferred_element_type=F32) + jnp.dot(tri, mid, preferred_element_type=F32)
            + jnp.dot(tri, low, preferred_element_type=F32))


def _intra_scores(q_ref, k_ref, b2_ref, col0):
    cols = pl.ds(col0, HEAD_DIM)
    rows_s = lax.broadcasted_iota(jnp.int32, (SUB, 1), 0)
    lanes_c = lax.broadcasted_iota(jnp.int32, (1, CHUNK), 1)
    blocks = []
    for i in range(N_SUB):
        lo = i * SUB
        qi = q_ref[lo:lo + SUB, cols]
        bi = b2_ref[lo:lo + SUB, cols]
        if i == 0:
            acc = jnp.zeros((SUB, CHUNK), F32)
        else:
            ref_i = b2_ref[lo:lo + 1, cols]
            qt = qi * jnp.exp2(bi - ref_i)
            kt = jnp.concatenate([k_ref[0:lo, cols] * jnp.exp2(ref_i - b2_ref[0:lo, cols]),
                                  jnp.zeros((CHUNK - lo, HEAD_DIM), F32)], axis=0)
            acc = _dot_nt(qt, kt)
        for s in range(SUB):
            b_s = b2_ref[lo + s:lo + s + 1, cols]
            k_s = k_ref[lo + s:lo + s + 1, cols]
            w = jnp.exp2(jnp.minimum(bi - b_s, 0.0))
            col = jnp.sum((qi * w) * k_s, axis=-1, keepdims=True)
            acc = jnp.where(lanes_c == lo + s, col, acc)
        blocks.append(jnp.where(lanes_c <= lo + rows_s, acc, 0.0))
    return jnp.concatenate(blocks, axis=0)


def _hgrn_forward(p, lb_logits, w_rest):
    n_rows = p.shape[1]
    n_chunks = n_rows // CHUNK
    width = HEADS * HEAD_DIM

    def body(q_ref, fz_ref, v_ref, lb_ref, rest_ref, o_ref, st_out_ref, a_out_ref, rest_out,
             state, k_vmem, b2_vmem, send_sems, recv_sems, local_sem):
        n = pl.program_id(0)
        own, sends, arrivals = _direct_gather(rest_ref, rest_out, send_sems, recv_sems, local_sem)

        @pl.when(n == 0)
        def _():
            state[...] = jnp.zeros_like(state)
            own.start()
            for cp in sends:
                cp.start()

        rows = n * CHUNK + lax.broadcasted_iota(jnp.int32, (CHUNK, 1), 0)
        valid = rows >= PAD_ROWS
        lb = _lower_bound(lb_ref)
        _, _, _, g2, k = _chunk_gates(fz_ref[0], lb, valid)
        k_vmem[...] = k
        b2_vmem[...] = _tri_dot(_tri(CHUNK), g2)
        q_view = q_ref.at[0]
        for h in range(HEADS):
            cols = pl.ds(h * HEAD_DIM, HEAD_DIM)
            st = state[h]
            st_out_ref[0, h] = st
            bh = b2_vmem[:, cols]
            kh = k_vmem[:, cols]
            vh = jnp.where(valid, v_ref[0, :, cols], 0.0)
            qe = q_ref[0, :, cols] * jnp.exp2(bh)
            a = _intra_scores(q_view, k_vmem, b2_vmem, h * HEAD_DIM).astype(BF16)
            a_out_ref[0, h] = a
            o_ref[:, cols] = _dot_nt(qe, st) + _dot(a, vh)
            b_last = b2_vmem[CHUNK - 1:CHUNK, cols]
            kd = kh * jnp.exp2(b_last - bh)
            state[h] = st * jnp.exp2(b_last) + _dot_tn(vh, kd)

        @pl.when(n == n_chunks - 1)
        def _():
            for cp in arrivals:
                cp.wait_recv()
            for cp in sends:
                cp.wait_send()
            own.wait()

    blk = lambda c: pl.BlockSpec((1, CHUNK, width), lambda n, c=c: (c, n, 0))
    return pl.pallas_call(
        body, name="hgrn_forward",
        out_shape=(jax.ShapeDtypeStruct((n_rows, width), F32),
                   jax.ShapeDtypeStruct((n_chunks, HEADS, HEAD_DIM, HEAD_DIM), F32),
                   jax.ShapeDtypeStruct((n_chunks, HEADS, CHUNK, CHUNK), BF16),
                   jax.ShapeDtypeStruct((N_DEV,) + w_rest.shape, w_rest.dtype)),
        grid=(n_chunks,),
        in_specs=[blk(0), blk(1), blk(2), pl.BlockSpec((2, width), lambda n: (0, 0)), ANY],
        out_specs=(pl.BlockSpec((CHUNK, width), lambda n: (n, 0)),
                   pl.BlockSpec((1, HEADS, HEAD_DIM, HEAD_DIM), lambda n: (n, 0, 0, 0)),
                   pl.BlockSpec((1, HEADS, CHUNK, CHUNK), lambda n: (n, 0, 0, 0)), ANY),
        scratch_shapes=[pltpu.VMEM((HEADS, HEAD_DIM, HEAD_DIM), F32), pltpu.VMEM((CHUNK, width), F32),
                        pltpu.VMEM((CHUNK, width), F32)] + list(GATHER_SEMS),
        compiler_params=_params(("arbitrary",)),
    )(p, p, p, lb_logits, w_rest)


def _hgrn_backward(p, lb_logits, states, scores, d_o, g_slabs, first_owner, g_rows):
    n_rows = p.shape[1]
    n_chunks = n_rows // CHUNK
    width = HEADS * HEAD_DIM

    def body(q_ref, fz_ref, v_ref, lb_ref, st_ref, a_ref, do_ref, gs_hbm, gr_hbm, dp_ref, dbias_ref, dlb_ref, rs_hbm, rr_hbm,
             dstate, k_vmem, b2_vmem, *sems):
        step = pl.program_id(0)
        n = n_chunks - 1 - step
        start_slabs, finish_slabs = _scatter_slabs(gs_hbm, first_owner, rs_hbm, *sems[:3])
        start_rows, finish_rows = _scatter_rows(gr_hbm, rr_hbm, *sems[3:])

        @pl.when(step == 0)
        def _():
            dstate[...] = jnp.zeros_like(dstate)
            dbias_ref[...] = jnp.zeros_like(dbias_ref)
            dlb_ref[...] = jnp.zeros_like(dlb_ref)
            start_slabs()
            start_rows()

        rows = n * CHUNK + lax.broadcasted_iota(jnp.int32, (CHUNK, 1), 0)
        valid = rows >= PAD_ROWS
        lb = _lower_bound(lb_ref)
        sig, nsig, f, g2, k = _chunk_gates(fz_ref[0], lb, valid)
        k_vmem[...] = k
        b2_vmem[...] = _tri_dot(_tri(CHUNK), g2)
        rows_c = lax.broadcasted_iota(jnp.int32, (CHUNK, 1), 0)
        rows_s = lax.broadcasted_iota(jnp.int32, (SUB, 1), 0)
        lanes_c = lax.broadcasted_iota(jnp.int32, (1, CHUNK), 1)
        causal = lax.broadcasted_iota(jnp.int32, (CHUNK, CHUNK), 0) >= lax.broadcasted_iota(jnp.int32, (CHUNK, CHUNK), 1)
        tri_up = _tri(CHUNK, upper=True)
        for h in range(HEADS):
            cols = pl.ds(h * HEAD_DIM, HEAD_DIM)
            st = st_ref[0, h]
            dst = dstate[h]
            qh = q_ref[0, :, cols]
            bh = b2_vmem[:, cols]
            kh = k_vmem[:, cols]
            vh = jnp.where(valid, v_ref[0, :, cols], 0.0)
            doh = do_ref[:, cols]
            eb = jnp.exp2(bh)
            qe = qh * eb
            b_last = b2_vmem[CHUNK - 1:CHUNK, cols]
            e_last = jnp.exp2(b_last)
            decay_k = jnp.exp2(b_last - bh)
            kd = kh * decay_k
            dqe = _dot(doh, st)
            da = jnp.where(causal, _dot_nt(doh, vh), 0.0)
            dv = _dot_tn(a_ref[0, h], doh) + _dot_nt(kd, dst)
            dkd = _dot(vh, dst)
            dstate[h] = dst * e_last + _dot_tn(doh, qe)
            db_last = (jnp.sum(dst * st, axis=0, keepdims=True) * e_last
                       + jnp.sum(dkd * kd, axis=0, keepdims=True))
            dq_blocks, dk_blocks = [], []
            dk_earlier = jnp.zeros((CHUNK, HEAD_DIM), F32)
            for i in range(N_SUB):
                lo = i * SUB
                qi = q_ref[0, lo:lo + SUB, cols]
                bi = b2_vmem[lo:lo + SUB, cols]
                da_i = da[lo:lo + SUB, :]
                if i == 0:
                    dq_i = jnp.zeros((SUB, HEAD_DIM), F32)
                else:
                    ref_i = b2_vmem[lo:lo + 1, cols]
                    eq = jnp.exp2(bi - ref_i)
                    ek = jnp.exp2(ref_i - b2_vmem[0:lo, cols])
                    later = jnp.zeros((CHUNK - lo, HEAD_DIM), F32)
                    kt = jnp.concatenate([k_vmem[0:lo, cols] * ek, later], axis=0)
                    dq_i = _dot(da_i, kt) * eq
                    dk_earlier = dk_earlier + jnp.concatenate([_dot_tn(da_i, qi * eq)[0:lo] * ek, later], axis=0)
                dk_i = jnp.zeros((SUB, HEAD_DIM), F32)
                for s in range(SUB):
                    b_s = b2_vmem[lo + s:lo + s + 1, cols]
                    k_s = k_vmem[lo + s:lo + s + 1, cols]
                    w = jnp.exp2(jnp.minimum(bi - b_s, 0.0))
                    da_col = jnp.sum(jnp.where(lanes_c == lo + s, da_i, 0.0), axis=-1, keepdims=True)
                    gw = da_col * w
                    dq_i = dq_i + gw * k_s
                    dk_i = jnp.where(rows_s == s, jnp.sum(gw * qi, axis=0, keepdims=True), dk_i)
                dq_blocks.append(dq_i)
                dk_blocks.append(dk_i)
            dq_intra = jnp.concatenate(dq_blocks, axis=0)
            dk_intra = jnp.concatenate(dk_blocks, axis=0) + dk_earlier
            dq = dqe * eb + dq_intra
            dk = dkd * decay_k + dk_intra
            db = dqe * qe - dkd * kd + qh * dq_intra - kh * dk_intra
            db = db + jnp.where(rows_c == CHUNK - 1, db_last, 0.0)
            dg = _tri_dot(tri_up, db)
            fh = f[:, h * HEAD_DIM:(h + 1) * HEAD_DIM]
            sh = sig[:, h * HEAD_DIM:(h + 1) * HEAD_DIM]
            nh = nsig[:, h * HEAD_DIM:(h + 1) * HEAD_DIM]
            lbh = lb[:, h * HEAD_DIM:(h + 1) * HEAD_DIM]
            df = jnp.where(valid, dg / fh - dk, 0.0)
            dfz = df * (1.0 - lbh) * sh * nh
            dq = jnp.where(valid, dq, 0.0)
            dv = jnp.where(valid, dv, 0.0)
            dlb_ref[:, cols] += jnp.sum(df * nh, axis=0, keepdims=True)
            dp_ref[0, :, cols] = dq.astype(BF16)
            dp_ref[1, :, cols] = dfz.astype(BF16)
            dp_ref[2, :, cols] = dv.astype(BF16)
            dbias_ref[0, :, cols] += jnp.sum(dq, axis=0, keepdims=True)
            dbias_ref[1, :, cols] += jnp.sum(dfz, axis=0, keepdims=True)
            dbias_ref[2, :, cols] += jnp.sum(dv, axis=0, keepdims=True)

        @pl.when(step == n_chunks - 1)
        def _():
            finish_slabs()
            finish_rows()

    rev = lambda s: n_chunks - 1 - s
    blk = lambda c: pl.BlockSpec((1, CHUNK, width), lambda s, c=c: (c, rev(s), 0))
    return pl.pallas_call(
        body, name="hgrn_backward",
        out_shape=(jax.ShapeDtypeStruct((3, n_rows, width), BF16),
                   jax.ShapeDtypeStruct((3, 1, width), F32),
                   jax.ShapeDtypeStruct((1, width), F32),
                   jax.ShapeDtypeStruct((N_DEV,) + g_slabs.shape[1:], g_slabs.dtype),
                   jax.ShapeDtypeStruct((N_DEV, N_ROW_GRADS, 128, g_rows.shape[2]), g_rows.dtype)),
        grid=(n_chunks,),
        in_specs=[blk(0), blk(1), blk(2), pl.BlockSpec((2, width), lambda s: (0, 0)),
                  pl.BlockSpec((1, HEADS, HEAD_DIM, HEAD_DIM), lambda s: (rev(s), 0, 0, 0)),
                  pl.BlockSpec((1, HEADS, CHUNK, CHUNK), lambda s: (rev(s), 0, 0, 0)),
                  pl.BlockSpec((CHUNK, width), lambda s: (rev(s), 0)), ANY, ANY],
        out_specs=(pl.BlockSpec((3, CHUNK, width), lambda s: (0, rev(s), 0)),
                   pl.BlockSpec((3, 1, width), lambda s: (0, 0, 0)),
                   pl.BlockSpec((1, width), lambda s: (0, 0)), ANY, ANY),
        scratch_shapes=[pltpu.VMEM((HEADS, HEAD_DIM, HEAD_DIM), F32), pltpu.VMEM((CHUNK, width), F32),
                        pltpu.VMEM((CHUNK, width), F32)] + list(SCATTER_SEMS) + list(SCATTER_ROWS_SEMS),
        compiler_params=_params(("arbitrary",)),
    )(p, p, p, lb_logits, states, scores, d_o, g_slabs, g_rows)


def _sigmoid_and_complement(x):
    s = 0.5 * jnp.tanh(0.5 * x) + 0.5
    return s, 1.0 - s


def _silu_and_grad(x):
    s, ns = _sigmoid_and_complement(x)
    return x * s, s * (1.0 + x * ns)


def _tail(p, o, zp, tgt, hg_norm_w, pool_w, pool_scale, w_down_hg, w_down_pool, w_out, final_norm_w):
    n_rows = zp.shape[0]
    tr = _row_tile(n_rows, 208)
    nt = n_rows // tr
    ext = tr + HALO
    n_groups = len(POOL_WINDOWS)

    def body(o_ref, ghg_ref, u_ref, gpool_ref, mhg_ref, mpool_ref, uhalo_ref, z_ref, tgt_hbm,
             hgw_ref, pw_ref, ps_ref, wdh_ref, wdp_ref, wout_ref, fnw_ref,
             do_ref, dp_ref, dz2_ref, lhs_ref, rhs_ref,
             dbias_ref, dhgw_ref, dpw_ref, dps_ref, dfnw_ref, loss_ref, halo_vmem, tgt_buf, tgt_sems):
        step = pl.program_id(0)
        ti = nt - 1 - step

        def target_rows(tile, slot, act):
            @pl.when(tile == 0)
            def _():
                cp = pltpu.make_async_copy(tgt_hbm.at[pl.ds(0, tr - CHUNK), :], tgt_buf.at[slot, pl.ds(CHUNK, tr - CHUNK), :],
                                           tgt_sems.at[slot])
                getattr(cp, act)()

            @pl.when(tile > 0)
            def _():
                cp = pltpu.make_async_copy(tgt_hbm.at[pl.ds(tile * tr - CHUNK, tr), :], tgt_buf.at[slot], tgt_sems.at[slot])
                getattr(cp, act)()

        @pl.when(step == 0)
        def _():
            halo_vmem[...] = jnp.zeros_like(halo_vmem)
            for r in (dbias_ref, dhgw_ref, dpw_ref, dps_ref, dfnw_ref, loss_ref):
                r[...] = jnp.zeros_like(r)
            if nt <= 2:
                tgt_buf[(nt - 1) % 2, 0:CHUNK, :] = jnp.zeros((CHUNK, D_MODEL), F32)
            target_rows(ti, 0, "start")

        @pl.when(ti > 0)
        def _():
            target_rows(ti - 1, (step + 1) % 2, "start")

        rows = ti * tr + lax.broadcasted_iota(jnp.int32, (tr, 1), 0)
        valid = rows >= PAD_ROWS
        in_loss = rows >= CHUNK
        count_pos = jnp.maximum(rows - PAD_ROWS + 1, 1).astype(F32)

        o = o_ref[...]
        hgw = hgw_ref[...]
        inv_o, on_parts = [], []
        for h in range(HEADS):
            oh = o[:, h * HEAD_DIM:(h + 1) * HEAD_DIM]
            r = lax.rsqrt(jnp.mean(oh * oh, axis=-1, keepdims=True) + EPS)
            inv_o.append(r)
            on_parts.append(oh * r)
        o_hat = jnp.concatenate(on_parts, axis=1)
        o_n = o_hat * hgw
        g_hg = ghg_ref[0]
        silu_hg, dsilu_hg = _silu_and_grad(g_hg)
        a_hg = o_n * silu_hg
        y_hg = _dot(a_hg, wdh_ref[...])

        u = jnp.where(valid, u_ref[0], 0.0)
        u_prev = jnp.where(ti > 0, uhalo_ref[0], 0.0)
        u_ext = jnp.concatenate([u_prev, u], axis=0)
        pooled_parts, mixed_parts, inv_cnt = [], [], []
        for gi, win in enumerate(POOL_WINDOWS):
            lanes = slice(gi * POOL_GDIM, (gi + 1) * POOL_GDIM)
            s = u_ext[:, lanes]
            shift = 1
            while shift < win:
                s = s + pltpu.roll(s, shift, 0)
                shift *= 2
            ic = 1.0 / jnp.minimum(count_pos, float(win))
            inv_cnt.append(ic)
            pooled = s[HALO:] * ic - u[:, lanes]
            pooled_parts.append(pooled)
            mixed_parts.append(_dot(pooled, pw_ref[gi]))
        mixed = jnp.concatenate(mixed_parts, axis=1)
        ps = ps_ref[...]
        g_pool = gpool_ref[0]
        silu_pool, dsilu_pool = _silu_and_grad(g_pool)
        a_pool = mixed * ps * silu_pool
        y_pool = _dot(a_pool, wdp_ref[...])

        m_hg, m_pool = mhg_ref[0], mpool_ref[0]
        s_hg, ns_hg = _sigmoid_and_complement(m_hg)
        s_pool, ns_pool = _sigmoid_and_complement(m_pool)
        merged = s_hg * y_hg + s_pool * y_pool
        z2 = z_ref[...] + _dot(merged, wout_ref[...])
        r2 = lax.rsqrt(jnp.mean(z2 * z2, axis=-1, keepdims=True) + EPS)
        n2 = z2 * r2
        fnw = fnw_ref[...]
        target_rows(ti, step % 2, "wait")
        err = jnp.where(in_loss, n2 * fnw - tgt_buf[step % 2], 0.0)
        loss_ref[...] += jnp.sum(jnp.sum(err * err, axis=0, keepdims=True), axis=1, keepdims=True) * (0.5 / D_MODEL)
        dy = err * (1.0 / D_MODEL)

        dfnw_ref[...] += jnp.sum(dy * n2, axis=0, keepdims=True)
        gy = dy * fnw
        dz2 = r2 * (gy - n2 * jnp.mean(gy * n2, axis=-1, keepdims=True))
        dmerged = _dot_nt(dz2, wout_ref[...])
        dy_hg = s_hg * dmerged
        dy_pool = s_pool * dmerged
        dm_hg = dmerged * y_hg * s_hg * ns_hg
        dm_pool = dmerged * y_pool * s_pool * ns_pool
        da_hg = _dot_nt(dy_hg, wdh_ref[...])
        da_pool = _dot_nt(dy_pool, wdp_ref[...])

        d_on = da_hg * silu_hg
        dg_hg = da_hg * o_n * dsilu_hg
        dhgw_ref[...] += jnp.sum(d_on * o_hat, axis=0, keepdims=True)
        gyo = d_on * hgw
        do_parts = []
        for h in range(HEADS):
            lanes = slice(h * HEAD_DIM, (h + 1) * HEAD_DIM)
            gh, nh = gyo[:, lanes], o_hat[:, lanes]
            do_parts.append(inv_o[h] * (gh - nh * jnp.mean(gh * nh, axis=-1, keepdims=True)))
        do_ref[...] = jnp.concatenate(do_parts, axis=1)

        dmixed = da_pool * ps * silu_pool
        dps_ref[...] += jnp.sum(da_pool * mixed * silu_pool, axis=0, keepdims=True)
        dg_pool = da_pool * mixed * ps * dsilu_pool
        du_parts = []
        for gi, win in enumerate(POOL_WINDOWS):
            lanes = slice(gi * POOL_GDIM, (gi + 1) * POOL_GDIM)
            dmx = dmixed[:, lanes]
            dpooled = _dot_nt(dmx, pw_ref[gi])
            dpw_ref[gi] += _dot_tn(pooled_parts[gi], dmx)
            dpt = dpooled * inv_cnt[gi]
            s = jnp.concatenate([dpt, halo_vmem[:, lanes]], axis=0)
            shift = 1
            while shift < win:
                s = s + pltpu.roll(s, ext - shift, 0)
                shift *= 2
            du_parts.append(s[:tr] - dpooled)
            halo_vmem[:, lanes] = dpt[:HALO]
        du = jnp.where(valid, jnp.concatenate(du_parts, axis=1), 0.0)

        for c, val in enumerate((dg_hg, du, dg_pool, dm_hg, dm_pool)):
            dp_ref[c] = val.astype(BF16)
            dbias_ref[c] += jnp.sum(val, axis=0, keepdims=True)
        dz2_ref[...] = dz2
        for c, (lhs, rhs) in enumerate(((merged, dz2), (a_hg, dy_hg), (a_pool, dy_pool))):
            lhs_ref[c] = lhs.astype(BF16)
            rhs_ref[c] = rhs.astype(BF16)

    rev = lambda s: nt - 1 - s
    rowblk = pl.BlockSpec((tr, D_MODEL), lambda s: (rev(s), 0))
    pblk = lambda c: pl.BlockSpec((1, tr, 1024), lambda s, c=c: (c, rev(s), 0))
    halo_blk = pl.BlockSpec((1, HALO, 1024), lambda s: (4, jnp.maximum(rev(s) * (tr // HALO) - 1, 0), 0))
    full = lambda shape: pl.BlockSpec(shape, lambda s: (0,) * len(shape))
    vec = full((1, D_MODEL))
    mat = full((D_MODEL, D_MODEL))
    act3 = jax.ShapeDtypeStruct((3, n_rows, D_MODEL), BF16)
    act3_blk = pl.BlockSpec((3, tr, D_MODEL), lambda s: (0, rev(s), 0))
    return pl.pallas_call(
        body, name="tail_forward_backward",
        out_shape=(jax.ShapeDtypeStruct((n_rows, D_MODEL), F32),
                   jax.ShapeDtypeStruct((5, n_rows, 1024), BF16),
                   jax.ShapeDtypeStruct((n_rows, D_MODEL), F32),
                   act3, act3,
                   jax.ShapeDtypeStruct((5, 1, 1024), F32),
                   jax.ShapeDtypeStruct((1, D_MODEL), F32),
                   jax.ShapeDtypeStruct((n_groups, POOL_GDIM, POOL_GDIM), F32),
                   jax.ShapeDtypeStruct((1, D_MODEL), F32),
                   jax.ShapeDtypeStruct((1, D_MODEL), F32),
                   jax.ShapeDtypeStruct((1, 1), F32)),
        grid=(nt,),
        in_specs=[rowblk, pblk(3), pblk(4), pblk(5), pblk(6), pblk(7), halo_blk, rowblk, ANY,
                  vec, full((n_groups, POOL_GDIM, POOL_GDIM)), vec, mat, mat, mat, vec],
        out_specs=(rowblk, pl.BlockSpec((5, tr, 1024), lambda s: (0, rev(s), 0)), rowblk,
                   act3_blk, act3_blk,
                   full((5, 1, 1024)), vec, full((n_groups, POOL_GDIM, POOL_GDIM)), vec, vec, full((1, 1))),
        scratch_shapes=[pltpu.VMEM((HALO, D_MODEL), F32), pltpu.VMEM((2, tr, D_MODEL), F32), pltpu.SemaphoreType.DMA((2,))],
        compiler_params=_params(("arbitrary",)),
    )(o, p, p, p, p, p, p, zp, tgt, hg_norm_w, pool_w, pool_scale, w_down_hg, w_down_pool, w_out, final_norm_w)


def _in_projection_backward(dp_a, dp_b, w_blocks, zp, dz2, norm_w, chip_sums, misc):
    n_rows = zp.shape[0]
    tr = _row_tile(n_rows, 416)
    nt = n_rows // tr
    na, nb = dp_a.shape[0], dp_b.shape[0]

    def body(dpa_ref, dpb_ref, w_hbm, z_ref, dz2_ref, nw_ref, gs_hbm, misc_hbm, gx_hbm, head_ref, dnw_ref, rs_hbm,
             rm_hbm, rmeta_hbm, rnw_hbm, w_vmem, sem, dz_buf, gx_sems, late, low_send, low_recv, low_local,
             m_send, m_recv, m_local, late_send, late_recv, late_local):
        i = pl.program_id(0)
        start_slabs, finish_slabs = _scatter_low(gs_hbm, rs_hbm, low_send, low_recv, low_local)
        start_misc, finish_misc = _scatter_slabs(misc_hbm, 0, rm_hbm, m_send, m_recv, m_local)
        me = _me()

        def late_copies():
            meta_cols = lambda dev: late.at[pl.ds(0, N_META), pl.ds(dev * 128, 128)]
            nw_rows = late.at[pl.ds(N_META, 8), :]
            local = [pltpu.make_async_copy(meta_cols(me), rmeta_hbm.at[me], late_local.at[0]),
                     pltpu.make_async_copy(nw_rows, rnw_hbm.at[me], late_local.at[1])]
            sends, arrivals = [], []
            for kk in range(1, N_DEV):
                peer = jnp.bitwise_xor(me, kk)
                for n, (src, dst, own_src) in enumerate(((meta_cols(peer), rmeta_hbm, meta_cols(me)),
                                                        (nw_rows, rnw_hbm, nw_rows))):
                    sems = (late_send.at[2 * (kk - 1) + n], late_recv.at[2 * (kk - 1) + n])
                    sends.append(_remote(src, dst.at[me], *sems, kk))
                    arrivals.append(_remote(own_src, dst.at[peer], *sems, kk))
            return local, sends, arrivals

        def wait_rows_out(tile):
            @pl.when(tile == 0)
            def _():
                pltpu.make_async_copy(dz_buf.at[0, pl.ds(CHUNK, tr - CHUNK), :], gx_hbm.at[pl.ds(0, tr - CHUNK), :],
                                      gx_sems.at[0]).wait()

            @pl.when(tile > 0)
            def _():
                pltpu.make_async_copy(dz_buf.at[tile % 2], gx_hbm.at[pl.ds(tile * tr - CHUNK, tr), :],
                                      gx_sems.at[tile % 2]).wait()

        @pl.when(i == 0)
        def _():
            start_slabs()
            start_misc()
            cp = pltpu.make_async_copy(w_hbm, w_vmem, sem)
            cp.start()
            cp.wait()
            dnw_ref[...] = jnp.zeros_like(dnw_ref)

        dh = jnp.zeros((tr, D_MODEL), F32)
        half_cols = w_vmem.shape[-1]
        for j in range(na + nb):
            dp_ref, jj = (dpa_ref, j) if j < na else (dpb_ref, j - na)
            for half in range(2):
                dh = dh + _dot_nt(dp_ref[jj, :, half * half_cols:(half + 1) * half_cols], w_vmem[j, half])
        z = z_ref[...]
        r = lax.rsqrt(jnp.mean(z * z, axis=-1, keepdims=True) + EPS)
        n1 = z * r
        dnw_ref[...] += jnp.sum(dh * n1, axis=0, keepdims=True)
        gh = dh * nw_ref[...]
        dz = dz2_ref[...] + r * (gh - n1 * jnp.mean(gh * n1, axis=-1, keepdims=True))

        @pl.when(i >= 2)
        def _():
            wait_rows_out(i - 2)

        dz_buf[i % 2] = dz

        @pl.when(i == 0)
        def _():
            head_ref[...] = dz[0:CHUNK]
            late[0:N_META, :] = dz[PAD_ROWS:CHUNK]
            pltpu.make_async_copy(dz_buf.at[0, pl.ds(CHUNK, tr - CHUNK), :], gx_hbm.at[pl.ds(0, tr - CHUNK), :],
                                  gx_sems.at[0]).start()

        @pl.when(i > 0)
        def _():
            pltpu.make_async_copy(dz_buf.at[i % 2], gx_hbm.at[pl.ds(i * tr - CHUNK, tr), :], gx_sems.at[i % 2]).start()

        @pl.when(i == nt - 1)
        def _():
            late[N_META:N_META + 8, :] = jnp.broadcast_to(dnw_ref[...], (8, D_MODEL))
            local, sends, arrivals = late_copies()
            for cp in local + sends:
                cp.start()
            if nt >= 2:
                wait_rows_out(i - 1)
            wait_rows_out(i)
            finish_slabs()
            finish_misc()
            for cp in arrivals:
                cp.wait_recv()
            for cp in sends:
                cp.wait_send()
            for cp in local:
                cp.wait()

    rowblk = pl.BlockSpec((tr, D_MODEL), lambda i: (i, 0))
    vec = pl.BlockSpec((1, D_MODEL), lambda i: (0, 0))
    dma = pltpu.SemaphoreType.DMA
    return pl.pallas_call(
        body, name="in_projection_backward",
        out_shape=(jax.ShapeDtypeStruct((n_rows - CHUNK, D_MODEL), F32), jax.ShapeDtypeStruct((CHUNK, D_MODEL), F32),
                   jax.ShapeDtypeStruct((1, D_MODEL), F32),
                   jax.ShapeDtypeStruct((4,) + chip_sums.shape[1:], chip_sums.dtype),
                   jax.ShapeDtypeStruct(misc.shape, misc.dtype),
                   jax.ShapeDtypeStruct((N_DEV, N_META, 128), F32), jax.ShapeDtypeStruct((N_DEV, 8, D_MODEL), F32)),
        grid=(nt,),
        in_specs=[pl.BlockSpec((na, tr, 1024), lambda i: (0, i, 0)), pl.BlockSpec((nb, tr, 1024), lambda i: (0, i, 0)),
                  ANY, rowblk, rowblk, vec, ANY, ANY],
        out_specs=(ANY, pl.BlockSpec((CHUNK, D_MODEL), lambda i: (0, 0)), vec, ANY, ANY, ANY, ANY),
        scratch_shapes=[pltpu.VMEM(w_blocks.shape, w_blocks.dtype), dma(()),
                        pltpu.VMEM((2, tr, D_MODEL), F32), dma((2,)), pltpu.VMEM((N_META + 8, D_MODEL), F32),
                        dma((2,)), dma((3,)), dma(())] + list(SCATTER_SEMS) + [dma((14,)), dma((14,)), dma((2,))],
        compiler_params=_params(("arbitrary",)),
    )(dp_a, dp_b, w_blocks, zp, dz2, norm_w, chip_sums, misc)


def _weight_grad(xs, ys, name):
    shared = xs.ndim == 2
    n_rows, m = xs.shape[-2:]
    nb, _, n = ys.shape
    tk = _row_tile(n_rows, 4160)
    n_k = n_rows // tk

    def body(x_ref, y_ref, o_ref, acc):
        k = pl.program_id(1)

        @pl.when(k == 0)
        def _():
            acc[...] = jnp.zeros_like(acc)

        acc[...] += _dot_tn(x_ref[...] if shared else x_ref[0], y_ref[0])

        @pl.when(k == n_k - 1)
        def _():
            o_ref[0] = acc[...].astype(o_ref.dtype)

    x_spec = pl.BlockSpec((tk, m), lambda j, k: (k, 0)) if shared else pl.BlockSpec((1, tk, m), lambda j, k: (j, k, 0))
    return pl.pallas_call(
        body, name=name,
        out_shape=jax.ShapeDtypeStruct((nb, m, n), BF16),
        grid=(nb, n_k),
        in_specs=[x_spec, pl.BlockSpec((1, tk, n), lambda j, k: (j, k, 0))],
        out_specs=pl.BlockSpec((1, m, n), lambda j, k: (j, 0, 0)),
        scratch_shapes=[pltpu.VMEM((m, n), F32)],
        compiler_params=_params(("arbitrary", "arbitrary")),
    )(xs, ys)


def kernel(x, meta_tokens, norm_w, w_in, b_in, lb_logits, hg_norm_w, pool_w, pool_scale, w_down_hg, w_down_pool, w_out, final_norm_w, loss_target, m_meta_tokens, m_norm_w, m_w_in, m_b_in, m_lb_logits, m_hg_norm_w, m_pool_w, m_pool_scale, m_w_down_hg, m_w_down_pool, m_w_out, m_final_norm_w, v_meta_tokens, v_norm_w, v_w_in, v_b_in, v_lb_logits, v_hg_norm_w, v_pool_w, v_pool_scale, v_w_down_hg, v_w_down_pool, v_w_out, v_final_norm_w):
    seq = x.shape[1]

    meta_full = _all_gather_small(meta_tokens).transpose(1, 0, 2).reshape(N_META, D_MODEL)
    w_rest = jnp.concatenate([w_down_hg[0].astype(BF16), w_down_pool[0].astype(BF16), w_out[0].astype(BF16),
                              pool_w[0].astype(BF16).reshape(32, 1024)], axis=0)

    zp = jnp.concatenate([jnp.zeros((PAD_ROWS, D_MODEL), F32), meta_full, x[0]], axis=0)
    p, h, w_blocks = _in_projection(zp, norm_w, w_in[0].astype(BF16), b_in.reshape(N_COLBLK, 1, 1024), _gather_units())
    o, states, scores, rest = _hgrn_forward(p, lb_logits, w_rest)
    wdh = rest[:, REST_W_DOWN_HG:REST_W_DOWN_HG + 128].reshape(1024, 1024)
    wdp = rest[:, REST_W_DOWN_POOL:REST_W_DOWN_POOL + 128].reshape(1024, 1024)
    wout = rest[:, REST_W_OUT:REST_W_OUT + 128].reshape(1024, 1024)
    pw = rest[:, REST_POOL_W:REST_POOL_W + 32].reshape(N_DEV, 4, 32, 256).transpose(1, 0, 2, 3).reshape(4, 256, 256)
    (d_o, dp_b, dz2, grad_lhs, grad_rhs, dbias_b, d_hgw, d_pw, d_ps, d_fnw, loss_part) = _tail(
        p, o, zp, loss_target[0], hg_norm_w, pw, pool_scale, wdh, wdp, wout, final_norm_w.reshape(1, D_MODEL))
    n_a = N_COLBLK - dp_b.shape[0]
    g_hi = _weight_grad(h, dp_b, "weight_grad_in_hi")
    g_rows = _weight_grad(grad_lhs, grad_rhs, "weight_grad_rows")
    dp_a, dbias_a, d_lb, recv_hi, recv_rows = _hgrn_backward(p, lb_logits, states, scores, d_o, g_hi, n_a, g_rows)
    assert n_a == LOW_OWNERS
    chip_lo = _weight_grad_low(h, dp_a)

    lb = jax.nn.sigmoid(lb_logits[0:1] - lb_logits[1:2])
    d_l0 = d_lb * lb * (1.0 - lb)
    replicated = jnp.concatenate([dbias_a.reshape(3, 1024), dbias_b.reshape(5, 1024), jnp.zeros((1, 1024), F32), d_l0, -d_l0,
                                  d_hgw, d_ps, d_fnw,
                                  jnp.pad(loss_part, ((0, MISC_ROWS - MISC_LOSS - 1), (0, 1023)))], axis=0)
    d_pw_blocks = d_pw.reshape(4, N_DEV, 32, 256).transpose(1, 0, 2, 3).reshape(N_DEV, 32, 1024)
    g_misc = jnp.concatenate([d_pw_blocks, jnp.zeros((N_DEV, N_META, 1024), F32),
                              jnp.broadcast_to(replicated[None], (N_DEV, 16, 1024))], axis=1)
    dz_seq, _, _, recv_lo, recv_misc, recv_meta, recv_nw = _in_projection_backward(
        dp_a, dp_b, w_blocks, zp, dz2, norm_w, chip_lo, g_misc)

    as_rows = lambda t, n: t.reshape(n, 1024)
    small = [(0, MISC_POOL_W, 1024, tuple(as_rows(t, 32) for t in (pool_w, m_pool_w, v_pool_w))),
             (1, 0, 128, (meta_tokens, m_meta_tokens, v_meta_tokens)),
             (0, MISC_B_IN, 1024, tuple(as_rows(t, 8) for t in (b_in, m_b_in, v_b_in))),
             (2, 0, 1024, (norm_w, m_norm_w, v_norm_w)),
             (0, MISC_LB, 1024, (lb_logits, m_lb_logits, v_lb_logits)),
             (0, MISC_HG_NORM_W, 1024, (hg_norm_w, m_hg_norm_w, v_hg_norm_w)),
             (0, MISC_POOL_SCALE, 1024, (pool_scale, m_pool_scale, v_pool_scale)),
             (0, MISC_FINAL_NORM_W, 1024, tuple(as_rows(t, 1) for t in (final_norm_w, m_final_norm_w, v_final_norm_w)))]
    res, loss = _finish(recv_hi, recv_lo, n_a, recv_rows, (recv_misc, recv_meta, recv_nw), (w_in, m_w_in, v_w_in),
                  [(w_out, m_w_out, v_w_out), (w_down_hg, m_w_down_hg, v_w_down_hg), (w_down_pool, m_w_down_pool, v_w_down_pool)],
                  small)
    r_w_in, r_w_out, r_wdh, r_wdp, r_pw, r_meta, r_b_in, r_nw, r_lb, r_hgw, r_ps, r_fnw = res
    grad_x = dz_seq.reshape(1, seq, D_MODEL)
    per_kind = [(r_meta[k], r_nw[k], r_w_in[k], r_b_in[k].reshape(1, 8192), r_lb[k], r_hgw[k], r_pw[k].reshape(1, 4, 32, 256),
                 r_ps[k], r_wdh[k], r_wdp[k], r_w_out[k], r_fnw[k].reshape(1024)) for k in range(4)]
    return (loss, grad_x, *per_kind[0], *per_kind[1], *per_kind[2], *per_kind[3])
```

```python
import functools

import jax
import jax.numpy as jnp
from jax import lax
from jax.experimental import pallas as pl
from jax.experimental.pallas import tpu as pltpu

F32 = jnp.float32
BF16 = jnp.bfloat16

D_MODEL = 1024
N_META = 16
HEADS = 8
HEAD_DIM = 128
CHUNK = 64
SUB = 8
N_SUB = CHUNK // SUB
PAD_ROWS = CHUNK - N_META
POOL_WINDOWS = (2, 4, 8, 16)
POOL_GDIM = D_MODEL // len(POOL_WINDOWS)
HALO = 16
EPS = 1e-6
N_DEV = 8
N_COLBLK = 8
ADAM_LR, ADAM_B1, ADAM_B2, ADAM_EPS, ADAM_WD, ADAM_STEP = 0.001, 0.9, 0.999, 1e-08, 0.01, 10

VMEM_LIMIT = 56 * 1024 * 1024
MESH = pl.DeviceIdType.MESH
ANY = pl.BlockSpec(memory_space=pl.ANY)

REST_W_DOWN_HG = 0
REST_W_DOWN_POOL = 128
REST_W_OUT = 256
REST_POOL_W = 384
MISC_POOL_W = 0
MISC_META = 32
MISC_B_IN = 48
MISC_NORM_W = 56
MISC_LB = 57
MISC_HG_NORM_W = 59
MISC_POOL_SCALE = 60
MISC_FINAL_NORM_W = 61
MISC_LOSS = 62
MISC_ROWS = 64


def _params(sem=None):
    return pltpu.CompilerParams(dimension_semantics=sem, vmem_limit_bytes=VMEM_LIMIT)


def _row_tile(n_rows, prefer):
    best = 16
    for t in range(16, prefer + 1, 16):
        if n_rows % t == 0:
            best = t
    return best


def _sigmoid_pair(x):
    e = jnp.exp(-jnp.abs(x))
    r = 1.0 / (1.0 + e)
    er = e * r
    pos = x >= 0
    return jnp.where(pos, r, er), jnp.where(pos, er, r)


def _dot(a, b):
    return jnp.dot(a.astype(BF16), b.astype(BF16), preferred_element_type=F32)


def _dot_nt(a, b):
    return lax.dot_general(a.astype(BF16), b.astype(BF16), (((1,), (1,)), ((), ())), preferred_element_type=F32)


def _dot_tn(a, b):
    return lax.dot_general(a.astype(BF16), b.astype(BF16), (((0,), (0,)), ((), ())), preferred_element_type=F32)


def _device_index(px, py, pc):
    return 4 * px + 2 * py + pc


def _direct_gather(src_ref, dst_ref, send_sems, recv_sems, local_sem):
    x, y, c = lax.axis_index("x"), lax.axis_index("y"), lax.axis_index("c")
    own = pltpu.make_async_copy(src_ref, dst_ref.at[_device_index(x, y, c)], local_sem)
    sends, arrivals = [], []
    for k in range(1, N_DEV):
        peer = (1 - x if k & 4 else x, 1 - y if k & 2 else y, 1 - c if k & 1 else c)
        for slot, out in ((_device_index(x, y, c), sends), (_device_index(*peer), arrivals)):
            out.append(pltpu.make_async_remote_copy(
                src_ref=src_ref, dst_ref=dst_ref.at[slot], send_sem=send_sems.at[k - 1], recv_sem=recv_sems.at[k - 1],
                device_id=peer, device_id_type=MESH))
    return own, sends, arrivals


GATHER_SEMS = [pltpu.SemaphoreType.DMA((N_DEV - 1,)), pltpu.SemaphoreType.DMA((N_DEV - 1,)), pltpu.SemaphoreType.DMA(())]


def _all_gather_small(block):
    def body(x_ref, out_ref, send_sems, recv_sems, local_sem):
        own, sends, arrivals = _direct_gather(x_ref, out_ref, send_sems, recv_sems, local_sem)
        own.start()
        for cp in sends:
            cp.start()
        for cp in arrivals:
            cp.wait_recv()
        for cp in sends:
            cp.wait_send()
        own.wait()

    return pl.pallas_call(
        body, name="all_gather_meta",
        out_shape=jax.ShapeDtypeStruct((N_DEV,) + block.shape, block.dtype),
        in_specs=[ANY], out_specs=ANY, scratch_shapes=list(GATHER_SEMS),
    )(block)


def _peer(k):
    x, y, c = lax.axis_index("x"), lax.axis_index("y"), lax.axis_index("c")
    return (1 - x if k & 4 else x, 1 - y if k & 2 else y, 1 - c if k & 1 else c)


def _me():
    return _device_index(lax.axis_index("x"), lax.axis_index("y"), lax.axis_index("c"))


def _remote(src, dst, send_sem, recv_sem, peer_bits):
    return pltpu.make_async_remote_copy(src_ref=src, dst_ref=dst, send_sem=send_sem, recv_sem=recv_sem,
                                        device_id=_peer(peer_bits), device_id_type=MESH)


N_ROW_GRADS = 3
SCATTER_SEMS = [pltpu.SemaphoreType.DMA((N_DEV - 1,)), pltpu.SemaphoreType.DMA((N_DEV - 1,)), pltpu.SemaphoreType.DMA(())]
SCATTER_ROWS_SEMS = [pltpu.SemaphoreType.DMA((7 * N_ROW_GRADS,)), pltpu.SemaphoreType.DMA((7 * N_ROW_GRADS,)),
                     pltpu.SemaphoreType.DMA((N_ROW_GRADS,))]


def _scatter_slabs(g_ref, first, recv_ref, send_sems, recv_sems, local_sem):
    n = g_ref.shape[0]
    me = _me()

    def each(on_send, on_local, on_arrival):
        for kk in range(1, N_DEV):
            peer = jnp.bitwise_xor(me, kk)

            @pl.when((peer >= first) & (peer < first + n))
            def _(kk=kk, peer=peer):
                on_send(_remote(g_ref.at[peer - first], recv_ref.at[me], send_sems.at[kk - 1], recv_sems.at[kk - 1], kk))

        @pl.when((me >= first) & (me < first + n))
        def _():
            on_local(pltpu.make_async_copy(g_ref.at[me - first], recv_ref.at[me], local_sem))
            if on_arrival is not None:
                for kk in range(1, N_DEV):
                    on_arrival(_remote(g_ref.at[0], recv_ref.at[jnp.bitwise_xor(me, kk)], send_sems.at[kk - 1],
                                       recv_sems.at[kk - 1], kk))

    start = lambda: each(lambda cp: cp.start(), lambda cp: cp.start(), None)
    finish = lambda: each(lambda cp: cp.wait_send(), lambda cp: cp.wait(), lambda cp: cp.wait_recv())
    return start, finish


def _scatter_rows(g_ref, recv_ref, send_sems, recv_sems, local_sems):
    me = _me()
    rows = lambda m, dev: g_ref.at[m, pl.ds(dev * 128, 128), :]

    def copies():
        local = [pltpu.make_async_copy(rows(m, me), recv_ref.at[me, m], local_sems.at[m]) for m in range(N_ROW_GRADS)]
        sends, arrivals = [], []
        for m in range(N_ROW_GRADS):
            for kk in range(1, N_DEV):
                peer, sems = jnp.bitwise_xor(me, kk), (send_sems.at[7 * m + kk - 1], recv_sems.at[7 * m + kk - 1])
                sends.append(_remote(rows(m, peer), recv_ref.at[me, m], *sems, kk))
                arrivals.append(_remote(rows(m, me), recv_ref.at[peer, m], *sems, kk))
        return local, sends, arrivals

    def start():
        local, sends, _ = copies()
        for cp in local + sends:
            cp.start()

    def finish():
        local, sends, arrivals = copies()
        for cp in arrivals:
            cp.wait_recv()
        for cp in sends:
            cp.wait_send()
        for cp in local:
            cp.wait()

    return start, finish


LOW_OWNERS = 3


def _weight_grad_low(h, dp_a):
    n_rows = h.shape[0]
    tk = _row_tile(n_rows, 2080)
    n_k = n_rows // tk
    order = jnp.where(lax.axis_index("c") == 0, jnp.array([1, 0, 2], jnp.int32), jnp.array([0, 2, 1], jnp.int32))

    def body(order_ref, h_ref, y_ref, out_ref, acc, send_buf, got, send_sems, recv_sems):
        t, k = pl.program_id(0), pl.program_id(1)
        c = lax.axis_index("c")

        @pl.when(k == 0)
        def _():
            acc[...] = jnp.zeros_like(acc)

        acc[...] += _dot_tn(h_ref[...], y_ref[0])

        def to_sibling(slot):
            return _remote(send_buf.at[slot], got.at[slot], send_sems.at[slot], recv_sems.at[slot], 1)

        def send(slot):
            send_buf[slot] = acc[...].astype(send_buf.dtype)
            to_sibling(slot).start()

        def keep(slot):
            to_sibling(slot).wait_recv()
            out_ref[slot] = (acc[...] + got[slot].astype(F32)).astype(out_ref.dtype)

        done = k == n_k - 1
        for core, step, action, slot in ((0, 0, send, 0), (0, 1, keep, 0), (0, 2, keep, 1),
                                         (1, 0, send, 0), (1, 1, send, 1), (1, 2, keep, 0)):
            @pl.when(done & (c == core) & (t == step))
            def _(action=action, slot=slot):
                action(slot)

        @pl.when(done & (t == LOW_OWNERS - 1))
        def _():
            to_sibling(0).wait_send()

            @pl.when(c == 1)
            def _():
                to_sibling(1).wait_send()
                out_ref[1] = jnp.zeros(out_ref.shape[1:], out_ref.dtype)

    pair = (2, h.shape[1], dp_a.shape[2])
    return pl.pallas_call(
        body, name="weight_grad_in_low",
        out_shape=jax.ShapeDtypeStruct(pair, BF16),
        grid_spec=pltpu.PrefetchScalarGridSpec(
            num_scalar_prefetch=1, grid=(LOW_OWNERS, n_k),
            in_specs=[pl.BlockSpec((tk, h.shape[1]), lambda t, k, o: (k, 0)),
                      pl.BlockSpec((1, tk, dp_a.shape[2]), lambda t, k, o: (o[t], k, 0))],
            out_specs=pl.BlockSpec(pair, lambda t, k, o: (0, 0, 0)),
            scratch_shapes=[pltpu.VMEM(pair[1:], F32), pltpu.VMEM(pair, BF16), pltpu.VMEM(pair, BF16),
                            pltpu.SemaphoreType.DMA((2,)), pltpu.SemaphoreType.DMA((2,))]),
        compiler_params=_params(("arbitrary", "arbitrary")),
    )(order, h, dp_a)


def _scatter_low(part_ref, recv_ref, send_sems, recv_sems, local_sem):
    x, y, c = lax.axis_index("x"), lax.axis_index("y"), lax.axis_index("c")
    chip = 2 * x + y
    routes = ((0, (0, 0, c), 0, None), (1, (0, 1, 0), 1, 0))

    def each(on_send, on_local, on_arrival):
        for slot, owner, owner_chip, core in routes:
            holds = (c == core) if core is not None else (c >= 0)
            rel = jnp.bitwise_xor(chip, owner_chip)

            @pl.when(holds & (rel != 0))
            def _(slot=slot, owner=owner, rel=rel):
                on_send(pltpu.make_async_remote_copy(
                    src_ref=part_ref.at[slot], dst_ref=recv_ref.at[chip], send_sem=send_sems.at[slot],
                    recv_sem=recv_sems.at[rel - 1], device_id=owner, device_id_type=MESH))

            @pl.when(holds & (rel == 0))
            def _(slot=slot, owner=owner, owner_chip=owner_chip):
                on_local(pltpu.make_async_copy(part_ref.at[slot], recv_ref.at[chip], local_sem))
                if on_arrival is not None:
                    for r in range(1, 4):
                        on_arrival(pltpu.make_async_remote_copy(
                            src_ref=part_ref.at[slot], dst_ref=recv_ref.at[r ^ owner_chip], send_sem=send_sems.at[slot],
                            recv_sem=recv_sems.at[r - 1], device_id=owner, device_id_type=MESH))

    start = lambda: each(lambda cp: cp.start(), lambda cp: cp.start(), None)
    finish = lambda: each(lambda cp: cp.wait_send(), lambda cp: cp.wait(), lambda cp: cp.wait_recv())
    return start, finish


def _adam_update(g, w, m, v):
    mn = ADAM_B1 * m + (1.0 - ADAM_B1) * g
    vn = ADAM_B2 * v + (1.0 - ADAM_B2) * (g * g)
    m_hat = mn / (1.0 - ADAM_B1 ** ADAM_STEP)
    v_hat = vn / (1.0 - ADAM_B2 ** ADAM_STEP)
    return -ADAM_LR * (m_hat / (jnp.sqrt(v_hat) + ADAM_EPS) + ADAM_WD * w), mn, vn


def _device_sum(parts):
    t = [p.astype(F32) for p in parts]
    return ((t[0] + t[1]) + (t[2] + t[3])) + ((t[4] + t[5]) + (t[6] + t[7]))


def _finish(a_hi, a_lo, n_lo, b3, sources, big, rows3, small):
    n_steps = 4
    tb, tr3 = 1024 // n_steps, 128 // n_steps

    def body(*refs):
        it = iter(refs)
        hi_ref, lo_ref, b_ref = next(it), next(it), next(it)
        src_refs = [next(it) for _ in sources]
        big_in = [next(it) for _ in range(3)]
        rows_in = [[next(it) for _ in range(3)] for _ in rows3]
        small_in = [[next(it) for _ in range(3)] for _ in small]
        big_out = [next(it) for _ in range(4)]
        rows_out = [[next(it) for _ in range(4)] for _ in rows3]
        small_out = [[next(it) for _ in range(4)] for _ in small]
        loss_ref = next(it)

        def apply(g, ins, outs):
            d, mn, vn = _adam_update(g, ins[0][...], ins[1][...], ins[2][...])
            for r, val in zip(outs, (g, d, mn, vn)):
                r[...] = val

        lo = [lo_ref[s].astype(F32) for s in range(4)]
        g_big = jnp.where(_me() < n_lo, (lo[0] + lo[1]) + (lo[2] + lo[3]), _device_sum([hi_ref[s] for s in range(N_DEV)]))
        apply(g_big[None], big_in, big_out)
        for k in range(len(rows3)):
            apply(_device_sum([b_ref[s, k] for s in range(N_DEV)])[None], rows_in[k], rows_out[k])

        @pl.when(pl.program_id(0) == 0)
        def _():
            loss_ref[...] = _device_sum([src_refs[0][s, MISC_LOSS:MISC_LOSS + 1, :] for s in range(N_DEV)])
            for (src, row0, lanes, ins), r_in, r_out in zip(small, small_in, small_out):
                n = ins[0].shape[0]
                apply(_device_sum([src_refs[src][s, row0:row0 + n, :lanes] for s in range(N_DEV)]), r_in, r_out)

    whole = lambda shape: pl.BlockSpec(shape, lambda i: (0,) * len(shape))
    big_blk = pl.BlockSpec((1, tb, 1024), lambda i: (0, i, 0))
    rows_blk = pl.BlockSpec((1, tr3, 1024), lambda i: (0, i, 0))
    in_specs = [pl.BlockSpec((N_DEV, tb, 1024), lambda i: (0, i, 0)), pl.BlockSpec((4, tb, 1024), lambda i: (0, i, 0)),
                pl.BlockSpec((N_DEV, 3, tr3, 1024), lambda i: (0, 0, i, 0))] + [whole(t.shape) for t in sources]
    in_specs += [big_blk] * 3 + [rows_blk] * (3 * len(rows3))
    out_specs = [big_blk] * 4 + [rows_blk] * (4 * len(rows3))
    out_shape = [jax.ShapeDtypeStruct(big[0].shape, F32)] * 4
    for w, _, _ in rows3:
        out_shape += [jax.ShapeDtypeStruct(w.shape, F32)] * 4
    args = [a_hi, a_lo, b3, *sources, *big]
    for t in rows3:
        args += list(t)
    for _, _, _, t in small:
        in_specs += [whole(t[0].shape)] * 3
        out_specs += [whole(t[0].shape)] * 4
        out_shape += [jax.ShapeDtypeStruct(t[0].shape, F32)] * 4
        args += list(t)
    out_specs.append(whole((1, 1024)))
    out_shape.append(jax.ShapeDtypeStruct((1, 1024), F32))
    outs = pl.pallas_call(
        body, name="reduce_sum_adamw", out_shape=tuple(out_shape), grid=(n_steps,),
        in_specs=in_specs, out_specs=tuple(out_specs),
        compiler_params=_params(("arbitrary",)),
    )(*args)
    return [tuple(outs[4 * k:4 * k + 4]) for k in range(len(outs) // 4)], outs[-1][0, 0]


GATHER_UNITS = ((0, 0), (0, 1), (1, 0), (1, 1)) + tuple((place, 0) for place in range(2, 8)) + tuple(
    (place, 1) for place in range(2, 8))


def _gather_order():
    x, y, c = lax.axis_index("x"), lax.axis_index("y"), lax.axis_index("c")
    chips = [(1 - x, y), (x, 1 - y), (1 - x, 1 - y)]
    order = [_device_index(x, y, c), _device_index(x, y, 1 - c)]
    order += [_device_index(*q, c) for q in chips] + [_device_index(*q, 1 - c) for q in chips]
    return order


def _gather_units():
    order = _gather_order()
    blocks = jnp.stack([order[place] for place, _ in GATHER_UNITS]).astype(jnp.int32)
    return blocks, jnp.array([half for _, half in GATHER_UNITS], jnp.int32)


def _in_projection(zp, norm_w, w_shard, b_blocks, units):
    n_rows = zp.shape[0]
    tr = _row_tile(n_rows, 832)
    nt = n_rows // tr

    half_cols = 1024 // 2
    n_units = len(GATHER_UNITS)

    def body(blocks_ref, halves_ref, z_ref, nw_ref, w_hbm, b_ref, p_ref, h_ref, w_out, w_vmem, h_all,
             send_sems, recv_sems, local_sem, out_sems):
        s, i = pl.program_id(0), pl.program_id(1)
        x, y, c = lax.axis_index("x"), lax.axis_index("y"), lax.axis_index("c")
        me, sibling = (x, y, c), (x, y, 1 - c)
        chips = [(1 - x, y), (x, 1 - y), (1 - x, 1 - y)]

        def slot(blk, half):
            return w_vmem.at[_device_index(*blk), half]

        def mine(half):
            return w_hbm.at[:, pl.ds(half * half_cols, half_cols)]

        def copy(k, half, blk, to, own_block=False):
            return pltpu.make_async_remote_copy(
                src_ref=mine(half) if own_block else slot(blk, half), dst_ref=slot(blk, half),
                send_sem=send_sems.at[2 * k + half], recv_sem=recv_sems.at[2 * k + half], device_id=to, device_id_type=MESH)

        own = [pltpu.make_async_copy(mine(half), slot(me, half), local_sem.at[half]) for half in range(2)]
        first = [copy(k, half, me, to, own_block=True) for half in range(2)
                 for k, to in enumerate([sibling] + [(*q, c) for q in chips])]
        passed = [[copy(4 + j, half, (*q, c), sibling) for j, q in enumerate(chips)] for half in range(2)]
        sources = [None, sibling] + [(*q, c) for q in chips] + [(*q, 1 - c) for q in chips]
        arrival = lambda place, half: copy(place - 1, half, sources[place], me)

        def keep(unit):
            blk, half = blocks_ref[unit], halves_ref[unit]
            return pltpu.make_async_copy(w_vmem.at[blk, half], w_out.at[blk, half], out_sems.at[unit])

        for unit, (place, half) in enumerate(GATHER_UNITS):
            @pl.when((i == 0) & (s == unit))
            def _(unit=unit, place=place, half=half):
                if unit == 0:
                    for cp in own + first:
                        cp.start()
                if place == 0:
                    own[half].wait()
                else:
                    arrival(place, half).wait_recv()
                    if 2 <= place <= 4:
                        passed[half][place - 2].start()
                keep(unit).start()

        @pl.when(s == 0)
        def _():
            z = z_ref[...]
            r = lax.rsqrt(jnp.mean(z * z, axis=-1, keepdims=True) + EPS)
            h = (z * r * nw_ref[...]).astype(BF16)
            h_all[i] = h
            h_ref[...] = h

        p_ref[0] = jnp.dot(h_all[i], w_vmem[blocks_ref[s], halves_ref[s]], preferred_element_type=F32) + b_ref[0]

        @pl.when((s == n_units - 1) & (i == nt - 1))
        def _():
            for cp in first + passed[0] + passed[1]:
                cp.wait_send()
            for unit in range(n_units):
                keep(unit).wait()

    first_pass = lambda s, i, rest: jnp.where(s == 0, i, rest)
    return pl.pallas_call(
        body, name="in_projection_gather",
        out_shape=(jax.ShapeDtypeStruct((N_COLBLK, n_rows, 1024), F32),
                   jax.ShapeDtypeStruct((n_rows, D_MODEL), BF16),
                   jax.ShapeDtypeStruct((N_DEV, 2, D_MODEL, half_cols), BF16)),
        grid_spec=pltpu.PrefetchScalarGridSpec(
            num_scalar_prefetch=2, grid=(n_units, nt),
            in_specs=[pl.BlockSpec((tr, D_MODEL), lambda s, i, blk, hf: (first_pass(s, i, 0), 0)),
                      pl.BlockSpec((1, D_MODEL), lambda s, i, blk, hf: (0, 0)), ANY,
                      pl.BlockSpec((1, 1, half_cols), lambda s, i, blk, hf: (blk[s], 0, hf[s]))],
            out_specs=(pl.BlockSpec((1, tr, half_cols), lambda s, i, blk, hf: (blk[s], i, hf[s])),
                       pl.BlockSpec((tr, D_MODEL), lambda s, i, blk, hf: (first_pass(s, i, nt - 1), 0)), ANY),
            scratch_shapes=[pltpu.VMEM((N_DEV, 2, D_MODEL, half_cols), BF16), pltpu.VMEM((nt, tr, D_MODEL), BF16),
                            pltpu.SemaphoreType.DMA((14,)), pltpu.SemaphoreType.DMA((14,)), pltpu.SemaphoreType.DMA((2,)),
                            pltpu.SemaphoreType.DMA((n_units,))]),
        compiler_params=_params(("arbitrary", "arbitrary")),
    )(*units, zp, norm_w, w_shard, b_blocks)


def _lower_bound(lb_ref):
    l0, l1 = lb_ref[0:1, :], lb_ref[1:2, :]
    _, lb = _sigmoid_pair(l1 - l0)
    return lb


def _chunk_gates(fz, lb, valid):
    sig, nsig = _sigmoid_pair(fz)
    f = lb + (1.0 - lb) * sig
    g2 = jnp.where(valid, jnp.log2(f), 0.0)
    k = jnp.where(valid, (1.0 - lb) * nsig, 0.0)
    return sig, nsig, f, g2, k


def _tri(n, upper=False):
    r = lax.broadcasted_iota(jnp.int32, (n, n), 0)
    c = lax.broadcasted_iota(jnp.int32, (n, n), 1)
    return jnp.where((r <= c) if upper else (r >= c), 1.0, 0.0).astype(BF16)


def _tri_dot(tri, x):
    hi = x.astype(BF16)
    rest = x - hi.astype(F32)
    mid = rest.astype(BF16)
    low = (rest - mid.astype(F32)).astype(BF16)
    return (jnp.dot(tri, hi, preferred_element_type=F32) + jnp.dot(tri, mid, preferred_element_type=F32)
            + jnp.dot(tri, low, preferred_element_type=F32))


def _intra_scores(q_ref, k_ref, b2_ref, col0):
    cols = pl.ds(col0, HEAD_DIM)
    rows_s = lax.broadcasted_iota(jnp.int32, (SUB, 1), 0)
    lanes_c = lax.broadcasted_iota(jnp.int32, (1, CHUNK), 1)
    blocks = []
    for i in range(N_SUB):
        lo = i * SUB
        qi = q_ref[lo:lo + SUB, cols]
        bi = b2_ref[lo:lo + SUB, cols]
        if i == 0:
            acc = jnp.zeros((SUB, CHUNK), F32)
        else:
            ref_i = b2_ref[lo:lo + 1, cols]
            qt = qi * jnp.exp2(bi - ref_i)
            kt = jnp.concatenate([k_ref[0:lo, cols] * jnp.exp2(ref_i - b2_ref[0:lo, cols]),
                                  jnp.zeros((CHUNK - lo, HEAD_DIM), F32)], axis=0)
            acc = _dot_nt(qt, kt)
        for s in range(SUB):
            b_s = b2_ref[lo + s:lo + s + 1, cols]
            k_s = k_ref[lo + s:lo + s + 1, cols]
            w = jnp.exp2(jnp.minimum(bi - b_s, 0.0))
            col = jnp.sum((qi * w) * k_s, axis=-1, keepdims=True)
            acc = jnp.where(lanes_c == lo + s, col, acc)
        blocks.append(jnp.where(lanes_c <= lo + rows_s, acc, 0.0))
    return jnp.concatenate(blocks, axis=0)


def _hgrn_forward(p, lb_logits, w_rest):
    n_rows = p.shape[1]
    n_chunks = n_rows // CHUNK
    width = HEADS * HEAD_DIM

    def body(q_ref, fz_ref, v_ref, lb_ref, rest_ref, o_ref, st_out_ref, a_out_ref, rest_out,
             state, k_vmem, b2_vmem, send_sems, recv_sems, local_sem):
        n = pl.program_id(0)
        own, sends, arrivals = _direct_gather(rest_ref, rest_out, send_sems, recv_sems, local_sem)

        @pl.when(n == 0)
        def _():
            state[...] = jnp.zeros_like(state)
            own.start()
            for cp in sends:
                cp.start()

        rows = n * CHUNK + lax.broadcasted_iota(jnp.int32, (CHUNK, 1), 0)
        valid = rows >= PAD_ROWS
        lb = _lower_bound(lb_ref)
        _, _, _, g2, k = _chunk_gates(fz_ref[0], lb, valid)
        k_vmem[...] = k
        b2_vmem[...] = _tri_dot(_tri(CHUNK), g2)
        q_view = q_ref.at[0]
        for h in range(HEADS):
            cols = pl.ds(h * HEAD_DIM, HEAD_DIM)
            st = state[h]
            st_out_ref[0, h] = st
            bh = b2_vmem[:, cols]
            kh = k_vmem[:, cols]
            vh = jnp.where(valid, v_ref[0, :, cols], 0.0)
            qe = q_ref[0, :, cols] * jnp.exp2(bh)
            a = _intra_scores(q_view, k_vmem, b2_vmem, h * HEAD_DIM).astype(BF16)
            a_out_ref[0, h] = a
            o_ref[:, cols] = _dot_nt(qe, st) + _dot(a, vh)
            b_last = b2_vmem[CHUNK - 1:CHUNK, cols]
            kd = kh * jnp.exp2(b_last - bh)
            state[h] = st * jnp.exp2(b_last) + _dot_tn(vh, kd)

        @pl.when(n == n_chunks - 1)
        def _():
            for cp in arrivals:
                cp.wait_recv()
            for cp in sends:
                cp.wait_send()
            own.wait()

    blk = lambda c: pl.BlockSpec((1, CHUNK, width), lambda n, c=c: (c, n, 0))
    return pl.pallas_call(
        body, name="hgrn_forward",
        out_shape=(jax.ShapeDtypeStruct((n_rows, width), F32),
                   jax.ShapeDtypeStruct((n_chunks, HEADS, HEAD_DIM, HEAD_DIM), F32),
                   jax.ShapeDtypeStruct((n_chunks, HEADS, CHUNK, CHUNK), BF16),
                   jax.ShapeDtypeStruct((N_DEV,) + w_rest.shape, w_rest.dtype)),
        grid=(n_chunks,),
        in_specs=[blk(0), blk(1), blk(2), pl.BlockSpec((2, width), lambda n: (0, 0)), ANY],
        out_specs=(pl.BlockSpec((CHUNK, width), lambda n: (n, 0)),
                   pl.BlockSpec((1, HEADS, HEAD_DIM, HEAD_DIM), lambda n: (n, 0, 0, 0)),
                   pl.BlockSpec((1, HEADS, CHUNK, CHUNK), lambda n: (n, 0, 0, 0)), ANY),
        scratch_shapes=[pltpu.VMEM((HEADS, HEAD_DIM, HEAD_DIM), F32), pltpu.VMEM((CHUNK, width), F32),
                        pltpu.VMEM((CHUNK, width), F32)] + list(GATHER_SEMS),
        compiler_params=_params(("arbitrary",)),
    )(p, p, p, lb_logits, w_rest)


def _hgrn_backward(p, lb_logits, states, scores, d_o, g_slabs, first_owner, g_rows):
    n_rows = p.shape[1]
    n_chunks = n_rows // CHUNK
    width = HEADS * HEAD_DIM

    def body(q_ref, fz_ref, v_ref, lb_ref, st_ref, a_ref, do_ref, gs_hbm, gr_hbm, dp_ref, dbias_ref, dlb_ref, rs_hbm, rr_hbm,
             dstate, k_vmem, b2_vmem, *sems):
        step = pl.program_id(0)
        n = n_chunks - 1 - step
        start_slabs, finish_slabs = _scatter_slabs(gs_hbm, first_owner, rs_hbm, *sems[:3])
        start_rows, finish_rows = _scatter_rows(gr_hbm, rr_hbm, *sems[3:])

        @pl.when(step == 0)
        def _():
            dstate[...] = jnp.zeros_like(dstate)
            dbias_ref[...] = jnp.zeros_like(dbias_ref)
            dlb_ref[...] = jnp.zeros_like(dlb_ref)
            start_slabs()
            start_rows()

        rows = n * CHUNK + lax.broadcasted_iota(jnp.int32, (CHUNK, 1), 0)
        valid = rows >= PAD_ROWS
        lb = _lower_bound(lb_ref)
        sig, nsig, f, g2, k = _chunk_gates(fz_ref[0], lb, valid)
        k_vmem[...] = k
        b2_vmem[...] = _tri_dot(_tri(CHUNK), g2)
        rows_c = lax.broadcasted_iota(jnp.int32, (CHUNK, 1), 0)
        rows_s = lax.broadcasted_iota(jnp.int32, (SUB, 1), 0)
        lanes_c = lax.broadcasted_iota(jnp.int32, (1, CHUNK), 1)
        causal = lax.broadcasted_iota(jnp.int32, (CHUNK, CHUNK), 0) >= lax.broadcasted_iota(jnp.int32, (CHUNK, CHUNK), 1)
        tri_up = _tri(CHUNK, upper=True)
        for h in range(HEADS):
            cols = pl.ds(h * HEAD_DIM, HEAD_DIM)
            st = st_ref[0, h]
            dst = dstate[h]
            qh = q_ref[0, :, cols]
            bh = b2_vmem[:, cols]
            kh = k_vmem[:, cols]
            vh = jnp.where(valid, v_ref[0, :, cols], 0.0)
            doh = do_ref[:, cols]
            eb = jnp.exp2(bh)
            qe = qh * eb
            b_last = b2_vmem[CHUNK - 1:CHUNK, cols]
            e_last = jnp.exp2(b_last)
            decay_k = jnp.exp2(b_last - bh)
            kd = kh * decay_k
            dqe = _dot(doh, st)
            da = jnp.where(causal, _dot_nt(doh, vh), 0.0)
            dv = _dot_tn(a_ref[0, h], doh) + _dot_nt(kd, dst)
            dkd = _dot(vh, dst)
            dstate[h] = dst * e_last + _dot_tn(doh, qe)
            db_last = (jnp.sum(dst * st, axis=0, keepdims=True) * e_last
                       + jnp.sum(dkd * kd, axis=0, keepdims=True))
            dq_blocks, dk_blocks = [], []
            dk_earlier = jnp.zeros((CHUNK, HEAD_DIM), F32)
            for i in range(N_SUB):
                lo = i * SUB
                qi = q_ref[0, lo:lo + SUB, cols]
                bi = b2_vmem[lo:lo + SUB, cols]
                da_i = da[lo:lo + SUB, :]
                if i == 0:
                    dq_i = jnp.zeros((SUB, HEAD_DIM), F32)
                else:
                    ref_i = b2_vmem[lo:lo + 1, cols]
                    eq = jnp.exp2(bi - ref_i)
                    ek = jnp.exp2(ref_i - b2_vmem[0:lo, cols])
                    later = jnp.zeros((CHUNK - lo, HEAD_DIM), F32)
                    kt = jnp.concatenate([k_vmem[0:lo, cols] * ek, later], axis=0)
                    dq_i = _dot(da_i, kt) * eq
                    dk_earlier = dk_earlier + jnp.concatenate([_dot_tn(da_i, qi * eq)[0:lo] * ek, later], axis=0)
                dk_i = jnp.zeros((SUB, HEAD_DIM), F32)
                for s in range(SUB):
                    b_s = b2_vmem[lo + s:lo + s + 1, cols]
                    k_s = k_vmem[lo + s:lo + s + 1, cols]
                    w = jnp.exp2(jnp.minimum(bi - b_s, 0.0))
                    da_col = jnp.sum(jnp.where(lanes_c == lo + s, da_i, 0.0), axis=-1, keepdims=True)
                    gw = da_col * w
                    dq_i = dq_i + gw * k_s
                    dk_i = jnp.where(rows_s == s, jnp.sum(gw * qi, axis=0, keepdims=True), dk_i)
                dq_blocks.append(dq_i)
                dk_blocks.append(dk_i)
            dq_intra = jnp.concatenate(dq_blocks, axis=0)
            dk_intra = jnp.concatenate(dk_blocks, axis=0) + dk_earlier
            dq = dqe * eb + dq_intra
            dk = dkd * decay_k + dk_intra
            db = dqe * qe - dkd * kd + qh * dq_intra - kh * dk_intra
            db = db + jnp.where(rows_c == CHUNK - 1, db_last, 0.0)
            dg = _tri_dot(tri_up, db)
            fh = f[:, h * HEAD_DIM:(h + 1) * HEAD_DIM]
            sh = sig[:, h * HEAD_DIM:(h + 1) * HEAD_DIM]
            nh = nsig[:, h * HEAD_DIM:(h + 1) * HEAD_DIM]
            lbh = lb[:, h * HEAD_DIM:(h + 1) * HEAD_DIM]
            df = jnp.where(valid, dg / fh - dk, 0.0)
            dfz = df * (1.0 - lbh) * sh * nh
            dq = jnp.where(valid, dq, 0.0)
            dv = jnp.where(valid, dv, 0.0)
            dlb_ref[:, cols] += jnp.sum(df * nh, axis=0, keepdims=True)
            dp_ref[0, :, cols] = dq.astype(BF16)
            dp_ref[1, :, cols] = dfz.astype(BF16)
            dp_ref[2, :, cols] = dv.astype(BF16)
            dbias_ref[0, :, cols] += jnp.sum(dq, axis=0, keepdims=True)
            dbias_ref[1, :, cols] += jnp.sum(dfz, axis=0, keepdims=True)
            dbias_ref[2, :, cols] += jnp.sum(dv, axis=0, keepdims=True)

        @pl.when(step == n_chunks - 1)
        def _():
            finish_slabs()
            finish_rows()

    rev = lambda s: n_chunks - 1 - s
    blk = lambda c: pl.BlockSpec((1, CHUNK, width), lambda s, c=c: (c, rev(s), 0))
    return pl.pallas_call(
        body, name="hgrn_backward",
        out_shape=(jax.ShapeDtypeStruct((3, n_rows, width), BF16),
                   jax.ShapeDtypeStruct((3, 1, width), F32),
                   jax.ShapeDtypeStruct((1, width), F32),
                   jax.ShapeDtypeStruct((N_DEV,) + g_slabs.shape[1:], g_slabs.dtype),
                   jax.ShapeDtypeStruct((N_DEV, N_ROW_GRADS, 128, g_rows.shape[2]), g_rows.dtype)),
        grid=(n_chunks,),
        in_specs=[blk(0), blk(1), blk(2), pl.BlockSpec((2, width), lambda s: (0, 0)),
                  pl.BlockSpec((1, HEADS, HEAD_DIM, HEAD_DIM), lambda s: (rev(s), 0, 0, 0)),
                  pl.BlockSpec((1, HEADS, CHUNK, CHUNK), lambda s: (rev(s), 0, 0, 0)),
                  pl.BlockSpec((CHUNK, width), lambda s: (rev(s), 0)), ANY, ANY],
        out_specs=(pl.BlockSpec((3, CHUNK, width), lambda s: (0, rev(s), 0)),
                   pl.BlockSpec((3, 1, width), lambda s: (0, 0, 0)),
                   pl.BlockSpec((1, width), lambda s: (0, 0)), ANY, ANY),
        scratch_shapes=[pltpu.VMEM((HEADS, HEAD_DIM, HEAD_DIM), F32), pltpu.VMEM((CHUNK, width), F32),
                        pltpu.VMEM((CHUNK, width), F32)] + list(SCATTER_SEMS) + list(SCATTER_ROWS_SEMS),
        compiler_params=_params(("arbitrary",)),
    )(p, p, p, lb_logits, states, scores, d_o, g_slabs, g_rows)


def _sigmoid_and_complement(x):
    s = 0.5 * jnp.tanh(0.5 * x) + 0.5
    return s, 1.0 - s


def _window_sums(band, rows):
    hi = rows.astype(BF16)
    lo = (rows - hi.astype(F32)).astype(BF16)
    return jnp.dot(band, hi, preferred_element_type=F32) + jnp.dot(band, lo, preferred_element_type=F32)


def _pool_bands(tr, k_pad):
    offset = jnp.arange(k_pad)[None, :] - jnp.arange(tr)[:, None]
    forward = jnp.stack([(offset > HALO - w) & (offset <= HALO) for w in POOL_WINDOWS])
    transposed = jnp.stack([(offset >= 0) & (offset < w) for w in POOL_WINDOWS])
    return forward.astype(BF16), transposed.astype(BF16)


def _silu_and_grad(x):
    s, ns = _sigmoid_and_complement(x)
    return x * s, s * (1.0 + x * ns)


def _tail(p, o, zp, tgt, hg_norm_w, pool_w, pool_scale, w_down_hg, w_down_pool, w_out, final_norm_w):
    n_rows = zp.shape[0]
    tr = _row_tile(n_rows, 208)
    nt = n_rows // tr
    ext = tr + HALO
    k_pad = -(-ext // 128) * 128
    n_groups = len(POOL_WINDOWS)
    band, band_t = _pool_bands(tr, k_pad)

    def body(o_ref, ghg_ref, u_ref, gpool_ref, mhg_ref, mpool_ref, uhalo_ref, z_ref, tgt_hbm,
             hgw_ref, pw_ref, ps_ref, wdh_ref, wdp_ref, wout_ref, fnw_ref, band_ref, band_t_ref,
             do_ref, dp_ref, dz2_ref, lhs_ref, rhs_ref,
             dbias_ref, dhgw_ref, dpw_ref, dps_ref, dfnw_ref, loss_ref, halo_vmem, tgt_buf, tgt_sems):
        step = pl.program_id(0)
        ti = nt - 1 - step

        def target_rows(tile, slot, act):
            @pl.when(tile == 0)
            def _():
                cp = pltpu.make_async_copy(tgt_hbm.at[pl.ds(0, tr - CHUNK), :], tgt_buf.at[slot, pl.ds(CHUNK, tr - CHUNK), :],
                                           tgt_sems.at[slot])
                getattr(cp, act)()

            @pl.when(tile > 0)
            def _():
                cp = pltpu.make_async_copy(tgt_hbm.at[pl.ds(tile * tr - CHUNK, tr), :], tgt_buf.at[slot], tgt_sems.at[slot])
                getattr(cp, act)()

        @pl.when(step == 0)
        def _():
            halo_vmem[...] = jnp.zeros_like(halo_vmem)
            for r in (dbias_ref, dhgw_ref, dpw_ref, dps_ref, dfnw_ref, loss_ref):
                r[...] = jnp.zeros_like(r)
            if nt <= 2:
                tgt_buf[(nt - 1) % 2, 0:CHUNK, :] = jnp.zeros((CHUNK, D_MODEL), F32)
            target_rows(ti, 0, "start")

        @pl.when(ti > 0)
        def _():
            target_rows(ti - 1, (step + 1) % 2, "start")

        rows = ti * tr + lax.broadcasted_iota(jnp.int32, (tr, 1), 0)
        valid = rows >= PAD_ROWS
        in_loss = rows >= CHUNK
        count_pos = jnp.maximum(rows - PAD_ROWS + 1, 1).astype(F32)

        o = o_ref[...]
        hgw = hgw_ref[...]
        inv_o, on_parts = [], []
        for h in range(HEADS):
            oh = o[:, h * HEAD_DIM:(h + 1) * HEAD_DIM]
            r = lax.rsqrt(jnp.mean(oh * oh, axis=-1, keepdims=True) + EPS)
            inv_o.append(r)
            on_parts.append(oh * r)
        o_hat = jnp.concatenate(on_parts, axis=1)
        o_n = o_hat * hgw
        g_hg = ghg_ref[0]
        silu_hg, dsilu_hg = _silu_and_grad(g_hg)
        a_hg = o_n * silu_hg
        y_hg = _dot(a_hg, wdh_ref[...])

        u = jnp.where(valid, u_ref[0], 0.0)
        u_prev = jnp.where(ti > 0, uhalo_ref[0], 0.0)
        u_ext = jnp.concatenate([u_prev, u, jnp.zeros((k_pad - ext, D_MODEL), F32)], axis=0)
        pooled_parts, mixed_parts, inv_cnt = [], [], []
        for gi, win in enumerate(POOL_WINDOWS):
            lanes = slice(gi * POOL_GDIM, (gi + 1) * POOL_GDIM)
            ic = 1.0 / jnp.minimum(count_pos, float(win))
            inv_cnt.append(ic)
            pooled = _window_sums(band_ref[gi], u_ext[:, lanes]) * ic - u[:, lanes]
            pooled_parts.append(pooled)
            mixed_parts.append(_dot(pooled, pw_ref[gi]))
        mixed = jnp.concatenate(mixed_parts, axis=1)
        ps = ps_ref[...]
        g_pool = gpool_ref[0]
        silu_pool, dsilu_pool = _silu_and_grad(g_pool)
        a_pool = mixed * ps * silu_pool
        y_pool = _dot(a_pool, wdp_ref[...])

        m_hg, m_pool = mhg_ref[0], mpool_ref[0]
        s_hg, ns_hg = _sigmoid_and_complement(m_hg)
        s_pool, ns_pool = _sigmoid_and_complement(m_pool)
        merged = s_hg * y_hg + s_pool * y_pool
        z2 = z_ref[...] + _dot(merged, wout_ref[...])
        r2 = lax.rsqrt(jnp.mean(z2 * z2, axis=-1, keepdims=True) + EPS)
        n2 = z2 * r2
        fnw = fnw_ref[...]
        target_rows(ti, step % 2, "wait")
        err = jnp.where(in_loss, n2 * fnw - tgt_buf[step % 2], 0.0)
        loss_ref[...] += jnp.sum(jnp.sum(err * err, axis=0, keepdims=True), axis=1, keepdims=True) * (0.5 / D_MODEL)
        dy = err * (1.0 / D_MODEL)

        dfnw_ref[...] += jnp.sum(dy * n2, axis=0, keepdims=True)
        gy = dy * fnw
        dz2 = r2 * (gy - n2 * jnp.mean(gy * n2, axis=-1, keepdims=True))
        dmerged = _dot_nt(dz2, wout_ref[...])
        dy_hg = s_hg * dmerged
        dy_pool = s_pool * dmerged
        dm_hg = dmerged * y_hg * s_hg * ns_hg
        dm_pool = dmerged * y_pool * s_pool * ns_pool
        da_hg = _dot_nt(dy_hg, wdh_ref[...])
        da_pool = _dot_nt(dy_pool, wdp_ref[...])

        d_on = da_hg * silu_hg
        dg_hg = da_hg * o_n * dsilu_hg
        dhgw_ref[...] += jnp.sum(d_on * o_hat, axis=0, keepdims=True)
        gyo = d_on * hgw
        do_parts = []
        for h in range(HEADS):
            lanes = slice(h * HEAD_DIM, (h + 1) * HEAD_DIM)
            gh, nh = gyo[:, lanes], o_hat[:, lanes]
            do_parts.append(inv_o[h] * (gh - nh * jnp.mean(gh * nh, axis=-1, keepdims=True)))
        do_ref[...] = jnp.concatenate(do_parts, axis=1)

        dmixed = da_pool * ps * silu_pool
        dps_ref[...] += jnp.sum(da_pool * mixed * silu_pool, axis=0, keepdims=True)
        dg_pool = da_pool * mixed * ps * dsilu_pool
        du_parts = []
        for gi, win in enumerate(POOL_WINDOWS):
            lanes = slice(gi * POOL_GDIM, (gi + 1) * POOL_GDIM)
            dmx = dmixed[:, lanes]
            dpooled = _dot_nt(dmx, pw_ref[gi])
            dpw_ref[gi] += _dot_tn(pooled_parts[gi], dmx)
            dpt = dpooled * inv_cnt[gi]
            following = jnp.concatenate([dpt, halo_vmem[:, lanes], jnp.zeros((k_pad - ext, POOL_GDIM), F32)], axis=0)
            du_parts.append(_window_sums(band_t_ref[gi], following) - dpooled)
            halo_vmem[:, lanes] = dpt[:HALO]
        du = jnp.where(valid, jnp.concatenate(du_parts, axis=1), 0.0)

        for c, val in enumerate((dg_hg, du, dg_pool, dm_hg, dm_pool)):
            dp_ref[c] = val.astype(BF16)
            dbias_ref[c] += jnp.sum(val, axis=0, keepdims=True)
        dz2_ref[...] = dz2
        for c, (lhs, rhs) in enumerate(((merged, dz2), (a_hg, dy_hg), (a_pool, dy_pool))):
            lhs_ref[c] = lhs.astype(BF16)
            rhs_ref[c] = rhs.astype(BF16)

    rev = lambda s: nt - 1 - s
    rowblk = pl.BlockSpec((tr, D_MODEL), lambda s: (rev(s), 0))
    pblk = lambda c: pl.BlockSpec((1, tr, 1024), lambda s, c=c: (c, rev(s), 0))
    halo_blk = pl.BlockSpec((1, HALO, 1024), lambda s: (4, jnp.maximum(rev(s) * (tr // HALO) - 1, 0), 0))
    full = lambda shape: pl.BlockSpec(shape, lambda s: (0,) * len(shape))
    vec = full((1, D_MODEL))
    mat = full((D_MODEL, D_MODEL))
    act3 = jax.ShapeDtypeStruct((3, n_rows, D_MODEL), BF16)
    act3_blk = pl.BlockSpec((3, tr, D_MODEL), lambda s: (0, rev(s), 0))
    return pl.pallas_call(
        body, name="tail_forward_backward",
        out_shape=(jax.ShapeDtypeStruct((n_rows, D_MODEL), F32),
                   jax.ShapeDtypeStruct((5, n_rows, 1024), BF16),
                   jax.ShapeDtypeStruct((n_rows, D_MODEL), F32),
                   act3, act3,
                   jax.ShapeDtypeStruct((5, 1, 1024), F32),
                   jax.ShapeDtypeStruct((1, D_MODEL), F32),
                   jax.ShapeDtypeStruct((n_groups, POOL_GDIM, POOL_GDIM), F32),
                   jax.ShapeDtypeStruct((1, D_MODEL), F32),
                   jax.ShapeDtypeStruct((1, D_MODEL), F32),
                   jax.ShapeDtypeStruct((1, 1), F32)),
        grid=(nt,),
        in_specs=[rowblk, pblk(3), pblk(4), pblk(5), pblk(6), pblk(7), halo_blk, rowblk, ANY,
                  vec, full((n_groups, POOL_GDIM, POOL_GDIM)), vec, mat, mat, mat, vec, full(band.shape), full(band_t.shape)],
        out_specs=(rowblk, pl.BlockSpec((5, tr, 1024), lambda s: (0, rev(s), 0)), rowblk,
                   act3_blk, act3_blk,
                   full((5, 1, 1024)), vec, full((n_groups, POOL_GDIM, POOL_GDIM)), vec, vec, full((1, 1))),
        scratch_shapes=[pltpu.VMEM((HALO, D_MODEL), F32), pltpu.VMEM((2, tr, D_MODEL), F32), pltpu.SemaphoreType.DMA((2,))],
        compiler_params=_params(("arbitrary",)),
    )(o, p, p, p, p, p, p, zp, tgt, hg_norm_w, pool_w, pool_scale, w_down_hg, w_down_pool, w_out, final_norm_w, band, band_t)


def _in_projection_backward(dp_a, dp_b, w_blocks, zp, dz2, norm_w, chip_sums, misc):
    n_rows = zp.shape[0]
    tr = _row_tile(n_rows, 416)
    nt = n_rows // tr
    na, nb = dp_a.shape[0], dp_b.shape[0]

    def body(dpa_ref, dpb_ref, w_hbm, z_ref, dz2_ref, nw_ref, gs_hbm, misc_hbm, gx_hbm, head_ref, dnw_ref, rs_hbm,
             rm_hbm, rmeta_hbm, rnw_hbm, w_vmem, sem, dz_buf, gx_sems, late, low_send, low_recv, low_local,
             m_send, m_recv, m_local, late_send, late_recv, late_local):
        i = pl.program_id(0)
        start_slabs, finish_slabs = _scatter_low(gs_hbm, rs_hbm, low_send, low_recv, low_local)
        start_misc, finish_misc = _scatter_slabs(misc_hbm, 0, rm_hbm, m_send, m_recv, m_local)
        me = _me()

        def late_copies():
            meta_cols = lambda dev: late.at[pl.ds(0, N_META), pl.ds(dev * 128, 128)]
            nw_rows = late.at[pl.ds(N_META, 8), :]
            local = [pltpu.make_async_copy(meta_cols(me), rmeta_hbm.at[me], late_local.at[0]),
                     pltpu.make_async_copy(nw_rows, rnw_hbm.at[me], late_local.at[1])]
            sends, arrivals = [], []
            for kk in range(1, N_DEV):
                peer = jnp.bitwise_xor(me, kk)
                for n, (src, dst, own_src) in enumerate(((meta_cols(peer), rmeta_hbm, meta_cols(me)),
                                                        (nw_rows, rnw_hbm, nw_rows))):
                    sems = (late_send.at[2 * (kk - 1) + n], late_recv.at[2 * (kk - 1) + n])
                    sends.append(_remote(src, dst.at[me], *sems, kk))
                    arrivals.append(_remote(own_src, dst.at[peer], *sems, kk))
            return local, sends, arrivals

        def wait_rows_out(tile):
            @pl.when(tile == 0)
            def _():
                pltpu.make_async_copy(dz_buf.at[0, pl.ds(CHUNK, tr - CHUNK), :], gx_hbm.at[pl.ds(0, tr - CHUNK), :],
                                      gx_sems.at[0]).wait()

            @pl.when(tile > 0)
            def _():
                pltpu.make_async_copy(dz_buf.at[tile % 2], gx_hbm.at[pl.ds(tile * tr - CHUNK, tr), :],
                                      gx_sems.at[tile % 2]).wait()

        @pl.when(i == 0)
        def _():
            start_slabs()
            start_misc()
            cp = pltpu.make_async_copy(w_hbm, w_vmem, sem)
            cp.start()
            cp.wait()
            dnw_ref[...] = jnp.zeros_like(dnw_ref)

        dh = jnp.zeros((tr, D_MODEL), F32)
        half_cols = w_vmem.shape[-1]
        for j in range(na + nb):
            dp_ref, jj = (dpa_ref, j) if j < na else (dpb_ref, j - na)
            for half in range(2):
                dh = dh + _dot_nt(dp_ref[jj, :, half * half_cols:(half + 1) * half_cols], w_vmem[j, half])
        z = z_ref[...]
        r = lax.rsqrt(jnp.mean(z * z, axis=-1, keepdims=True) + EPS)
        n1 = z * r
        dnw_ref[...] += jnp.sum(dh * n1, axis=0, keepdims=True)
        gh = dh * nw_ref[...]
        dz = dz2_ref[...] + r * (gh - n1 * jnp.mean(gh * n1, axis=-1, keepdims=True))

        @pl.when(i >= 2)
        def _():
            wait_rows_out(i - 2)

        dz_buf[i % 2] = dz

        @pl.when(i == 0)
        def _():
            head_ref[...] = dz[0:CHUNK]
            late[0:N_META, :] = dz[PAD_ROWS:CHUNK]
            pltpu.make_async_copy(dz_buf.at[0, pl.ds(CHUNK, tr - CHUNK), :], gx_hbm.at[pl.ds(0, tr - CHUNK), :],
                                  gx_sems.at[0]).start()

        @pl.when(i > 0)
        def _():
            pltpu.make_async_copy(dz_buf.at[i % 2], gx_hbm.at[pl.ds(i * tr - CHUNK, tr), :], gx_sems.at[i % 2]).start()

        @pl.when(i == nt - 1)
        def _():
            late[N_META:N_META + 8, :] = jnp.broadcast_to(dnw_ref[...], (8, D_MODEL))
            local, sends, arrivals = late_copies()
            for cp in local + sends:
                cp.start()
            if nt >= 2:
                wait_rows_out(i - 1)
            wait_rows_out(i)
            finish_slabs()
            finish_misc()
            for cp in arrivals:
                cp.wait_recv()
            for cp in sends:
                cp.wait_send()
            for cp in local:
                cp.wait()

    rowblk = pl.BlockSpec((tr, D_MODEL), lambda i: (i, 0))
    vec = pl.BlockSpec((1, D_MODEL), lambda i: (0, 0))
    dma = pltpu.SemaphoreType.DMA
    return pl.pallas_call(
        body, name="in_projection_backward",
        out_shape=(jax.ShapeDtypeStruct((n_rows - CHUNK, D_MODEL), F32), jax.ShapeDtypeStruct((CHUNK, D_MODEL), F32),
                   jax.ShapeDtypeStruct((1, D_MODEL), F32),
                   jax.ShapeDtypeStruct((4,) + chip_sums.shape[1:], chip_sums.dtype),
                   jax.ShapeDtypeStruct(misc.shape, misc.dtype),
                   jax.ShapeDtypeStruct((N_DEV, N_META, 128), F32), jax.ShapeDtypeStruct((N_DEV, 8, D_MODEL), F32)),
        grid=(nt,),
        in_specs=[pl.BlockSpec((na, tr, 1024), lambda i: (0, i, 0)), pl.BlockSpec((nb, tr, 1024), lambda i: (0, i, 0)),
                  ANY, rowblk, rowblk, vec, ANY, ANY],
        out_specs=(ANY, pl.BlockSpec((CHUNK, D_MODEL), lambda i: (0, 0)), vec, ANY, ANY, ANY, ANY),
        scratch_shapes=[pltpu.VMEM(w_blocks.shape, w_blocks.dtype), dma(()),
                        pltpu.VMEM((2, tr, D_MODEL), F32), dma((2,)), pltpu.VMEM((N_META + 8, D_MODEL), F32),
                        dma((2,)), dma((3,)), dma(())] + list(SCATTER_SEMS) + [dma((14,)), dma((14,)), dma((2,))],
        compiler_params=_params(("arbitrary",)),
    )(dp_a, dp_b, w_blocks, zp, dz2, norm_w, chip_sums, misc)


def _weight_grad(xs, ys, name):
    shared = xs.ndim == 2
    n_rows, m = xs.shape[-2:]
    nb, _, n = ys.shape
    tk = _row_tile(n_rows, 4160)
    n_k = n_rows // tk

    def body(x_ref, y_ref, o_ref, acc):
        k = pl.program_id(1)

        @pl.when(k == 0)
        def _():
            acc[...] = jnp.zeros_like(acc)

        acc[...] += _dot_tn(x_ref[...] if shared else x_ref[0], y_ref[0])

        @pl.when(k == n_k - 1)
        def _():
            o_ref[0] = acc[...].astype(o_ref.dtype)

    x_spec = pl.BlockSpec((tk, m), lambda j, k: (k, 0)) if shared else pl.BlockSpec((1, tk, m), lambda j, k: (j, k, 0))
    return pl.pallas_call(
        body, name=name,
        out_shape=jax.ShapeDtypeStruct((nb, m, n), BF16),
        grid=(nb, n_k),
        in_specs=[x_spec, pl.BlockSpec((1, tk, n), lambda j, k: (j, k, 0))],
        out_specs=pl.BlockSpec((1, m, n), lambda j, k: (j, 0, 0)),
        scratch_shapes=[pltpu.VMEM((m, n), F32)],
        compiler_params=_params(("arbitrary", "arbitrary")),
    )(xs, ys)


def kernel(x, meta_tokens, norm_w, w_in, b_in, lb_logits, hg_norm_w, pool_w, pool_scale, w_down_hg, w_down_pool, w_out, final_norm_w, loss_target, m_meta_tokens, m_norm_w, m_w_in, m_b_in, m_lb_logits, m_hg_norm_w, m_pool_w, m_pool_scale, m_w_down_hg, m_w_down_pool, m_w_out, m_final_norm_w, v_meta_tokens, v_norm_w, v_w_in, v_b_in, v_lb_logits, v_hg_norm_w, v_pool_w, v_pool_scale, v_w_down_hg, v_w_down_pool, v_w_out, v_final_norm_w):
    seq = x.shape[1]

    meta_full = _all_gather_small(meta_tokens).transpose(1, 0, 2).reshape(N_META, D_MODEL)
    w_rest = jnp.concatenate([w_down_hg[0].astype(BF16), w_down_pool[0].astype(BF16), w_out[0].astype(BF16),
                              pool_w[0].astype(BF16).reshape(32, 1024)], axis=0)

    zp = jnp.concatenate([jnp.zeros((PAD_ROWS, D_MODEL), F32), meta_full, x[0]], axis=0)
    p, h, w_blocks = _in_projection(zp, norm_w, w_in[0].astype(BF16), b_in.reshape(N_COLBLK, 1, 1024), _gather_units())
    o, states, scores, rest = _hgrn_forward(p, lb_logits, w_rest)
    wdh = rest[:, REST_W_DOWN_HG:REST_W_DOWN_HG + 128].reshape(1024, 1024)
    wdp = rest[:, REST_W_DOWN_POOL:REST_W_DOWN_POOL + 128].reshape(1024, 1024)
    wout = rest[:, REST_W_OUT:REST_W_OUT + 128].reshape(1024, 1024)
    pw = rest[:, REST_POOL_W:REST_POOL_W + 32].reshape(N_DEV, 4, 32, 256).transpose(1, 0, 2, 3).reshape(4, 256, 256)
    (d_o, dp_b, dz2, grad_lhs, grad_rhs, dbias_b, d_hgw, d_pw, d_ps, d_fnw, loss_part) = _tail(
        p, o, zp, loss_target[0], hg_norm_w, pw, pool_scale, wdh, wdp, wout, final_norm_w.reshape(1, D_MODEL))
    n_a = N_COLBLK - dp_b.shape[0]
    g_hi = _weight_grad(h, dp_b, "weight_grad_in_hi")
    g_rows = _weight_grad(grad_lhs, grad_rhs, "weight_grad_rows")
    dp_a, dbias_a, d_lb, recv_hi, recv_rows = _hgrn_backward(p, lb_logits, states, scores, d_o, g_hi, n_a, g_rows)
    assert n_a == LOW_OWNERS
    chip_lo = _weight_grad_low(h, dp_a)

    lb = jax.nn.sigmoid(lb_logits[0:1] - lb_logits[1:2])
    d_l0 = d_lb * lb * (1.0 - lb)
    replicated = jnp.concatenate([dbias_a.reshape(3, 1024), dbias_b.reshape(5, 1024), jnp.zeros((1, 1024), F32), d_l0, -d_l0,
                                  d_hgw, d_ps, d_fnw,
                                  jnp.pad(loss_part, ((0, MISC_ROWS - MISC_LOSS - 1), (0, 1023)))], axis=0)
    d_pw_blocks = d_pw.reshape(4, N_DEV, 32, 256).transpose(1, 0, 2, 3).reshape(N_DEV, 32, 1024)
    g_misc = jnp.concatenate([d_pw_blocks, jnp.zeros((N_DEV, N_META, 1024), F32),
                              jnp.broadcast_to(replicated[None], (N_DEV, 16, 1024))], axis=1)
    dz_seq, _, _, recv_lo, recv_misc, recv_meta, recv_nw = _in_projection_backward(
        dp_a, dp_b, w_blocks, zp, dz2, norm_w, chip_lo, g_misc)

    as_rows = lambda t, n: t.reshape(n, 1024)
    small = [(0, MISC_POOL_W, 1024, tuple(as_rows(t, 32) for t in (pool_w, m_pool_w, v_pool_w))),
             (1, 0, 128, (meta_tokens, m_meta_tokens, v_meta_tokens)),
             (0, MISC_B_IN, 1024, tuple(as_rows(t, 8) for t in (b_in, m_b_in, v_b_in))),
             (2, 0, 1024, (norm_w, m_norm_w, v_norm_w)),
             (0, MISC_LB, 1024, (lb_logits, m_lb_logits, v_lb_logits)),
             (0, MISC_HG_NORM_W, 1024, (hg_norm_w, m_hg_norm_w, v_hg_norm_w)),
             (0, MISC_POOL_SCALE, 1024, (pool_scale, m_pool_scale, v_pool_scale)),
             (0, MISC_FINAL_NORM_W, 1024, tuple(as_rows(t, 1) for t in (final_norm_w, m_final_norm_w, v_final_norm_w)))]
    res, loss = _finish(recv_hi, recv_lo, n_a, recv_rows, (recv_misc, recv_meta, recv_nw), (w_in, m_w_in, v_w_in),
                  [(w_out, m_w_out, v_w_out), (w_down_hg, m_w_down_hg, v_w_down_hg), (w_down_pool, m_w_down_pool, v_w_down_pool)],
                  small)
    r_w_in, r_w_out, r_wdh, r_wdp, r_pw, r_meta, r_b_in, r_nw, r_lb, r_hgw, r_ps, r_fnw = res
    grad_x = dz_seq.reshape(1, seq, D_MODEL)
    per_kind = [(r_meta[k], r_nw[k], r_w_in[k], r_b_in[k].reshape(1, 8192), r_lb[k], r_hgw[k], r_pw[k].reshape(1, 4, 32, 256),
                 r_ps[k], r_wdh[k], r_wdp[k], r_w_out[k], r_fnw[k].reshape(1024)) for k in range(4)]
    return (loss, grad_x, *per_kind[0], *per_kind[1], *per_kind[2], *per_kind[3])
```

```python
import functools

import jax
import jax.numpy as jnp
from jax import lax
from jax.experimental import pallas as pl
from jax.experimental.pallas import tpu as pltpu

F32 = jnp.float32
BF16 = jnp.bfloat16

D_MODEL = 1024
N_META = 16
HEADS = 8
HEAD_DIM = 128
CHUNK = 64
SUB = 8
N_SUB = CHUNK // SUB
PAD_ROWS = CHUNK - N_META
POOL_WINDOWS = (2, 4, 8, 16)
POOL_GDIM = D_MODEL // len(POOL_WINDOWS)
HALO = 16
EPS = 1e-6
N_DEV = 8
N_COLBLK = 8
ADAM_LR, ADAM_B1, ADAM_B2, ADAM_EPS, ADAM_WD, ADAM_STEP = 0.001, 0.9, 0.999, 1e-08, 0.01, 10

VMEM_LIMIT = 56 * 1024 * 1024
MESH = pl.DeviceIdType.MESH
ANY = pl.BlockSpec(memory_space=pl.ANY)

REST_W_DOWN_HG = 0
REST_W_DOWN_POOL = 128
REST_W_OUT = 256
REST_POOL_W = 384
MISC_POOL_W = 0
MISC_META = 32
MISC_B_IN = 48
MISC_NORM_W = 56
MISC_LB = 57
MISC_HG_NORM_W = 59
MISC_POOL_SCALE = 60
MISC_FINAL_NORM_W = 61
MISC_LOSS = 62
MISC_ROWS = 64


def _params(sem=None):
    return pltpu.CompilerParams(dimension_semantics=sem, vmem_limit_bytes=VMEM_LIMIT)


def _row_tile(n_rows, prefer):
    best = 16
    for t in range(16, prefer + 1, 16):
        if n_rows % t == 0:
            best = t
    return best


def _sigmoid_pair(x):
    e = jnp.exp(-jnp.abs(x))
    r = 1.0 / (1.0 + e)
    er = e * r
    pos = x >= 0
    return jnp.where(pos, r, er), jnp.where(pos, er, r)


def _dot(a, b):
    return jnp.dot(a.astype(BF16), b.astype(BF16), preferred_element_type=F32)


def _dot_nt(a, b):
    return lax.dot_general(a.astype(BF16), b.astype(BF16), (((1,), (1,)), ((), ())), preferred_element_type=F32)


def _dot_tn(a, b):
    return lax.dot_general(a.astype(BF16), b.astype(BF16), (((0,), (0,)), ((), ())), preferred_element_type=F32)


def _device_index(px, py, pc):
    return 4 * px + 2 * py + pc


def _direct_gather(src_ref, dst_ref, send_sems, recv_sems, local_sem):
    x, y, c = lax.axis_index("x"), lax.axis_index("y"), lax.axis_index("c")
    own = pltpu.make_async_copy(src_ref, dst_ref.at[_device_index(x, y, c)], local_sem)
    sends, arrivals = [], []
    for k in range(1, N_DEV):
        peer = (1 - x if k & 4 else x, 1 - y if k & 2 else y, 1 - c if k & 1 else c)
        for slot, out in ((_device_index(x, y, c), sends), (_device_index(*peer), arrivals)):
            out.append(pltpu.make_async_remote_copy(
                src_ref=src_ref, dst_ref=dst_ref.at[slot], send_sem=send_sems.at[k - 1], recv_sem=recv_sems.at[k - 1],
                device_id=peer, device_id_type=MESH))
    return own, sends, arrivals


GATHER_SEMS = [pltpu.SemaphoreType.DMA((N_DEV - 1,)), pltpu.SemaphoreType.DMA((N_DEV - 1,)), pltpu.SemaphoreType.DMA(())]


def _all_gather_small(block):
    def body(x_ref, out_ref, send_sems, recv_sems, local_sem):
        own, sends, arrivals = _direct_gather(x_ref, out_ref, send_sems, recv_sems, local_sem)
        own.start()
        for cp in sends:
            cp.start()
        for cp in arrivals:
            cp.wait_recv()
        for cp in sends:
            cp.wait_send()
        own.wait()

    return pl.pallas_call(
        body, name="all_gather_meta",
        out_shape=jax.ShapeDtypeStruct((N_DEV,) + block.shape, block.dtype),
        in_specs=[ANY], out_specs=ANY, scratch_shapes=list(GATHER_SEMS),
    )(block)


def _peer(k):
    x, y, c = lax.axis_index("x"), lax.axis_index("y"), lax.axis_index("c")
    return (1 - x if k & 4 else x, 1 - y if k & 2 else y, 1 - c if k & 1 else c)


def _me():
    return _device_index(lax.axis_index("x"), lax.axis_index("y"), lax.axis_index("c"))


def _remote(src, dst, send_sem, recv_sem, peer_bits):
    return pltpu.make_async_remote_copy(src_ref=src, dst_ref=dst, send_sem=send_sem, recv_sem=recv_sem,
                                        device_id=_peer(peer_bits), device_id_type=MESH)


N_ROW_GRADS = 3
SCATTER_SEMS = [pltpu.SemaphoreType.DMA((N_DEV - 1,)), pltpu.SemaphoreType.DMA((N_DEV - 1,)), pltpu.SemaphoreType.DMA(())]
SCATTER_ROWS_SEMS = [pltpu.SemaphoreType.DMA((7 * N_ROW_GRADS,)), pltpu.SemaphoreType.DMA((7 * N_ROW_GRADS,)),
                     pltpu.SemaphoreType.DMA((N_ROW_GRADS,))]


def _scatter_slabs(g_ref, first, recv_ref, send_sems, recv_sems, local_sem):
    n = g_ref.shape[0]
    me = _me()

    def each(on_send, on_local, on_arrival):
        for kk in range(1, N_DEV):
            peer = jnp.bitwise_xor(me, kk)

            @pl.when((peer >= first) & (peer < first + n))
            def _(kk=kk, peer=peer):
                on_send(_remote(g_ref.at[peer - first], recv_ref.at[me], send_sems.at[kk - 1], recv_sems.at[kk - 1], kk))

        @pl.when((me >= first) & (me < first + n))
        def _():
            on_local(pltpu.make_async_copy(g_ref.at[me - first], recv_ref.at[me], local_sem))
            if on_arrival is not None:
                for kk in range(1, N_DEV):
                    on_arrival(_remote(g_ref.at[0], recv_ref.at[jnp.bitwise_xor(me, kk)], send_sems.at[kk - 1],
                                       recv_sems.at[kk - 1], kk))

    start = lambda: each(lambda cp: cp.start(), lambda cp: cp.start(), None)
    finish = lambda: each(lambda cp: cp.wait_send(), lambda cp: cp.wait(), lambda cp: cp.wait_recv())
    return start, finish


def _scatter_rows(g_ref, recv_ref, send_sems, recv_sems, local_sems):
    me = _me()
    rows = lambda m, dev: g_ref.at[m, pl.ds(dev * 128, 128), :]

    def copies():
        local = [pltpu.make_async_copy(rows(m, me), recv_ref.at[me, m], local_sems.at[m]) for m in range(N_ROW_GRADS)]
        sends, arrivals = [], []
        for m in range(N_ROW_GRADS):
            for kk in range(1, N_DEV):
                peer, sems = jnp.bitwise_xor(me, kk), (send_sems.at[7 * m + kk - 1], recv_sems.at[7 * m + kk - 1])
                sends.append(_remote(rows(m, peer), recv_ref.at[me, m], *sems, kk))
                arrivals.append(_remote(rows(m, me), recv_ref.at[peer, m], *sems, kk))
        return local, sends, arrivals

    def start():
        local, sends, _ = copies()
        for cp in local + sends:
            cp.start()

    def finish():
        local, sends, arrivals = copies()
        for cp in arrivals:
            cp.wait_recv()
        for cp in sends:
            cp.wait_send()
        for cp in local:
            cp.wait()

    return start, finish


LOW_OWNERS = 3


def _weight_grad_low(h, dp_a):
    n_rows = h.shape[0]
    tk = _row_tile(n_rows, 2080)
    n_k = n_rows // tk
    order = jnp.where(lax.axis_index("c") == 0, jnp.array([1, 0, 2], jnp.int32), jnp.array([0, 2, 1], jnp.int32))

    def body(order_ref, h_ref, y_ref, out_ref, acc, send_buf, got, send_sems, recv_sems):
        t, k = pl.program_id(0), pl.program_id(1)
        c = lax.axis_index("c")

        @pl.when(k == 0)
        def _():
            acc[...] = jnp.zeros_like(acc)

        acc[...] += _dot_tn(h_ref[...], y_ref[0])

        def to_sibling(slot):
            return _remote(send_buf.at[slot], got.at[slot], send_sems.at[slot], recv_sems.at[slot], 1)

        def send(slot):
            send_buf[slot] = acc[...].astype(send_buf.dtype)
            to_sibling(slot).start()

        def keep(slot):
            to_sibling(slot).wait_recv()
            out_ref[slot] = (acc[...] + got[slot].astype(F32)).astype(out_ref.dtype)

        done = k == n_k - 1
        for core, step, action, slot in ((0, 0, send, 0), (0, 1, keep, 0), (0, 2, keep, 1),
                                         (1, 0, send, 0), (1, 1, send, 1), (1, 2, keep, 0)):
            @pl.when(done & (c == core) & (t == step))
            def _(action=action, slot=slot):
                action(slot)

        @pl.when(done & (t == LOW_OWNERS - 1))
        def _():
            to_sibling(0).wait_send()

            @pl.when(c == 1)
            def _():
                to_sibling(1).wait_send()
                out_ref[1] = jnp.zeros(out_ref.shape[1:], out_ref.dtype)

    pair = (2, h.shape[1], dp_a.shape[2])
    return pl.pallas_call(
        body, name="weight_grad_in_low",
        out_shape=jax.ShapeDtypeStruct(pair, BF16),
        grid_spec=pltpu.PrefetchScalarGridSpec(
            num_scalar_prefetch=1, grid=(LOW_OWNERS, n_k),
            in_specs=[pl.BlockSpec((tk, h.shape[1]), lambda t, k, o: (k, 0)),
                      pl.BlockSpec((1, tk, dp_a.shape[2]), lambda t, k, o: (o[t], k, 0))],
            out_specs=pl.BlockSpec(pair, lambda t, k, o: (0, 0, 0)),
            scratch_shapes=[pltpu.VMEM(pair[1:], F32), pltpu.VMEM(pair, BF16), pltpu.VMEM(pair, BF16),
                            pltpu.SemaphoreType.DMA((2,)), pltpu.SemaphoreType.DMA((2,))]),
        compiler_params=_params(("arbitrary", "arbitrary")),
    )(order, h, dp_a)


def _exchange_small(misc):
    def body(m_ref, out_ref, send_sems, recv_sems, local_sem):
        start, finish = _scatter_slabs(m_ref, 0, out_ref, send_sems, recv_sems, local_sem)
        start()
        finish()

    return pl.pallas_call(
        body, name="exchange_small", out_shape=jax.ShapeDtypeStruct(misc.shape, misc.dtype),
        in_specs=[ANY], out_specs=ANY, scratch_shapes=list(SCATTER_SEMS),
    )(misc)


def _scatter_low(part_ref, recv_ref, send_sems, recv_sems, local_sem):
    x, y, c = lax.axis_index("x"), lax.axis_index("y"), lax.axis_index("c")
    chip = 2 * x + y
    routes = ((0, (0, 0, c), 0, None), (1, (0, 1, 0), 1, 0))

    def each(on_send, on_local, on_arrival):
        for slot, owner, owner_chip, core in routes:
            holds = (c == core) if core is not None else (c >= 0)
            rel = jnp.bitwise_xor(chip, owner_chip)

            @pl.when(holds & (rel != 0))
            def _(slot=slot, owner=owner, rel=rel):
                on_send(pltpu.make_async_remote_copy(
                    src_ref=part_ref.at[slot], dst_ref=recv_ref.at[chip], send_sem=send_sems.at[slot],
                    recv_sem=recv_sems.at[rel - 1], device_id=owner, device_id_type=MESH))

            @pl.when(holds & (rel == 0))
            def _(slot=slot, owner=owner, owner_chip=owner_chip):
                on_local(pltpu.make_async_copy(part_ref.at[slot], recv_ref.at[chip], local_sem))
                if on_arrival is not None:
                    for r in range(1, 4):
                        on_arrival(pltpu.make_async_remote_copy(
                            src_ref=part_ref.at[slot], dst_ref=recv_ref.at[r ^ owner_chip], send_sem=send_sems.at[slot],
                            recv_sem=recv_sems.at[r - 1], device_id=owner, device_id_type=MESH))

    start = lambda: each(lambda cp: cp.start(), lambda cp: cp.start(), None)
    finish = lambda: each(lambda cp: cp.wait_send(), lambda cp: cp.wait(), lambda cp: cp.wait_recv())
    return start, finish


def _adam_update(g, w, m, v):
    mn = ADAM_B1 * m + (1.0 - ADAM_B1) * g
    vn = ADAM_B2 * v + (1.0 - ADAM_B2) * (g * g)
    m_hat = mn / (1.0 - ADAM_B1 ** ADAM_STEP)
    v_hat = vn / (1.0 - ADAM_B2 ** ADAM_STEP)
    return -ADAM_LR * (m_hat / (jnp.sqrt(v_hat) + ADAM_EPS) + ADAM_WD * w), mn, vn


def _device_sum(parts):
    t = [p.astype(F32) for p in parts]
    return ((t[0] + t[1]) + (t[2] + t[3])) + ((t[4] + t[5]) + (t[6] + t[7]))


def _finish(a_hi, a_lo, n_lo, b3, misc, big, rows3, small):
    n_steps = 4
    tb, tr3 = 1024 // n_steps, 128 // n_steps

    def body(*refs):
        it = iter(refs)
        hi_ref, lo_ref, b_ref, m_ref = next(it), next(it), next(it), next(it)
        big_in = [next(it) for _ in range(3)]
        rows_in = [[next(it) for _ in range(3)] for _ in rows3]
        small_in = [[next(it) for _ in range(3)] for _ in small]
        big_out = [next(it) for _ in range(4)]
        rows_out = [[next(it) for _ in range(4)] for _ in rows3]
        small_out = [[next(it) for _ in range(4)] for _ in small]
        loss_ref = next(it)

        def apply(g, ins, outs):
            d, mn, vn = _adam_update(g, ins[0][...], ins[1][...], ins[2][...])
            for r, val in zip(outs, (g, d, mn, vn)):
                r[...] = val

        lo = [lo_ref[s].astype(F32) for s in range(4)]
        g_big = jnp.where(_me() < n_lo, (lo[0] + lo[1]) + (lo[2] + lo[3]), _device_sum([hi_ref[s] for s in range(N_DEV)]))
        apply(g_big[None], big_in, big_out)
        for k in range(len(rows3)):
            apply(_device_sum([b_ref[s, k] for s in range(N_DEV)])[None], rows_in[k], rows_out[k])

        @pl.when(pl.program_id(0) == 0)
        def _():
            loss_ref[...] = _device_sum([m_ref[s, MISC_LOSS:MISC_LOSS + 1, :] for s in range(N_DEV)])
            for (row0, lanes, ins), r_in, r_out in zip(small, small_in, small_out):
                n = ins[0].shape[0]
                apply(_device_sum([m_ref[s, row0:row0 + n, :lanes] for s in range(N_DEV)]), r_in, r_out)

    whole = lambda shape: pl.BlockSpec(shape, lambda i: (0,) * len(shape))
    big_blk = pl.BlockSpec((1, tb, 1024), lambda i: (0, i, 0))
    rows_blk = pl.BlockSpec((1, tr3, 1024), lambda i: (0, i, 0))
    in_specs = [pl.BlockSpec((N_DEV, tb, 1024), lambda i: (0, i, 0)), pl.BlockSpec((4, tb, 1024), lambda i: (0, i, 0)),
                pl.BlockSpec((N_DEV, 3, tr3, 1024), lambda i: (0, 0, i, 0)), whole(misc.shape)]
    in_specs += [big_blk] * 3 + [rows_blk] * (3 * len(rows3))
    out_specs = [big_blk] * 4 + [rows_blk] * (4 * len(rows3))
    out_shape = [jax.ShapeDtypeStruct(big[0].shape, F32)] * 4
    for w, _, _ in rows3:
        out_shape += [jax.ShapeDtypeStruct(w.shape, F32)] * 4
    args = [a_hi, a_lo, b3, misc, *big]
    for t in rows3:
        args += list(t)
    for _, _, t in small:
        in_specs += [whole(t[0].shape)] * 3
        out_specs += [whole(t[0].shape)] * 4
        out_shape += [jax.ShapeDtypeStruct(t[0].shape, F32)] * 4
        args += list(t)
    out_specs.append(whole((1, 1024)))
    out_shape.append(jax.ShapeDtypeStruct((1, 1024), F32))
    outs = pl.pallas_call(
        body, name="reduce_sum_adamw", out_shape=tuple(out_shape), grid=(n_steps,),
        in_specs=in_specs, out_specs=tuple(out_specs),
        compiler_params=_params(("arbitrary",)),
    )(*args)
    return [tuple(outs[4 * k:4 * k + 4]) for k in range(len(outs) // 4)], outs[-1][0, 0]


GATHER_UNITS = ((0, 0), (0, 1), (1, 0), (1, 1)) + tuple((place, 0) for place in range(2, 8)) + tuple(
    (place, 1) for place in range(2, 8))


def _gather_order():
    x, y, c = lax.axis_index("x"), lax.axis_index("y"), lax.axis_index("c")
    chips = [(1 - x, y), (x, 1 - y), (1 - x, 1 - y)]
    order = [_device_index(x, y, c), _device_index(x, y, 1 - c)]
    order += [_device_index(*q, c) for q in chips] + [_device_index(*q, 1 - c) for q in chips]
    return order


def _gather_units():
    order = _gather_order()
    blocks = jnp.stack([order[place] for place, _ in GATHER_UNITS]).astype(jnp.int32)
    return blocks, jnp.array([half for _, half in GATHER_UNITS], jnp.int32)


def _in_projection(zp, norm_w, w_shard, b_blocks, units):
    n_rows = zp.shape[0]
    tr = _row_tile(n_rows, 1040)
    nt = n_rows // tr

    half_cols = 1024 // 2
    n_units = len(GATHER_UNITS)

    def body(blocks_ref, halves_ref, z_ref, nw_ref, w_hbm, b_ref, p_ref, h_ref, w_out, w_vmem, h_all,
             send_sems, recv_sems, local_sem, out_sems):
        s, i = pl.program_id(0), pl.program_id(1)
        x, y, c = lax.axis_index("x"), lax.axis_index("y"), lax.axis_index("c")
        me, sibling = (x, y, c), (x, y, 1 - c)
        chips = [(1 - x, y), (x, 1 - y), (1 - x, 1 - y)]

        def slot(blk, half):
            return w_vmem.at[_device_index(*blk), half]

        def mine(half):
            return w_hbm.at[:, pl.ds(half * half_cols, half_cols)]

        def copy(k, half, blk, to, own_block=False):
            return pltpu.make_async_remote_copy(
                src_ref=mine(half) if own_block else slot(blk, half), dst_ref=slot(blk, half),
                send_sem=send_sems.at[2 * k + half], recv_sem=recv_sems.at[2 * k + half], device_id=to, device_id_type=MESH)

        own = [pltpu.make_async_copy(mine(half), slot(me, half), local_sem.at[half]) for half in range(2)]
        first = [copy(k, half, me, to, own_block=True) for half in range(2)
                 for k, to in enumerate([sibling] + [(*q, c) for q in chips])]
        passed = [[copy(4 + j, half, (*q, c), sibling) for j, q in enumerate(chips)] for half in range(2)]
        sources = [None, sibling] + [(*q, c) for q in chips] + [(*q, 1 - c) for q in chips]
        arrival = lambda place, half: copy(place - 1, half, sources[place], me)

        def keep(unit):
            blk, half = blocks_ref[unit], halves_ref[unit]
            return pltpu.make_async_copy(w_vmem.at[blk, half], w_out.at[blk, half], out_sems.at[unit])

        for unit, (place, half) in enumerate(GATHER_UNITS):
            @pl.when((i == 0) & (s == unit))
            def _(unit=unit, place=place, half=half):
                if unit == 0:
                    for cp in own + first:
                        cp.start()
                if place == 0:
                    own[half].wait()
                else:
                    arrival(place, half).wait_recv()
                    if 2 <= place <= 4:
                        passed[half][place - 2].start()
                keep(unit).start()

        @pl.when(s == 0)
        def _():
            z = z_ref[...]
            r = lax.rsqrt(jnp.mean(z * z, axis=-1, keepdims=True) + EPS)
            h = (z * r * nw_ref[...]).astype(BF16)
            h_all[i] = h
            h_ref[...] = h

        p_ref[0] = jnp.dot(h_all[i], w_vmem[blocks_ref[s], halves_ref[s]], preferred_element_type=F32) + b_ref[0]

        @pl.when((s == n_units - 1) & (i == nt - 1))
        def _():
            for cp in first + passed[0] + passed[1]:
                cp.wait_send()
            for unit in range(n_units):
                keep(unit).wait()

    first_pass = lambda s, i, rest: jnp.where(s == 0, i, rest)
    return pl.pallas_call(
        body, name="in_projection_gather",
        out_shape=(jax.ShapeDtypeStruct((N_COLBLK, n_rows, 1024), F32),
                   jax.ShapeDtypeStruct((n_rows, D_MODEL), BF16),
                   jax.ShapeDtypeStruct((N_DEV, 2, D_MODEL, half_cols), BF16)),
        grid_spec=pltpu.PrefetchScalarGridSpec(
            num_scalar_prefetch=2, grid=(n_units, nt),
            in_specs=[pl.BlockSpec((tr, D_MODEL), lambda s, i, blk, hf: (first_pass(s, i, 0), 0)),
                      pl.BlockSpec((1, D_MODEL), lambda s, i, blk, hf: (0, 0)), ANY,
                      pl.BlockSpec((1, 1, half_cols), lambda s, i, blk, hf: (blk[s], 0, hf[s]))],
            out_specs=(pl.BlockSpec((1, tr, half_cols), lambda s, i, blk, hf: (blk[s], i, hf[s])),
                       pl.BlockSpec((tr, D_MODEL), lambda s, i, blk, hf: (first_pass(s, i, nt - 1), 0)), ANY),
            scratch_shapes=[pltpu.VMEM((N_DEV, 2, D_MODEL, half_cols), BF16), pltpu.VMEM((nt, tr, D_MODEL), BF16),
                            pltpu.SemaphoreType.DMA((14,)), pltpu.SemaphoreType.DMA((14,)), pltpu.SemaphoreType.DMA((2,)),
                            pltpu.SemaphoreType.DMA((n_units,))]),
        compiler_params=_params(("arbitrary", "arbitrary")),
    )(*units, zp, norm_w, w_shard, b_blocks)


def _lower_bound(lb_ref):
    l0, l1 = lb_ref[0:1, :], lb_ref[1:2, :]
    _, lb = _sigmoid_pair(l1 - l0)
    return lb


def _chunk_gates(fz, lb, valid):
    sig, nsig = _sigmoid_pair(fz)
    f = lb + (1.0 - lb) * sig
    g2 = jnp.where(valid, jnp.log2(f), 0.0)
    k = jnp.where(valid, (1.0 - lb) * nsig, 0.0)
    return sig, nsig, f, g2, k


def _tri(n, upper=False):
    r = lax.broadcasted_iota(jnp.int32, (n, n), 0)
    c = lax.broadcasted_iota(jnp.int32, (n, n), 1)
    return jnp.where((r <= c) if upper else (r >= c), 1.0, 0.0).astype(BF16)


def _tri_dot(tri, x):
    hi = x.astype(BF16)
    rest = x - hi.astype(F32)
    mid = rest.astype(BF16)
    low = (rest - mid.astype(F32)).astype(BF16)
    return (jnp.dot(tri, hi, preferred_element_type=F32) + jnp.dot(tri, mid, preferred_element_type=F32)
            + jnp.dot(tri, low, preferred_element_type=F32))


def _intra_scores(q_ref, k_ref, b2_ref, col0):
    cols = pl.ds(col0, HEAD_DIM)
    rows_s = lax.broadcasted_iota(jnp.int32, (SUB, 1), 0)
    lanes_c = lax.broadcasted_iota(jnp.int32, (1, CHUNK), 1)
    blocks = []
    for i in range(N_SUB):
        lo = i * SUB
        qi = q_ref[lo:lo + SUB, cols]
        bi = b2_ref[lo:lo + SUB, cols]
        if i == 0:
            acc = jnp.zeros((SUB, CHUNK), F32)
        else:
            ref_i = b2_ref[lo:lo + 1, cols]
            qt = qi * jnp.exp2(bi - ref_i)
            kt = jnp.concatenate([k_ref[0:lo, cols] * jnp.exp2(ref_i - b2_ref[0:lo, cols]),
                                  jnp.zeros((CHUNK - lo, HEAD_DIM), F32)], axis=0)
            acc = _dot_nt(qt, kt)
        for s in range(SUB):
            b_s = b2_ref[lo + s:lo + s + 1, cols]
            k_s = k_ref[lo + s:lo + s + 1, cols]
            w = jnp.exp2(jnp.minimum(bi - b_s, 0.0))
            col = jnp.sum((qi * w) * k_s, axis=-1, keepdims=True)
            acc = jnp.where(lanes_c == lo + s, col, acc)
        blocks.append(jnp.where(lanes_c <= lo + rows_s, acc, 0.0))
    return jnp.concatenate(blocks, axis=0)


def _hgrn_forward(p, lb_logits, w_rest):
    n_rows = p.shape[1]
    n_chunks = n_rows // CHUNK
    width = HEADS * HEAD_DIM

    def body(q_ref, fz_ref, v_ref, lb_ref, rest_ref, o_ref, st_out_ref, a_out_ref, rest_out,
             state, k_vmem, b2_vmem, send_sems, recv_sems, local_sem):
        n = pl.program_id(0)
        own, sends, arrivals = _direct_gather(rest_ref, rest_out, send_sems, recv_sems, local_sem)

        @pl.when(n == 0)
        def _():
            state[...] = jnp.zeros_like(state)
            own.start()
            for cp in sends:
                cp.start()

        rows = n * CHUNK + lax.broadcasted_iota(jnp.int32, (CHUNK, 1), 0)
        valid = rows >= PAD_ROWS
        lb = _lower_bound(lb_ref)
        _, _, _, g2, k = _chunk_gates(fz_ref[0], lb, valid)
        k_vmem[...] = k
        b2_vmem[...] = _tri_dot(_tri(CHUNK), g2)
        q_view = q_ref.at[0]
        for h in range(HEADS):
            cols = pl.ds(h * HEAD_DIM, HEAD_DIM)
            st = state[h]
            st_out_ref[0, h] = st
            bh = b2_vmem[:, cols]
            kh = k_vmem[:, cols]
            vh = jnp.where(valid, v_ref[0, :, cols], 0.0)
            qe = q_ref[0, :, cols] * jnp.exp2(bh)
            a = _intra_scores(q_view, k_vmem, b2_vmem, h * HEAD_DIM).astype(BF16)
            a_out_ref[0, h] = a
            o_ref[:, cols] = _dot_nt(qe, st) + _dot(a, vh)
            b_last = b2_vmem[CHUNK - 1:CHUNK, cols]
            kd = kh * jnp.exp2(b_last - bh)
            state[h] = st * jnp.exp2(b_last) + _dot_tn(vh, kd)

        @pl.when(n == n_chunks - 1)
        def _():
            for cp in arrivals:
                cp.wait_recv()
            for cp in sends:
                cp.wait_send()
            own.wait()

    blk = lambda c: pl.BlockSpec((1, CHUNK, width), lambda n, c=c: (c, n, 0))
    return pl.pallas_call(
        body, name="hgrn_forward",
        out_shape=(jax.ShapeDtypeStruct((n_rows, width), F32),
                   jax.ShapeDtypeStruct((n_chunks, HEADS, HEAD_DIM, HEAD_DIM), F32),
                   jax.ShapeDtypeStruct((n_chunks, HEADS, CHUNK, CHUNK), BF16),
                   jax.ShapeDtypeStruct((N_DEV,) + w_rest.shape, w_rest.dtype)),
        grid=(n_chunks,),
        in_specs=[blk(0), blk(1), blk(2), pl.BlockSpec((2, width), lambda n: (0, 0)), ANY],
        out_specs=(pl.BlockSpec((CHUNK, width), lambda n: (n, 0)),
                   pl.BlockSpec((1, HEADS, HEAD_DIM, HEAD_DIM), lambda n: (n, 0, 0, 0)),
                   pl.BlockSpec((1, HEADS, CHUNK, CHUNK), lambda n: (n, 0, 0, 0)), ANY),
        scratch_shapes=[pltpu.VMEM((HEADS, HEAD_DIM, HEAD_DIM), F32), pltpu.VMEM((CHUNK, width), F32),
                        pltpu.VMEM((CHUNK, width), F32)] + list(GATHER_SEMS),
        compiler_params=_params(("arbitrary",)),
    )(p, p, p, lb_logits, w_rest)


def _hgrn_backward(p, lb_logits, states, scores, d_o, g_slabs, first_owner, g_rows):
    n_rows = p.shape[1]
    n_chunks = n_rows // CHUNK
    width = HEADS * HEAD_DIM

    def body(q_ref, fz_ref, v_ref, lb_ref, st_ref, a_ref, do_ref, gs_hbm, gr_hbm, dp_ref, dbias_ref, dlb_ref, rs_hbm, rr_hbm,
             dstate, k_vmem, b2_vmem, *sems):
        step = pl.program_id(0)
        n = n_chunks - 1 - step
        start_slabs, finish_slabs = _scatter_slabs(gs_hbm, first_owner, rs_hbm, *sems[:3])
        start_rows, finish_rows = _scatter_rows(gr_hbm, rr_hbm, *sems[3:])

        @pl.when(step == 0)
        def _():
            dstate[...] = jnp.zeros_like(dstate)
            dbias_ref[...] = jnp.zeros_like(dbias_ref)
            dlb_ref[...] = jnp.zeros_like(dlb_ref)
            start_slabs()
            start_rows()

        rows = n * CHUNK + lax.broadcasted_iota(jnp.int32, (CHUNK, 1), 0)
        valid = rows >= PAD_ROWS
        lb = _lower_bound(lb_ref)
        sig, nsig, f, g2, k = _chunk_gates(fz_ref[0], lb, valid)
        k_vmem[...] = k
        b2_vmem[...] = _tri_dot(_tri(CHUNK), g2)
        rows_c = lax.broadcasted_iota(jnp.int32, (CHUNK, 1), 0)
        rows_s = lax.broadcasted_iota(jnp.int32, (SUB, 1), 0)
        lanes_c = lax.broadcasted_iota(jnp.int32, (1, CHUNK), 1)
        causal = lax.broadcasted_iota(jnp.int32, (CHUNK, CHUNK), 0) >= lax.broadcasted_iota(jnp.int32, (CHUNK, CHUNK), 1)
        tri_up = _tri(CHUNK, upper=True)
        for h in range(HEADS):
            cols = pl.ds(h * HEAD_DIM, HEAD_DIM)
            st = st_ref[0, h]
            dst = dstate[h]
            qh = q_ref[0, :, cols]
            bh = b2_vmem[:, cols]
            kh = k_vmem[:, cols]
            vh = jnp.where(valid, v_ref[0, :, cols], 0.0)
            doh = do_ref[:, cols]
            eb = jnp.exp2(bh)
            qe = qh * eb
            b_last = b2_vmem[CHUNK - 1:CHUNK, cols]
            e_last = jnp.exp2(b_last)
            decay_k = jnp.exp2(b_last - bh)
            kd = kh * decay_k
            dqe = _dot(doh, st)
            da = jnp.where(causal, _dot_nt(doh, vh), 0.0)
            dv = _dot_tn(a_ref[0, h], doh) + _dot_nt(kd, dst)
            dkd = _dot(vh, dst)
            dstate[h] = dst * e_last + _dot_tn(doh, qe)
            db_last = (jnp.sum(dst * st, axis=0, keepdims=True) * e_last
                       + jnp.sum(dkd * kd, axis=0, keepdims=True))
            dq_blocks, dk_blocks = [], []
            dk_earlier = jnp.zeros((CHUNK, HEAD_DIM), F32)
            for i in range(N_SUB):
                lo = i * SUB
                qi = q_ref[0, lo:lo + SUB, cols]
                bi = b2_vmem[lo:lo + SUB, cols]
                da_i = da[lo:lo + SUB, :]
                if i == 0:
                    dq_i = jnp.zeros((SUB, HEAD_DIM), F32)
                else:
                    ref_i = b2_vmem[lo:lo + 1, cols]
                    eq = jnp.exp2(bi - ref_i)
                    ek = jnp.exp2(ref_i - b2_vmem[0:lo, cols])
                    later = jnp.zeros((CHUNK - lo, HEAD_DIM), F32)
                    kt = jnp.concatenate([k_vmem[0:lo, cols] * ek, later], axis=0)
                    dq_i = _dot(da_i, kt) * eq
                    dk_earlier = dk_earlier + jnp.concatenate([_dot_tn(da_i, qi * eq)[0:lo] * ek, later], axis=0)
                dk_i = jnp.zeros((SUB, HEAD_DIM), F32)
                for s in range(SUB):
                    b_s = b2_vmem[lo + s:lo + s + 1, cols]
                    k_s = k_vmem[lo + s:lo + s + 1, cols]
                    w = jnp.exp2(jnp.minimum(bi - b_s, 0.0))
                    da_col = jnp.sum(jnp.where(lanes_c == lo + s, da_i, 0.0), axis=-1, keepdims=True)
                    gw = da_col * w
                    dq_i = dq_i + gw * k_s
                    dk_i = jnp.where(rows_s == s, jnp.sum(gw * qi, axis=0, keepdims=True), dk_i)
                dq_blocks.append(dq_i)
                dk_blocks.append(dk_i)
            dq_intra = jnp.concatenate(dq_blocks, axis=0)
            dk_intra = jnp.concatenate(dk_blocks, axis=0) + dk_earlier
            dq = dqe * eb + dq_intra
            dk = dkd * decay_k + dk_intra
            db = dqe * qe - dkd * kd + qh * dq_intra - kh * dk_intra
            db = db + jnp.where(rows_c == CHUNK - 1, db_last, 0.0)
            dg = _tri_dot(tri_up, db)
            fh = f[:, h * HEAD_DIM:(h + 1) * HEAD_DIM]
            sh = sig[:, h * HEAD_DIM:(h + 1) * HEAD_DIM]
            nh = nsig[:, h * HEAD_DIM:(h + 1) * HEAD_DIM]
            lbh = lb[:, h * HEAD_DIM:(h + 1) * HEAD_DIM]
            df = jnp.where(valid, dg / fh - dk, 0.0)
            dfz = df * (1.0 - lbh) * sh * nh
            dq = jnp.where(valid, dq, 0.0)
            dv = jnp.where(valid, dv, 0.0)
            dlb_ref[:, cols] += jnp.sum(df * nh, axis=0, keepdims=True)
            dp_ref[0, :, cols] = dq.astype(BF16)
            dp_ref[1, :, cols] = dfz.astype(BF16)
            dp_ref[2, :, cols] = dv.astype(BF16)
            dbias_ref[0, :, cols] += jnp.sum(dq, axis=0, keepdims=True)
            dbias_ref[1, :, cols] += jnp.sum(dfz, axis=0, keepdims=True)
            dbias_ref[2, :, cols] += jnp.sum(dv, axis=0, keepdims=True)

        @pl.when(step == n_chunks - 1)
        def _():
            finish_slabs()
            finish_rows()

    rev = lambda s: n_chunks - 1 - s
    blk = lambda c: pl.BlockSpec((1, CHUNK, width), lambda s, c=c: (c, rev(s), 0))
    return pl.pallas_call(
        body, name="hgrn_backward",
        out_shape=(jax.ShapeDtypeStruct((3, n_rows, width), BF16),
                   jax.ShapeDtypeStruct((3, 1, width), F32),
                   jax.ShapeDtypeStruct((1, width), F32),
                   jax.ShapeDtypeStruct((N_DEV,) + g_slabs.shape[1:], g_slabs.dtype),
                   jax.ShapeDtypeStruct((N_DEV, N_ROW_GRADS, 128, g_rows.shape[2]), g_rows.dtype)),
        grid=(n_chunks,),
        in_specs=[blk(0), blk(1), blk(2), pl.BlockSpec((2, width), lambda s: (0, 0)),
                  pl.BlockSpec((1, HEADS, HEAD_DIM, HEAD_DIM), lambda s: (rev(s), 0, 0, 0)),
                  pl.BlockSpec((1, HEADS, CHUNK, CHUNK), lambda s: (rev(s), 0, 0, 0)),
                  pl.BlockSpec((CHUNK, width), lambda s: (rev(s), 0)), ANY, ANY],
        out_specs=(pl.BlockSpec((3, CHUNK, width), lambda s: (0, rev(s), 0)),
                   pl.BlockSpec((3, 1, width), lambda s: (0, 0, 0)),
                   pl.BlockSpec((1, width), lambda s: (0, 0)), ANY, ANY),
        scratch_shapes=[pltpu.VMEM((HEADS, HEAD_DIM, HEAD_DIM), F32), pltpu.VMEM((CHUNK, width), F32),
                        pltpu.VMEM((CHUNK, width), F32)] + list(SCATTER_SEMS) + list(SCATTER_ROWS_SEMS),
        compiler_params=_params(("arbitrary",)),
    )(p, p, p, lb_logits, states, scores, d_o, g_slabs, g_rows)


def _sigmoid_and_complement(x):
    s = 0.5 * jnp.tanh(0.5 * x) + 0.5
    return s, 1.0 - s


def _silu_and_grad(x):
    s, ns = _sigmoid_and_complement(x)
    return x * s, s * (1.0 + x * ns)


def _tail(p, o, zp, tgt, hg_norm_w, pool_w, pool_scale, w_down_hg, w_down_pool, w_out, final_norm_w):
    n_rows = zp.shape[0]
    tr = _row_tile(n_rows, 208)
    nt = n_rows // tr
    ext = tr + HALO
    n_groups = len(POOL_WINDOWS)

    def body(o_ref, ghg_ref, u_ref, gpool_ref, mhg_ref, mpool_ref, uhalo_ref, z_ref, tgt_hbm,
             hgw_ref, pw_ref, ps_ref, wdh_ref, wdp_ref, wout_ref, fnw_ref,
             do_ref, dp_ref, dz2_ref, lhs_ref, rhs_ref,
             dbias_ref, dhgw_ref, dpw_ref, dps_ref, dfnw_ref, loss_ref, halo_vmem, tgt_buf, tgt_sems):
        step = pl.program_id(0)
        ti = nt - 1 - step

        def target_rows(tile, slot, act):
            @pl.when(tile == 0)
            def _():
                cp = pltpu.make_async_copy(tgt_hbm.at[pl.ds(0, tr - CHUNK), :], tgt_buf.at[slot, pl.ds(CHUNK, tr - CHUNK), :],
                                           tgt_sems.at[slot])
                getattr(cp, act)()

            @pl.when(tile > 0)
            def _():
                cp = pltpu.make_async_copy(tgt_hbm.at[pl.ds(tile * tr - CHUNK, tr), :], tgt_buf.at[slot], tgt_sems.at[slot])
                getattr(cp, act)()

        @pl.when(step == 0)
        def _():
            halo_vmem[...] = jnp.zeros_like(halo_vmem)
            for r in (dbias_ref, dhgw_ref, dpw_ref, dps_ref, dfnw_ref, loss_ref):
                r[...] = jnp.zeros_like(r)
            if nt <= 2:
                tgt_buf[(nt - 1) % 2, 0:CHUNK, :] = jnp.zeros((CHUNK, D_MODEL), F32)
            target_rows(ti, 0, "start")

        @pl.when(ti > 0)
        def _():
            target_rows(ti - 1, (step + 1) % 2, "start")

        rows = ti * tr + lax.broadcasted_iota(jnp.int32, (tr, 1), 0)
        valid = rows >= PAD_ROWS
        in_loss = rows >= CHUNK
        count_pos = jnp.maximum(rows - PAD_ROWS + 1, 1).astype(F32)

        o = o_ref[...]
        hgw = hgw_ref[...]
        inv_o, on_parts = [], []
        for h in range(HEADS):
            oh = o[:, h * HEAD_DIM:(h + 1) * HEAD_DIM]
            r = lax.rsqrt(jnp.mean(oh * oh, axis=-1, keepdims=True) + EPS)
            inv_o.append(r)
            on_parts.append(oh * r)
        o_hat = jnp.concatenate(on_parts, axis=1)
        o_n = o_hat * hgw
        g_hg = ghg_ref[0]
        silu_hg, dsilu_hg = _silu_and_grad(g_hg)
        a_hg = o_n * silu_hg
        y_hg = _dot(a_hg, wdh_ref[...])

        u = jnp.where(valid, u_ref[0], 0.0)
        u_prev = jnp.where(ti > 0, uhalo_ref[0], 0.0)
        u_ext = jnp.concatenate([u_prev, u], axis=0)
        pooled_parts, mixed_parts, inv_cnt = [], [], []
        for gi, win in enumerate(POOL_WINDOWS):
            lanes = slice(gi * POOL_GDIM, (gi + 1) * POOL_GDIM)
            s = u_ext[:, lanes]
            shift = 1
            while shift < win:
                s = s + pltpu.roll(s, shift, 0)
                shift *= 2
            ic = 1.0 / jnp.minimum(count_pos, float(win))
            inv_cnt.append(ic)
            pooled = s[HALO:] * ic - u[:, lanes]
            pooled_parts.append(pooled)
            mixed_parts.append(_dot(pooled, pw_ref[gi]))
        mixed = jnp.concatenate(mixed_parts, axis=1)
        ps = ps_ref[...]
        g_pool = gpool_ref[0]
        silu_pool, dsilu_pool = _silu_and_grad(g_pool)
        a_pool = mixed * ps * silu_pool
        y_pool = _dot(a_pool, wdp_ref[...])

        m_hg, m_pool = mhg_ref[0], mpool_ref[0]
        s_hg, ns_hg = _sigmoid_and_complement(m_hg)
        s_pool, ns_pool = _sigmoid_and_complement(m_pool)
        merged = s_hg * y_hg + s_pool * y_pool
        z2 = z_ref[...] + _dot(merged, wout_ref[...])
        r2 = lax.rsqrt(jnp.mean(z2 * z2, axis=-1, keepdims=True) + EPS)
        n2 = z2 * r2
        fnw = fnw_ref[...]
        target_rows(ti, step % 2, "wait")
        err = jnp.where(in_loss, n2 * fnw - tgt_buf[step % 2], 0.0)
        loss_ref[...] += jnp.sum(jnp.sum(err * err, axis=0, keepdims=True), axis=1, keepdims=True) * (0.5 / D_MODEL)
        dy = err * (1.0 / D_MODEL)

        dfnw_ref[...] += jnp.sum(dy * n2, axis=0, keepdims=True)
        gy = dy * fnw
        dz2 = r2 * (gy - n2 * jnp.mean(gy * n2, axis=-1, keepdims=True))
        dmerged = _dot_nt(dz2, wout_ref[...])
        dy_hg = s_hg * dmerged
        dy_pool = s_pool * dmerged
        dm_hg = dmerged * y_hg * s_hg * ns_hg
        dm_pool = dmerged * y_pool * s_pool * ns_pool
        da_hg = _dot_nt(dy_hg, wdh_ref[...])
        da_pool = _dot_nt(dy_pool, wdp_ref[...])

        d_on = da_hg * silu_hg
        dg_hg = da_hg * o_n * dsilu_hg
        dhgw_ref[...] += jnp.sum(d_on * o_hat, axis=0, keepdims=True)
        gyo = d_on * hgw
        do_parts = []
        for h in range(HEADS):
            lanes = slice(h * HEAD_DIM, (h + 1) * HEAD_DIM)
            gh, nh = gyo[:, lanes], o_hat[:, lanes]
            do_parts.append(inv_o[h] * (gh - nh * jnp.mean(gh * nh, axis=-1, keepdims=True)))
        do_ref[...] = jnp.concatenate(do_parts, axis=1)

        dmixed = da_pool * ps * silu_pool
        dps_ref[...] += jnp.sum(da_pool * mixed * silu_pool, axis=0, keepdims=True)
        dg_pool = da_pool * mixed * ps * dsilu_pool
        du_parts = []
        for gi, win in enumerate(POOL_WINDOWS):
            lanes = slice(gi * POOL_GDIM, (gi + 1) * POOL_GDIM)
            dmx = dmixed[:, lanes]
            dpooled = _dot_nt(dmx, pw_ref[gi])
            dpw_ref[gi] += _dot_tn(pooled_parts[gi], dmx)
            dpt = dpooled * inv_cnt[gi]
            s = jnp.concatenate([dpt, halo_vmem[:, lanes]], axis=0)
            shift = 1
            while shift < win:
                s = s + pltpu.roll(s, ext - shift, 0)
                shift *= 2
            du_parts.append(s[:tr] - dpooled)
            halo_vmem[:, lanes] = dpt[:HALO]
        du = jnp.where(valid, jnp.concatenate(du_parts, axis=1), 0.0)

        for c, val in enumerate((dg_hg, du, dg_pool, dm_hg, dm_pool)):
            dp_ref[c] = val.astype(BF16)
            dbias_ref[c] += jnp.sum(val, axis=0, keepdims=True)
        dz2_ref[...] = dz2
        for c, (lhs, rhs) in enumerate(((merged, dz2), (a_hg, dy_hg), (a_pool, dy_pool))):
            lhs_ref[c] = lhs.astype(BF16)
            rhs_ref[c] = rhs.astype(BF16)

    rev = lambda s: nt - 1 - s
    rowblk = pl.BlockSpec((tr, D_MODEL), lambda s: (rev(s), 0))
    pblk = lambda c: pl.BlockSpec((1, tr, 1024), lambda s, c=c: (c, rev(s), 0))
    halo_blk = pl.BlockSpec((1, HALO, 1024), lambda s: (4, jnp.maximum(rev(s) * (tr // HALO) - 1, 0), 0))
    full = lambda shape: pl.BlockSpec(shape, lambda s: (0,) * len(shape))
    vec = full((1, D_MODEL))
    mat = full((D_MODEL, D_MODEL))
    act3 = jax.ShapeDtypeStruct((3, n_rows, D_MODEL), BF16)
    act3_blk = pl.BlockSpec((3, tr, D_MODEL), lambda s: (0, rev(s), 0))
    return pl.pallas_call(
        body, name="tail_forward_backward",
        out_shape=(jax.ShapeDtypeStruct((n_rows, D_MODEL), F32),
                   jax.ShapeDtypeStruct((5, n_rows, 1024), BF16),
                   jax.ShapeDtypeStruct((n_rows, D_MODEL), F32),
                   act3, act3,
                   jax.ShapeDtypeStruct((5, 1, 1024), F32),
                   jax.ShapeDtypeStruct((1, D_MODEL), F32),
                   jax.ShapeDtypeStruct((n_groups, POOL_GDIM, POOL_GDIM), F32),
                   jax.ShapeDtypeStruct((1, D_MODEL), F32),
                   jax.ShapeDtypeStruct((1, D_MODEL), F32),
                   jax.ShapeDtypeStruct((1, 1), F32)),
        grid=(nt,),
        in_specs=[rowblk, pblk(3), pblk(4), pblk(5), pblk(6), pblk(7), halo_blk, rowblk, ANY,
                  vec, full((n_groups, POOL_GDIM, POOL_GDIM)), vec, mat, mat, mat, vec],
        out_specs=(rowblk, pl.BlockSpec((5, tr, 1024), lambda s: (0, rev(s), 0)), rowblk,
                   act3_blk, act3_blk,
                   full((5, 1, 1024)), vec, full((n_groups, POOL_GDIM, POOL_GDIM)), vec, vec, full((1, 1))),
        scratch_shapes=[pltpu.VMEM((HALO, D_MODEL), F32), pltpu.VMEM((2, tr, D_MODEL), F32), pltpu.SemaphoreType.DMA((2,))],
        compiler_params=_params(("arbitrary",)),
    )(o, p, p, p, p, p, p, zp, tgt, hg_norm_w, pool_w, pool_scale, w_down_hg, w_down_pool, w_out, final_norm_w)


def _in_projection_backward(dp_a, dp_b, w_blocks, zp, dz2, norm_w, chip_sums):
    n_rows = zp.shape[0]
    tr = _row_tile(n_rows, 416)
    nt = n_rows // tr
    na, nb = dp_a.shape[0], dp_b.shape[0]

    def body(dpa_ref, dpb_ref, w_hbm, z_ref, dz2_ref, nw_ref, gs_hbm, gx_hbm, head_ref, dnw_ref, rs_hbm,
             w_vmem, sem, dz_buf, gx_sems, *sems):
        i = pl.program_id(0)
        start_slabs, finish_slabs = _scatter_low(gs_hbm, rs_hbm, *sems)

        def wait_rows_out(tile):
            @pl.when(tile == 0)
            def _():
                pltpu.make_async_copy(dz_buf.at[0, pl.ds(CHUNK, tr - CHUNK), :], gx_hbm.at[pl.ds(0, tr - CHUNK), :],
                                      gx_sems.at[0]).wait()

            @pl.when(tile > 0)
            def _():
                pltpu.make_async_copy(dz_buf.at[tile % 2], gx_hbm.at[pl.ds(tile * tr - CHUNK, tr), :],
                                      gx_sems.at[tile % 2]).wait()

        @pl.when(i == 0)
        def _():
            start_slabs()
            cp = pltpu.make_async_copy(w_hbm, w_vmem, sem)
            cp.start()
            cp.wait()
            dnw_ref[...] = jnp.zeros_like(dnw_ref)

        dh = jnp.zeros((tr, D_MODEL), F32)
        half_cols = w_vmem.shape[-1]
        for j in range(na + nb):
            dp_ref, jj = (dpa_ref, j) if j < na else (dpb_ref, j - na)
            for half in range(2):
                dh = dh + _dot_nt(dp_ref[jj, :, half * half_cols:(half + 1) * half_cols], w_vmem[j, half])
        z = z_ref[...]
        r = lax.rsqrt(jnp.mean(z * z, axis=-1, keepdims=True) + EPS)
        n1 = z * r
        dnw_ref[...] += jnp.sum(dh * n1, axis=0, keepdims=True)
        gh = dh * nw_ref[...]
        dz = dz2_ref[...] + r * (gh - n1 * jnp.mean(gh * n1, axis=-1, keepdims=True))

        @pl.when(i >= 2)
        def _():
            wait_rows_out(i - 2)

        dz_buf[i % 2] = dz

        @pl.when(i == 0)
        def _():
            head_ref[...] = dz[0:CHUNK]
            pltpu.make_async_copy(dz_buf.at[0, pl.ds(CHUNK, tr - CHUNK), :], gx_hbm.at[pl.ds(0, tr - CHUNK), :],
                                  gx_sems.at[0]).start()

        @pl.when(i > 0)
        def _():
            pltpu.make_async_copy(dz_buf.at[i % 2], gx_hbm.at[pl.ds(i * tr - CHUNK, tr), :], gx_sems.at[i % 2]).start()

        @pl.when(i == nt - 1)
        def _():
            if nt >= 2:
                wait_rows_out(i - 1)
            wait_rows_out(i)
            finish_slabs()

    rowblk = pl.BlockSpec((tr, D_MODEL), lambda i: (i, 0))
    vec = pl.BlockSpec((1, D_MODEL), lambda i: (0, 0))
    return pl.pallas_call(
        body, name="in_projection_backward",
        out_shape=(jax.ShapeDtypeStruct((n_rows - CHUNK, D_MODEL), F32), jax.ShapeDtypeStruct((CHUNK, D_MODEL), F32),
                   jax.ShapeDtypeStruct((1, D_MODEL), F32),
                   jax.ShapeDtypeStruct((4,) + chip_sums.shape[1:], chip_sums.dtype)),
        grid=(nt,),
        in_specs=[pl.BlockSpec((na, tr, 1024), lambda i: (0, i, 0)), pl.BlockSpec((nb, tr, 1024), lambda i: (0, i, 0)),
                  ANY, rowblk, rowblk, vec, ANY],
        out_specs=(ANY, pl.BlockSpec((CHUNK, D_MODEL), lambda i: (0, 0)), vec, ANY),
        scratch_shapes=[pltpu.VMEM(w_blocks.shape, w_blocks.dtype), pltpu.SemaphoreType.DMA(()),
                        pltpu.VMEM((2, tr, D_MODEL), F32), pltpu.SemaphoreType.DMA((2,)),
                        pltpu.SemaphoreType.DMA((2,)), pltpu.SemaphoreType.DMA((3,)), pltpu.SemaphoreType.DMA(())],
        compiler_params=_params(("arbitrary",)),
    )(dp_a, dp_b, w_blocks, zp, dz2, norm_w, chip_sums)


def _weight_grad(xs, ys, name):
    shared = xs.ndim == 2
    n_rows, m = xs.shape[-2:]
    nb, _, n = ys.shape
    tk = _row_tile(n_rows, 4160)
    n_k = n_rows // tk

    def body(x_ref, y_ref, o_ref, acc):
        k = pl.program_id(1)

        @pl.when(k == 0)
        def _():
            acc[...] = jnp.zeros_like(acc)

        acc[...] += _dot_tn(x_ref[...] if shared else x_ref[0], y_ref[0])

        @pl.when(k == n_k - 1)
        def _():
            o_ref[0] = acc[...].astype(o_ref.dtype)

    x_spec = pl.BlockSpec((tk, m), lambda j, k: (k, 0)) if shared else pl.BlockSpec((1, tk, m), lambda j, k: (j, k, 0))
    return pl.pallas_call(
        body, name=name,
        out_shape=jax.ShapeDtypeStruct((nb, m, n), BF16),
        grid=(nb, n_k),
        in_specs=[x_spec, pl.BlockSpec((1, tk, n), lambda j, k: (j, k, 0))],
        out_specs=pl.BlockSpec((1, m, n), lambda j, k: (j, 0, 0)),
        scratch_shapes=[pltpu.VMEM((m, n), F32)],
        compiler_params=_params(("arbitrary", "arbitrary")),
    )(xs, ys)


def kernel(x, meta_tokens, norm_w, w_in, b_in, lb_logits, hg_norm_w, pool_w, pool_scale, w_down_hg, w_down_pool, w_out, final_norm_w, loss_target, m_meta_tokens, m_norm_w, m_w_in, m_b_in, m_lb_logits, m_hg_norm_w, m_pool_w, m_pool_scale, m_w_down_hg, m_w_down_pool, m_w_out, m_final_norm_w, v_meta_tokens, v_norm_w, v_w_in, v_b_in, v_lb_logits, v_hg_norm_w, v_pool_w, v_pool_scale, v_w_down_hg, v_w_down_pool, v_w_out, v_final_norm_w):
    seq = x.shape[1]

    meta_full = _all_gather_small(meta_tokens).transpose(1, 0, 2).reshape(N_META, D_MODEL)
    w_rest = jnp.concatenate([w_down_hg[0].astype(BF16), w_down_pool[0].astype(BF16), w_out[0].astype(BF16),
                              pool_w[0].astype(BF16).reshape(32, 1024)], axis=0)

    zp = jnp.concatenate([jnp.zeros((PAD_ROWS, D_MODEL), F32), meta_full, x[0]], axis=0)
    p, h, w_blocks = _in_projection(zp, norm_w, w_in[0].astype(BF16), b_in.reshape(N_COLBLK, 1, 1024), _gather_units())
    o, states, scores, rest = _hgrn_forward(p, lb_logits, w_rest)
    wdh = rest[:, REST_W_DOWN_HG:REST_W_DOWN_HG + 128].reshape(1024, 1024)
    wdp = rest[:, REST_W_DOWN_POOL:REST_W_DOWN_POOL + 128].reshape(1024, 1024)
    wout = rest[:, REST_W_OUT:REST_W_OUT + 128].reshape(1024, 1024)
    pw = rest[:, REST_POOL_W:REST_POOL_W + 32].reshape(N_DEV, 4, 32, 256).transpose(1, 0, 2, 3).reshape(4, 256, 256)
    (d_o, dp_b, dz2, grad_lhs, grad_rhs, dbias_b, d_hgw, d_pw, d_ps, d_fnw, loss_part) = _tail(
        p, o, zp, loss_target[0], hg_norm_w, pw, pool_scale, wdh, wdp, wout, final_norm_w.reshape(1, D_MODEL))
    n_a = N_COLBLK - dp_b.shape[0]
    g_hi = _weight_grad(h, dp_b, "weight_grad_in_hi")
    g_rows = _weight_grad(grad_lhs, grad_rhs, "weight_grad_rows")
    dp_a, dbias_a, d_lb, recv_hi, recv_rows = _hgrn_backward(p, lb_logits, states, scores, d_o, g_hi, n_a, g_rows)
    assert n_a == LOW_OWNERS
    chip_lo = _weight_grad_low(h, dp_a)
    dz_seq, dz_head, d_nw, recv_lo = _in_projection_backward(dp_a, dp_b, w_blocks, zp, dz2, norm_w, chip_lo)

    lb = jax.nn.sigmoid(lb_logits[0:1] - lb_logits[1:2])
    d_l0 = d_lb * lb * (1.0 - lb)
    replicated = jnp.concatenate([dbias_a.reshape(3, 1024), dbias_b.reshape(5, 1024), d_nw, d_l0, -d_l0, d_hgw, d_ps, d_fnw,
                                  jnp.pad(loss_part, ((0, MISC_ROWS - MISC_LOSS - 1), (0, 1023)))], axis=0)
    d_meta = dz_head[PAD_ROWS:CHUNK].reshape(N_META, N_DEV, 128).transpose(1, 0, 2)
    d_pw_blocks = d_pw.reshape(4, N_DEV, 32, 256).transpose(1, 0, 2, 3).reshape(N_DEV, 32, 1024)
    g_misc = jnp.concatenate([d_pw_blocks, jnp.pad(d_meta, ((0, 0), (0, 0), (0, 1024 - 128))),
                              jnp.broadcast_to(replicated[None], (N_DEV, 16, 1024))], axis=1)

    as_rows = lambda t, n: t.reshape(n, 1024)
    small = [(MISC_POOL_W, 1024, tuple(as_rows(t, 32) for t in (pool_w, m_pool_w, v_pool_w))),
             (MISC_META, 128, (meta_tokens, m_meta_tokens, v_meta_tokens)),
             (MISC_B_IN, 1024, tuple(as_rows(t, 8) for t in (b_in, m_b_in, v_b_in))),
             (MISC_NORM_W, 1024, (norm_w, m_norm_w, v_norm_w)),
             (MISC_LB, 1024, (lb_logits, m_lb_logits, v_lb_logits)),
             (MISC_HG_NORM_W, 1024, (hg_norm_w, m_hg_norm_w, v_hg_norm_w)),
             (MISC_POOL_SCALE, 1024, (pool_scale, m_pool_scale, v_pool_scale)),
             (MISC_FINAL_NORM_W, 1024, tuple(as_rows(t, 1) for t in (final_norm_w, m_final_norm_w, v_final_norm_w)))]
    res, loss = _finish(recv_hi, recv_lo, n_a, recv_rows, _exchange_small(g_misc), (w_in, m_w_in, v_w_in),
                  [(w_out, m_w_out, v_w_out), (w_down_hg, m_w_down_hg, v_w_down_hg), (w_down_pool, m_w_down_pool, v_w_down_pool)],
                  small)
    r_w_in, r_w_out, r_wdh, r_wdp, r_pw, r_meta, r_b_in, r_nw, r_lb, r_hgw, r_ps, r_fnw = res
    grad_x = dz_seq.reshape(1, seq, D_MODEL)
    per_kind = [(r_meta[k], r_nw[k], r_w_in[k], r_b_in[k].reshape(1, 8192), r_lb[k], r_hgw[k], r_pw[k].reshape(1, 4, 32, 256),
                 r_ps[k], r_wdh[k], r_wdp[k], r_w_out[k], r_fnw[k].reshape(1024)) for k in range(4)]
    return (loss, grad_x, *per_kind[0], *per_kind[1], *per_kind[2], *per_kind[3])
```

```python
import functools

import jax
import jax.numpy as jnp
from jax import lax
from jax.experimental import pallas as pl
from jax.experimental.pallas import tpu as pltpu

F32 = jnp.float32
BF16 = jnp.bfloat16

D_MODEL = 1024
N_META = 16
HEADS = 8
HEAD_DIM = 128
CHUNK = 64
SUB = 8
N_SUB = CHUNK // SUB
PAD_ROWS = CHUNK - N_META
POOL_WINDOWS = (2, 4, 8, 16)
POOL_GDIM = D_MODEL // len(POOL_WINDOWS)
HALO = 16
EPS = 1e-6
N_DEV = 8
N_COLBLK = 8
ADAM_LR, ADAM_B1, ADAM_B2, ADAM_EPS, ADAM_WD, ADAM_STEP = 0.001, 0.9, 0.999, 1e-08, 0.01, 10

VMEM_LIMIT = 56 * 1024 * 1024
MESH = pl.DeviceIdType.MESH
ANY = pl.BlockSpec(memory_space=pl.ANY)

REST_W_DOWN_HG = 0
REST_W_DOWN_POOL = 128
REST_W_OUT = 256
REST_POOL_W = 384
MISC_POOL_W = 0
MISC_META = 32
MISC_B_IN = 48
MISC_NORM_W = 56
MISC_LB = 57
MISC_HG_NORM_W = 59
MISC_POOL_SCALE = 60
MISC_FINAL_NORM_W = 61
MISC_LOSS = 62
MISC_ROWS = 64


def _params(sem=None):
    return pltpu.CompilerParams(dimension_semantics=sem, vmem_limit_bytes=VMEM_LIMIT)


def _row_tile(n_rows, prefer):
    best = 16
    for t in range(16, prefer + 1, 16):
        if n_rows % t == 0:
            best = t
    return best


def _sigmoid_pair(x):
    e = jnp.exp(-jnp.abs(x))
    r = 1.0 / (1.0 + e)
    er = e * r
    pos = x >= 0
    return jnp.where(pos, r, er), jnp.where(pos, er, r)


def _dot(a, b):
    return jnp.dot(a.astype(BF16), b.astype(BF16), preferred_element_type=F32)


def _dot_nt(a, b):
    return lax.dot_general(a.astype(BF16), b.astype(BF16), (((1,), (1,)), ((), ())), preferred_element_type=F32)


def _dot_tn(a, b):
    return lax.dot_general(a.astype(BF16), b.astype(BF16), (((0,), (0,)), ((), ())), preferred_element_type=F32)


def _device_index(px, py, pc):
    return 4 * px + 2 * py + pc


def _direct_gather(src_ref, dst_ref, send_sems, recv_sems, local_sem):
    x, y, c = lax.axis_index("x"), lax.axis_index("y"), lax.axis_index("c")
    own = pltpu.make_async_copy(src_ref, dst_ref.at[_device_index(x, y, c)], local_sem)
    sends, arrivals = [], []
    for k in range(1, N_DEV):
        peer = (1 - x if k & 4 else x, 1 - y if k & 2 else y, 1 - c if k & 1 else c)
        for slot, out in ((_device_index(x, y, c), sends), (_device_index(*peer), arrivals)):
            out.append(pltpu.make_async_remote_copy(
                src_ref=src_ref, dst_ref=dst_ref.at[slot], send_sem=send_sems.at[k - 1], recv_sem=recv_sems.at[k - 1],
                device_id=peer, device_id_type=MESH))
    return own, sends, arrivals


GATHER_SEMS = [pltpu.SemaphoreType.DMA((N_DEV - 1,)), pltpu.SemaphoreType.DMA((N_DEV - 1,)), pltpu.SemaphoreType.DMA(())]


def _all_gather_small(block):
    def body(x_ref, out_ref, send_sems, recv_sems, local_sem):
        own, sends, arrivals = _direct_gather(x_ref, out_ref, send_sems, recv_sems, local_sem)
        own.start()
        for cp in sends:
            cp.start()
        for cp in arrivals:
            cp.wait_recv()
        for cp in sends:
            cp.wait_send()
        own.wait()

    return pl.pallas_call(
        body, name="all_gather_meta",
        out_shape=jax.ShapeDtypeStruct((N_DEV,) + block.shape, block.dtype),
        in_specs=[ANY], out_specs=ANY, scratch_shapes=list(GATHER_SEMS),
    )(block)


def _peer(k):
    x, y, c = lax.axis_index("x"), lax.axis_index("y"), lax.axis_index("c")
    return (1 - x if k & 4 else x, 1 - y if k & 2 else y, 1 - c if k & 1 else c)


def _me():
    return _device_index(lax.axis_index("x"), lax.axis_index("y"), lax.axis_index("c"))


def _remote(src, dst, send_sem, recv_sem, peer_bits):
    return pltpu.make_async_remote_copy(src_ref=src, dst_ref=dst, send_sem=send_sem, recv_sem=recv_sem,
                                        device_id=_peer(peer_bits), device_id_type=MESH)


N_ROW_GRADS = 3
SCATTER_SEMS = [pltpu.SemaphoreType.DMA((N_DEV - 1,)), pltpu.SemaphoreType.DMA((N_DEV - 1,)), pltpu.SemaphoreType.DMA(())]
SCATTER_ROWS_SEMS = [pltpu.SemaphoreType.DMA((7 * N_ROW_GRADS,)), pltpu.SemaphoreType.DMA((7 * N_ROW_GRADS,)),
                     pltpu.SemaphoreType.DMA((N_ROW_GRADS,))]


def _scatter_slabs(g_ref, first, recv_ref, send_sems, recv_sems, local_sem):
    n = g_ref.shape[0]
    me = _me()

    def each(on_send, on_local, on_arrival):
        for kk in range(1, N_DEV):
            peer = jnp.bitwise_xor(me, kk)

            @pl.when((peer >= first) & (peer < first + n))
            def _(kk=kk, peer=peer):
                on_send(_remote(g_ref.at[peer - first], recv_ref.at[me], send_sems.at[kk - 1], recv_sems.at[kk - 1], kk))

        @pl.when((me >= first) & (me < first + n))
        def _():
            on_local(pltpu.make_async_copy(g_ref.at[me - first], recv_ref.at[me], local_sem))
            if on_arrival is not None:
                for kk in range(1, N_DEV):
                    on_arrival(_remote(g_ref.at[0], recv_ref.at[jnp.bitwise_xor(me, kk)], send_sems.at[kk - 1],
                                       recv_sems.at[kk - 1], kk))

    start = lambda: each(lambda cp: cp.start(), lambda cp: cp.start(), None)
    finish = lambda: each(lambda cp: cp.wait_send(), lambda cp: cp.wait(), lambda cp: cp.wait_recv())
    return start, finish


def _scatter_rows(g_ref, recv_ref, send_sems, recv_sems, local_sems):
    me = _me()
    rows = lambda m, dev: g_ref.at[m, pl.ds(dev * 128, 128), :]

    def copies():
        local = [pltpu.make_async_copy(rows(m, me), recv_ref.at[me, m], local_sems.at[m]) for m in range(N_ROW_GRADS)]
        sends, arrivals = [], []
        for m in range(N_ROW_GRADS):
            for kk in range(1, N_DEV):
                peer, sems = jnp.bitwise_xor(me, kk), (send_sems.at[7 * m + kk - 1], recv_sems.at[7 * m + kk - 1])
                sends.append(_remote(rows(m, peer), recv_ref.at[me, m], *sems, kk))
                arrivals.append(_remote(rows(m, me), recv_ref.at[peer, m], *sems, kk))
        return local, sends, arrivals

    def start():
        local, sends, _ = copies()
        for cp in local + sends:
            cp.start()

    def finish():
        local, sends, arrivals = copies()
        for cp in arrivals:
            cp.wait_recv()
        for cp in sends:
            cp.wait_send()
        for cp in local:
            cp.wait()

    return start, finish


LOW_OWNERS = 3


def _weight_grad_low(h, dp_a):
    n_rows = h.shape[0]
    tk = _row_tile(n_rows, 2080)
    n_k = n_rows // tk
    order = jnp.where(lax.axis_index("c") == 0, jnp.array([1, 0, 2], jnp.int32), jnp.array([0, 2, 1], jnp.int32))

    def body(order_ref, h_ref, y_ref, out_ref, acc, send_buf, got, send_sems, recv_sems):
        t, k = pl.program_id(0), pl.program_id(1)
        c = lax.axis_index("c")

        @pl.when(k == 0)
        def _():
            acc[...] = jnp.zeros_like(acc)

        acc[...] += _dot_tn(h_ref[...], y_ref[0])

        def to_sibling(slot):
            return _remote(send_buf.at[slot], got.at[slot], send_sems.at[slot], recv_sems.at[slot], 1)

        def send(slot):
            send_buf[slot] = acc[...].astype(send_buf.dtype)
            to_sibling(slot).start()

        def keep(slot):
            to_sibling(slot).wait_recv()
            out_ref[slot] = (acc[...] + got[slot].astype(F32)).astype(out_ref.dtype)

        done = k == n_k - 1
        for core, step, action, slot in ((0, 0, send, 0), (0, 1, keep, 0), (0, 2, keep, 1),
                                         (1, 0, send, 0), (1, 1, send, 1), (1, 2, keep, 0)):
            @pl.when(done & (c == core) & (t == step))
            def _(action=action, slot=slot):
                action(slot)

        @pl.when(done & (t == LOW_OWNERS - 1))
        def _():
            to_sibling(0).wait_send()

            @pl.when(c == 1)
            def _():
                to_sibling(1).wait_send()
                out_ref[1] = jnp.zeros(out_ref.shape[1:], out_ref.dtype)

    pair = (2, h.shape[1], dp_a.shape[2])
    return pl.pallas_call(
        body, name="weight_grad_in_low",
        out_shape=jax.ShapeDtypeStruct(pair, BF16),
        grid_spec=pltpu.PrefetchScalarGridSpec(
            num_scalar_prefetch=1, grid=(LOW_OWNERS, n_k),
            in_specs=[pl.BlockSpec((tk, h.shape[1]), lambda t, k, o: (k, 0)),
                      pl.BlockSpec((1, tk, dp_a.shape[2]), lambda t, k, o: (o[t], k, 0))],
            out_specs=pl.BlockSpec(pair, lambda t, k, o: (0, 0, 0)),
            scratch_shapes=[pltpu.VMEM(pair[1:], F32), pltpu.VMEM(pair, BF16), pltpu.VMEM(pair, BF16),
                            pltpu.SemaphoreType.DMA((2,)), pltpu.SemaphoreType.DMA((2,))]),
        compiler_params=_params(("arbitrary", "arbitrary")),
    )(order, h, dp_a)


def _exchange_small(misc):
    def body(m_ref, out_ref, send_sems, recv_sems, local_sem):
        start, finish = _scatter_slabs(m_ref, 0, out_ref, send_sems, recv_sems, local_sem)
        start()
        finish()

    return pl.pallas_call(
        body, name="exchange_small", out_shape=jax.ShapeDtypeStruct(misc.shape, misc.dtype),
        in_specs=[ANY], out_specs=ANY, scratch_shapes=list(SCATTER_SEMS),
    )(misc)


def _scatter_low(part_ref, recv_ref, send_sems, recv_sems, local_sem):
    x, y, c = lax.axis_index("x"), lax.axis_index("y"), lax.axis_index("c")
    chip = 2 * x + y
    routes = ((0, (0, 0, c), 0, None), (1, (0, 1, 0), 1, 0))

    def each(on_send, on_local, on_arrival):
        for slot, owner, owner_chip, core in routes:
            holds = (c == core) if core is not None else (c >= 0)
            rel = jnp.bitwise_xor(chip, owner_chip)

            @pl.when(holds & (rel != 0))
            def _(slot=slot, owner=owner, rel=rel):
                on_send(pltpu.make_async_remote_copy(
                    src_ref=part_ref.at[slot], dst_ref=recv_ref.at[chip], send_sem=send_sems.at[slot],
                    recv_sem=recv_sems.at[rel - 1], device_id=owner, device_id_type=MESH))

            @pl.when(holds & (rel == 0))
            def _(slot=slot, owner=owner, owner_chip=owner_chip):
                on_local(pltpu.make_async_copy(part_ref.at[slot], recv_ref.at[chip], local_sem))
                if on_arrival is not None:
                    for r in range(1, 4):
                        on_arrival(pltpu.make_async_remote_copy(
                            src_ref=part_ref.at[slot], dst_ref=recv_ref.at[r ^ owner_chip], send_sem=send_sems.at[slot],
                            recv_sem=recv_sems.at[r - 1], device_id=owner, device_id_type=MESH))

    start = lambda: each(lambda cp: cp.start(), lambda cp: cp.start(), None)
    finish = lambda: each(lambda cp: cp.wait_send(), lambda cp: cp.wait(), lambda cp: cp.wait_recv())
    return start, finish


def _adam_update(g, w, m, v):
    mn = ADAM_B1 * m + (1.0 - ADAM_B1) * g
    vn = ADAM_B2 * v + (1.0 - ADAM_B2) * (g * g)
    m_hat = mn / (1.0 - ADAM_B1 ** ADAM_STEP)
    v_hat = vn / (1.0 - ADAM_B2 ** ADAM_STEP)
    return -ADAM_LR * (m_hat / (jnp.sqrt(v_hat) + ADAM_EPS) + ADAM_WD * w), mn, vn


def _device_sum(parts):
    t = [p.astype(F32) for p in parts]
    return ((t[0] + t[1]) + (t[2] + t[3])) + ((t[4] + t[5]) + (t[6] + t[7]))


def _finish(a_hi, a_lo, n_lo, b3, misc, big, rows3, small):
    n_steps = 4
    tb, tr3 = 1024 // n_steps, 128 // n_steps

    def body(*refs):
        it = iter(refs)
        hi_ref, lo_ref, b_ref, m_ref = next(it), next(it), next(it), next(it)
        big_in = [next(it) for _ in range(3)]
        rows_in = [[next(it) for _ in range(3)] for _ in rows3]
        small_in = [[next(it) for _ in range(3)] for _ in small]
        big_out = [next(it) for _ in range(4)]
        rows_out = [[next(it) for _ in range(4)] for _ in rows3]
        small_out = [[next(it) for _ in range(4)] for _ in small]
        loss_ref = next(it)

        def apply(g, ins, outs):
            d, mn, vn = _adam_update(g, ins[0][...], ins[1][...], ins[2][...])
            for r, val in zip(outs, (g, d, mn, vn)):
                r[...] = val

        lo = [lo_ref[s].astype(F32) for s in range(4)]
        g_big = jnp.where(_me() < n_lo, (lo[0] + lo[1]) + (lo[2] + lo[3]), _device_sum([hi_ref[s] for s in range(N_DEV)]))
        apply(g_big[None], big_in, big_out)
        for k in range(len(rows3)):
            apply(_device_sum([b_ref[s, k] for s in range(N_DEV)])[None], rows_in[k], rows_out[k])

        @pl.when(pl.program_id(0) == 0)
        def _():
            loss_ref[...] = _device_sum([m_ref[s, MISC_LOSS:MISC_LOSS + 1, :] for s in range(N_DEV)])
            for (row0, lanes, ins), r_in, r_out in zip(small, small_in, small_out):
                n = ins[0].shape[0]
                apply(_device_sum([m_ref[s, row0:row0 + n, :lanes] for s in range(N_DEV)]), r_in, r_out)

    whole = lambda shape: pl.BlockSpec(shape, lambda i: (0,) * len(shape))
    big_blk = pl.BlockSpec((1, tb, 1024), lambda i: (0, i, 0))
    rows_blk = pl.BlockSpec((1, tr3, 1024), lambda i: (0, i, 0))
    in_specs = [pl.BlockSpec((N_DEV, tb, 1024), lambda i: (0, i, 0)), pl.BlockSpec((4, tb, 1024), lambda i: (0, i, 0)),
                pl.BlockSpec((N_DEV, 3, tr3, 1024), lambda i: (0, 0, i, 0)), whole(misc.shape)]
    in_specs += [big_blk] * 3 + [rows_blk] * (3 * len(rows3))
    out_specs = [big_blk] * 4 + [rows_blk] * (4 * len(rows3))
    out_shape = [jax.ShapeDtypeStruct(big[0].shape, F32)] * 4
    for w, _, _ in rows3:
        out_shape += [jax.ShapeDtypeStruct(w.shape, F32)] * 4
    args = [a_hi, a_lo, b3, misc, *big]
    for t in rows3:
        args += list(t)
    for _, _, t in small:
        in_specs += [whole(t[0].shape)] * 3
        out_specs += [whole(t[0].shape)] * 4
        out_shape += [jax.ShapeDtypeStruct(t[0].shape, F32)] * 4
        args += list(t)
    out_specs.append(whole((1, 1024)))
    out_shape.append(jax.ShapeDtypeStruct((1, 1024), F32))
    outs = pl.pallas_call(
        body, name="reduce_sum_adamw", out_shape=tuple(out_shape), grid=(n_steps,),
        in_specs=in_specs, out_specs=tuple(out_specs),
        compiler_params=_params(("arbitrary",)),
    )(*args)
    return [tuple(outs[4 * k:4 * k + 4]) for k in range(len(outs) // 4)], outs[-1][0, 0]


GATHER_UNITS = ((0, 0), (0, 1), (1, 0), (1, 1)) + tuple((place, 0) for place in range(2, 8)) + tuple(
    (place, 1) for place in range(2, 8))


def _gather_order():
    x, y, c = lax.axis_index("x"), lax.axis_index("y"), lax.axis_index("c")
    chips = [(1 - x, y), (x, 1 - y), (1 - x, 1 - y)]
    order = [_device_index(x, y, c), _device_index(x, y, 1 - c)]
    order += [_device_index(*q, c) for q in chips] + [_device_index(*q, 1 - c) for q in chips]
    return order


def _gather_units():
    order = _gather_order()
    blocks = jnp.stack([order[place] for place, _ in GATHER_UNITS]).astype(jnp.int32)
    return blocks, jnp.array([half for _, half in GATHER_UNITS], jnp.int32)


def _in_projection(zp, norm_w, w_shard, b_blocks, units):
    n_rows = zp.shape[0]
    tr = _row_tile(n_rows, 1040)
    nt = n_rows // tr

    half_cols = 1024 // 2
    n_units = len(GATHER_UNITS)

    def body(blocks_ref, halves_ref, z_ref, nw_ref, w_hbm, b_ref, p_ref, h_ref, w_out, w_vmem, h_all,
             send_sems, recv_sems, local_sem, out_sems):
        s, i = pl.program_id(0), pl.program_id(1)
        x, y, c = lax.axis_index("x"), lax.axis_index("y"), lax.axis_index("c")
        me, sibling = (x, y, c), (x, y, 1 - c)
        chips = [(1 - x, y), (x, 1 - y), (1 - x, 1 - y)]

        def slot(blk, half):
            return w_vmem.at[_device_index(*blk), half]

        def mine(half):
            return w_hbm.at[:, pl.ds(half * half_cols, half_cols)]

        def copy(k, half, blk, to, own_block=False):
            return pltpu.make_async_remote_copy(
                src_ref=mine(half) if own_block else slot(blk, half), dst_ref=slot(blk, half),
                send_sem=send_sems.at[2 * k + half], recv_sem=recv_sems.at[2 * k + half], device_id=to, device_id_type=MESH)

        own = [pltpu.make_async_copy(mine(half), slot(me, half), local_sem.at[half]) for half in range(2)]
        first = [copy(k, half, me, to, own_block=True) for half in range(2)
                 for k, to in enumerate([sibling] + [(*q, c) for q in chips])]
        passed = [[copy(4 + j, half, (*q, c), sibling) for j, q in enumerate(chips)] for half in range(2)]
        sources = [None, sibling] + [(*q, c) for q in chips] + [(*q, 1 - c) for q in chips]
        arrival = lambda place, half: copy(place - 1, half, sources[place], me)

        def keep(unit):
            blk, half = blocks_ref[unit], halves_ref[unit]
            return pltpu.make_async_copy(w_vmem.at[blk, half], w_out.at[blk, half], out_sems.at[unit])

        for unit, (place, half) in enumerate(GATHER_UNITS):
            @pl.when((i == 0) & (s == unit))
            def _(unit=unit, place=place, half=half):
                if unit == 0:
                    for cp in own + first:
                        cp.start()
                if place == 0:
                    own[half].wait()
                else:
                    arrival(place, half).wait_recv()
                    if 2 <= place <= 4:
                        passed[half][place - 2].start()
                keep(unit).start()

        @pl.when(s == 0)
        def _():
            z = z_ref[...]
            r = lax.rsqrt(jnp.mean(z * z, axis=-1, keepdims=True) + EPS)
            h = (z * r * nw_ref[...]).astype(BF16)
            h_all[i] = h
            h_ref[...] = h

        p_ref[0] = jnp.dot(h_all[i], w_vmem[blocks_ref[s], halves_ref[s]], preferred_element_type=F32) + b_ref[0]

        @pl.when((s == n_units - 1) & (i == nt - 1))
        def _():
            for cp in first + passed[0] + passed[1]:
                cp.wait_send()
            for unit in range(n_units):
                keep(unit).wait()

    first_pass = lambda s, i, rest: jnp.where(s == 0, i, rest)
    return pl.pallas_call(
        body, name="in_projection_gather",
        out_shape=(jax.ShapeDtypeStruct((N_COLBLK, n_rows, 1024), F32),
                   jax.ShapeDtypeStruct((n_rows, D_MODEL), BF16),
                   jax.ShapeDtypeStruct((N_DEV, 2, D_MODEL, half_cols), BF16)),
        grid_spec=pltpu.PrefetchScalarGridSpec(
            num_scalar_prefetch=2, grid=(n_units, nt),
            in_specs=[pl.BlockSpec((tr, D_MODEL), lambda s, i, blk, hf: (first_pass(s, i, 0), 0)),
                      pl.BlockSpec((1, D_MODEL), lambda s, i, blk, hf: (0, 0)), ANY,
                      pl.BlockSpec((1, 1, half_cols), lambda s, i, blk, hf: (blk[s], 0, hf[s]))],
            out_specs=(pl.BlockSpec((1, tr, half_cols), lambda s, i, blk, hf: (blk[s], i, hf[s])),
                       pl.BlockSpec((tr, D_MODEL), lambda s, i, blk, hf: (first_pass(s, i, nt - 1), 0)), ANY),
            scratch_shapes=[pltpu.VMEM((N_DEV, 2, D_MODEL, half_cols), BF16), pltpu.VMEM((nt, tr, D_MODEL), BF16),
                            pltpu.SemaphoreType.DMA((14,)), pltpu.SemaphoreType.DMA((14,)), pltpu.SemaphoreType.DMA((2,)),
                            pltpu.SemaphoreType.DMA((n_units,))]),
        compiler_params=_params(("arbitrary", "arbitrary")),
    )(*units, zp, norm_w, w_shard, b_blocks)


def _lower_bound(lb_ref):
    l0, l1 = lb_ref[0:1, :], lb_ref[1:2, :]
    _, lb = _sigmoid_pair(l1 - l0)
    return lb


def _chunk_gates(fz, lb, valid):
    sig, nsig = _sigmoid_pair(fz)
    f = lb + (1.0 - lb) * sig
    g2 = jnp.where(valid, jnp.log2(f), 0.0)
    k = jnp.where(valid, (1.0 - lb) * nsig, 0.0)
    return sig, nsig, f, g2, k


def _tri(n, upper=False):
    r = lax.broadcasted_iota(jnp.int32, (n, n), 0)
    c = lax.broadcasted_iota(jnp.int32, (n, n), 1)
    return jnp.where((r <= c) if upper else (r >= c), 1.0, 0.0).astype(BF16)


def _tri_dot(tri, x):
    hi = x.astype(BF16)
    rest = x - hi.astype(F32)
    mid = rest.astype(BF16)
    low = (rest - mid.astype(F32)).astype(BF16)
    return (jnp.dot(tri, hi, preferred_element_type=F32) + jnp.dot(tri, mid, preferred_element_type=F32)
            + jnp.dot(tri, low, preferred_element_type=F32))


def _intra_scores(q_ref, k_ref, b2_ref, col0):
    cols = pl.ds(col0, HEAD_DIM)
    rows_s = lax.broadcasted_iota(jnp.int32, (SUB, 1), 0)
    lanes_c = lax.broadcasted_iota(jnp.int32, (1, CHUNK), 1)
    blocks = []
    for i in range(N_SUB):
        lo = i * SUB
        qi = q_ref[lo:lo + SUB, cols]
        bi = b2_ref[lo:lo + SUB, cols]
        if i == 0:
            acc = jnp.zeros((SUB, CHUNK), F32)
        else:
            ref_i = b2_ref[lo:lo + 1, cols]
            qt = qi * jnp.exp2(bi - ref_i)
            kt = jnp.concatenate([k_ref[0:lo, cols] * jnp.exp2(ref_i - b2_ref[0:lo, cols]),
                                  jnp.zeros((CHUNK - lo, HEAD_DIM), F32)], axis=0)
            acc = _dot_nt(qt, kt)
        for s in range(SUB):
            b_s = b2_ref[lo + s:lo + s + 1, cols]
            k_s = k_ref[lo + s:lo + s + 1, cols]
            w = jnp.exp2(jnp.minimum(bi - b_s, 0.0))
            col = jnp.sum((qi * w) * k_s, axis=-1, keepdims=True)
            acc = jnp.where(lanes_c == lo + s, col, acc)
        blocks.append(jnp.where(lanes_c <= lo + rows_s, acc, 0.0))
    return jnp.concatenate(blocks, axis=0)


def _hgrn_forward(p, lb_logits, w_rest):
    n_rows = p.shape[1]
    n_chunks = n_rows // CHUNK
    width = HEADS * HEAD_DIM
    per_step = next(c for c in (5, 3, 1) if n_chunks % c == 0)
    n_steps = n_chunks // per_step
    span = per_step * CHUNK

    def body(q_ref, fz_ref, v_ref, lb_ref, rest_ref, o_ref, st_out_ref, a_out_ref, rest_out,
             state, k_all, b2_all, send_sems, recv_sems, local_sem):
        n = pl.program_id(0)
        own, sends, arrivals = _direct_gather(rest_ref, rest_out, send_sems, recv_sems, local_sem)

        @pl.when(n == 0)
        def _():
            state[...] = jnp.zeros_like(state)
            own.start()
            for cp in sends:
                cp.start()

        lb = _lower_bound(lb_ref)
        for cc in range(per_step):
            r0 = cc * CHUNK
            k_vmem, b2_vmem = k_all.at[cc], b2_all.at[cc]
            rows = (n * per_step + cc) * CHUNK + lax.broadcasted_iota(jnp.int32, (CHUNK, 1), 0)
            valid = rows >= PAD_ROWS
            _, _, _, g2, k = _chunk_gates(fz_ref[0, r0:r0 + CHUNK, :], lb, valid)
            k_vmem[...] = k
            b2_vmem[...] = _tri_dot(_tri(CHUNK), g2)
            q_view = q_ref.at[0, pl.ds(r0, CHUNK), :]
            for h in range(HEADS):
                cols = pl.ds(h * HEAD_DIM, HEAD_DIM)
                st = state[h]
                st_out_ref[cc, h] = st
                bh = b2_vmem[:, cols]
                kh = k_vmem[:, cols]
                vh = jnp.where(valid, v_ref[0, r0:r0 + CHUNK, cols], 0.0)
                qe = q_view[:, cols] * jnp.exp2(bh)
                a = _intra_scores(q_view, k_vmem, b2_vmem, h * HEAD_DIM).astype(BF16)
                a_out_ref[cc, h] = a
                o_ref[r0:r0 + CHUNK, cols] = _dot_nt(qe, st) + _dot(a, vh)
                b_last = b2_vmem[CHUNK - 1:CHUNK, cols]
                kd = kh * jnp.exp2(b_last - bh)
                state[h] = st * jnp.exp2(b_last) + _dot_tn(vh, kd)

        @pl.when(n == n_steps - 1)
        def _():
            for cp in arrivals:
                cp.wait_recv()
            for cp in sends:
                cp.wait_send()
            own.wait()

    blk = lambda c: pl.BlockSpec((1, span, width), lambda n, c=c: (c, n, 0))
    return pl.pallas_call(
        body, name="hgrn_forward",
        out_shape=(jax.ShapeDtypeStruct((n_rows, width), F32),
                   jax.ShapeDtypeStruct((n_chunks, HEADS, HEAD_DIM, HEAD_DIM), F32),
                   jax.ShapeDtypeStruct((n_chunks, HEADS, CHUNK, CHUNK), BF16),
                   jax.ShapeDtypeStruct((N_DEV,) + w_rest.shape, w_rest.dtype)),
        grid=(n_steps,),
        in_specs=[blk(0), blk(1), blk(2), pl.BlockSpec((2, width), lambda n: (0, 0)), ANY],
        out_specs=(pl.BlockSpec((span, width), lambda n: (n, 0)),
                   pl.BlockSpec((per_step, HEADS, HEAD_DIM, HEAD_DIM), lambda n: (n, 0, 0, 0)),
                   pl.BlockSpec((per_step, HEADS, CHUNK, CHUNK), lambda n: (n, 0, 0, 0)), ANY),
        scratch_shapes=[pltpu.VMEM((HEADS, HEAD_DIM, HEAD_DIM), F32), pltpu.VMEM((per_step, CHUNK, width), F32),
                        pltpu.VMEM((per_step, CHUNK, width), F32)] + list(GATHER_SEMS),
        compiler_params=_params(("arbitrary",)),
    )(p, p, p, lb_logits, w_rest)


def _hgrn_backward(p, lb_logits, states, scores, d_o, g_slabs, first_owner, g_rows):
    n_rows = p.shape[1]
    n_chunks = n_rows // CHUNK
    width = HEADS * HEAD_DIM

    def body(q_ref, fz_ref, v_ref, lb_ref, st_ref, a_ref, do_ref, gs_hbm, gr_hbm, dp_ref, dbias_ref, dlb_ref, rs_hbm, rr_hbm,
             dstate, k_vmem, b2_vmem, *sems):
        step = pl.program_id(0)
        n = n_chunks - 1 - step
        start_slabs, finish_slabs = _scatter_slabs(gs_hbm, first_owner, rs_hbm, *sems[:3])
        start_rows, finish_rows = _scatter_rows(gr_hbm, rr_hbm, *sems[3:])

        @pl.when(step == 0)
        def _():
            dstate[...] = jnp.zeros_like(dstate)
            dbias_ref[...] = jnp.zeros_like(dbias_ref)
            dlb_ref[...] = jnp.zeros_like(dlb_ref)
            start_slabs()
            start_rows()

        rows = n * CHUNK + lax.broadcasted_iota(jnp.int32, (CHUNK, 1), 0)
        valid = rows >= PAD_ROWS
        lb = _lower_bound(lb_ref)
        sig, nsig, f, g2, k = _chunk_gates(fz_ref[0], lb, valid)
        k_vmem[...] = k
        b2_vmem[...] = _tri_dot(_tri(CHUNK), g2)
        rows_c = lax.broadcasted_iota(jnp.int32, (CHUNK, 1), 0)
        rows_s = lax.broadcasted_iota(jnp.int32, (SUB, 1), 0)
        lanes_c = lax.broadcasted_iota(jnp.int32, (1, CHUNK), 1)
        causal = lax.broadcasted_iota(jnp.int32, (CHUNK, CHUNK), 0) >= lax.broadcasted_iota(jnp.int32, (CHUNK, CHUNK), 1)
        tri_up = _tri(CHUNK, upper=True)
        for h in range(HEADS):
            cols = pl.ds(h * HEAD_DIM, HEAD_DIM)
            st = st_ref[0, h]
            dst = dstate[h]
            qh = q_ref[0, :, cols]
            bh = b2_vmem[:, cols]
            kh = k_vmem[:, cols]
            vh = jnp.where(valid, v_ref[0, :, cols], 0.0)
            doh = do_ref[:, cols]
            eb = jnp.exp2(bh)
            qe = qh * eb
            b_last = b2_vmem[CHUNK - 1:CHUNK, cols]
            e_last = jnp.exp2(b_last)
            decay_k = jnp.exp2(b_last - bh)
            kd = kh * decay_k
            dqe = _dot(doh, st)
            da = jnp.where(causal, _dot_nt(doh, vh), 0.0)
            dv = _dot_tn(a_ref[0, h], doh) + _dot_nt(kd, dst)
            dkd = _dot(vh, dst)
            dstate[h] = dst * e_last + _dot_tn(doh, qe)
            db_last = (jnp.sum(dst * st, axis=0, keepdims=True) * e_last
                       + jnp.sum(dkd * kd, axis=0, keepdims=True))
            dq_blocks, dk_blocks = [], []
            dk_earlier = jnp.zeros((CHUNK, HEAD_DIM), F32)
            for i in range(N_SUB):
                lo = i * SUB
                qi = q_ref[0, lo:lo + SUB, cols]
                bi = b2_vmem[lo:lo + SUB, cols]
                da_i = da[lo:lo + SUB, :]
                if i == 0:
                    dq_i = jnp.zeros((SUB, HEAD_DIM), F32)
                else:
                    ref_i = b2_vmem[lo:lo + 1, cols]
                    eq = jnp.exp2(bi - ref_i)
                    ek = jnp.exp2(ref_i - b2_vmem[0:lo, cols])
                    later = jnp.zeros((CHUNK - lo, HEAD_DIM), F32)
                    kt = jnp.concatenate([k_vmem[0:lo, cols] * ek, later], axis=0)
                    dq_i = _dot(da_i, kt) * eq
                    dk_earlier = dk_earlier + jnp.concatenate([_dot_tn(da_i, qi * eq)[0:lo] * ek, later], axis=0)
                dk_i = jnp.zeros((SUB, HEAD_DIM), F32)
                for s in range(SUB):
                    b_s = b2_vmem[lo + s:lo + s + 1, cols]
                    k_s = k_vmem[lo + s:lo + s + 1, cols]
                    w = jnp.exp2(jnp.minimum(bi - b_s, 0.0))
                    da_col = jnp.sum(jnp.where(lanes_c == lo + s, da_i, 0.0), axis=-1, keepdims=True)
                    gw = da_col * w
                    dq_i = dq_i + gw * k_s
                    dk_i = jnp.where(rows_s == s, jnp.sum(gw * qi, axis=0, keepdims=True), dk_i)
                dq_blocks.append(dq_i)
                dk_blocks.append(dk_i)
            dq_intra = jnp.concatenate(dq_blocks, axis=0)
            dk_intra = jnp.concatenate(dk_blocks, axis=0) + dk_earlier
            dq = dqe * eb + dq_intra
            dk = dkd * decay_k + dk_intra
            db = dqe * qe - dkd * kd + qh * dq_intra - kh * dk_intra
            db = db + jnp.where(rows_c == CHUNK - 1, db_last, 0.0)
            dg = _tri_dot(tri_up, db)
            fh = f[:, h * HEAD_DIM:(h + 1) * HEAD_DIM]
            sh = sig[:, h * HEAD_DIM:(h + 1) * HEAD_DIM]
            nh = nsig[:, h * HEAD_DIM:(h + 1) * HEAD_DIM]
            lbh = lb[:, h * HEAD_DIM:(h + 1) * HEAD_DIM]
            df = jnp.where(valid, dg / fh - dk, 0.0)
            dfz = df * (1.0 - lbh) * sh * nh
            dq = jnp.where(valid, dq, 0.0)
            dv = jnp.where(valid, dv, 0.0)
            dlb_ref[:, cols] += jnp.sum(df * nh, axis=0, keepdims=True)
            dp_ref[0, :, cols] = dq.astype(BF16)
            dp_ref[1, :, cols] = dfz.astype(BF16)
            dp_ref[2, :, cols] = dv.astype(BF16)
            dbias_ref[0, :, cols] += jnp.sum(dq, axis=0, keepdims=True)
            dbias_ref[1, :, cols] += jnp.sum(dfz, axis=0, keepdims=True)
            dbias_ref[2, :, cols] += jnp.sum(dv, axis=0, keepdims=True)

        @pl.when(step == n_chunks - 1)
        def _():
            finish_slabs()
            finish_rows()

    rev = lambda s: n_chunks - 1 - s
    blk = lambda c: pl.BlockSpec((1, CHUNK, width), lambda s, c=c: (c, rev(s), 0))
    return pl.pallas_call(
        body, name="hgrn_backward",
        out_shape=(jax.ShapeDtypeStruct((3, n_rows, width), BF16),
                   jax.ShapeDtypeStruct((3, 1, width), F32),
                   jax.ShapeDtypeStruct((1, width), F32),
                   jax.ShapeDtypeStruct((N_DEV,) + g_slabs.shape[1:], g_slabs.dtype),
                   jax.ShapeDtypeStruct((N_DEV, N_ROW_GRADS, 128, g_rows.shape[2]), g_rows.dtype)),
        grid=(n_chunks,),
        in_specs=[blk(0), blk(1), blk(2), pl.BlockSpec((2, width), lambda s: (0, 0)),
                  pl.BlockSpec((1, HEADS, HEAD_DIM, HEAD_DIM), lambda s: (rev(s), 0, 0, 0)),
                  pl.BlockSpec((1, HEADS, CHUNK, CHUNK), lambda s: (rev(s), 0, 0, 0)),
                  pl.BlockSpec((CHUNK, width), lambda s: (rev(s), 0)), ANY, ANY],
        out_specs=(pl.BlockSpec((3, CHUNK, width), lambda s: (0, rev(s), 0)),
                   pl.BlockSpec((3, 1, width), lambda s: (0, 0, 0)),
                   pl.BlockSpec((1, width), lambda s: (0, 0)), ANY, ANY),
        scratch_shapes=[pltpu.VMEM((HEADS, HEAD_DIM, HEAD_DIM), F32), pltpu.VMEM((CHUNK, width), F32),
                        pltpu.VMEM((CHUNK, width), F32)] + list(SCATTER_SEMS) + list(SCATTER_ROWS_SEMS),
        compiler_params=_params(("arbitrary",)),
    )(p, p, p, lb_logits, states, scores, d_o, g_slabs, g_rows)


def _sigmoid_and_complement(x):
    s = 0.5 * jnp.tanh(0.5 * x) + 0.5
    return s, 1.0 - s


def _silu_and_grad(x):
    s, ns = _sigmoid_and_complement(x)
    return x * s, s * (1.0 + x * ns)


def _tail(p, o, zp, tgt, hg_norm_w, pool_w, pool_scale, w_down_hg, w_down_pool, w_out, final_norm_w):
    n_rows = zp.shape[0]
    tr = _row_tile(n_rows, 208)
    nt = n_rows // tr
    ext = tr + HALO
    n_groups = len(POOL_WINDOWS)

    def body(o_ref, ghg_ref, u_ref, gpool_ref, mhg_ref, mpool_ref, uhalo_ref, z_ref, tgt_hbm,
             hgw_ref, pw_ref, ps_ref, wdh_ref, wdp_ref, wout_ref, fnw_ref,
             do_ref, dp_ref, dz2_ref, lhs_ref, rhs_ref,
             dbias_ref, dhgw_ref, dpw_ref, dps_ref, dfnw_ref, loss_ref, halo_vmem, tgt_buf, tgt_sems):
        step = pl.program_id(0)
        ti = nt - 1 - step

        def target_rows(tile, slot, act):
            @pl.when(tile == 0)
            def _():
                cp = pltpu.make_async_copy(tgt_hbm.at[pl.ds(0, tr - CHUNK), :], tgt_buf.at[slot, pl.ds(CHUNK, tr - CHUNK), :],
                                           tgt_sems.at[slot])
                getattr(cp, act)()

            @pl.when(tile > 0)
            def _():
                cp = pltpu.make_async_copy(tgt_hbm.at[pl.ds(tile * tr - CHUNK, tr), :], tgt_buf.at[slot], tgt_sems.at[slot])
                getattr(cp, act)()

        @pl.when(step == 0)
        def _():
            halo_vmem[...] = jnp.zeros_like(halo_vmem)
            for r in (dbias_ref, dhgw_ref, dpw_ref, dps_ref, dfnw_ref, loss_ref):
                r[...] = jnp.zeros_like(r)
            if nt <= 2:
                tgt_buf[(nt - 1) % 2, 0:CHUNK, :] = jnp.zeros((CHUNK, D_MODEL), F32)
            target_rows(ti, 0, "start")

        @pl.when(ti > 0)
        def _():
            target_rows(ti - 1, (step + 1) % 2, "start")

        rows = ti * tr + lax.broadcasted_iota(jnp.int32, (tr, 1), 0)
        valid = rows >= PAD_ROWS
        in_loss = rows >= CHUNK
        count_pos = jnp.maximum(rows - PAD_ROWS + 1, 1).astype(F32)

        o = o_ref[...]
        hgw = hgw_ref[...]
        inv_o, on_parts = [], []
        for h in range(HEADS):
            oh = o[:, h * HEAD_DIM:(h + 1) * HEAD_DIM]
            r = lax.rsqrt(jnp.mean(oh * oh, axis=-1, keepdims=True) + EPS)
            inv_o.append(r)
            on_parts.append(oh * r)
        o_hat = jnp.concatenate(on_parts, axis=1)
        o_n = o_hat * hgw
        g_hg = ghg_ref[0]
        silu_hg, dsilu_hg = _silu_and_grad(g_hg)
        a_hg = o_n * silu_hg
        y_hg = _dot(a_hg, wdh_ref[...])

        u = jnp.where(valid, u_ref[0], 0.0)
        u_prev = jnp.where(ti > 0, uhalo_ref[0], 0.0)
        u_ext = jnp.concatenate([u_prev, u], axis=0)
        pooled_parts, mixed_parts, inv_cnt = [], [], []
        for gi, win in enumerate(POOL_WINDOWS):
            lanes = slice(gi * POOL_GDIM, (gi + 1) * POOL_GDIM)
            s = u_ext[:, lanes]
            shift = 1
            while shift < win:
                s = s + pltpu.roll(s, shift, 0)
                shift *= 2
            ic = 1.0 / jnp.minimum(count_pos, float(win))
            inv_cnt.append(ic)
            pooled = s[HALO:] * ic - u[:, lanes]
            pooled_parts.append(pooled)
            mixed_parts.append(_dot(pooled, pw_ref[gi]))
        mixed = jnp.concatenate(mixed_parts, axis=1)
        ps = ps_ref[...]
        g_pool = gpool_ref[0]
        silu_pool, dsilu_pool = _silu_and_grad(g_pool)
        a_pool = mixed * ps * silu_pool
        y_pool = _dot(a_pool, wdp_ref[...])

        m_hg, m_pool = mhg_ref[0], mpool_ref[0]
        s_hg, ns_hg = _sigmoid_and_complement(m_hg)
        s_pool, ns_pool = _sigmoid_and_complement(m_pool)
        merged = s_hg * y_hg + s_pool * y_pool
        z2 = z_ref[...] + _dot(merged, wout_ref[...])
        r2 = lax.rsqrt(jnp.mean(z2 * z2, axis=-1, keepdims=True) + EPS)
        n2 = z2 * r2
        fnw = fnw_ref[...]
        target_rows(ti, step % 2, "wait")
        err = jnp.where(in_loss, n2 * fnw - tgt_buf[step % 2], 0.0)
        loss_ref[...] += jnp.sum(jnp.sum(err * err, axis=0, keepdims=True), axis=1, keepdims=True) * (0.5 / D_MODEL)
        dy = err * (1.0 / D_MODEL)

        dfnw_ref[...] += jnp.sum(dy * n2, axis=0, keepdims=True)
        gy = dy * fnw
        dz2 = r2 * (gy - n2 * jnp.mean(gy * n2, axis=-1, keepdims=True))
        dmerged = _dot_nt(dz2, wout_ref[...])
        dy_hg = s_hg * dmerged
        dy_pool = s_pool * dmerged
        dm_hg = dmerged * y_hg * s_hg * ns_hg
        dm_pool = dmerged * y_pool * s_pool * ns_pool
        da_hg = _dot_nt(dy_hg, wdh_ref[...])
        da_pool = _dot_nt(dy_pool, wdp_ref[...])

        d_on = da_hg * silu_hg
        dg_hg = da_hg * o_n * dsilu_hg
        dhgw_ref[...] += jnp.sum(d_on * o_hat, axis=0, keepdims=True)
        gyo = d_on * hgw
        do_parts = []
        for h in range(HEADS):
            lanes = slice(h * HEAD_DIM, (h + 1) * HEAD_DIM)
            gh, nh = gyo[:, lanes], o_hat[:, lanes]
            do_parts.append(inv_o[h] * (gh - nh * jnp.mean(gh * nh, axis=-1, keepdims=True)))
        do_ref[...] = jnp.concatenate(do_parts, axis=1)

        dmixed = da_pool * ps * silu_pool
        dps_ref[...] += jnp.sum(da_pool * mixed * silu_pool, axis=0, keepdims=True)
        dg_pool = da_pool * mixed * ps * dsilu_pool
        du_parts = []
        for gi, win in enumerate(POOL_WINDOWS):
            lanes = slice(gi * POOL_GDIM, (gi + 1) * POOL_GDIM)
            dmx = dmixed[:, lanes]
            dpooled = _dot_nt(dmx, pw_ref[gi])
            dpw_ref[gi] += _dot_tn(pooled_parts[gi], dmx)
            dpt = dpooled * inv_cnt[gi]
            s = jnp.concatenate([dpt, halo_vmem[:, lanes]], axis=0)
            shift = 1
            while shift < win:
                s = s + pltpu.roll(s, ext - shift, 0)
                shift *= 2
            du_parts.append(s[:tr] - dpooled)
            halo_vmem[:, lanes] = dpt[:HALO]
        du = jnp.where(valid, jnp.concatenate(du_parts, axis=1), 0.0)

        for c, val in enumerate((dg_hg, du, dg_pool, dm_hg, dm_pool)):
            dp_ref[c] = val.astype(BF16)
            dbias_ref[c] += jnp.sum(val, axis=0, keepdims=True)
        dz2_ref[...] = dz2
        for c, (lhs, rhs) in enumerate(((merged, dz2), (a_hg, dy_hg), (a_pool, dy_pool))):
            lhs_ref[c] = lhs.astype(BF16)
            rhs_ref[c] = rhs.astype(BF16)

    rev = lambda s: nt - 1 - s
    rowblk = pl.BlockSpec((tr, D_MODEL), lambda s: (rev(s), 0))
    pblk = lambda c: pl.BlockSpec((1, tr, 1024), lambda s, c=c: (c, rev(s), 0))
    halo_blk = pl.BlockSpec((1, HALO, 1024), lambda s: (4, jnp.maximum(rev(s) * (tr // HALO) - 1, 0), 0))
    full = lambda shape: pl.BlockSpec(shape, lambda s: (0,) * len(shape))
    vec = full((1, D_MODEL))
    mat = full((D_MODEL, D_MODEL))
    act3 = jax.ShapeDtypeStruct((3, n_rows, D_MODEL), BF16)
    act3_blk = pl.BlockSpec((3, tr, D_MODEL), lambda s: (0, rev(s), 0))
    return pl.pallas_call(
        body, name="tail_forward_backward",
        out_shape=(jax.ShapeDtypeStruct((n_rows, D_MODEL), F32),
                   jax.ShapeDtypeStruct((5, n_rows, 1024), BF16),
                   jax.ShapeDtypeStruct((n_rows, D_MODEL), F32),
                   act3, act3,
                   jax.ShapeDtypeStruct((5, 1, 1024), F32),
                   jax.ShapeDtypeStruct((1, D_MODEL), F32),
                   jax.ShapeDtypeStruct((n_groups, POOL_GDIM, POOL_GDIM), F32),
                   jax.ShapeDtypeStruct((1, D_MODEL), F32),
                   jax.ShapeDtypeStruct((1, D_MODEL), F32),
                   jax.ShapeDtypeStruct((1, 1), F32)),
        grid=(nt,),
        in_specs=[rowblk, pblk(3), pblk(4), pblk(5), pblk(6), pblk(7), halo_blk, rowblk, ANY,
                  vec, full((n_groups, POOL_GDIM, POOL_GDIM)), vec, mat, mat, mat, vec],
        out_specs=(rowblk, pl.BlockSpec((5, tr, 1024), lambda s: (0, rev(s), 0)), rowblk,
                   act3_blk, act3_blk,
                   full((5, 1, 1024)), vec, full((n_groups, POOL_GDIM, POOL_GDIM)), vec, vec, full((1, 1))),
        scratch_shapes=[pltpu.VMEM((HALO, D_MODEL), F32), pltpu.VMEM((2, tr, D_MODEL), F32), pltpu.SemaphoreType.DMA((2,))],
        compiler_params=_params(("arbitrary",)),
    )(o, p, p, p, p, p, p, zp, tgt, hg_norm_w, pool_w, pool_scale, w_down_hg, w_down_pool, w_out, final_norm_w)


def _in_projection_backward(dp_a, dp_b, w_blocks, zp, dz2, norm_w, chip_sums):
    n_rows = zp.shape[0]
    tr = _row_tile(n_rows, 416)
    nt = n_rows // tr
    na, nb = dp_a.shape[0], dp_b.shape[0]

    def body(dpa_ref, dpb_ref, w_hbm, z_ref, dz2_ref, nw_ref, gs_hbm, gx_hbm, head_ref, dnw_ref, rs_hbm,
             w_vmem, sem, dz_buf, gx_sems, *sems):
        i = pl.program_id(0)
        start_slabs, finish_slabs = _scatter_low(gs_hbm, rs_hbm, *sems)

        def wait_rows_out(tile):
            @pl.when(tile == 0)
            def _():
                pltpu.make_async_copy(dz_buf.at[0, pl.ds(CHUNK, tr - CHUNK), :], gx_hbm.at[pl.ds(0, tr - CHUNK), :],
                                      gx_sems.at[0]).wait()

            @pl.when(tile > 0)
            def _():
                pltpu.make_async_copy(dz_buf.at[tile % 2], gx_hbm.at[pl.ds(tile * tr - CHUNK, tr), :],
                                      gx_sems.at[tile % 2]).wait()

        @pl.when(i == 0)
        def _():
            start_slabs()
            cp = pltpu.make_async_copy(w_hbm, w_vmem, sem)
            cp.start()
            cp.wait()
            dnw_ref[...] = jnp.zeros_like(dnw_ref)

        dh = jnp.zeros((tr, D_MODEL), F32)
        half_cols = w_vmem.shape[-1]
        for j in range(na + nb):
            dp_ref, jj = (dpa_ref, j) if j < na else (dpb_ref, j - na)
            for half in range(2):
                dh = dh + _dot_nt(dp_ref[jj, :, half * half_cols:(half + 1) * half_cols], w_vmem[j, half])
        z = z_ref[...]
        r = lax.rsqrt(jnp.mean(z * z, axis=-1, keepdims=True) + EPS)
        n1 = z * r
        dnw_ref[...] += jnp.sum(dh * n1, axis=0, keepdims=True)
        gh = dh * nw_ref[...]
        dz = dz2_ref[...] + r * (gh - n1 * jnp.mean(gh * n1, axis=-1, keepdims=True))

        @pl.when(i >= 2)
        def _():
            wait_rows_out(i - 2)

        dz_buf[i % 2] = dz

        @pl.when(i == 0)
        def _():
            head_ref[...] = dz[0:CHUNK]
            pltpu.make_async_copy(dz_buf.at[0, pl.ds(CHUNK, tr - CHUNK), :], gx_hbm.at[pl.ds(0, tr - CHUNK), :],
                                  gx_sems.at[0]).start()

        @pl.when(i > 0)
        def _():
            pltpu.make_async_copy(dz_buf.at[i % 2], gx_hbm.at[pl.ds(i * tr - CHUNK, tr), :], gx_sems.at[i % 2]).start()

        @pl.when(i == nt - 1)
        def _():
            if nt >= 2:
                wait_rows_out(i - 1)
            wait_rows_out(i)
            finish_slabs()

    rowblk = pl.BlockSpec((tr, D_MODEL), lambda i: (i, 0))
    vec = pl.BlockSpec((1, D_MODEL), lambda i: (0, 0))
    return pl.pallas_call(
        body, name="in_projection_backward",
        out_shape=(jax.ShapeDtypeStruct((n_rows - CHUNK, D_MODEL), F32), jax.ShapeDtypeStruct((CHUNK, D_MODEL), F32),
                   jax.ShapeDtypeStruct((1, D_MODEL), F32),
                   jax.ShapeDtypeStruct((4,) + chip_sums.shape[1:], chip_sums.dtype)),
        grid=(nt,),
        in_specs=[pl.BlockSpec((na, tr, 1024), lambda i: (0, i, 0)), pl.BlockSpec((nb, tr, 1024), lambda i: (0, i, 0)),
                  ANY, rowblk, rowblk, vec, ANY],
        out_specs=(ANY, pl.BlockSpec((CHUNK, D_MODEL), lambda i: (0, 0)), vec, ANY),
        scratch_shapes=[pltpu.VMEM(w_blocks.shape, w_blocks.dtype), pltpu.SemaphoreType.DMA(()),
                        pltpu.VMEM((2, tr, D_MODEL), F32), pltpu.SemaphoreType.DMA((2,)),
                        pltpu.SemaphoreType.DMA((2,)), pltpu.SemaphoreType.DMA((3,)), pltpu.SemaphoreType.DMA(())],
        compiler_params=_params(("arbitrary",)),
    )(dp_a, dp_b, w_blocks, zp, dz2, norm_w, chip_sums)


def _weight_grad(xs, ys, name):
    shared = xs.ndim == 2
    n_rows, m = xs.shape[-2:]
    nb, _, n = ys.shape
    tk = _row_tile(n_rows, 4160)
    n_k = n_rows // tk

    def body(x_ref, y_ref, o_ref, acc):
        k = pl.program_id(1)

        @pl.when(k == 0)
        def _():
            acc[...] = jnp.zeros_like(acc)

        acc[...] += _dot_tn(x_ref[...] if shared else x_ref[0], y_ref[0])

        @pl.when(k == n_k - 1)
        def _():
            o_ref[0] = acc[...].astype(o_ref.dtype)

    x_spec = pl.BlockSpec((tk, m), lambda j, k: (k, 0)) if shared else pl.BlockSpec((1, tk, m), lambda j, k: (j, k, 0))
    return pl.pallas_call(
        body, name=name,
        out_shape=jax.ShapeDtypeStruct((nb, m, n), BF16),
        grid=(nb, n_k),
        in_specs=[x_spec, pl.BlockSpec((1, tk, n), lambda j, k: (j, k, 0))],
        out_specs=pl.BlockSpec((1, m, n), lambda j, k: (j, 0, 0)),
        scratch_shapes=[pltpu.VMEM((m, n), F32)],
        compiler_params=_params(("arbitrary", "arbitrary")),
    )(xs, ys)


def kernel(x, meta_tokens, norm_w, w_in, b_in, lb_logits, hg_norm_w, pool_w, pool_scale, w_down_hg, w_down_pool, w_out, final_norm_w, loss_target, m_meta_tokens, m_norm_w, m_w_in, m_b_in, m_lb_logits, m_hg_norm_w, m_pool_w, m_pool_scale, m_w_down_hg, m_w_down_pool, m_w_out, m_final_norm_w, v_meta_tokens, v_norm_w, v_w_in, v_b_in, v_lb_logits, v_hg_norm_w, v_pool_w, v_pool_scale, v_w_down_hg, v_w_down_pool, v_w_out, v_final_norm_w):
    seq = x.shape[1]

    meta_full = _all_gather_small(meta_tokens).transpose(1, 0, 2).reshape(N_META, D_MODEL)
    w_rest = jnp.concatenate([w_down_hg[0].astype(BF16), w_down_pool[0].astype(BF16), w_out[0].astype(BF16),
                              pool_w[0].astype(BF16).reshape(32, 1024)], axis=0)

    zp = jnp.concatenate([jnp.zeros((PAD_ROWS, D_MODEL), F32), meta_full, x[0]], axis=0)
    p, h, w_blocks = _in_projection(zp, norm_w, w_in[0].astype(BF16), b_in.reshape(N_COLBLK, 1, 1024), _gather_units())
    o, states, scores, rest = _hgrn_forward(p, lb_logits, w_rest)
    wdh = rest[:, REST_W_DOWN_HG:REST_W_DOWN_HG + 128].reshape(1024, 1024)
    wdp = rest[:, REST_W_DOWN_POOL:REST_W_DOWN_POOL + 128].reshape(1024, 1024)
    wout = rest[:, REST_W_OUT:REST_W_OUT + 128].reshape(1024, 1024)
    pw = rest[:, REST_POOL_W:REST_POOL_W + 32].reshape(N_DEV, 4, 32, 256).transpose(1, 0, 2, 3).reshape(4, 256, 256)
    (d_o, dp_b, dz2, grad_lhs, grad_rhs, dbias_b, d_hgw, d_pw, d_ps, d_fnw, loss_part) = _tail(
        p, o, zp, loss_target[0], hg_norm_w, pw, pool_scale, wdh, wdp, wout, final_norm_w.reshape(1, D_MODEL))
    n_a = N_COLBLK - dp_b.shape[0]
    g_hi = _weight_grad(h, dp_b, "weight_grad_in_hi")
    g_rows = _weight_grad(grad_lhs, grad_rhs, "weight_grad_rows")
    dp_a, dbias_a, d_lb, recv_hi, recv_rows = _hgrn_backward(p, lb_logits, states, scores, d_o, g_hi, n_a, g_rows)
    assert n_a == LOW_OWNERS
    chip_lo = _weight_grad_low(h, dp_a)
    dz_seq, dz_head, d_nw, recv_lo = _in_projection_backward(dp_a, dp_b, w_blocks, zp, dz2, norm_w, chip_lo)

    lb = jax.nn.sigmoid(lb_logits[0:1] - lb_logits[1:2])
    d_l0 = d_lb * lb * (1.0 - lb)
    replicated = jnp.concatenate([dbias_a.reshape(3, 1024), dbias_b.reshape(5, 1024), d_nw, d_l0, -d_l0, d_hgw, d_ps, d_fnw,
                                  jnp.pad(loss_part, ((0, MISC_ROWS - MISC_LOSS - 1), (0, 1023)))], axis=0)
    d_meta = dz_head[PAD_ROWS:CHUNK].reshape(N_META, N_DEV, 128).transpose(1, 0, 2)
    d_pw_blocks = d_pw.reshape(4, N_DEV, 32, 256).transpose(1, 0, 2, 3).reshape(N_DEV, 32, 1024)
    g_misc = jnp.concatenate([d_pw_blocks, jnp.pad(d_meta, ((0, 0), (0, 0), (0, 1024 - 128))),
                              jnp.broadcast_to(replicated[None], (N_DEV, 16, 1024))], axis=1)

    as_rows = lambda t, n: t.reshape(n, 1024)
    small = [(MISC_POOL_W, 1024, tuple(as_rows(t, 32) for t in (pool_w, m_pool_w, v_pool_w))),
             (MISC_META, 128, (meta_tokens, m_meta_tokens, v_meta_tokens)),
             (MISC_B_IN, 1024, tuple(as_rows(t, 8) for t in (b_in, m_b_in, v_b_in))),
             (MISC_NORM_W, 1024, (norm_w, m_norm_w, v_norm_w)),
             (MISC_LB, 1024, (lb_logits, m_lb_logits, v_lb_logits)),
             (MISC_HG_NORM_W, 1024, (hg_norm_w, m_hg_norm_w, v_hg_norm_w)),
             (MISC_POOL_SCALE, 1024, (pool_scale, m_pool_scale, v_pool_scale)),
             (MISC_FINAL_NORM_W, 1024, tuple(as_rows(t, 1) for t in (final_norm_w, m_final_norm_w, v_final_norm_w)))]
    res, loss = _finish(recv_hi, recv_lo, n_a, recv_rows, _exchange_small(g_misc), (w_in, m_w_in, v_w_in),
                  [(w_out, m_w_out, v_w_out), (w_down_hg, m_w_down_hg, v_w_down_hg), (w_down_pool, m_w_down_pool, v_w_down_pool)],
                  small)
    r_w_in, r_w_out, r_wdh, r_wdp, r_pw, r_meta, r_b_in, r_nw, r_lb, r_hgw, r_ps, r_fnw = res
    grad_x = dz_seq.reshape(1, seq, D_MODEL)
    per_kind = [(r_meta[k], r_nw[k], r_w_in[k], r_b_in[k].reshape(1, 8192), r_lb[k], r_hgw[k], r_pw[k].reshape(1, 4, 32, 256),
                 r_ps[k], r_wdh[k], r_wdp[k], r_w_out[k], r_fnw[k].reshape(1024)) for k in range(4)]
    return (loss, grad_x, *per_kind[0], *per_kind[1], *per_kind[2], *per_kind[3])
```

```python
import functools

import jax
import jax.numpy as jnp
from jax import lax
from jax.experimental import pallas as pl
from jax.experimental.pallas import tpu as pltpu

F32 = jnp.float32
BF16 = jnp.bfloat16

D_MODEL = 1024
N_META = 16
HEADS = 8
HEAD_DIM = 128
CHUNK = 64
SUB = 8
N_SUB = CHUNK // SUB
PAD_ROWS = CHUNK - N_META
POOL_WINDOWS = (2, 4, 8, 16)
POOL_GDIM = D_MODEL // len(POOL_WINDOWS)
HALO = 16
EPS = 1e-6
N_DEV = 8
N_COLBLK = 8
ADAM_LR, ADAM_B1, ADAM_B2, ADAM_EPS, ADAM_WD, ADAM_STEP = 0.001, 0.9, 0.999, 1e-08, 0.01, 10

VMEM_LIMIT = 56 * 1024 * 1024
MESH = pl.DeviceIdType.MESH
ANY = pl.BlockSpec(memory_space=pl.ANY)

REST_W_DOWN_HG = 0
REST_W_DOWN_POOL = 128
REST_W_OUT = 256
REST_POOL_W = 384
MISC_POOL_W = 0
MISC_META = 32
MISC_B_IN = 48
MISC_NORM_W = 56
MISC_LB = 57
MISC_HG_NORM_W = 59
MISC_POOL_SCALE = 60
MISC_FINAL_NORM_W = 61
MISC_LOSS = 62
MISC_ROWS = 64


def _params(sem=None):
    return pltpu.CompilerParams(dimension_semantics=sem, vmem_limit_bytes=VMEM_LIMIT)


def _row_tile(n_rows, prefer):
    best = 16
    for t in range(16, prefer + 1, 16):
        if n_rows % t == 0:
            best = t
    return best


def _sigmoid_pair(x):
    e = jnp.exp(-jnp.abs(x))
    r = 1.0 / (1.0 + e)
    er = e * r
    pos = x >= 0
    return jnp.where(pos, r, er), jnp.where(pos, er, r)


def _dot(a, b):
    return jnp.dot(a.astype(BF16), b.astype(BF16), preferred_element_type=F32)


def _dot_nt(a, b):
    return lax.dot_general(a.astype(BF16), b.astype(BF16), (((1,), (1,)), ((), ())), preferred_element_type=F32)


def _dot_tn(a, b):
    return lax.dot_general(a.astype(BF16), b.astype(BF16), (((0,), (0,)), ((), ())), preferred_element_type=F32)


def _device_index(px, py, pc):
    return 4 * px + 2 * py + pc


def _direct_gather(src_ref, dst_ref, send_sems, recv_sems, local_sem):
    x, y, c = lax.axis_index("x"), lax.axis_index("y"), lax.axis_index("c")
    own = pltpu.make_async_copy(src_ref, dst_ref.at[_device_index(x, y, c)], local_sem)
    sends, arrivals = [], []
    for k in range(1, N_DEV):
        peer = (1 - x if k & 4 else x, 1 - y if k & 2 else y, 1 - c if k & 1 else c)
        for slot, out in ((_device_index(x, y, c), sends), (_device_index(*peer), arrivals)):
            out.append(pltpu.make_async_remote_copy(
                src_ref=src_ref, dst_ref=dst_ref.at[slot], send_sem=send_sems.at[k - 1], recv_sem=recv_sems.at[k - 1],
                device_id=peer, device_id_type=MESH))
    return own, sends, arrivals


GATHER_SEMS = [pltpu.SemaphoreType.DMA((N_DEV - 1,)), pltpu.SemaphoreType.DMA((N_DEV - 1,)), pltpu.SemaphoreType.DMA(())]


def _all_gather_small(block):
    def body(x_ref, out_ref, send_sems, recv_sems, local_sem):
        own, sends, arrivals = _direct_gather(x_ref, out_ref, send_sems, recv_sems, local_sem)
        own.start()
        for cp in sends:
            cp.start()
        for cp in arrivals:
            cp.wait_recv()
        for cp in sends:
            cp.wait_send()
        own.wait()

    return pl.pallas_call(
        body, name="all_gather_meta",
        out_shape=jax.ShapeDtypeStruct((N_DEV,) + block.shape, block.dtype),
        in_specs=[ANY], out_specs=ANY, scratch_shapes=list(GATHER_SEMS),
    )(block)


def _peer(k):
    x, y, c = lax.axis_index("x"), lax.axis_index("y"), lax.axis_index("c")
    return (1 - x if k & 4 else x, 1 - y if k & 2 else y, 1 - c if k & 1 else c)


def _me():
    return _device_index(lax.axis_index("x"), lax.axis_index("y"), lax.axis_index("c"))


def _remote(src, dst, send_sem, recv_sem, peer_bits):
    return pltpu.make_async_remote_copy(src_ref=src, dst_ref=dst, send_sem=send_sem, recv_sem=recv_sem,
                                        device_id=_peer(peer_bits), device_id_type=MESH)


N_ROW_GRADS = 3
SCATTER_SEMS = [pltpu.SemaphoreType.DMA((N_DEV - 1,)), pltpu.SemaphoreType.DMA((N_DEV - 1,)), pltpu.SemaphoreType.DMA(())]
SCATTER_ROWS_SEMS = [pltpu.SemaphoreType.DMA((7 * N_ROW_GRADS,)), pltpu.SemaphoreType.DMA((7 * N_ROW_GRADS,)),
                     pltpu.SemaphoreType.DMA((N_ROW_GRADS,))]


def _scatter_slabs(g_ref, first, recv_ref, send_sems, recv_sems, local_sem):
    n = g_ref.shape[0]
    me = _me()

    def each(on_send, on_local, on_arrival):
        for kk in range(1, N_DEV):
            peer = jnp.bitwise_xor(me, kk)

            @pl.when((peer >= first) & (peer < first + n))
            def _(kk=kk, peer=peer):
                on_send(_remote(g_ref.at[peer - first], recv_ref.at[me], send_sems.at[kk - 1], recv_sems.at[kk - 1], kk))

        @pl.when((me >= first) & (me < first + n))
        def _():
            on_local(pltpu.make_async_copy(g_ref.at[me - first], recv_ref.at[me], local_sem))
            if on_arrival is not None:
                for kk in range(1, N_DEV):
                    on_arrival(_remote(g_ref.at[0], recv_ref.at[jnp.bitwise_xor(me, kk)], send_sems.at[kk - 1],
                                       recv_sems.at[kk - 1], kk))

    start = lambda: each(lambda cp: cp.start(), lambda cp: cp.start(), None)
    finish = lambda: each(lambda cp: cp.wait_send(), lambda cp: cp.wait(), lambda cp: cp.wait_recv())
    return start, finish


def _scatter_rows(g_ref, recv_ref, send_sems, recv_sems, local_sems):
    me = _me()
    rows = lambda m, dev: g_ref.at[m, pl.ds(dev * 128, 128), :]

    def copies():
        local = [pltpu.make_async_copy(rows(m, me), recv_ref.at[me, m], local_sems.at[m]) for m in range(N_ROW_GRADS)]
        sends, arrivals = [], []
        for m in range(N_ROW_GRADS):
            for kk in range(1, N_DEV):
                peer, sems = jnp.bitwise_xor(me, kk), (send_sems.at[7 * m + kk - 1], recv_sems.at[7 * m + kk - 1])
                sends.append(_remote(rows(m, peer), recv_ref.at[me, m], *sems, kk))
                arrivals.append(_remote(rows(m, me), recv_ref.at[peer, m], *sems, kk))
        return local, sends, arrivals

    def start():
        local, sends, _ = copies()
        for cp in local + sends:
            cp.start()

    def finish():
        local, sends, arrivals = copies()
        for cp in arrivals:
            cp.wait_recv()
        for cp in sends:
            cp.wait_send()
        for cp in local:
            cp.wait()

    return start, finish


LOW_OWNERS = 3


def _weight_grad_low(h, dp_a):
    n_rows = h.shape[0]
    tk = _row_tile(n_rows, 2080)
    n_k = n_rows // tk
    order = jnp.where(lax.axis_index("c") == 0, jnp.array([1, 0, 2], jnp.int32), jnp.array([0, 2, 1], jnp.int32))

    def body(order_ref, h_ref, y_ref, out_ref, acc, send_buf, got, send_sems, recv_sems):
        t, k = pl.program_id(0), pl.program_id(1)
        c = lax.axis_index("c")

        @pl.when(k == 0)
        def _():
            acc[...] = jnp.zeros_like(acc)

        acc[...] += _dot_tn(h_ref[...], y_ref[0])

        def to_sibling(slot):
            return _remote(send_buf.at[slot], got.at[slot], send_sems.at[slot], recv_sems.at[slot], 1)

        def send(slot):
            send_buf[slot] = acc[...].astype(send_buf.dtype)
            to_sibling(slot).start()

        def keep(slot):
            to_sibling(slot).wait_recv()
            out_ref[slot] = (acc[...] + got[slot].astype(F32)).astype(out_ref.dtype)

        done = k == n_k - 1
        for core, step, action, slot in ((0, 0, send, 0), (0, 1, keep, 0), (0, 2, keep, 1),
                                         (1, 0, send, 0), (1, 1, send, 1), (1, 2, keep, 0)):
            @pl.when(done & (c == core) & (t == step))
            def _(action=action, slot=slot):
                action(slot)

        @pl.when(done & (t == LOW_OWNERS - 1))
        def _():
            to_sibling(0).wait_send()

            @pl.when(c == 1)
            def _():
                to_sibling(1).wait_send()
                out_ref[1] = jnp.zeros(out_ref.shape[1:], out_ref.dtype)

    pair = (2, h.shape[1], dp_a.shape[2])
    return pl.pallas_call(
        body, name="weight_grad_in_low",
        out_shape=jax.ShapeDtypeStruct(pair, BF16),
        grid_spec=pltpu.PrefetchScalarGridSpec(
            num_scalar_prefetch=1, grid=(LOW_OWNERS, n_k),
            in_specs=[pl.BlockSpec((tk, h.shape[1]), lambda t, k, o: (k, 0)),
                      pl.BlockSpec((1, tk, dp_a.shape[2]), lambda t, k, o: (o[t], k, 0))],
            out_specs=pl.BlockSpec(pair, lambda t, k, o: (0, 0, 0)),
            scratch_shapes=[pltpu.VMEM(pair[1:], F32), pltpu.VMEM(pair, BF16), pltpu.VMEM(pair, BF16),
                            pltpu.SemaphoreType.DMA((2,)), pltpu.SemaphoreType.DMA((2,))]),
        compiler_params=_params(("arbitrary", "arbitrary")),
    )(order, h, dp_a)


def _exchange_small(misc):
    def body(m_ref, out_ref, send_sems, recv_sems, local_sem):
        start, finish = _scatter_slabs(m_ref, 0, out_ref, send_sems, recv_sems, local_sem)
        start()
        finish()

    return pl.pallas_call(
        body, name="exchange_small", out_shape=jax.ShapeDtypeStruct(misc.shape, misc.dtype),
        in_specs=[ANY], out_specs=ANY, scratch_shapes=list(SCATTER_SEMS),
    )(misc)


def _scatter_low(part_ref, recv_ref, send_sems, recv_sems, local_sem):
    x, y, c = lax.axis_index("x"), lax.axis_index("y"), lax.axis_index("c")
    chip = 2 * x + y
    routes = ((0, (0, 0, c), 0, None), (1, (0, 1, 0), 1, 0))

    def each(on_send, on_local, on_arrival):
        for slot, owner, owner_chip, core in routes:
            holds = (c == core) if core is not None else (c >= 0)
            rel = jnp.bitwise_xor(chip, owner_chip)

            @pl.when(holds & (rel != 0))
            def _(slot=slot, owner=owner, rel=rel):
                on_send(pltpu.make_async_remote_copy(
                    src_ref=part_ref.at[slot], dst_ref=recv_ref.at[chip], send_sem=send_sems.at[slot],
                    recv_sem=recv_sems.at[rel - 1], device_id=owner, device_id_type=MESH))

            @pl.when(holds & (rel == 0))
            def _(slot=slot, owner=owner, owner_chip=owner_chip):
                on_local(pltpu.make_async_copy(part_ref.at[slot], recv_ref.at[chip], local_sem))
                if on_arrival is not None:
                    for r in range(1, 4):
                        on_arrival(pltpu.make_async_remote_copy(
                            src_ref=part_ref.at[slot], dst_ref=recv_ref.at[r ^ owner_chip], send_sem=send_sems.at[slot],
                            recv_sem=recv_sems.at[r - 1], device_id=owner, device_id_type=MESH))

    start = lambda: each(lambda cp: cp.start(), lambda cp: cp.start(), None)
    finish = lambda: each(lambda cp: cp.wait_send(), lambda cp: cp.wait(), lambda cp: cp.wait_recv())
    return start, finish


def _adam_update(g, w, m, v):
    mn = ADAM_B1 * m + (1.0 - ADAM_B1) * g
    vn = ADAM_B2 * v + (1.0 - ADAM_B2) * (g * g)
    m_hat = mn / (1.0 - ADAM_B1 ** ADAM_STEP)
    v_hat = vn / (1.0 - ADAM_B2 ** ADAM_STEP)
    return -ADAM_LR * (m_hat / (jnp.sqrt(v_hat) + ADAM_EPS) + ADAM_WD * w), mn, vn


def _device_sum(parts):
    t = [p.astype(F32) for p in parts]
    return ((t[0] + t[1]) + (t[2] + t[3])) + ((t[4] + t[5]) + (t[6] + t[7]))


def _finish(a_hi, a_lo, n_lo, b3, misc, big, rows3, small):
    n_steps = 4
    tb, tr3 = 1024 // n_steps, 128 // n_steps

    def body(*refs):
        it = iter(refs)
        hi_ref, lo_ref, b_ref, m_ref = next(it), next(it), next(it), next(it)
        big_in = [next(it) for _ in range(3)]
        rows_in = [[next(it) for _ in range(3)] for _ in rows3]
        small_in = [[next(it) for _ in range(3)] for _ in small]
        big_out = [next(it) for _ in range(4)]
        rows_out = [[next(it) for _ in range(4)] for _ in rows3]
        small_out = [[next(it) for _ in range(4)] for _ in small]
        loss_ref = next(it)

        def apply(g, ins, outs):
            d, mn, vn = _adam_update(g, ins[0][...], ins[1][...], ins[2][...])
            for r, val in zip(outs, (g, d, mn, vn)):
                r[...] = val

        lo = [lo_ref[s].astype(F32) for s in range(4)]
        g_big = jnp.where(_me() < n_lo, (lo[0] + lo[1]) + (lo[2] + lo[3]), _device_sum([hi_ref[s] for s in range(N_DEV)]))
        apply(g_big[None], big_in, big_out)
        for k in range(len(rows3)):
            apply(_device_sum([b_ref[s, k] for s in range(N_DEV)])[None], rows_in[k], rows_out[k])

        @pl.when(pl.program_id(0) == 0)
        def _():
            loss_ref[...] = _device_sum([m_ref[s, MISC_LOSS:MISC_LOSS + 1, :] for s in range(N_DEV)])
            for (row0, lanes, ins), r_in, r_out in zip(small, small_in, small_out):
                n = ins[0].shape[0]
                apply(_device_sum([m_ref[s, row0:row0 + n, :lanes] for s in range(N_DEV)]), r_in, r_out)

    whole = lambda shape: pl.BlockSpec(shape, lambda i: (0,) * len(shape))
    big_blk = pl.BlockSpec((1, tb, 1024), lambda i: (0, i, 0))
    rows_blk = pl.BlockSpec((1, tr3, 1024), lambda i: (0, i, 0))
    in_specs = [pl.BlockSpec((N_DEV, tb, 1024), lambda i: (0, i, 0)), pl.BlockSpec((4, tb, 1024), lambda i: (0, i, 0)),
                pl.BlockSpec((N_DEV, 3, tr3, 1024), lambda i: (0, 0, i, 0)), whole(misc.shape)]
    in_specs += [big_blk] * 3 + [rows_blk] * (3 * len(rows3))
    out_specs = [big_blk] * 4 + [rows_blk] * (4 * len(rows3))
    out_shape = [jax.ShapeDtypeStruct(big[0].shape, F32)] * 4
    for w, _, _ in rows3:
        out_shape += [jax.ShapeDtypeStruct(w.shape, F32)] * 4
    args = [a_hi, a_lo, b3, misc, *big]
    for t in rows3:
        args += list(t)
    for _, _, t in small:
        in_specs += [whole(t[0].shape)] * 3
        out_specs += [whole(t[0].shape)] * 4
        out_shape += [jax.ShapeDtypeStruct(t[0].shape, F32)] * 4
        args += list(t)
    out_specs.append(whole((1, 1024)))
    out_shape.append(jax.ShapeDtypeStruct((1, 1024), F32))
    outs = pl.pallas_call(
        body, name="reduce_sum_adamw", out_shape=tuple(out_shape), grid=(n_steps,),
        in_specs=in_specs, out_specs=tuple(out_specs),
        compiler_params=_params(("arbitrary",)),
    )(*args)
    return [tuple(outs[4 * k:4 * k + 4]) for k in range(len(outs) // 4)], outs[-1][0, 0]


GATHER_UNITS = ((0, 0), (0, 1), (1, 0), (1, 1)) + tuple((place, 0) for place in range(2, 8)) + tuple(
    (place, 1) for place in range(2, 8))


def _gather_order():
    x, y, c = lax.axis_index("x"), lax.axis_index("y"), lax.axis_index("c")
    chips = [(1 - x, y), (x, 1 - y), (1 - x, 1 - y)]
    order = [_device_index(x, y, c), _device_index(x, y, 1 - c)]
    order += [_device_index(*q, c) for q in chips] + [_device_index(*q, 1 - c) for q in chips]
    return order


def _gather_units():
    order = _gather_order()
    blocks = jnp.stack([order[place] for place, _ in GATHER_UNITS]).astype(jnp.int32)
    return blocks, jnp.array([half for _, half in GATHER_UNITS], jnp.int32)


def _in_projection(zp, norm_w, w_shard, b_blocks, units):
    n_rows = zp.shape[0]
    tr = _row_tile(n_rows, 1040)
    nt = n_rows // tr

    half_cols = 1024 // 2
    n_units = len(GATHER_UNITS)

    def body(blocks_ref, halves_ref, z_ref, nw_ref, w_hbm, b_ref, p_ref, h_ref, w_out, w_vmem, h_all,
             send_sems, recv_sems, local_sem, out_sems):
        s, i = pl.program_id(0), pl.program_id(1)
        x, y, c = lax.axis_index("x"), lax.axis_index("y"), lax.axis_index("c")
        me, sibling = (x, y, c), (x, y, 1 - c)
        chips = [(1 - x, y), (x, 1 - y), (1 - x, 1 - y)]

        def slot(blk, half):
            return w_vmem.at[_device_index(*blk), half]

        def mine(half):
            return w_hbm.at[:, pl.ds(half * half_cols, half_cols)]

        def copy(k, half, blk, to, own_block=False):
            return pltpu.make_async_remote_copy(
                src_ref=mine(half) if own_block else slot(blk, half), dst_ref=slot(blk, half),
                send_sem=send_sems.at[2 * k + half], recv_sem=recv_sems.at[2 * k + half], device_id=to, device_id_type=MESH)

        own = [pltpu.make_async_copy(mine(half), slot(me, half), local_sem.at[half]) for half in range(2)]
        first = [copy(k, half, me, to, own_block=True) for half in range(2)
                 for k, to in enumerate([sibling] + [(*q, c) for q in chips])]
        passed = [[copy(4 + j, half, (*q, c), sibling) for j, q in enumerate(chips)] for half in range(2)]
        sources = [None, sibling] + [(*q, c) for q in chips] + [(*q, 1 - c) for q in chips]
        arrival = lambda place, half: copy(place - 1, half, sources[place], me)

        def keep(unit):
            blk, half = blocks_ref[unit], halves_ref[unit]
            return pltpu.make_async_copy(w_vmem.at[blk, half], w_out.at[blk, half], out_sems.at[unit])

        for unit, (place, half) in enumerate(GATHER_UNITS):
            @pl.when((i == 0) & (s == unit))
            def _(unit=unit, place=place, half=half):
                if unit == 0:
                    for cp in own + first:
                        cp.start()
                if place == 0:
                    own[half].wait()
                else:
                    arrival(place, half).wait_recv()
                    if 2 <= place <= 4:
                        passed[half][place - 2].start()
                keep(unit).start()

        @pl.when(s == 0)
        def _():
            z = z_ref[...]
            r = lax.rsqrt(jnp.mean(z * z, axis=-1, keepdims=True) + EPS)
            h = (z * r * nw_ref[...]).astype(BF16)
            h_all[i] = h
            h_ref[...] = h

        p_ref[0] = jnp.dot(h_all[i], w_vmem[blocks_ref[s], halves_ref[s]], preferred_element_type=F32) + b_ref[0]

        @pl.when((s == n_units - 1) & (i == nt - 1))
        def _():
            for cp in first + passed[0] + passed[1]:
                cp.wait_send()
            for unit in range(n_units):
                keep(unit).wait()

    first_pass = lambda s, i, rest: jnp.where(s == 0, i, rest)
    return pl.pallas_call(
        body, name="in_projection_gather",
        out_shape=(jax.ShapeDtypeStruct((N_COLBLK, n_rows, 1024), F32),
                   jax.ShapeDtypeStruct((n_rows, D_MODEL), BF16),
                   jax.ShapeDtypeStruct((N_DEV, 2, D_MODEL, half_cols), BF16)),
        grid_spec=pltpu.PrefetchScalarGridSpec(
            num_scalar_prefetch=2, grid=(n_units, nt),
            in_specs=[pl.BlockSpec((tr, D_MODEL), lambda s, i, blk, hf: (first_pass(s, i, 0), 0)),
                      pl.BlockSpec((1, D_MODEL), lambda s, i, blk, hf: (0, 0)), ANY,
                      pl.BlockSpec((1, 1, half_cols), lambda s, i, blk, hf: (blk[s], 0, hf[s]))],
            out_specs=(pl.BlockSpec((1, tr, half_cols), lambda s, i, blk, hf: (blk[s], i, hf[s])),
                       pl.BlockSpec((tr, D_MODEL), lambda s, i, blk, hf: (first_pass(s, i, nt - 1), 0)), ANY),
            scratch_shapes=[pltpu.VMEM((N_DEV, 2, D_MODEL, half_cols), BF16), pltpu.VMEM((nt, tr, D_MODEL), BF16),
                            pltpu.SemaphoreType.DMA((14,)), pltpu.SemaphoreType.DMA((14,)), pltpu.SemaphoreType.DMA((2,)),
                            pltpu.SemaphoreType.DMA((n_units,))]),
        compiler_params=_params(("arbitrary", "arbitrary")),
    )(*units, zp, norm_w, w_shard, b_blocks)


def _lower_bound(lb_ref):
    l0, l1 = lb_ref[0:1, :], lb_ref[1:2, :]
    _, lb = _sigmoid_pair(l1 - l0)
    return lb


def _chunk_gates(fz, lb, valid):
    sig, nsig = _sigmoid_pair(fz)
    f = lb + (1.0 - lb) * sig
    g2 = jnp.where(valid, jnp.log2(f), 0.0)
    k = jnp.where(valid, (1.0 - lb) * nsig, 0.0)
    return sig, nsig, f, g2, k


def _tri(n, upper=False):
    r = lax.broadcasted_iota(jnp.int32, (n, n), 0)
    c = lax.broadcasted_iota(jnp.int32, (n, n), 1)
    return jnp.where((r <= c) if upper else (r >= c), 1.0, 0.0).astype(BF16)


def _tri_dot(tri, x):
    hi = x.astype(BF16)
    rest = x - hi.astype(F32)
    mid = rest.astype(BF16)
    low = (rest - mid.astype(F32)).astype(BF16)
    return (jnp.dot(tri, hi, preferred_element_type=F32) + jnp.dot(tri, mid, preferred_element_type=F32)
            + jnp.dot(tri, low, preferred_element_type=F32))


def _intra_scores(q_ref, k_ref, b2_ref, col0):
    cols = pl.ds(col0, HEAD_DIM)
    rows_s = lax.broadcasted_iota(jnp.int32, (SUB, 1), 0)
    lanes_c = lax.broadcasted_iota(jnp.int32, (1, CHUNK), 1)
    blocks = []
    for i in range(N_SUB):
        lo = i * SUB
        qi = q_ref[lo:lo + SUB, cols]
        bi = b2_ref[lo:lo + SUB, cols]
        if i == 0:
            acc = jnp.zeros((SUB, CHUNK), F32)
        else:
            ref_i = b2_ref[lo:lo + 1, cols]
            qt = qi * jnp.exp2(bi - ref_i)
            kt = jnp.concatenate([k_ref[0:lo, cols] * jnp.exp2(ref_i - b2_ref[0:lo, cols]),
                                  jnp.zeros((CHUNK - lo, HEAD_DIM), F32)], axis=0)
            acc = _dot_nt(qt, kt)
        for s in range(SUB):
            b_s = b2_ref[lo + s:lo + s + 1, cols]
            k_s = k_ref[lo + s:lo + s + 1, cols]
            w = jnp.exp2(jnp.minimum(bi - b_s, 0.0))
            col = jnp.sum((qi * w) * k_s, axis=-1, keepdims=True)
            acc = jnp.where(lanes_c == lo + s, col, acc)
        blocks.append(jnp.where(lanes_c <= lo + rows_s, acc, 0.0))
    return jnp.concatenate(blocks, axis=0)


def _hgrn_forward(p, lb_logits, w_rest):
    n_rows = p.shape[1]
    n_chunks = n_rows // CHUNK
    width = HEADS * HEAD_DIM
    per_step = next(c for c in (5, 3, 1) if n_chunks % c == 0)
    n_steps = n_chunks // per_step
    span = per_step * CHUNK

    def body(q_ref, fz_ref, v_ref, lb_ref, rest_ref, o_ref, st_out_ref, a_out_ref, rest_out,
             state, k_all, b2_all, send_sems, recv_sems, local_sem):
        n = pl.program_id(0)
        own, sends, arrivals = _direct_gather(rest_ref, rest_out, send_sems, recv_sems, local_sem)

        @pl.when(n == 0)
        def _():
            state[...] = jnp.zeros_like(state)
            own.start()
            for cp in sends:
                cp.start()

        lb = _lower_bound(lb_ref)
        for cc in range(per_step):
            r0 = cc * CHUNK
            k_vmem, b2_vmem = k_all.at[cc], b2_all.at[cc]
            rows = (n * per_step + cc) * CHUNK + lax.broadcasted_iota(jnp.int32, (CHUNK, 1), 0)
            valid = rows >= PAD_ROWS
            _, _, _, g2, k = _chunk_gates(fz_ref[0, r0:r0 + CHUNK, :], lb, valid)
            k_vmem[...] = k
            b2_vmem[...] = _tri_dot(_tri(CHUNK), g2)
            q_view = q_ref.at[0, pl.ds(r0, CHUNK), :]
            for h in range(HEADS):
                cols = pl.ds(h * HEAD_DIM, HEAD_DIM)
                st = state[h]
                st_out_ref[cc, h] = st
                bh = b2_vmem[:, cols]
                kh = k_vmem[:, cols]
                vh = jnp.where(valid, v_ref[0, r0:r0 + CHUNK, cols], 0.0)
                qe = q_view[:, cols] * jnp.exp2(bh)
                a = _intra_scores(q_view, k_vmem, b2_vmem, h * HEAD_DIM).astype(BF16)
                a_out_ref[cc, h] = a
                o_ref[r0:r0 + CHUNK, cols] = _dot_nt(qe, st) + _dot(a, vh)
                b_last = b2_vmem[CHUNK - 1:CHUNK, cols]
                kd = kh * jnp.exp2(b_last - bh)
                state[h] = st * jnp.exp2(b_last) + _dot_tn(vh, kd)

        @pl.when(n == n_steps - 1)
        def _():
            for cp in arrivals:
                cp.wait_recv()
            for cp in sends:
                cp.wait_send()
            own.wait()

    blk = lambda c: pl.BlockSpec((1, span, width), lambda n, c=c: (c, n, 0))
    return pl.pallas_call(
        body, name="hgrn_forward",
        out_shape=(jax.ShapeDtypeStruct((n_rows, width), F32),
                   jax.ShapeDtypeStruct((n_chunks, HEADS, HEAD_DIM, HEAD_DIM), F32),
                   jax.ShapeDtypeStruct((n_chunks, HEADS, CHUNK, CHUNK), BF16),
                   jax.ShapeDtypeStruct((N_DEV,) + w_rest.shape, w_rest.dtype)),
        grid=(n_steps,),
        in_specs=[blk(0), blk(1), blk(2), pl.BlockSpec((2, width), lambda n: (0, 0)), ANY],
        out_specs=(pl.BlockSpec((span, width), lambda n: (n, 0)),
                   pl.BlockSpec((per_step, HEADS, HEAD_DIM, HEAD_DIM), lambda n: (n, 0, 0, 0)),
                   pl.BlockSpec((per_step, HEADS, CHUNK, CHUNK), lambda n: (n, 0, 0, 0)), ANY),
        scratch_shapes=[pltpu.VMEM((HEADS, HEAD_DIM, HEAD_DIM), F32), pltpu.VMEM((per_step, CHUNK, width), F32),
                        pltpu.VMEM((per_step, CHUNK, width), F32)] + list(GATHER_SEMS),
        compiler_params=_params(("arbitrary",)),
    )(p, p, p, lb_logits, w_rest)


def _hgrn_backward(p, lb_logits, states, scores, d_o, g_slabs, first_owner, g_rows):
    n_rows = p.shape[1]
    n_chunks = n_rows // CHUNK
    width = HEADS * HEAD_DIM

    per_step = next(c for c in (5, 3, 1) if n_chunks % c == 0)
    n_steps = n_chunks // per_step
    span = per_step * CHUNK

    def one_chunk(n, lb, q_ref, fz_ref, v_ref, st_ref, a_ref, do_ref, dp_ref, dbias_ref, dlb_ref, dstate, k_vmem, b2_vmem):
        rows = n * CHUNK + lax.broadcasted_iota(jnp.int32, (CHUNK, 1), 0)
        valid = rows >= PAD_ROWS
        sig, nsig, f, g2, k = _chunk_gates(fz_ref[0], lb, valid)
        k_vmem[...] = k
        b2_vmem[...] = _tri_dot(_tri(CHUNK), g2)
        rows_c = lax.broadcasted_iota(jnp.int32, (CHUNK, 1), 0)
        rows_s = lax.broadcasted_iota(jnp.int32, (SUB, 1), 0)
        lanes_c = lax.broadcasted_iota(jnp.int32, (1, CHUNK), 1)
        causal = lax.broadcasted_iota(jnp.int32, (CHUNK, CHUNK), 0) >= lax.broadcasted_iota(jnp.int32, (CHUNK, CHUNK), 1)
        tri_up = _tri(CHUNK, upper=True)
        for h in range(HEADS):
            cols = pl.ds(h * HEAD_DIM, HEAD_DIM)
            st = st_ref[0, h]
            dst = dstate[h]
            qh = q_ref[0, :, cols]
            bh = b2_vmem[:, cols]
            kh = k_vmem[:, cols]
            vh = jnp.where(valid, v_ref[0, :, cols], 0.0)
            doh = do_ref[:, cols]
            eb = jnp.exp2(bh)
            qe = qh * eb
            b_last = b2_vmem[CHUNK - 1:CHUNK, cols]
            e_last = jnp.exp2(b_last)
            decay_k = jnp.exp2(b_last - bh)
            kd = kh * decay_k
            dqe = _dot(doh, st)
            da = jnp.where(causal, _dot_nt(doh, vh), 0.0)
            dv = _dot_tn(a_ref[0, h], doh) + _dot_nt(kd, dst)
            dkd = _dot(vh, dst)
            dstate[h] = dst * e_last + _dot_tn(doh, qe)
            db_last = (jnp.sum(dst * st, axis=0, keepdims=True) * e_last
                       + jnp.sum(dkd * kd, axis=0, keepdims=True))
            dq_blocks, dk_blocks = [], []
            dk_earlier = jnp.zeros((CHUNK, HEAD_DIM), F32)
            for i in range(N_SUB):
                lo = i * SUB
                qi = q_ref[0, lo:lo + SUB, cols]
                bi = b2_vmem[lo:lo + SUB, cols]
                da_i = da[lo:lo + SUB, :]
                if i == 0:
                    dq_i = jnp.zeros((SUB, HEAD_DIM), F32)
                else:
                    ref_i = b2_vmem[lo:lo + 1, cols]
                    eq = jnp.exp2(bi - ref_i)
                    ek = jnp.exp2(ref_i - b2_vmem[0:lo, cols])
                    later = jnp.zeros((CHUNK - lo, HEAD_DIM), F32)
                    kt = jnp.concatenate([k_vmem[0:lo, cols] * ek, later], axis=0)
                    dq_i = _dot(da_i, kt) * eq
                    dk_earlier = dk_earlier + jnp.concatenate([_dot_tn(da_i, qi * eq)[0:lo] * ek, later], axis=0)
                dk_i = jnp.zeros((SUB, HEAD_DIM), F32)
                for s in range(SUB):
                    b_s = b2_vmem[lo + s:lo + s + 1, cols]
                    k_s = k_vmem[lo + s:lo + s + 1, cols]
                    w = jnp.exp2(jnp.minimum(bi - b_s, 0.0))
                    da_col = jnp.sum(jnp.where(lanes_c == lo + s, da_i, 0.0), axis=-1, keepdims=True)
                    gw = da_col * w
                    dq_i = dq_i + gw * k_s
                    dk_i = jnp.where(rows_s == s, jnp.sum(gw * qi, axis=0, keepdims=True), dk_i)
                dq_blocks.append(dq_i)
                dk_blocks.append(dk_i)
            dq_intra = jnp.concatenate(dq_blocks, axis=0)
            dk_intra = jnp.concatenate(dk_blocks, axis=0) + dk_earlier
            dq = dqe * eb + dq_intra
            dk = dkd * decay_k + dk_intra
            db = dqe * qe - dkd * kd + qh * dq_intra - kh * dk_intra
            db = db + jnp.where(rows_c == CHUNK - 1, db_last, 0.0)
            dg = _tri_dot(tri_up, db)
            fh = f[:, h * HEAD_DIM:(h + 1) * HEAD_DIM]
            sh = sig[:, h * HEAD_DIM:(h + 1) * HEAD_DIM]
            nh = nsig[:, h * HEAD_DIM:(h + 1) * HEAD_DIM]
            lbh = lb[:, h * HEAD_DIM:(h + 1) * HEAD_DIM]
            df = jnp.where(valid, dg / fh - dk, 0.0)
            dfz = df * (1.0 - lbh) * sh * nh
            dq = jnp.where(valid, dq, 0.0)
            dv = jnp.where(valid, dv, 0.0)
            dlb_ref[:, cols] += jnp.sum(df * nh, axis=0, keepdims=True)
            dp_ref[0, :, cols] = dq.astype(BF16)
            dp_ref[1, :, cols] = dfz.astype(BF16)
            dp_ref[2, :, cols] = dv.astype(BF16)
            dbias_ref[0, :, cols] += jnp.sum(dq, axis=0, keepdims=True)
            dbias_ref[1, :, cols] += jnp.sum(dfz, axis=0, keepdims=True)
            dbias_ref[2, :, cols] += jnp.sum(dv, axis=0, keepdims=True)

    def body(q_ref, fz_ref, v_ref, lb_ref, st_ref, a_ref, do_ref, gs_hbm, gr_hbm, dp_ref, dbias_ref, dlb_ref, rs_hbm, rr_hbm,
             dstate, k_all, b2_all, *sems):
        step = pl.program_id(0)
        start_slabs, finish_slabs = _scatter_slabs(gs_hbm, first_owner, rs_hbm, *sems[:3])
        start_rows, finish_rows = _scatter_rows(gr_hbm, rr_hbm, *sems[3:])

        @pl.when(step == 0)
        def _():
            dstate[...] = jnp.zeros_like(dstate)
            dbias_ref[...] = jnp.zeros_like(dbias_ref)
            dlb_ref[...] = jnp.zeros_like(dlb_ref)
            start_slabs()
            start_rows()

        lb = _lower_bound(lb_ref)
        for cc in reversed(range(per_step)):
            rows_cc = pl.ds(cc * CHUNK, CHUNK)
            one_chunk((n_steps - 1 - step) * per_step + cc, lb, q_ref.at[:, rows_cc, :], fz_ref.at[:, rows_cc, :],
                      v_ref.at[:, rows_cc, :], st_ref.at[pl.ds(cc, 1)], a_ref.at[pl.ds(cc, 1)], do_ref.at[rows_cc, :],
                      dp_ref.at[:, rows_cc, :], dbias_ref, dlb_ref, dstate, k_all.at[cc], b2_all.at[cc])

        @pl.when(step == n_steps - 1)
        def _():
            finish_slabs()
            finish_rows()

    rev = lambda s: n_steps - 1 - s
    blk = lambda c: pl.BlockSpec((1, span, width), lambda s, c=c: (c, rev(s), 0))
    return pl.pallas_call(
        body, name="hgrn_backward",
        out_shape=(jax.ShapeDtypeStruct((3, n_rows, width), BF16),
                   jax.ShapeDtypeStruct((3, 1, width), F32),
                   jax.ShapeDtypeStruct((1, width), F32),
                   jax.ShapeDtypeStruct((N_DEV,) + g_slabs.shape[1:], g_slabs.dtype),
                   jax.ShapeDtypeStruct((N_DEV, N_ROW_GRADS, 128, g_rows.shape[2]), g_rows.dtype)),
        grid=(n_steps,),
        in_specs=[blk(0), blk(1), blk(2), pl.BlockSpec((2, width), lambda s: (0, 0)),
                  pl.BlockSpec((per_step, HEADS, HEAD_DIM, HEAD_DIM), lambda s: (rev(s), 0, 0, 0)),
                  pl.BlockSpec((per_step, HEADS, CHUNK, CHUNK), lambda s: (rev(s), 0, 0, 0)),
                  pl.BlockSpec((span, width), lambda s: (rev(s), 0)), ANY, ANY],
        out_specs=(pl.BlockSpec((3, span, width), lambda s: (0, rev(s), 0)),
                   pl.BlockSpec((3, 1, width), lambda s: (0, 0, 0)),
                   pl.BlockSpec((1, width), lambda s: (0, 0)), ANY, ANY),
        scratch_shapes=[pltpu.VMEM((HEADS, HEAD_DIM, HEAD_DIM), F32), pltpu.VMEM((per_step, CHUNK, width), F32),
                        pltpu.VMEM((per_step, CHUNK, width), F32)] + list(SCATTER_SEMS) + list(SCATTER_ROWS_SEMS),
        compiler_params=_params(("arbitrary",)),
    )(p, p, p, lb_logits, states, scores, d_o, g_slabs, g_rows)


def _sigmoid_and_complement(x):
    s = 0.5 * jnp.tanh(0.5 * x) + 0.5
    return s, 1.0 - s


def _silu_and_grad(x):
    s, ns = _sigmoid_and_complement(x)
    return x * s, s * (1.0 + x * ns)


def _tail(p, o, zp, tgt, hg_norm_w, pool_w, pool_scale, w_down_hg, w_down_pool, w_out, final_norm_w):
    n_rows = zp.shape[0]
    tr = _row_tile(n_rows, 208)
    nt = n_rows // tr
    ext = tr + HALO
    n_groups = len(POOL_WINDOWS)

    def body(o_ref, ghg_ref, u_ref, gpool_ref, mhg_ref, mpool_ref, uhalo_ref, z_ref, tgt_hbm,
             hgw_ref, pw_ref, ps_ref, wdh_ref, wdp_ref, wout_ref, fnw_ref,
             do_ref, dp_ref, dz2_ref, lhs_ref, rhs_ref,
             dbias_ref, dhgw_ref, dpw_ref, dps_ref, dfnw_ref, loss_ref, halo_vmem, tgt_buf, tgt_sems):
        step = pl.program_id(0)
        ti = nt - 1 - step

        def target_rows(tile, slot, act):
            @pl.when(tile == 0)
            def _():
                cp = pltpu.make_async_copy(tgt_hbm.at[pl.ds(0, tr - CHUNK), :], tgt_buf.at[slot, pl.ds(CHUNK, tr - CHUNK), :],
                                           tgt_sems.at[slot])
                getattr(cp, act)()

            @pl.when(tile > 0)
            def _():
                cp = pltpu.make_async_copy(tgt_hbm.at[pl.ds(tile * tr - CHUNK, tr), :], tgt_buf.at[slot], tgt_sems.at[slot])
                getattr(cp, act)()

        @pl.when(step == 0)
        def _():
            halo_vmem[...] = jnp.zeros_like(halo_vmem)
            for r in (dbias_ref, dhgw_ref, dpw_ref, dps_ref, dfnw_ref, loss_ref):
                r[...] = jnp.zeros_like(r)
            if nt <= 2:
                tgt_buf[(nt - 1) % 2, 0:CHUNK, :] = jnp.zeros((CHUNK, D_MODEL), F32)
            target_rows(ti, 0, "start")

        @pl.when(ti > 0)
        def _():
            target_rows(ti - 1, (step + 1) % 2, "start")

        rows = ti * tr + lax.broadcasted_iota(jnp.int32, (tr, 1), 0)
        valid = rows >= PAD_ROWS
        in_loss = rows >= CHUNK
        count_pos = jnp.maximum(rows - PAD_ROWS + 1, 1).astype(F32)

        o = o_ref[...]
        hgw = hgw_ref[...]
        inv_o, on_parts = [], []
        for h in range(HEADS):
            oh = o[:, h * HEAD_DIM:(h + 1) * HEAD_DIM]
            r = lax.rsqrt(jnp.mean(oh * oh, axis=-1, keepdims=True) + EPS)
            inv_o.append(r)
            on_parts.append(oh * r)
        o_hat = jnp.concatenate(on_parts, axis=1)
        o_n = o_hat * hgw
        g_hg = ghg_ref[0]
        silu_hg, dsilu_hg = _silu_and_grad(g_hg)
        a_hg = o_n * silu_hg
        y_hg = _dot(a_hg, wdh_ref[...])

        u = jnp.where(valid, u_ref[0], 0.0)
        u_prev = jnp.where(ti > 0, uhalo_ref[0], 0.0)
        u_ext = jnp.concatenate([u_prev, u], axis=0)
        pooled_parts, mixed_parts, inv_cnt = [], [], []
        for gi, win in enumerate(POOL_WINDOWS):
            lanes = slice(gi * POOL_GDIM, (gi + 1) * POOL_GDIM)
            s = u_ext[:, lanes]
            shift = 1
            while shift < win:
                s = s + pltpu.roll(s, shift, 0)
                shift *= 2
            ic = 1.0 / jnp.minimum(count_pos, float(win))
            inv_cnt.append(ic)
            pooled = s[HALO:] * ic - u[:, lanes]
            pooled_parts.append(pooled)
            mixed_parts.append(_dot(pooled, pw_ref[gi]))
        mixed = jnp.concatenate(mixed_parts, axis=1)
        ps = ps_ref[...]
        g_pool = gpool_ref[0]
        silu_pool, dsilu_pool = _silu_and_grad(g_pool)
        a_pool = mixed * ps * silu_pool
        y_pool = _dot(a_pool, wdp_ref[...])

        m_hg, m_pool = mhg_ref[0], mpool_ref[0]
        s_hg, ns_hg = _sigmoid_and_complement(m_hg)
        s_pool, ns_pool = _sigmoid_and_complement(m_pool)
        merged = s_hg * y_hg + s_pool * y_pool
        z2 = z_ref[...] + _dot(merged, wout_ref[...])
        r2 = lax.rsqrt(jnp.mean(z2 * z2, axis=-1, keepdims=True) + EPS)
        n2 = z2 * r2
        fnw = fnw_ref[...]
        target_rows(ti, step % 2, "wait")
        err = jnp.where(in_loss, n2 * fnw - tgt_buf[step % 2], 0.0)
        loss_ref[...] += jnp.sum(jnp.sum(err * err, axis=0, keepdims=True), axis=1, keepdims=True) * (0.5 / D_MODEL)
        dy = err * (1.0 / D_MODEL)

        dfnw_ref[...] += jnp.sum(dy * n2, axis=0, keepdims=True)
        gy = dy * fnw
        dz2 = r2 * (gy - n2 * jnp.mean(gy * n2, axis=-1, keepdims=True))
        dmerged = _dot_nt(dz2, wout_ref[...])
        dy_hg = s_hg * dmerged
        dy_pool = s_pool * dmerged
        dm_hg = dmerged * y_hg * s_hg * ns_hg
        dm_pool = dmerged * y_pool * s_pool * ns_pool
        da_hg = _dot_nt(dy_hg, wdh_ref[...])
        da_pool = _dot_nt(dy_pool, wdp_ref[...])

        d_on = da_hg * silu_hg
        dg_hg = da_hg * o_n * dsilu_hg
        dhgw_ref[...] += jnp.sum(d_on * o_hat, axis=0, keepdims=True)
        gyo = d_on * hgw
        do_parts = []
        for h in range(HEADS):
            lanes = slice(h * HEAD_DIM, (h + 1) * HEAD_DIM)
            gh, nh = gyo[:, lanes], o_hat[:, lanes]
            do_parts.append(inv_o[h] * (gh - nh * jnp.mean(gh * nh, axis=-1, keepdims=True)))
        do_ref[...] = jnp.concatenate(do_parts, axis=1)

        dmixed = da_pool * ps * silu_pool
        dps_ref[...] += jnp.sum(da_pool * mixed * silu_pool, axis=0, keepdims=True)
        dg_pool = da_pool * mixed * ps * dsilu_pool
        du_parts = []
        for gi, win in enumerate(POOL_WINDOWS):
            lanes = slice(gi * POOL_GDIM, (gi + 1) * POOL_GDIM)
            dmx = dmixed[:, lanes]
            dpooled = _dot_nt(dmx, pw_ref[gi])
            dpw_ref[gi] += _dot_tn(pooled_parts[gi], dmx)
            dpt = dpooled * inv_cnt[gi]
            s = jnp.concatenate([dpt, halo_vmem[:, lanes]], axis=0)
            shift = 1
            while shift < win:
                s = s + pltpu.roll(s, ext - shift, 0)
                shift *= 2
            du_parts.append(s[:tr] - dpooled)
            halo_vmem[:, lanes] = dpt[:HALO]
        du = jnp.where(valid, jnp.concatenate(du_parts, axis=1), 0.0)

        for c, val in enumerate((dg_hg, du, dg_pool, dm_hg, dm_pool)):
            dp_ref[c] = val.astype(BF16)
            dbias_ref[c] += jnp.sum(val, axis=0, keepdims=True)
        dz2_ref[...] = dz2
        for c, (lhs, rhs) in enumerate(((merged, dz2), (a_hg, dy_hg), (a_pool, dy_pool))):
            lhs_ref[c] = lhs.astype(BF16)
            rhs_ref[c] = rhs.astype(BF16)

    rev = lambda s: nt - 1 - s
    rowblk = pl.BlockSpec((tr, D_MODEL), lambda s: (rev(s), 0))
    pblk = lambda c: pl.BlockSpec((1, tr, 1024), lambda s, c=c: (c, rev(s), 0))
    halo_blk = pl.BlockSpec((1, HALO, 1024), lambda s: (4, jnp.maximum(rev(s) * (tr // HALO) - 1, 0), 0))
    full = lambda shape: pl.BlockSpec(shape, lambda s: (0,) * len(shape))
    vec = full((1, D_MODEL))
    mat = full((D_MODEL, D_MODEL))
    act3 = jax.ShapeDtypeStruct((3, n_rows, D_MODEL), BF16)
    act3_blk = pl.BlockSpec((3, tr, D_MODEL), lambda s: (0, rev(s), 0))
    return pl.pallas_call(
        body, name="tail_forward_backward",
        out_shape=(jax.ShapeDtypeStruct((n_rows, D_MODEL), F32),
                   jax.ShapeDtypeStruct((5, n_rows, 1024), BF16),
                   jax.ShapeDtypeStruct((n_rows, D_MODEL), F32),
                   act3, act3,
                   jax.ShapeDtypeStruct((5, 1, 1024), F32),
                   jax.ShapeDtypeStruct((1, D_MODEL), F32),
                   jax.ShapeDtypeStruct((n_groups, POOL_GDIM, POOL_GDIM), F32),
                   jax.ShapeDtypeStruct((1, D_MODEL), F32),
                   jax.ShapeDtypeStruct((1, D_MODEL), F32),
                   jax.ShapeDtypeStruct((1, 1), F32)),
        grid=(nt,),
        in_specs=[rowblk, pblk(3), pblk(4), pblk(5), pblk(6), pblk(7), halo_blk, rowblk, ANY,
                  vec, full((n_groups, POOL_GDIM, POOL_GDIM)), vec, mat, mat, mat, vec],
        out_specs=(rowblk, pl.BlockSpec((5, tr, 1024), lambda s: (0, rev(s), 0)), rowblk,
                   act3_blk, act3_blk,
                   full((5, 1, 1024)), vec, full((n_groups, POOL_GDIM, POOL_GDIM)), vec, vec, full((1, 1))),
        scratch_shapes=[pltpu.VMEM((HALO, D_MODEL), F32), pltpu.VMEM((2, tr, D_MODEL), F32), pltpu.SemaphoreType.DMA((2,))],
        compiler_params=_params(("arbitrary",)),
    )(o, p, p, p, p, p, p, zp, tgt, hg_norm_w, pool_w, pool_scale, w_down_hg, w_down_pool, w_out, final_norm_w)


def _in_projection_backward(dp_a, dp_b, w_blocks, zp, dz2, norm_w, chip_sums):
    n_rows = zp.shape[0]
    tr = _row_tile(n_rows, 416)
    nt = n_rows // tr
    na, nb = dp_a.shape[0], dp_b.shape[0]

    def body(dpa_ref, dpb_ref, w_hbm, z_ref, dz2_ref, nw_ref, gs_hbm, gx_hbm, head_ref, dnw_ref, rs_hbm,
             w_vmem, sem, dz_buf, gx_sems, *sems):
        i = pl.program_id(0)
        start_slabs, finish_slabs = _scatter_low(gs_hbm, rs_hbm, *sems)

        def wait_rows_out(tile):
            @pl.when(tile == 0)
            def _():
                pltpu.make_async_copy(dz_buf.at[0, pl.ds(CHUNK, tr - CHUNK), :], gx_hbm.at[pl.ds(0, tr - CHUNK), :],
                                      gx_sems.at[0]).wait()

            @pl.when(tile > 0)
            def _():
                pltpu.make_async_copy(dz_buf.at[tile % 2], gx_hbm.at[pl.ds(tile * tr - CHUNK, tr), :],
                                      gx_sems.at[tile % 2]).wait()

        @pl.when(i == 0)
        def _():
            start_slabs()
            cp = pltpu.make_async_copy(w_hbm, w_vmem, sem)
            cp.start()
            cp.wait()
            dnw_ref[...] = jnp.zeros_like(dnw_ref)

        dh = jnp.zeros((tr, D_MODEL), F32)
        half_cols = w_vmem.shape[-1]
        for j in range(na + nb):
            dp_ref, jj = (dpa_ref, j) if j < na else (dpb_ref, j - na)
            for half in range(2):
                dh = dh + _dot_nt(dp_ref[jj, :, half * half_cols:(half + 1) * half_cols], w_vmem[j, half])
        z = z_ref[...]
        r = lax.rsqrt(jnp.mean(z * z, axis=-1, keepdims=True) + EPS)
        n1 = z * r
        dnw_ref[...] += jnp.sum(dh * n1, axis=0, keepdims=True)
        gh = dh * nw_ref[...]
        dz = dz2_ref[...] + r * (gh - n1 * jnp.mean(gh * n1, axis=-1, keepdims=True))

        @pl.when(i >= 2)
        def _():
            wait_rows_out(i - 2)

        dz_buf[i % 2] = dz

        @pl.when(i == 0)
        def _():
            head_ref[...] = dz[0:CHUNK]
            pltpu.make_async_copy(dz_buf.at[0, pl.ds(CHUNK, tr - CHUNK), :], gx_hbm.at[pl.ds(0, tr - CHUNK), :],
                                  gx_sems.at[0]).start()

        @pl.when(i > 0)
        def _():
            pltpu.make_async_copy(dz_buf.at[i % 2], gx_hbm.at[pl.ds(i * tr - CHUNK, tr), :], gx_sems.at[i % 2]).start()

        @pl.when(i == nt - 1)
        def _():
            if nt >= 2:
                wait_rows_out(i - 1)
            wait_rows_out(i)
            finish_slabs()

    rowblk = pl.BlockSpec((tr, D_MODEL), lambda i: (i, 0))
    vec = pl.BlockSpec((1, D_MODEL), lambda i: (0, 0))
    return pl.pallas_call(
        body, name="in_projection_backward",
        out_shape=(jax.ShapeDtypeStruct((n_rows - CHUNK, D_MODEL), F32), jax.ShapeDtypeStruct((CHUNK, D_MODEL), F32),
                   jax.ShapeDtypeStruct((1, D_MODEL), F32),
                   jax.ShapeDtypeStruct((4,) + chip_sums.shape[1:], chip_sums.dtype)),
        grid=(nt,),
        in_specs=[pl.BlockSpec((na, tr, 1024), lambda i: (0, i, 0)), pl.BlockSpec((nb, tr, 1024), lambda i: (0, i, 0)),
                  ANY, rowblk, rowblk, vec, ANY],
        out_specs=(ANY, pl.BlockSpec((CHUNK, D_MODEL), lambda i: (0, 0)), vec, ANY),
        scratch_shapes=[pltpu.VMEM(w_blocks.shape, w_blocks.dtype), pltpu.SemaphoreType.DMA(()),
                        pltpu.VMEM((2, tr, D_MODEL), F32), pltpu.SemaphoreType.DMA((2,)),
                        pltpu.SemaphoreType.DMA((2,)), pltpu.SemaphoreType.DMA((3,)), pltpu.SemaphoreType.DMA(())],
        compiler_params=_params(("arbitrary",)),
    )(dp_a, dp_b, w_blocks, zp, dz2, norm_w, chip_sums)


def _weight_grad(xs, ys, name):
    shared = xs.ndim == 2
    n_rows, m = xs.shape[-2:]
    nb, _, n = ys.shape
    tk = _row_tile(n_rows, 4160)
    n_k = n_rows // tk

    def body(x_ref, y_ref, o_ref, acc):
        k = pl.program_id(1)

        @pl.when(k == 0)
        def _():
            acc[...] = jnp.zeros_like(acc)

        acc[...] += _dot_tn(x_ref[...] if shared else x_ref[0], y_ref[0])

        @pl.when(k == n_k - 1)
        def _():
            o_ref[0] = acc[...].astype(o_ref.dtype)

    x_spec = pl.BlockSpec((tk, m), lambda j, k: (k, 0)) if shared else pl.BlockSpec((1, tk, m), lambda j, k: (j, k, 0))
    return pl.pallas_call(
        body, name=name,
        out_shape=jax.ShapeDtypeStruct((nb, m, n), BF16),
        grid=(nb, n_k),
        in_specs=[x_spec, pl.BlockSpec((1, tk, n), lambda j, k: (j, k, 0))],
        out_specs=pl.BlockSpec((1, m, n), lambda j, k: (j, 0, 0)),
        scratch_shapes=[pltpu.VMEM((m, n), F32)],
        compiler_params=_params(("arbitrary", "arbitrary")),
    )(xs, ys)


def kernel(x, meta_tokens, norm_w, w_in, b_in, lb_logits, hg_norm_w, pool_w, pool_scale, w_down_hg, w_down_pool, w_out, final_norm_w, loss_target, m_meta_tokens, m_norm_w, m_w_in, m_b_in, m_lb_logits, m_hg_norm_w, m_pool_w, m_pool_scale, m_w_down_hg, m_w_down_pool, m_w_out, m_final_norm_w, v_meta_tokens, v_norm_w, v_w_in, v_b_in, v_lb_logits, v_hg_norm_w, v_pool_w, v_pool_scale, v_w_down_hg, v_w_down_pool, v_w_out, v_final_norm_w):
    seq = x.shape[1]

    meta_full = _all_gather_small(meta_tokens).transpose(1, 0, 2).reshape(N_META, D_MODEL)
    w_rest = jnp.concatenate([w_down_hg[0].astype(BF16), w_down_pool[0].astype(BF16), w_out[0].astype(BF16),
                              pool_w[0].astype(BF16).reshape(32, 1024)], axis=0)

    zp = jnp.concatenate([jnp.zeros((PAD_ROWS, D_MODEL), F32), meta_full, x[0]], axis=0)
    p, h, w_blocks = _in_projection(zp, norm_w, w_in[0].astype(BF16), b_in.reshape(N_COLBLK, 1, 1024), _gather_units())
    o, states, scores, rest = _hgrn_forward(p, lb_logits, w_rest)
    wdh = rest[:, REST_W_DOWN_HG:REST_W_DOWN_HG + 128].reshape(1024, 1024)
    wdp = rest[:, REST_W_DOWN_POOL:REST_W_DOWN_POOL + 128].reshape(1024, 1024)
    wout = rest[:, REST_W_OUT:REST_W_OUT + 128].reshape(1024, 1024)
    pw = rest[:, REST_POOL_W:REST_POOL_W + 32].reshape(N_DEV, 4, 32, 256).transpose(1, 0, 2, 3).reshape(4, 256, 256)
    (d_o, dp_b, dz2, grad_lhs, grad_rhs, dbias_b, d_hgw, d_pw, d_ps, d_fnw, loss_part) = _tail(
        p, o, zp, loss_target[0], hg_norm_w, pw, pool_scale, wdh, wdp, wout, final_norm_w.reshape(1, D_MODEL))
    n_a = N_COLBLK - dp_b.shape[0]
    g_hi = _weight_grad(h, dp_b, "weight_grad_in_hi")
    g_rows = _weight_grad(grad_lhs, grad_rhs, "weight_grad_rows")
    dp_a, dbias_a, d_lb, recv_hi, recv_rows = _hgrn_backward(p, lb_logits, states, scores, d_o, g_hi, n_a, g_rows)
    assert n_a == LOW_OWNERS
    chip_lo = _weight_grad_low(h, dp_a)
    dz_seq, dz_head, d_nw, recv_lo = _in_projection_backward(dp_a, dp_b, w_blocks, zp, dz2, norm_w, chip_lo)

    lb = jax.nn.sigmoid(lb_logits[0:1] - lb_logits[1:2])
    d_l0 = d_lb * lb * (1.0 - lb)
    replicated = jnp.concatenate([dbias_a.reshape(3, 1024), dbias_b.reshape(5, 1024), d_nw, d_l0, -d_l0, d_hgw, d_ps, d_fnw,
                                  jnp.pad(loss_part, ((0, MISC_ROWS - MISC_LOSS - 1), (0, 1023)))], axis=0)
    d_meta = dz_head[PAD_ROWS:CHUNK].reshape(N_META, N_DEV, 128).transpose(1, 0, 2)
    d_pw_blocks = d_pw.reshape(4, N_DEV, 32, 256).transpose(1, 0, 2, 3).reshape(N_DEV, 32, 1024)
    g_misc = jnp.concatenate([d_pw_blocks, jnp.pad(d_meta, ((0, 0), (0, 0), (0, 1024 - 128))),
                              jnp.broadcast_to(replicated[None], (N_DEV, 16, 1024))], axis=1)

    as_rows = lambda t, n: t.reshape(n, 1024)
    small = [(MISC_POOL_W, 1024, tuple(as_rows(t, 32) for t in (pool_w, m_pool_w, v_pool_w))),
             (MISC_META, 128, (meta_tokens, m_meta_tokens, v_meta_tokens)),
             (MISC_B_IN, 1024, tuple(as_rows(t, 8) for t in (b_in, m_b_in, v_b_in))),
             (MISC_NORM_W, 1024, (norm_w, m_norm_w, v_norm_w)),
             (MISC_LB, 1024, (lb_logits, m_lb_logits, v_lb_logits)),
             (MISC_HG_NORM_W, 1024, (hg_norm_w, m_hg_norm_w, v_hg_norm_w)),
             (MISC_POOL_SCALE, 1024, (pool_scale, m_pool_scale, v_pool_scale)),
             (MISC_FINAL_NORM_W, 1024, tuple(as_rows(t, 1) for t in (final_norm_w, m_final_norm_w, v_final_norm_w)))]
    res, loss = _finish(recv_hi, recv_lo, n_a, recv_rows, _exchange_small(g_misc), (w_in, m_w_in, v_w_in),
                  [(w_out, m_w_out, v_w_out), (w_down_hg, m_w_down_hg, v_w_down_hg), (w_down_pool, m_w_down_pool, v_w_down_pool)],
                  small)
    r_w_in, r_w_out, r_wdh, r_wdp, r_pw, r_meta, r_b_in, r_nw, r_lb, r_hgw, r_ps, r_fnw = res
    grad_x = dz_seq.reshape(1, seq, D_MODEL)
    per_kind = [(r_meta[k], r_nw[k], r_w_in[k], r_b_in[k].reshape(1, 8192), r_lb[k], r_hgw[k], r_pw[k].reshape(1, 4, 32, 256),
                 r_ps[k], r_wdh[k], r_wdp[k], r_w_out[k], r_fnw[k].reshape(1024)) for k in range(4)]
    return (loss, grad_x, *per_kind[0], *per_kind[1], *per_kind[2], *per_kind[3])
```

```python
import functools

import jax
import jax.numpy as jnp
from jax import lax
from jax.experimental import pallas as pl
from jax.experimental.pallas import tpu as pltpu

F32 = jnp.float32
BF16 = jnp.bfloat16

D_MODEL = 1024
N_META = 16
HEADS = 8
HEAD_DIM = 128
CHUNK = 64
SUB = 8
N_SUB = CHUNK // SUB
PAD_ROWS = CHUNK - N_META
POOL_WINDOWS = (2, 4, 8, 16)
POOL_GDIM = D_MODEL // len(POOL_WINDOWS)
HALO = 16
EPS = 1e-6
N_DEV = 8
N_COLBLK = 8
ADAM_LR, ADAM_B1, ADAM_B2, ADAM_EPS, ADAM_WD, ADAM_STEP = 0.001, 0.9, 0.999, 1e-08, 0.01, 10

VMEM_LIMIT = 56 * 1024 * 1024
MESH = pl.DeviceIdType.MESH
ANY = pl.BlockSpec(memory_space=pl.ANY)

REST_W_DOWN_HG = 0
REST_W_DOWN_POOL = 128
REST_W_OUT = 256
REST_POOL_W = 384
MISC_POOL_W = 0
MISC_META = 32
MISC_B_IN = 48
MISC_NORM_W = 56
MISC_LB = 57
MISC_HG_NORM_W = 59
MISC_POOL_SCALE = 60
MISC_FINAL_NORM_W = 61
MISC_LOSS = 62
MISC_ROWS = 64


def _params(sem=None):
    return pltpu.CompilerParams(dimension_semantics=sem, vmem_limit_bytes=VMEM_LIMIT)


def _row_tile(n_rows, prefer):
    best = 16
    for t in range(16, prefer + 1, 16):
        if n_rows % t == 0:
            best = t
    return best


def _sigmoid_pair(x):
    e = jnp.exp(-jnp.abs(x))
    r = 1.0 / (1.0 + e)
    er = e * r
    pos = x >= 0
    return jnp.where(pos, r, er), jnp.where(pos, er, r)


def _dot(a, b):
    return jnp.dot(a.astype(BF16), b.astype(BF16), preferred_element_type=F32)


def _dot_nt(a, b):
    return lax.dot_general(a.astype(BF16), b.astype(BF16), (((1,), (1,)), ((), ())), preferred_element_type=F32)


def _dot_tn(a, b):
    return lax.dot_general(a.astype(BF16), b.astype(BF16), (((0,), (0,)), ((), ())), preferred_element_type=F32)


def _device_index(px, py, pc):
    return 4 * px + 2 * py + pc


def _direct_gather(src_ref, dst_ref, send_sems, recv_sems, local_sem):
    x, y, c = lax.axis_index("x"), lax.axis_index("y"), lax.axis_index("c")
    own = pltpu.make_async_copy(src_ref, dst_ref.at[_device_index(x, y, c)], local_sem)
    sends, arrivals = [], []
    for k in range(1, N_DEV):
        peer = (1 - x if k & 4 else x, 1 - y if k & 2 else y, 1 - c if k & 1 else c)
        for slot, out in ((_device_index(x, y, c), sends), (_device_index(*peer), arrivals)):
            out.append(pltpu.make_async_remote_copy(
                src_ref=src_ref, dst_ref=dst_ref.at[slot], send_sem=send_sems.at[k - 1], recv_sem=recv_sems.at[k - 1],
                device_id=peer, device_id_type=MESH))
    return own, sends, arrivals


GATHER_SEMS = [pltpu.SemaphoreType.DMA((N_DEV - 1,)), pltpu.SemaphoreType.DMA((N_DEV - 1,)), pltpu.SemaphoreType.DMA(())]


def _all_gather_small(block):
    def body(x_ref, out_ref, send_sems, recv_sems, local_sem):
        own, sends, arrivals = _direct_gather(x_ref, out_ref, send_sems, recv_sems, local_sem)
        own.start()
        for cp in sends:
            cp.start()
        for cp in arrivals:
            cp.wait_recv()
        for cp in sends:
            cp.wait_send()
        own.wait()

    return pl.pallas_call(
        body, name="all_gather_meta",
        out_shape=jax.ShapeDtypeStruct((N_DEV,) + block.shape, block.dtype),
        in_specs=[ANY], out_specs=ANY, scratch_shapes=list(GATHER_SEMS),
    )(block)


def _peer(k):
    x, y, c = lax.axis_index("x"), lax.axis_index("y"), lax.axis_index("c")
    return (1 - x if k & 4 else x, 1 - y if k & 2 else y, 1 - c if k & 1 else c)


def _me():
    return _device_index(lax.axis_index("x"), lax.axis_index("y"), lax.axis_index("c"))


def _remote(src, dst, send_sem, recv_sem, peer_bits):
    return pltpu.make_async_remote_copy(src_ref=src, dst_ref=dst, send_sem=send_sem, recv_sem=recv_sem,
                                        device_id=_peer(peer_bits), device_id_type=MESH)


N_ROW_GRADS = 3
SCATTER_SEMS = [pltpu.SemaphoreType.DMA((N_DEV - 1,)), pltpu.SemaphoreType.DMA((N_DEV - 1,)), pltpu.SemaphoreType.DMA(())]
SCATTER_ROWS_SEMS = [pltpu.SemaphoreType.DMA((7 * N_ROW_GRADS,)), pltpu.SemaphoreType.DMA((7 * N_ROW_GRADS,)),
                     pltpu.SemaphoreType.DMA((N_ROW_GRADS,))]


def _scatter_slabs(g_ref, first, recv_ref, send_sems, recv_sems, local_sem):
    n = g_ref.shape[0]
    me = _me()

    def each(on_send, on_local, on_arrival):
        for kk in range(1, N_DEV):
            peer = jnp.bitwise_xor(me, kk)

            @pl.when((peer >= first) & (peer < first + n))
            def _(kk=kk, peer=peer):
                on_send(_remote(g_ref.at[peer - first], recv_ref.at[me], send_sems.at[kk - 1], recv_sems.at[kk - 1], kk))

        @pl.when((me >= first) & (me < first + n))
        def _():
            on_local(pltpu.make_async_copy(g_ref.at[me - first], recv_ref.at[me], local_sem))
            if on_arrival is not None:
                for kk in range(1, N_DEV):
                    on_arrival(_remote(g_ref.at[0], recv_ref.at[jnp.bitwise_xor(me, kk)], send_sems.at[kk - 1],
                                       recv_sems.at[kk - 1], kk))

    start = lambda: each(lambda cp: cp.start(), lambda cp: cp.start(), None)
    finish = lambda: each(lambda cp: cp.wait_send(), lambda cp: cp.wait(), lambda cp: cp.wait_recv())
    return start, finish


def _scatter_rows(g_ref, recv_ref, send_sems, recv_sems, local_sems):
    me = _me()
    rows = lambda m, dev: g_ref.at[m, pl.ds(dev * 128, 128), :]

    def copies():
        local = [pltpu.make_async_copy(rows(m, me), recv_ref.at[me, m], local_sems.at[m]) for m in range(N_ROW_GRADS)]
        sends, arrivals = [], []
        for m in range(N_ROW_GRADS):
            for kk in range(1, N_DEV):
                peer, sems = jnp.bitwise_xor(me, kk), (send_sems.at[7 * m + kk - 1], recv_sems.at[7 * m + kk - 1])
                sends.append(_remote(rows(m, peer), recv_ref.at[me, m], *sems, kk))
                arrivals.append(_remote(rows(m, me), recv_ref.at[peer, m], *sems, kk))
        return local, sends, arrivals

    def start():
        local, sends, _ = copies()
        for cp in local + sends:
            cp.start()

    def finish():
        local, sends, arrivals = copies()
        for cp in arrivals:
            cp.wait_recv()
        for cp in sends:
            cp.wait_send()
        for cp in local:
            cp.wait()

    return start, finish


LOW_OWNERS = 3


def _weight_grad_low(h, dp_a):
    n_rows = h.shape[0]
    tk = _row_tile(n_rows, 2080)
    n_k = n_rows // tk
    order = jnp.where(lax.axis_index("c") == 0, jnp.array([1, 0, 2], jnp.int32), jnp.array([0, 2, 1], jnp.int32))

    def body(order_ref, h_ref, y_ref, out_ref, acc, send_buf, got, send_sems, recv_sems):
        t, k = pl.program_id(0), pl.program_id(1)
        c = lax.axis_index("c")

        @pl.when(k == 0)
        def _():
            acc[...] = jnp.zeros_like(acc)

        acc[...] += _dot_tn(h_ref[...], y_ref[0])

        def to_sibling(slot):
            return _remote(send_buf.at[slot], got.at[slot], send_sems.at[slot], recv_sems.at[slot], 1)

        def send(slot):
            send_buf[slot] = acc[...].astype(send_buf.dtype)
            to_sibling(slot).start()

        def keep(slot):
            to_sibling(slot).wait_recv()
            out_ref[slot] = (acc[...] + got[slot].astype(F32)).astype(out_ref.dtype)

        done = k == n_k - 1
        for core, step, action, slot in ((0, 0, send, 0), (0, 1, keep, 0), (0, 2, keep, 1),
                                         (1, 0, send, 0), (1, 1, send, 1), (1, 2, keep, 0)):
            @pl.when(done & (c == core) & (t == step))
            def _(action=action, slot=slot):
                action(slot)

        @pl.when(done & (t == LOW_OWNERS - 1))
        def _():
            to_sibling(0).wait_send()

            @pl.when(c == 1)
            def _():
                to_sibling(1).wait_send()
                out_ref[1] = jnp.zeros(out_ref.shape[1:], out_ref.dtype)

    pair = (2, h.shape[1], dp_a.shape[2])
    return pl.pallas_call(
        body, name="weight_grad_in_low",
        out_shape=jax.ShapeDtypeStruct(pair, BF16),
        grid_spec=pltpu.PrefetchScalarGridSpec(
            num_scalar_prefetch=1, grid=(LOW_OWNERS, n_k),
            in_specs=[pl.BlockSpec((tk, h.shape[1]), lambda t, k, o: (k, 0)),
                      pl.BlockSpec((1, tk, dp_a.shape[2]), lambda t, k, o: (o[t], k, 0))],
            out_specs=pl.BlockSpec(pair, lambda t, k, o: (0, 0, 0)),
            scratch_shapes=[pltpu.VMEM(pair[1:], F32), pltpu.VMEM(pair, BF16), pltpu.VMEM(pair, BF16),
                            pltpu.SemaphoreType.DMA((2,)), pltpu.SemaphoreType.DMA((2,))]),
        compiler_params=_params(("arbitrary", "arbitrary")),
    )(order, h, dp_a)


def _exchange_small(misc):
    def body(m_ref, out_ref, send_sems, recv_sems, local_sem):
        start, finish = _scatter_slabs(m_ref, 0, out_ref, send_sems, recv_sems, local_sem)
        start()
        finish()

    return pl.pallas_call(
        body, name="exchange_small", out_shape=jax.ShapeDtypeStruct(misc.shape, misc.dtype),
        in_specs=[ANY], out_specs=ANY, scratch_shapes=list(SCATTER_SEMS),
    )(misc)


def _scatter_low(part_ref, recv_ref, send_sems, recv_sems, local_sem):
    x, y, c = lax.axis_index("x"), lax.axis_index("y"), lax.axis_index("c")
    chip = 2 * x + y
    routes = ((0, (0, 0, c), 0, None), (1, (0, 1, 0), 1, 0))

    def each(on_send, on_local, on_arrival):
        for slot, owner, owner_chip, core in routes:
            holds = (c == core) if core is not None else (c >= 0)
            rel = jnp.bitwise_xor(chip, owner_chip)

            @pl.when(holds & (rel != 0))
            def _(slot=slot, owner=owner, rel=rel):
                on_send(pltpu.make_async_remote_copy(
                    src_ref=part_ref.at[slot], dst_ref=recv_ref.at[chip], send_sem=send_sems.at[slot],
                    recv_sem=recv_sems.at[rel - 1], device_id=owner, device_id_type=MESH))

            @pl.when(holds & (rel == 0))
            def _(slot=slot, owner=owner, owner_chip=owner_chip):
                on_local(pltpu.make_async_copy(part_ref.at[slot], recv_ref.at[chip], local_sem))
                if on_arrival is not None:
                    for r in range(1, 4):
                        on_arrival(pltpu.make_async_remote_copy(
                            src_ref=part_ref.at[slot], dst_ref=recv_ref.at[r ^ owner_chip], send_sem=send_sems.at[slot],
                            recv_sem=recv_sems.at[r - 1], device_id=owner, device_id_type=MESH))

    start = lambda: each(lambda cp: cp.start(), lambda cp: cp.start(), None)
    finish = lambda: each(lambda cp: cp.wait_send(), lambda cp: cp.wait(), lambda cp: cp.wait_recv())
    return start, finish


def _adam_update(g, w, m, v):
    mn = ADAM_B1 * m + (1.0 - ADAM_B1) * g
    vn = ADAM_B2 * v + (1.0 - ADAM_B2) * (g * g)
    m_hat = mn / (1.0 - ADAM_B1 ** ADAM_STEP)
    v_hat = vn / (1.0 - ADAM_B2 ** ADAM_STEP)
    return -ADAM_LR * (m_hat / (jnp.sqrt(v_hat) + ADAM_EPS) + ADAM_WD * w), mn, vn


def _device_sum(parts):
    t = [p.astype(F32) for p in parts]
    return ((t[0] + t[1]) + (t[2] + t[3])) + ((t[4] + t[5]) + (t[6] + t[7]))


def _finish(a_hi, a_lo, n_lo, b3, misc, big, rows3, small):
    n_steps = 4
    tb, tr3 = 1024 // n_steps, 128 // n_steps

    def body(*refs):
        it = iter(refs)
        hi_ref, lo_ref, b_ref, m_ref = next(it), next(it), next(it), next(it)
        big_in = [next(it) for _ in range(3)]
        rows_in = [[next(it) for _ in range(3)] for _ in rows3]
        small_in = [[next(it) for _ in range(3)] for _ in small]
        big_out = [next(it) for _ in range(4)]
        rows_out = [[next(it) for _ in range(4)] for _ in rows3]
        small_out = [[next(it) for _ in range(4)] for _ in small]
        loss_ref = next(it)

        def apply(g, ins, outs):
            d, mn, vn = _adam_update(g, ins[0][...], ins[1][...], ins[2][...])
            for r, val in zip(outs, (g, d, mn, vn)):
                r[...] = val

        lo = [lo_ref[s].astype(F32) for s in range(4)]
        g_big = jnp.where(_me() < n_lo, (lo[0] + lo[1]) + (lo[2] + lo[3]), _device_sum([hi_ref[s] for s in range(N_DEV)]))
        apply(g_big[None], big_in, big_out)
        for k in range(len(rows3)):
            apply(_device_sum([b_ref[s, k] for s in range(N_DEV)])[None], rows_in[k], rows_out[k])

        @pl.when(pl.program_id(0) == 0)
        def _():
            loss_ref[...] = _device_sum([m_ref[s, MISC_LOSS:MISC_LOSS + 1, :] for s in range(N_DEV)])
            for (row0, lanes, ins), r_in, r_out in zip(small, small_in, small_out):
                n = ins[0].shape[0]
                apply(_device_sum([m_ref[s, row0:row0 + n, :lanes] for s in range(N_DEV)]), r_in, r_out)

    whole = lambda shape: pl.BlockSpec(shape, lambda i: (0,) * len(shape))
    big_blk = pl.BlockSpec((1, tb, 1024), lambda i: (0, i, 0))
    rows_blk = pl.BlockSpec((1, tr3, 1024), lambda i: (0, i, 0))
    in_specs = [pl.BlockSpec((N_DEV, tb, 1024), lambda i: (0, i, 0)), pl.BlockSpec((4, tb, 1024), lambda i: (0, i, 0)),
                pl.BlockSpec((N_DEV, 3, tr3, 1024), lambda i: (0, 0, i, 0)), whole(misc.shape)]
    in_specs += [big_blk] * 3 + [rows_blk] * (3 * len(rows3))
    out_specs = [big_blk] * 4 + [rows_blk] * (4 * len(rows3))
    out_shape = [jax.ShapeDtypeStruct(big[0].shape, F32)] * 4
    for w, _, _ in rows3:
        out_shape += [jax.ShapeDtypeStruct(w.shape, F32)] * 4
    args = [a_hi, a_lo, b3, misc, *big]
    for t in rows3:
        args += list(t)
    for _, _, t in small:
        in_specs += [whole(t[0].shape)] * 3
        out_specs += [whole(t[0].shape)] * 4
        out_shape += [jax.ShapeDtypeStruct(t[0].shape, F32)] * 4
        args += list(t)
    out_specs.append(whole((1, 1024)))
    out_shape.append(jax.ShapeDtypeStruct((1, 1024), F32))
    outs = pl.pallas_call(
        body, name="reduce_sum_adamw", out_shape=tuple(out_shape), grid=(n_steps,),
        in_specs=in_specs, out_specs=tuple(out_specs),
        compiler_params=_params(("arbitrary",)),
    )(*args)
    return [tuple(outs[4 * k:4 * k + 4]) for k in range(len(outs) // 4)], outs[-1][0, 0]


GATHER_UNITS = ((0, 0), (0, 1), (1, 0), (1, 1)) + tuple((place, 0) for place in range(2, 8)) + tuple(
    (place, 1) for place in range(2, 8))


def _gather_order():
    x, y, c = lax.axis_index("x"), lax.axis_index("y"), lax.axis_index("c")
    chips = [(1 - x, y), (x, 1 - y), (1 - x, 1 - y)]
    order = [_device_index(x, y, c), _device_index(x, y, 1 - c)]
    order += [_device_index(*q, c) for q in chips] + [_device_index(*q, 1 - c) for q in chips]
    return order


def _gather_units():
    order = _gather_order()
    blocks = jnp.stack([order[place] for place, _ in GATHER_UNITS]).astype(jnp.int32)
    return blocks, jnp.array([half for _, half in GATHER_UNITS], jnp.int32)


def _in_projection(zp, norm_w, w_shard, b_blocks, units):
    n_rows = zp.shape[0]
    tr = _row_tile(n_rows, 1040)
    nt = n_rows // tr

    half_cols = 1024 // 2
    n_units = len(GATHER_UNITS)

    def body(blocks_ref, halves_ref, z_ref, nw_ref, w_hbm, b_ref, p_ref, h_ref, w_out, w_vmem, h_all,
             send_sems, recv_sems, local_sem, out_sems):
        s, i = pl.program_id(0), pl.program_id(1)
        x, y, c = lax.axis_index("x"), lax.axis_index("y"), lax.axis_index("c")
        me, sibling = (x, y, c), (x, y, 1 - c)
        chips = [(1 - x, y), (x, 1 - y), (1 - x, 1 - y)]

        def slot(blk, half):
            return w_vmem.at[_device_index(*blk), half]

        def mine(half):
            return w_hbm.at[:, pl.ds(half * half_cols, half_cols)]

        def copy(k, half, blk, to, own_block=False):
            return pltpu.make_async_remote_copy(
                src_ref=mine(half) if own_block else slot(blk, half), dst_ref=slot(blk, half),
                send_sem=send_sems.at[2 * k + half], recv_sem=recv_sems.at[2 * k + half], device_id=to, device_id_type=MESH)

        own = [pltpu.make_async_copy(mine(half), slot(me, half), local_sem.at[half]) for half in range(2)]
        first = [copy(k, half, me, to, own_block=True) for half in range(2)
                 for k, to in enumerate([sibling] + [(*q, c) for q in chips])]
        passed = [[copy(4 + j, half, (*q, c), sibling) for j, q in enumerate(chips)] for half in range(2)]
        sources = [None, sibling] + [(*q, c) for q in chips] + [(*q, 1 - c) for q in chips]
        arrival = lambda place, half: copy(place - 1, half, sources[place], me)

        def keep(unit):
            blk, half = blocks_ref[unit], halves_ref[unit]
            return pltpu.make_async_copy(w_vmem.at[blk, half], w_out.at[blk, half], out_sems.at[unit])

        for unit, (place, half) in enumerate(GATHER_UNITS):
            @pl.when((i == 0) & (s == unit))
            def _(unit=unit, place=place, half=half):
                if unit == 0:
                    for cp in own + first:
                        cp.start()
                if place == 0:
                    own[half].wait()
                else:
                    arrival(place, half).wait_recv()
                    if 2 <= place <= 4:
                        passed[half][place - 2].start()
                keep(unit).start()

        @pl.when(s == 0)
        def _():
            z = z_ref[...]
            r = lax.rsqrt(jnp.mean(z * z, axis=-1, keepdims=True) + EPS)
            h = (z * r * nw_ref[...]).astype(BF16)
            h_all[i] = h
            h_ref[...] = h

        p_ref[0] = jnp.dot(h_all[i], w_vmem[blocks_ref[s], halves_ref[s]], preferred_element_type=F32) + b_ref[0]

        @pl.when((s == n_units - 1) & (i == nt - 1))
        def _():
            for cp in first + passed[0] + passed[1]:
                cp.wait_send()
            for unit in range(n_units):
                keep(unit).wait()

    first_pass = lambda s, i, rest: jnp.where(s == 0, i, rest)
    return pl.pallas_call(
        body, name="in_projection_gather",
        out_shape=(jax.ShapeDtypeStruct((N_COLBLK, n_rows, 1024), F32),
                   jax.ShapeDtypeStruct((n_rows, D_MODEL), BF16),
                   jax.ShapeDtypeStruct((N_DEV, 2, D_MODEL, half_cols), BF16)),
        grid_spec=pltpu.PrefetchScalarGridSpec(
            num_scalar_prefetch=2, grid=(n_units, nt),
            in_specs=[pl.BlockSpec((tr, D_MODEL), lambda s, i, blk, hf: (first_pass(s, i, 0), 0)),
                      pl.BlockSpec((1, D_MODEL), lambda s, i, blk, hf: (0, 0)), ANY,
                      pl.BlockSpec((1, 1, half_cols), lambda s, i, blk, hf: (blk[s], 0, hf[s]))],
            out_specs=(pl.BlockSpec((1, tr, half_cols), lambda s, i, blk, hf: (blk[s], i, hf[s])),
                       pl.BlockSpec((tr, D_MODEL), lambda s, i, blk, hf: (first_pass(s, i, nt - 1), 0)), ANY),
            scratch_shapes=[pltpu.VMEM((N_DEV, 2, D_MODEL, half_cols), BF16), pltpu.VMEM((nt, tr, D_MODEL), BF16),
                            pltpu.SemaphoreType.DMA((14,)), pltpu.SemaphoreType.DMA((14,)), pltpu.SemaphoreType.DMA((2,)),
                            pltpu.SemaphoreType.DMA((n_units,))]),
        compiler_params=_params(("arbitrary", "arbitrary")),
    )(*units, zp, norm_w, w_shard, b_blocks)


def _lower_bound(lb_ref):
    l0, l1 = lb_ref[0:1, :], lb_ref[1:2, :]
    _, lb = _sigmoid_pair(l1 - l0)
    return lb


def _chunk_gates(fz, lb, valid):
    sig, nsig = _sigmoid_pair(fz)
    f = lb + (1.0 - lb) * sig
    g2 = jnp.where(valid, jnp.log2(f), 0.0)
    k = jnp.where(valid, (1.0 - lb) * nsig, 0.0)
    return sig, nsig, f, g2, k


def _tri(n, upper=False):
    r = lax.broadcasted_iota(jnp.int32, (n, n), 0)
    c = lax.broadcasted_iota(jnp.int32, (n, n), 1)
    return jnp.where((r <= c) if upper else (r >= c), 1.0, 0.0).astype(BF16)


def _tri_dot(tri, x):
    hi = x.astype(BF16)
    rest = x - hi.astype(F32)
    mid = rest.astype(BF16)
    low = (rest - mid.astype(F32)).astype(BF16)
    return (jnp.dot(tri, hi, preferred_element_type=F32) + jnp.dot(tri, mid, preferred_element_type=F32)
            + jnp.dot(tri, low, preferred_element_type=F32))


def _intra_scores(q_ref, k_ref, b2_ref, col0):
    cols = pl.ds(col0, HEAD_DIM)
    rows_s = lax.broadcasted_iota(jnp.int32, (SUB, 1), 0)
    lanes_c = lax.broadcasted_iota(jnp.int32, (1, CHUNK), 1)
    blocks = []
    for i in range(N_SUB):
        lo = i * SUB
        qi = q_ref[lo:lo + SUB, cols]
        bi = b2_ref[lo:lo + SUB, cols]
        if i == 0:
            acc = jnp.zeros((SUB, CHUNK), F32)
        else:
            ref_i = b2_ref[lo:lo + 1, cols]
            qt = qi * jnp.exp2(bi - ref_i)
            kt = jnp.concatenate([k_ref[0:lo, cols] * jnp.exp2(ref_i - b2_ref[0:lo, cols]),
                                  jnp.zeros((CHUNK - lo, HEAD_DIM), F32)], axis=0)
            acc = _dot_nt(qt, kt)
        for s in range(SUB):
            b_s = b2_ref[lo + s:lo + s + 1, cols]
            k_s = k_ref[lo + s:lo + s + 1, cols]
            w = jnp.exp2(jnp.minimum(bi - b_s, 0.0))
            col = jnp.sum((qi * w) * k_s, axis=-1, keepdims=True)
            acc = jnp.where(lanes_c == lo + s, col, acc)
        blocks.append(jnp.where(lanes_c <= lo + rows_s, acc, 0.0))
    return jnp.concatenate(blocks, axis=0)


def _hgrn_forward(p, lb_logits, w_rest):
    n_rows = p.shape[1]
    n_chunks = n_rows // CHUNK
    width = HEADS * HEAD_DIM
    per_step = next(c for c in (13, 5, 3, 1) if n_chunks % c == 0)
    n_steps = n_chunks // per_step
    span = per_step * CHUNK

    def body(q_ref, fz_ref, v_ref, lb_ref, rest_ref, o_ref, st_out_ref, a_out_ref, rest_out,
             state, k_all, b2_all, send_sems, recv_sems, local_sem):
        n = pl.program_id(0)
        own, sends, arrivals = _direct_gather(rest_ref, rest_out, send_sems, recv_sems, local_sem)

        @pl.when(n == 0)
        def _():
            state[...] = jnp.zeros_like(state)
            own.start()
            for cp in sends:
                cp.start()

        lb = _lower_bound(lb_ref)
        for cc in range(per_step):
            r0 = cc * CHUNK
            k_vmem, b2_vmem = k_all.at[cc], b2_all.at[cc]
            rows = (n * per_step + cc) * CHUNK + lax.broadcasted_iota(jnp.int32, (CHUNK, 1), 0)
            valid = rows >= PAD_ROWS
            _, _, _, g2, k = _chunk_gates(fz_ref[0, r0:r0 + CHUNK, :], lb, valid)
            k_vmem[...] = k
            b2_vmem[...] = _tri_dot(_tri(CHUNK), g2)
            q_view = q_ref.at[0, pl.ds(r0, CHUNK), :]
            for h in range(HEADS):
                cols = pl.ds(h * HEAD_DIM, HEAD_DIM)
                st = state[h]
                st_out_ref[cc, h] = st
                bh = b2_vmem[:, cols]
                kh = k_vmem[:, cols]
                vh = jnp.where(valid, v_ref[0, r0:r0 + CHUNK, cols], 0.0)
                qe = q_view[:, cols] * jnp.exp2(bh)
                a = _intra_scores(q_view, k_vmem, b2_vmem, h * HEAD_DIM).astype(BF16)
                a_out_ref[cc, h] = a
                o_ref[r0:r0 + CHUNK, cols] = _dot_nt(qe, st) + _dot(a, vh)
                b_last = b2_vmem[CHUNK - 1:CHUNK, cols]
                kd = kh * jnp.exp2(b_last - bh)
                state[h] = st * jnp.exp2(b_last) + _dot_tn(vh, kd)

        @pl.when(n == n_steps - 1)
        def _():
            for cp in arrivals:
                cp.wait_recv()
            for cp in sends:
                cp.wait_send()
            own.wait()

    blk = lambda c: pl.BlockSpec((1, span, width), lambda n, c=c: (c, n, 0))
    return pl.pallas_call(
        body, name="hgrn_forward",
        out_shape=(jax.ShapeDtypeStruct((n_rows, width), F32),
                   jax.ShapeDtypeStruct((n_chunks, HEADS, HEAD_DIM, HEAD_DIM), F32),
                   jax.ShapeDtypeStruct((n_chunks, HEADS, CHUNK, CHUNK), BF16),
                   jax.ShapeDtypeStruct((N_DEV,) + w_rest.shape, w_rest.dtype)),
        grid=(n_steps,),
        in_specs=[blk(0), blk(1), blk(2), pl.BlockSpec((2, width), lambda n: (0, 0)), ANY],
        out_specs=(pl.BlockSpec((span, width), lambda n: (n, 0)),
                   pl.BlockSpec((per_step, HEADS, HEAD_DIM, HEAD_DIM), lambda n: (n, 0, 0, 0)),
                   pl.BlockSpec((per_step, HEADS, CHUNK, CHUNK), lambda n: (n, 0, 0, 0)), ANY),
        scratch_shapes=[pltpu.VMEM((HEADS, HEAD_DIM, HEAD_DIM), F32), pltpu.VMEM((per_step, CHUNK, width), F32),
                        pltpu.VMEM((per_step, CHUNK, width), F32)] + list(GATHER_SEMS),
        compiler_params=_params(("arbitrary",)),
    )(p, p, p, lb_logits, w_rest)


def _hgrn_backward(p, lb_logits, states, scores, d_o, g_slabs, first_owner, g_rows):
    n_rows = p.shape[1]
    n_chunks = n_rows // CHUNK
    width = HEADS * HEAD_DIM

    def body(q_ref, fz_ref, v_ref, lb_ref, st_ref, a_ref, do_ref, gs_hbm, gr_hbm, dp_ref, dbias_ref, dlb_ref, rs_hbm, rr_hbm,
             dstate, k_vmem, b2_vmem, *sems):
        step = pl.program_id(0)
        n = n_chunks - 1 - step
        start_slabs, finish_slabs = _scatter_slabs(gs_hbm, first_owner, rs_hbm, *sems[:3])
        start_rows, finish_rows = _scatter_rows(gr_hbm, rr_hbm, *sems[3:])

        @pl.when(step == 0)
        def _():
            dstate[...] = jnp.zeros_like(dstate)
            dbias_ref[...] = jnp.zeros_like(dbias_ref)
            dlb_ref[...] = jnp.zeros_like(dlb_ref)
            start_slabs()
            start_rows()

        rows = n * CHUNK + lax.broadcasted_iota(jnp.int32, (CHUNK, 1), 0)
        valid = rows >= PAD_ROWS
        lb = _lower_bound(lb_ref)
        sig, nsig, f, g2, k = _chunk_gates(fz_ref[0], lb, valid)
        k_vmem[...] = k
        b2_vmem[...] = _tri_dot(_tri(CHUNK), g2)
        rows_c = lax.broadcasted_iota(jnp.int32, (CHUNK, 1), 0)
        rows_s = lax.broadcasted_iota(jnp.int32, (SUB, 1), 0)
        lanes_c = lax.broadcasted_iota(jnp.int32, (1, CHUNK), 1)
        causal = lax.broadcasted_iota(jnp.int32, (CHUNK, CHUNK), 0) >= lax.broadcasted_iota(jnp.int32, (CHUNK, CHUNK), 1)
        tri_up = _tri(CHUNK, upper=True)
        for h in range(HEADS):
            cols = pl.ds(h * HEAD_DIM, HEAD_DIM)
            st = st_ref[0, h]
            dst = dstate[h]
            qh = q_ref[0, :, cols]
            bh = b2_vmem[:, cols]
            kh = k_vmem[:, cols]
            vh = jnp.where(valid, v_ref[0, :, cols], 0.0)
            doh = do_ref[:, cols]
            eb = jnp.exp2(bh)
            qe = qh * eb
            b_last = b2_vmem[CHUNK - 1:CHUNK, cols]
            e_last = jnp.exp2(b_last)
            decay_k = jnp.exp2(b_last - bh)
            kd = kh * decay_k
            dqe = _dot(doh, st)
            da = jnp.where(causal, _dot_nt(doh, vh), 0.0)
            dv = _dot_tn(a_ref[0, h], doh) + _dot_nt(kd, dst)
            dkd = _dot(vh, dst)
            dstate[h] = dst * e_last + _dot_tn(doh, qe)
            db_last = (jnp.sum(dst * st, axis=0, keepdims=True) * e_last
                       + jnp.sum(dkd * kd, axis=0, keepdims=True))
            dq_blocks, dk_blocks = [], []
            dk_earlier = jnp.zeros((CHUNK, HEAD_DIM), F32)
            for i in range(N_SUB):
                lo = i * SUB
                qi = q_ref[0, lo:lo + SUB, cols]
                bi = b2_vmem[lo:lo + SUB, cols]
                da_i = da[lo:lo + SUB, :]
                if i == 0:
                    dq_i = jnp.zeros((SUB, HEAD_DIM), F32)
                else:
                    ref_i = b2_vmem[lo:lo + 1, cols]
                    eq = jnp.exp2(bi - ref_i)
                    ek = jnp.exp2(ref_i - b2_vmem[0:lo, cols])
                    later = jnp.zeros((CHUNK - lo, HEAD_DIM), F32)
                    kt = jnp.concatenate([k_vmem[0:lo, cols] * ek, later], axis=0)
                    dq_i = _dot(da_i, kt) * eq
                    dk_earlier = dk_earlier + jnp.concatenate([_dot_tn(da_i, qi * eq)[0:lo] * ek, later], axis=0)
                dk_i = jnp.zeros((SUB, HEAD_DIM), F32)
                for s in range(SUB):
                    b_s = b2_vmem[lo + s:lo + s + 1, cols]
                    k_s = k_vmem[lo + s:lo + s + 1, cols]
                    w = jnp.exp2(jnp.minimum(bi - b_s, 0.0))
                    da_col = jnp.sum(jnp.where(lanes_c == lo + s, da_i, 0.0), axis=-1, keepdims=True)
                    gw = da_col * w
                    dq_i = dq_i + gw * k_s
                    dk_i = jnp.where(rows_s == s, jnp.sum(gw * qi, axis=0, keepdims=True), dk_i)
                dq_blocks.append(dq_i)
                dk_blocks.append(dk_i)
            dq_intra = jnp.concatenate(dq_blocks, axis=0)
            dk_intra = jnp.concatenate(dk_blocks, axis=0) + dk_earlier
            dq = dqe * eb + dq_intra
            dk = dkd * decay_k + dk_intra
            db = dqe * qe - dkd * kd + qh * dq_intra - kh * dk_intra
            db = db + jnp.where(rows_c == CHUNK - 1, db_last, 0.0)
            dg = _tri_dot(tri_up, db)
            fh = f[:, h * HEAD_DIM:(h + 1) * HEAD_DIM]
            sh = sig[:, h * HEAD_DIM:(h + 1) * HEAD_DIM]
            nh = nsig[:, h * HEAD_DIM:(h + 1) * HEAD_DIM]
            lbh = lb[:, h * HEAD_DIM:(h + 1) * HEAD_DIM]
            df = jnp.where(valid, dg / fh - dk, 0.0)
            dfz = df * (1.0 - lbh) * sh * nh
            dq = jnp.where(valid, dq, 0.0)
            dv = jnp.where(valid, dv, 0.0)
            dlb_ref[:, cols] += jnp.sum(df * nh, axis=0, keepdims=True)
            dp_ref[0, :, cols] = dq.astype(BF16)
            dp_ref[1, :, cols] = dfz.astype(BF16)
            dp_ref[2, :, cols] = dv.astype(BF16)
            dbias_ref[0, :, cols] += jnp.sum(dq, axis=0, keepdims=True)
            dbias_ref[1, :, cols] += jnp.sum(dfz, axis=0, keepdims=True)
            dbias_ref[2, :, cols] += jnp.sum(dv, axis=0, keepdims=True)

        @pl.when(step == n_chunks - 1)
        def _():
            finish_slabs()
            finish_rows()

    rev = lambda s: n_chunks - 1 - s
    blk = lambda c: pl.BlockSpec((1, CHUNK, width), lambda s, c=c: (c, rev(s), 0))
    return pl.pallas_call(
        body, name="hgrn_backward",
        out_shape=(jax.ShapeDtypeStruct((3, n_rows, width), BF16),
                   jax.ShapeDtypeStruct((3, 1, width), F32),
                   jax.ShapeDtypeStruct((1, width), F32),
                   jax.ShapeDtypeStruct((N_DEV,) + g_slabs.shape[1:], g_slabs.dtype),
                   jax.ShapeDtypeStruct((N_DEV, N_ROW_GRADS, 128, g_rows.shape[2]), g_rows.dtype)),
        grid=(n_chunks,),
        in_specs=[blk(0), blk(1), blk(2), pl.BlockSpec((2, width), lambda s: (0, 0)),
                  pl.BlockSpec((1, HEADS, HEAD_DIM, HEAD_DIM), lambda s: (rev(s), 0, 0, 0)),
                  pl.BlockSpec((1, HEADS, CHUNK, CHUNK), lambda s: (rev(s), 0, 0, 0)),
                  pl.BlockSpec((CHUNK, width), lambda s: (rev(s), 0)), ANY, ANY],
        out_specs=(pl.BlockSpec((3, CHUNK, width), lambda s: (0, rev(s), 0)),
                   pl.BlockSpec((3, 1, width), lambda s: (0, 0, 0)),
                   pl.BlockSpec((1, width), lambda s: (0, 0)), ANY, ANY),
        scratch_shapes=[pltpu.VMEM((HEADS, HEAD_DIM, HEAD_DIM), F32), pltpu.VMEM((CHUNK, width), F32),
                        pltpu.VMEM((CHUNK, width), F32)] + list(SCATTER_SEMS) + list(SCATTER_ROWS_SEMS),
        compiler_params=_params(("arbitrary",)),
    )(p, p, p, lb_logits, states, scores, d_o, g_slabs, g_rows)


def _sigmoid_and_complement(x):
    s = 0.5 * jnp.tanh(0.5 * x) + 0.5
    return s, 1.0 - s


def _silu_and_grad(x):
    s, ns = _sigmoid_and_complement(x)
    return x * s, s * (1.0 + x * ns)


def _tail(p, o, zp, tgt, hg_norm_w, pool_w, pool_scale, w_down_hg, w_down_pool, w_out, final_norm_w):
    n_rows = zp.shape[0]
    tr = _row_tile(n_rows, 208)
    nt = n_rows // tr
    ext = tr + HALO
    n_groups = len(POOL_WINDOWS)

    def body(o_ref, ghg_ref, u_ref, gpool_ref, mhg_ref, mpool_ref, uhalo_ref, z_ref, tgt_hbm,
             hgw_ref, pw_ref, ps_ref, wdh_ref, wdp_ref, wout_ref, fnw_ref,
             do_ref, dp_ref, dz2_ref, lhs_ref, rhs_ref,
             dbias_ref, dhgw_ref, dpw_ref, dps_ref, dfnw_ref, loss_ref, halo_vmem, tgt_buf, tgt_sems):
        step = pl.program_id(0)
        ti = nt - 1 - step

        def target_rows(tile, slot, act):
            @pl.when(tile == 0)
            def _():
                cp = pltpu.make_async_copy(tgt_hbm.at[pl.ds(0, tr - CHUNK), :], tgt_buf.at[slot, pl.ds(CHUNK, tr - CHUNK), :],
                                           tgt_sems.at[slot])
                getattr(cp, act)()

            @pl.when(tile > 0)
            def _():
                cp = pltpu.make_async_copy(tgt_hbm.at[pl.ds(tile * tr - CHUNK, tr), :], tgt_buf.at[slot], tgt_sems.at[slot])
                getattr(cp, act)()

        @pl.when(step == 0)
        def _():
            halo_vmem[...] = jnp.zeros_like(halo_vmem)
            for r in (dbias_ref, dhgw_ref, dpw_ref, dps_ref, dfnw_ref, loss_ref):
                r[...] = jnp.zeros_like(r)
            if nt <= 2:
                tgt_buf[(nt - 1) % 2, 0:CHUNK, :] = jnp.zeros((CHUNK, D_MODEL), F32)
            target_rows(ti, 0, "start")

        @pl.when(ti > 0)
        def _():
            target_rows(ti - 1, (step + 1) % 2, "start")

        rows = ti * tr + lax.broadcasted_iota(jnp.int32, (tr, 1), 0)
        valid = rows >= PAD_ROWS
        in_loss = rows >= CHUNK
        count_pos = jnp.maximum(rows - PAD_ROWS + 1, 1).astype(F32)

        o = o_ref[...]
        hgw = hgw_ref[...]
        inv_o, on_parts = [], []
        for h in range(HEADS):
            oh = o[:, h * HEAD_DIM:(h + 1) * HEAD_DIM]
            r = lax.rsqrt(jnp.mean(oh * oh, axis=-1, keepdims=True) + EPS)
            inv_o.append(r)
            on_parts.append(oh * r)
        o_hat = jnp.concatenate(on_parts, axis=1)
        o_n = o_hat * hgw
        g_hg = ghg_ref[0]
        silu_hg, dsilu_hg = _silu_and_grad(g_hg)
        a_hg = o_n * silu_hg
        y_hg = _dot(a_hg, wdh_ref[...])

        u = jnp.where(valid, u_ref[0], 0.0)
        u_prev = jnp.where(ti > 0, uhalo_ref[0], 0.0)
        u_ext = jnp.concatenate([u_prev, u], axis=0)
        pooled_parts, mixed_parts, inv_cnt = [], [], []
        for gi, win in enumerate(POOL_WINDOWS):
            lanes = slice(gi * POOL_GDIM, (gi + 1) * POOL_GDIM)
            s = u_ext[:, lanes]
            shift = 1
            while shift < win:
                s = s + pltpu.roll(s, shift, 0)
                shift *= 2
            ic = 1.0 / jnp.minimum(count_pos, float(win))
            inv_cnt.append(ic)
            pooled = s[HALO:] * ic - u[:, lanes]
            pooled_parts.append(pooled)
            mixed_parts.append(_dot(pooled, pw_ref[gi]))
        mixed = jnp.concatenate(mixed_parts, axis=1)
        ps = ps_ref[...]
        g_pool = gpool_ref[0]
        silu_pool, dsilu_pool = _silu_and_grad(g_pool)
        a_pool = mixed * ps * silu_pool
        y_pool = _dot(a_pool, wdp_ref[...])

        m_hg, m_pool = mhg_ref[0], mpool_ref[0]
        s_hg, ns_hg = _sigmoid_and_complement(m_hg)
        s_pool, ns_pool = _sigmoid_and_complement(m_pool)
        merged = s_hg * y_hg + s_pool * y_pool
        z2 = z_ref[...] + _dot(merged, wout_ref[...])
        r2 = lax.rsqrt(jnp.mean(z2 * z2, axis=-1, keepdims=True) + EPS)
        n2 = z2 * r2
        fnw = fnw_ref[...]
        target_rows(ti, step % 2, "wait")
        err = jnp.where(in_loss, n2 * fnw - tgt_buf[step % 2], 0.0)
        loss_ref[...] += jnp.sum(jnp.sum(err * err, axis=0, keepdims=True), axis=1, keepdims=True) * (0.5 / D_MODEL)
        dy = err * (1.0 / D_MODEL)

        dfnw_ref[...] += jnp.sum(dy * n2, axis=0, keepdims=True)
        gy = dy * fnw
        dz2 = r2 * (gy - n2 * jnp.mean(gy * n2, axis=-1, keepdims=True))
        dmerged = _dot_nt(dz2, wout_ref[...])
        dy_hg = s_hg * dmerged
        dy_pool = s_pool * dmerged
        dm_hg = dmerged * y_hg * s_hg * ns_hg
        dm_pool = dmerged * y_pool * s_pool * ns_pool
        da_hg = _dot_nt(dy_hg, wdh_ref[...])
        da_pool = _dot_nt(dy_pool, wdp_ref[...])

        d_on = da_hg * silu_hg
        dg_hg = da_hg * o_n * dsilu_hg
        dhgw_ref[...] += jnp.sum(d_on * o_hat, axis=0, keepdims=True)
        gyo = d_on * hgw
        do_parts = []
        for h in range(HEADS):
            lanes = slice(h * HEAD_DIM, (h + 1) * HEAD_DIM)
            gh, nh = gyo[:, lanes], o_hat[:, lanes]
            do_parts.append(inv_o[h] * (gh - nh * jnp.mean(gh * nh, axis=-1, keepdims=True)))
        do_ref[...] = jnp.concatenate(do_parts, axis=1)

        dmixed = da_pool * ps * silu_pool
        dps_ref[...] += jnp.sum(da_pool * mixed * silu_pool, axis=0, keepdims=True)
        dg_pool = da_pool * mixed * ps * dsilu_pool
        du_parts = []
        for gi, win in enumerate(POOL_WINDOWS):
            lanes = slice(gi * POOL_GDIM, (gi + 1) * POOL_GDIM)
            dmx = dmixed[:, lanes]
            dpooled = _dot_nt(dmx, pw_ref[gi])
            dpw_ref[gi] += _dot_tn(pooled_parts[gi], dmx)
            dpt = dpooled * inv_cnt[gi]
            s = jnp.concatenate([dpt, halo_vmem[:, lanes]], axis=0)
            shift = 1
            while shift < win:
                s = s + pltpu.roll(s, ext - shift, 0)
                shift *= 2
            du_parts.append(s[:tr] - dpooled)
            halo_vmem[:, lanes] = dpt[:HALO]
        du = jnp.where(valid, jnp.concatenate(du_parts, axis=1), 0.0)

        for c, val in enumerate((dg_hg, du, dg_pool, dm_hg, dm_pool)):
            dp_ref[c] = val.astype(BF16)
            dbias_ref[c] += jnp.sum(val, axis=0, keepdims=True)
        dz2_ref[...] = dz2
        for c, (lhs, rhs) in enumerate(((merged, dz2), (a_hg, dy_hg), (a_pool, dy_pool))):
            lhs_ref[c] = lhs.astype(BF16)
            rhs_ref[c] = rhs.astype(BF16)

    rev = lambda s: nt - 1 - s
    rowblk = pl.BlockSpec((tr, D_MODEL), lambda s: (rev(s), 0))
    pblk = lambda c: pl.BlockSpec((1, tr, 1024), lambda s, c=c: (c, rev(s), 0))
    halo_blk = pl.BlockSpec((1, HALO, 1024), lambda s: (4, jnp.maximum(rev(s) * (tr // HALO) - 1, 0), 0))
    full = lambda shape: pl.BlockSpec(shape, lambda s: (0,) * len(shape))
    vec = full((1, D_MODEL))
    mat = full((D_MODEL, D_MODEL))
    act3 = jax.ShapeDtypeStruct((3, n_rows, D_MODEL), BF16)
    act3_blk = pl.BlockSpec((3, tr, D_MODEL), lambda s: (0, rev(s), 0))
    return pl.pallas_call(
        body, name="tail_forward_backward",
        out_shape=(jax.ShapeDtypeStruct((n_rows, D_MODEL), F32),
                   jax.ShapeDtypeStruct((5, n_rows, 1024), BF16),
                   jax.ShapeDtypeStruct((n_rows, D_MODEL), F32),
                   act3, act3,
                   jax.ShapeDtypeStruct((5, 1, 1024), F32),
                   jax.ShapeDtypeStruct((1, D_MODEL), F32),
                   jax.ShapeDtypeStruct((n_groups, POOL_GDIM, POOL_GDIM), F32),
                   jax.ShapeDtypeStruct((1, D_MODEL), F32),
                   jax.ShapeDtypeStruct((1, D_MODEL), F32),
                   jax.ShapeDtypeStruct((1, 1), F32)),
        grid=(nt,),
        in_specs=[rowblk, pblk(3), pblk(4), pblk(5), pblk(6), pblk(7), halo_blk, rowblk, ANY,
                  vec, full((n_groups, POOL_GDIM, POOL_GDIM)), vec, mat, mat, mat, vec],
        out_specs=(rowblk, pl.BlockSpec((5, tr, 1024), lambda s: (0, rev(s), 0)), rowblk,
                   act3_blk, act3_blk,
                   full((5, 1, 1024)), vec, full((n_groups, POOL_GDIM, POOL_GDIM)), vec, vec, full((1, 1))),
        scratch_shapes=[pltpu.VMEM((HALO, D_MODEL), F32), pltpu.VMEM((2, tr, D_MODEL), F32), pltpu.SemaphoreType.DMA((2,))],
        compiler_params=_params(("arbitrary",)),
    )(o, p, p, p, p, p, p, zp, tgt, hg_norm_w, pool_w, pool_scale, w_down_hg, w_down_pool, w_out, final_norm_w)


def _in_projection_backward(dp_a, dp_b, w_blocks, zp, dz2, norm_w, chip_sums):
    n_rows = zp.shape[0]
    tr = _row_tile(n_rows, 416)
    nt = n_rows // tr
    na, nb = dp_a.shape[0], dp_b.shape[0]

    def body(dpa_ref, dpb_ref, w_hbm, z_ref, dz2_ref, nw_ref, gs_hbm, gx_hbm, head_ref, dnw_ref, rs_hbm,
             w_vmem, sem, dz_buf, gx_sems, *sems):
        i = pl.program_id(0)
        start_slabs, finish_slabs = _scatter_low(gs_hbm, rs_hbm, *sems)

        def wait_rows_out(tile):
            @pl.when(tile == 0)
            def _():
                pltpu.make_async_copy(dz_buf.at[0, pl.ds(CHUNK, tr - CHUNK), :], gx_hbm.at[pl.ds(0, tr - CHUNK), :],
                                      gx_sems.at[0]).wait()

            @pl.when(tile > 0)
            def _():
                pltpu.make_async_copy(dz_buf.at[tile % 2], gx_hbm.at[pl.ds(tile * tr - CHUNK, tr), :],
                                      gx_sems.at[tile % 2]).wait()

        @pl.when(i == 0)
        def _():
            start_slabs()
            cp = pltpu.make_async_copy(w_hbm, w_vmem, sem)
            cp.start()
            cp.wait()
            dnw_ref[...] = jnp.zeros_like(dnw_ref)

        dh = jnp.zeros((tr, D_MODEL), F32)
        half_cols = w_vmem.shape[-1]
        for j in range(na + nb):
            dp_ref, jj = (dpa_ref, j) if j < na else (dpb_ref, j - na)
            for half in range(2):
                dh = dh + _dot_nt(dp_ref[jj, :, half * half_cols:(half + 1) * half_cols], w_vmem[j, half])
        z = z_ref[...]
        r = lax.rsqrt(jnp.mean(z * z, axis=-1, keepdims=True) + EPS)
        n1 = z * r
        dnw_ref[...] += jnp.sum(dh * n1, axis=0, keepdims=True)
        gh = dh * nw_ref[...]
        dz = dz2_ref[...] + r * (gh - n1 * jnp.mean(gh * n1, axis=-1, keepdims=True))

        @pl.when(i >= 2)
        def _():
            wait_rows_out(i - 2)

        dz_buf[i % 2] = dz

        @pl.when(i == 0)
        def _():
            head_ref[...] = dz[0:CHUNK]
            pltpu.make_async_copy(dz_buf.at[0, pl.ds(CHUNK, tr - CHUNK), :], gx_hbm.at[pl.ds(0, tr - CHUNK), :],
                                  gx_sems.at[0]).start()

        @pl.when(i > 0)
        def _():
            pltpu.make_async_copy(dz_buf.at[i % 2], gx_hbm.at[pl.ds(i * tr - CHUNK, tr), :], gx_sems.at[i % 2]).start()

        @pl.when(i == nt - 1)
        def _():
            if nt >= 2:
                wait_rows_out(i - 1)
            wait_rows_out(i)
            finish_slabs()

    rowblk = pl.BlockSpec((tr, D_MODEL), lambda i: (i, 0))
    vec = pl.BlockSpec((1, D_MODEL), lambda i: (0, 0))
    return pl.pallas_call(
        body, name="in_projection_backward",
        out_shape=(jax.ShapeDtypeStruct((n_rows - CHUNK, D_MODEL), F32), jax.ShapeDtypeStruct((CHUNK, D_MODEL), F32),
                   jax.ShapeDtypeStruct((1, D_MODEL), F32),
                   jax.ShapeDtypeStruct((4,) + chip_sums.shape[1:], chip_sums.dtype)),
        grid=(nt,),
        in_specs=[pl.BlockSpec((na, tr, 1024), lambda i: (0, i, 0)), pl.BlockSpec((nb, tr, 1024), lambda i: (0, i, 0)),
                  ANY, rowblk, rowblk, vec, ANY],
        out_specs=(ANY, pl.BlockSpec((CHUNK, D_MODEL), lambda i: (0, 0)), vec, ANY),
        scratch_shapes=[pltpu.VMEM(w_blocks.shape, w_blocks.dtype), pltpu.SemaphoreType.DMA(()),
                        pltpu.VMEM((2, tr, D_MODEL), F32), pltpu.SemaphoreType.DMA((2,)),
                        pltpu.SemaphoreType.DMA((2,)), pltpu.SemaphoreType.DMA((3,)), pltpu.SemaphoreType.DMA(())],
        compiler_params=_params(("arbitrary",)),
    )(dp_a, dp_b, w_blocks, zp, dz2, norm_w, chip_sums)


def _weight_grad(xs, ys, name):
    shared = xs.ndim == 2
    n_rows, m = xs.shape[-2:]
    nb, _, n = ys.shape
    tk = _row_tile(n_rows, 4160)
    n_k = n_rows // tk

    def body(x_ref, y_ref, o_ref, acc):
        k = pl.program_id(1)

        @pl.when(k == 0)
        def _():
            acc[...] = jnp.zeros_like(acc)

        acc[...] += _dot_tn(x_ref[...] if shared else x_ref[0], y_ref[0])

        @pl.when(k == n_k - 1)
        def _():
            o_ref[0] = acc[...].astype(o_ref.dtype)

    x_spec = pl.BlockSpec((tk, m), lambda j, k: (k, 0)) if shared else pl.BlockSpec((1, tk, m), lambda j, k: (j, k, 0))
    return pl.pallas_call(
        body, name=name,
        out_shape=jax.ShapeDtypeStruct((nb, m, n), BF16),
        grid=(nb, n_k),
        in_specs=[x_spec, pl.BlockSpec((1, tk, n), lambda j, k: (j, k, 0))],
        out_specs=pl.BlockSpec((1, m, n), lambda j, k: (j, 0, 0)),
        scratch_shapes=[pltpu.VMEM((m, n), F32)],
        compiler_params=_params(("arbitrary", "arbitrary")),
    )(xs, ys)


def kernel(x, meta_tokens, norm_w, w_in, b_in, lb_logits, hg_norm_w, pool_w, pool_scale, w_down_hg, w_down_pool, w_out, final_norm_w, loss_target, m_meta_tokens, m_norm_w, m_w_in, m_b_in, m_lb_logits, m_hg_norm_w, m_pool_w, m_pool_scale, m_w_down_hg, m_w_down_pool, m_w_out, m_final_norm_w, v_meta_tokens, v_norm_w, v_w_in, v_b_in, v_lb_logits, v_hg_norm_w, v_pool_w, v_pool_scale, v_w_down_hg, v_w_down_pool, v_w_out, v_final_norm_w):
    seq = x.shape[1]

    meta_full = _all_gather_small(meta_tokens).transpose(1, 0, 2).reshape(N_META, D_MODEL)
    w_rest = jnp.concatenate([w_down_hg[0].astype(BF16), w_down_pool[0].astype(BF16), w_out[0].astype(BF16),
                              pool_w[0].astype(BF16).reshape(32, 1024)], axis=0)

    zp = jnp.concatenate([jnp.zeros((PAD_ROWS, D_MODEL), F32), meta_full, x[0]], axis=0)
    p, h, w_blocks = _in_projection(zp, norm_w, w_in[0].astype(BF16), b_in.reshape(N_COLBLK, 1, 1024), _gather_units())
    o, states, scores, rest = _hgrn_forward(p, lb_logits, w_rest)
    wdh = rest[:, REST_W_DOWN_HG:REST_W_DOWN_HG + 128].reshape(1024, 1024)
    wdp = rest[:, REST_W_DOWN_POOL:REST_W_DOWN_POOL + 128].reshape(1024, 1024)
    wout = rest[:, REST_W_OUT:REST_W_OUT + 128].reshape(1024, 1024)
    pw = rest[:, REST_POOL_W:REST_POOL_W + 32].reshape(N_DEV, 4, 32, 256).transpose(1, 0, 2, 3).reshape(4, 256, 256)
    (d_o, dp_b, dz2, grad_lhs, grad_rhs, dbias_b, d_hgw, d_pw, d_ps, d_fnw, loss_part) = _tail(
        p, o, zp, loss_target[0], hg_norm_w, pw, pool_scale, wdh, wdp, wout, final_norm_w.reshape(1, D_MODEL))
    n_a = N_COLBLK - dp_b.shape[0]
    g_hi = _weight_grad(h, dp_b, "weight_grad_in_hi")
    g_rows = _weight_grad(grad_lhs, grad_rhs, "weight_grad_rows")
    dp_a, dbias_a, d_lb, recv_hi, recv_rows = _hgrn_backward(p, lb_logits, states, scores, d_o, g_hi, n_a, g_rows)
    assert n_a == LOW_OWNERS
    chip_lo = _weight_grad_low(h, dp_a)
    dz_seq, dz_head, d_nw, recv_lo = _in_projection_backward(dp_a, dp_b, w_blocks, zp, dz2, norm_w, chip_lo)

    lb = jax.nn.sigmoid(lb_logits[0:1] - lb_logits[1:2])
    d_l0 = d_lb * lb * (1.0 - lb)
    replicated = jnp.concatenate([dbias_a.reshape(3, 1024), dbias_b.reshape(5, 1024), d_nw, d_l0, -d_l0, d_hgw, d_ps, d_fnw,
                                  jnp.pad(loss_part, ((0, MISC_ROWS - MISC_LOSS - 1), (0, 1023)))], axis=0)
    d_meta = dz_head[PAD_ROWS:CHUNK].reshape(N_META, N_DEV, 128).transpose(1, 0, 2)
    d_pw_blocks = d_pw.reshape(4, N_DEV, 32, 256).transpose(1, 0, 2, 3).reshape(N_DEV, 32, 1024)
    g_misc = jnp.concatenate([d_pw_blocks, jnp.pad(d_meta, ((0, 0), (0, 0), (0, 1024 - 128))),
                              jnp.broadcast_to(replicated[None], (N_DEV, 16, 1024))], axis=1)

    as_rows = lambda t, n: t.reshape(n, 1024)
    small = [(MISC_POOL_W, 1024, tuple(as_rows(t, 32) for t in (pool_w, m_pool_w, v_pool_w))),
             (MISC_META, 128, (meta_tokens, m_meta_tokens, v_meta_tokens)),
             (MISC_B_IN, 1024, tuple(as_rows(t, 8) for t in (b_in, m_b_in, v_b_in))),
             (MISC_NORM_W, 1024, (norm_w, m_norm_w, v_norm_w)),
             (MISC_LB, 1024, (lb_logits, m_lb_logits, v_lb_logits)),
             (MISC_HG_NORM_W, 1024, (hg_norm_w, m_hg_norm_w, v_hg_norm_w)),
             (MISC_POOL_SCALE, 1024, (pool_scale, m_pool_scale, v_pool_scale)),
             (MISC_FINAL_NORM_W, 1024, tuple(as_rows(t, 1) for t in (final_norm_w, m_final_norm_w, v_final_norm_w)))]
    res, loss = _finish(recv_hi, recv_lo, n_a, recv_rows, _exchange_small(g_misc), (w_in, m_w_in, v_w_in),
                  [(w_out, m_w_out, v_w_out), (w_down_hg, m_w_down_hg, v_w_down_hg), (w_down_pool, m_w_down_pool, v_w_down_pool)],
                  small)
    r_w_in, r_w_out, r_wdh, r_wdp, r_pw, r_meta, r_b_in, r_nw, r_lb, r_hgw, r_ps, r_fnw = res
    grad_x = dz_seq.reshape(1, seq, D_MODEL)
    per_kind = [(r_meta[k], r_nw[k], r_w_in[k], r_b_in[k].reshape(1, 8192), r_lb[k], r_hgw[k], r_pw[k].reshape(1, 4, 32, 256),
                 r_ps[k], r_wdh[k], r_wdp[k], r_w_out[k], r_fnw[k].reshape(1024)) for k in range(4)]
    return (loss, grad_x, *per_kind[0], *per_kind[1], *per_kind[2], *per_kind[3])
```
